```python
import jax, jax.numpy as jnp
from jax import lax
import numpy as np

D_MODEL = 1024
BATCH = 8
SEQ = 2048
DEPTH = 2

GRID_W = 64
CTX_LEN = 256
EPS = 1e-6

F_GROUPS = 4
F_GROUP_DIM = D_MODEL // 16
F_WIDTH = F_GROUPS * F_GROUP_DIM
M_HEADS = 4
M_HEAD_DIM = 3 * D_MODEL // 32
M_WIDTH = M_HEADS * M_HEAD_DIM
M_CHUNK = 64
K_CONV = 5
A_HEADS = 4
A_NOPE = 64
A_ROPE = 32
A_V = 3 * D_MODEL // 32
A_WIDTH = A_HEADS * A_V
Q_LORA = D_MODEL // 4
KV_LORA = D_MODEL // 8
Q_BLOCK = 128
ROPE_THETA = 10000.0
MLP_HIDDEN = 4 * D_MODEL
IN_WIDTHS = (F_WIDTH, M_WIDTH, M_WIDTH, M_WIDTH, M_WIDTH, 4 * M_HEADS, Q_LORA, KV_LORA, A_ROPE)
IN_WIDTH = F_WIDTH + 4 * M_WIDTH + 4 * M_HEADS + Q_LORA + KV_LORA + A_ROPE
MIX_WIDTH = F_WIDTH + M_WIDTH + A_WIDTH

kernel_name = "hybrid_fourier_mlstm_mla_dit_block"


def rmsnorm(x, g):
    xf = x.astype(jnp.float32)
    y = xf * lax.rsqrt(jnp.mean(xf * xf, axis=-1, keepdims=True) + EPS)
    return (y * g.astype(jnp.float32)).astype(x.dtype)


def modulate(h, shift, scale):
    return h * (1 + scale) + shift


def split_proj(p):
    out, off = [], 0
    for w in IN_WIDTHS:
        out.append(p[..., off:off + w])
        off += w
    return out


def axial_rope(n):
    rows = n // GRID_W
    row = jnp.repeat(jnp.arange(rows, dtype=jnp.float32), GRID_W)
    col = jnp.tile(jnp.arange(GRID_W, dtype=jnp.float32), rows)
    nf = A_ROPE // 4
    freqs = ROPE_THETA ** (-jnp.arange(nf, dtype=jnp.float32) / nf)
    ang = jnp.stack([row[:, None] * freqs, col[:, None] * freqs], axis=1)
    return jnp.cos(ang), jnp.sin(ang)


def apply_rope(x, cos, sin):
    xs = x.reshape(x.shape[:-1] + (2, 2, A_ROPE // 4)).astype(jnp.float32)
    x1, x2 = xs[..., 0, :], xs[..., 1, :]
    out = jnp.stack([x1 * cos - x2 * sin, x1 * sin + x2 * cos], axis=-2)
    return out.reshape(x.shape).astype(x.dtype)


def fourier_mix(u):
    B, N, _ = u.shape
    ug = u.astype(jnp.float32).reshape(B, N, F_GROUPS, F_GROUP_DIM)
    y = jnp.fft.fft2(ug, axes=(1, 3), norm="ortho").real
    return y.reshape(B, N, F_WIDTH).astype(u.dtype)


def dwconv_centred(u, w):
    C = u.shape[-1]
    pad = w.shape[0] // 2
    return lax.conv_general_dilated(u, w[:, None, :].astype(u.dtype), (1,), [(pad, pad)],
                                    dimension_numbers=("NWC", "WIO", "NWC"), feature_group_count=C)


def mlstm_inputs(mq, mk, mv, mg, conv_w, b_g):
    B, N, _ = mq.shape
    qk = jax.nn.silu(dwconv_centred(jnp.concatenate([mq, mk], axis=-1), conv_w))
    heads = lambda a: a.reshape(B, N, M_HEADS, M_HEAD_DIM).transpose(0, 2, 1, 3).astype(jnp.float32)
    q = heads(qk[..., :M_WIDTH]) * (M_HEAD_DIM ** -0.5)
    k = heads(qk[..., M_WIDTH:])
    v = heads(mv)
    g = (mg.astype(jnp.float32) + b_g.astype(jnp.float32)).reshape(B, N, 4, M_HEADS).transpose(2, 0, 3, 1)
    fwd = (q, k, v, g[0], jax.nn.log_sigmoid(g[1]))
    bwd = (q, k, v, g[2], jax.nn.log_sigmoid(g[3]))
    return fwd, bwd


def mlstm_zero_state(B):
    return (jnp.zeros((B, M_HEADS, M_HEAD_DIM, M_HEAD_DIM), jnp.float32),
            jnp.zeros((B, M_HEADS, M_HEAD_DIM), jnp.float32),
            jnp.zeros((B, M_HEADS), jnp.float32))


def mlstm_chunkwise(q, k, v, li, lf, state):
    B, H, N, _ = q.shape
    L = M_CHUNK
    nc = N // L
    to_chunks = lambda a: jnp.moveaxis(a.reshape((B, H, nc, L) + a.shape[3:]), 2, 0)
    tri = jnp.tril(jnp.ones((L, L), dtype=bool))

    def step(carry, inp):
        C, n, m = carry
        qb, kb, vb, ib, fb = inp
        b = jnp.cumsum(fb, axis=-1)
        dmat = jnp.where(tri, b[..., :, None] - b[..., None, :] + ib[..., None, :], -jnp.inf)
        inter = b + m[..., None]
        m_t = jnp.maximum(inter, dmat.max(axis=-1))
        w_intra = jnp.exp(dmat - m_t[..., None])
        w_inter = jnp.exp(inter - m_t)
        s = jnp.einsum("bhtd,bhsd->bhts", qb, kb) * w_intra
        num = jnp.einsum("bhts,bhsv->bhtv", s, vb) + w_inter[..., None] * jnp.einsum("bhtd,bhdv->bhtv", qb, C)
        den = s.sum(axis=-1) + w_inter * jnp.einsum("bhtd,bhd->bht", qb, n)
        h = num / jnp.maximum(jnp.abs(den), jnp.exp(-m_t))[..., None]
        b_last = b[..., -1]
        g = b_last[..., None] - b + ib
        m_new = jnp.maximum(b_last + m, g.max(axis=-1))
        decay = jnp.exp(b_last + m - m_new)
        wk = jnp.exp(g - m_new[..., None])
        C_new = decay[..., None, None] * C + jnp.einsum("bhs,bhsd,bhsv->bhdv", wk, kb, vb)
        n_new = decay[..., None] * n + jnp.einsum("bhs,bhsd->bhd", wk, kb)
        return (C_new, n_new, m_new), h

    state, hc = lax.scan(step, state, tuple(to_chunks(a) for a in (q, k, v, li, lf)))
    h = jnp.moveaxis(hc, 0, 2).reshape(B, H, N, -1)
    return h, state


def mlstm_direction(seq, state, reverse):
    if reverse:
        seq = tuple(jnp.flip(a, axis=2) for a in seq)
    h, state = mlstm_chunkwise(*seq, state)
    if reverse:
        h = jnp.flip(h, axis=2)
    return h, state


def mlstm_output(h_f, h_b, mo, g_m):
    B, H, N, dv = h_f.shape
    h = (h_f + h_b).transpose(0, 2, 1, 3)
    h = h * lax.rsqrt(jnp.mean(h * h, axis=-1, keepdims=True) + EPS)
    h = h.reshape(B, N, M_WIDTH) * g_m.astype(jnp.float32) * jax.nn.sigmoid(mo.astype(jnp.float32))
    return h.astype(mo.dtype)


def mla_queries(cq, g_qn, w_uq, rope):
    B, N, _ = cq.shape
    q = (rmsnorm(cq, g_qn) @ w_uq).reshape(B, N, A_HEADS, A_NOPE + A_ROPE).transpose(0, 2, 1, 3)
    qn, qr = q[..., :A_NOPE], q[..., A_NOPE:]
    if rope is not None:
        qr = apply_rope(qr, *rope)
    return qn, qr


def mla_keys(ckv, kr, g_kvn, w_ukv, rope):
    B, N, _ = ckv.shape
    kv = (rmsnorm(ckv, g_kvn) @ w_ukv).reshape(B, N, A_HEADS, A_NOPE + A_V).transpose(0, 2, 1, 3)
    kn, v = kv[..., :A_NOPE], kv[..., A_NOPE:]
    if rope is not None:
        kr = apply_rope(kr, *rope)
    return kn, kr, v


def softmax_attend(qn, qr, key_sets):
    scale = (A_NOPE + A_ROPE) ** -0.5
    s = jnp.concatenate([jnp.einsum("bhqd,bhkd->bhqk", qn, kn) + jnp.einsum("bhqr,bkr->bhqk", qr, kr)
                         for kn, kr, _ in key_sets], axis=-1)
    p = jax.nn.softmax(s.astype(jnp.float32) * scale, axis=-1)
    out, off = None, 0
    for _, _, v in key_sets:
        nk = v.shape[2]
        o = jnp.einsum("bhqk,bhkd->bhqd", p[..., off:off + nk].astype(v.dtype), v)
        out = o if out is None else out + o
        off += nk
    return out


def blocked_attend(qn, qr, key_sets):
    B, H, N, _ = qn.shape
    nb = N // Q_BLOCK
    blk = lambda a: jnp.moveaxis(a.reshape(B, H, nb, Q_BLOCK, a.shape[-1]), 2, 0)
    out = lax.map(lambda qs: softmax_attend(qs[0], qs[1], key_sets), (blk(qn), blk(qr)))
    return jnp.moveaxis(out, 0, 2).reshape(B, H, N, -1)


def merge_heads(y):
    B, H, N, dv = y.shape
    return y.transpose(0, 2, 1, 3).reshape(B, N, H * dv)


def sqrelu_mlp(h, w_up, w_down):
    return jnp.square(jax.nn.relu(h @ w_up)) @ w_down


def setup_inputs(seed: int = 0) -> dict:
    key = jax.random.key(seed)
    ks = jax.random.split(key, 24)
    nrm = lambda k, shape, s: jax.random.normal(k, shape, jnp.float32) * s
    gain = lambda k, shape: 1.0 + 0.02 * jax.random.normal(k, shape, jnp.float32)
    f_bias = jnp.linspace(3.0, 6.0, M_HEADS, dtype=jnp.float32)
    z = jnp.zeros((M_HEADS,), jnp.float32)
    gate_base = jnp.concatenate([z, f_bias, z, f_bias])
    return {
        "x": nrm(ks[0], (BATCH, SEQ, D_MODEL), 1.0),
        "c": nrm(ks[1], (BATCH, D_MODEL), 1.0),
        "ctx": nrm(ks[2], (BATCH, CTX_LEN, D_MODEL), 1.0),
        "c_ctx": nrm(ks[3], (D_MODEL,), 1.0),
        "w_mod": nrm(ks[4], (DEPTH, D_MODEL, 6 * D_MODEL), 0.5 * D_MODEL ** -0.5),
        "b_mod": nrm(ks[5], (DEPTH, 6 * D_MODEL), 0.01),
        "g_norm1": gain(ks[6], (DEPTH, D_MODEL)),
        "g_norm2": gain(ks[7], (DEPTH, D_MODEL)),
        "w_in": nrm(ks[8], (DEPTH, D_MODEL, IN_WIDTH), D_MODEL ** -0.5),
        "b_gates": gate_base[None, :] + nrm(ks[9], (DEPTH, 4 * M_HEADS), 0.1),
        "conv_qk": nrm(ks[10], (DEPTH, K_CONV, 2 * M_WIDTH), K_CONV ** -0.5),
        "g_mlstm": gain(ks[11], (DEPTH, M_WIDTH)),
        "g_q_norm": gain(ks[12], (DEPTH, Q_LORA)),
        "g_kv_norm": gain(ks[13], (DEPTH, KV_LORA)),
        "w_uq": nrm(ks[14], (DEPTH, Q_LORA, A_HEADS * (A_NOPE + A_ROPE)), Q_LORA ** -0.5),
        "w_ukv": nrm(ks[15], (DEPTH, KV_LORA, A_HEADS * (A_NOPE + A_V)), KV_LORA ** -0.5),
        "w_out": nrm(ks[16], (DEPTH, MIX_WIDTH, D_MODEL), MIX_WIDTH ** -0.5),
        "w_up": nrm(ks[17], (DEPTH, D_MODEL, MLP_HIDDEN), D_MODEL ** -0.5),
        "w_down": nrm(ks[18], (DEPTH, MLP_HIDDEN, D_MODEL), MLP_HIDDEN ** -0.5),
        "g_final": gain(ks[19], (D_MODEL,)),
    }


def reference(x, c, ctx, c_ctx, w_mod, b_mod, g_norm1, g_norm2, w_in, b_gates, conv_qk, g_mlstm,
              g_q_norm, g_kv_norm, w_uq, w_ukv, w_out, w_up, w_down, g_final):
    B, S, _ = x.shape
    rope_lat = axial_rope(S)
    xc = ctx
    for l in range(DEPTH):
        last = l == DEPTH - 1
        mod = (jax.nn.silu(c) @ w_mod[l] + b_mod[l])[:, None, :]
        mod_c = jax.nn.silu(c_ctx) @ w_mod[l] + b_mod[l]
        sh1, sc1, ga1, sh2, sc2, ga2 = jnp.split(mod, 6, axis=-1)
        sh1c, sc1c, ga1c, sh2c, sc2c, ga2c = jnp.split(mod_c, 6, axis=-1)

        h = modulate(rmsnorm(x, g_norm1[l]), sh1, sc1)
        hc = modulate(rmsnorm(xc, g_norm1[l]), sh1c, sc1c)
        pf, mq, mk, mv, mo, mg, cq, ckv, kr = split_proj(h @ w_in[l])
        pfc, mqc, mkc, mvc, moc, mgc, cqc, ckvc, krc = split_proj(hc @ w_in[l])

        y_f = fourier_mix(pf)

        fwd_c, bwd_c = mlstm_inputs(mqc, mkc, mvc, mgc, conv_qk[l], b_gates[l])
        fwd, bwd = mlstm_inputs(mq, mk, mv, mg, conv_qk[l], b_gates[l])
        zero = mlstm_zero_state(B)
        hcf, st_f = mlstm_direction(fwd_c, zero, False)
        hcb, st_b = mlstm_direction(bwd_c, zero, True)
        hf, _ = mlstm_direction(fwd, st_f, False)
        hb, _ = mlstm_direction(bwd, st_b, True)
        y_m = mlstm_output(hf, hb, mo, g_mlstm[l])

        ctx_keys = mla_keys(ckvc, krc, g_kv_norm[l], w_ukv[l], None)
        lat_keys = mla_keys(ckv, kr, g_kv_norm[l], w_ukv[l], rope_lat)
        qn, qr = mla_queries(cq, g_q_norm[l], w_uq[l], rope_lat)
        y_a = merge_heads(blocked_attend(qn, qr, (lat_keys, ctx_keys)))

        x = x + ga1 * (jnp.concatenate([y_f, y_m, y_a], axis=-1) @ w_out[l])
        x = x + ga2 * sqrelu_mlp(modulate(rmsnorm(x, g_norm2[l]), sh2, sc2), w_up[l], w_down[l])

        if not last:
            y_fc = fourier_mix(pfc)
            y_mc = mlstm_output(hcf, hcb, moc, g_mlstm[l])
            qn_c, qr_c = mla_queries(cqc, g_q_norm[l], w_uq[l], None)
            y_ac = merge_heads(softmax_attend(qn_c, qr_c, (ctx_keys,)))
            xc = xc + ga1c * (jnp.concatenate([y_fc, y_mc, y_ac], axis=-1) @ w_out[l])
            xc = xc + ga2c * sqrelu_mlp(modulate(rmsnorm(xc, g_norm2[l]), sh2c, sc2c), w_up[l], w_down[l])

    return rmsnorm(x, g_final)
```

```python
import functools

import numpy as np
import jax
import jax.numpy as jnp
from jax import lax
from jax.experimental import pallas as pl
from jax.experimental.pallas import tpu as pltpu

D_MODEL = 1024
GRID_W = 64
EPS = 1e-6
F_GROUPS = 4
F_GROUP_DIM = D_MODEL // 16
F_WIDTH = F_GROUPS * F_GROUP_DIM
M_HEADS = 4
M_HEAD_DIM = 3 * D_MODEL // 32
M_WIDTH = M_HEADS * M_HEAD_DIM
M_CHUNK = 64
K_CONV = 5
A_HEADS = 4
A_NOPE = 64
A_ROPE = 32
A_V = 3 * D_MODEL // 32
Q_LORA = D_MODEL // 4
KV_LORA = D_MODEL // 8
ROPE_THETA = 10000.0
MLP_HIDDEN = 4 * D_MODEL

LANES = 128
HEAD_PAD = 128
MP_WIDTH = M_HEADS * HEAD_PAD
AP_WIDTH = A_HEADS * HEAD_PAD
VMEM_LIMIT = 56 * 1024 * 1024

OFF_PF = 0
OFF_MQ = OFF_PF + F_WIDTH
OFF_MK = OFF_MQ + MP_WIDTH
OFF_MV = OFF_MK + MP_WIDTH
OFF_MO = OFF_MV + MP_WIDTH
OFF_MG = OFF_MO + MP_WIDTH
OFF_CQ = OFF_MG + LANES
OFF_CKV = OFF_CQ + Q_LORA
OFF_KR = OFF_CKV + KV_LORA
OFF_KRS = OFF_KR + LANES
IN_PAD = OFF_KRS + LANES

BF16 = jnp.bfloat16
F32 = jnp.float32


def _cparams(sem):
    return pltpu.CompilerParams(dimension_semantics=sem, vmem_limit_bytes=VMEM_LIMIT)


def _const_spec(shape):
    nd = len(shape)
    return pl.BlockSpec(shape, lambda *_: (0,) * nd, pipeline_mode=pl.Buffered(1))


def _split3(a):
    hi = a.astype(BF16)
    r1 = a - hi.astype(F32)
    mid = r1.astype(BF16)
    lo = (r1 - mid.astype(F32)).astype(BF16)
    return hi, mid, lo


def _dot(a, b):
    return jnp.dot(a, b, preferred_element_type=F32)


def _dot_nt(a, b):
    return lax.dot_general(a, b, (((1,), (1,)), ((), ())), preferred_element_type=F32)


def _dot_tn(a, b):
    return lax.dot_general(a, b, (((0,), (0,)), ((), ())), preferred_element_type=F32)


def _rms(x, g):
    return x * lax.rsqrt(jnp.mean(x * x, axis=-1, keepdims=True) + EPS) * g


def _mod_kernel(c_ref, w_ref, b_ref, o_ref):
    c = c_ref[...]
    a = c * jax.nn.sigmoid(c)
    a_hi = a.astype(BF16)
    a_lo = (a - a_hi.astype(F32)).astype(BF16)
    w = w_ref[0]
    w_hi = w.astype(BF16)
    w_lo = (w - w_hi.astype(F32)).astype(BF16)
    acc = _dot(a_hi, w_hi) + _dot(a_hi, w_lo) + _dot(a_lo, w_hi)
    o_ref[0] = acc + b_ref[0]


def _modulation(cc, w_mod, b_mod):
    depth, d, n = w_mod.shape
    rows = cc.shape[0]
    tn = 1536
    return pl.pallas_call(
        _mod_kernel,
        grid=(depth, n // tn),
        in_specs=[
            pl.BlockSpec((rows, d), lambda l, j: (0, 0)),
            pl.BlockSpec((1, d, tn), lambda l, j: (l, 0, j)),
            pl.BlockSpec((1, 1, tn), lambda l, j: (l, 0, j)),
        ],
        out_specs=pl.BlockSpec((1, rows, tn), lambda l, j: (l, 0, j)),
        out_shape=jax.ShapeDtypeStruct((depth, rows, n), F32),
        compiler_params=_cparams(("arbitrary", "arbitrary")),
        name="modulation",
    )(cc, w_mod, b_mod.reshape(depth, 1, n))


def _inproj_kernel(x_ref, g_ref, sh_ref, sc_ref, w_ref, cc_ref, cs_ref,
                   zc_ref, zs_ref, mqk_ref, mv_ref, mo_ref, mg_ref, cq_ref, ckv_ref, kr_ref):
    x = x_ref[...]
    h = _rms(x, g_ref[...]) * (1.0 + sc_ref[0]) + sh_ref[0]
    hb = h.astype(BF16)

    def proj(off, width):
        return _dot(hb, w_ref[:, off:off + width])

    pf = proj(OFF_PF, F_WIDTH).astype(BF16)
    zc_ref[...] = _dot(pf, cc_ref[...]).astype(BF16)
    zs_ref[...] = _dot(pf, cs_ref[...]).astype(BF16)
    mqk_ref[...] = proj(OFF_MQ, 2 * MP_WIDTH)
    mv_ref[...] = proj(OFF_MV, MP_WIDTH).astype(BF16)
    mo_ref[...] = proj(OFF_MO, MP_WIDTH)
    mg_ref[...] = proj(OFF_MG, LANES)
    cq_ref[...] = proj(OFF_CQ, Q_LORA)
    ckv_ref[...] = proj(OFF_CKV, KV_LORA)
    kr_ref[...] = proj(OFF_KR, 2 * LANES)


def _inproj(x2d, seq, g, sh, sc, per_batch_mod, w_in_p, dft_cc, dft_cs, tm):
    t, d = x2d.shape
    nb = t // seq
    tiles_per_seq = seq // tm
    if per_batch_mod:
        mod_map = lambda i: (i // tiles_per_seq, 0, 0)
    else:
        mod_map = lambda i: (0, 0, 0)
    tok = lambda w: pl.BlockSpec((tm, w), lambda i: (i, 0))
    z_spec = pl.BlockSpec((tm, F_WIDTH), lambda i: (i % tiles_per_seq, i // tiles_per_seq))
    shapes = [
        jax.ShapeDtypeStruct((seq, nb * F_WIDTH), BF16),
        jax.ShapeDtypeStruct((seq, nb * F_WIDTH), BF16),
        jax.ShapeDtypeStruct((t, 2 * MP_WIDTH), F32),
        jax.ShapeDtypeStruct((t, MP_WIDTH), BF16),
        jax.ShapeDtypeStruct((t, MP_WIDTH), F32),
        jax.ShapeDtypeStruct((t, LANES), F32),
        jax.ShapeDtypeStruct((t, Q_LORA), F32),
        jax.ShapeDtypeStruct((t, KV_LORA), F32),
        jax.ShapeDtypeStruct((t, 2 * LANES), F32),
    ]
    out_specs = [z_spec, z_spec, tok(2 * MP_WIDTH), tok(MP_WIDTH), tok(MP_WIDTH), tok(LANES),
                 tok(Q_LORA), tok(KV_LORA), tok(2 * LANES)]
    return pl.pallas_call(
        _inproj_kernel,
        grid=(t // tm,),
        in_specs=[
            tok(d),
            _const_spec((1, d)),
            pl.BlockSpec((1, 1, d), mod_map),
            pl.BlockSpec((1, 1, d), mod_map),
            _const_spec((d, IN_PAD)),
            _const_spec((F_WIDTH, F_WIDTH)),
            _const_spec((F_WIDTH, F_WIDTH)),
        ],
        out_specs=out_specs,
        out_shape=shapes,
        compiler_params=_cparams(("arbitrary",)),
        name="inproj",
    )(x2d, g, sh, sc, w_in_p, dft_cc, dft_cs)


def _fourier_kernel(c_ref, s_ref, zc_ref, zs_ref, o_ref):
    y = _dot(c_ref[...], zc_ref[...]) - _dot(s_ref[...], zs_ref[...])
    o_ref[...] = y.astype(BF16)


def _fourier(dft_c, dft_s, zc, zs):
    n, cols = zc.shape
    tr = min(n, 512)
    tc = min(cols, 512)
    return pl.pallas_call(
        _fourier_kernel,
        grid=(n // tr, cols // tc),
        in_specs=[
            pl.BlockSpec((tr, n), lambda i, j: (i, 0)),
            pl.BlockSpec((tr, n), lambda i, j: (i, 0)),
            pl.BlockSpec((n, tc), lambda i, j: (0, j)),
            pl.BlockSpec((n, tc), lambda i, j: (0, j)),
        ],
        out_specs=pl.BlockSpec((tr, tc), lambda i, j: (i, j)),
        out_shape=jax.ShapeDtypeStruct((n, cols), BF16),
        compiler_params=_cparams(("arbitrary", "arbitrary")),
        name="fourier",
    )(dft_c, dft_s, zc, zs)


CONV_ROWS = 256
CONV_HALO = 8


def _conv_kernel(u_ref, w_ref, q_ref, k_ref, pad_ref):
    n = u_ref.shape[1]
    width = u_ref.shape[2]
    zeros = jnp.zeros((CONV_HALO, width), F32)
    pad_ref[0:CONV_HALO, :] = zeros
    pad_ref[CONV_HALO + n:2 * CONV_HALO + n, :] = zeros
    pad_ref[CONV_HALO:CONV_HALO + n, :] = u_ref[0]
    w = w_ref[...]
    rows = min(CONV_ROWS, n)
    for r in range(n // rows):
        base = CONV_HALO + r * rows - K_CONV // 2
        acc = pad_ref[base:base + rows, :] * w[0:1, :]
        for j in range(1, K_CONV):
            acc = acc + pad_ref[base + j:base + j + rows, :] * w[j:j + 1, :]
        act = acc * jax.nn.sigmoid(acc)
        q_ref[0, r * rows:(r + 1) * rows, :] = (act[:, :MP_WIDTH] * (M_HEAD_DIM ** -0.5)).astype(BF16)
        k_ref[0, r * rows:(r + 1) * rows, :] = act[:, MP_WIDTH:].astype(BF16)


def _conv_silu(mqk, conv_w_p):
    nb, n, width = mqk.shape
    return pl.pallas_call(
        _conv_kernel,
        grid=(nb,),
        in_specs=[
            pl.BlockSpec((1, n, width), lambda b: (b, 0, 0)),
            _const_spec((8, width)),
        ],
        out_specs=[
            pl.BlockSpec((1, n, MP_WIDTH), lambda b: (b, 0, 0)),
            pl.BlockSpec((1, n, MP_WIDTH), lambda b: (b, 0, 0)),
        ],
        out_shape=[
            jax.ShapeDtypeStruct((nb, n, MP_WIDTH), BF16),
            jax.ShapeDtypeStruct((nb, n, MP_WIDTH), BF16),
        ],
        scratch_shapes=[pltpu.VMEM((n + 2 * CONV_HALO, width), F32)],
        compiler_params=_cparams(("arbitrary",)),
        name="conv_silu",
    )(mqk, conv_w_p)


def _log_sigmoid(x):
    return jnp.minimum(x, 0.0) - jnp.log(1.0 + jnp.exp(-jnp.abs(x)))


def _exact_dot_01(a, tri_bf16, a_on_left):
    out = None
    for term in _split3(a):
        d = _dot(term, tri_bf16) if a_on_left else _dot(tri_bf16, term)
        out = d if out is None else out + d
    return out


def _mlstm_kernel(ql_ref, kl_ref, vl_ref, gl_ref, gtl_ref,
                  qc_ref, kc_ref, vc_ref, gc_ref, gtc_ref,
                  bcol_ref, brow_ref,
                  hl_ref, hc_ref,
                  c_st, n_st, m_st):
    L = M_CHUNK
    nh = M_HEADS
    ti = lax.broadcasted_iota(jnp.int32, (L, L), 0)
    si = lax.broadcasted_iota(jnp.int32, (L, L), 1)
    lower = si <= ti
    upper = si >= ti
    tri_l = lower.astype(BF16)
    tri_u = upper.astype(BF16)

    c_st[...] = jnp.zeros_like(c_st)
    n_st[...] = jnp.zeros_like(n_st)
    m_st[...] = jnp.zeros_like(m_st)
    hl_ref[...] = jnp.zeros_like(hl_ref)
    hc_ref[...] = jnp.zeros_like(hc_ref)

    bcol = bcol_ref[...]
    brow = brow_ref[...]

    def stream_step(stream, q, k, v, b_col, i_col, r_row, total, mask, out_ref, rows, lanes):
        m_prev = m_st[stream][:, 0:1]
        dmat = jnp.where(mask, b_col + r_row, -jnp.inf)
        inter = b_col + m_prev
        m_t = jnp.maximum(inter, jnp.max(dmat, axis=-1, keepdims=True))
        w_intra = jnp.exp(dmat - m_t)
        w_inter = jnp.exp(inter - m_t)
        s = _dot_nt(q, k) * w_intra
        c_prev = c_st[stream]
        n_prev = n_st[stream]
        num = _dot(s.astype(BF16), v) + w_inter * _dot(q, c_prev.astype(BF16))
        den = jnp.sum(s, axis=-1, keepdims=True) + w_inter * jnp.sum(
            q.astype(F32) * n_prev, axis=-1, keepdims=True)
        h = num / jnp.maximum(jnp.abs(den), jnp.exp(-m_t))
        out_ref[0, rows, lanes] = out_ref[0, rows, lanes] + h
        g = total - b_col + i_col
        m_new = jnp.maximum(total + m_prev, jnp.max(g, axis=0, keepdims=True))
        decay = jnp.exp(total + m_prev - m_new)
        wk = jnp.exp(g - m_new) * k.astype(F32)
        c_st[stream] = decay * c_prev + _dot_tn(wk.astype(BF16), v)
        n_st[stream] = decay * n_prev + jnp.sum(wk, axis=0, keepdims=True)
        m_st[stream] = jnp.broadcast_to(m_new, (1, LANES))

    def run_sequence(q_ref, k_ref, v_ref, g_ref, gt_ref, out_ref):
        nc = q_ref.shape[1] // L

        def direction(chunk, reverse):
            r0 = pl.multiple_of(chunk * L, L)
            rows = pl.ds(r0, L)
            gcol = g_ref[0, rows, :] + bcol
            fcol = _log_sigmoid(gcol)
            grow = gt_ref[0, chunk] + brow
            frow = _log_sigmoid(grow)
            if reverse:
                bc_all = _exact_dot_01(fcol, tri_u, a_on_left=False)
                br_all = _exact_dot_01(frow, tri_l, a_on_left=True)
                mask = upper
                gi, gf = 2 * nh, 3 * nh
            else:
                bc_all = _exact_dot_01(fcol, tri_l, a_on_left=False)
                br_all = _exact_dot_01(frow, tri_u, a_on_left=True)
                mask = lower
                gi, gf = 0, nh
            tot_all = jnp.sum(fcol, axis=0, keepdims=True)
            for hd in range(nh):
                lanes = slice(hd * HEAD_PAD, (hd + 1) * HEAD_PAD)
                q = q_ref[0, rows, lanes]
                k = k_ref[0, rows, lanes]
                v = v_ref[0, rows, lanes]
                b_col = bc_all[:, gf + hd:gf + hd + 1]
                i_col = gcol[:, gi + hd:gi + hd + 1]
                r_row = grow[gi + hd:gi + hd + 1, :] - br_all[gf + hd:gf + hd + 1, :]
                total = tot_all[:, gf + hd:gf + hd + 1]
                stream = (nh if reverse else 0) + hd
                stream_step(stream, q, k, v, b_col, i_col, r_row, total, mask, out_ref, rows, lanes)

        def body(j, carry):
            direction(j, False)
            direction(nc - 1 - j, True)
            return carry

        lax.fori_loop(0, nc, body, 0)

    run_sequence(qc_ref, kc_ref, vc_ref, gc_ref, gtc_ref, hc_ref)
    run_sequence(ql_ref, kl_ref, vl_ref, gl_ref, gtl_ref, hl_ref)


def _mlstm(ql, kl, vl, gl, gtl, qc, kc, vc, gc, gtc, b_col, b_row):
    nb, n, _ = ql.shape
    nctx = qc.shape[1]
    seq_spec = lambda rows, w: pl.BlockSpec((1, rows, w), lambda b: (b, 0, 0))
    gt_spec = lambda rows: pl.BlockSpec((1, rows // M_CHUNK, 4 * M_HEADS, M_CHUNK), lambda b: (b, 0, 0, 0))
    return pl.pallas_call(
        _mlstm_kernel,
        grid=(nb,),
        in_specs=[
            seq_spec(n, MP_WIDTH), seq_spec(n, MP_WIDTH), seq_spec(n, MP_WIDTH), seq_spec(n, LANES), gt_spec(n),
            seq_spec(nctx, MP_WIDTH), seq_spec(nctx, MP_WIDTH), seq_spec(nctx, MP_WIDTH), seq_spec(nctx, LANES),
            gt_spec(nctx),
            _const_spec((1, LANES)), _const_spec((4 * M_HEADS, 1)),
        ],
        out_specs=[seq_spec(n, MP_WIDTH), seq_spec(nctx, MP_WIDTH)],
        out_shape=[
            jax.ShapeDtypeStruct((nb, n, MP_WIDTH), F32),
            jax.ShapeDtypeStruct((nb, nctx, MP_WIDTH), F32),
        ],
        scratch_shapes=[
            pltpu.VMEM((2 * M_HEADS, HEAD_PAD, HEAD_PAD), F32),
            pltpu.VMEM((2 * M_HEADS, 1, HEAD_PAD), F32),
            pltpu.VMEM((2 * M_HEADS, 1, LANES), F32),
        ],
        compiler_params=_cparams(("arbitrary",)),
        name="mlstm",
    )(ql, kl, vl, gl, gtl, qc, kc, vc, gc, gtc, b_col, b_row)


def _mla_proj_kernel(cq_ref, ckv_ref, kr_ref, cos_ref, sin_ref, gq_ref, gkv_ref,
                     wq_ref, wqs_ref, wk_ref, wv_ref, q_ref, k_ref, v_ref, *, with_q):
    cos = cos_ref[...]
    sin = sin_ref[...]
    kvn = _rms(ckv_ref[...], gkv_ref[...]).astype(BF16)
    kr = kr_ref[...]
    k_rope = kr[:, :LANES] * cos + kr[:, LANES:] * sin
    for hd in range(A_HEADS):
        lanes = slice(hd * HEAD_PAD, (hd + 1) * HEAD_PAD)
        k_ref[:, lanes] = (_dot(kvn, wk_ref[:, lanes]) + k_rope).astype(BF16)
        v_ref[:, lanes] = _dot(kvn, wv_ref[:, lanes]).astype(BF16)
    if with_q:
        qn = _rms(cq_ref[...], gq_ref[...]).astype(BF16)
        scale = (A_NOPE + A_ROPE) ** -0.5
        for hd in range(A_HEADS):
            lanes = slice(hd * HEAD_PAD, (hd + 1) * HEAD_PAD)
            q = _dot(qn, wq_ref[:, lanes]) * cos + _dot(qn, wqs_ref[:, lanes]) * sin
            q_ref[:, lanes] = (q * scale).astype(BF16)
    else:
        q_ref[...] = jnp.zeros_like(q_ref)


def _mla_proj(cq, ckv, kr, cos, sin, seq, gq, gkv, wq, wqs, wk, wv, tm, with_q):
    t = cq.shape[0]
    tiles_per_seq = seq // tm
    tok = lambda w: pl.BlockSpec((tm, w), lambda i: (i, 0))
    pos = pl.BlockSpec((tm, LANES), lambda i: (i % tiles_per_seq, 0))
    out = jax.ShapeDtypeStruct((t, AP_WIDTH), BF16)
    return pl.pallas_call(
        functools.partial(_mla_proj_kernel, with_q=with_q),
        grid=(t // tm,),
        in_specs=[
            tok(Q_LORA), tok(KV_LORA), tok(2 * LANES), pos, pos,
            _const_spec((1, Q_LORA)), _const_spec((1, KV_LORA)),
            _const_spec((Q_LORA, AP_WIDTH)), _const_spec((Q_LORA, AP_WIDTH)),
            _const_spec((KV_LORA, AP_WIDTH)), _const_spec((KV_LORA, AP_WIDTH)),
        ],
        out_specs=[tok(AP_WIDTH), tok(AP_WIDTH), tok(AP_WIDTH)],
        out_shape=[out, out, out],
        compiler_params=_cparams(("arbitrary",)),
        name="mla_proj",
    )(cq, ckv, kr, cos, sin, gq, gkv, wq, wqs, wk, wv)


def _attn_kernel(*refs, n_sets):
    q_ref = refs[0]
    kv_refs = refs[1:1 + 2 * n_sets]
    o_ref = refs[1 + 2 * n_sets]
    q = q_ref[0]
    scores = [_dot_nt(q, kv_refs[2 * i][0]) for i in range(n_sets)]
    m = None
    for s in scores:
        sm = jnp.max(s, axis=-1, keepdims=True)
        m = sm if m is None else jnp.maximum(m, sm)
    acc = None
    den = None
    for i, s in enumerate(scores):
        p = jnp.exp(s - m)
        l = jnp.sum(p, axis=-1, keepdims=True)
        o = _dot(p.astype(BF16), kv_refs[2 * i + 1][0])
        acc = o if acc is None else acc + o
        den = l if den is None else den + l
    o_ref[0] = (acc / den).astype(BF16)


def _attention(q, key_sets, tq):
    nb, n, _ = q.shape
    n_sets = len(key_sets)
    in_specs = [pl.BlockSpec((1, tq, HEAD_PAD), lambda b, h, i: (b, i, h))]
    args = [q]
    for k, v in key_sets:
        nk = k.shape[1]
        spec = pl.BlockSpec((1, nk, HEAD_PAD), lambda b, h, i: (b, 0, h))
        in_specs += [spec, spec]
        args += [k, v]
    return pl.pallas_call(
        functools.partial(_attn_kernel, n_sets=n_sets),
        grid=(nb, A_HEADS, n // tq),
        in_specs=in_specs,
        out_specs=pl.BlockSpec((1, tq, HEAD_PAD), lambda b, h, i: (b, i, h)),
        out_shape=jax.ShapeDtypeStruct((nb, n, AP_WIDTH), BF16),
        compiler_params=_cparams(("arbitrary", "arbitrary", "arbitrary")),
        name="attention",
    )(*args)


def _out_mlp_kernel(x_ref, yf_ref, hm_ref, mo_ref, ya_ref, mod_ref,
                    gm_ref, g2_ref, gfin_ref, wof_ref, wom_ref, woa_ref, wup_ref, wdn_ref,
                    o_ref, *, final_norm):
    mod = mod_ref[0]
    ga1, sh2, sc2, ga2 = mod[0:1], mod[1:2], mod[2:3], mod[3:4]
    mix = _dot(yf_ref[...], wof_ref[...]) + _dot(ya_ref[...], woa_ref[...])
    gm = gm_ref[...]
    for hd in range(M_HEADS):
        lanes = slice(hd * HEAD_PAD, (hd + 1) * HEAD_PAD)
        hh = hm_ref[:, lanes]
        ms = jnp.sum(hh * hh, axis=-1, keepdims=True) * (1.0 / M_HEAD_DIM)
        ym = hh * lax.rsqrt(ms + EPS) * gm[:, lanes] * jax.nn.sigmoid(mo_ref[:, lanes])
        mix = mix + _dot(ym.astype(BF16), wom_ref[lanes, :])
    x1 = x_ref[...] + ga1 * mix
    h2 = (_rms(x1, g2_ref[...]) * (1.0 + sc2) + sh2).astype(BF16)
    hidden = wup_ref.shape[1]
    hc = 1024
    acc = None
    for c in range(hidden // hc):
        u = jnp.maximum(_dot(h2, wup_ref[:, c * hc:(c + 1) * hc]), 0.0)
        d = _dot((u * u).astype(BF16), wdn_ref[c * hc:(c + 1) * hc, :])
        acc = d if acc is None else acc + d
    x2 = x1 + ga2 * acc
    if final_norm:
        x2 = _rms(x2, gfin_ref[...])
    o_ref[...] = x2


def _out_mlp(x2d, seq, yf, hm, mo, ya, mod4, per_batch_mod, gm, g2, gfin,
             wof, wom, woa, wup, wdn, tm, final_norm):
    t, d = x2d.shape
    tiles_per_seq = seq // tm
    if per_batch_mod:
        mod_map = lambda i: (i // tiles_per_seq, 0, 0)
    else:
        mod_map = lambda i: (0, 0, 0)
    tok = lambda w: pl.BlockSpec((tm, w), lambda i: (i, 0))
    yf_spec = pl.BlockSpec((tm, F_WIDTH), lambda i: (i % tiles_per_seq, i // tiles_per_seq))
    return pl.pallas_call(
        functools.partial(_out_mlp_kernel, final_norm=final_norm),
        grid=(t // tm,),
        in_specs=[
            tok(d), yf_spec, tok(MP_WIDTH), tok(MP_WIDTH), tok(AP_WIDTH),
            pl.BlockSpec((1, 4, d), mod_map),
            _const_spec((1, MP_WIDTH)), _const_spec((1, d)), _const_spec((1, d)),
            _const_spec((F_WIDTH, d)), _const_spec((MP_WIDTH, d)), _const_spec((AP_WIDTH, d)),
            _const_spec((d, wup.shape[1])), _const_spec((wdn.shape[0], d)),
        ],
        out_specs=tok(d),
        out_shape=jax.ShapeDtypeStruct((t, d), F32),
        compiler_params=_cparams(("arbitrary",)),
        name="out_mlp",
    )(x2d, yf, hm, mo, ya, mod4, gm, g2, gfin, wof, wom, woa, wup, wdn)


def _dft_tables(n):
    idx = (np.arange(n, dtype=np.int64)[:, None] * np.arange(n, dtype=np.int64)[None, :]) % n
    ang = 2.0 * np.pi * idx.astype(np.float64) / n
    scale = 1.0 / np.sqrt(n)
    return np.cos(ang) * scale, np.sin(ang) * scale


def _channel_dft():
    c, s = _dft_tables(F_GROUP_DIM)
    eye = np.eye(F_GROUPS)
    return (jnp.asarray(np.kron(eye, c), dtype=F32).astype(BF16),
            jnp.asarray(np.kron(eye, s), dtype=F32).astype(BF16))


def _position_dft(n):
    c, s = _dft_tables(n)
    return jnp.asarray(c, dtype=F32).astype(BF16), jnp.asarray(s, dtype=F32).astype(BF16)


def _rope_tables(n, rotate):
    cos = np.zeros((n, HEAD_PAD), np.float32)
    sin = np.zeros((n, HEAD_PAD), np.float32)
    cos[:, :A_NOPE + A_ROPE] = 1.0
    if rotate:
        nf = A_ROPE // 4
        t = np.arange(n)
        row = (t // GRID_W).astype(np.float32)
        col = (t % GRID_W).astype(np.float32)
        freqs = (np.float32(ROPE_THETA) ** (-np.arange(nf, dtype=np.float32) / np.float32(nf))).astype(np.float32)
        for seg, pos in enumerate((row, col)):
            ang = pos[:, None] * freqs[None, :]
            c, s = np.cos(ang), np.sin(ang)
            base = A_NOPE + seg * 2 * nf
            cos[:, base:base + nf] = c
            cos[:, base + nf:base + 2 * nf] = c
            sin[:, base:base + nf] = -s
            sin[:, base + nf:base + 2 * nf] = s
    return jnp.asarray(cos), jnp.asarray(sin)


def _rope_swap_perm():
    nf = A_ROPE // 4
    perm = np.arange(A_ROPE)
    for seg in range(2):
        b = seg * 2 * nf
        perm[b:b + nf] = np.arange(b + nf, b + 2 * nf)
        perm[b + nf:b + 2 * nf] = np.arange(b, b + nf)
    return perm


def _pad_heads_cols(w, heads, width):
    lead = w.shape[:-1]
    w = w.reshape(lead + (heads, width))
    w = jnp.pad(w, [(0, 0)] * len(lead) + [(0, 0), (0, HEAD_PAD - width)])
    return w.reshape(lead + (heads * HEAD_PAD,))


def _pad_cols(w, width):
    return jnp.pad(w, [(0, 0)] * (w.ndim - 1) + [(0, width - w.shape[-1])])


def _layer_weights(l, w_in, conv_qk, b_gates, g_mlstm, w_uq, w_ukv, w_out):
    swap = _rope_swap_perm()
    wl = w_in[l]
    offs = np.cumsum([0, F_WIDTH, M_WIDTH, M_WIDTH, M_WIDTH, M_WIDTH, 4 * M_HEADS, Q_LORA, KV_LORA, A_ROPE])
    part = lambda i: wl[:, offs[i]:offs[i + 1]]
    kr_w = part(8)
    place_rope = lambda w: jnp.pad(w, [(0, 0), (A_NOPE, LANES - A_NOPE - A_ROPE)])
    w_in_p = jnp.concatenate([
        part(0),
        _pad_heads_cols(part(1), M_HEADS, M_HEAD_DIM),
        _pad_heads_cols(part(2), M_HEADS, M_HEAD_DIM),
        _pad_heads_cols(part(3), M_HEADS, M_HEAD_DIM),
        _pad_heads_cols(part(4), M_HEADS, M_HEAD_DIM),
        _pad_cols(part(5), LANES),
        part(6),
        part(7),
        place_rope(kr_w),
        place_rope(kr_w[:, swap]),
    ], axis=1).astype(BF16)
    assert w_in_p.shape[1] == IN_PAD

    conv = conv_qk[l]
    conv_p = jnp.concatenate([
        _pad_heads_cols(conv[:, :M_WIDTH], M_HEADS, M_HEAD_DIM),
        _pad_heads_cols(conv[:, M_WIDTH:], M_HEADS, M_HEAD_DIM),
    ], axis=1)
    conv_p = jnp.pad(conv_p, [(0, 8 - K_CONV), (0, 0)])

    bg = b_gates[l]
    b_col = _pad_cols(bg[None, :], LANES)
    b_row = bg[:, None]

    gm = _pad_heads_cols(g_mlstm[l][None, :], M_HEADS, M_HEAD_DIM)

    uq = w_uq[l].reshape(Q_LORA, A_HEADS, A_NOPE + A_ROPE)
    uq_s = jnp.concatenate([jnp.zeros_like(uq[..., :A_NOPE]), uq[..., A_NOPE:][..., swap]], axis=-1)
    pad_q = lambda w: jnp.pad(w, [(0, 0), (0, 0), (0, HEAD_PAD - A_NOPE - A_ROPE)]).reshape(Q_LORA, AP_WIDTH)
    wq = pad_q(uq).astype(BF16)
    wqs = pad_q(uq_s).astype(BF16)
    ukv = w_ukv[l].reshape(KV_LORA, A_HEADS, A_NOPE + A_V)
    wk = jnp.pad(ukv[..., :A_NOPE], [(0, 0), (0, 0), (0, HEAD_PAD - A_NOPE)]).reshape(KV_LORA, AP_WIDTH).astype(BF16)
    wv = jnp.pad(ukv[..., A_NOPE:], [(0, 0), (0, 0), (0, HEAD_PAD - A_V)]).reshape(KV_LORA, AP_WIDTH).astype(BF16)

    wo = w_out[l]
    pad_rows = lambda w, heads, width: jnp.pad(
        w.reshape(heads, width, -1), [(0, 0), (0, HEAD_PAD - width), (0, 0)]).reshape(heads * HEAD_PAD, -1)
    wof = wo[:F_WIDTH].astype(BF16)
    wom = pad_rows(wo[F_WIDTH:F_WIDTH + M_WIDTH], M_HEADS, M_HEAD_DIM).astype(BF16)
    woa = pad_rows(wo[F_WIDTH + M_WIDTH:], A_HEADS, A_V).astype(BF16)
    return dict(w_in_p=w_in_p, conv_p=conv_p, b_col=b_col, b_row=b_row, gm=gm,
                wq=wq, wqs=wqs, wk=wk, wv=wv, wof=wof, wom=wom, woa=woa)


def _gates_rowmajor(mg, nb, n):
    g = mg[:, :4 * M_HEADS].reshape(nb, n // M_CHUNK, M_CHUNK, 4 * M_HEADS)
    return g.transpose(0, 1, 3, 2)


def kernel(x, c, ctx, c_ctx, w_mod, b_mod, g_norm1, g_norm2, w_in, b_gates, conv_qk, g_mlstm,
           g_q_norm, g_kv_norm, w_uq, w_ukv, w_out, w_up, w_down, g_final):
    nb, seq, d = x.shape
    nctx = ctx.shape[1]
    depth = w_mod.shape[0]
    assert d == D_MODEL and seq % 256 == 0 and nctx % M_CHUNK == 0

    tm = 256
    tm_ctx = min(256, nctx)
    tq = 256
    tq_ctx = min(256, nctx)

    dft_cc, dft_cs = _channel_dft()
    dft_lat = _position_dft(seq)
    dft_ctx = _position_dft(nctx)
    rope_lat = _rope_tables(seq, True)
    rope_ctx = _rope_tables(nctx, False)

    rows = ((nb + 1 + 7) // 8) * 8
    cc = jnp.concatenate([c, c_ctx[None, :], jnp.zeros((rows - nb - 1, d), F32)], axis=0)
    mod_all = _modulation(cc, w_mod, b_mod)

    xl = x.reshape(nb * seq, d)
    xc = ctx.reshape(nb * nctx, d)
    row = lambda v: v.reshape(1, -1)

    for l in range(depth):
        last = l == depth - 1
        lw = _layer_weights(l, w_in, conv_qk, b_gates, g_mlstm, w_uq, w_ukv, w_out)
        wup = w_up[l].astype(BF16)
        wdn = w_down[l].astype(BF16)
        mod = mod_all[l].reshape(rows, 6, d)
        mod_lat, mod_ctx = mod[:nb], mod[nb:nb + 1]

        def tokenwise(xt, n, m, per_batch, tile):
            return _inproj(xt, n, row(g_norm1[l]), m[:, 0:1], m[:, 1:2], per_batch,
                           lw["w_in_p"], dft_cc, dft_cs, tile)

        zc, zs, mqk, mv, mo, mg, cq, ckv, kr = tokenwise(xl, seq, mod_lat, True, tm)
        zc_c, zs_c, mqk_c, mv_c, mo_c, mg_c, cq_c, ckv_c, kr_c = tokenwise(xc, nctx, mod_ctx, False, tm_ctx)

        yf = _fourier(*dft_lat, zc, zs)

        q_m, k_m = _conv_silu(mqk.reshape(nb, seq, 2 * MP_WIDTH), lw["conv_p"])
        q_mc, k_mc = _conv_silu(mqk_c.reshape(nb, nctx, 2 * MP_WIDTH), lw["conv_p"])
        hm, hm_c = _mlstm(
            q_m, k_m, mv.reshape(nb, seq, MP_WIDTH), mg.reshape(nb, seq, LANES), _gates_rowmajor(mg, nb, seq),
            q_mc, k_mc, mv_c.reshape(nb, nctx, MP_WIDTH), mg_c.reshape(nb, nctx, LANES),
            _gates_rowmajor(mg_c, nb, nctx), lw["b_col"], lw["b_row"])

        gq, gkv = row(g_q_norm[l]), row(g_kv_norm[l])
        q_a, k_a, v_a = _mla_proj(cq, ckv, kr, *rope_lat, seq, gq, gkv,
                                  lw["wq"], lw["wqs"], lw["wk"], lw["wv"], tm, True)
        q_ac, k_ac, v_ac = _mla_proj(cq_c, ckv_c, kr_c, *rope_ctx, nctx, gq, gkv,
                                     lw["wq"], lw["wqs"], lw["wk"], lw["wv"], tm_ctx, not last)
        b3 = lambda a, n: a.reshape(nb, n, AP_WIDTH)
        keys_ctx = (b3(k_ac, nctx), b3(v_ac, nctx))
        ya = _attention(b3(q_a, seq), [(b3(k_a, seq), b3(v_a, seq)), keys_ctx], tq)

        mlp = functools.partial(
            _out_mlp, gm=lw["gm"], g2=row(g_norm2[l]), gfin=row(g_final),
            wof=lw["wof"], wom=lw["wom"], woa=lw["woa"], wup=wup, wdn=wdn)
        xl = mlp(xl, seq, yf, hm.reshape(nb * seq, MP_WIDTH), mo, ya.reshape(nb * seq, AP_WIDTH),
                 mod_lat[:, 2:6], True, tm=tm, final_norm=last)

        if not last:
            yf_c = _fourier(*dft_ctx, zc_c, zs_c)
            ya_c = _attention(b3(q_ac, nctx), [keys_ctx], tq_ctx)
            xc = mlp(xc, nctx, yf_c, hm_c.reshape(nb * nctx, MP_WIDTH), mo_c,
                     ya_c.reshape(nb * nctx, AP_WIDTH), mod_ctx[:, 2:6], False, tm=tm_ctx, final_norm=False)

    return xl.reshape(nb, seq, d)
```

```python
import functools

import numpy as np
import jax
import jax.numpy as jnp
from jax import lax
from jax.experimental import pallas as pl
from jax.experimental.pallas import tpu as pltpu

D_MODEL = 1024
GRID_W = 64
EPS = 1e-6
F_GROUPS = 4
F_GROUP_DIM = D_MODEL // 16
F_WIDTH = F_GROUPS * F_GROUP_DIM
M_HEADS = 4
M_HEAD_DIM = 3 * D_MODEL // 32
M_WIDTH = M_HEADS * M_HEAD_DIM
M_CHUNK = 64
K_CONV = 5
A_HEADS = 4
A_NOPE = 64
A_ROPE = 32
A_V = 3 * D_MODEL // 32
Q_LORA = D_MODEL // 4
KV_LORA = D_MODEL // 8
ROPE_THETA = 10000.0
MLP_HIDDEN = 4 * D_MODEL

LANES = 128
HEAD_PAD = 128
MP_WIDTH = M_HEADS * HEAD_PAD
AP_WIDTH = A_HEADS * HEAD_PAD
VMEM_LIMIT = 56 * 1024 * 1024
MCH = 128

OFF_PF = 0
OFF_MQ = OFF_PF + F_WIDTH
OFF_MK = OFF_MQ + MP_WIDTH
OFF_MO = OFF_MK + MP_WIDTH
OFF_GI = OFF_MO + MP_WIDTH
OFF_GF = OFF_GI + LANES
OFF_CQ = OFF_GF + LANES
OFF_CKV = OFF_CQ + Q_LORA
OFF_KR = OFF_CKV + KV_LORA
OFF_KRS = OFF_KR + LANES
IN_PAD = OFF_KRS + LANES

BF16 = jnp.bfloat16
F32 = jnp.float32


def _cparams(sem):
    return pltpu.CompilerParams(dimension_semantics=sem, vmem_limit_bytes=VMEM_LIMIT)


def _const_spec(shape):
    nd = len(shape)
    return pl.BlockSpec(shape, lambda *_: (0,) * nd, pipeline_mode=pl.Buffered(1))


def _split3(a):
    hi = a.astype(BF16)
    r1 = a - hi.astype(F32)
    mid = r1.astype(BF16)
    lo = (r1 - mid.astype(F32)).astype(BF16)
    return hi, mid, lo


def _dot(a, b):
    return jnp.dot(a, b, preferred_element_type=F32)


def _dot_nt(a, b):
    return lax.dot_general(a, b, (((1,), (1,)), ((), ())), preferred_element_type=F32)


def _dot_tn(a, b):
    return lax.dot_general(a, b, (((0,), (0,)), ((), ())), preferred_element_type=F32)


def _rms(x, g):
    return x * lax.rsqrt(jnp.mean(x * x, axis=-1, keepdims=True) + EPS) * g


def _mod_kernel(c_ref, w_ref, b_ref, o_ref):
    c = c_ref[...]
    a = c * jax.nn.sigmoid(c)
    a_hi = a.astype(BF16)
    a_lo = (a - a_hi.astype(F32)).astype(BF16)
    w = w_ref[0]
    w_hi = w.astype(BF16)
    w_lo = (w - w_hi.astype(F32)).astype(BF16)
    acc = _dot(a_hi, w_hi) + _dot(a_hi, w_lo) + _dot(a_lo, w_hi)
    o_ref[0] = acc + b_ref[0]


def _modulation(cc, w_mod, b_mod):
    depth, d, n = w_mod.shape
    rows = cc.shape[0]
    tn = 1536
    return pl.pallas_call(
        _mod_kernel,
        grid=(depth, n // tn),
        in_specs=[
            pl.BlockSpec((rows, d), lambda l, j: (0, 0)),
            pl.BlockSpec((1, d, tn), lambda l, j: (l, 0, j)),
            pl.BlockSpec((1, 1, tn), lambda l, j: (l, 0, j)),
        ],
        out_specs=pl.BlockSpec((1, rows, tn), lambda l, j: (l, 0, j)),
        out_shape=jax.ShapeDtypeStruct((depth, rows, n), F32),
        compiler_params=_cparams(("arbitrary", "arbitrary")),
        name="modulation",
    )(cc, w_mod, b_mod.reshape(depth, 1, n))


def _inproj_kernel(x_ref, g_ref, sh_ref, sc_ref, w_ref, wvt_ref, cc_ref, cs_ref,
                   zc_ref, zs_ref, mqk_ref, vt_ref, mo_ref, gi_ref, gf_ref, cq_ref, ckv_ref, kr_ref):
    x = x_ref[...]
    h = _rms(x, g_ref[...]) * (1.0 + sc_ref[0]) + sh_ref[0]
    hb = h.astype(BF16)

    def proj(off, width):
        return _dot(hb, w_ref[:, off:off + width])

    pf = proj(OFF_PF, F_WIDTH).astype(BF16)
    zc_ref[...] = _dot(pf, cc_ref[...]).astype(BF16)
    zs_ref[...] = _dot(pf, cs_ref[...]).astype(BF16)
    mqk_ref[...] = proj(OFF_MQ, 2 * MP_WIDTH)
    vt = _dot_nt(wvt_ref[...], hb).astype(BF16)
    for j in range(vt_ref.shape[0]):
        vt_ref[j] = vt[:, j * MCH:(j + 1) * MCH]
    mo_ref[...] = proj(OFF_MO, MP_WIDTH)
    gi_ref[...] = proj(OFF_GI, LANES)
    gf_ref[...] = proj(OFF_GF, LANES)
    cq_ref[...] = proj(OFF_CQ, Q_LORA)
    ckv_ref[...] = proj(OFF_CKV, KV_LORA)
    kr_ref[...] = proj(OFF_KR, 2 * LANES)


def _inproj(x2d, seq, g, sh, sc, per_batch_mod, w_in_p, w_vt, dft_cc, dft_cs, tm):
    t, d = x2d.shape
    nb = t // seq
    tiles_per_seq = seq // tm
    if per_batch_mod:
        mod_map = lambda i: (i // tiles_per_seq, 0, 0)
    else:
        mod_map = lambda i: (0, 0, 0)
    tok = lambda w: pl.BlockSpec((tm, w), lambda i: (i, 0))
    z_spec = pl.BlockSpec((tm, F_WIDTH), lambda i: (i % tiles_per_seq, i // tiles_per_seq))
    shapes = [
        jax.ShapeDtypeStruct((seq, nb * F_WIDTH), BF16),
        jax.ShapeDtypeStruct((seq, nb * F_WIDTH), BF16),
        jax.ShapeDtypeStruct((t, 2 * MP_WIDTH), F32),
        jax.ShapeDtypeStruct((t // MCH, MP_WIDTH, MCH), BF16),
        jax.ShapeDtypeStruct((t, MP_WIDTH), F32),
        jax.ShapeDtypeStruct((t, LANES), F32),
        jax.ShapeDtypeStruct((t, LANES), F32),
        jax.ShapeDtypeStruct((t, Q_LORA), F32),
        jax.ShapeDtypeStruct((t, KV_LORA), F32),
        jax.ShapeDtypeStruct((t, 2 * LANES), F32),
    ]
    vt_spec = pl.BlockSpec((tm // MCH, MP_WIDTH, MCH), lambda i: (i, 0, 0))
    out_specs = [z_spec, z_spec, tok(2 * MP_WIDTH), vt_spec, tok(MP_WIDTH), tok(LANES), tok(LANES),
                 tok(Q_LORA), tok(KV_LORA), tok(2 * LANES)]
    return pl.pallas_call(
        _inproj_kernel,
        grid=(t // tm,),
        in_specs=[
            tok(d),
            _const_spec((1, d)),
            pl.BlockSpec((1, 1, d), mod_map),
            pl.BlockSpec((1, 1, d), mod_map),
            _const_spec((d, IN_PAD)),
            _const_spec((MP_WIDTH, d)),
            _const_spec((F_WIDTH, F_WIDTH)),
            _const_spec((F_WIDTH, F_WIDTH)),
        ],
        out_specs=out_specs,
        out_shape=shapes,
        compiler_params=_cparams(("arbitrary",)),
        name="inproj",
    )(x2d, g, sh, sc, w_in_p, w_vt, dft_cc, dft_cs)


def _fourier_kernel(c_ref, s_ref, zc_ref, zs_ref, o_ref):
    y = _dot(c_ref[...], zc_ref[...]) - _dot(s_ref[...], zs_ref[...])
    o_ref[...] = y.astype(BF16)


def _fourier(dft_c, dft_s, zc, zs):
    n, cols = zc.shape
    tr = min(n, 512)
    tc = min(cols, 512)
    return pl.pallas_call(
        _fourier_kernel,
        grid=(n // tr, cols // tc),
        in_specs=[
            pl.BlockSpec((tr, n), lambda i, j: (i, 0)),
            pl.BlockSpec((tr, n), lambda i, j: (i, 0)),
            pl.BlockSpec((n, tc), lambda i, j: (0, j)),
            pl.BlockSpec((n, tc), lambda i, j: (0, j)),
        ],
        out_specs=pl.BlockSpec((tr, tc), lambda i, j: (i, j)),
        out_shape=jax.ShapeDtypeStruct((n, cols), BF16),
        compiler_params=_cparams(("arbitrary", "arbitrary")),
        name="fourier",
    )(dft_c, dft_s, zc, zs)


CONV_ROWS = 256
CONV_HALO = 8


def _conv_kernel(u_ref, w_ref, q_ref, k_ref, pad_ref):
    n = u_ref.shape[1]
    width = u_ref.shape[2]
    zeros = jnp.zeros((CONV_HALO, width), F32)
    pad_ref[0:CONV_HALO, :] = zeros
    pad_ref[CONV_HALO + n:2 * CONV_HALO + n, :] = zeros
    pad_ref[CONV_HALO:CONV_HALO + n, :] = u_ref[0]
    w = w_ref[...]
    rows = min(CONV_ROWS, n)
    for r in range(n // rows):
        base = CONV_HALO + r * rows - K_CONV // 2
        acc = pad_ref[base:base + rows, :] * w[0:1, :]
        for j in range(1, K_CONV):
            acc = acc + pad_ref[base + j:base + j + rows, :] * w[j:j + 1, :]
        act = acc * jax.nn.sigmoid(acc)
        q_ref[0, r * rows:(r + 1) * rows, :] = (act[:, :MP_WIDTH] * (M_HEAD_DIM ** -0.5)).astype(BF16)
        k_ref[0, r * rows:(r + 1) * rows, :] = act[:, MP_WIDTH:].astype(BF16)


def _conv_silu(mqk, conv_w_p):
    nb, n, width = mqk.shape
    return pl.pallas_call(
        _conv_kernel,
        grid=(nb,),
        in_specs=[
            pl.BlockSpec((1, n, width), lambda b: (b, 0, 0)),
            _const_spec((8, width)),
        ],
        out_specs=[
            pl.BlockSpec((1, n, MP_WIDTH), lambda b: (b, 0, 0)),
            pl.BlockSpec((1, n, MP_WIDTH), lambda b: (b, 0, 0)),
        ],
        out_shape=[
            jax.ShapeDtypeStruct((nb, n, MP_WIDTH), BF16),
            jax.ShapeDtypeStruct((nb, n, MP_WIDTH), BF16),
        ],
        scratch_shapes=[pltpu.VMEM((n + 2 * CONV_HALO, width), F32)],
        compiler_params=_cparams(("arbitrary",)),
        name="conv_silu",
    )(mqk, conv_w_p)


def _log_sigmoid(x):
    return jnp.minimum(x, 0.0) - jnp.log(1.0 + jnp.exp(-jnp.abs(x)))


def _exact_dot_01(a, tri_bf16, a_on_left):
    out = None
    for term in _split3(a):
        d = _dot(term, tri_bf16) if a_on_left else _dot(tri_bf16, term)
        out = d if out is None else out + d
    return out


N_STREAM = 2 * M_HEADS
ONE_ROW = M_HEAD_DIM
ROW_A, ROW_WI, ROW_ELD = 0, 1, 2


def _mlstm_kernel(ql_ref, kl_ref, vtl_ref, gil_ref, gfl_ref, gril_ref, grfl_ref,
                  qc_ref, kc_ref, vtc_ref, gic_ref, gfc_ref, gric_ref, grfc_ref,
                  bic_ref, bfc_ref, bir_ref, bfr_ref, sel_ref,
                  hl_ref, hc_ref,
                  ct_st, st_sc, rows_sc):
    L = MCH
    nh = M_HEADS
    ncc = qc_ref.shape[1] // L
    ncl = ql_ref.shape[1] // L

    d0 = lax.broadcasted_iota(jnp.int32, (L, L), 0)
    d1 = lax.broadcasted_iota(jnp.int32, (L, L), 1)
    le = d0 <= d1
    ge = d0 >= d1
    tri_le = le.astype(BF16)
    tri_ge = ge.astype(BF16)
    lane8 = lax.broadcasted_iota(jnp.int32, (N_STREAM, L), 1)
    fwd_rows = lax.broadcasted_iota(jnp.int32, (N_STREAM, L), 0) < nh
    fwd_lanes = lax.broadcasted_iota(jnp.int32, (L, LANES), 1) < nh
    feat = lax.broadcasted_iota(jnp.int32, (HEAD_PAD, L), 0)
    one_row = feat == ONE_ROW
    keep_rows = feat < M_HEAD_DIM

    bic, bfc = bic_ref[...], bfc_ref[...]
    bir, bfr = bir_ref[...], bfr_ref[...]

    ct_st[...] = jnp.zeros_like(ct_st)

    def value_slab(vt_ref, c, hd):
        vt = vt_ref[c, hd * HEAD_PAD:(hd + 1) * HEAD_PAD, :].astype(F32)
        return jnp.where(one_row, 1.0, vt)

    def pass1(k_ref, vt_ref, gri_ref, grf_ref, nc, slot0, m0):
        def step(j, m_prev):
            cf = j
            cb = nc - 1 - j
            gi = jnp.where(fwd_rows, gri_ref[0, cf], gri_ref[0, cb]) + bir
            f = _log_sigmoid(jnp.where(fwd_rows, grf_ref[0, cf], grf_ref[0, cb]) + bfr)
            pre = _exact_dot_01(f, tri_le, a_on_left=True)
            total = jnp.sum(f, axis=1, keepdims=True)
            b = jnp.where(fwd_rows, pre, total - pre + f)
            r = gi - b
            cm = r
            sh = 1
            while sh < L:
                from_left = jnp.where(lane8 >= sh, pltpu.roll(cm, sh, 1), -jnp.inf)
                from_right = jnp.where(lane8 < L - sh, pltpu.roll(cm, L - sh, 1), -jnp.inf)
                cm = jnp.maximum(cm, jnp.where(fwd_rows, from_left, from_right))
                sh *= 2
            big = jnp.maximum(m_prev, jnp.max(r, axis=1, keepdims=True))
            a = -jnp.maximum(m_prev, cm)
            per_slot = {ROW_A: a, ROW_WI: jnp.exp(m_prev + a), ROW_ELD: jnp.exp(a - b)}
            for idx, val in per_slot.items():
                rows_sc[slot0 + cf, idx, 0:nh, :] = val[0:nh]
                rows_sc[slot0 + cb, idx, nh:N_STREAM, :] = val[nh:N_STREAM]
            decay = jnp.exp(m_prev - big)
            wk = jnp.exp(r - big)
            for sidx in range(N_STREAM):
                hd = sidx % nh
                c = cf if sidx < nh else cb
                kk = k_ref[0, pl.ds(pl.multiple_of(c * L, L), L), hd * HEAD_PAD:(hd + 1) * HEAD_PAD]
                ut = _dot((value_slab(vt_ref, c, hd) * wk[sidx:sidx + 1, :]).astype(BF16), kk)
                prev = ct_st[sidx]
                st_sc[sidx, slot0 + c] = prev.astype(BF16)
                ct_st[sidx] = decay[sidx:sidx + 1, :] * prev + ut
            return total + big

        return lax.fori_loop(0, nc, step, m0)

    def pass2(q_ref, k_ref, vt_ref, gi_ref, gf_ref, out_ref, nc, slot0):
        def step(c, carry):
            rows = pl.ds(pl.multiple_of(c * L, L), L)
            gi_col = gi_ref[0, rows, :] + bic
            f_col = _log_sigmoid(gf_ref[0, rows, :] + bfc)
            pre = _exact_dot_01(f_col, tri_ge, a_on_left=False)
            total = jnp.sum(f_col, axis=0, keepdims=True)
            r_col = gi_col - jnp.where(fwd_lanes, pre, total - pre + f_col)
            r3 = jnp.concatenate(_split3(r_col), axis=1)
            a_rows = rows_sc[slot0 + c, ROW_A]
            wi_rows = rows_sc[slot0 + c, ROW_WI]
            eld_rows = rows_sc[slot0 + c, ROW_ELD]
            head_lanes = [slice(hd * HEAD_PAD, (hd + 1) * HEAD_PAD) for hd in range(nh)]
            qs = [q_ref[0, rows, lanes] for lanes in head_lanes]
            s_ts = [_dot_nt(k_ref[0, rows, lanes], q) for lanes, q in zip(head_lanes, qs)]
            inters = [_dot_nt(st_sc[sidx, slot0 + c], qs[sidx % nh]) for sidx in range(N_STREAM)]
            p_ts = []
            for sidx in range(N_STREAM):
                one = slice(sidx, sidx + 1)
                dm = jnp.where(le if sidx < nh else ge, r_col[:, one] + a_rows[one, :], -jnp.inf)
                p_ts.append((s_ts[sidx % nh] * jnp.exp(dm)).astype(BF16))
            vtas = [value_slab(vt_ref, c, hd).astype(BF16) for hd in range(nh)]
            z_ts = [_dot(vtas[sidx % nh], p_ts[sidx]) for sidx in range(N_STREAM)]
            for hd in range(nh):
                hsum = None
                for sidx in (hd, nh + hd):
                    one = slice(sidx, sidx + 1)
                    z_t = z_ts[sidx] + inters[sidx] * wi_rows[one, :]
                    den = z_t[ONE_ROW:ONE_ROW + 1, :]
                    h_t = z_t * (1.0 / jnp.maximum(jnp.abs(den), eld_rows[one, :]))
                    hsum = h_t if hsum is None else hsum + h_t
                out_ref[0, rows, head_lanes[hd]] = jnp.where(keep_rows, hsum, 0.0).T
            return carry

        lax.fori_loop(0, nc, step, 0)

    m0 = jnp.zeros((N_STREAM, L), F32)
    m1 = pass1(kc_ref, vtc_ref, gric_ref, grfc_ref, ncc, 0, m0)
    pass1(kl_ref, vtl_ref, gril_ref, grfl_ref, ncl, ncc, m1)
    pass2(qc_ref, kc_ref, vtc_ref, gic_ref, gfc_ref, hc_ref, ncc, 0)
    pass2(ql_ref, kl_ref, vtl_ref, gil_ref, gfl_ref, hl_ref, ncl, ncc)


def _mlstm(lat, ctx, bic, bfc, bir, bfr, sel):
    nb, n, _ = lat[0].shape
    nctx = ctx[0].shape[1]
    assert MCH == LANES == HEAD_PAD
    nct = (n + nctx) // MCH

    def specs(rows):
        nc = rows // MCH
        seq = lambda w: pl.BlockSpec((1, rows, w), lambda b: (b, 0, 0))
        gr = pl.BlockSpec((1, nc, N_STREAM, MCH), lambda b: (b, 0, 0, 0))
        return [seq(MP_WIDTH), seq(MP_WIDTH), pl.BlockSpec((nc, MP_WIDTH, MCH), lambda b: (b, 0, 0)),
                seq(LANES), seq(LANES), gr, gr]

    out_spec = lambda rows: pl.BlockSpec((1, rows, MP_WIDTH), lambda b: (b, 0, 0))
    return pl.pallas_call(
        _mlstm_kernel,
        grid=(nb,),
        in_specs=specs(n) + specs(nctx) + [
            _const_spec((1, LANES)), _const_spec((1, LANES)),
            _const_spec((N_STREAM, 1)), _const_spec((N_STREAM, 1)),
            _const_spec((N_STREAM, 3 * LANES, MCH)),
        ],
        out_specs=[out_spec(n), out_spec(nctx)],
        out_shape=[
            jax.ShapeDtypeStruct((nb, n, MP_WIDTH), F32),
            jax.ShapeDtypeStruct((nb, nctx, MP_WIDTH), F32),
        ],
        scratch_shapes=[
            pltpu.VMEM((N_STREAM, HEAD_PAD, HEAD_PAD), F32),
            pltpu.VMEM((N_STREAM, nct, HEAD_PAD, HEAD_PAD), BF16),
            pltpu.VMEM((nct, 3, N_STREAM, MCH), F32),
        ],
        compiler_params=_cparams(("arbitrary",)),
        name="mlstm",
    )(*lat, *ctx, bic, bfc, bir, bfr, sel)


def _mla_proj_kernel(cq_ref, ckv_ref, kr_ref, cos_ref, sin_ref, gq_ref, gkv_ref,
                     wq_ref, wqs_ref, wk_ref, wv_ref, q_ref, k_ref, v_ref, *, with_q):
    cos = cos_ref[...]
    sin = sin_ref[...]
    kvn = _rms(ckv_ref[...], gkv_ref[...]).astype(BF16)
    kr = kr_ref[...]
    k_rope = kr[:, :LANES] * cos + kr[:, LANES:] * sin
    for hd in range(A_HEADS):
        lanes = slice(hd * HEAD_PAD, (hd + 1) * HEAD_PAD)
        k_ref[:, lanes] = (_dot(kvn, wk_ref[:, lanes]) + k_rope).astype(BF16)
        v_ref[:, lanes] = _dot(kvn, wv_ref[:, lanes]).astype(BF16)
    if with_q:
        qn = _rms(cq_ref[...], gq_ref[...]).astype(BF16)
        scale = (A_NOPE + A_ROPE) ** -0.5
        for hd in range(A_HEADS):
            lanes = slice(hd * HEAD_PAD, (hd + 1) * HEAD_PAD)
            q = _dot(qn, wq_ref[:, lanes]) * cos + _dot(qn, wqs_ref[:, lanes]) * sin
            q_ref[:, lanes] = (q * scale).astype(BF16)
    else:
        q_ref[...] = jnp.zeros_like(q_ref)


def _mla_proj(cq, ckv, kr, cos, sin, seq, gq, gkv, wq, wqs, wk, wv, tm, with_q):
    t = cq.shape[0]
    tiles_per_seq = seq // tm
    tok = lambda w: pl.BlockSpec((tm, w), lambda i: (i, 0))
    pos = pl.BlockSpec((tm, LANES), lambda i: (i % tiles_per_seq, 0))
    out = jax.ShapeDtypeStruct((t, AP_WIDTH), BF16)
    return pl.pallas_call(
        functools.partial(_mla_proj_kernel, with_q=with_q),
        grid=(t // tm,),
        in_specs=[
            tok(Q_LORA), tok(KV_LORA), tok(2 * LANES), pos, pos,
            _const_spec((1, Q_LORA)), _const_spec((1, KV_LORA)),
            _const_spec((Q_LORA, AP_WIDTH)), _const_spec((Q_LORA, AP_WIDTH)),
            _const_spec((KV_LORA, AP_WIDTH)), _const_spec((KV_LORA, AP_WIDTH)),
        ],
        out_specs=[tok(AP_WIDTH), tok(AP_WIDTH), tok(AP_WIDTH)],
        out_shape=[out, out, out],
        compiler_params=_cparams(("arbitrary",)),
        name="mla_proj",
    )(cq, ckv, kr, cos, sin, gq, gkv, wq, wqs, wk, wv)


def _attn_kernel(*refs, n_sets):
    q_ref = refs[0]
    kv_refs = refs[1:1 + 2 * n_sets]
    o_ref = refs[1 + 2 * n_sets]
    q = q_ref[0]
    scores = [_dot_nt(q, kv_refs[2 * i][0]) for i in range(n_sets)]
    m = None
    for s in scores:
        sm = jnp.max(s, axis=-1, keepdims=True)
        m = sm if m is None else jnp.maximum(m, sm)
    acc = None
    den = None
    for i, s in enumerate(scores):
        p = jnp.exp(s - m)
        l = jnp.sum(p, axis=-1, keepdims=True)
        o = _dot(p.astype(BF16), kv_refs[2 * i + 1][0])
        acc = o if acc is None else acc + o
        den = l if den is None else den + l
    o_ref[0] = (acc / den).astype(BF16)


def _attention(q, key_sets, tq):
    nb, n, _ = q.shape
    n_sets = len(key_sets)
    in_specs = [pl.BlockSpec((1, tq, HEAD_PAD), lambda b, h, i: (b, i, h))]
    args = [q]
    for k, v in key_sets:
        nk = k.shape[1]
        spec = pl.BlockSpec((1, nk, HEAD_PAD), lambda b, h, i: (b, 0, h))
        in_specs += [spec, spec]
        args += [k, v]
    return pl.pallas_call(
        functools.partial(_attn_kernel, n_sets=n_sets),
        grid=(nb, A_HEADS, n // tq),
        in_specs=in_specs,
        out_specs=pl.BlockSpec((1, tq, HEAD_PAD), lambda b, h, i: (b, i, h)),
        out_shape=jax.ShapeDtypeStruct((nb, n, AP_WIDTH), BF16),
        compiler_params=_cparams(("arbitrary", "arbitrary", "arbitrary")),
        name="attention",
    )(*args)


def _out_mlp_kernel(x_ref, yf_ref, hm_ref, mo_ref, ya_ref, mod_ref,
                    gm_ref, g2_ref, gfin_ref, wof_ref, wom_ref, woa_ref, wup_ref, wdn_ref,
                    o_ref, *, final_norm):
    mod = mod_ref[0]
    ga1, sh2, sc2, ga2 = mod[0:1], mod[1:2], mod[2:3], mod[3:4]
    mix = _dot(yf_ref[...], wof_ref[...]) + _dot(ya_ref[...], woa_ref[...])
    gm = gm_ref[...]
    for hd in range(M_HEADS):
        lanes = slice(hd * HEAD_PAD, (hd + 1) * HEAD_PAD)
        hh = hm_ref[:, lanes]
        ms = jnp.sum(hh * hh, axis=-1, keepdims=True) * (1.0 / M_HEAD_DIM)
        ym = hh * lax.rsqrt(ms + EPS) * gm[:, lanes] * jax.nn.sigmoid(mo_ref[:, lanes])
        mix = mix + _dot(ym.astype(BF16), wom_ref[lanes, :])
    x1 = x_ref[...] + ga1 * mix
    h2 = (_rms(x1, g2_ref[...]) * (1.0 + sc2) + sh2).astype(BF16)
    hidden = wup_ref.shape[1]
    hc = 1024
    acc = None
    for c in range(hidden // hc):
        u = jnp.maximum(_dot(h2, wup_ref[:, c * hc:(c + 1) * hc]), 0.0)
        d = _dot((u * u).astype(BF16), wdn_ref[c * hc:(c + 1) * hc, :])
        acc = d if acc is None else acc + d
    x2 = x1 + ga2 * acc
    if final_norm:
        x2 = _rms(x2, gfin_ref[...])
    o_ref[...] = x2


def _out_mlp(x2d, seq, yf, hm, mo, ya, mod4, per_batch_mod, gm, g2, gfin,
             wof, wom, woa, wup, wdn, tm, final_norm):
    t, d = x2d.shape
    tiles_per_seq = seq // tm
    if per_batch_mod:
        mod_map = lambda i: (i // tiles_per_seq, 0, 0)
    else:
        mod_map = lambda i: (0, 0, 0)
    tok = lambda w: pl.BlockSpec((tm, w), lambda i: (i, 0))
    yf_spec = pl.BlockSpec((tm, F_WIDTH), lambda i: (i % tiles_per_seq, i // tiles_per_seq))
    return pl.pallas_call(
        functools.partial(_out_mlp_kernel, final_norm=final_norm),
        grid=(t // tm,),
        in_specs=[
            tok(d), yf_spec, tok(MP_WIDTH), tok(MP_WIDTH), tok(AP_WIDTH),
            pl.BlockSpec((1, 4, d), mod_map),
            _const_spec((1, MP_WIDTH)), _const_spec((1, d)), _const_spec((1, d)),
            _const_spec((F_WIDTH, d)), _const_spec((MP_WIDTH, d)), _const_spec((AP_WIDTH, d)),
            _const_spec((d, wup.shape[1])), _const_spec((wdn.shape[0], d)),
        ],
        out_specs=tok(d),
        out_shape=jax.ShapeDtypeStruct((t, d), F32),
        compiler_params=_cparams(("arbitrary",)),
        name="out_mlp",
    )(x2d, yf, hm, mo, ya, mod4, gm, g2, gfin, wof, wom, woa, wup, wdn)


def _dft_tables(n):
    idx = (np.arange(n, dtype=np.int64)[:, None] * np.arange(n, dtype=np.int64)[None, :]) % n
    ang = 2.0 * np.pi * idx.astype(np.float64) / n
    scale = 1.0 / np.sqrt(n)
    return np.cos(ang) * scale, np.sin(ang) * scale


def _channel_dft():
    c, s = _dft_tables(F_GROUP_DIM)
    eye = np.eye(F_GROUPS)
    return (jnp.asarray(np.kron(eye, c), dtype=F32).astype(BF16),
            jnp.asarray(np.kron(eye, s), dtype=F32).astype(BF16))


def _position_dft(n):
    c, s = _dft_tables(n)
    return jnp.asarray(c, dtype=F32).astype(BF16), jnp.asarray(s, dtype=F32).astype(BF16)


def _rope_tables(n, rotate):
    cos = np.zeros((n, HEAD_PAD), np.float32)
    sin = np.zeros((n, HEAD_PAD), np.float32)
    cos[:, :A_NOPE + A_ROPE] = 1.0
    if rotate:
        nf = A_ROPE // 4
        t = np.arange(n)
        row = (t // GRID_W).astype(np.float32)
        col = (t % GRID_W).astype(np.float32)
        freqs = (np.float32(ROPE_THETA) ** (-np.arange(nf, dtype=np.float32) / np.float32(nf))).astype(np.float32)
        for seg, pos in enumerate((row, col)):
            ang = pos[:, None] * freqs[None, :]
            c, s = np.cos(ang), np.sin(ang)
            base = A_NOPE + seg * 2 * nf
            cos[:, base:base + nf] = c
            cos[:, base + nf:base + 2 * nf] = c
            sin[:, base:base + nf] = -s
            sin[:, base + nf:base + 2 * nf] = s
    return jnp.asarray(cos), jnp.asarray(sin)


def _rope_swap_perm():
    nf = A_ROPE // 4
    perm = np.arange(A_ROPE)
    for seg in range(2):
        b = seg * 2 * nf
        perm[b:b + nf] = np.arange(b + nf, b + 2 * nf)
        perm[b + nf:b + 2 * nf] = np.arange(b, b + nf)
    return perm


def _pad_heads_cols(w, heads, width):
    lead = w.shape[:-1]
    w = w.reshape(lead + (heads, width))
    w = jnp.pad(w, [(0, 0)] * len(lead) + [(0, 0), (0, HEAD_PAD - width)])
    return w.reshape(lead + (heads * HEAD_PAD,))


def _pad_cols(w, width):
    return jnp.pad(w, [(0, 0)] * (w.ndim - 1) + [(0, width - w.shape[-1])])


def _layer_weights(l, w_in, conv_qk, b_gates, g_mlstm, w_uq, w_ukv, w_out):
    swap = _rope_swap_perm()
    wl = w_in[l]
    offs = np.cumsum([0, F_WIDTH, M_WIDTH, M_WIDTH, M_WIDTH, M_WIDTH, 4 * M_HEADS, Q_LORA, KV_LORA, A_ROPE])
    part = lambda i: wl[:, offs[i]:offs[i + 1]]
    kr_w = part(8)
    place_rope = lambda w: jnp.pad(w, [(0, 0), (A_NOPE, LANES - A_NOPE - A_ROPE)])
    w_in_p = jnp.concatenate([
        part(0),
        _pad_heads_cols(part(1), M_HEADS, M_HEAD_DIM),
        _pad_heads_cols(part(2), M_HEADS, M_HEAD_DIM),
        _pad_heads_cols(part(4), M_HEADS, M_HEAD_DIM),
        _pad_cols(part(5)[:, GATE_I_COLS], LANES),
        _pad_cols(part(5)[:, GATE_F_COLS], LANES),
        part(6),
        part(7),
        place_rope(kr_w),
        place_rope(kr_w[:, swap]),
    ], axis=1).astype(BF16)
    assert w_in_p.shape[1] == IN_PAD
    w_vt = _pad_heads_cols(part(3), M_HEADS, M_HEAD_DIM).T.astype(BF16)

    conv = conv_qk[l]
    conv_p = jnp.concatenate([
        _pad_heads_cols(conv[:, :M_WIDTH], M_HEADS, M_HEAD_DIM),
        _pad_heads_cols(conv[:, M_WIDTH:], M_HEADS, M_HEAD_DIM),
    ], axis=1)
    conv_p = jnp.pad(conv_p, [(0, 8 - K_CONV), (0, 0)])

    bg = b_gates[l]
    bi, bf = bg[GATE_I_COLS], bg[GATE_F_COLS]
    gate_bias = dict(bic=_pad_cols(bi[None, :], LANES), bfc=_pad_cols(bf[None, :], LANES),
                     bir=bi[:, None], bfr=bf[:, None])

    gm = _pad_heads_cols(g_mlstm[l][None, :], M_HEADS, M_HEAD_DIM)

    uq = w_uq[l].reshape(Q_LORA, A_HEADS, A_NOPE + A_ROPE)
    uq_s = jnp.concatenate([jnp.zeros_like(uq[..., :A_NOPE]), uq[..., A_NOPE:][..., swap]], axis=-1)
    pad_q = lambda w: jnp.pad(w, [(0, 0), (0, 0), (0, HEAD_PAD - A_NOPE - A_ROPE)]).reshape(Q_LORA, AP_WIDTH)
    wq = pad_q(uq).astype(BF16)
    wqs = pad_q(uq_s).astype(BF16)
    ukv = w_ukv[l].reshape(KV_LORA, A_HEADS, A_NOPE + A_V)
    wk = jnp.pad(ukv[..., :A_NOPE], [(0, 0), (0, 0), (0, HEAD_PAD - A_NOPE)]).reshape(KV_LORA, AP_WIDTH).astype(BF16)
    wv = jnp.pad(ukv[..., A_NOPE:], [(0, 0), (0, 0), (0, HEAD_PAD - A_V)]).reshape(KV_LORA, AP_WIDTH).astype(BF16)

    wo = w_out[l]
    pad_rows = lambda w, heads, width: jnp.pad(
        w.reshape(heads, width, -1), [(0, 0), (0, HEAD_PAD - width), (0, 0)]).reshape(heads * HEAD_PAD, -1)
    wof = wo[:F_WIDTH].astype(BF16)
    wom = pad_rows(wo[F_WIDTH:F_WIDTH + M_WIDTH], M_HEADS, M_HEAD_DIM).astype(BF16)
    woa = pad_rows(wo[F_WIDTH + M_WIDTH:], A_HEADS, A_V).astype(BF16)
    return dict(w_in_p=w_in_p, w_vt=w_vt, conv_p=conv_p, gate_bias=gate_bias, gm=gm,
                wq=wq, wqs=wqs, wk=wk, wv=wv, wof=wof, wom=wom, woa=woa)


GATE_I_COLS = np.concatenate([np.arange(M_HEADS), 2 * M_HEADS + np.arange(M_HEADS)])
GATE_F_COLS = GATE_I_COLS + M_HEADS


def _gates_rowmajor(g, nb, n):
    g = g[:, :N_STREAM].reshape(nb, n // MCH, MCH, N_STREAM)
    return g.transpose(0, 1, 3, 2)


def _stream_selectors():
    sel = np.zeros((N_STREAM, 3 * LANES, MCH), np.float32)
    for s in range(N_STREAM):
        for part in range(3):
            sel[s, part * LANES + s, :] = 1.0
    return jnp.asarray(sel, dtype=BF16)


def kernel(x, c, ctx, c_ctx, w_mod, b_mod, g_norm1, g_norm2, w_in, b_gates, conv_qk, g_mlstm,
           g_q_norm, g_kv_norm, w_uq, w_ukv, w_out, w_up, w_down, g_final):
    nb, seq, d = x.shape
    nctx = ctx.shape[1]
    depth = w_mod.shape[0]
    assert d == D_MODEL and seq % 256 == 0 and nctx % MCH == 0
    sel = _stream_selectors()

    tm = 256
    tm_ctx = min(256, nctx)
    tq = 256
    tq_ctx = min(256, nctx)

    dft_cc, dft_cs = _channel_dft()
    dft_lat = _position_dft(seq)
    dft_ctx = _position_dft(nctx)
    rope_lat = _rope_tables(seq, True)
    rope_ctx = _rope_tables(nctx, False)

    rows = ((nb + 1 + 7) // 8) * 8
    cc = jnp.concatenate([c, c_ctx[None, :], jnp.zeros((rows - nb - 1, d), F32)], axis=0)
    mod_all = _modulation(cc, w_mod, b_mod)

    xl = x.reshape(nb * seq, d)
    xc = ctx.reshape(nb * nctx, d)
    row = lambda v: v.reshape(1, -1)

    for l in range(depth):
        last = l == depth - 1
        lw = _layer_weights(l, w_in, conv_qk, b_gates, g_mlstm, w_uq, w_ukv, w_out)
        wup = w_up[l].astype(BF16)
        wdn = w_down[l].astype(BF16)
        mod = mod_all[l].reshape(rows, 6, d)
        mod_lat, mod_ctx = mod[:nb], mod[nb:nb + 1]

        def tokenwise(xt, n, m, per_batch, tile):
            return _inproj(xt, n, row(g_norm1[l]), m[:, 0:1], m[:, 1:2], per_batch,
                           lw["w_in_p"], lw["w_vt"], dft_cc, dft_cs, tile)

        zc, zs, mqk, vt, mo, gi, gf, cq, ckv, kr = tokenwise(xl, seq, mod_lat, True, tm)
        zc_c, zs_c, mqk_c, vt_c, mo_c, gi_c, gf_c, cq_c, ckv_c, kr_c = tokenwise(xc, nctx, mod_ctx, False, tm_ctx)

        yf = _fourier(*dft_lat, zc, zs)

        def mlstm_inputs(mqk_s, vt_s, gi_s, gf_s, n):
            q_s, k_s = _conv_silu(mqk_s.reshape(nb, n, 2 * MP_WIDTH), lw["conv_p"])
            return (q_s, k_s, vt_s, gi_s.reshape(nb, n, LANES), gf_s.reshape(nb, n, LANES),
                    _gates_rowmajor(gi_s, nb, n), _gates_rowmajor(gf_s, nb, n))

        hm, hm_c = _mlstm(mlstm_inputs(mqk, vt, gi, gf, seq), mlstm_inputs(mqk_c, vt_c, gi_c, gf_c, nctx),
                          sel=sel, **lw["gate_bias"])

        gq, gkv = row(g_q_norm[l]), row(g_kv_norm[l])
        q_a, k_a, v_a = _mla_proj(cq, ckv, kr, *rope_lat, seq, gq, gkv,
                                  lw["wq"], lw["wqs"], lw["wk"], lw["wv"], tm, True)
        q_ac, k_ac, v_ac = _mla_proj(cq_c, ckv_c, kr_c, *rope_ctx, nctx, gq, gkv,
                                     lw["wq"], lw["wqs"], lw["wk"], lw["wv"], tm_ctx, not last)
        b3 = lambda a, n: a.reshape(nb, n, AP_WIDTH)
        keys_ctx = (b3(k_ac, nctx), b3(v_ac, nctx))
        ya = _attention(b3(q_a, seq), [(b3(k_a, seq), b3(v_a, seq)), keys_ctx], tq)

        mlp = functools.partial(
            _out_mlp, gm=lw["gm"], g2=row(g_norm2[l]), gfin=row(g_final),
            wof=lw["wof"], wom=lw["wom"], woa=lw["woa"], wup=wup, wdn=wdn)
        xl = mlp(xl, seq, yf, hm.reshape(nb * seq, MP_WIDTH), mo, ya.reshape(nb * seq, AP_WIDTH),
                 mod_lat[:, 2:6], True, tm=tm, final_norm=last)

        if not last:
            yf_c = _fourier(*dft_ctx, zc_c, zs_c)
            ya_c = _attention(b3(q_ac, nctx), [keys_ctx], tq_ctx)
            xc = mlp(xc, nctx, yf_c, hm_c.reshape(nb * nctx, MP_WIDTH), mo_c,
                     ya_c.reshape(nb * nctx, AP_WIDTH), mod_ctx[:, 2:6], False, tm=tm_ctx, final_norm=False)

    return xl.reshape(nb, seq, d)
```

```python
import functools

import numpy as np
import jax
import jax.numpy as jnp
from jax import lax
from jax.experimental import pallas as pl
from jax.experimental.pallas import tpu as pltpu

D_MODEL = 1024
GRID_W = 64
EPS = 1e-6
F_GROUPS = 4
F_GROUP_DIM = D_MODEL // 16
F_WIDTH = F_GROUPS * F_GROUP_DIM
M_HEADS = 4
M_HEAD_DIM = 3 * D_MODEL // 32
M_WIDTH = M_HEADS * M_HEAD_DIM
M_CHUNK = 64
K_CONV = 5
A_HEADS = 4
A_NOPE = 64
A_ROPE = 32
A_V = 3 * D_MODEL // 32
Q_LORA = D_MODEL // 4
KV_LORA = D_MODEL // 8
ROPE_THETA = 10000.0
MLP_HIDDEN = 4 * D_MODEL

LANES = 128
HEAD_PAD = 128
MP_WIDTH = M_HEADS * HEAD_PAD
AP_WIDTH = A_HEADS * HEAD_PAD
VMEM_LIMIT = 56 * 1024 * 1024
MCH = 128

OFF_PF = 0
OFF_MQ = OFF_PF + F_WIDTH
OFF_MK = OFF_MQ + MP_WIDTH
OFF_MO = OFF_MK + MP_WIDTH
OFF_GI = OFF_MO + MP_WIDTH
OFF_GF = OFF_GI + LANES
OFF_CQ = OFF_GF + LANES
OFF_CKV = OFF_CQ + Q_LORA
OFF_KR = OFF_CKV + KV_LORA
OFF_KRS = OFF_KR + LANES
IN_PAD = OFF_KRS + LANES

BF16 = jnp.bfloat16
F32 = jnp.float32


def _cparams(sem):
    return pltpu.CompilerParams(dimension_semantics=sem, vmem_limit_bytes=VMEM_LIMIT)


def _const_spec(shape):
    nd = len(shape)
    return pl.BlockSpec(shape, lambda *_: (0,) * nd, pipeline_mode=pl.Buffered(1))


def _split3(a):
    hi = a.astype(BF16)
    r1 = a - hi.astype(F32)
    mid = r1.astype(BF16)
    lo = (r1 - mid.astype(F32)).astype(BF16)
    return hi, mid, lo


def _dot(a, b):
    return jnp.dot(a, b, preferred_element_type=F32)


def _dot_nt(a, b):
    return lax.dot_general(a, b, (((1,), (1,)), ((), ())), preferred_element_type=F32)


def _dot_tn(a, b):
    return lax.dot_general(a, b, (((0,), (0,)), ((), ())), preferred_element_type=F32)


def _rms(x, g):
    return x * lax.rsqrt(jnp.mean(x * x, axis=-1, keepdims=True) + EPS) * g


def _mod_kernel(c_ref, w_ref, b_ref, o_ref):
    c = c_ref[...]
    a = c * jax.nn.sigmoid(c)
    a_hi = a.astype(BF16)
    a_lo = (a - a_hi.astype(F32)).astype(BF16)
    w = w_ref[0]
    w_hi = w.astype(BF16)
    w_lo = (w - w_hi.astype(F32)).astype(BF16)
    acc = _dot(a_hi, w_hi) + _dot(a_hi, w_lo) + _dot(a_lo, w_hi)
    o_ref[0] = acc + b_ref[0]


def _modulation(cc, w_mod, b_mod):
    depth, d, n = w_mod.shape
    rows = cc.shape[0]
    tn = 1536
    return pl.pallas_call(
        _mod_kernel,
        grid=(depth, n // tn),
        in_specs=[
            pl.BlockSpec((rows, d), lambda l, j: (0, 0)),
            pl.BlockSpec((1, d, tn), lambda l, j: (l, 0, j)),
            pl.BlockSpec((1, 1, tn), lambda l, j: (l, 0, j)),
        ],
        out_specs=pl.BlockSpec((1, rows, tn), lambda l, j: (l, 0, j)),
        out_shape=jax.ShapeDtypeStruct((depth, rows, n), F32),
        compiler_params=_cparams(("arbitrary", "arbitrary")),
        name="modulation",
    )(cc, w_mod, b_mod.reshape(depth, 1, n))


def _inproj_kernel(x_ref, g_ref, sh_ref, sc_ref, w_ref, wvt_ref, cc_ref, cs_ref,
                   zc_ref, zs_ref, mqk_ref, vt_ref, mo_ref, gi_ref, gf_ref, cq_ref, ckv_ref, kr_ref):
    x = x_ref[...]
    h = _rms(x, g_ref[...]) * (1.0 + sc_ref[0]) + sh_ref[0]
    hb = h.astype(BF16)

    def proj(off, width):
        return _dot(hb, w_ref[:, off:off + width])

    pf = proj(OFF_PF, F_WIDTH).astype(BF16)
    zc_ref[...] = _dot(pf, cc_ref[...]).astype(BF16)
    zs_ref[...] = _dot(pf, cs_ref[...]).astype(BF16)
    mqk_ref[...] = proj(OFF_MQ, 2 * MP_WIDTH)
    vt = _dot_nt(wvt_ref[...], hb).astype(BF16)
    for j in range(vt_ref.shape[0]):
        vt_ref[j] = vt[:, j * MCH:(j + 1) * MCH]
    mo_ref[...] = proj(OFF_MO, MP_WIDTH)
    gi_ref[...] = proj(OFF_GI, LANES)
    gf_ref[...] = proj(OFF_GF, LANES)
    cq_ref[...] = proj(OFF_CQ, Q_LORA)
    ckv_ref[...] = proj(OFF_CKV, KV_LORA)
    kr_ref[...] = proj(OFF_KR, 2 * LANES)


def _inproj(x2d, seq, g, sh, sc, per_batch_mod, w_in_p, w_vt, dft_cc, dft_cs, tm):
    t, d = x2d.shape
    nb = t // seq
    tiles_per_seq = seq // tm
    if per_batch_mod:
        mod_map = lambda i: (i // tiles_per_seq, 0, 0)
    else:
        mod_map = lambda i: (0, 0, 0)
    tok = lambda w: pl.BlockSpec((tm, w), lambda i: (i, 0))
    z_spec = pl.BlockSpec((tm, F_WIDTH), lambda i: (i % tiles_per_seq, i // tiles_per_seq))
    shapes = [
        jax.ShapeDtypeStruct((seq, nb * F_WIDTH), BF16),
        jax.ShapeDtypeStruct((seq, nb * F_WIDTH), BF16),
        jax.ShapeDtypeStruct((t, 2 * MP_WIDTH), F32),
        jax.ShapeDtypeStruct((t // MCH, MP_WIDTH, MCH), BF16),
        jax.ShapeDtypeStruct((t, MP_WIDTH), F32),
        jax.ShapeDtypeStruct((t, LANES), F32),
        jax.ShapeDtypeStruct((t, LANES), F32),
        jax.ShapeDtypeStruct((t, Q_LORA), F32),
        jax.ShapeDtypeStruct((t, KV_LORA), F32),
        jax.ShapeDtypeStruct((t, 2 * LANES), F32),
    ]
    vt_spec = pl.BlockSpec((tm // MCH, MP_WIDTH, MCH), lambda i: (i, 0, 0))
    out_specs = [z_spec, z_spec, tok(2 * MP_WIDTH), vt_spec, tok(MP_WIDTH), tok(LANES), tok(LANES),
                 tok(Q_LORA), tok(KV_LORA), tok(2 * LANES)]
    return pl.pallas_call(
        _inproj_kernel,
        grid=(t // tm,),
        in_specs=[
            tok(d),
            _const_spec((1, d)),
            pl.BlockSpec((1, 1, d), mod_map),
            pl.BlockSpec((1, 1, d), mod_map),
            _const_spec((d, IN_PAD)),
            _const_spec((MP_WIDTH, d)),
            _const_spec((F_WIDTH, F_WIDTH)),
            _const_spec((F_WIDTH, F_WIDTH)),
        ],
        out_specs=out_specs,
        out_shape=shapes,
        compiler_params=_cparams(("arbitrary",)),
        name="inproj",
    )(x2d, g, sh, sc, w_in_p, w_vt, dft_cc, dft_cs)


def _fourier_kernel(c_ref, s_ref, zc_ref, zs_ref, o_ref):
    y = _dot(c_ref[...], zc_ref[...]) - _dot(s_ref[...], zs_ref[...])
    o_ref[...] = y.astype(BF16)


def _fourier(dft_c, dft_s, zc, zs):
    n, cols = zc.shape
    tr = min(n, 512)
    tc = min(cols, 512)
    return pl.pallas_call(
        _fourier_kernel,
        grid=(n // tr, cols // tc),
        in_specs=[
            pl.BlockSpec((tr, n), lambda i, j: (i, 0)),
            pl.BlockSpec((tr, n), lambda i, j: (i, 0)),
            pl.BlockSpec((n, tc), lambda i, j: (0, j)),
            pl.BlockSpec((n, tc), lambda i, j: (0, j)),
        ],
        out_specs=pl.BlockSpec((tr, tc), lambda i, j: (i, j)),
        out_shape=jax.ShapeDtypeStruct((n, cols), BF16),
        compiler_params=_cparams(("arbitrary", "arbitrary")),
        name="fourier",
    )(dft_c, dft_s, zc, zs)


CONV_ROWS = 256
CONV_HALO = 8


def _conv_kernel(u_ref, w_ref, q_ref, k_ref, pad_ref):
    n = u_ref.shape[1]
    width = u_ref.shape[2]
    zeros = jnp.zeros((CONV_HALO, width), F32)
    pad_ref[0:CONV_HALO, :] = zeros
    pad_ref[CONV_HALO + n:2 * CONV_HALO + n, :] = zeros
    pad_ref[CONV_HALO:CONV_HALO + n, :] = u_ref[0]
    w = w_ref[...]
    rows = min(CONV_ROWS, n)
    for r in range(n // rows):
        base = CONV_HALO + r * rows - K_CONV // 2
        acc = pad_ref[base:base + rows, :] * w[0:1, :]
        for j in range(1, K_CONV):
            acc = acc + pad_ref[base + j:base + j + rows, :] * w[j:j + 1, :]
        act = acc * jax.nn.sigmoid(acc)
        q_ref[0, r * rows:(r + 1) * rows, :] = (act[:, :MP_WIDTH] * (M_HEAD_DIM ** -0.5)).astype(BF16)
        k_ref[0, r * rows:(r + 1) * rows, :] = act[:, MP_WIDTH:].astype(BF16)


def _conv_silu(mqk, conv_w_p):
    nb, n, width = mqk.shape
    return pl.pallas_call(
        _conv_kernel,
        grid=(nb,),
        in_specs=[
            pl.BlockSpec((1, n, width), lambda b: (b, 0, 0)),
            _const_spec((8, width)),
        ],
        out_specs=[
            pl.BlockSpec((1, n, MP_WIDTH), lambda b: (b, 0, 0)),
            pl.BlockSpec((1, n, MP_WIDTH), lambda b: (b, 0, 0)),
        ],
        out_shape=[
            jax.ShapeDtypeStruct((nb, n, MP_WIDTH), BF16),
            jax.ShapeDtypeStruct((nb, n, MP_WIDTH), BF16),
        ],
        scratch_shapes=[pltpu.VMEM((n + 2 * CONV_HALO, width), F32)],
        compiler_params=_cparams(("arbitrary",)),
        name="conv_silu",
    )(mqk, conv_w_p)


def _log_sigmoid(x):
    return jnp.minimum(x, 0.0) - jnp.log(1.0 + jnp.exp(-jnp.abs(x)))


def _exact_dot_01(a, tri_bf16, a_on_left):
    out = None
    for term in _split3(a):
        d = _dot(term, tri_bf16) if a_on_left else _dot(tri_bf16, term)
        out = d if out is None else out + d
    return out


N_STREAM = 2 * M_HEADS
ONE_ROW = M_HEAD_DIM
ROW_A, ROW_WI, ROW_ELD = 0, 1, 2


def _mlstm_kernel(ql_ref, kl_ref, vtl_ref, gil_ref, gfl_ref, gril_ref, grfl_ref,
                  qc_ref, kc_ref, vtc_ref, gic_ref, gfc_ref, gric_ref, grfc_ref,
                  bic_ref, bfc_ref, bir_ref, bfr_ref, sel_ref,
                  hl_ref, hc_ref,
                  ct_st, st_sc, rows_sc):
    L = MCH
    nh = M_HEADS
    ncc = qc_ref.shape[1] // L
    ncl = ql_ref.shape[1] // L

    d0 = lax.broadcasted_iota(jnp.int32, (L, L), 0)
    d1 = lax.broadcasted_iota(jnp.int32, (L, L), 1)
    le = d0 <= d1
    ge = d0 >= d1
    tri_le = le.astype(BF16)
    tri_ge = ge.astype(BF16)
    lane8 = lax.broadcasted_iota(jnp.int32, (N_STREAM, L), 1)
    fwd_rows = lax.broadcasted_iota(jnp.int32, (N_STREAM, L), 0) < nh
    fwd_lanes = lax.broadcasted_iota(jnp.int32, (L, LANES), 1) < nh
    feat = lax.broadcasted_iota(jnp.int32, (HEAD_PAD, L), 0)
    one_row = feat == ONE_ROW
    keep_rows = feat < M_HEAD_DIM

    bic, bfc = bic_ref[...], bfc_ref[...]
    bir, bfr = bir_ref[...], bfr_ref[...]

    ct_st[...] = jnp.zeros_like(ct_st)

    def value_slab(vt_ref, c, hd):
        vt = vt_ref[c, hd * HEAD_PAD:(hd + 1) * HEAD_PAD, :].astype(F32)
        return jnp.where(one_row, 1.0, vt)

    def pass1(k_ref, vt_ref, gri_ref, grf_ref, nc, slot0, m0):
        def step(j, m_prev):
            cf = j
            cb = nc - 1 - j
            gi = jnp.where(fwd_rows, gri_ref[0, cf], gri_ref[0, cb]) + bir
            f = _log_sigmoid(jnp.where(fwd_rows, grf_ref[0, cf], grf_ref[0, cb]) + bfr)
            pre = _exact_dot_01(f, tri_le, a_on_left=True)
            total = jnp.sum(f, axis=1, keepdims=True)
            b = jnp.where(fwd_rows, pre, total - pre + f)
            r = gi - b
            cm = r
            sh = 1
            while sh < L:
                from_left = jnp.where(lane8 >= sh, pltpu.roll(cm, sh, 1), -jnp.inf)
                from_right = jnp.where(lane8 < L - sh, pltpu.roll(cm, L - sh, 1), -jnp.inf)
                cm = jnp.maximum(cm, jnp.where(fwd_rows, from_left, from_right))
                sh *= 2
            big = jnp.maximum(m_prev, jnp.max(r, axis=1, keepdims=True))
            a = -jnp.maximum(m_prev, cm)
            per_slot = {ROW_A: a, ROW_WI: jnp.exp(m_prev + a), ROW_ELD: jnp.exp(a - b)}
            for idx, val in per_slot.items():
                rows_sc[slot0 + cf, idx, 0:nh, :] = val[0:nh]
                rows_sc[slot0 + cb, idx, nh:N_STREAM, :] = val[nh:N_STREAM]
            decay = jnp.exp(m_prev - big)
            wk = jnp.exp(r - big)
            for sidx in range(N_STREAM):
                hd = sidx % nh
                c = cf if sidx < nh else cb
                kk = k_ref[0, pl.ds(pl.multiple_of(c * L, L), L), hd * HEAD_PAD:(hd + 1) * HEAD_PAD]
                ut = _dot((value_slab(vt_ref, c, hd) * wk[sidx:sidx + 1, :]).astype(BF16), kk)
                prev = ct_st[sidx]
                st_sc[sidx, slot0 + c] = prev.astype(BF16)
                ct_st[sidx] = decay[sidx:sidx + 1, :] * prev + ut
            return total + big

        return lax.fori_loop(0, nc, step, m0)

    def pass2(q_ref, k_ref, vt_ref, gi_ref, gf_ref, out_ref, nc, slot0):
        def step(c, carry):
            rows = pl.ds(pl.multiple_of(c * L, L), L)
            gi_col = gi_ref[0, rows, :] + bic
            f_col = _log_sigmoid(gf_ref[0, rows, :] + bfc)
            pre = _exact_dot_01(f_col, tri_ge, a_on_left=False)
            total = jnp.sum(f_col, axis=0, keepdims=True)
            r_col = gi_col - jnp.where(fwd_lanes, pre, total - pre + f_col)
            r3 = jnp.concatenate(_split3(r_col), axis=1)
            a_rows = rows_sc[slot0 + c, ROW_A]
            wi_rows = rows_sc[slot0 + c, ROW_WI]
            eld_rows = rows_sc[slot0 + c, ROW_ELD]
            head_lanes = [slice(hd * HEAD_PAD, (hd + 1) * HEAD_PAD) for hd in range(nh)]
            qs = [q_ref[0, rows, lanes] for lanes in head_lanes]
            s_ts = [_dot_nt(k_ref[0, rows, lanes], q) for lanes, q in zip(head_lanes, qs)]
            inters = [_dot_nt(st_sc[sidx, slot0 + c], qs[sidx % nh]) for sidx in range(N_STREAM)]
            p_ts = []
            for sidx in range(N_STREAM):
                one = slice(sidx, sidx + 1)
                dm = jnp.where(le if sidx < nh else ge, r_col[:, one] + a_rows[one, :], -jnp.inf)
                p_ts.append((s_ts[sidx % nh] * jnp.exp(dm)).astype(BF16))
            vtas = [value_slab(vt_ref, c, hd).astype(BF16) for hd in range(nh)]
            z_ts = [_dot(vtas[sidx % nh], p_ts[sidx]) for sidx in range(N_STREAM)]
            for hd in range(nh):
                hsum = None
                for sidx in (hd, nh + hd):
                    one = slice(sidx, sidx + 1)
                    z_t = z_ts[sidx] + inters[sidx] * wi_rows[one, :]
                    den = z_t[ONE_ROW:ONE_ROW + 1, :]
                    h_t = z_t * (1.0 / jnp.maximum(jnp.abs(den), eld_rows[one, :]))
                    hsum = h_t if hsum is None else hsum + h_t
                out_ref[0, rows, head_lanes[hd]] = jnp.where(keep_rows, hsum, 0.0).T
            return carry

        lax.fori_loop(0, nc, step, 0)

    m0 = jnp.zeros((N_STREAM, L), F32)
    m1 = pass1(kc_ref, vtc_ref, gric_ref, grfc_ref, ncc, 0, m0)
    pass1(kl_ref, vtl_ref, gril_ref, grfl_ref, ncl, ncc, m1)
    pass2(qc_ref, kc_ref, vtc_ref, gic_ref, gfc_ref, hc_ref, ncc, 0)
    pass2(ql_ref, kl_ref, vtl_ref, gil_ref, gfl_ref, hl_ref, ncl, ncc)


def _mlstm(lat, ctx, bic, bfc, bir, bfr, sel):
    nb, n, _ = lat[0].shape
    nctx = ctx[0].shape[1]
    assert MCH == LANES == HEAD_PAD
    nct = (n + nctx) // MCH

    def specs(rows):
        nc = rows // MCH
        seq = lambda w: pl.BlockSpec((1, rows, w), lambda b: (b, 0, 0))
        gr = pl.BlockSpec((1, nc, N_STREAM, MCH), lambda b: (b, 0, 0, 0))
        return [seq(MP_WIDTH), seq(MP_WIDTH), pl.BlockSpec((nc, MP_WIDTH, MCH), lambda b: (b, 0, 0)),
                seq(LANES), seq(LANES), gr, gr]

    out_spec = lambda rows: pl.BlockSpec((1, rows, MP_WIDTH), lambda b: (b, 0, 0))
    return pl.pallas_call(
        _mlstm_kernel,
        grid=(nb,),
        in_specs=specs(n) + specs(nctx) + [
            _const_spec((1, LANES)), _const_spec((1, LANES)),
            _const_spec((N_STREAM, 1)), _const_spec((N_STREAM, 1)),
            _const_spec((N_STREAM, 3 * LANES, MCH)),
        ],
        out_specs=[out_spec(n), out_spec(nctx)],
        out_shape=[
            jax.ShapeDtypeStruct((nb, n, MP_WIDTH), F32),
            jax.ShapeDtypeStruct((nb, nctx, MP_WIDTH), F32),
        ],
        scratch_shapes=[
            pltpu.VMEM((N_STREAM, HEAD_PAD, HEAD_PAD), F32),
            pltpu.VMEM((N_STREAM, nct, HEAD_PAD, HEAD_PAD), BF16),
            pltpu.VMEM((nct, 3, N_STREAM, MCH), F32),
        ],
        compiler_params=_cparams(("arbitrary",)),
        name="mlstm",
    )(*lat, *ctx, bic, bfc, bir, bfr, sel)


def _mla_proj_kernel(cq_ref, ckv_ref, kr_ref, cos_ref, sin_ref, gq_ref, gkv_ref,
                     wq_ref, wqs_ref, wk_ref, wv_ref, q_ref, k_ref, v_ref, *, with_q):
    cos = cos_ref[...]
    sin = sin_ref[...]
    kvn = _rms(ckv_ref[...], gkv_ref[...]).astype(BF16)
    kr = kr_ref[...]
    k_rope = kr[:, :LANES] * cos + kr[:, LANES:] * sin
    for hd in range(A_HEADS):
        lanes = slice(hd * HEAD_PAD, (hd + 1) * HEAD_PAD)
        k_ref[:, lanes] = (_dot(kvn, wk_ref[:, lanes]) + k_rope).astype(BF16)
        v_ref[:, lanes] = _dot(kvn, wv_ref[:, lanes]).astype(BF16)
    if with_q:
        qn = _rms(cq_ref[...], gq_ref[...]).astype(BF16)
        scale = (A_NOPE + A_ROPE) ** -0.5
        for hd in range(A_HEADS):
            lanes = slice(hd * HEAD_PAD, (hd + 1) * HEAD_PAD)
            q = _dot(qn, wq_ref[:, lanes]) * cos + _dot(qn, wqs_ref[:, lanes]) * sin
            q_ref[:, lanes] = (q * scale).astype(BF16)
    else:
        q_ref[...] = jnp.zeros_like(q_ref)


def _mla_proj(cq, ckv, kr, cos, sin, seq, gq, gkv, wq, wqs, wk, wv, tm, with_q):
    t = cq.shape[0]
    tiles_per_seq = seq // tm
    tok = lambda w: pl.BlockSpec((tm, w), lambda i: (i, 0))
    pos = pl.BlockSpec((tm, LANES), lambda i: (i % tiles_per_seq, 0))
    out = jax.ShapeDtypeStruct((t, AP_WIDTH), BF16)
    return pl.pallas_call(
        functools.partial(_mla_proj_kernel, with_q=with_q),
        grid=(t // tm,),
        in_specs=[
            tok(Q_LORA), tok(KV_LORA), tok(2 * LANES), pos, pos,
            _const_spec((1, Q_LORA)), _const_spec((1, KV_LORA)),
            _const_spec((Q_LORA, AP_WIDTH)), _const_spec((Q_LORA, AP_WIDTH)),
            _const_spec((KV_LORA, AP_WIDTH)), _const_spec((KV_LORA, AP_WIDTH)),
        ],
        out_specs=[tok(AP_WIDTH), tok(AP_WIDTH), tok(AP_WIDTH)],
        out_shape=[out, out, out],
        compiler_params=_cparams(("arbitrary",)),
        name="mla_proj",
    )(cq, ckv, kr, cos, sin, gq, gkv, wq, wqs, wk, wv)


ATTN_SUB = 256


def _attn_kernel(*refs, n_sets):
    q_ref = refs[0]
    kv_refs = refs[1:1 + 2 * n_sets]
    o_ref = refs[1 + 2 * n_sets]
    sub = min(ATTN_SUB, q_ref.shape[1])
    n_sub = q_ref.shape[1] // sub

    def scores_of(t):
        q = q_ref[0, t * sub:(t + 1) * sub, :]
        return [_dot_nt(q, kv_refs[2 * i][0]) for i in range(n_sets)]

    nxt = scores_of(0)
    for t in range(n_sub):
        rows = slice(t * sub, (t + 1) * sub)
        scores = nxt
        if t + 1 < n_sub:
            nxt = scores_of(t + 1)
        m = None
        for s in scores:
            sm = jnp.max(s, axis=-1, keepdims=True)
            m = sm if m is None else jnp.maximum(m, sm)
        acc = None
        den = None
        for i, s in enumerate(scores):
            p = jnp.exp(s - m)
            l = jnp.sum(p, axis=-1, keepdims=True)
            o = _dot(p.astype(BF16), kv_refs[2 * i + 1][0])
            acc = o if acc is None else acc + o
            den = l if den is None else den + l
        o_ref[0, rows, :] = (acc / den).astype(BF16)


def _attention(q, key_sets, tq):
    nb, n, _ = q.shape
    n_sets = len(key_sets)
    in_specs = [pl.BlockSpec((1, tq, HEAD_PAD), lambda b, h, i: (b, i, h))]
    args = [q]
    for k, v in key_sets:
        nk = k.shape[1]
        spec = pl.BlockSpec((1, nk, HEAD_PAD), lambda b, h, i: (b, 0, h))
        in_specs += [spec, spec]
        args += [k, v]
    return pl.pallas_call(
        functools.partial(_attn_kernel, n_sets=n_sets),
        grid=(nb, A_HEADS, n // tq),
        in_specs=in_specs,
        out_specs=pl.BlockSpec((1, tq, HEAD_PAD), lambda b, h, i: (b, i, h)),
        out_shape=jax.ShapeDtypeStruct((nb, n, AP_WIDTH), BF16),
        compiler_params=_cparams(("arbitrary", "arbitrary", "arbitrary")),
        name="attention",
    )(*args)


def _out_mlp_kernel(x_ref, yf_ref, hm_ref, mo_ref, ya_ref, mod_ref,
                    gm_ref, g2_ref, gfin_ref, wof_ref, wom_ref, woa_ref, wup_ref, wdn_ref,
                    o_ref, *, final_norm):
    mod = mod_ref[0]
    ga1, sh2, sc2, ga2 = mod[0:1], mod[1:2], mod[2:3], mod[3:4]
    mix = _dot(yf_ref[...], wof_ref[...]) + _dot(ya_ref[...], woa_ref[...])
    gm = gm_ref[...]
    for hd in range(M_HEADS):
        lanes = slice(hd * HEAD_PAD, (hd + 1) * HEAD_PAD)
        hh = hm_ref[:, lanes]
        ms = jnp.sum(hh * hh, axis=-1, keepdims=True) * (1.0 / M_HEAD_DIM)
        ym = hh * lax.rsqrt(ms + EPS) * gm[:, lanes] * jax.nn.sigmoid(mo_ref[:, lanes])
        mix = mix + _dot(ym.astype(BF16), wom_ref[lanes, :])
    x1 = x_ref[...] + ga1 * mix
    h2 = (_rms(x1, g2_ref[...]) * (1.0 + sc2) + sh2).astype(BF16)
    hidden = wup_ref.shape[1]
    hc = 1024
    acc = None
    for c in range(hidden // hc):
        u = jnp.maximum(_dot(h2, wup_ref[:, c * hc:(c + 1) * hc]), 0.0)
        d = _dot((u * u).astype(BF16), wdn_ref[c * hc:(c + 1) * hc, :])
        acc = d if acc is None else acc + d
    x2 = x1 + ga2 * acc
    if final_norm:
        x2 = _rms(x2, gfin_ref[...])
    o_ref[...] = x2


def _out_mlp(x2d, seq, yf, hm, mo, ya, mod4, per_batch_mod, gm, g2, gfin,
             wof, wom, woa, wup, wdn, tm, final_norm):
    t, d = x2d.shape
    tiles_per_seq = seq // tm
    if per_batch_mod:
        mod_map = lambda i: (i // tiles_per_seq, 0, 0)
    else:
        mod_map = lambda i: (0, 0, 0)
    tok = lambda w: pl.BlockSpec((tm, w), lambda i: (i, 0))
    yf_spec = pl.BlockSpec((tm, F_WIDTH), lambda i: (i % tiles_per_seq, i // tiles_per_seq))
    return pl.pallas_call(
        functools.partial(_out_mlp_kernel, final_norm=final_norm),
        grid=(t // tm,),
        in_specs=[
            tok(d), yf_spec, tok(MP_WIDTH), tok(MP_WIDTH), tok(AP_WIDTH),
            pl.BlockSpec((1, 4, d), mod_map),
            _const_spec((1, MP_WIDTH)), _const_spec((1, d)), _const_spec((1, d)),
            _const_spec((F_WIDTH, d)), _const_spec((MP_WIDTH, d)), _const_spec((AP_WIDTH, d)),
            _const_spec((d, wup.shape[1])), _const_spec((wdn.shape[0], d)),
        ],
        out_specs=tok(d),
        out_shape=jax.ShapeDtypeStruct((t, d), F32),
        compiler_params=_cparams(("arbitrary",)),
        name="out_mlp",
    )(x2d, yf, hm, mo, ya, mod4, gm, g2, gfin, wof, wom, woa, wup, wdn)


def _dft_tables(n):
    idx = (np.arange(n, dtype=np.int64)[:, None] * np.arange(n, dtype=np.int64)[None, :]) % n
    ang = 2.0 * np.pi * idx.astype(np.float64) / n
    scale = 1.0 / np.sqrt(n)
    return np.cos(ang) * scale, np.sin(ang) * scale


def _channel_dft():
    c, s = _dft_tables(F_GROUP_DIM)
    eye = np.eye(F_GROUPS)
    return (jnp.asarray(np.kron(eye, c), dtype=F32).astype(BF16),
            jnp.asarray(np.kron(eye, s), dtype=F32).astype(BF16))


def _position_dft(n):
    c, s = _dft_tables(n)
    return jnp.asarray(c, dtype=F32).astype(BF16), jnp.asarray(s, dtype=F32).astype(BF16)


def _rope_tables(n, rotate):
    cos = np.zeros((n, HEAD_PAD), np.float32)
    sin = np.zeros((n, HEAD_PAD), np.float32)
    cos[:, :A_NOPE + A_ROPE] = 1.0
    if rotate:
        nf = A_ROPE // 4
        t = np.arange(n)
        row = (t // GRID_W).astype(np.float32)
        col = (t % GRID_W).astype(np.float32)
        freqs = (np.float32(ROPE_THETA) ** (-np.arange(nf, dtype=np.float32) / np.float32(nf))).astype(np.float32)
        for seg, pos in enumerate((row, col)):
            ang = pos[:, None] * freqs[None, :]
            c, s = np.cos(ang), np.sin(ang)
            base = A_NOPE + seg * 2 * nf
            cos[:, base:base + nf] = c
            cos[:, base + nf:base + 2 * nf] = c
            sin[:, base:base + nf] = -s
            sin[:, base + nf:base + 2 * nf] = s
    return jnp.asarray(cos), jnp.asarray(sin)


def _rope_swap_perm():
    nf = A_ROPE // 4
    perm = np.arange(A_ROPE)
    for seg in range(2):
        b = seg * 2 * nf
        perm[b:b + nf] = np.arange(b + nf, b + 2 * nf)
        perm[b + nf:b + 2 * nf] = np.arange(b, b + nf)
    return perm


def _pad_heads_cols(w, heads, width):
    lead = w.shape[:-1]
    w = w.reshape(lead + (heads, width))
    w = jnp.pad(w, [(0, 0)] * len(lead) + [(0, 0), (0, HEAD_PAD - width)])
    return w.reshape(lead + (heads * HEAD_PAD,))


def _pad_cols(w, width):
    return jnp.pad(w, [(0, 0)] * (w.ndim - 1) + [(0, width - w.shape[-1])])


def _layer_weights(l, w_in, conv_qk, b_gates, g_mlstm, w_uq, w_ukv, w_out):
    swap = _rope_swap_perm()
    wl = w_in[l]
    offs = np.cumsum([0, F_WIDTH, M_WIDTH, M_WIDTH, M_WIDTH, M_WIDTH, 4 * M_HEADS, Q_LORA, KV_LORA, A_ROPE])
    part = lambda i: wl[:, offs[i]:offs[i + 1]]
    kr_w = part(8)
    place_rope = lambda w: jnp.pad(w, [(0, 0), (A_NOPE, LANES - A_NOPE - A_ROPE)])
    w_in_p = jnp.concatenate([
        part(0),
        _pad_heads_cols(part(1), M_HEADS, M_HEAD_DIM),
        _pad_heads_cols(part(2), M_HEADS, M_HEAD_DIM),
        _pad_heads_cols(part(4), M_HEADS, M_HEAD_DIM),
        _pad_cols(part(5)[:, GATE_I_COLS], LANES),
        _pad_cols(part(5)[:, GATE_F_COLS], LANES),
        part(6),
        part(7),
        place_rope(kr_w),
        place_rope(kr_w[:, swap]),
    ], axis=1).astype(BF16)
    assert w_in_p.shape[1] == IN_PAD
    w_vt = _pad_heads_cols(part(3), M_HEADS, M_HEAD_DIM).T.astype(BF16)

    conv = conv_qk[l]
    conv_p = jnp.concatenate([
        _pad_heads_cols(conv[:, :M_WIDTH], M_HEADS, M_HEAD_DIM),
        _pad_heads_cols(conv[:, M_WIDTH:], M_HEADS, M_HEAD_DIM),
    ], axis=1)
    conv_p = jnp.pad(conv_p, [(0, 8 - K_CONV), (0, 0)])

    bg = b_gates[l]
    bi, bf = bg[GATE_I_COLS], bg[GATE_F_COLS]
    gate_bias = dict(bic=_pad_cols(bi[None, :], LANES), bfc=_pad_cols(bf[None, :], LANES),
                     bir=bi[:, None], bfr=bf[:, None])

    gm = _pad_heads_cols(g_mlstm[l][None, :], M_HEADS, M_HEAD_DIM)

    uq = w_uq[l].reshape(Q_LORA, A_HEADS, A_NOPE + A_ROPE)
    uq_s = jnp.concatenate([jnp.zeros_like(uq[..., :A_NOPE]), uq[..., A_NOPE:][..., swap]], axis=-1)
    pad_q = lambda w: jnp.pad(w, [(0, 0), (0, 0), (0, HEAD_PAD - A_NOPE - A_ROPE)]).reshape(Q_LORA, AP_WIDTH)
    wq = pad_q(uq).astype(BF16)
    wqs = pad_q(uq_s).astype(BF16)
    ukv = w_ukv[l].reshape(KV_LORA, A_HEADS, A_NOPE + A_V)
    wk = jnp.pad(ukv[..., :A_NOPE], [(0, 0), (0, 0), (0, HEAD_PAD - A_NOPE)]).reshape(KV_LORA, AP_WIDTH).astype(BF16)
    wv = jnp.pad(ukv[..., A_NOPE:], [(0, 0), (0, 0), (0, HEAD_PAD - A_V)]).reshape(KV_LORA, AP_WIDTH).astype(BF16)

    wo = w_out[l]
    pad_rows = lambda w, heads, width: jnp.pad(
        w.reshape(heads, width, -1), [(0, 0), (0, HEAD_PAD - width), (0, 0)]).reshape(heads * HEAD_PAD, -1)
    wof = wo[:F_WIDTH].astype(BF16)
    wom = pad_rows(wo[F_WIDTH:F_WIDTH + M_WIDTH], M_HEADS, M_HEAD_DIM).astype(BF16)
    woa = pad_rows(wo[F_WIDTH + M_WIDTH:], A_HEADS, A_V).astype(BF16)
    return dict(w_in_p=w_in_p, w_vt=w_vt, conv_p=conv_p, gate_bias=gate_bias, gm=gm,
                wq=wq, wqs=wqs, wk=wk, wv=wv, wof=wof, wom=wom, woa=woa)


GATE_I_COLS = np.concatenate([np.arange(M_HEADS), 2 * M_HEADS + np.arange(M_HEADS)])
GATE_F_COLS = GATE_I_COLS + M_HEADS


def _gates_rowmajor(g, nb, n):
    g = g[:, :N_STREAM].reshape(nb, n // MCH, MCH, N_STREAM)
    return g.transpose(0, 1, 3, 2)


def _stream_selectors():
    sel = np.zeros((N_STREAM, 3 * LANES, MCH), np.float32)
    for s in range(N_STREAM):
        for part in range(3):
            sel[s, part * LANES + s, :] = 1.0
    return jnp.asarray(sel, dtype=BF16)


def kernel(x, c, ctx, c_ctx, w_mod, b_mod, g_norm1, g_norm2, w_in, b_gates, conv_qk, g_mlstm,
           g_q_norm, g_kv_norm, w_uq, w_ukv, w_out, w_up, w_down, g_final):
    nb, seq, d = x.shape
    nctx = ctx.shape[1]
    depth = w_mod.shape[0]
    assert d == D_MODEL and seq % 256 == 0 and nctx % MCH == 0
    sel = _stream_selectors()

    tm = 256
    tm_ctx = min(256, nctx)
    tq = 4 * ATTN_SUB
    tq_ctx = min(ATTN_SUB, nctx)

    dft_cc, dft_cs = _channel_dft()
    dft_lat = _position_dft(seq)
    dft_ctx = _position_dft(nctx)
    rope_lat = _rope_tables(seq, True)
    rope_ctx = _rope_tables(nctx, False)

    rows = ((nb + 1 + 7) // 8) * 8
    cc = jnp.concatenate([c, c_ctx[None, :], jnp.zeros((rows - nb - 1, d), F32)], axis=0)
    mod_all = _modulation(cc, w_mod, b_mod)

    xl = x.reshape(nb * seq, d)
    xc = ctx.reshape(nb * nctx, d)
    row = lambda v: v.reshape(1, -1)

    for l in range(depth):
        last = l == depth - 1
        lw = _layer_weights(l, w_in, conv_qk, b_gates, g_mlstm, w_uq, w_ukv, w_out)
        wup = w_up[l].astype(BF16)
        wdn = w_down[l].astype(BF16)
        mod = mod_all[l].reshape(rows, 6, d)
        mod_lat, mod_ctx = mod[:nb], mod[nb:nb + 1]

        def tokenwise(xt, n, m, per_batch, tile):
            return _inproj(xt, n, row(g_norm1[l]), m[:, 0:1], m[:, 1:2], per_batch,
                           lw["w_in_p"], lw["w_vt"], dft_cc, dft_cs, tile)

        zc, zs, mqk, vt, mo, gi, gf, cq, ckv, kr = tokenwise(xl, seq, mod_lat, True, tm)
        zc_c, zs_c, mqk_c, vt_c, mo_c, gi_c, gf_c, cq_c, ckv_c, kr_c = tokenwise(xc, nctx, mod_ctx, False, tm_ctx)

        yf = _fourier(*dft_lat, zc, zs)

        def mlstm_inputs(mqk_s, vt_s, gi_s, gf_s, n):
            q_s, k_s = _conv_silu(mqk_s.reshape(nb, n, 2 * MP_WIDTH), lw["conv_p"])
            return (q_s, k_s, vt_s, gi_s.reshape(nb, n, LANES), gf_s.reshape(nb, n, LANES),
                    _gates_rowmajor(gi_s, nb, n), _gates_rowmajor(gf_s, nb, n))

        hm, hm_c = _mlstm(mlstm_inputs(mqk, vt, gi, gf, seq), mlstm_inputs(mqk_c, vt_c, gi_c, gf_c, nctx),
                          sel=sel, **lw["gate_bias"])

        gq, gkv = row(g_q_norm[l]), row(g_kv_norm[l])
        q_a, k_a, v_a = _mla_proj(cq, ckv, kr, *rope_lat, seq, gq, gkv,
                                  lw["wq"], lw["wqs"], lw["wk"], lw["wv"], tm, True)
        q_ac, k_ac, v_ac = _mla_proj(cq_c, ckv_c, kr_c, *rope_ctx, nctx, gq, gkv,
                                     lw["wq"], lw["wqs"], lw["wk"], lw["wv"], tm_ctx, not last)
        b3 = lambda a, n: a.reshape(nb, n, AP_WIDTH)
        keys_ctx = (b3(k_ac, nctx), b3(v_ac, nctx))
        ya = _attention(b3(q_a, seq), [(b3(k_a, seq), b3(v_a, seq)), keys_ctx], tq)

        mlp = functools.partial(
            _out_mlp, gm=lw["gm"], g2=row(g_norm2[l]), gfin=row(g_final),
            wof=lw["wof"], wom=lw["wom"], woa=lw["woa"], wup=wup, wdn=wdn)
        xl = mlp(xl, seq, yf, hm.reshape(nb * seq, MP_WIDTH), mo, ya.reshape(nb * seq, AP_WIDTH),
                 mod_lat[:, 2:6], True, tm=tm, final_norm=last)

        if not last:
            yf_c = _fourier(*dft_ctx, zc_c, zs_c)
            ya_c = _attention(b3(q_ac, nctx), [keys_ctx], tq_ctx)
            xc = mlp(xc, nctx, yf_c, hm_c.reshape(nb * nctx, MP_WIDTH), mo_c,
                     ya_c.reshape(nb * nctx, AP_WIDTH), mod_ctx[:, 2:6], False, tm=tm_ctx, final_norm=False)

    return xl.reshape(nb, seq, d)
```

```python
import functools

import numpy as np
import jax
import jax.numpy as jnp
from jax import lax
from jax.experimental import pallas as pl
from jax.experimental.pallas import tpu as pltpu

D_MODEL = 1024
GRID_W = 64
EPS = 1e-6
F_GROUPS = 4
F_GROUP_DIM = D_MODEL // 16
F_WIDTH = F_GROUPS * F_GROUP_DIM
M_HEADS = 4
M_HEAD_DIM = 3 * D_MODEL // 32
M_WIDTH = M_HEADS * M_HEAD_DIM
M_CHUNK = 64
K_CONV = 5
A_HEADS = 4
A_NOPE = 64
A_ROPE = 32
A_V = 3 * D_MODEL // 32
Q_LORA = D_MODEL // 4
KV_LORA = D_MODEL // 8
ROPE_THETA = 10000.0
MLP_HIDDEN = 4 * D_MODEL

LANES = 128
HEAD_PAD = 128
MP_WIDTH = M_HEADS * HEAD_PAD
AP_WIDTH = A_HEADS * HEAD_PAD
VMEM_LIMIT = 56 * 1024 * 1024
MCH = 128

OFF_PF = 0
OFF_MQ = OFF_PF + F_WIDTH
OFF_MK = OFF_MQ + MP_WIDTH
OFF_MO = OFF_MK + MP_WIDTH
OFF_GI = OFF_MO + MP_WIDTH
OFF_GF = OFF_GI + LANES
OFF_CQ = OFF_GF + LANES
OFF_CKV = OFF_CQ + Q_LORA
OFF_KR = OFF_CKV + KV_LORA
OFF_KRS = OFF_KR + LANES
IN_PAD = OFF_KRS + LANES

BF16 = jnp.bfloat16
F32 = jnp.float32


def _cparams(sem):
    return pltpu.CompilerParams(dimension_semantics=sem, vmem_limit_bytes=VMEM_LIMIT)


def _const_spec(shape):
    nd = len(shape)
    return pl.BlockSpec(shape, lambda *_: (0,) * nd, pipeline_mode=pl.Buffered(1))


def _split3(a):
    hi = a.astype(BF16)
    r1 = a - hi.astype(F32)
    mid = r1.astype(BF16)
    lo = (r1 - mid.astype(F32)).astype(BF16)
    return hi, mid, lo


def _dot(a, b):
    return jnp.dot(a, b, preferred_element_type=F32)


def _dot_nt(a, b):
    return lax.dot_general(a, b, (((1,), (1,)), ((), ())), preferred_element_type=F32)


def _dot_tn(a, b):
    return lax.dot_general(a, b, (((0,), (0,)), ((), ())), preferred_element_type=F32)


def _rms(x, g):
    return x * lax.rsqrt(jnp.mean(x * x, axis=-1, keepdims=True) + EPS) * g


def _mod_kernel(c_ref, w_ref, b_ref, o_ref):
    c = c_ref[...]
    a = c * jax.nn.sigmoid(c)
    a_hi = a.astype(BF16)
    a_lo = (a - a_hi.astype(F32)).astype(BF16)
    w = w_ref[0]
    w_hi = w.astype(BF16)
    w_lo = (w - w_hi.astype(F32)).astype(BF16)
    acc = _dot(a_hi, w_hi) + _dot(a_hi, w_lo) + _dot(a_lo, w_hi)
    o_ref[0] = acc + b_ref[0]


def _modulation(cc, w_mod, b_mod):
    depth, d, n = w_mod.shape
    rows = cc.shape[0]
    tn = 1536
    return pl.pallas_call(
        _mod_kernel,
        grid=(depth, n // tn),
        in_specs=[
            pl.BlockSpec((rows, d), lambda l, j: (0, 0)),
            pl.BlockSpec((1, d, tn), lambda l, j: (l, 0, j)),
            pl.BlockSpec((1, 1, tn), lambda l, j: (l, 0, j)),
        ],
        out_specs=pl.BlockSpec((1, rows, tn), lambda l, j: (l, 0, j)),
        out_shape=jax.ShapeDtypeStruct((depth, rows, n), F32),
        compiler_params=_cparams(("arbitrary", "arbitrary")),
        name="modulation",
    )(cc, w_mod, b_mod.reshape(depth, 1, n))


def _inproj_kernel(x_ref, g_ref, sh_ref, sc_ref, w_ref, wvt_ref, cc_ref, cs_ref,
                   zc_ref, zs_ref, mqk_ref, vt_ref, mo_ref, gi_ref, gf_ref, cq_ref, ckv_ref, kr_ref):
    x = x_ref[...]
    h = _rms(x, g_ref[...]) * (1.0 + sc_ref[0]) + sh_ref[0]
    hb = h.astype(BF16)

    def proj(off, width):
        return _dot(hb, w_ref[:, off:off + width])

    pf = proj(OFF_PF, F_WIDTH).astype(BF16)
    zc_ref[...] = _dot(pf, cc_ref[...]).astype(BF16)
    zs_ref[...] = _dot(pf, cs_ref[...]).astype(BF16)
    mqk_ref[...] = proj(OFF_MQ, 2 * MP_WIDTH)
    vt = _dot_nt(wvt_ref[...], hb).astype(BF16)
    for j in range(vt_ref.shape[0]):
        vt_ref[j] = vt[:, j * MCH:(j + 1) * MCH]
    mo_ref[...] = proj(OFF_MO, MP_WIDTH)
    gi_ref[...] = proj(OFF_GI, LANES)
    gf_ref[...] = proj(OFF_GF, LANES)
    cq_ref[...] = proj(OFF_CQ, Q_LORA)
    ckv_ref[...] = proj(OFF_CKV, KV_LORA)
    kr_ref[...] = proj(OFF_KR, 2 * LANES)


def _inproj(x2d, seq, g, sh, sc, per_batch_mod, w_in_p, w_vt, dft_cc, dft_cs, tm):
    t, d = x2d.shape
    nb = t // seq
    tiles_per_seq = seq // tm
    if per_batch_mod:
        mod_map = lambda i: (i // tiles_per_seq, 0, 0)
    else:
        mod_map = lambda i: (0, 0, 0)
    tok = lambda w: pl.BlockSpec((tm, w), lambda i: (i, 0))
    z_spec = pl.BlockSpec((tm, F_WIDTH), lambda i: (i % tiles_per_seq, i // tiles_per_seq))
    shapes = [
        jax.ShapeDtypeStruct((seq, nb * F_WIDTH), BF16),
        jax.ShapeDtypeStruct((seq, nb * F_WIDTH), BF16),
        jax.ShapeDtypeStruct((t, 2 * MP_WIDTH), F32),
        jax.ShapeDtypeStruct((t // MCH, MP_WIDTH, MCH), BF16),
        jax.ShapeDtypeStruct((t, MP_WIDTH), F32),
        jax.ShapeDtypeStruct((t, LANES), F32),
        jax.ShapeDtypeStruct((t, LANES), F32),
        jax.ShapeDtypeStruct((t, Q_LORA), F32),
        jax.ShapeDtypeStruct((t, KV_LORA), F32),
        jax.ShapeDtypeStruct((t, 2 * LANES), F32),
    ]
    vt_spec = pl.BlockSpec((tm // MCH, MP_WIDTH, MCH), lambda i: (i, 0, 0))
    out_specs = [z_spec, z_spec, tok(2 * MP_WIDTH), vt_spec, tok(MP_WIDTH), tok(LANES), tok(LANES),
                 tok(Q_LORA), tok(KV_LORA), tok(2 * LANES)]
    return pl.pallas_call(
        _inproj_kernel,
        grid=(t // tm,),
        in_specs=[
            tok(d),
            _const_spec((1, d)),
            pl.BlockSpec((1, 1, d), mod_map),
            pl.BlockSpec((1, 1, d), mod_map),
            _const_spec((d, IN_PAD)),
            _const_spec((MP_WIDTH, d)),
            _const_spec((F_WIDTH, F_WIDTH)),
            _const_spec((F_WIDTH, F_WIDTH)),
        ],
        out_specs=out_specs,
        out_shape=shapes,
        compiler_params=_cparams(("arbitrary",)),
        name="inproj",
    )(x2d, g, sh, sc, w_in_p, w_vt, dft_cc, dft_cs)


def _fourier_kernel(c_ref, s_ref, zc_ref, zs_ref, o_ref):
    y = _dot(c_ref[...], zc_ref[...]) - _dot(s_ref[...], zs_ref[...])
    o_ref[...] = y.astype(BF16)


def _fourier(dft_c, dft_s, zc, zs):
    n, cols = zc.shape
    tr = min(n, 512)
    tc = min(cols, 512)
    return pl.pallas_call(
        _fourier_kernel,
        grid=(n // tr, cols // tc),
        in_specs=[
            pl.BlockSpec((tr, n), lambda i, j: (i, 0)),
            pl.BlockSpec((tr, n), lambda i, j: (i, 0)),
            pl.BlockSpec((n, tc), lambda i, j: (0, j)),
            pl.BlockSpec((n, tc), lambda i, j: (0, j)),
        ],
        out_specs=pl.BlockSpec((tr, tc), lambda i, j: (i, j)),
        out_shape=jax.ShapeDtypeStruct((n, cols), BF16),
        compiler_params=_cparams(("arbitrary", "arbitrary")),
        name="fourier",
    )(dft_c, dft_s, zc, zs)


CONV_ROWS = 256
CONV_HALO = 8


def _conv_kernel(u_ref, w_ref, q_ref, k_ref, pad_ref):
    n = u_ref.shape[1]
    width = u_ref.shape[2]
    zeros = jnp.zeros((CONV_HALO, width), F32)
    pad_ref[0:CONV_HALO, :] = zeros
    pad_ref[CONV_HALO + n:2 * CONV_HALO + n, :] = zeros
    pad_ref[CONV_HALO:CONV_HALO + n, :] = u_ref[0]
    w = w_ref[...]
    rows = min(CONV_ROWS, n)
    for r in range(n // rows):
        base = CONV_HALO + r * rows - K_CONV // 2
        acc = pad_ref[base:base + rows, :] * w[0:1, :]
        for j in range(1, K_CONV):
            acc = acc + pad_ref[base + j:base + j + rows, :] * w[j:j + 1, :]
        act = acc * jax.nn.sigmoid(acc)
        q_ref[0, r * rows:(r + 1) * rows, :] = (act[:, :MP_WIDTH] * (M_HEAD_DIM ** -0.5)).astype(BF16)
        k_ref[0, r * rows:(r + 1) * rows, :] = act[:, MP_WIDTH:].astype(BF16)


def _conv_silu(mqk, conv_w_p):
    nb, n, width = mqk.shape
    return pl.pallas_call(
        _conv_kernel,
        grid=(nb,),
        in_specs=[
            pl.BlockSpec((1, n, width), lambda b: (b, 0, 0)),
            _const_spec((8, width)),
        ],
        out_specs=[
            pl.BlockSpec((1, n, MP_WIDTH), lambda b: (b, 0, 0)),
            pl.BlockSpec((1, n, MP_WIDTH), lambda b: (b, 0, 0)),
        ],
        out_shape=[
            jax.ShapeDtypeStruct((nb, n, MP_WIDTH), BF16),
            jax.ShapeDtypeStruct((nb, n, MP_WIDTH), BF16),
        ],
        scratch_shapes=[pltpu.VMEM((n + 2 * CONV_HALO, width), F32)],
        compiler_params=_cparams(("arbitrary",)),
        name="conv_silu",
    )(mqk, conv_w_p)


def _log_sigmoid(x):
    return jnp.minimum(x, 0.0) - jnp.log(1.0 + jnp.exp(-jnp.abs(x)))


def _exact_dot_01(a, tri_bf16, a_on_left):
    out = None
    for term in _split3(a):
        d = _dot(term, tri_bf16) if a_on_left else _dot(tri_bf16, term)
        out = d if out is None else out + d
    return out


N_STREAM = 2 * M_HEADS
ONE_ROW = M_HEAD_DIM
(F_R, F_B, F_CM, F_TOT, F_CML, F_MP, F_A, F_WI, F_ELD, F_DEC, F_WK, F_HI, F_MID, F_LO) = range(14)
N_FIELDS = 14
OUT_GROUP = 4
SEL_ROWS = 32


def _mlstm_kernel(ql_ref, kl_ref, vtl_ref, gril_ref, grfl_ref,
                  qc_ref, kc_ref, vtc_ref, gric_ref, grfc_ref,
                  bir_ref, bfr_ref, sel_ref,
                  hl_ref, hc_ref,
                  ct_st, st_sc, rows_sc):
    L = MCH
    nh = M_HEADS
    ncc = qc_ref.shape[1] // L
    ncl = ql_ref.shape[1] // L

    d0 = lax.broadcasted_iota(jnp.int32, (L, L), 0)
    d1 = lax.broadcasted_iota(jnp.int32, (L, L), 1)
    le = d0 <= d1
    ge = d0 >= d1
    tri_le = le.astype(BF16)
    fwd_rows = lax.broadcasted_iota(jnp.int32, (N_STREAM, L), 0) < nh
    feat = lax.broadcasted_iota(jnp.int32, (HEAD_PAD, L), 0)
    one_row = feat == ONE_ROW
    keep_rows = feat < M_HEAD_DIM

    ct_st[...] = jnp.zeros_like(ct_st)

    def field(f, slot0, nc):
        return rows_sc[f, slot0:slot0 + nc].reshape(nc * N_STREAM, L)

    def set_field(f, slot0, nc, val):
        rows_sc[f, slot0:slot0 + nc] = val.reshape(nc, N_STREAM, L)

    def gate_pass(gri_ref, grf_ref, nc, slot0):
        n8 = nc * N_STREAM
        fwd = (lax.broadcasted_iota(jnp.int32, (n8, L), 0) & (N_STREAM - 1)) < nh
        lane = lax.broadcasted_iota(jnp.int32, (n8, L), 1)
        gi = gri_ref[0].reshape(n8, L) + bir_ref[0:n8, :]
        f = _log_sigmoid(grf_ref[0].reshape(n8, L) + bfr_ref[0:n8, :])
        pre = _exact_dot_01(f, tri_le, a_on_left=True)
        total = jnp.sum(f, axis=1, keepdims=True)
        b = jnp.where(fwd, pre, total - pre + f)
        r = gi - b
        cm = r
        sh = 1
        while sh < L:
            from_left = jnp.where(lane >= sh, pltpu.roll(cm, sh, 1), -jnp.inf)
            from_right = jnp.where(lane < L - sh, pltpu.roll(cm, L - sh, 1), -jnp.inf)
            cm = jnp.maximum(cm, jnp.where(fwd, from_left, from_right))
            sh *= 2
        set_field(F_R, slot0, nc, r)
        set_field(F_B, slot0, nc, b)
        set_field(F_CM, slot0, nc, cm)
        set_field(F_TOT, slot0, nc, jnp.broadcast_to(total, (n8, L)))
        set_field(F_CML, slot0, nc, jnp.broadcast_to(jnp.max(r, axis=1, keepdims=True), (n8, L)))

    def m_scan(nc, slot0, m0):
        def step(j, m):
            sf = slot0 + j
            sb = slot0 + nc - 1 - j
            tot = jnp.where(fwd_rows, rows_sc[F_TOT, sf], rows_sc[F_TOT, sb])
            cml = jnp.where(fwd_rows, rows_sc[F_CML, sf], rows_sc[F_CML, sb])
            rows_sc[F_MP, sf, 0:nh, :] = m[0:nh]
            rows_sc[F_MP, sb, nh:N_STREAM, :] = m[nh:N_STREAM]
            return tot + jnp.maximum(m, cml)

        return lax.fori_loop(0, nc, step, m0)

    def weight_pass(nc, slot0):
        mp, cm, b, r = (field(f, slot0, nc) for f in (F_MP, F_CM, F_B, F_R))
        big = jnp.maximum(mp, field(F_CML, slot0, nc))
        a = -jnp.maximum(mp, cm)
        set_field(F_A, slot0, nc, a)
        set_field(F_WI, slot0, nc, jnp.exp(mp + a))
        set_field(F_ELD, slot0, nc, jnp.exp(a - b))
        set_field(F_DEC, slot0, nc, jnp.exp(mp - big))
        set_field(F_WK, slot0, nc, jnp.exp(r - big))
        for f, term in zip((F_HI, F_MID, F_LO), _split3(r)):
            set_field(f, slot0, nc, term.astype(F32))

    def value_slab(vt_ref, c, hd):
        vt = vt_ref[c, hd * HEAD_PAD:(hd + 1) * HEAD_PAD, :].astype(F32)
        return jnp.where(one_row, 1.0, vt)

    def state_pass(k_ref, vt_ref, nc, slot0):
        def step(j, carry):
            uts = []
            for sidx in range(N_STREAM):
                hd = sidx % nh
                c = j if sidx < nh else nc - 1 - j
                wk = rows_sc[F_WK, slot0 + c, sidx:sidx + 1, :]
                kk = k_ref[0, pl.ds(pl.multiple_of(c * L, L), L), hd * HEAD_PAD:(hd + 1) * HEAD_PAD]
                uts.append(_dot((value_slab(vt_ref, c, hd) * wk).astype(BF16), kk))
            for sidx in range(N_STREAM):
                slot = slot0 + (j if sidx < nh else nc - 1 - j)
                prev = ct_st[sidx]
                st_sc[sidx, slot] = prev.astype(BF16)
                ct_st[sidx] = rows_sc[F_DEC, slot, sidx:sidx + 1, :] * prev + uts[sidx]
            return carry

        lax.fori_loop(0, nc, step, 0)

    def output_pass(q_ref, k_ref, vt_ref, out_ref, nc, slot0):
        head_lanes = [slice(hd * HEAD_PAD, (hd + 1) * HEAD_PAD) for hd in range(nh)]

        def independent_matmuls(c):
            rows = pl.ds(pl.multiple_of(c * L, L), L)
            slot = slot0 + c
            r3 = jnp.concatenate([rows_sc[F_HI, slot], rows_sc[F_MID, slot], rows_sc[F_LO, slot],
                                  jnp.zeros((N_STREAM, L), F32)], axis=0).astype(BF16)
            qs = [q_ref[0, rows, lanes] for lanes in head_lanes]
            s_ts = [_dot_nt(k_ref[0, rows, lanes], q) for lanes, q in zip(head_lanes, qs)]
            inters = [_dot_nt(st_sc[sidx, slot], qs[sidx % nh]) for sidx in range(N_STREAM)]
            r_ts = [_dot_tn(r3, sel_ref[sidx]) for sidx in range(N_STREAM)]
            return s_ts, inters, r_ts

        def finish(c, s_ts, inters, r_ts):
            rows = pl.ds(pl.multiple_of(c * L, L), L)
            slot = slot0 + c
            a_rows = rows_sc[F_A, slot]
            wi_rows = rows_sc[F_WI, slot]
            eld_rows = rows_sc[F_ELD, slot]
            p_ts = []
            for sidx in range(N_STREAM):
                one = slice(sidx, sidx + 1)
                dm = jnp.where(le if sidx < nh else ge, r_ts[sidx] + a_rows[one, :], -jnp.inf)
                p_ts.append((s_ts[sidx % nh] * jnp.exp(dm)).astype(BF16))
            vtas = [value_slab(vt_ref, c, hd).astype(BF16) for hd in range(nh)]
            z_ts = [_dot(vtas[sidx % nh], p_ts[sidx]) for sidx in range(N_STREAM)]
            for hd in range(nh):
                hsum = None
                for sidx in (hd, nh + hd):
                    one = slice(sidx, sidx + 1)
                    z_t = z_ts[sidx] + inters[sidx] * wi_rows[one, :]
                    den = z_t[ONE_ROW:ONE_ROW + 1, :]
                    h_t = z_t * (1.0 / jnp.maximum(jnp.abs(den), eld_rows[one, :]))
                    hsum = h_t if hsum is None else hsum + h_t
                out_ref[0, rows, head_lanes[hd]] = jnp.where(keep_rows, hsum, 0.0).T

        group = min(OUT_GROUP, nc)

        def step(g, carry):
            ahead = independent_matmuls(g * group)
            for u in range(group):
                cur = ahead
                if u + 1 < group:
                    ahead = independent_matmuls(g * group + u + 1)
                finish(g * group + u, *cur)
            return carry

        lax.fori_loop(0, nc // group, step, 0)

    gate_pass(gric_ref, grfc_ref, ncc, 0)
    gate_pass(gril_ref, grfl_ref, ncl, ncc)
    m1 = m_scan(ncc, 0, jnp.zeros((N_STREAM, L), F32))
    m_scan(ncl, ncc, m1)
    weight_pass(ncc, 0)
    weight_pass(ncl, ncc)
    state_pass(kc_ref, vtc_ref, ncc, 0)
    state_pass(kl_ref, vtl_ref, ncl, ncc)
    output_pass(qc_ref, kc_ref, vtc_ref, hc_ref, ncc, 0)
    output_pass(ql_ref, kl_ref, vtl_ref, hl_ref, ncl, ncc)


def _mlstm(lat, ctx, bir, bfr, sel):
    nb, n, _ = lat[0].shape
    nctx = ctx[0].shape[1]
    assert MCH == LANES == HEAD_PAD
    nct = (n + nctx) // MCH
    n8 = max(n, nctx) // MCH * N_STREAM

    def specs(rows):
        nc = rows // MCH
        seq = lambda w: pl.BlockSpec((1, rows, w), lambda b: (b, 0, 0))
        gr = pl.BlockSpec((1, nc, N_STREAM, MCH), lambda b: (b, 0, 0, 0))
        return [seq(MP_WIDTH), seq(MP_WIDTH), pl.BlockSpec((nc, MP_WIDTH, MCH), lambda b: (b, 0, 0)), gr, gr]

    out_spec = lambda rows: pl.BlockSpec((1, rows, MP_WIDTH), lambda b: (b, 0, 0))
    return pl.pallas_call(
        _mlstm_kernel,
        grid=(nb,),
        in_specs=specs(n) + specs(nctx) + [
            _const_spec((n8, 1)), _const_spec((n8, 1)),
            _const_spec((N_STREAM, SEL_ROWS, MCH)),
        ],
        out_specs=[out_spec(n), out_spec(nctx)],
        out_shape=[
            jax.ShapeDtypeStruct((nb, n, MP_WIDTH), F32),
            jax.ShapeDtypeStruct((nb, nctx, MP_WIDTH), F32),
        ],
        scratch_shapes=[
            pltpu.VMEM((N_STREAM, HEAD_PAD, HEAD_PAD), F32),
            pltpu.VMEM((N_STREAM, nct, HEAD_PAD, HEAD_PAD), BF16),
            pltpu.VMEM((N_FIELDS, nct, N_STREAM, MCH), F32),
        ],
        compiler_params=_cparams(("arbitrary",)),
        name="mlstm",
    )(*lat, *ctx, jnp.tile(bir, (n8 // N_STREAM, 1)), jnp.tile(bfr, (n8 // N_STREAM, 1)), sel)


def _mla_proj_kernel(cq_ref, ckv_ref, kr_ref, cos_ref, sin_ref, gq_ref, gkv_ref,
                     wq_ref, wqs_ref, wk_ref, wv_ref, q_ref, k_ref, v_ref, *, with_q):
    cos = cos_ref[...]
    sin = sin_ref[...]
    kvn = _rms(ckv_ref[...], gkv_ref[...]).astype(BF16)
    kr = kr_ref[...]
    k_rope = kr[:, :LANES] * cos + kr[:, LANES:] * sin
    for hd in range(A_HEADS):
        lanes = slice(hd * HEAD_PAD, (hd + 1) * HEAD_PAD)
        k_ref[:, lanes] = (_dot(kvn, wk_ref[:, lanes]) + k_rope).astype(BF16)
        v_ref[:, lanes] = _dot(kvn, wv_ref[:, lanes]).astype(BF16)
    if with_q:
        qn = _rms(cq_ref[...], gq_ref[...]).astype(BF16)
        scale = (A_NOPE + A_ROPE) ** -0.5
        for hd in range(A_HEADS):
            lanes = slice(hd * HEAD_PAD, (hd + 1) * HEAD_PAD)
            q = _dot(qn, wq_ref[:, lanes]) * cos + _dot(qn, wqs_ref[:, lanes]) * sin
            q_ref[:, lanes] = (q * scale).astype(BF16)
    else:
        q_ref[...] = jnp.zeros_like(q_ref)


def _mla_proj(cq, ckv, kr, cos, sin, seq, gq, gkv, wq, wqs, wk, wv, tm, with_q):
    t = cq.shape[0]
    tiles_per_seq = seq // tm
    tok = lambda w: pl.BlockSpec((tm, w), lambda i: (i, 0))
    pos = pl.BlockSpec((tm, LANES), lambda i: (i % tiles_per_seq, 0))
    out = jax.ShapeDtypeStruct((t, AP_WIDTH), BF16)
    return pl.pallas_call(
        functools.partial(_mla_proj_kernel, with_q=with_q),
        grid=(t // tm,),
        in_specs=[
            tok(Q_LORA), tok(KV_LORA), tok(2 * LANES), pos, pos,
            _const_spec((1, Q_LORA)), _const_spec((1, KV_LORA)),
            _const_spec((Q_LORA, AP_WIDTH)), _const_spec((Q_LORA, AP_WIDTH)),
            _const_spec((KV_LORA, AP_WIDTH)), _const_spec((KV_LORA, AP_WIDTH)),
        ],
        out_specs=[tok(AP_WIDTH), tok(AP_WIDTH), tok(AP_WIDTH)],
        out_shape=[out, out, out],
        compiler_params=_cparams(("arbitrary",)),
        name="mla_proj",
    )(cq, ckv, kr, cos, sin, gq, gkv, wq, wqs, wk, wv)


ATTN_SUB = 256


def _attn_kernel(*refs, n_sets):
    q_ref = refs[0]
    kv_refs = refs[1:1 + 2 * n_sets]
    o_ref = refs[1 + 2 * n_sets]
    sub = min(ATTN_SUB, q_ref.shape[1])
    n_sub = q_ref.shape[1] // sub

    def scores_of(t):
        q = q_ref[0, t * sub:(t + 1) * sub, :]
        return [_dot_nt(q, kv_refs[2 * i][0]) for i in range(n_sets)]

    nxt = scores_of(0)
    for t in range(n_sub):
        rows = slice(t * sub, (t + 1) * sub)
        scores = nxt
        if t + 1 < n_sub:
            nxt = scores_of(t + 1)
        m = None
        for s in scores:
            sm = jnp.max(s, axis=-1, keepdims=True)
            m = sm if m is None else jnp.maximum(m, sm)
        acc = None
        den = None
        for i, s in enumerate(scores):
            p = jnp.exp(s - m)
            l = jnp.sum(p, axis=-1, keepdims=True)
            o = _dot(p.astype(BF16), kv_refs[2 * i + 1][0])
            acc = o if acc is None else acc + o
            den = l if den is None else den + l
        o_ref[0, rows, :] = (acc / den).astype(BF16)


def _attention(q, key_sets, tq):
    nb, n, _ = q.shape
    n_sets = len(key_sets)
    in_specs = [pl.BlockSpec((1, tq, HEAD_PAD), lambda b, h, i: (b, i, h))]
    args = [q]
    for k, v in key_sets:
        nk = k.shape[1]
        spec = pl.BlockSpec((1, nk, HEAD_PAD), lambda b, h, i: (b, 0, h))
        in_specs += [spec, spec]
        args += [k, v]
    return pl.pallas_call(
        functools.partial(_attn_kernel, n_sets=n_sets),
        grid=(nb, A_HEADS, n // tq),
        in_specs=in_specs,
        out_specs=pl.BlockSpec((1, tq, HEAD_PAD), lambda b, h, i: (b, i, h)),
        out_shape=jax.ShapeDtypeStruct((nb, n, AP_WIDTH), BF16),
        compiler_params=_cparams(("arbitrary", "arbitrary", "arbitrary")),
        name="attention",
    )(*args)


def _out_mlp_kernel(x_ref, yf_ref, hm_ref, mo_ref, ya_ref, mod_ref,
                    gm_ref, g2_ref, gfin_ref, wof_ref, wom_ref, woa_ref, wup_ref, wdn_ref,
                    o_ref, *, final_norm):
    mod = mod_ref[0]
    ga1, sh2, sc2, ga2 = mod[0:1], mod[1:2], mod[2:3], mod[3:4]
    mix = _dot(yf_ref[...], wof_ref[...]) + _dot(ya_ref[...], woa_ref[...])
    gm = gm_ref[...]
    for hd in range(M_HEADS):
        lanes = slice(hd * HEAD_PAD, (hd + 1) * HEAD_PAD)
        hh = hm_ref[:, lanes]
        ms = jnp.sum(hh * hh, axis=-1, keepdims=True) * (1.0 / M_HEAD_DIM)
        ym = hh * lax.rsqrt(ms + EPS) * gm[:, lanes] * jax.nn.sigmoid(mo_ref[:, lanes])
        mix = mix + _dot(ym.astype(BF16), wom_ref[lanes, :])
    x1 = x_ref[...] + ga1 * mix
    h2 = (_rms(x1, g2_ref[...]) * (1.0 + sc2) + sh2).astype(BF16)
    hidden = wup_ref.shape[1]
    hc = 1024
    acc = None
    for c in range(hidden // hc):
        u = jnp.maximum(_dot(h2, wup_ref[:, c * hc:(c + 1) * hc]), 0.0)
        d = _dot((u * u).astype(BF16), wdn_ref[c * hc:(c + 1) * hc, :])
        acc = d if acc is None else acc + d
    x2 = x1 + ga2 * acc
    if final_norm:
        x2 = _rms(x2, gfin_ref[...])
    o_ref[...] = x2


def _out_mlp(x2d, seq, yf, hm, mo, ya, mod4, per_batch_mod, gm, g2, gfin,
             wof, wom, woa, wup, wdn, tm, final_norm):
    t, d = x2d.shape
    tiles_per_seq = seq // tm
    if per_batch_mod:
        mod_map = lambda i: (i // tiles_per_seq, 0, 0)
    else:
        mod_map = lambda i: (0, 0, 0)
    tok = lambda w: pl.BlockSpec((tm, w), lambda i: (i, 0))
    yf_spec = pl.BlockSpec((tm, F_WIDTH), lambda i: (i % tiles_per_seq, i // tiles_per_seq))
    return pl.pallas_call(
        functools.partial(_out_mlp_kernel, final_norm=final_norm),
        grid=(t // tm,),
        in_specs=[
            tok(d), yf_spec, tok(MP_WIDTH), tok(MP_WIDTH), tok(AP_WIDTH),
            pl.BlockSpec((1, 4, d), mod_map),
            _const_spec((1, MP_WIDTH)), _const_spec((1, d)), _const_spec((1, d)),
            _const_spec((F_WIDTH, d)), _const_spec((MP_WIDTH, d)), _const_spec((AP_WIDTH, d)),
            _const_spec((d, wup.shape[1])), _const_spec((wdn.shape[0], d)),
        ],
        out_specs=tok(d),
        out_shape=jax.ShapeDtypeStruct((t, d), F32),
        compiler_params=_cparams(("arbitrary",)),
        name="out_mlp",
    )(x2d, yf, hm, mo, ya, mod4, gm, g2, gfin, wof, wom, woa, wup, wdn)


def _dft_tables(n):
    idx = (np.arange(n, dtype=np.int64)[:, None] * np.arange(n, dtype=np.int64)[None, :]) % n
    ang = 2.0 * np.pi * idx.astype(np.float64) / n
    scale = 1.0 / np.sqrt(n)
    return np.cos(ang) * scale, np.sin(ang) * scale


def _channel_dft():
    c, s = _dft_tables(F_GROUP_DIM)
    eye = np.eye(F_GROUPS)
    return (jnp.asarray(np.kron(eye, c), dtype=F32).astype(BF16),
            jnp.asarray(np.kron(eye, s), dtype=F32).astype(BF16))


def _position_dft(n):
    c, s = _dft_tables(n)
    return jnp.asarray(c, dtype=F32).astype(BF16), jnp.asarray(s, dtype=F32).astype(BF16)


def _rope_tables(n, rotate):
    cos = np.zeros((n, HEAD_PAD), np.float32)
    sin = np.zeros((n, HEAD_PAD), np.float32)
    cos[:, :A_NOPE + A_ROPE] = 1.0
    if rotate:
        nf = A_ROPE // 4
        t = np.arange(n)
        row = (t // GRID_W).astype(np.float32)
        col = (t % GRID_W).astype(np.float32)
        freqs = (np.float32(ROPE_THETA) ** (-np.arange(nf, dtype=np.float32) / np.float32(nf))).astype(np.float32)
        for seg, pos in enumerate((row, col)):
            ang = pos[:, None] * freqs[None, :]
            c, s = np.cos(ang), np.sin(ang)
            base = A_NOPE + seg * 2 * nf
            cos[:, base:base + nf] = c
            cos[:, base + nf:base + 2 * nf] = c
            sin[:, base:base + nf] = -s
            sin[:, base + nf:base + 2 * nf] = s
    return jnp.asarray(cos), jnp.asarray(sin)


def _rope_swap_perm():
    nf = A_ROPE // 4
    perm = np.arange(A_ROPE)
    for seg in range(2):
        b = seg * 2 * nf
        perm[b:b + nf] = np.arange(b + nf, b + 2 * nf)
        perm[b + nf:b + 2 * nf] = np.arange(b, b + nf)
    return perm


def _pad_heads_cols(w, heads, width):
    lead = w.shape[:-1]
    w = w.reshape(lead + (heads, width))
    w = jnp.pad(w, [(0, 0)] * len(lead) + [(0, 0), (0, HEAD_PAD - width)])
    return w.reshape(lead + (heads * HEAD_PAD,))


def _pad_cols(w, width):
    return jnp.pad(w, [(0, 0)] * (w.ndim - 1) + [(0, width - w.shape[-1])])


def _layer_weights(l, w_in, conv_qk, b_gates, g_mlstm, w_uq, w_ukv, w_out):
    swap = _rope_swap_perm()
    wl = w_in[l]
    offs = np.cumsum([0, F_WIDTH, M_WIDTH, M_WIDTH, M_WIDTH, M_WIDTH, 4 * M_HEADS, Q_LORA, KV_LORA, A_ROPE])
    part = lambda i: wl[:, offs[i]:offs[i + 1]]
    kr_w = part(8)
    place_rope = lambda w: jnp.pad(w, [(0, 0), (A_NOPE, LANES - A_NOPE - A_ROPE)])
    w_in_p = jnp.concatenate([
        part(0),
        _pad_heads_cols(part(1), M_HEADS, M_HEAD_DIM),
        _pad_heads_cols(part(2), M_HEADS, M_HEAD_DIM),
        _pad_heads_cols(part(4), M_HEADS, M_HEAD_DIM),
        _pad_cols(part(5)[:, GATE_I_COLS], LANES),
        _pad_cols(part(5)[:, GATE_F_COLS], LANES),
        part(6),
        part(7),
        place_rope(kr_w),
        place_rope(kr_w[:, swap]),
    ], axis=1).astype(BF16)
    assert w_in_p.shape[1] == IN_PAD
    w_vt = _pad_heads_cols(part(3), M_HEADS, M_HEAD_DIM).T.astype(BF16)

    conv = conv_qk[l]
    conv_p = jnp.concatenate([
        _pad_heads_cols(conv[:, :M_WIDTH], M_HEADS, M_HEAD_DIM),
        _pad_heads_cols(conv[:, M_WIDTH:], M_HEADS, M_HEAD_DIM),
    ], axis=1)
    conv_p = jnp.pad(conv_p, [(0, 8 - K_CONV), (0, 0)])

    bg = b_gates[l]
    bi, bf = bg[GATE_I_COLS], bg[GATE_F_COLS]
    gate_bias = dict(bir=bi[:, None], bfr=bf[:, None])

    gm = _pad_heads_cols(g_mlstm[l][None, :], M_HEADS, M_HEAD_DIM)

    uq = w_uq[l].reshape(Q_LORA, A_HEADS, A_NOPE + A_ROPE)
    uq_s = jnp.concatenate([jnp.zeros_like(uq[..., :A_NOPE]), uq[..., A_NOPE:][..., swap]], axis=-1)
    pad_q = lambda w: jnp.pad(w, [(0, 0), (0, 0), (0, HEAD_PAD - A_NOPE - A_ROPE)]).reshape(Q_LORA, AP_WIDTH)
    wq = pad_q(uq).astype(BF16)
    wqs = pad_q(uq_s).astype(BF16)
    ukv = w_ukv[l].reshape(KV_LORA, A_HEADS, A_NOPE + A_V)
    wk = jnp.pad(ukv[..., :A_NOPE], [(0, 0), (0, 0), (0, HEAD_PAD - A_NOPE)]).reshape(KV_LORA, AP_WIDTH).astype(BF16)
    wv = jnp.pad(ukv[..., A_NOPE:], [(0, 0), (0, 0), (0, HEAD_PAD - A_V)]).reshape(KV_LORA, AP_WIDTH).astype(BF16)

    wo = w_out[l]
    pad_rows = lambda w, heads, width: jnp.pad(
        w.reshape(heads, width, -1), [(0, 0), (0, HEAD_PAD - width), (0, 0)]).reshape(heads * HEAD_PAD, -1)
    wof = wo[:F_WIDTH].astype(BF16)
    wom = pad_rows(wo[F_WIDTH:F_WIDTH + M_WIDTH], M_HEADS, M_HEAD_DIM).astype(BF16)
    woa = pad_rows(wo[F_WIDTH + M_WIDTH:], A_HEADS, A_V).astype(BF16)
    return dict(w_in_p=w_in_p, w_vt=w_vt, conv_p=conv_p, gate_bias=gate_bias, gm=gm,
                wq=wq, wqs=wqs, wk=wk, wv=wv, wof=wof, wom=wom, woa=woa)


GATE_I_COLS = np.concatenate([np.arange(M_HEADS), 2 * M_HEADS + np.arange(M_HEADS)])
GATE_F_COLS = GATE_I_COLS + M_HEADS


def _gates_rowmajor(g, nb, n):
    g = g[:, :N_STREAM].reshape(nb, n // MCH, MCH, N_STREAM)
    return g.transpose(0, 1, 3, 2)


def _stream_selectors():
    sel = np.zeros((N_STREAM, SEL_ROWS, MCH), np.float32)
    for s in range(N_STREAM):
        for part in range(3):
            sel[s, part * N_STREAM + s, :] = 1.0
    return jnp.asarray(sel, dtype=BF16)


def kernel(x, c, ctx, c_ctx, w_mod, b_mod, g_norm1, g_norm2, w_in, b_gates, conv_qk, g_mlstm,
           g_q_norm, g_kv_norm, w_uq, w_ukv, w_out, w_up, w_down, g_final):
    nb, seq, d = x.shape
    nctx = ctx.shape[1]
    depth = w_mod.shape[0]
    assert d == D_MODEL and seq % 256 == 0 and nctx % MCH == 0
    sel = _stream_selectors()

    tm = 256
    tm_ctx = min(256, nctx)
    tq = 4 * ATTN_SUB
    tq_ctx = min(ATTN_SUB, nctx)

    dft_cc, dft_cs = _channel_dft()
    dft_lat = _position_dft(seq)
    dft_ctx = _position_dft(nctx)
    rope_lat = _rope_tables(seq, True)
    rope_ctx = _rope_tables(nctx, False)

    rows = ((nb + 1 + 7) // 8) * 8
    cc = jnp.concatenate([c, c_ctx[None, :], jnp.zeros((rows - nb - 1, d), F32)], axis=0)
    mod_all = _modulation(cc, w_mod, b_mod)

    xl = x.reshape(nb * seq, d)
    xc = ctx.reshape(nb * nctx, d)
    row = lambda v: v.reshape(1, -1)

    for l in range(depth):
        last = l == depth - 1
        lw = _layer_weights(l, w_in, conv_qk, b_gates, g_mlstm, w_uq, w_ukv, w_out)
        wup = w_up[l].astype(BF16)
        wdn = w_down[l].astype(BF16)
        mod = mod_all[l].reshape(rows, 6, d)
        mod_lat, mod_ctx = mod[:nb], mod[nb:nb + 1]

        def tokenwise(xt, n, m, per_batch, tile):
            return _inproj(xt, n, row(g_norm1[l]), m[:, 0:1], m[:, 1:2], per_batch,
                           lw["w_in_p"], lw["w_vt"], dft_cc, dft_cs, tile)

        zc, zs, mqk, vt, mo, gi, gf, cq, ckv, kr = tokenwise(xl, seq, mod_lat, True, tm)
        zc_c, zs_c, mqk_c, vt_c, mo_c, gi_c, gf_c, cq_c, ckv_c, kr_c = tokenwise(xc, nctx, mod_ctx, False, tm_ctx)

        yf = _fourier(*dft_lat, zc, zs)

        def mlstm_inputs(mqk_s, vt_s, gi_s, gf_s, n):
            q_s, k_s = _conv_silu(mqk_s.reshape(nb, n, 2 * MP_WIDTH), lw["conv_p"])
            return (q_s, k_s, vt_s, _gates_rowmajor(gi_s, nb, n), _gates_rowmajor(gf_s, nb, n))

        hm, hm_c = _mlstm(mlstm_inputs(mqk, vt, gi, gf, seq), mlstm_inputs(mqk_c, vt_c, gi_c, gf_c, nctx),
                          sel=sel, **lw["gate_bias"])

        gq, gkv = row(g_q_norm[l]), row(g_kv_norm[l])
        q_a, k_a, v_a = _mla_proj(cq, ckv, kr, *rope_lat, seq, gq, gkv,
                                  lw["wq"], lw["wqs"], lw["wk"], lw["wv"], tm, True)
        q_ac, k_ac, v_ac = _mla_proj(cq_c, ckv_c, kr_c, *rope_ctx, nctx, gq, gkv,
                                     lw["wq"], lw["wqs"], lw["wk"], lw["wv"], tm_ctx, not last)
        b3 = lambda a, n: a.reshape(nb, n, AP_WIDTH)
        keys_ctx = (b3(k_ac, nctx), b3(v_ac, nctx))
        ya = _attention(b3(q_a, seq), [(b3(k_a, seq), b3(v_a, seq)), keys_ctx], tq)

        mlp = functools.partial(
            _out_mlp, gm=lw["gm"], g2=row(g_norm2[l]), gfin=row(g_final),
            wof=lw["wof"], wom=lw["wom"], woa=lw["woa"], wup=wup, wdn=wdn)
        xl = mlp(xl, seq, yf, hm.reshape(nb * seq, MP_WIDTH), mo, ya.reshape(nb * seq, AP_WIDTH),
                 mod_lat[:, 2:6], True, tm=tm, final_norm=last)

        if not last:
            yf_c = _fourier(*dft_ctx, zc_c, zs_c)
            ya_c = _attention(b3(q_ac, nctx), [keys_ctx], tq_ctx)
            xc = mlp(xc, nctx, yf_c, hm_c.reshape(nb * nctx, MP_WIDTH), mo_c,
                     ya_c.reshape(nb * nctx, AP_WIDTH), mod_ctx[:, 2:6], False, tm=tm_ctx, final_norm=False)

    return xl.reshape(nb, seq, d)
```

```python
import functools

import numpy as np
import jax
import jax.numpy as jnp
from jax import lax
from jax.experimental import pallas as pl
from jax.experimental.pallas import tpu as pltpu

D_MODEL = 1024
GRID_W = 64
EPS = 1e-6
F_GROUPS = 4
F_GROUP_DIM = D_MODEL // 16
F_WIDTH = F_GROUPS * F_GROUP_DIM
M_HEADS = 4
M_HEAD_DIM = 3 * D_MODEL // 32
M_WIDTH = M_HEADS * M_HEAD_DIM
M_CHUNK = 64
K_CONV = 5
A_HEADS = 4
A_NOPE = 64
A_ROPE = 32
A_V = 3 * D_MODEL // 32
Q_LORA = D_MODEL // 4
KV_LORA = D_MODEL // 8
ROPE_THETA = 10000.0
MLP_HIDDEN = 4 * D_MODEL

LANES = 128
HEAD_PAD = 128
MP_WIDTH = M_HEADS * HEAD_PAD
AP_WIDTH = A_HEADS * HEAD_PAD
VMEM_LIMIT = 56 * 1024 * 1024
MCH = 128

OFF_PF = 0
OFF_MQ = OFF_PF + F_WIDTH
OFF_MK = OFF_MQ + MP_WIDTH
OFF_MO = OFF_MK + MP_WIDTH
OFF_GI = OFF_MO + MP_WIDTH
OFF_GF = OFF_GI + LANES
OFF_CQ = OFF_GF + LANES
OFF_CKV = OFF_CQ + Q_LORA
OFF_KR = OFF_CKV + KV_LORA
IN_PAD = OFF_KR + LANES

BF16 = jnp.bfloat16
F32 = jnp.float32


def _cparams(sem):
    return pltpu.CompilerParams(dimension_semantics=sem, vmem_limit_bytes=VMEM_LIMIT)


def _const_spec(shape):
    nd = len(shape)
    return pl.BlockSpec(shape, lambda *_: (0,) * nd, pipeline_mode=pl.Buffered(1))


def _split3(a):
    hi = a.astype(BF16)
    r1 = a - hi.astype(F32)
    mid = r1.astype(BF16)
    lo = (r1 - mid.astype(F32)).astype(BF16)
    return hi, mid, lo


def _dot(a, b):
    return jnp.dot(a, b, preferred_element_type=F32)


def _dot_nt(a, b):
    return lax.dot_general(a, b, (((1,), (1,)), ((), ())), preferred_element_type=F32)


def _dot_tn(a, b):
    return lax.dot_general(a, b, (((0,), (0,)), ((), ())), preferred_element_type=F32)


def _rms(x, g):
    return x * lax.rsqrt(jnp.mean(x * x, axis=-1, keepdims=True) + EPS) * g


def _mod_kernel(c_ref, w_ref, b_ref, o_ref):
    c = c_ref[...]
    a = c * jax.nn.sigmoid(c)
    a_hi = a.astype(BF16)
    a_lo = (a - a_hi.astype(F32)).astype(BF16)
    w = w_ref[0]
    w_hi = w.astype(BF16)
    w_lo = (w - w_hi.astype(F32)).astype(BF16)
    acc = _dot(a_hi, w_hi) + _dot(a_hi, w_lo) + _dot(a_lo, w_hi)
    o_ref[0] = acc + b_ref[0]


def _modulation(cc, w_mod, b_mod):
    depth, d, n = w_mod.shape
    rows = cc.shape[0]
    tn = 1536
    return pl.pallas_call(
        _mod_kernel,
        grid=(depth, n // tn),
        in_specs=[
            pl.BlockSpec((rows, d), lambda l, j: (0, 0)),
            pl.BlockSpec((1, d, tn), lambda l, j: (l, 0, j)),
            pl.BlockSpec((1, 1, tn), lambda l, j: (l, 0, j)),
        ],
        out_specs=pl.BlockSpec((1, rows, tn), lambda l, j: (l, 0, j)),
        out_shape=jax.ShapeDtypeStruct((depth, rows, n), F32),
        compiler_params=_cparams(("arbitrary", "arbitrary")),
        name="modulation",
    )(cc, w_mod, b_mod.reshape(depth, 1, n))


def _rope(x, cos, sin, first_half):
    half = A_ROPE // 4
    partner = jnp.where(first_half, pltpu.roll(x, LANES - half, 1), pltpu.roll(x, half, 1))
    return x * cos + partner * sin


def _inproj_kernel(x_ref, g_ref, sh_ref, sc_ref, w_ref, wvt_ref, cc_ref, cs_ref,
                   cos_ref, sin_ref, gq_ref, gkv_ref, wq_ref, wk_ref, wv_ref,
                   zc_ref, zs_ref, mqk_ref, vt_ref, mo_ref, gates_ref, qa_ref, ka_ref, va_ref, *, with_q):
    x = x_ref[...]
    h = _rms(x, g_ref[...]) * (1.0 + sc_ref[0]) + sh_ref[0]
    hb = h.astype(BF16)

    def proj(off, width):
        return _dot(hb, w_ref[:, off:off + width])

    ckv_kr = proj(OFF_CKV, 2 * LANES)
    cq = proj(OFF_CQ, Q_LORA) if with_q else None
    pf = proj(OFF_PF, F_WIDTH).astype(BF16)

    mqk_ref[...] = proj(OFF_MQ, 2 * MP_WIDTH)

    cos = cos_ref[...]
    sin = sin_ref[...]
    lane = lax.broadcasted_iota(jnp.int32, cos.shape, 1)
    first_half = ((lane - A_NOPE) & (A_ROPE // 2 - 1)) < A_ROPE // 4
    kvn = _rms(ckv_kr[:, :KV_LORA], gkv_ref[...]).astype(BF16)
    k_rope = _rope(ckv_kr[:, KV_LORA:], cos, sin, first_half)
    for hd in range(A_HEADS):
        lanes = slice(hd * HEAD_PAD, (hd + 1) * HEAD_PAD)
        ka_ref[:, lanes] = (_dot(kvn, wk_ref[:, lanes]) + k_rope).astype(BF16)
        va_ref[:, lanes] = _dot(kvn, wv_ref[:, lanes]).astype(BF16)
    if with_q:
        qn = _rms(cq, gq_ref[...]).astype(BF16)
        q_raw = [_dot(qn, wq_ref[:, hd * HEAD_PAD:(hd + 1) * HEAD_PAD]) for hd in range(A_HEADS)]
    zc_ref[...] = _dot(pf, cc_ref[...]).astype(BF16)
    zs_ref[...] = _dot(pf, cs_ref[...]).astype(BF16)

    vt = _dot_nt(wvt_ref[...], hb).astype(BF16)
    for j in range(vt_ref.shape[0]):
        vt_ref[j] = vt[:, j * MCH:(j + 1) * MCH]
    mo_ref[...] = proj(OFF_MO, MP_WIDTH)
    gates_ref[...] = proj(OFF_GI, 2 * LANES)

    if with_q:
        scale = (A_NOPE + A_ROPE) ** -0.5
        for hd in range(A_HEADS):
            q = _rope(q_raw[hd], cos, sin, first_half)
            qa_ref[:, hd * HEAD_PAD:(hd + 1) * HEAD_PAD] = (q * scale).astype(BF16)
    else:
        qa_ref[...] = jnp.zeros_like(qa_ref)


def _inproj(x2d, seq, g, sh, sc, per_batch_mod, w_in_p, w_vt, dft_cc, dft_cs,
            cos, sin, gq, gkv, wq, wk, wv, tm, with_q):
    t, d = x2d.shape
    nb = t // seq
    tiles_per_seq = seq // tm
    if per_batch_mod:
        mod_map = lambda i: (i // tiles_per_seq, 0, 0)
    else:
        mod_map = lambda i: (0, 0, 0)
    tok = lambda w: pl.BlockSpec((tm, w), lambda i: (i, 0))
    z_spec = pl.BlockSpec((tm, F_WIDTH), lambda i: (i % tiles_per_seq, i // tiles_per_seq))
    pos = pl.BlockSpec((tm, LANES), lambda i: (i % tiles_per_seq, 0))
    heads_bf16 = jax.ShapeDtypeStruct((t, AP_WIDTH), BF16)
    shapes = [
        jax.ShapeDtypeStruct((seq, nb * F_WIDTH), BF16),
        jax.ShapeDtypeStruct((seq, nb * F_WIDTH), BF16),
        jax.ShapeDtypeStruct((t, 2 * MP_WIDTH), F32),
        jax.ShapeDtypeStruct((t // MCH, MP_WIDTH, MCH), BF16),
        jax.ShapeDtypeStruct((t, MP_WIDTH), F32),
        jax.ShapeDtypeStruct((t, 2 * LANES), F32),
        heads_bf16, heads_bf16, heads_bf16,
    ]
    vt_spec = pl.BlockSpec((tm // MCH, MP_WIDTH, MCH), lambda i: (i, 0, 0))
    out_specs = [z_spec, z_spec, tok(2 * MP_WIDTH), vt_spec, tok(MP_WIDTH), tok(2 * LANES),
                 tok(AP_WIDTH), tok(AP_WIDTH), tok(AP_WIDTH)]
    return pl.pallas_call(
        functools.partial(_inproj_kernel, with_q=with_q),
        grid=(t // tm,),
        in_specs=[
            tok(d),
            _const_spec((1, d)),
            pl.BlockSpec((1, 1, d), mod_map),
            pl.BlockSpec((1, 1, d), mod_map),
            _const_spec((d, IN_PAD)),
            _const_spec((MP_WIDTH, d)),
            _const_spec((F_WIDTH, F_WIDTH)),
            _const_spec((F_WIDTH, F_WIDTH)),
            pos, pos,
            _const_spec((1, Q_LORA)), _const_spec((1, KV_LORA)),
            _const_spec((Q_LORA, AP_WIDTH)), _const_spec((KV_LORA, AP_WIDTH)), _const_spec((KV_LORA, AP_WIDTH)),
        ],
        out_specs=out_specs,
        out_shape=shapes,
        compiler_params=_cparams(("arbitrary",)),
        name="inproj",
    )(x2d, g, sh, sc, w_in_p, w_vt, dft_cc, dft_cs, cos, sin, gq, gkv, wq, wk, wv)


def _fourier_kernel(c_ref, s_ref, zc_ref, zs_ref, o_ref):
    y = _dot(c_ref[...], zc_ref[...]) - _dot(s_ref[...], zs_ref[...])
    o_ref[...] = y.astype(BF16)


def _fourier(dft_c, dft_s, zc, zs):
    n, cols = zc.shape
    tr = min(n, 512)
    tc = min(cols, 512)
    return pl.pallas_call(
        _fourier_kernel,
        grid=(n // tr, cols // tc),
        in_specs=[
            pl.BlockSpec((tr, n), lambda i, j: (i, 0)),
            pl.BlockSpec((tr, n), lambda i, j: (i, 0)),
            pl.BlockSpec((n, tc), lambda i, j: (0, j)),
            pl.BlockSpec((n, tc), lambda i, j: (0, j)),
        ],
        out_specs=pl.BlockSpec((tr, tc), lambda i, j: (i, j)),
        out_shape=jax.ShapeDtypeStruct((n, cols), BF16),
        compiler_params=_cparams(("arbitrary", "arbitrary")),
        name="fourier",
    )(dft_c, dft_s, zc, zs)


CONV_ROWS = 256
CONV_HALO = 8


def _conv_kernel(u_ref, w_ref, q_ref, k_ref, pad_ref):
    n = u_ref.shape[1]
    width = u_ref.shape[2]
    zeros = jnp.zeros((CONV_HALO, width), F32)
    pad_ref[0:CONV_HALO, :] = zeros
    pad_ref[CONV_HALO + n:2 * CONV_HALO + n, :] = zeros
    pad_ref[CONV_HALO:CONV_HALO + n, :] = u_ref[0]
    w = w_ref[...]
    rows = min(CONV_ROWS, n)
    for r in range(n // rows):
        base = CONV_HALO + r * rows - K_CONV // 2
        acc = pad_ref[base:base + rows, :] * w[0:1, :]
        for j in range(1, K_CONV):
            acc = acc + pad_ref[base + j:base + j + rows, :] * w[j:j + 1, :]
        act = acc * jax.nn.sigmoid(acc)
        q_ref[0, r * rows:(r + 1) * rows, :] = (act[:, :MP_WIDTH] * (M_HEAD_DIM ** -0.5)).astype(BF16)
        k_ref[0, r * rows:(r + 1) * rows, :] = act[:, MP_WIDTH:].astype(BF16)


def _conv_silu(mqk, conv_w_p):
    nb, n, width = mqk.shape
    return pl.pallas_call(
        _conv_kernel,
        grid=(nb,),
        in_specs=[
            pl.BlockSpec((1, n, width), lambda b: (b, 0, 0)),
            _const_spec((8, width)),
        ],
        out_specs=[
            pl.BlockSpec((1, n, MP_WIDTH), lambda b: (b, 0, 0)),
            pl.BlockSpec((1, n, MP_WIDTH), lambda b: (b, 0, 0)),
        ],
        out_shape=[
            jax.ShapeDtypeStruct((nb, n, MP_WIDTH), BF16),
            jax.ShapeDtypeStruct((nb, n, MP_WIDTH), BF16),
        ],
        scratch_shapes=[pltpu.VMEM((n + 2 * CONV_HALO, width), F32)],
        compiler_params=_cparams(("arbitrary",)),
        name="conv_silu",
    )(mqk, conv_w_p)


def _log_sigmoid(x):
    return jnp.minimum(x, 0.0) - jnp.log(1.0 + jnp.exp(-jnp.abs(x)))


def _exact_dot_01(a, tri_bf16, a_on_left):
    out = None
    for term in _split3(a):
        d = _dot(term, tri_bf16) if a_on_left else _dot(tri_bf16, term)
        out = d if out is None else out + d
    return out


N_STREAM = 2 * M_HEADS
ONE_ROW = M_HEAD_DIM
(F_R, F_B, F_CM, F_TOT, F_CML, F_MP, F_A, F_WI, F_ELD, F_DEC, F_WK, F_HI, F_MID, F_LO) = range(14)
N_FIELDS = 14
OUT_GROUP = 4
SEL_ROWS = 32


def _mlstm_kernel(ql_ref, kl_ref, vtl_ref, gril_ref, grfl_ref,
                  qc_ref, kc_ref, vtc_ref, gric_ref, grfc_ref,
                  bir_ref, bfr_ref, sel_ref,
                  hl_ref, hc_ref,
                  ct_st, st_sc, rows_sc):
    L = MCH
    nh = M_HEADS
    ncc = qc_ref.shape[1] // L
    ncl = ql_ref.shape[1] // L

    d0 = lax.broadcasted_iota(jnp.int32, (L, L), 0)
    d1 = lax.broadcasted_iota(jnp.int32, (L, L), 1)
    le = d0 <= d1
    ge = d0 >= d1
    tri_le = le.astype(BF16)
    fwd_rows = lax.broadcasted_iota(jnp.int32, (N_STREAM, L), 0) < nh
    feat = lax.broadcasted_iota(jnp.int32, (HEAD_PAD, L), 0)
    one_row = feat == ONE_ROW
    keep_rows = feat < M_HEAD_DIM

    ct_st[...] = jnp.zeros_like(ct_st)

    def field(f, slot0, nc):
        return rows_sc[f, slot0:slot0 + nc].reshape(nc * N_STREAM, L)

    def set_field(f, slot0, nc, val):
        rows_sc[f, slot0:slot0 + nc] = val.reshape(nc, N_STREAM, L)

    def gate_pass(gri_ref, grf_ref, nc, slot0):
        n8 = nc * N_STREAM
        fwd = (lax.broadcasted_iota(jnp.int32, (n8, L), 0) & (N_STREAM - 1)) < nh
        lane = lax.broadcasted_iota(jnp.int32, (n8, L), 1)
        gi = gri_ref[0].reshape(n8, L) + bir_ref[0:n8, :]
        f = _log_sigmoid(grf_ref[0].reshape(n8, L) + bfr_ref[0:n8, :])
        pre = _exact_dot_01(f, tri_le, a_on_left=True)
        total = jnp.sum(f, axis=1, keepdims=True)
        b = jnp.where(fwd, pre, total - pre + f)
        r = gi - b
        cm = r
        sh = 1
        while sh < L:
            from_left = jnp.where(lane >= sh, pltpu.roll(cm, sh, 1), -jnp.inf)
            from_right = jnp.where(lane < L - sh, pltpu.roll(cm, L - sh, 1), -jnp.inf)
            cm = jnp.maximum(cm, jnp.where(fwd, from_left, from_right))
            sh *= 2
        set_field(F_R, slot0, nc, r)
        set_field(F_B, slot0, nc, b)
        set_field(F_CM, slot0, nc, cm)
        set_field(F_TOT, slot0, nc, jnp.broadcast_to(total, (n8, L)))
        set_field(F_CML, slot0, nc, jnp.broadcast_to(jnp.max(r, axis=1, keepdims=True), (n8, L)))

    def m_scan(nc, slot0, m0):
        def step(j, m):
            sf = slot0 + j
            sb = slot0 + nc - 1 - j
            tot = jnp.where(fwd_rows, rows_sc[F_TOT, sf], rows_sc[F_TOT, sb])
            cml = jnp.where(fwd_rows, rows_sc[F_CML, sf], rows_sc[F_CML, sb])
            rows_sc[F_MP, sf, 0:nh, :] = m[0:nh]
            rows_sc[F_MP, sb, nh:N_STREAM, :] = m[nh:N_STREAM]
            return tot + jnp.maximum(m, cml)

        return lax.fori_loop(0, nc, step, m0)

    def weight_pass(nc, slot0):
        mp, cm, b, r = (field(f, slot0, nc) for f in (F_MP, F_CM, F_B, F_R))
        big = jnp.maximum(mp, field(F_CML, slot0, nc))
        a = -jnp.maximum(mp, cm)
        set_field(F_A, slot0, nc, a)
        set_field(F_WI, slot0, nc, jnp.exp(mp + a))
        set_field(F_ELD, slot0, nc, jnp.exp(a - b))
        set_field(F_DEC, slot0, nc, jnp.exp(mp - big))
        set_field(F_WK, slot0, nc, jnp.exp(r - big))
        for f, term in zip((F_HI, F_MID, F_LO), _split3(r)):
            set_field(f, slot0, nc, term.astype(F32))

    def value_slab(vt_ref, c, hd):
        vt = vt_ref[c, hd * HEAD_PAD:(hd + 1) * HEAD_PAD, :].astype(F32)
        return jnp.where(one_row, 1.0, vt)

    def state_pass(k_ref, vt_ref, nc, slot0):
        def step(j, carry):
            uts = []
            for sidx in range(N_STREAM):
                hd = sidx % nh
                c = j if sidx < nh else nc - 1 - j
                wk = rows_sc[F_WK, slot0 + c, sidx:sidx + 1, :]
                kk = k_ref[0, pl.ds(pl.multiple_of(c * L, L), L), hd * HEAD_PAD:(hd + 1) * HEAD_PAD]
                uts.append(_dot((value_slab(vt_ref, c, hd) * wk).astype(BF16), kk))
            for sidx in range(N_STREAM):
                slot = slot0 + (j if sidx < nh else nc - 1 - j)
                prev = ct_st[sidx]
                st_sc[sidx, slot] = prev.astype(BF16)
                ct_st[sidx] = rows_sc[F_DEC, slot, sidx:sidx + 1, :] * prev + uts[sidx]
            return carry

        lax.fori_loop(0, nc, step, 0)

    def output_pass(q_ref, k_ref, vt_ref, out_ref, nc, slot0):
        head_lanes = [slice(hd * HEAD_PAD, (hd + 1) * HEAD_PAD) for hd in range(nh)]

        def independent_matmuls(c):
            rows = pl.ds(pl.multiple_of(c * L, L), L)
            slot = slot0 + c
            r3 = jnp.concatenate([rows_sc[F_HI, slot], rows_sc[F_MID, slot], rows_sc[F_LO, slot],
                                  jnp.zeros((N_STREAM, L), F32)], axis=0).astype(BF16)
            qs = [q_ref[0, rows, lanes] for lanes in head_lanes]
            s_ts = [_dot_nt(k_ref[0, rows, lanes], q) for lanes, q in zip(head_lanes, qs)]
            inters = [_dot_nt(st_sc[sidx, slot], qs[sidx % nh]) for sidx in range(N_STREAM)]
            r_ts = [_dot_tn(r3, sel_ref[sidx]) for sidx in range(N_STREAM)]
            return s_ts, inters, r_ts

        def finish(c, s_ts, inters, r_ts):
            rows = pl.ds(pl.multiple_of(c * L, L), L)
            slot = slot0 + c
            a_rows = rows_sc[F_A, slot]
            wi_rows = rows_sc[F_WI, slot]
            eld_rows = rows_sc[F_ELD, slot]
            p_ts = []
            for sidx in range(N_STREAM):
                one = slice(sidx, sidx + 1)
                dm = jnp.where(le if sidx < nh else ge, r_ts[sidx] + a_rows[one, :], -jnp.inf)
                p_ts.append((s_ts[sidx % nh] * jnp.exp(dm)).astype(BF16))
            vtas = [value_slab(vt_ref, c, hd).astype(BF16) for hd in range(nh)]
            z_ts = [_dot(vtas[sidx % nh], p_ts[sidx]) for sidx in range(N_STREAM)]
            for hd in range(nh):
                hsum = None
                for sidx in (hd, nh + hd):
                    one = slice(sidx, sidx + 1)
                    z_t = z_ts[sidx] + inters[sidx] * wi_rows[one, :]
                    den = z_t[ONE_ROW:ONE_ROW + 1, :]
                    h_t = z_t * (1.0 / jnp.maximum(jnp.abs(den), eld_rows[one, :]))
                    hsum = h_t if hsum is None else hsum + h_t
                out_ref[0, rows, head_lanes[hd]] = jnp.where(keep_rows, hsum, 0.0).T

        group = min(OUT_GROUP, nc)

        def step(g, carry):
            ahead = independent_matmuls(g * group)
            for u in range(group):
                cur = ahead
                if u + 1 < group:
                    ahead = independent_matmuls(g * group + u + 1)
                finish(g * group + u, *cur)
            return carry

        lax.fori_loop(0, nc // group, step, 0)

    gate_pass(gric_ref, grfc_ref, ncc, 0)
    gate_pass(gril_ref, grfl_ref, ncl, ncc)
    m1 = m_scan(ncc, 0, jnp.zeros((N_STREAM, L), F32))
    m_scan(ncl, ncc, m1)
    weight_pass(ncc, 0)
    weight_pass(ncl, ncc)
    state_pass(kc_ref, vtc_ref, ncc, 0)
    state_pass(kl_ref, vtl_ref, ncl, ncc)
    output_pass(qc_ref, kc_ref, vtc_ref, hc_ref, ncc, 0)
    output_pass(ql_ref, kl_ref, vtl_ref, hl_ref, ncl, ncc)


def _mlstm(lat, ctx, bir, bfr, sel):
    nb, n, _ = lat[0].shape
    nctx = ctx[0].shape[1]
    assert MCH == LANES == HEAD_PAD
    nct = (n + nctx) // MCH
    n8 = max(n, nctx) // MCH * N_STREAM

    def specs(rows):
        nc = rows // MCH
        seq = lambda w: pl.BlockSpec((1, rows, w), lambda b: (b, 0, 0))
        gr = pl.BlockSpec((1, nc, N_STREAM, MCH), lambda b: (b, 0, 0, 0))
        return [seq(MP_WIDTH), seq(MP_WIDTH), pl.BlockSpec((nc, MP_WIDTH, MCH), lambda b: (b, 0, 0)), gr, gr]

    out_spec = lambda rows: pl.BlockSpec((1, rows, MP_WIDTH), lambda b: (b, 0, 0))
    return pl.pallas_call(
        _mlstm_kernel,
        grid=(nb,),
        in_specs=specs(n) + specs(nctx) + [
            _const_spec((n8, 1)), _const_spec((n8, 1)),
            _const_spec((N_STREAM, SEL_ROWS, MCH)),
        ],
        out_specs=[out_spec(n), out_spec(nctx)],
        out_shape=[
            jax.ShapeDtypeStruct((nb, n, MP_WIDTH), F32),
            jax.ShapeDtypeStruct((nb, nctx, MP_WIDTH), F32),
        ],
        scratch_shapes=[
            pltpu.VMEM((N_STREAM, HEAD_PAD, HEAD_PAD), F32),
            pltpu.VMEM((N_STREAM, nct, HEAD_PAD, HEAD_PAD), BF16),
            pltpu.VMEM((N_FIELDS, nct, N_STREAM, MCH), F32),
        ],
        compiler_params=_cparams(("arbitrary",)),
        name="mlstm",
    )(*lat, *ctx, jnp.tile(bir, (n8 // N_STREAM, 1)), jnp.tile(bfr, (n8 // N_STREAM, 1)), sel)


ATTN_SUB = 256


def _attn_kernel(*refs, n_sets):
    q_ref = refs[0]
    kv_refs = refs[1:1 + 2 * n_sets]
    o_ref = refs[1 + 2 * n_sets]
    sub = min(ATTN_SUB, q_ref.shape[1])
    n_sub = q_ref.shape[1] // sub

    def scores_of(t):
        q = q_ref[0, t * sub:(t + 1) * sub, :]
        return [_dot_nt(q, kv_refs[2 * i][0]) for i in range(n_sets)]

    nxt = scores_of(0)
    for t in range(n_sub):
        rows = slice(t * sub, (t + 1) * sub)
        scores = nxt
        if t + 1 < n_sub:
            nxt = scores_of(t + 1)
        m = None
        for s in scores:
            sm = jnp.max(s, axis=-1, keepdims=True)
            m = sm if m is None else jnp.maximum(m, sm)
        acc = None
        den = None
        for i, s in enumerate(scores):
            p = jnp.exp(s - m)
            l = jnp.sum(p, axis=-1, keepdims=True)
            o = _dot(p.astype(BF16), kv_refs[2 * i + 1][0])
            acc = o if acc is None else acc + o
            den = l if den is None else den + l
        o_ref[0, rows, :] = (acc / den).astype(BF16)


def _attention(q, key_sets, tq):
    nb, n, _ = q.shape
    n_sets = len(key_sets)
    in_specs = [pl.BlockSpec((1, tq, HEAD_PAD), lambda b, h, i: (b, i, h))]
    args = [q]
    for k, v in key_sets:
        nk = k.shape[1]
        spec = pl.BlockSpec((1, nk, HEAD_PAD), lambda b, h, i: (b, 0, h))
        in_specs += [spec, spec]
        args += [k, v]
    return pl.pallas_call(
        functools.partial(_attn_kernel, n_sets=n_sets),
        grid=(nb, A_HEADS, n // tq),
        in_specs=in_specs,
        out_specs=pl.BlockSpec((1, tq, HEAD_PAD), lambda b, h, i: (b, i, h)),
        out_shape=jax.ShapeDtypeStruct((nb, n, AP_WIDTH), BF16),
        compiler_params=_cparams(("arbitrary", "arbitrary", "arbitrary")),
        name="attention",
    )(*args)


def _out_mlp_kernel(x_ref, yf_ref, hm_ref, mo_ref, ya_ref, mod_ref,
                    gm_ref, g2_ref, gfin_ref, wof_ref, wom_ref, woa_ref, wup_ref, wdn_ref,
                    o_ref, *, final_norm):
    mod = mod_ref[0]
    ga1, sh2, sc2, ga2 = mod[0:1], mod[1:2], mod[2:3], mod[3:4]
    mix = _dot(yf_ref[...], wof_ref[...]) + _dot(ya_ref[...], woa_ref[...])
    gm = gm_ref[...]
    for hd in range(M_HEADS):
        lanes = slice(hd * HEAD_PAD, (hd + 1) * HEAD_PAD)
        hh = hm_ref[:, lanes]
        ms = jnp.sum(hh * hh, axis=-1, keepdims=True) * (1.0 / M_HEAD_DIM)
        ym = hh * lax.rsqrt(ms + EPS) * gm[:, lanes] * jax.nn.sigmoid(mo_ref[:, lanes])
        mix = mix + _dot(ym.astype(BF16), wom_ref[lanes, :])
    x1 = x_ref[...] + ga1 * mix
    h2 = (_rms(x1, g2_ref[...]) * (1.0 + sc2) + sh2).astype(BF16)
    hidden = wup_ref.shape[1]
    hc = 1024
    acc = None
    for c in range(hidden // hc):
        u = jnp.maximum(_dot(h2, wup_ref[:, c * hc:(c + 1) * hc]), 0.0)
        d = _dot((u * u).astype(BF16), wdn_ref[c * hc:(c + 1) * hc, :])
        acc = d if acc is None else acc + d
    x2 = x1 + ga2 * acc
    if final_norm:
        x2 = _rms(x2, gfin_ref[...])
    o_ref[...] = x2


def _out_mlp(x2d, seq, yf, hm, mo, ya, mod4, per_batch_mod, gm, g2, gfin,
             wof, wom, woa, wup, wdn, tm, final_norm):
    t, d = x2d.shape
    tiles_per_seq = seq // tm
    if per_batch_mod:
        mod_map = lambda i: (i // tiles_per_seq, 0, 0)
    else:
        mod_map = lambda i: (0, 0, 0)
    tok = lambda w: pl.BlockSpec((tm, w), lambda i: (i, 0))
    yf_spec = pl.BlockSpec((tm, F_WIDTH), lambda i: (i % tiles_per_seq, i // tiles_per_seq))
    return pl.pallas_call(
        functools.partial(_out_mlp_kernel, final_norm=final_norm),
        grid=(t // tm,),
        in_specs=[
            tok(d), yf_spec, tok(MP_WIDTH), tok(MP_WIDTH), tok(AP_WIDTH),
            pl.BlockSpec((1, 4, d), mod_map),
            _const_spec((1, MP_WIDTH)), _const_spec((1, d)), _const_spec((1, d)),
            _const_spec((F_WIDTH, d)), _const_spec((MP_WIDTH, d)), _const_spec((AP_WIDTH, d)),
            _const_spec((d, wup.shape[1])), _const_spec((wdn.shape[0], d)),
        ],
        out_specs=tok(d),
        out_shape=jax.ShapeDtypeStruct((t, d), F32),
        compiler_params=_cparams(("arbitrary",)),
        name="out_mlp",
    )(x2d, yf, hm, mo, ya, mod4, gm, g2, gfin, wof, wom, woa, wup, wdn)


def _dft_tables(n):
    idx = (np.arange(n, dtype=np.int64)[:, None] * np.arange(n, dtype=np.int64)[None, :]) % n
    ang = 2.0 * np.pi * idx.astype(np.float64) / n
    scale = 1.0 / np.sqrt(n)
    return np.cos(ang) * scale, np.sin(ang) * scale


def _channel_dft():
    c, s = _dft_tables(F_GROUP_DIM)
    eye = np.eye(F_GROUPS)
    return (jnp.asarray(np.kron(eye, c), dtype=F32).astype(BF16),
            jnp.asarray(np.kron(eye, s), dtype=F32).astype(BF16))


def _position_dft(n):
    c, s = _dft_tables(n)
    return jnp.asarray(c, dtype=F32).astype(BF16), jnp.asarray(s, dtype=F32).astype(BF16)


def _rope_tables(n, rotate):
    cos = np.zeros((n, HEAD_PAD), np.float32)
    sin = np.zeros((n, HEAD_PAD), np.float32)
    cos[:, :A_NOPE + A_ROPE] = 1.0
    if rotate:
        nf = A_ROPE // 4
        t = np.arange(n)
        row = (t // GRID_W).astype(np.float32)
        col = (t % GRID_W).astype(np.float32)
        freqs = (np.float32(ROPE_THETA) ** (-np.arange(nf, dtype=np.float32) / np.float32(nf))).astype(np.float32)
        for seg, pos in enumerate((row, col)):
            ang = pos[:, None] * freqs[None, :]
            c, s = np.cos(ang), np.sin(ang)
            base = A_NOPE + seg * 2 * nf
            cos[:, base:base + nf] = c
            cos[:, base + nf:base + 2 * nf] = c
            sin[:, base:base + nf] = -s
            sin[:, base + nf:base + 2 * nf] = s
    return jnp.asarray(cos), jnp.asarray(sin)


def _pad_heads_cols(w, heads, width):
    lead = w.shape[:-1]
    w = w.reshape(lead + (heads, width))
    w = jnp.pad(w, [(0, 0)] * len(lead) + [(0, 0), (0, HEAD_PAD - width)])
    return w.reshape(lead + (heads * HEAD_PAD,))


def _pad_cols(w, width):
    return jnp.pad(w, [(0, 0)] * (w.ndim - 1) + [(0, width - w.shape[-1])])


def _layer_weights(l, w_in, conv_qk, b_gates, g_mlstm, w_uq, w_ukv, w_out):
    wl = w_in[l]
    offs = np.cumsum([0, F_WIDTH, M_WIDTH, M_WIDTH, M_WIDTH, M_WIDTH, 4 * M_HEADS, Q_LORA, KV_LORA, A_ROPE])
    part = lambda i: wl[:, offs[i]:offs[i + 1]]
    place_rope = lambda w: jnp.pad(w, [(0, 0), (A_NOPE, LANES - A_NOPE - A_ROPE)])
    w_in_p = jnp.concatenate([
        part(0),
        _pad_heads_cols(part(1), M_HEADS, M_HEAD_DIM),
        _pad_heads_cols(part(2), M_HEADS, M_HEAD_DIM),
        _pad_heads_cols(part(4), M_HEADS, M_HEAD_DIM),
        _pad_cols(part(5)[:, GATE_I_COLS], LANES),
        _pad_cols(part(5)[:, GATE_F_COLS], LANES),
        part(6),
        part(7),
        place_rope(part(8)),
    ], axis=1).astype(BF16)
    assert w_in_p.shape[1] == IN_PAD
    w_vt = _pad_heads_cols(part(3), M_HEADS, M_HEAD_DIM).T.astype(BF16)

    conv = conv_qk[l]
    conv_p = jnp.concatenate([
        _pad_heads_cols(conv[:, :M_WIDTH], M_HEADS, M_HEAD_DIM),
        _pad_heads_cols(conv[:, M_WIDTH:], M_HEADS, M_HEAD_DIM),
    ], axis=1)
    conv_p = jnp.pad(conv_p, [(0, 8 - K_CONV), (0, 0)])

    bg = b_gates[l]
    bi, bf = bg[GATE_I_COLS], bg[GATE_F_COLS]
    gate_bias = dict(bir=bi[:, None], bfr=bf[:, None])

    gm = _pad_heads_cols(g_mlstm[l][None, :], M_HEADS, M_HEAD_DIM)

    wq = _pad_heads_cols(w_uq[l], A_HEADS, A_NOPE + A_ROPE).astype(BF16)
    ukv = w_ukv[l].reshape(KV_LORA, A_HEADS, A_NOPE + A_V)
    wk = jnp.pad(ukv[..., :A_NOPE], [(0, 0), (0, 0), (0, HEAD_PAD - A_NOPE)]).reshape(KV_LORA, AP_WIDTH).astype(BF16)
    wv = jnp.pad(ukv[..., A_NOPE:], [(0, 0), (0, 0), (0, HEAD_PAD - A_V)]).reshape(KV_LORA, AP_WIDTH).astype(BF16)

    wo = w_out[l]
    pad_rows = lambda w, heads, width: jnp.pad(
        w.reshape(heads, width, -1), [(0, 0), (0, HEAD_PAD - width), (0, 0)]).reshape(heads * HEAD_PAD, -1)
    wof = wo[:F_WIDTH].astype(BF16)
    wom = pad_rows(wo[F_WIDTH:F_WIDTH + M_WIDTH], M_HEADS, M_HEAD_DIM).astype(BF16)
    woa = pad_rows(wo[F_WIDTH + M_WIDTH:], A_HEADS, A_V).astype(BF16)
    return dict(w_in_p=w_in_p, w_vt=w_vt, conv_p=conv_p, gate_bias=gate_bias, gm=gm,
                wq=wq, wk=wk, wv=wv, wof=wof, wom=wom, woa=woa)


GATE_I_COLS = np.concatenate([np.arange(M_HEADS), 2 * M_HEADS + np.arange(M_HEADS)])
GATE_F_COLS = GATE_I_COLS + M_HEADS


def _gates_rowmajor(g, nb, n):
    g = g[:, :N_STREAM].reshape(nb, n // MCH, MCH, N_STREAM)
    return g.transpose(0, 1, 3, 2)


def _split_gates(gates, nb, n):
    return _gates_rowmajor(gates[:, :LANES], nb, n), _gates_rowmajor(gates[:, LANES:], nb, n)


def _stream_selectors():
    sel = np.zeros((N_STREAM, SEL_ROWS, MCH), np.float32)
    for s in range(N_STREAM):
        for part in range(3):
            sel[s, part * N_STREAM + s, :] = 1.0
    return jnp.asarray(sel, dtype=BF16)


def kernel(x, c, ctx, c_ctx, w_mod, b_mod, g_norm1, g_norm2, w_in, b_gates, conv_qk, g_mlstm,
           g_q_norm, g_kv_norm, w_uq, w_ukv, w_out, w_up, w_down, g_final):
    nb, seq, d = x.shape
    nctx = ctx.shape[1]
    depth = w_mod.shape[0]
    assert d == D_MODEL and seq % 256 == 0 and nctx % MCH == 0
    sel = _stream_selectors()

    tm = 256
    tm_ctx = min(256, nctx)
    tq = min(8 * ATTN_SUB, seq)
    tq_ctx = min(ATTN_SUB, nctx)

    dft_cc, dft_cs = _channel_dft()
    dft_lat = _position_dft(seq)
    dft_ctx = _position_dft(nctx)
    rope_lat = _rope_tables(seq, True)
    rope_ctx = _rope_tables(nctx, False)

    rows = ((nb + 1 + 7) // 8) * 8
    cc = jnp.concatenate([c, c_ctx[None, :], jnp.zeros((rows - nb - 1, d), F32)], axis=0)
    mod_all = _modulation(cc, w_mod, b_mod)

    xl = x.reshape(nb * seq, d)
    xc = ctx.reshape(nb * nctx, d)
    row = lambda v: v.reshape(1, -1)

    for l in range(depth):
        last = l == depth - 1
        lw = _layer_weights(l, w_in, conv_qk, b_gates, g_mlstm, w_uq, w_ukv, w_out)
        wup = w_up[l].astype(BF16)
        wdn = w_down[l].astype(BF16)
        mod = mod_all[l].reshape(rows, 6, d)
        mod_lat, mod_ctx = mod[:nb], mod[nb:nb + 1]

        def tokenwise(xt, n, m, per_batch, rope, tile, with_q):
            return _inproj(xt, n, row(g_norm1[l]), m[:, 0:1], m[:, 1:2], per_batch,
                           lw["w_in_p"], lw["w_vt"], dft_cc, dft_cs, *rope,
                           row(g_q_norm[l]), row(g_kv_norm[l]), lw["wq"], lw["wk"], lw["wv"], tile, with_q)

        zc, zs, mqk, vt, mo, gates, q_a, k_a, v_a = tokenwise(xl, seq, mod_lat, True, rope_lat, tm, True)
        zc_c, zs_c, mqk_c, vt_c, mo_c, gates_c, q_ac, k_ac, v_ac = tokenwise(
            xc, nctx, mod_ctx, False, rope_ctx, tm_ctx, not last)

        yf = _fourier(*dft_lat, zc, zs)

        def mlstm_inputs(mqk_s, vt_s, gates_s, n):
            q_s, k_s = _conv_silu(mqk_s.reshape(nb, n, 2 * MP_WIDTH), lw["conv_p"])
            return (q_s, k_s, vt_s) + _split_gates(gates_s, nb, n)

        hm, hm_c = _mlstm(mlstm_inputs(mqk, vt, gates, seq), mlstm_inputs(mqk_c, vt_c, gates_c, nctx),
                          sel=sel, **lw["gate_bias"])

        b3 = lambda a, n: a.reshape(nb, n, AP_WIDTH)
        keys_ctx = (b3(k_ac, nctx), b3(v_ac, nctx))
        ya = _attention(b3(q_a, seq), [(b3(k_a, seq), b3(v_a, seq)), keys_ctx], tq)

        mlp = functools.partial(
            _out_mlp, gm=lw["gm"], g2=row(g_norm2[l]), gfin=row(g_final),
            wof=lw["wof"], wom=lw["wom"], woa=lw["woa"], wup=wup, wdn=wdn)
        xl = mlp(xl, seq, yf, hm.reshape(nb * seq, MP_WIDTH), mo, ya.reshape(nb * seq, AP_WIDTH),
                 mod_lat[:, 2:6], True, tm=tm, final_norm=last)

        if not last:
            yf_c = _fourier(*dft_ctx, zc_c, zs_c)
            ya_c = _attention(b3(q_ac, nctx), [keys_ctx], tq_ctx)
            xc = mlp(xc, nctx, yf_c, hm_c.reshape(nb * nctx, MP_WIDTH), mo_c,
                     ya_c.reshape(nb * nctx, AP_WIDTH), mod_ctx[:, 2:6], False, tm=tm_ctx, final_norm=False)

    return xl.reshape(nb, seq, d)
```

```python
import functools

import numpy as np
import jax
import jax.numpy as jnp
from jax import lax
from jax.experimental import pallas as pl
from jax.experimental.pallas import tpu as pltpu

D_MODEL = 1024
GRID_W = 64
EPS = 1e-6
F_GROUPS = 4
F_GROUP_DIM = D_MODEL // 16
F_WIDTH = F_GROUPS * F_GROUP_DIM
M_HEADS = 4
M_HEAD_DIM = 3 * D_MODEL // 32
M_WIDTH = M_HEADS * M_HEAD_DIM
M_CHUNK = 64
K_CONV = 5
A_HEADS = 4
A_NOPE = 64
A_ROPE = 32
A_V = 3 * D_MODEL // 32
Q_LORA = D_MODEL // 4
KV_LORA = D_MODEL // 8
ROPE_THETA = 10000.0
MLP_HIDDEN = 4 * D_MODEL

LANES = 128
HEAD_PAD = 128
MP_WIDTH = M_HEADS * HEAD_PAD
AP_WIDTH = A_HEADS * HEAD_PAD
VMEM_LIMIT = 56 * 1024 * 1024
MCH = 128

OFF_PF = 0
OFF_MQ = OFF_PF + F_WIDTH
OFF_MK = OFF_MQ + MP_WIDTH
OFF_MO = OFF_MK + MP_WIDTH
OFF_CQ = OFF_MO + MP_WIDTH
OFF_CKV = OFF_CQ + Q_LORA
OFF_KR = OFF_CKV + KV_LORA
IN_PAD = OFF_KR + LANES

BF16 = jnp.bfloat16
F32 = jnp.float32


def _cparams(sem):
    return pltpu.CompilerParams(dimension_semantics=sem, vmem_limit_bytes=VMEM_LIMIT)


def _const_spec(shape):
    nd = len(shape)
    return pl.BlockSpec(shape, lambda *_: (0,) * nd, pipeline_mode=pl.Buffered(1))


def _split3(a):
    hi = a.astype(BF16)
    r1 = a - hi.astype(F32)
    mid = r1.astype(BF16)
    lo = (r1 - mid.astype(F32)).astype(BF16)
    return hi, mid, lo


def _dot(a, b):
    return jnp.dot(a, b, preferred_element_type=F32)


def _dot_nt(a, b):
    return lax.dot_general(a, b, (((1,), (1,)), ((), ())), preferred_element_type=F32)


def _dot_tn(a, b):
    return lax.dot_general(a, b, (((0,), (0,)), ((), ())), preferred_element_type=F32)


def _rms(x, g):
    return x * lax.rsqrt(jnp.mean(x * x, axis=-1, keepdims=True) + EPS) * g


def _mod_kernel(c_ref, w_ref, b_ref, o_ref):
    c = c_ref[...]
    a = c * jax.nn.sigmoid(c)
    a_hi = a.astype(BF16)
    a_lo = (a - a_hi.astype(F32)).astype(BF16)
    w = w_ref[0]
    w_hi = w.astype(BF16)
    w_lo = (w - w_hi.astype(F32)).astype(BF16)
    acc = _dot(a_hi, w_hi) + _dot(a_hi, w_lo) + _dot(a_lo, w_hi)
    o_ref[0] = acc + b_ref[0]


def _modulation(cc, w_mod, b_mod):
    depth, d, n = w_mod.shape
    rows = cc.shape[0]
    tn = 1536
    return pl.pallas_call(
        _mod_kernel,
        grid=(depth, n // tn),
        in_specs=[
            pl.BlockSpec((rows, d), lambda l, j: (0, 0)),
            pl.BlockSpec((1, d, tn), lambda l, j: (l, 0, j)),
            pl.BlockSpec((1, 1, tn), lambda l, j: (l, 0, j)),
        ],
        out_specs=pl.BlockSpec((1, rows, tn), lambda l, j: (l, 0, j)),
        out_shape=jax.ShapeDtypeStruct((depth, rows, n), F32),
        compiler_params=_cparams(("arbitrary", "arbitrary")),
        name="modulation",
    )(cc, w_mod, b_mod.reshape(depth, 1, n))


def _rope(x, cos, sin, first_half):
    half = A_ROPE // 4
    partner = jnp.where(first_half, pltpu.roll(x, LANES - half, 1), pltpu.roll(x, half, 1))
    return x * cos + partner * sin


def _inproj_kernel(x_ref, g_ref, mod_ref, w_ref, wvt_ref, cc_ref, cs_ref,
                   cos_ref, sin_ref, gq_ref, gkv_ref, wq_ref, wk_ref, wv_ref,
                   zc_ref, zs_ref, mqk_ref, vt_ref, mo_ref, gr_ref, qa_ref, ka_ref, va_ref, *, with_q):
    x = x_ref[...]
    mod = mod_ref[0]
    h = _rms(x, g_ref[...]) * (1.0 + mod[1:2]) + mod[0:1]
    hb = h.astype(BF16)

    def proj(off, width):
        return _dot(hb, w_ref[:, off:off + width])

    ckv_kr = proj(OFF_CKV, 2 * LANES)
    cq = proj(OFF_CQ, Q_LORA) if with_q else None
    pf = proj(OFF_PF, F_WIDTH).astype(BF16)

    mqk_ref[...] = proj(OFF_MQ, 2 * MP_WIDTH)

    cos = cos_ref[...]
    sin = sin_ref[...]
    lane = lax.broadcasted_iota(jnp.int32, cos.shape, 1)
    first_half = ((lane - A_NOPE) & (A_ROPE // 2 - 1)) < A_ROPE // 4
    kvn = _rms(ckv_kr[:, :KV_LORA], gkv_ref[...]).astype(BF16)
    k_rope = _rope(ckv_kr[:, KV_LORA:], cos, sin, first_half)
    for hd in range(A_HEADS):
        lanes = slice(hd * HEAD_PAD, (hd + 1) * HEAD_PAD)
        ka_ref[:, lanes] = (_dot(kvn, wk_ref[:, lanes]) + k_rope).astype(BF16)
        va_ref[:, lanes] = _dot(kvn, wv_ref[:, lanes]).astype(BF16)
    if with_q:
        qn = _rms(cq, gq_ref[...]).astype(BF16)
        q_raw = [_dot(qn, wq_ref[:, hd * HEAD_PAD:(hd + 1) * HEAD_PAD]) for hd in range(A_HEADS)]
    zc_ref[...] = _dot(pf, cc_ref[...]).astype(BF16)
    zs_ref[...] = _dot(pf, cs_ref[...]).astype(BF16)

    vg = _dot_nt(wvt_ref[...], hb)
    vt = vg[:MP_WIDTH].astype(BF16)
    for j in range(vt_ref.shape[0]):
        vt_ref[j] = vt[:, j * MCH:(j + 1) * MCH]
        gr_ref[j] = vg[MP_WIDTH:, j * MCH:(j + 1) * MCH]
    mo_ref[...] = proj(OFF_MO, MP_WIDTH)

    if with_q:
        scale = (A_NOPE + A_ROPE) ** -0.5
        for hd in range(A_HEADS):
            q = _rope(q_raw[hd], cos, sin, first_half)
            qa_ref[:, hd * HEAD_PAD:(hd + 1) * HEAD_PAD] = (q * scale).astype(BF16)
    else:
        qa_ref[...] = jnp.zeros_like(qa_ref)


def _mod_map(mod_row0, per_batch_mod, tiles_per_seq):
    if per_batch_mod:
        return lambda i: (mod_row0 + i // tiles_per_seq, 0, 0)
    return lambda i: (mod_row0, 0, 0)


def _inproj(x2d, seq, g, mod, mod_row0, per_batch_mod, w_in_p, w_vt, dft_cc, dft_cs,
            cos, sin, gq, gkv, wq, wk, wv, tm, with_q):
    t, d = x2d.shape
    nb = t // seq
    tiles_per_seq = seq // tm
    tok = lambda w: pl.BlockSpec((tm, w), lambda i: (i, 0))
    z_spec = pl.BlockSpec((tm, F_WIDTH), lambda i: (i % tiles_per_seq, i // tiles_per_seq))
    pos = pl.BlockSpec((tm, LANES), lambda i: (i % tiles_per_seq, 0))
    heads_bf16 = jax.ShapeDtypeStruct((t, AP_WIDTH), BF16)
    shapes = [
        jax.ShapeDtypeStruct((seq, nb * F_WIDTH), BF16),
        jax.ShapeDtypeStruct((seq, nb * F_WIDTH), BF16),
        jax.ShapeDtypeStruct((t, 2 * MP_WIDTH), F32),
        jax.ShapeDtypeStruct((t // MCH, MP_WIDTH, MCH), BF16),
        jax.ShapeDtypeStruct((t, MP_WIDTH), F32),
        jax.ShapeDtypeStruct((t // MCH, 2 * N_STREAM, MCH), F32),
        heads_bf16, heads_bf16, heads_bf16,
    ]
    vt_spec = pl.BlockSpec((tm // MCH, MP_WIDTH, MCH), lambda i: (i, 0, 0))
    gr_spec = pl.BlockSpec((tm // MCH, 2 * N_STREAM, MCH), lambda i: (i, 0, 0))
    out_specs = [z_spec, z_spec, tok(2 * MP_WIDTH), vt_spec, tok(MP_WIDTH), gr_spec,
                 tok(AP_WIDTH), tok(AP_WIDTH), tok(AP_WIDTH)]
    return pl.pallas_call(
        functools.partial(_inproj_kernel, with_q=with_q),
        grid=(t // tm,),
        in_specs=[
            tok(d),
            _const_spec((1, d)),
            pl.BlockSpec((1, 6, d), _mod_map(mod_row0, per_batch_mod, tiles_per_seq)),
            _const_spec((d, IN_PAD)),
            _const_spec((MP_WIDTH + 2 * N_STREAM, d)),
            _const_spec((F_WIDTH, F_WIDTH)),
            _const_spec((F_WIDTH, F_WIDTH)),
            pos, pos,
            _const_spec((1, Q_LORA)), _const_spec((1, KV_LORA)),
            _const_spec((Q_LORA, AP_WIDTH)), _const_spec((KV_LORA, AP_WIDTH)), _const_spec((KV_LORA, AP_WIDTH)),
        ],
        out_specs=out_specs,
        out_shape=shapes,
        compiler_params=_cparams(("arbitrary",)),
        name="inproj",
    )(x2d, g, mod, w_in_p, w_vt, dft_cc, dft_cs, cos, sin, gq, gkv, wq, wk, wv)


def _fourier_kernel(c_ref, s_ref, zc_ref, zs_ref, o_ref):
    y = _dot(c_ref[...], zc_ref[...]) - _dot(s_ref[...], zs_ref[...])
    o_ref[...] = y.astype(BF16)


def _fourier(dft_c, dft_s, zc, zs):
    n, cols = zc.shape
    tr = min(n, 512)
    tc = min(cols, 512)
    return pl.pallas_call(
        _fourier_kernel,
        grid=(n // tr, cols // tc),
        in_specs=[
            pl.BlockSpec((tr, n), lambda i, j: (i, 0)),
            pl.BlockSpec((tr, n), lambda i, j: (i, 0)),
            pl.BlockSpec((n, tc), lambda i, j: (0, j)),
            pl.BlockSpec((n, tc), lambda i, j: (0, j)),
        ],
        out_specs=pl.BlockSpec((tr, tc), lambda i, j: (i, j)),
        out_shape=jax.ShapeDtypeStruct((n, cols), BF16),
        compiler_params=_cparams(("arbitrary", "arbitrary")),
        name="fourier",
    )(dft_c, dft_s, zc, zs)


def _fourier4_kernel(tab_ref, twc_ref, tws_ref, zc_ref, zs_ref, o_ref):
    m = zc_ref.shape[0] // 4
    reps = zc_ref.shape[1] // LANES
    k1 = pl.program_id(1)
    c0, c1, c2, c3 = (zc_ref[j * m:(j + 1) * m, :].astype(F32) for j in range(4))
    s0, s1, s2, s3 = (zs_ref[j * m:(j + 1) * m, :].astype(F32) for j in range(4))

    def emit(br, bi, k):
        if k:
            cos = jnp.concatenate([twc_ref[k - 1]] * reps, axis=1)
            sin = jnp.concatenate([tws_ref[k - 1]] * reps, axis=1)
            br, bi = br * cos + bi * sin, bi * cos - br * sin
        stacked = jnp.concatenate([br.astype(BF16), bi.astype(BF16)], axis=0)
        o_ref[...] = _dot(tab_ref[...], stacked).astype(BF16)

    @pl.when(k1 == 0)
    def _():
        emit((c0 + c2) + (c1 + c3), -((s0 + s2) + (s1 + s3)), 0)

    @pl.when(k1 == 1)
    def _():
        emit((c0 - c2) - (s1 - s3), -(s0 - s2) - (c1 - c3), 1)

    @pl.when(k1 == 2)
    def _():
        emit((c0 + c2) - (c1 + c3), (s1 + s3) - (s0 + s2), 2)

    @pl.when(k1 == 3)
    def _():
        emit((c0 - c2) + (s1 - s3), (c1 - c3) - (s0 - s2), 3)


def _fourier4(tables, zc, zs):
    tab, twc, tws = tables
    n, cols = zc.shape
    m = n // 4
    tc = min(cols, 512)
    nj = cols // tc
    out = pl.pallas_call(
        _fourier4_kernel,
        grid=(nj, 4),
        in_specs=[
            _const_spec((m, 2 * m)),
            _const_spec((3, m, LANES)),
            _const_spec((3, m, LANES)),
            pl.BlockSpec((n, tc), lambda j, k: (0, j)),
            pl.BlockSpec((n, tc), lambda j, k: (0, j)),
        ],
        out_specs=pl.BlockSpec((m, tc), lambda j, k: (0, k * nj + j)),
        out_shape=jax.ShapeDtypeStruct((m, 4 * cols), BF16),
        compiler_params=_cparams(("arbitrary", "arbitrary")),
        name="fourier4",
    )(tab, twc, tws, zc, zs)
    return out.reshape(n, cols)


def _fourier4_tables(n):
    m = n // 4
    idx = (np.arange(m, dtype=np.int64)[:, None] * np.arange(m, dtype=np.int64)[None, :]) % m
    ang = 2.0 * np.pi * idx.astype(np.float64) / m
    tab = np.concatenate([np.cos(ang), np.sin(ang)], axis=1) / np.sqrt(n)
    theta = 2.0 * np.pi * np.arange(m, dtype=np.float64)[None, :] * np.arange(1, 4, dtype=np.float64)[:, None] / n
    bcast = lambda t: jnp.asarray(np.repeat(t[:, :, None], LANES, axis=2), dtype=F32)
    return jnp.asarray(tab, dtype=F32).astype(BF16), bcast(np.cos(theta)), bcast(np.sin(theta))


CONV_ROWS = 256
CONV_HALO = 8


def _conv_kernel(u_ref, w_ref, q_ref, k_ref, pad_ref):
    n = u_ref.shape[1]
    width = u_ref.shape[2]
    zeros = jnp.zeros((CONV_HALO, width), F32)
    pad_ref[0:CONV_HALO, :] = zeros
    pad_ref[CONV_HALO + n:2 * CONV_HALO + n, :] = zeros
    pad_ref[CONV_HALO:CONV_HALO + n, :] = u_ref[0]
    w = w_ref[...]
    rows = min(CONV_ROWS, n)
    for r in range(n // rows):
        base = CONV_HALO + r * rows - K_CONV // 2
        acc = pad_ref[base:base + rows, :] * w[0:1, :]
        for j in range(1, K_CONV):
            acc = acc + pad_ref[base + j:base + j + rows, :] * w[j:j + 1, :]
        act = acc * jax.nn.sigmoid(acc)
        q_ref[0, r * rows:(r + 1) * rows, :] = (act[:, :MP_WIDTH] * (M_HEAD_DIM ** -0.5)).astype(BF16)
        k_ref[0, r * rows:(r + 1) * rows, :] = act[:, MP_WIDTH:].astype(BF16)


def _conv_silu(mqk, conv_w_p):
    nb, n, width = mqk.shape
    return pl.pallas_call(
        _conv_kernel,
        grid=(nb,),
        in_specs=[
            pl.BlockSpec((1, n, width), lambda b: (b, 0, 0)),
            _const_spec((8, width)),
        ],
        out_specs=[
            pl.BlockSpec((1, n, MP_WIDTH), lambda b: (b, 0, 0)),
            pl.BlockSpec((1, n, MP_WIDTH), lambda b: (b, 0, 0)),
        ],
        out_shape=[
            jax.ShapeDtypeStruct((nb, n, MP_WIDTH), BF16),
            jax.ShapeDtypeStruct((nb, n, MP_WIDTH), BF16),
        ],
        scratch_shapes=[pltpu.VMEM((n + 2 * CONV_HALO, width), F32)],
        compiler_params=_cparams(("arbitrary",)),
        name="conv_silu",
    )(mqk, conv_w_p)


def _log_sigmoid(x):
    return jnp.minimum(x, 0.0) - jnp.log(1.0 + jnp.exp(-jnp.abs(x)))


def _exact_dot_01(a, tri_bf16, a_on_left):
    out = None
    for term in _split3(a):
        d = _dot(term, tri_bf16) if a_on_left else _dot(tri_bf16, term)
        out = d if out is None else out + d
    return out


N_STREAM = 2 * M_HEADS
ONE_ROW = M_HEAD_DIM
(F_R, F_B, F_CM, F_TOT, F_CML, F_MP, F_A, F_WI, F_ELD, F_DEC, F_WK, F_HI, F_MID, F_LO) = range(14)
N_FIELDS = 14
OUT_GROUP = 4
SEL_ROWS = 32


def _mlstm_kernel(ql_ref, kl_ref, vtl_ref, grl_ref,
                  qc_ref, kc_ref, vtc_ref, grc_ref,
                  bir_ref, bfr_ref, sel_ref,
                  hl_ref, hc_ref,
                  ct_st, st_sc, rows_sc):
    L = MCH
    nh = M_HEADS
    ncc = qc_ref.shape[1] // L
    ncl = ql_ref.shape[1] // L

    d0 = lax.broadcasted_iota(jnp.int32, (L, L), 0)
    d1 = lax.broadcasted_iota(jnp.int32, (L, L), 1)
    le = d0 <= d1
    ge = d0 >= d1
    tri_le = le.astype(BF16)
    fwd_rows = lax.broadcasted_iota(jnp.int32, (N_STREAM, L), 0) < nh
    feat = lax.broadcasted_iota(jnp.int32, (HEAD_PAD, L), 0)
    one_row = feat == ONE_ROW
    keep_rows = feat < M_HEAD_DIM

    ct_st[...] = jnp.zeros_like(ct_st)

    def field(f, slot0, nc):
        return rows_sc[f, slot0:slot0 + nc].reshape(nc * N_STREAM, L)

    def set_field(f, slot0, nc, val):
        rows_sc[f, slot0:slot0 + nc] = val.reshape(nc, N_STREAM, L)

    def gate_pass(gr_ref, nc, slot0):
        n8 = nc * N_STREAM
        fwd = (lax.broadcasted_iota(jnp.int32, (n8, L), 0) & (N_STREAM - 1)) < nh
        lane = lax.broadcasted_iota(jnp.int32, (n8, L), 1)
        gi = gr_ref[:, 0:N_STREAM, :].reshape(n8, L) + bir_ref[0:n8, :]
        f = _log_sigmoid(gr_ref[:, N_STREAM:2 * N_STREAM, :].reshape(n8, L) + bfr_ref[0:n8, :])
        pre = _exact_dot_01(f, tri_le, a_on_left=True)
        total = jnp.sum(f, axis=1, keepdims=True)
        b = jnp.where(fwd, pre, total - pre + f)
        r = gi - b
        cm = r
        sh = 1
        while sh < L:
            from_left = jnp.where(lane >= sh, pltpu.roll(cm, sh, 1), -jnp.inf)
            from_right = jnp.where(lane < L - sh, pltpu.roll(cm, L - sh, 1), -jnp.inf)
            cm = jnp.maximum(cm, jnp.where(fwd, from_left, from_right))
            sh *= 2
        set_field(F_R, slot0, nc, r)
        set_field(F_B, slot0, nc, b)
        set_field(F_CM, slot0, nc, cm)
        set_field(F_TOT, slot0, nc, jnp.broadcast_to(total, (n8, L)))
        set_field(F_CML, slot0, nc, jnp.broadcast_to(jnp.max(r, axis=1, keepdims=True), (n8, L)))

    def m_scan(nc, slot0, m0):
        def step(j, m):
            sf = slot0 + j
            sb = slot0 + nc - 1 - j
            tot = jnp.where(fwd_rows, rows_sc[F_TOT, sf], rows_sc[F_TOT, sb])
            cml = jnp.where(fwd_rows, rows_sc[F_CML, sf], rows_sc[F_CML, sb])
            rows_sc[F_MP, sf, 0:nh, :] = m[0:nh]
            rows_sc[F_MP, sb, nh:N_STREAM, :] = m[nh:N_STREAM]
            return tot + jnp.maximum(m, cml)

        return lax.fori_loop(0, nc, step, m0)

    def weight_pass(nc, slot0):
        mp, cm, b, r = (field(f, slot0, nc) for f in (F_MP, F_CM, F_B, F_R))
        big = jnp.maximum(mp, field(F_CML, slot0, nc))
        a = -jnp.maximum(mp, cm)
        set_field(F_A, slot0, nc, a)
        set_field(F_WI, slot0, nc, jnp.exp(mp + a))
        set_field(F_ELD, slot0, nc, jnp.exp(a - b))
        set_field(F_DEC, slot0, nc, jnp.exp(mp - big))
        set_field(F_WK, slot0, nc, jnp.exp(r - big))
        for f, term in zip((F_HI, F_MID, F_LO), _split3(r)):
            set_field(f, slot0, nc, term.astype(F32))

    def value_slab(vt_ref, c, hd):
        vt = vt_ref[c, hd * HEAD_PAD:(hd + 1) * HEAD_PAD, :].astype(F32)
        return jnp.where(one_row, 1.0, vt)

    def state_pass(k_ref, vt_ref, nc, slot0):
        def step(j, carry):
            uts = []
            for sidx in range(N_STREAM):
                hd = sidx % nh
                c = j if sidx < nh else nc - 1 - j
                wk = rows_sc[F_WK, slot0 + c, sidx:sidx + 1, :]
                kk = k_ref[0, pl.ds(pl.multiple_of(c * L, L), L), hd * HEAD_PAD:(hd + 1) * HEAD_PAD]
                uts.append(_dot((value_slab(vt_ref, c, hd) * wk).astype(BF16), kk))
            for sidx in range(N_STREAM):
                slot = slot0 + (j if sidx < nh else nc - 1 - j)
                prev = ct_st[sidx]
                st_sc[sidx, slot] = prev.astype(BF16)
                ct_st[sidx] = rows_sc[F_DEC, slot, sidx:sidx + 1, :] * prev + uts[sidx]
            return carry

        lax.fori_loop(0, nc, step, 0)

    def output_pass(q_ref, k_ref, vt_ref, out_ref, nc, slot0):
        head_lanes = [slice(hd * HEAD_PAD, (hd + 1) * HEAD_PAD) for hd in range(nh)]

        def independent_matmuls(c):
            rows = pl.ds(pl.multiple_of(c * L, L), L)
            slot = slot0 + c
            r3 = jnp.concatenate([rows_sc[F_HI, slot], rows_sc[F_MID, slot], rows_sc[F_LO, slot],
                                  jnp.zeros((N_STREAM, L), F32)], axis=0).astype(BF16)
            qs = [q_ref[0, rows, lanes] for lanes in head_lanes]
            s_ts = [_dot_nt(k_ref[0, rows, lanes], q) for lanes, q in zip(head_lanes, qs)]
            inters = [_dot_nt(st_sc[sidx, slot], qs[sidx % nh]) for sidx in range(N_STREAM)]
            r_ts = [_dot_tn(r3, sel_ref[sidx]) for sidx in range(N_STREAM)]
            return s_ts, inters, r_ts

        def finish(c, s_ts, inters, r_ts):
            rows = pl.ds(pl.multiple_of(c * L, L), L)
            slot = slot0 + c
            a_rows = rows_sc[F_A, slot]
            wi_rows = rows_sc[F_WI, slot]
            eld_rows = rows_sc[F_ELD, slot]
            p_ts = []
            for sidx in range(N_STREAM):
                one = slice(sidx, sidx + 1)
                dm = jnp.where(le if sidx < nh else ge, r_ts[sidx] + a_rows[one, :], -jnp.inf)
                p_ts.append((s_ts[sidx % nh] * jnp.exp(dm)).astype(BF16))
            vtas = [value_slab(vt_ref, c, hd).astype(BF16) for hd in range(nh)]
            z_ts = [_dot(vtas[sidx % nh], p_ts[sidx]) for sidx in range(N_STREAM)]
            for hd in range(nh):
                hsum = None
                for sidx in (hd, nh + hd):
                    one = slice(sidx, sidx + 1)
                    z_t = z_ts[sidx] + inters[sidx] * wi_rows[one, :]
                    den = z_t[ONE_ROW:ONE_ROW + 1, :]
                    h_t = z_t * (1.0 / jnp.maximum(jnp.abs(den), eld_rows[one, :]))
                    hsum = h_t if hsum is None else hsum + h_t
                out_ref[0, rows, head_lanes[hd]] = jnp.where(keep_rows, hsum, 0.0).T

        group = min(OUT_GROUP, nc)

        def step(g, carry):
            ahead = independent_matmuls(g * group)
            for u in range(group):
                cur = ahead
                if u + 1 < group:
                    ahead = independent_matmuls(g * group + u + 1)
                finish(g * group + u, *cur)
            return carry

        lax.fori_loop(0, nc // group, step, 0)

    gate_pass(grc_ref, ncc, 0)
    gate_pass(grl_ref, ncl, ncc)
    m1 = m_scan(ncc, 0, jnp.zeros((N_STREAM, L), F32))
    m_scan(ncl, ncc, m1)
    weight_pass(ncc, 0)
    weight_pass(ncl, ncc)
    state_pass(kc_ref, vtc_ref, ncc, 0)
    state_pass(kl_ref, vtl_ref, ncl, ncc)
    output_pass(qc_ref, kc_ref, vtc_ref, hc_ref, ncc, 0)
    output_pass(ql_ref, kl_ref, vtl_ref, hl_ref, ncl, ncc)


def _mlstm(lat, ctx, bir, bfr, sel):
    nb, n, _ = lat[0].shape
    nctx = ctx[0].shape[1]
    assert MCH == LANES == HEAD_PAD
    nct = (n + nctx) // MCH
    n8 = max(n, nctx) // MCH * N_STREAM

    def specs(rows):
        nc = rows // MCH
        seq = lambda w: pl.BlockSpec((1, rows, w), lambda b: (b, 0, 0))
        chunked = lambda rows_: pl.BlockSpec((nc, rows_, MCH), lambda b: (b, 0, 0))
        return [seq(MP_WIDTH), seq(MP_WIDTH), chunked(MP_WIDTH), chunked(2 * N_STREAM)]

    out_spec = lambda rows: pl.BlockSpec((1, rows, MP_WIDTH), lambda b: (b, 0, 0))
    return pl.pallas_call(
        _mlstm_kernel,
        grid=(nb,),
        in_specs=specs(n) + specs(nctx) + [
            _const_spec((n8, 1)), _const_spec((n8, 1)),
            _const_spec((N_STREAM, SEL_ROWS, MCH)),
        ],
        out_specs=[out_spec(n), out_spec(nctx)],
        out_shape=[
            jax.ShapeDtypeStruct((nb, n, MP_WIDTH), F32),
            jax.ShapeDtypeStruct((nb, nctx, MP_WIDTH), F32),
        ],
        scratch_shapes=[
            pltpu.VMEM((N_STREAM, HEAD_PAD, HEAD_PAD), F32),
            pltpu.VMEM((N_STREAM, nct, HEAD_PAD, HEAD_PAD), BF16),
            pltpu.VMEM((N_FIELDS, nct, N_STREAM, MCH), F32),
        ],
        compiler_params=_cparams(("arbitrary",)),
        name="mlstm",
    )(*lat, *ctx, jnp.tile(bir, (n8 // N_STREAM, 1)), jnp.tile(bfr, (n8 // N_STREAM, 1)), sel)


ATTN_SUB = 256


def _attn_kernel(*refs, n_sets):
    q_ref = refs[0]
    kv_refs = refs[1:1 + 2 * n_sets]
    o_ref = refs[1 + 2 * n_sets]
    sub = min(ATTN_SUB, q_ref.shape[1])
    n_sub = q_ref.shape[1] // sub

    def scores_of(t):
        q = q_ref[0, t * sub:(t + 1) * sub, :]
        return [_dot_nt(q, kv_refs[2 * i][0]) for i in range(n_sets)]

    nxt = scores_of(0)
    for t in range(n_sub):
        rows = slice(t * sub, (t + 1) * sub)
        scores = nxt
        if t + 1 < n_sub:
            nxt = scores_of(t + 1)
        m = None
        for s in scores:
            sm = jnp.max(s, axis=-1, keepdims=True)
            m = sm if m is None else jnp.maximum(m, sm)
        acc = None
        den = None
        for i, s in enumerate(scores):
            p = jnp.exp(s - m)
            l = jnp.sum(p, axis=-1, keepdims=True)
            o = _dot(p.astype(BF16), kv_refs[2 * i + 1][0])
            acc = o if acc is None else acc + o
            den = l if den is None else den + l
        o_ref[0, rows, :] = (acc / den).astype(BF16)


def _attention(q, key_sets, tq):
    nb, n, _ = q.shape
    n_sets = len(key_sets)
    in_specs = [pl.BlockSpec((1, tq, HEAD_PAD), lambda b, h, i: (b, i, h))]
    args = [q]
    for k, v in key_sets:
        nk = k.shape[1]
        spec = pl.BlockSpec((1, nk, HEAD_PAD), lambda b, h, i: (b, 0, h))
        in_specs += [spec, spec]
        args += [k, v]
    return pl.pallas_call(
        functools.partial(_attn_kernel, n_sets=n_sets),
        grid=(nb, A_HEADS, n // tq),
        in_specs=in_specs,
        out_specs=pl.BlockSpec((1, tq, HEAD_PAD), lambda b, h, i: (b, i, h)),
        out_shape=jax.ShapeDtypeStruct((nb, n, AP_WIDTH), BF16),
        compiler_params=_cparams(("arbitrary", "arbitrary", "arbitrary")),
        name="attention",
    )(*args)


def _out_mlp_kernel(x_ref, yf_ref, hm_ref, mo_ref, ya_ref, mod_ref,
                    gm_ref, g2_ref, gfin_ref, wof_ref, wom_ref, woa_ref, wup_ref, wdn_ref,
                    o_ref, *, final_norm):
    mod = mod_ref[0]
    ga1, sh2, sc2, ga2 = mod[2:3], mod[3:4], mod[4:5], mod[5:6]
    mix = _dot(yf_ref[...], wof_ref[...]) + _dot(ya_ref[...], woa_ref[...])
    gm = gm_ref[...]
    for hd in range(M_HEADS):
        lanes = slice(hd * HEAD_PAD, (hd + 1) * HEAD_PAD)
        hh = hm_ref[:, lanes]
        ms = jnp.sum(hh * hh, axis=-1, keepdims=True) * (1.0 / M_HEAD_DIM)
        ym = hh * lax.rsqrt(ms + EPS) * gm[:, lanes] * jax.nn.sigmoid(mo_ref[:, lanes])
        mix = mix + _dot(ym.astype(BF16), wom_ref[lanes, :])
    x1 = x_ref[...] + ga1 * mix
    h2 = (_rms(x1, g2_ref[...]) * (1.0 + sc2) + sh2).astype(BF16)
    hidden = wup_ref.shape[1]
    hc = 1024
    acc = None
    for c in range(hidden // hc):
        u = jnp.maximum(_dot(h2, wup_ref[:, c * hc:(c + 1) * hc]), 0.0)
        d = _dot((u * u).astype(BF16), wdn_ref[c * hc:(c + 1) * hc, :])
        acc = d if acc is None else acc + d
    x2 = x1 + ga2 * acc
    if final_norm:
        x2 = _rms(x2, gfin_ref[...])
    o_ref[...] = x2


def _out_mlp(x2d, seq, yf, hm, mo, ya, mod, mod_row0, per_batch_mod, gm, g2, gfin,
             wof, wom, woa, wup, wdn, tm, final_norm):
    t, d = x2d.shape
    tiles_per_seq = seq // tm
    tok = lambda w: pl.BlockSpec((tm, w), lambda i: (i, 0))
    yf_spec = pl.BlockSpec((tm, F_WIDTH), lambda i: (i % tiles_per_seq, i // tiles_per_seq))
    return pl.pallas_call(
        functools.partial(_out_mlp_kernel, final_norm=final_norm),
        grid=(t // tm,),
        in_specs=[
            tok(d), yf_spec, tok(MP_WIDTH), tok(MP_WIDTH), tok(AP_WIDTH),
            pl.BlockSpec((1, 6, d), _mod_map(mod_row0, per_batch_mod, tiles_per_seq)),
            _const_spec((1, MP_WIDTH)), _const_spec((1, d)), _const_spec((1, d)),
            _const_spec((F_WIDTH, d)), _const_spec((MP_WIDTH, d)), _const_spec((AP_WIDTH, d)),
            _const_spec((d, wup.shape[1])), _const_spec((wdn.shape[0], d)),
        ],
        out_specs=tok(d),
        out_shape=jax.ShapeDtypeStruct((t, d), F32),
        compiler_params=_cparams(("arbitrary",)),
        name="out_mlp",
    )(x2d, yf, hm, mo, ya, mod, gm, g2, gfin, wof, wom, woa, wup, wdn)


def _dft_tables(n):
    idx = (np.arange(n, dtype=np.int64)[:, None] * np.arange(n, dtype=np.int64)[None, :]) % n
    ang = 2.0 * np.pi * idx.astype(np.float64) / n
    scale = 1.0 / np.sqrt(n)
    return np.cos(ang) * scale, np.sin(ang) * scale


def _channel_dft():
    c, s = _dft_tables(F_GROUP_DIM)
    eye = np.eye(F_GROUPS)
    return (jnp.asarray(np.kron(eye, c), dtype=F32).astype(BF16),
            jnp.asarray(np.kron(eye, s), dtype=F32).astype(BF16))


def _position_dft(n):
    c, s = _dft_tables(n)
    return jnp.asarray(c, dtype=F32).astype(BF16), jnp.asarray(s, dtype=F32).astype(BF16)


def _rope_tables(n, rotate):
    cos = np.zeros((n, HEAD_PAD), np.float32)
    sin = np.zeros((n, HEAD_PAD), np.float32)
    cos[:, :A_NOPE + A_ROPE] = 1.0
    if rotate:
        nf = A_ROPE // 4
        t = np.arange(n)
        row = (t // GRID_W).astype(np.float32)
        col = (t % GRID_W).astype(np.float32)
        freqs = (np.float32(ROPE_THETA) ** (-np.arange(nf, dtype=np.float32) / np.float32(nf))).astype(np.float32)
        for seg, pos in enumerate((row, col)):
            ang = pos[:, None] * freqs[None, :]
            c, s = np.cos(ang), np.sin(ang)
            base = A_NOPE + seg * 2 * nf
            cos[:, base:base + nf] = c
            cos[:, base + nf:base + 2 * nf] = c
            sin[:, base:base + nf] = -s
            sin[:, base + nf:base + 2 * nf] = s
    return jnp.asarray(cos), jnp.asarray(sin)


def _pad_heads_cols(w, heads, width):
    lead = w.shape[:-1]
    w = w.reshape(lead + (heads, width))
    w = jnp.pad(w, [(0, 0)] * len(lead) + [(0, 0), (0, HEAD_PAD - width)])
    return w.reshape(lead + (heads * HEAD_PAD,))


def _pad_cols(w, width):
    return jnp.pad(w, [(0, 0)] * (w.ndim - 1) + [(0, width - w.shape[-1])])


def _layer_weights(l, w_in, conv_qk, b_gates, g_mlstm, w_uq, w_ukv, w_out):
    wl = w_in[l]
    offs = np.cumsum([0, F_WIDTH, M_WIDTH, M_WIDTH, M_WIDTH, M_WIDTH, 4 * M_HEADS, Q_LORA, KV_LORA, A_ROPE])
    part = lambda i: wl[:, offs[i]:offs[i + 1]]
    place_rope = lambda w: jnp.pad(w, [(0, 0), (A_NOPE, LANES - A_NOPE - A_ROPE)])
    w_in_p = jnp.concatenate([
        part(0),
        _pad_heads_cols(part(1), M_HEADS, M_HEAD_DIM),
        _pad_heads_cols(part(2), M_HEADS, M_HEAD_DIM),
        _pad_heads_cols(part(4), M_HEADS, M_HEAD_DIM),
        part(6),
        part(7),
        place_rope(part(8)),
    ], axis=1).astype(BF16)
    assert w_in_p.shape[1] == IN_PAD
    w_vt = jnp.concatenate([_pad_heads_cols(part(3), M_HEADS, M_HEAD_DIM),
                            part(5)[:, GATE_I_COLS], part(5)[:, GATE_F_COLS]], axis=1).T.astype(BF16)

    conv = conv_qk[l]
    conv_p = jnp.concatenate([
        _pad_heads_cols(conv[:, :M_WIDTH], M_HEADS, M_HEAD_DIM),
        _pad_heads_cols(conv[:, M_WIDTH:], M_HEADS, M_HEAD_DIM),
    ], axis=1)
    conv_p = jnp.pad(conv_p, [(0, 8 - K_CONV), (0, 0)])

    bg = b_gates[l]
    bi, bf = bg[GATE_I_COLS], bg[GATE_F_COLS]
    gate_bias = dict(bir=bi[:, None], bfr=bf[:, None])

    gm = _pad_heads_cols(g_mlstm[l][None, :], M_HEADS, M_HEAD_DIM)

    wq = _pad_heads_cols(w_uq[l], A_HEADS, A_NOPE + A_ROPE).astype(BF16)
    ukv = w_ukv[l].reshape(KV_LORA, A_HEADS, A_NOPE + A_V)
    wk = jnp.pad(ukv[..., :A_NOPE], [(0, 0), (0, 0), (0, HEAD_PAD - A_NOPE)]).reshape(KV_LORA, AP_WIDTH).astype(BF16)
    wv = jnp.pad(ukv[..., A_NOPE:], [(0, 0), (0, 0), (0, HEAD_PAD - A_V)]).reshape(KV_LORA, AP_WIDTH).astype(BF16)

    wo = w_out[l]
    pad_rows = lambda w, heads, width: jnp.pad(
        w.reshape(heads, width, -1), [(0, 0), (0, HEAD_PAD - width), (0, 0)]).reshape(heads * HEAD_PAD, -1)
    wof = wo[:F_WIDTH].astype(BF16)
    wom = pad_rows(wo[F_WIDTH:F_WIDTH + M_WIDTH], M_HEADS, M_HEAD_DIM).astype(BF16)
    woa = pad_rows(wo[F_WIDTH + M_WIDTH:], A_HEADS, A_V).astype(BF16)
    return dict(w_in_p=w_in_p, w_vt=w_vt, conv_p=conv_p, gate_bias=gate_bias, gm=gm,
                wq=wq, wk=wk, wv=wv, wof=wof, wom=wom, woa=woa)


GATE_I_COLS = np.concatenate([np.arange(M_HEADS), 2 * M_HEADS + np.arange(M_HEADS)])
GATE_F_COLS = GATE_I_COLS + M_HEADS


def _stream_selectors():
    sel = np.zeros((N_STREAM, SEL_ROWS, MCH), np.float32)
    for s in range(N_STREAM):
        for part in range(3):
            sel[s, part * N_STREAM + s, :] = 1.0
    return jnp.asarray(sel, dtype=BF16)


def kernel(x, c, ctx, c_ctx, w_mod, b_mod, g_norm1, g_norm2, w_in, b_gates, conv_qk, g_mlstm,
           g_q_norm, g_kv_norm, w_uq, w_ukv, w_out, w_up, w_down, g_final):
    nb, seq, d = x.shape
    nctx = ctx.shape[1]
    depth = w_mod.shape[0]
    assert d == D_MODEL and seq % 256 == 0 and nctx % MCH == 0
    sel = _stream_selectors()

    tm = 256
    tm_ctx = min(256, nctx)
    tq = min(8 * ATTN_SUB, seq)
    tq_ctx = min(ATTN_SUB, nctx)

    dft_cc, dft_cs = _channel_dft()
    fft_lat = _fourier4_tables(seq)
    dft_ctx = _position_dft(nctx)
    rope_lat = _rope_tables(seq, True)
    rope_ctx = _rope_tables(nctx, False)

    rows = ((nb + 1 + 7) // 8) * 8
    cc = jnp.concatenate([c, c_ctx[None, :], jnp.zeros((rows - nb - 1, d), F32)], axis=0)
    mod_all = _modulation(cc, w_mod, b_mod).reshape(depth * rows, 6, d)

    xl = x.reshape(nb * seq, d)
    xc = ctx.reshape(nb * nctx, d)
    row = lambda v: v.reshape(1, -1)

    for l in range(depth):
        last = l == depth - 1
        lw = _layer_weights(l, w_in, conv_qk, b_gates, g_mlstm, w_uq, w_ukv, w_out)
        wup = w_up[l].astype(BF16)
        wdn = w_down[l].astype(BF16)
        row_lat, row_ctx = l * rows, l * rows + nb

        def tokenwise(xt, n, mod_row0, per_batch, rope, tile, with_q):
            return _inproj(xt, n, row(g_norm1[l]), mod_all, mod_row0, per_batch,
                           lw["w_in_p"], lw["w_vt"], dft_cc, dft_cs, *rope,
                           row(g_q_norm[l]), row(g_kv_norm[l]), lw["wq"], lw["wk"], lw["wv"], tile, with_q)

        zc, zs, mqk, vt, mo, gr, q_a, k_a, v_a = tokenwise(xl, seq, row_lat, True, rope_lat, tm, True)
        zc_c, zs_c, mqk_c, vt_c, mo_c, gr_c, q_ac, k_ac, v_ac = tokenwise(
            xc, nctx, row_ctx, False, rope_ctx, tm_ctx, not last)

        yf = _fourier4(fft_lat, zc, zs)

        def mlstm_inputs(mqk_s, vt_s, gr_s, n):
            q_s, k_s = _conv_silu(mqk_s.reshape(nb, n, 2 * MP_WIDTH), lw["conv_p"])
            return (q_s, k_s, vt_s, gr_s)

        hm, hm_c = _mlstm(mlstm_inputs(mqk, vt, gr, seq), mlstm_inputs(mqk_c, vt_c, gr_c, nctx),
                          sel=sel, **lw["gate_bias"])

        b3 = lambda a, n: a.reshape(nb, n, AP_WIDTH)
        keys_ctx = (b3(k_ac, nctx), b3(v_ac, nctx))
        ya = _attention(b3(q_a, seq), [(b3(k_a, seq), b3(v_a, seq)), keys_ctx], tq)

        mlp = functools.partial(
            _out_mlp, gm=lw["gm"], g2=row(g_norm2[l]), gfin=row(g_final),
            wof=lw["wof"], wom=lw["wom"], woa=lw["woa"], wup=wup, wdn=wdn)
        xl = mlp(xl, seq, yf, hm.reshape(nb * seq, MP_WIDTH), mo, ya.reshape(nb * seq, AP_WIDTH),
                 mod_all, row_lat, True, tm=tm, final_norm=last)

        if not last:
            yf_c = _fourier(*dft_ctx, zc_c, zs_c)
            ya_c = _attention(b3(q_ac, nctx), [keys_ctx], tq_ctx)
            xc = mlp(xc, nctx, yf_c, hm_c.reshape(nb * nctx, MP_WIDTH), mo_c,
                     ya_c.reshape(nb * nctx, AP_WIDTH), mod_all, row_ctx, False, tm=tm_ctx, final_norm=False)

    return xl.reshape(nb, seq, d)
```

```python
import functools

import numpy as np
import jax
import jax.numpy as jnp
from jax import lax
from jax.experimental import pallas as pl
from jax.experimental.pallas import tpu as pltpu

D_MODEL = 1024
GRID_W = 64
EPS = 1e-6
F_GROUPS = 4
F_GROUP_DIM = D_MODEL // 16
F_WIDTH = F_GROUPS * F_GROUP_DIM
M_HEADS = 4
M_HEAD_DIM = 3 * D_MODEL // 32
M_WIDTH = M_HEADS * M_HEAD_DIM
M_CHUNK = 64
K_CONV = 5
A_HEADS = 4
A_NOPE = 64
A_ROPE = 32
A_V = 3 * D_MODEL // 32
Q_LORA = D_MODEL // 4
KV_LORA = D_MODEL // 8
ROPE_THETA = 10000.0
MLP_HIDDEN = 4 * D_MODEL

LANES = 128
HEAD_PAD = 128
MP_WIDTH = M_HEADS * HEAD_PAD
AP_WIDTH = A_HEADS * HEAD_PAD
VMEM_LIMIT = 56 * 1024 * 1024
MCH = 128

OFF_PF = 0
OFF_MQ = OFF_PF + F_WIDTH
OFF_MK = OFF_MQ + MP_WIDTH
OFF_MO = OFF_MK + MP_WIDTH
OFF_CQ = OFF_MO + MP_WIDTH
OFF_CKV = OFF_CQ + Q_LORA
OFF_KR = OFF_CKV + KV_LORA
IN_PAD = OFF_KR + LANES

BF16 = jnp.bfloat16
F32 = jnp.float32


def _cparams(sem):
    return pltpu.CompilerParams(dimension_semantics=sem, vmem_limit_bytes=VMEM_LIMIT)


def _const_spec(shape):
    nd = len(shape)
    return pl.BlockSpec(shape, lambda *_: (0,) * nd, pipeline_mode=pl.Buffered(1))


def _split3(a):
    hi = a.astype(BF16)
    r1 = a - hi.astype(F32)
    mid = r1.astype(BF16)
    lo = (r1 - mid.astype(F32)).astype(BF16)
    return hi, mid, lo


def _dot(a, b):
    return jnp.dot(a, b, preferred_element_type=F32)


def _dot_nt(a, b):
    return lax.dot_general(a, b, (((1,), (1,)), ((), ())), preferred_element_type=F32)


def _dot_tn(a, b):
    return lax.dot_general(a, b, (((0,), (0,)), ((), ())), preferred_element_type=F32)


def _rms(x, g):
    return x * lax.rsqrt(jnp.mean(x * x, axis=-1, keepdims=True) + EPS) * g


def _mod_kernel(c_ref, w_ref, b_ref, o_ref):
    c = c_ref[...]
    a = c * jax.nn.sigmoid(c)
    a_hi = a.astype(BF16)
    a_lo = (a - a_hi.astype(F32)).astype(BF16)
    w = w_ref[0]
    w_hi = w.astype(BF16)
    w_lo = (w - w_hi.astype(F32)).astype(BF16)
    acc = _dot(a_hi, w_hi) + _dot(a_hi, w_lo) + _dot(a_lo, w_hi)
    o_ref[0] = acc + b_ref[0]


def _modulation(cc, w_mod, b_mod):
    depth, d, n = w_mod.shape
    rows = cc.shape[0]
    tn = 1536
    return pl.pallas_call(
        _mod_kernel,
        grid=(depth, n // tn),
        in_specs=[
            pl.BlockSpec((rows, d), lambda l, j: (0, 0)),
            pl.BlockSpec((1, d, tn), lambda l, j: (l, 0, j)),
            pl.BlockSpec((1, 1, tn), lambda l, j: (l, 0, j)),
        ],
        out_specs=pl.BlockSpec((1, rows, tn), lambda l, j: (l, 0, j)),
        out_shape=jax.ShapeDtypeStruct((depth, rows, n), F32),
        compiler_params=_cparams(("arbitrary", "arbitrary")),
        name="modulation",
    )(cc, w_mod, b_mod.reshape(depth, 1, n))


def _rope(x, cos, sin, first_half):
    half = A_ROPE // 4
    partner = jnp.where(first_half, pltpu.roll(x, LANES - half, 1), pltpu.roll(x, half, 1))
    return x * cos + partner * sin


def _inproj_kernel(x_ref, g_ref, mod_ref, w_ref, wvt_ref, cc_ref, cs_ref,
                   cos_ref, sin_ref, gq_ref, gkv_ref, wq_ref, wk_ref, wv_ref,
                   zc_ref, zs_ref, mqk_ref, vt_ref, mo_ref, gr_ref, qa_ref, ka_ref, va_ref, *, with_q):
    x = x_ref[...]
    mod = mod_ref[0]
    h = _rms(x, g_ref[...]) * (1.0 + mod[1:2]) + mod[0:1]
    hb = h.astype(BF16)

    def proj(off, width):
        return _dot(hb, w_ref[:, off:off + width])

    ckv_kr = proj(OFF_CKV, 2 * LANES)
    cq = proj(OFF_CQ, Q_LORA) if with_q else None
    pf = proj(OFF_PF, F_WIDTH).astype(BF16)

    mqk_ref[...] = proj(OFF_MQ, 2 * MP_WIDTH)

    cos = cos_ref[...]
    sin = sin_ref[...]
    lane = lax.broadcasted_iota(jnp.int32, cos.shape, 1)
    first_half = ((lane - A_NOPE) & (A_ROPE // 2 - 1)) < A_ROPE // 4
    kvn = _rms(ckv_kr[:, :KV_LORA], gkv_ref[...]).astype(BF16)
    k_rope = _rope(ckv_kr[:, KV_LORA:], cos, sin, first_half)
    for hd in range(A_HEADS):
        lanes = slice(hd * HEAD_PAD, (hd + 1) * HEAD_PAD)
        ka_ref[:, lanes] = (_dot(kvn, wk_ref[:, lanes]) + k_rope).astype(BF16)
        va_ref[:, lanes] = _dot(kvn, wv_ref[:, lanes]).astype(BF16)
    if with_q:
        qn = _rms(cq, gq_ref[...]).astype(BF16)
        q_raw = [_dot(qn, wq_ref[:, hd * HEAD_PAD:(hd + 1) * HEAD_PAD]) for hd in range(A_HEADS)]
    zc_ref[...] = _dot(pf, cc_ref[...]).astype(BF16)
    zs_ref[...] = _dot(pf, cs_ref[...]).astype(BF16)

    vg = _dot_nt(wvt_ref[...], hb)
    vt = vg[:MP_WIDTH].astype(BF16)
    for j in range(vt_ref.shape[0]):
        vt_ref[j] = vt[:, j * MCH:(j + 1) * MCH]
        gr_ref[j] = vg[MP_WIDTH:, j * MCH:(j + 1) * MCH]
    mo_ref[...] = proj(OFF_MO, MP_WIDTH)

    if with_q:
        scale = (A_NOPE + A_ROPE) ** -0.5
        for hd in range(A_HEADS):
            q = _rope(q_raw[hd], cos, sin, first_half)
            qa_ref[:, hd * HEAD_PAD:(hd + 1) * HEAD_PAD] = (q * scale).astype(BF16)
    else:
        qa_ref[...] = jnp.zeros_like(qa_ref)


def _mod_map(mod_row0, per_batch_mod, tiles_per_seq):
    if per_batch_mod:
        return lambda i: (mod_row0 + i // tiles_per_seq, 0, 0)
    return lambda i: (mod_row0, 0, 0)


def _inproj(x2d, seq, g, mod, mod_row0, per_batch_mod, w_in_p, w_vt, dft_cc, dft_cs,
            cos, sin, gq, gkv, wq, wk, wv, tm, with_q):
    t, d = x2d.shape
    nb = t // seq
    tiles_per_seq = seq // tm
    tok = lambda w: pl.BlockSpec((tm, w), lambda i: (i, 0))
    z_spec = pl.BlockSpec((tm, F_WIDTH), lambda i: (i % tiles_per_seq, i // tiles_per_seq))
    pos = pl.BlockSpec((tm, LANES), lambda i: (i % tiles_per_seq, 0))
    heads_bf16 = jax.ShapeDtypeStruct((t, AP_WIDTH), BF16)
    shapes = [
        jax.ShapeDtypeStruct((seq, nb * F_WIDTH), BF16),
        jax.ShapeDtypeStruct((seq, nb * F_WIDTH), BF16),
        jax.ShapeDtypeStruct((t, 2 * MP_WIDTH), F32),
        jax.ShapeDtypeStruct((t // MCH, MP_WIDTH, MCH), BF16),
        jax.ShapeDtypeStruct((t, MP_WIDTH), F32),
        jax.ShapeDtypeStruct((t // MCH, 2 * N_STREAM, MCH), F32),
        heads_bf16, heads_bf16, heads_bf16,
    ]
    vt_spec = pl.BlockSpec((tm // MCH, MP_WIDTH, MCH), lambda i: (i, 0, 0))
    gr_spec = pl.BlockSpec((tm // MCH, 2 * N_STREAM, MCH), lambda i: (i, 0, 0))
    out_specs = [z_spec, z_spec, tok(2 * MP_WIDTH), vt_spec, tok(MP_WIDTH), gr_spec,
                 tok(AP_WIDTH), tok(AP_WIDTH), tok(AP_WIDTH)]
    return pl.pallas_call(
        functools.partial(_inproj_kernel, with_q=with_q),
        grid=(t // tm,),
        in_specs=[
            tok(d),
            _const_spec((1, d)),
            pl.BlockSpec((1, 6, d), _mod_map(mod_row0, per_batch_mod, tiles_per_seq)),
            _const_spec((d, IN_PAD)),
            _const_spec((MP_WIDTH + 2 * N_STREAM, d)),
            _const_spec((F_WIDTH, F_WIDTH)),
            _const_spec((F_WIDTH, F_WIDTH)),
            pos, pos,
            _const_spec((1, Q_LORA)), _const_spec((1, KV_LORA)),
            _const_spec((Q_LORA, AP_WIDTH)), _const_spec((KV_LORA, AP_WIDTH)), _const_spec((KV_LORA, AP_WIDTH)),
        ],
        out_specs=out_specs,
        out_shape=shapes,
        compiler_params=_cparams(("arbitrary",)),
        name="inproj",
    )(x2d, g, mod, w_in_p, w_vt, dft_cc, dft_cs, cos, sin, gq, gkv, wq, wk, wv)


def _fourier_kernel(c_ref, s_ref, zc_ref, zs_ref, o_ref):
    y = _dot(c_ref[...], zc_ref[...]) - _dot(s_ref[...], zs_ref[...])
    for slab in range(o_ref.shape[0]):
        o_ref[slab] = y[:, slab * LANES:(slab + 1) * LANES]


def _fourier(dft_c, dft_s, zc, zs):
    n, cols = zc.shape
    tr = min(n, 512)
    tc = min(cols, 512)
    return pl.pallas_call(
        _fourier_kernel,
        grid=(n // tr, cols // tc),
        in_specs=[
            pl.BlockSpec((tr, n), lambda i, j: (i, 0)),
            pl.BlockSpec((tr, n), lambda i, j: (i, 0)),
            pl.BlockSpec((n, tc), lambda i, j: (0, j)),
            pl.BlockSpec((n, tc), lambda i, j: (0, j)),
        ],
        out_specs=pl.BlockSpec((tc // LANES, tr, LANES), lambda i, j: (j, i, 0)),
        out_shape=jax.ShapeDtypeStruct((cols // LANES, n, LANES), F32),
        compiler_params=_cparams(("arbitrary", "arbitrary")),
        name="fourier",
    )(dft_c, dft_s, zc, zs)


def _fourier4_kernel(tab_ref, twc_ref, tws_ref, zc_ref, zs_ref, o_ref):
    m = zc_ref.shape[0] // 4
    reps = zc_ref.shape[1] // LANES
    k1 = pl.program_id(1)
    c0, c1, c2, c3 = (zc_ref[j * m:(j + 1) * m, :].astype(F32) for j in range(4))
    s0, s1, s2, s3 = (zs_ref[j * m:(j + 1) * m, :].astype(F32) for j in range(4))

    def emit(br, bi, k):
        if k:
            cos = jnp.concatenate([twc_ref[k - 1]] * reps, axis=1)
            sin = jnp.concatenate([tws_ref[k - 1]] * reps, axis=1)
            br, bi = br * cos + bi * sin, bi * cos - br * sin
        stacked = jnp.concatenate([br.astype(BF16), bi.astype(BF16)], axis=0)
        y = _dot(tab_ref[...], stacked)
        for slab in range(reps):
            o_ref[slab, pl.ds(k, m, stride=4), :] = y[:, slab * LANES:(slab + 1) * LANES]

    @pl.when(k1 == 0)
    def _():
        emit((c0 + c2) + (c1 + c3), -((s0 + s2) + (s1 + s3)), 0)

    @pl.when(k1 == 1)
    def _():
        emit((c0 - c2) - (s1 - s3), -(s0 - s2) - (c1 - c3), 1)

    @pl.when(k1 == 2)
    def _():
        emit((c0 + c2) - (c1 + c3), (s1 + s3) - (s0 + s2), 2)

    @pl.when(k1 == 3)
    def _():
        emit((c0 - c2) + (s1 - s3), (c1 - c3) - (s0 - s2), 3)


def _fourier4(tables, zc, zs):
    tab, twc, tws = tables
    n, cols = zc.shape
    m = n // 4
    tc = min(cols, 512)
    nj = cols // tc
    out = pl.pallas_call(
        _fourier4_kernel,
        grid=(nj, 4),
        in_specs=[
            _const_spec((m, 2 * m)),
            _const_spec((3, m, LANES)),
            _const_spec((3, m, LANES)),
            pl.BlockSpec((n, tc), lambda j, k: (0, j)),
            pl.BlockSpec((n, tc), lambda j, k: (0, j)),
        ],
        out_specs=pl.BlockSpec((tc // LANES, n, LANES), lambda j, k: (j, 0, 0)),
        out_shape=jax.ShapeDtypeStruct((cols // LANES, n, LANES), F32),
        compiler_params=_cparams(("arbitrary", "arbitrary")),
        name="fourier4",
    )(tab, twc, tws, zc, zs)
    return out


def _fourier4_tables(n):
    m = n // 4
    idx = (np.arange(m, dtype=np.int64)[:, None] * np.arange(m, dtype=np.int64)[None, :]) % m
    ang = 2.0 * np.pi * idx.astype(np.float64) / m
    tab = np.concatenate([np.cos(ang), np.sin(ang)], axis=1) / np.sqrt(n)
    theta = 2.0 * np.pi * np.arange(m, dtype=np.float64)[None, :] * np.arange(1, 4, dtype=np.float64)[:, None] / n
    bcast = lambda t: jnp.asarray(np.repeat(t[:, :, None], LANES, axis=2), dtype=F32)
    return jnp.asarray(tab, dtype=F32).astype(BF16), bcast(np.cos(theta)), bcast(np.sin(theta))


CONV_ROWS = 256
CONV_HALO = 8


def _conv_kernel(u_ref, w_ref, q_ref, k_ref, pad_ref):
    n = u_ref.shape[1]
    width = u_ref.shape[2]
    zeros = jnp.zeros((CONV_HALO, width), F32)
    pad_ref[0:CONV_HALO, :] = zeros
    pad_ref[CONV_HALO + n:2 * CONV_HALO + n, :] = zeros
    pad_ref[CONV_HALO:CONV_HALO + n, :] = u_ref[0]
    w = w_ref[...]
    rows = min(CONV_ROWS, n)
    for r in range(n // rows):
        base = CONV_HALO + r * rows - K_CONV // 2
        acc = pad_ref[base:base + rows, :] * w[0:1, :]
        for j in range(1, K_CONV):
            acc = acc + pad_ref[base + j:base + j + rows, :] * w[j:j + 1, :]
        act = acc * jax.nn.sigmoid(acc)
        q_ref[0, r * rows:(r + 1) * rows, :] = (act[:, :MP_WIDTH] * (M_HEAD_DIM ** -0.5)).astype(BF16)
        k_ref[0, r * rows:(r + 1) * rows, :] = act[:, MP_WIDTH:].astype(BF16)


def _conv_silu(mqk, conv_w_p):
    nb, n, width = mqk.shape
    return pl.pallas_call(
        _conv_kernel,
        grid=(nb,),
        in_specs=[
            pl.BlockSpec((1, n, width), lambda b: (b, 0, 0)),
            _const_spec((8, width)),
        ],
        out_specs=[
            pl.BlockSpec((1, n, MP_WIDTH), lambda b: (b, 0, 0)),
            pl.BlockSpec((1, n, MP_WIDTH), lambda b: (b, 0, 0)),
        ],
        out_shape=[
            jax.ShapeDtypeStruct((nb, n, MP_WIDTH), BF16),
            jax.ShapeDtypeStruct((nb, n, MP_WIDTH), BF16),
        ],
        scratch_shapes=[pltpu.VMEM((n + 2 * CONV_HALO, width), F32)],
        compiler_params=_cparams(("arbitrary",)),
        name="conv_silu",
    )(mqk, conv_w_p)


def _log_sigmoid(x):
    return jnp.minimum(x, 0.0) - jnp.log(1.0 + jnp.exp(-jnp.abs(x)))


def _exact_dot_01(a, tri_bf16, a_on_left):
    out = None
    for term in _split3(a):
        d = _dot(term, tri_bf16) if a_on_left else _dot(tri_bf16, term)
        out = d if out is None else out + d
    return out


N_STREAM = 2 * M_HEADS
ONE_ROW = M_HEAD_DIM
(F_R, F_B, F_CM, F_TOT, F_CML, F_MP, F_A, F_WI, F_ELD, F_DEC, F_WK, F_HI, F_MID, F_LO) = range(14)
N_FIELDS = 14
OUT_GROUP = 4
SEL_ROWS = 32


def _mlstm_kernel(ql_ref, kl_ref, vtl_ref, grl_ref,
                  qc_ref, kc_ref, vtc_ref, grc_ref,
                  bir_ref, bfr_ref, sel_ref,
                  hl_ref, hc_ref,
                  ct_st, st_sc, rows_sc):
    L = MCH
    nh = M_HEADS
    ncc = qc_ref.shape[1] // L
    ncl = ql_ref.shape[1] // L

    d0 = lax.broadcasted_iota(jnp.int32, (L, L), 0)
    d1 = lax.broadcasted_iota(jnp.int32, (L, L), 1)
    le = d0 <= d1
    ge = d0 >= d1
    tri_le = le.astype(BF16)
    fwd_rows = lax.broadcasted_iota(jnp.int32, (N_STREAM, L), 0) < nh
    feat = lax.broadcasted_iota(jnp.int32, (HEAD_PAD, L), 0)
    one_row = feat == ONE_ROW
    keep_rows = feat < M_HEAD_DIM

    ct_st[...] = jnp.zeros_like(ct_st)

    def field(f, slot0, nc):
        return rows_sc[f, slot0:slot0 + nc].reshape(nc * N_STREAM, L)

    def set_field(f, slot0, nc, val):
        rows_sc[f, slot0:slot0 + nc] = val.reshape(nc, N_STREAM, L)

    def gate_pass(gr_ref, nc, slot0):
        n8 = nc * N_STREAM
        fwd = (lax.broadcasted_iota(jnp.int32, (n8, L), 0) & (N_STREAM - 1)) < nh
        lane = lax.broadcasted_iota(jnp.int32, (n8, L), 1)
        gi = gr_ref[:, 0:N_STREAM, :].reshape(n8, L) + bir_ref[0:n8, :]
        f = _log_sigmoid(gr_ref[:, N_STREAM:2 * N_STREAM, :].reshape(n8, L) + bfr_ref[0:n8, :])
        pre = _exact_dot_01(f, tri_le, a_on_left=True)
        total = jnp.sum(f, axis=1, keepdims=True)
        b = jnp.where(fwd, pre, total - pre + f)
        r = gi - b
        cm = r
        sh = 1
        while sh < L:
            from_left = jnp.where(lane >= sh, pltpu.roll(cm, sh, 1), -jnp.inf)
            from_right = jnp.where(lane < L - sh, pltpu.roll(cm, L - sh, 1), -jnp.inf)
            cm = jnp.maximum(cm, jnp.where(fwd, from_left, from_right))
            sh *= 2
        set_field(F_R, slot0, nc, r)
        set_field(F_B, slot0, nc, b)
        set_field(F_CM, slot0, nc, cm)
        set_field(F_TOT, slot0, nc, jnp.broadcast_to(total, (n8, L)))
        set_field(F_CML, slot0, nc, jnp.broadcast_to(jnp.max(r, axis=1, keepdims=True), (n8, L)))

    def m_scan(nc, slot0, m0):
        def step(j, m):
            sf = slot0 + j
            sb = slot0 + nc - 1 - j
            tot = jnp.where(fwd_rows, rows_sc[F_TOT, sf], rows_sc[F_TOT, sb])
            cml = jnp.where(fwd_rows, rows_sc[F_CML, sf], rows_sc[F_CML, sb])
            rows_sc[F_MP, sf, 0:nh, :] = m[0:nh]
            rows_sc[F_MP, sb, nh:N_STREAM, :] = m[nh:N_STREAM]
            return tot + jnp.maximum(m, cml)

        return lax.fori_loop(0, nc, step, m0)

    def weight_pass(nc, slot0):
        mp, cm, b, r = (field(f, slot0, nc) for f in (F_MP, F_CM, F_B, F_R))
        big = jnp.maximum(mp, field(F_CML, slot0, nc))
        a = -jnp.maximum(mp, cm)
        set_field(F_A, slot0, nc, a)
        set_field(F_WI, slot0, nc, jnp.exp(mp + a))
        set_field(F_ELD, slot0, nc, jnp.exp(a - b))
        set_field(F_DEC, slot0, nc, jnp.exp(mp - big))
        set_field(F_WK, slot0, nc, jnp.exp(r - big))
        for f, term in zip((F_HI, F_MID, F_LO), _split3(r)):
            set_field(f, slot0, nc, term.astype(F32))

    def value_slab(vt_ref, c, hd):
        vt = vt_ref[c, hd * HEAD_PAD:(hd + 1) * HEAD_PAD, :].astype(F32)
        return jnp.where(one_row, 1.0, vt)

    def state_pass(k_ref, vt_ref, nc, slot0):
        def step(j, carry):
            uts = []
            for sidx in range(N_STREAM):
                hd = sidx % nh
                c = j if sidx < nh else nc - 1 - j
                wk = rows_sc[F_WK, slot0 + c, sidx:sidx + 1, :]
                kk = k_ref[0, pl.ds(pl.multiple_of(c * L, L), L), hd * HEAD_PAD:(hd + 1) * HEAD_PAD]
                uts.append(_dot((value_slab(vt_ref, c, hd) * wk).astype(BF16), kk))
            for sidx in range(N_STREAM):
                slot = slot0 + (j if sidx < nh else nc - 1 - j)
                prev = ct_st[sidx]
                st_sc[sidx, slot] = prev.astype(BF16)
                ct_st[sidx] = rows_sc[F_DEC, slot, sidx:sidx + 1, :] * prev + uts[sidx]
            return carry

        lax.fori_loop(0, nc, step, 0)

    def output_pass(q_ref, k_ref, vt_ref, out_ref, nc, slot0):
        head_lanes = [slice(hd * HEAD_PAD, (hd + 1) * HEAD_PAD) for hd in range(nh)]

        def independent_matmuls(c):
            rows = pl.ds(pl.multiple_of(c * L, L), L)
            slot = slot0 + c
            r3 = jnp.concatenate([rows_sc[F_HI, slot], rows_sc[F_MID, slot], rows_sc[F_LO, slot],
                                  jnp.zeros((N_STREAM, L), F32)], axis=0).astype(BF16)
            qs = [q_ref[0, rows, lanes] for lanes in head_lanes]
            s_ts = [_dot_nt(k_ref[0, rows, lanes], q) for lanes, q in zip(head_lanes, qs)]
            inters = [_dot_nt(st_sc[sidx, slot], qs[sidx % nh]) for sidx in range(N_STREAM)]
            r_ts = [_dot_tn(r3, sel_ref[sidx]) for sidx in range(N_STREAM)]
            return s_ts, inters, r_ts

        def finish(c, s_ts, inters, r_ts):
            rows = pl.ds(pl.multiple_of(c * L, L), L)
            slot = slot0 + c
            a_rows = rows_sc[F_A, slot]
            wi_rows = rows_sc[F_WI, slot]
            eld_rows = rows_sc[F_ELD, slot]
            p_ts = []
            for sidx in range(N_STREAM):
                one = slice(sidx, sidx + 1)
                dm = jnp.where(le if sidx < nh else ge, r_ts[sidx] + a_rows[one, :], -jnp.inf)
                p_ts.append((s_ts[sidx % nh] * jnp.exp(dm)).astype(BF16))
            vtas = [value_slab(vt_ref, c, hd).astype(BF16) for hd in range(nh)]
            z_ts = [_dot(vtas[sidx % nh], p_ts[sidx]) for sidx in range(N_STREAM)]
            for hd in range(nh):
                hsum = None
                for sidx in (hd, nh + hd):
                    one = slice(sidx, sidx + 1)
                    z_t = z_ts[sidx] + inters[sidx] * wi_rows[one, :]
                    den = z_t[ONE_ROW:ONE_ROW + 1, :]
                    h_t = z_t * (1.0 / jnp.maximum(jnp.abs(den), eld_rows[one, :]))
                    hsum = h_t if hsum is None else hsum + h_t
                out_ref[0, rows, head_lanes[hd]] = jnp.where(keep_rows, hsum, 0.0).T

        group = min(OUT_GROUP, nc)

        def step(g, carry):
            ahead = independent_matmuls(g * group)
            for u in range(group):
                cur = ahead
                if u + 1 < group:
                    ahead = independent_matmuls(g * group + u + 1)
                finish(g * group + u, *cur)
            return carry

        lax.fori_loop(0, nc // group, step, 0)

    gate_pass(grc_ref, ncc, 0)
    gate_pass(grl_ref, ncl, ncc)
    m1 = m_scan(ncc, 0, jnp.zeros((N_STREAM, L), F32))
    m_scan(ncl, ncc, m1)
    weight_pass(ncc, 0)
    weight_pass(ncl, ncc)
    state_pass(kc_ref, vtc_ref, ncc, 0)
    state_pass(kl_ref, vtl_ref, ncl, ncc)
    output_pass(qc_ref, kc_ref, vtc_ref, hc_ref, ncc, 0)
    output_pass(ql_ref, kl_ref, vtl_ref, hl_ref, ncl, ncc)


def _mlstm(lat, ctx, bir, bfr, sel):
    nb, n, _ = lat[0].shape
    nctx = ctx[0].shape[1]
    assert MCH == LANES == HEAD_PAD
    nct = (n + nctx) // MCH
    n8 = max(n, nctx) // MCH * N_STREAM

    def specs(rows):
        nc = rows // MCH
        seq = lambda w: pl.BlockSpec((1, rows, w), lambda b: (b, 0, 0))
        chunked = lambda rows_: pl.BlockSpec((nc, rows_, MCH), lambda b: (b, 0, 0))
        return [seq(MP_WIDTH), seq(MP_WIDTH), chunked(MP_WIDTH), chunked(2 * N_STREAM)]

    out_spec = lambda rows: pl.BlockSpec((1, rows, MP_WIDTH), lambda b: (b, 0, 0))
    return pl.pallas_call(
        _mlstm_kernel,
        grid=(nb,),
        in_specs=specs(n) + specs(nctx) + [
            _const_spec((n8, 1)), _const_spec((n8, 1)),
            _const_spec((N_STREAM, SEL_ROWS, MCH)),
        ],
        out_specs=[out_spec(n), out_spec(nctx)],
        out_shape=[
            jax.ShapeDtypeStruct((nb, n, MP_WIDTH), F32),
            jax.ShapeDtypeStruct((nb, nctx, MP_WIDTH), F32),
        ],
        scratch_shapes=[
            pltpu.VMEM((N_STREAM, HEAD_PAD, HEAD_PAD), F32),
            pltpu.VMEM((N_STREAM, nct, HEAD_PAD, HEAD_PAD), BF16),
            pltpu.VMEM((N_FIELDS, nct, N_STREAM, MCH), F32),
        ],
        compiler_params=_cparams(("arbitrary",)),
        name="mlstm",
    )(*lat, *ctx, jnp.tile(bir, (n8 // N_STREAM, 1)), jnp.tile(bfr, (n8 // N_STREAM, 1)), sel)


ATTN_SUB = 256


def _attn_kernel(*refs, n_sets):
    q_ref = refs[0]
    kv_refs = refs[1:1 + 2 * n_sets]
    o_ref = refs[1 + 2 * n_sets]
    sub = min(ATTN_SUB, q_ref.shape[1])
    n_sub = q_ref.shape[1] // sub

    def scores_of(t):
        q = q_ref[0, t * sub:(t + 1) * sub, :]
        return [_dot_nt(q, kv_refs[2 * i][0]) for i in range(n_sets)]

    nxt = scores_of(0)
    for t in range(n_sub):
        rows = slice(t * sub, (t + 1) * sub)
        scores = nxt
        if t + 1 < n_sub:
            nxt = scores_of(t + 1)
        m = None
        for s in scores:
            sm = jnp.max(s, axis=-1, keepdims=True)
            m = sm if m is None else jnp.maximum(m, sm)
        acc = None
        den = None
        for i, s in enumerate(scores):
            p = jnp.exp(s - m)
            l = jnp.sum(p, axis=-1, keepdims=True)
            o = _dot(p.astype(BF16), kv_refs[2 * i + 1][0])
            acc = o if acc is None else acc + o
            den = l if den is None else den + l
        o_ref[0, rows, :] = (acc / den).astype(BF16)


def _attention(q, key_sets, tq):
    nb, n, _ = q.shape
    n_sets = len(key_sets)
    in_specs = [pl.BlockSpec((1, tq, HEAD_PAD), lambda b, h, i: (b, i, h))]
    args = [q]
    for k, v in key_sets:
        nk = k.shape[1]
        spec = pl.BlockSpec((1, nk, HEAD_PAD), lambda b, h, i: (b, 0, h))
        in_specs += [spec, spec]
        args += [k, v]
    return pl.pallas_call(
        functools.partial(_attn_kernel, n_sets=n_sets),
        grid=(nb, A_HEADS, n // tq),
        in_specs=in_specs,
        out_specs=pl.BlockSpec((1, tq, HEAD_PAD), lambda b, h, i: (b, i, h)),
        out_shape=jax.ShapeDtypeStruct((nb, n, AP_WIDTH), BF16),
        compiler_params=_cparams(("arbitrary", "arbitrary", "arbitrary")),
        name="attention",
    )(*args)


def _out_mlp_kernel(x_ref, yf_ref, hm_ref, mo_ref, ya_ref, mod_ref,
                    gm_ref, g2_ref, gfin_ref, wof_ref, wom_ref, woa_ref, wup_ref, wdn_ref,
                    o_ref, *, final_norm):
    mod = mod_ref[0]
    ga1, sh2, sc2, ga2 = mod[2:3], mod[3:4], mod[4:5], mod[5:6]
    mix = _dot(ya_ref[...], woa_ref[...])
    for slab in range(yf_ref.shape[0]):
        mix = mix + _dot(yf_ref[slab].astype(BF16), wof_ref[slab * LANES:(slab + 1) * LANES, :])
    gm = gm_ref[...]
    for hd in range(M_HEADS):
        lanes = slice(hd * HEAD_PAD, (hd + 1) * HEAD_PAD)
        hh = hm_ref[:, lanes]
        ms = jnp.sum(hh * hh, axis=-1, keepdims=True) * (1.0 / M_HEAD_DIM)
        ym = hh * lax.rsqrt(ms + EPS) * gm[:, lanes] * jax.nn.sigmoid(mo_ref[:, lanes])
        mix = mix + _dot(ym.astype(BF16), wom_ref[lanes, :])
    x1 = x_ref[...] + ga1 * mix
    h2 = (_rms(x1, g2_ref[...]) * (1.0 + sc2) + sh2).astype(BF16)
    hidden = wup_ref.shape[1]
    hc = 1024
    acc = None
    for c in range(hidden // hc):
        u = jnp.maximum(_dot(h2, wup_ref[:, c * hc:(c + 1) * hc]), 0.0)
        d = _dot((u * u).astype(BF16), wdn_ref[c * hc:(c + 1) * hc, :])
        acc = d if acc is None else acc + d
    x2 = x1 + ga2 * acc
    if final_norm:
        x2 = _rms(x2, gfin_ref[...])
    o_ref[...] = x2


def _out_mlp(x2d, seq, yf, hm, mo, ya, mod, mod_row0, per_batch_mod, gm, g2, gfin,
             wof, wom, woa, wup, wdn, tm, final_norm):
    t, d = x2d.shape
    tiles_per_seq = seq // tm
    tok = lambda w: pl.BlockSpec((tm, w), lambda i: (i, 0))
    yf_spec = pl.BlockSpec((F_WIDTH // LANES, tm, LANES), lambda i: (i // tiles_per_seq, i % tiles_per_seq, 0))
    return pl.pallas_call(
        functools.partial(_out_mlp_kernel, final_norm=final_norm),
        grid=(t // tm,),
        in_specs=[
            tok(d), yf_spec, tok(MP_WIDTH), tok(MP_WIDTH), tok(AP_WIDTH),
            pl.BlockSpec((1, 6, d), _mod_map(mod_row0, per_batch_mod, tiles_per_seq)),
            _const_spec((1, MP_WIDTH)), _const_spec((1, d)), _const_spec((1, d)),
            _const_spec((F_WIDTH, d)), _const_spec((MP_WIDTH, d)), _const_spec((AP_WIDTH, d)),
            _const_spec((d, wup.shape[1])), _const_spec((wdn.shape[0], d)),
        ],
        out_specs=tok(d),
        out_shape=jax.ShapeDtypeStruct((t, d), F32),
        compiler_params=_cparams(("arbitrary",)),
        name="out_mlp",
    )(x2d, yf, hm, mo, ya, mod, gm, g2, gfin, wof, wom, woa, wup, wdn)


def _dft_tables(n):
    idx = (np.arange(n, dtype=np.int64)[:, None] * np.arange(n, dtype=np.int64)[None, :]) % n
    ang = 2.0 * np.pi * idx.astype(np.float64) / n
    scale = 1.0 / np.sqrt(n)
    return np.cos(ang) * scale, np.sin(ang) * scale


def _channel_dft():
    c, s = _dft_tables(F_GROUP_DIM)
    eye = np.eye(F_GROUPS)
    return (jnp.asarray(np.kron(eye, c), dtype=F32).astype(BF16),
            jnp.asarray(np.kron(eye, s), dtype=F32).astype(BF16))


def _position_dft(n):
    c, s = _dft_tables(n)
    return jnp.asarray(c, dtype=F32).astype(BF16), jnp.asarray(s, dtype=F32).astype(BF16)


def _rope_tables(n, rotate):
    cos = np.zeros((n, HEAD_PAD), np.float32)
    sin = np.zeros((n, HEAD_PAD), np.float32)
    cos[:, :A_NOPE + A_ROPE] = 1.0
    if rotate:
        nf = A_ROPE // 4
        t = np.arange(n)
        row = (t // GRID_W).astype(np.float32)
        col = (t % GRID_W).astype(np.float32)
        freqs = (np.float32(ROPE_THETA) ** (-np.arange(nf, dtype=np.float32) / np.float32(nf))).astype(np.float32)
        for seg, pos in enumerate((row, col)):
            ang = pos[:, None] * freqs[None, :]
            c, s = np.cos(ang), np.sin(ang)
            base = A_NOPE + seg * 2 * nf
            cos[:, base:base + nf] = c
            cos[:, base + nf:base + 2 * nf] = c
            sin[:, base:base + nf] = -s
            sin[:, base + nf:base + 2 * nf] = s
    return jnp.asarray(cos), jnp.asarray(sin)


def _pad_heads_cols(w, heads, width):
    lead = w.shape[:-1]
    w = w.reshape(lead + (heads, width))
    w = jnp.pad(w, [(0, 0)] * len(lead) + [(0, 0), (0, HEAD_PAD - width)])
    return w.reshape(lead + (heads * HEAD_PAD,))


def _pad_cols(w, width):
    return jnp.pad(w, [(0, 0)] * (w.ndim - 1) + [(0, width - w.shape[-1])])


def _layer_weights(l, w_in, conv_qk, b_gates, g_mlstm, w_uq, w_ukv, w_out):
    wl = w_in[l]
    offs = np.cumsum([0, F_WIDTH, M_WIDTH, M_WIDTH, M_WIDTH, M_WIDTH, 4 * M_HEADS, Q_LORA, KV_LORA, A_ROPE])
    part = lambda i: wl[:, offs[i]:offs[i + 1]]
    place_rope = lambda w: jnp.pad(w, [(0, 0), (A_NOPE, LANES - A_NOPE - A_ROPE)])
    w_in_p = jnp.concatenate([
        part(0),
        _pad_heads_cols(part(1), M_HEADS, M_HEAD_DIM),
        _pad_heads_cols(part(2), M_HEADS, M_HEAD_DIM),
        _pad_heads_cols(part(4), M_HEADS, M_HEAD_DIM),
        part(6),
        part(7),
        place_rope(part(8)),
    ], axis=1).astype(BF16)
    assert w_in_p.shape[1] == IN_PAD
    w_vt = jnp.concatenate([_pad_heads_cols(part(3), M_HEADS, M_HEAD_DIM),
                            part(5)[:, GATE_I_COLS], part(5)[:, GATE_F_COLS]], axis=1).T.astype(BF16)

    conv = conv_qk[l]
    conv_p = jnp.concatenate([
        _pad_heads_cols(conv[:, :M_WIDTH], M_HEADS, M_HEAD_DIM),
        _pad_heads_cols(conv[:, M_WIDTH:], M_HEADS, M_HEAD_DIM),
    ], axis=1)
    conv_p = jnp.pad(conv_p, [(0, 8 - K_CONV), (0, 0)])

    bg = b_gates[l]
    bi, bf = bg[GATE_I_COLS], bg[GATE_F_COLS]
    gate_bias = dict(bir=bi[:, None], bfr=bf[:, None])

    gm = _pad_heads_cols(g_mlstm[l][None, :], M_HEADS, M_HEAD_DIM)

    wq = _pad_heads_cols(w_uq[l], A_HEADS, A_NOPE + A_ROPE).astype(BF16)
    ukv = w_ukv[l].reshape(KV_LORA, A_HEADS, A_NOPE + A_V)
    wk = jnp.pad(ukv[..., :A_NOPE], [(0, 0), (0, 0), (0, HEAD_PAD - A_NOPE)]).reshape(KV_LORA, AP_WIDTH).astype(BF16)
    wv = jnp.pad(ukv[..., A_NOPE:], [(0, 0), (0, 0), (0, HEAD_PAD - A_V)]).reshape(KV_LORA, AP_WIDTH).astype(BF16)

    wo = w_out[l]
    pad_rows = lambda w, heads, width: jnp.pad(
        w.reshape(heads, width, -1), [(0, 0), (0, HEAD_PAD - width), (0, 0)]).reshape(heads * HEAD_PAD, -1)
    wof = wo[:F_WIDTH].astype(BF16)
    wom = pad_rows(wo[F_WIDTH:F_WIDTH + M_WIDTH], M_HEADS, M_HEAD_DIM).astype(BF16)
    woa = pad_rows(wo[F_WIDTH + M_WIDTH:], A_HEADS, A_V).astype(BF16)
    return dict(w_in_p=w_in_p, w_vt=w_vt, conv_p=conv_p, gate_bias=gate_bias, gm=gm,
                wq=wq, wk=wk, wv=wv, wof=wof, wom=wom, woa=woa)


GATE_I_COLS = np.concatenate([np.arange(M_HEADS), 2 * M_HEADS + np.arange(M_HEADS)])
GATE_F_COLS = GATE_I_COLS + M_HEADS


def _stream_selectors():
    sel = np.zeros((N_STREAM, SEL_ROWS, MCH), np.float32)
    for s in range(N_STREAM):
        for part in range(3):
            sel[s, part * N_STREAM + s, :] = 1.0
    return jnp.asarray(sel, dtype=BF16)


def kernel(x, c, ctx, c_ctx, w_mod, b_mod, g_norm1, g_norm2, w_in, b_gates, conv_qk, g_mlstm,
           g_q_norm, g_kv_norm, w_uq, w_ukv, w_out, w_up, w_down, g_final):
    nb, seq, d = x.shape
    nctx = ctx.shape[1]
    depth = w_mod.shape[0]
    assert d == D_MODEL and seq % 256 == 0 and nctx % MCH == 0
    sel = _stream_selectors()

    tm = 256
    tm_ctx = min(256, nctx)
    tq = min(8 * ATTN_SUB, seq)
    tq_ctx = min(ATTN_SUB, nctx)

    dft_cc, dft_cs = _channel_dft()
    fft_lat = _fourier4_tables(seq)
    dft_ctx = _position_dft(nctx)
    rope_lat = _rope_tables(seq, True)
    rope_ctx = _rope_tables(nctx, False)

    rows = ((nb + 1 + 7) // 8) * 8
    cc = jnp.concatenate([c, c_ctx[None, :], jnp.zeros((rows - nb - 1, d), F32)], axis=0)
    mod_all = _modulation(cc, w_mod, b_mod).reshape(depth * rows, 6, d)

    xl = x.reshape(nb * seq, d)
    xc = ctx.reshape(nb * nctx, d)
    row = lambda v: v.reshape(1, -1)

    for l in range(depth):
        last = l == depth - 1
        lw = _layer_weights(l, w_in, conv_qk, b_gates, g_mlstm, w_uq, w_ukv, w_out)
        wup = w_up[l].astype(BF16)
        wdn = w_down[l].astype(BF16)
        row_lat, row_ctx = l * rows, l * rows + nb

        def tokenwise(xt, n, mod_row0, per_batch, rope, tile, with_q):
            return _inproj(xt, n, row(g_norm1[l]), mod_all, mod_row0, per_batch,
                           lw["w_in_p"], lw["w_vt"], dft_cc, dft_cs, *rope,
                           row(g_q_norm[l]), row(g_kv_norm[l]), lw["wq"], lw["wk"], lw["wv"], tile, with_q)

        zc, zs, mqk, vt, mo, gr, q_a, k_a, v_a = tokenwise(xl, seq, row_lat, True, rope_lat, tm, True)
        zc_c, zs_c, mqk_c, vt_c, mo_c, gr_c, q_ac, k_ac, v_ac = tokenwise(
            xc, nctx, row_ctx, False, rope_ctx, tm_ctx, not last)

        yf = _fourier4(fft_lat, zc, zs)

        def mlstm_inputs(mqk_s, vt_s, gr_s, n):
            q_s, k_s = _conv_silu(mqk_s.reshape(nb, n, 2 * MP_WIDTH), lw["conv_p"])
            return (q_s, k_s, vt_s, gr_s)

        hm, hm_c = _mlstm(mlstm_inputs(mqk, vt, gr, seq), mlstm_inputs(mqk_c, vt_c, gr_c, nctx),
                          sel=sel, **lw["gate_bias"])

        b3 = lambda a, n: a.reshape(nb, n, AP_WIDTH)
        keys_ctx = (b3(k_ac, nctx), b3(v_ac, nctx))
        ya = _attention(b3(q_a, seq), [(b3(k_a, seq), b3(v_a, seq)), keys_ctx], tq)

        mlp = functools.partial(
            _out_mlp, gm=lw["gm"], g2=row(g_norm2[l]), gfin=row(g_final),
            wof=lw["wof"], wom=lw["wom"], woa=lw["woa"], wup=wup, wdn=wdn)
        xl = mlp(xl, seq, yf, hm.reshape(nb * seq, MP_WIDTH), mo, ya.reshape(nb * seq, AP_WIDTH),
                 mod_all, row_lat, True, tm=tm, final_norm=last)

        if not last:
            yf_c = _fourier(*dft_ctx, zc_c, zs_c)
            ya_c = _attention(b3(q_ac, nctx), [keys_ctx], tq_ctx)
            xc = mlp(xc, nctx, yf_c, hm_c.reshape(nb * nctx, MP_WIDTH), mo_c,
                     ya_c.reshape(nb * nctx, AP_WIDTH), mod_all, row_ctx, False, tm=tm_ctx, final_norm=False)

    return xl.reshape(nb, seq, d)
```

```python
import functools

import numpy as np
import jax
import jax.numpy as jnp
from jax import lax
from jax.experimental import pallas as pl
from jax.experimental.pallas import tpu as pltpu

D_MODEL = 1024
GRID_W = 64
EPS = 1e-6
F_GROUPS = 4
F_GROUP_DIM = D_MODEL // 16
F_WIDTH = F_GROUPS * F_GROUP_DIM
M_HEADS = 4
M_HEAD_DIM = 3 * D_MODEL // 32
M_WIDTH = M_HEADS * M_HEAD_DIM
K_CONV = 5
A_HEADS = 4
A_NOPE = 64
A_ROPE = 32
A_V = 3 * D_MODEL // 32
Q_LORA = D_MODEL // 4
KV_LORA = D_MODEL // 8
ROPE_THETA = 10000.0
MLP_HIDDEN = 4 * D_MODEL

LANES = 128
HEAD_PAD = 128
MP_WIDTH = M_HEADS * HEAD_PAD
AP_WIDTH = A_HEADS * HEAD_PAD
VMEM_LIMIT = 56 * 1024 * 1024
MCH = 128
N_STREAM = 2 * M_HEADS

OFF_PF = 0
OFF_MQ = OFF_PF + F_WIDTH
OFF_MK = OFF_MQ + MP_WIDTH
OFF_MO = OFF_MK + MP_WIDTH
OFF_CQ = OFF_MO + MP_WIDTH
OFF_CKV = OFF_CQ + Q_LORA
OFF_KR = OFF_CKV + KV_LORA
IN_PAD = OFF_KR + LANES

BF16 = jnp.bfloat16
F32 = jnp.float32


def _cparams(sem):
    return pltpu.CompilerParams(dimension_semantics=sem, vmem_limit_bytes=VMEM_LIMIT)


def _const_spec(shape):
    nd = len(shape)
    return pl.BlockSpec(shape, lambda *_: (0,) * nd, pipeline_mode=pl.Buffered(1))


def _layer_spec(arr, layer):
    nd = arr.ndim
    return pl.BlockSpec((1,) + arr.shape[1:], lambda *_: (layer,) + (0,) * (nd - 1), pipeline_mode=pl.Buffered(1))


def _split3(a):
    hi = a.astype(BF16)
    r1 = a - hi.astype(F32)
    mid = r1.astype(BF16)
    lo = (r1 - mid.astype(F32)).astype(BF16)
    return hi, mid, lo


def _dot(a, b):
    return jnp.dot(a, b, preferred_element_type=F32)


def _dot_nt(a, b):
    return lax.dot_general(a, b, (((1,), (1,)), ((), ())), preferred_element_type=F32)


def _dot_tn(a, b):
    return lax.dot_general(a, b, (((0,), (0,)), ((), ())), preferred_element_type=F32)


def _rms(x, g):
    return x * lax.rsqrt(jnp.mean(x * x, axis=-1, keepdims=True) + EPS) * g


def _mod_kernel(c_ref, w_ref, b_ref, o_ref):
    c = c_ref[...]
    a = c * jax.nn.sigmoid(c)
    a_hi = a.astype(BF16)
    a_lo = (a - a_hi.astype(F32)).astype(BF16)
    w = w_ref[0]
    w_hi = w.astype(BF16)
    w_lo = (w - w_hi.astype(F32)).astype(BF16)
    acc = _dot(a_hi, w_hi) + _dot(a_hi, w_lo) + _dot(a_lo, w_hi)
    o_ref[0] = acc + b_ref[0]


def _modulation(cc, w_mod, b_mod):
    depth, d, n = w_mod.shape
    rows = cc.shape[0]
    tn = 1536
    return pl.pallas_call(
        _mod_kernel,
        grid=(depth, n // tn),
        in_specs=[
            pl.BlockSpec((rows, d), lambda l, j: (0, 0)),
            pl.BlockSpec((1, d, tn), lambda l, j: (l, 0, j)),
            pl.BlockSpec((1, 1, tn), lambda l, j: (l, 0, j)),
        ],
        out_specs=pl.BlockSpec((1, rows, tn), lambda l, j: (l, 0, j)),
        out_shape=jax.ShapeDtypeStruct((depth, rows, n), F32),
        compiler_params=_cparams(("arbitrary", "arbitrary")),
        name="modulation",
    )(cc, w_mod, b_mod.reshape(depth, 1, n))


def _rope(x, cos, sin, first_half):
    half = A_ROPE // 4
    partner = jnp.where(first_half, pltpu.roll(x, LANES - half, 1), pltpu.roll(x, half, 1))
    return x * cos + partner * sin


def _inproj_kernel(x_ref, g_ref, mod_ref, w_ref, wvt_ref, cc_ref, cs_ref,
                   cos_ref, sin_ref, gq_ref, gkv_ref, wq_ref, wk_ref, wv_ref,
                   zc_ref, zs_ref, mqk_ref, vt_ref, mo_ref, gr_ref, qa_ref, ka_ref, va_ref, *, with_q):
    x = x_ref[...]
    mod = mod_ref[0]
    h = _rms(x, g_ref[0]) * (1.0 + mod[1:2]) + mod[0:1]
    hb = h.astype(BF16)

    def proj(off, width):
        return _dot(hb, w_ref[0, :, off:off + width])

    ckv_kr = proj(OFF_CKV, 2 * LANES)
    cq = proj(OFF_CQ, Q_LORA) if with_q else None
    pf = proj(OFF_PF, F_WIDTH).astype(BF16)

    mqk_ref[...] = proj(OFF_MQ, 2 * MP_WIDTH)

    cos = cos_ref[...]
    sin = sin_ref[...]
    lane = lax.broadcasted_iota(jnp.int32, cos.shape, 1)
    first_half = ((lane - A_NOPE) & (A_ROPE // 2 - 1)) < A_ROPE // 4
    kvn = _rms(ckv_kr[:, :KV_LORA], gkv_ref[0]).astype(BF16)
    k_rope = _rope(ckv_kr[:, KV_LORA:], cos, sin, first_half)
    for hd in range(A_HEADS):
        lanes = slice(hd * HEAD_PAD, (hd + 1) * HEAD_PAD)
        ka_ref[:, lanes] = (_dot(kvn, wk_ref[0, :, lanes]) + k_rope).astype(BF16)
        va_ref[:, lanes] = _dot(kvn, wv_ref[0, :, lanes]).astype(BF16)
    if with_q:
        qn = _rms(cq, gq_ref[0]).astype(BF16)
        q_raw = [_dot(qn, wq_ref[0, :, hd * HEAD_PAD:(hd + 1) * HEAD_PAD]) for hd in range(A_HEADS)]
    zc_ref[...] = _dot(pf, cc_ref[...]).astype(BF16)
    zs_ref[...] = _dot(pf, cs_ref[...]).astype(BF16)

    vg = _dot_nt(wvt_ref[0], hb)
    vt = vg[:MP_WIDTH].astype(BF16)
    for j in range(vt_ref.shape[0]):
        vt_ref[j] = vt[:, j * MCH:(j + 1) * MCH]
        gr_ref[j] = vg[MP_WIDTH:, j * MCH:(j + 1) * MCH]
    mo_ref[...] = proj(OFF_MO, MP_WIDTH)

    if with_q:
        scale = (A_NOPE + A_ROPE) ** -0.5
        for hd in range(A_HEADS):
            q = _rope(q_raw[hd], cos, sin, first_half)
            qa_ref[:, hd * HEAD_PAD:(hd + 1) * HEAD_PAD] = (q * scale).astype(BF16)
    else:
        qa_ref[...] = jnp.zeros_like(qa_ref)


def _mod_map(mod_row0, per_batch_mod, tiles_per_seq):
    if per_batch_mod:
        return lambda i: (mod_row0 + i // tiles_per_seq, 0, 0)
    return lambda i: (mod_row0, 0, 0)


def _inproj(x2d, seq, layer, wts, mod, mod_row0, per_batch_mod, dft_cc, dft_cs, cos, sin, tm, with_q):
    t, d = x2d.shape
    nb = t // seq
    tiles_per_seq = seq // tm
    tok = lambda w: pl.BlockSpec((tm, w), lambda i: (i, 0))
    z_spec = pl.BlockSpec((tm, F_WIDTH), lambda i: (i % tiles_per_seq, i // tiles_per_seq))
    pos = pl.BlockSpec((tm, LANES), lambda i: (i % tiles_per_seq, 0))
    heads_bf16 = jax.ShapeDtypeStruct((t, AP_WIDTH), BF16)
    shapes = [
        jax.ShapeDtypeStruct((seq, nb * F_WIDTH), BF16),
        jax.ShapeDtypeStruct((seq, nb * F_WIDTH), BF16),
        jax.ShapeDtypeStruct((t, 2 * MP_WIDTH), F32),
        jax.ShapeDtypeStruct((t // MCH, MP_WIDTH, MCH), BF16),
        jax.ShapeDtypeStruct((t, MP_WIDTH), F32),
        jax.ShapeDtypeStruct((t // MCH, 2 * N_STREAM, MCH), F32),
        heads_bf16, heads_bf16, heads_bf16,
    ]
    vt_spec = pl.BlockSpec((tm // MCH, MP_WIDTH, MCH), lambda i: (i, 0, 0))
    gr_spec = pl.BlockSpec((tm // MCH, 2 * N_STREAM, MCH), lambda i: (i, 0, 0))
    out_specs = [z_spec, z_spec, tok(2 * MP_WIDTH), vt_spec, tok(MP_WIDTH), gr_spec,
                 tok(AP_WIDTH), tok(AP_WIDTH), tok(AP_WIDTH)]
    lay = lambda name: _layer_spec(wts[name], layer)
    return pl.pallas_call(
        functools.partial(_inproj_kernel, with_q=with_q),
        grid=(t // tm,),
        in_specs=[
            tok(d),
            lay("g1"),
            pl.BlockSpec((1, 6, d), _mod_map(mod_row0, per_batch_mod, tiles_per_seq)),
            lay("w_in_p"), lay("w_vt"),
            _const_spec((F_WIDTH, F_WIDTH)),
            _const_spec((F_WIDTH, F_WIDTH)),
            pos, pos,
            lay("gq"), lay("gkv"), lay("wq"), lay("wk"), lay("wv"),
        ],
        out_specs=out_specs,
        out_shape=shapes,
        compiler_params=_cparams(("arbitrary",)),
        name="inproj",
    )(x2d, wts["g1"], mod, wts["w_in_p"], wts["w_vt"], dft_cc, dft_cs, cos, sin,
      wts["gq"], wts["gkv"], wts["wq"], wts["wk"], wts["wv"])


def _fourier_kernel(c_ref, s_ref, zc_ref, zs_ref, o_ref):
    y = _dot(c_ref[...], zc_ref[...]) - _dot(s_ref[...], zs_ref[...])
    for slab in range(o_ref.shape[0]):
        o_ref[slab] = y[:, slab * LANES:(slab + 1) * LANES]


def _fourier(dft_c, dft_s, zc, zs):
    n, cols = zc.shape
    tr = min(n, 512)
    tc = min(cols, 512)
    return pl.pallas_call(
        _fourier_kernel,
        grid=(n // tr, cols // tc),
        in_specs=[
            pl.BlockSpec((tr, n), lambda i, j: (i, 0)),
            pl.BlockSpec((tr, n), lambda i, j: (i, 0)),
            pl.BlockSpec((n, tc), lambda i, j: (0, j)),
            pl.BlockSpec((n, tc), lambda i, j: (0, j)),
        ],
        out_specs=pl.BlockSpec((tc // LANES, tr, LANES), lambda i, j: (j, i, 0)),
        out_shape=jax.ShapeDtypeStruct((cols // LANES, n, LANES), F32),
        compiler_params=_cparams(("arbitrary", "arbitrary")),
        name="fourier",
    )(dft_c, dft_s, zc, zs)


def _fourier4_kernel(tab_ref, twc_ref, tws_ref, zc_ref, zs_ref, o_ref):
    m = zc_ref.shape[0] // 4
    reps = zc_ref.shape[1] // LANES
    k1 = pl.program_id(1)
    c0, c1, c2, c3 = (zc_ref[j * m:(j + 1) * m, :].astype(F32) for j in range(4))
    s0, s1, s2, s3 = (zs_ref[j * m:(j + 1) * m, :].astype(F32) for j in range(4))

    def emit(br, bi, k):
        if k:
            cos = jnp.concatenate([twc_ref[k - 1]] * reps, axis=1)
            sin = jnp.concatenate([tws_ref[k - 1]] * reps, axis=1)
            br, bi = br * cos + bi * sin, bi * cos - br * sin
        stacked = jnp.concatenate([br.astype(BF16), bi.astype(BF16)], axis=0)
        y = _dot(tab_ref[...], stacked)
        for slab in range(reps):
            o_ref[slab, pl.ds(k, m, stride=4), :] = y[:, slab * LANES:(slab + 1) * LANES]

    @pl.when(k1 == 0)
    def _():
        emit((c0 + c2) + (c1 + c3), -((s0 + s2) + (s1 + s3)), 0)

    @pl.when(k1 == 1)
    def _():
        emit((c0 - c2) - (s1 - s3), -(s0 - s2) - (c1 - c3), 1)

    @pl.when(k1 == 2)
    def _():
        emit((c0 + c2) - (c1 + c3), (s1 + s3) - (s0 + s2), 2)

    @pl.when(k1 == 3)
    def _():
        emit((c0 - c2) + (s1 - s3), (c1 - c3) - (s0 - s2), 3)


def _fourier4(tables, zc, zs):
    tab, twc, tws = tables
    n, cols = zc.shape
    m = n // 4
    tc = min(cols, 512)
    nj = cols // tc
    return pl.pallas_call(
        _fourier4_kernel,
        grid=(nj, 4),
        in_specs=[
            _const_spec((m, 2 * m)),
            _const_spec((3, m, LANES)),
            _const_spec((3, m, LANES)),
            pl.BlockSpec((n, tc), lambda j, k: (0, j)),
            pl.BlockSpec((n, tc), lambda j, k: (0, j)),
        ],
        out_specs=pl.BlockSpec((tc // LANES, n, LANES), lambda j, k: (j, 0, 0)),
        out_shape=jax.ShapeDtypeStruct((cols // LANES, n, LANES), F32),
        compiler_params=_cparams(("arbitrary", "arbitrary")),
        name="fourier4",
    )(tab, twc, tws, zc, zs)


def _fourier4_tables(n):
    m = n // 4
    idx = (np.arange(m, dtype=np.int64)[:, None] * np.arange(m, dtype=np.int64)[None, :]) % m
    ang = 2.0 * np.pi * idx.astype(np.float64) / m
    tab = np.concatenate([np.cos(ang), np.sin(ang)], axis=1) / np.sqrt(n)
    theta = 2.0 * np.pi * np.arange(m, dtype=np.float64)[None, :] * np.arange(1, 4, dtype=np.float64)[:, None] / n
    bcast = lambda t: jnp.asarray(np.repeat(t[:, :, None], LANES, axis=2), dtype=F32)
    return jnp.asarray(tab, dtype=F32).astype(BF16), bcast(np.cos(theta)), bcast(np.sin(theta))


CONV_ROWS = 256
CONV_HALO = 8


def _conv_kernel(u_ref, w_ref, q_ref, k_ref, pad_ref):
    n = u_ref.shape[1]
    width = u_ref.shape[2]
    zeros = jnp.zeros((CONV_HALO, width), F32)
    pad_ref[0:CONV_HALO, :] = zeros
    pad_ref[CONV_HALO + n:2 * CONV_HALO + n, :] = zeros
    pad_ref[CONV_HALO:CONV_HALO + n, :] = u_ref[0]
    w = w_ref[0]
    rows = min(CONV_ROWS, n)
    span = rows + 2 * CONV_HALO
    for r in range(n // rows):
        block = pad_ref[r * rows:r * rows + span, :]
        acc = None
        for j in range(K_CONV):
            shift = (K_CONV // 2 - j) % span
            tap = block if shift == 0 else pltpu.roll(block, shift, 0)
            term = tap[CONV_HALO:CONV_HALO + rows, :] * w[j:j + 1, :]
            acc = term if acc is None else acc + term
        act = acc * jax.nn.sigmoid(acc)
        q_ref[0, r * rows:(r + 1) * rows, :] = (act[:, :MP_WIDTH] * (M_HEAD_DIM ** -0.5)).astype(BF16)
        k_ref[0, r * rows:(r + 1) * rows, :] = act[:, MP_WIDTH:].astype(BF16)


def _conv_silu(mqk, layer, wts):
    nb, n, width = mqk.shape
    return pl.pallas_call(
        _conv_kernel,
        grid=(nb,),
        in_specs=[
            pl.BlockSpec((1, n, width), lambda b: (b, 0, 0)),
            _layer_spec(wts["conv_p"], layer),
        ],
        out_specs=[
            pl.BlockSpec((1, n, MP_WIDTH), lambda b: (b, 0, 0)),
            pl.BlockSpec((1, n, MP_WIDTH), lambda b: (b, 0, 0)),
        ],
        out_shape=[
            jax.ShapeDtypeStruct((nb, n, MP_WIDTH), BF16),
            jax.ShapeDtypeStruct((nb, n, MP_WIDTH), BF16),
        ],
        scratch_shapes=[pltpu.VMEM((n + 2 * CONV_HALO, width), F32)],
        compiler_params=_cparams(("arbitrary",)),
        name="conv_silu",
    )(mqk, wts["conv_p"])


def _log_sigmoid(x):
    return jnp.minimum(x, 0.0) - jnp.log(1.0 + jnp.exp(-jnp.abs(x)))


def _exact_dot_01(a, tri_bf16, a_on_left):
    out = None
    for term in _split3(a):
        d = _dot(term, tri_bf16) if a_on_left else _dot(tri_bf16, term)
        out = d if out is None else out + d
    return out


ONE_ROW = M_HEAD_DIM
(F_R, F_B, F_CM, F_TOT, F_CML, F_MP, F_A, F_WI, F_ELD, F_DEC, F_WK, F_HI, F_MID, F_LO) = range(14)
N_FIELDS = 14
OUT_GROUP = 4
SEL_ROWS = 32


def _mlstm_kernel(ql_ref, kl_ref, vtl_ref, grl_ref,
                  qc_ref, kc_ref, vtc_ref, grc_ref,
                  bir_ref, bfr_ref, sel_ref,
                  hl_ref, hc_ref,
                  ct_st, st_sc, rows_sc):
    L = MCH
    nh = M_HEADS
    ncc = qc_ref.shape[1] // L
    ncl = ql_ref.shape[1] // L

    d0 = lax.broadcasted_iota(jnp.int32, (L, L), 0)
    d1 = lax.broadcasted_iota(jnp.int32, (L, L), 1)
    le = d0 <= d1
    ge = d0 >= d1
    tri_le = le.astype(BF16)
    fwd_rows = lax.broadcasted_iota(jnp.int32, (N_STREAM, L), 0) < nh
    feat = lax.broadcasted_iota(jnp.int32, (HEAD_PAD, L), 0)
    one_row = feat == ONE_ROW
    keep_rows = feat < M_HEAD_DIM

    ct_st[...] = jnp.zeros_like(ct_st)

    def field(f, slot0, nc):
        return rows_sc[f, slot0:slot0 + nc].reshape(nc * N_STREAM, L)

    def set_field(f, slot0, nc, val):
        rows_sc[f, slot0:slot0 + nc] = val.reshape(nc, N_STREAM, L)

    def gate_pass(gr_ref, nc, slot0):
        n8 = nc * N_STREAM
        fwd = (lax.broadcasted_iota(jnp.int32, (n8, L), 0) & (N_STREAM - 1)) < nh
        lane = lax.broadcasted_iota(jnp.int32, (n8, L), 1)
        gi = gr_ref[:, 0:N_STREAM, :].reshape(n8, L) + bir_ref[0, 0:n8, :]
        f = _log_sigmoid(gr_ref[:, N_STREAM:2 * N_STREAM, :].reshape(n8, L) + bfr_ref[0, 0:n8, :])
        pre = _exact_dot_01(f, tri_le, a_on_left=True)
        total = jnp.sum(f, axis=1, keepdims=True)
        b = jnp.where(fwd, pre, total - pre + f)
        r = gi - b
        cm = r
        sh = 1
        while sh < L:
            from_left = jnp.where(lane >= sh, pltpu.roll(cm, sh, 1), -jnp.inf)
            from_right = jnp.where(lane < L - sh, pltpu.roll(cm, L - sh, 1), -jnp.inf)
            cm = jnp.maximum(cm, jnp.where(fwd, from_left, from_right))
            sh *= 2
        set_field(F_R, slot0, nc, r)
        set_field(F_B, slot0, nc, b)
        set_field(F_CM, slot0, nc, cm)
        set_field(F_TOT, slot0, nc, jnp.broadcast_to(total, (n8, L)))
        set_field(F_CML, slot0, nc, jnp.broadcast_to(jnp.max(r, axis=1, keepdims=True), (n8, L)))

    def m_scan(nc, slot0, m0):
        def step(j, m):
            sf = slot0 + j
            sb = slot0 + nc - 1 - j
            tot = jnp.where(fwd_rows, rows_sc[F_TOT, sf], rows_sc[F_TOT, sb])
            cml = jnp.where(fwd_rows, rows_sc[F_CML, sf], rows_sc[F_CML, sb])
            rows_sc[F_MP, sf, 0:nh, :] = m[0:nh]
            rows_sc[F_MP, sb, nh:N_STREAM, :] = m[nh:N_STREAM]
            return tot + jnp.maximum(m, cml)

        return lax.fori_loop(0, nc, step, m0)

    def weight_pass(nc, slot0):
        mp, cm, b, r = (field(f, slot0, nc) for f in (F_MP, F_CM, F_B, F_R))
        big = jnp.maximum(mp, field(F_CML, slot0, nc))
        a = -jnp.maximum(mp, cm)
        set_field(F_A, slot0, nc, a)
        set_field(F_WI, slot0, nc, jnp.exp(mp + a))
        set_field(F_ELD, slot0, nc, jnp.exp(a - b))
        set_field(F_DEC, slot0, nc, jnp.exp(mp - big))
        set_field(F_WK, slot0, nc, jnp.exp(r - big))
        for f, term in zip((F_HI, F_MID, F_LO), _split3(r)):
            set_field(f, slot0, nc, term.astype(F32))

    def value_slab(vt_ref, c, hd):
        vt = vt_ref[c, hd * HEAD_PAD:(hd + 1) * HEAD_PAD, :].astype(F32)
        return jnp.where(one_row, 1.0, vt)

    def state_pass(k_ref, vt_ref, nc, slot0):
        def step(j, carry):
            uts = []
            for sidx in range(N_STREAM):
                hd = sidx % nh
                c = j if sidx < nh else nc - 1 - j
                wk = rows_sc[F_WK, slot0 + c, sidx:sidx + 1, :]
                kk = k_ref[0, pl.ds(pl.multiple_of(c * L, L), L), hd * HEAD_PAD:(hd + 1) * HEAD_PAD]
                uts.append(_dot((value_slab(vt_ref, c, hd) * wk).astype(BF16), kk))
            for sidx in range(N_STREAM):
                slot = slot0 + (j if sidx < nh else nc - 1 - j)
                prev = ct_st[sidx]
                st_sc[sidx, slot] = prev.astype(BF16)
                ct_st[sidx] = rows_sc[F_DEC, slot, sidx:sidx + 1, :] * prev + uts[sidx]
            return carry

        lax.fori_loop(0, nc, step, 0)

    def output_pass(q_ref, k_ref, vt_ref, out_ref, nc, slot0):
        head_lanes = [slice(hd * HEAD_PAD, (hd + 1) * HEAD_PAD) for hd in range(nh)]

        def independent_matmuls(c):
            rows = pl.ds(pl.multiple_of(c * L, L), L)
            slot = slot0 + c
            r3 = jnp.concatenate([rows_sc[F_HI, slot], rows_sc[F_MID, slot], rows_sc[F_LO, slot],
                                  jnp.zeros((N_STREAM, L), F32)], axis=0).astype(BF16)
            qs = [q_ref[0, rows, lanes] for lanes in head_lanes]
            s_ts = [_dot_nt(k_ref[0, rows, lanes], q) for lanes, q in zip(head_lanes, qs)]
            inters = [_dot_nt(st_sc[sidx, slot], qs[sidx % nh]) for sidx in range(N_STREAM)]
            r_ts = [_dot_tn(r3, sel_ref[sidx]) for sidx in range(N_STREAM)]
            return s_ts, inters, r_ts

        def finish(c, s_ts, inters, r_ts):
            rows = pl.ds(pl.multiple_of(c * L, L), L)
            slot = slot0 + c
            a_rows = rows_sc[F_A, slot]
            wi_rows = rows_sc[F_WI, slot]
            eld_rows = rows_sc[F_ELD, slot]
            p_ts = []
            for sidx in range(N_STREAM):
                one = slice(sidx, sidx + 1)
                dm = jnp.where(le if sidx < nh else ge, r_ts[sidx] + a_rows[one, :], -jnp.inf)
                p_ts.append((s_ts[sidx % nh] * jnp.exp(dm)).astype(BF16))
            vtas = [value_slab(vt_ref, c, hd).astype(BF16) for hd in range(nh)]
            z_ts = [_dot(vtas[sidx % nh], p_ts[sidx]) for sidx in range(N_STREAM)]
            for hd in range(nh):
                hsum = None
                for sidx in (hd, nh + hd):
                    one = slice(sidx, sidx + 1)
                    z_t = z_ts[sidx] + inters[sidx] * wi_rows[one, :]
                    den = z_t[ONE_ROW:ONE_ROW + 1, :]
                    h_t = z_t * (1.0 / jnp.maximum(jnp.abs(den), eld_rows[one, :]))
                    hsum = h_t if hsum is None else hsum + h_t
                out_ref[0, rows, head_lanes[hd]] = jnp.where(keep_rows, hsum, 0.0).T

        group = min(OUT_GROUP, nc)

        def step(g, carry):
            ahead = independent_matmuls(g * group)
            for u in range(group):
                cur = ahead
                if u + 1 < group:
                    ahead = independent_matmuls(g * group + u + 1)
                finish(g * group + u, *cur)
            return carry

        lax.fori_loop(0, nc // group, step, 0)

    gate_pass(grc_ref, ncc, 0)
    gate_pass(grl_ref, ncl, ncc)
    m1 = m_scan(ncc, 0, jnp.zeros((N_STREAM, L), F32))
    m_scan(ncl, ncc, m1)
    weight_pass(ncc, 0)
    weight_pass(ncl, ncc)
    state_pass(kc_ref, vtc_ref, ncc, 0)
    state_pass(kl_ref, vtl_ref, ncl, ncc)
    output_pass(qc_ref, kc_ref, vtc_ref, hc_ref, ncc, 0)
    output_pass(ql_ref, kl_ref, vtl_ref, hl_ref, ncl, ncc)


def _mlstm(lat, ctx, layer, wts, sel):
    nb, n, _ = lat[0].shape
    nctx = ctx[0].shape[1]
    assert MCH == LANES == HEAD_PAD
    nct = (n + nctx) // MCH
    assert wts["bir"].shape[1] >= max(n, nctx) // MCH * N_STREAM

    def specs(rows):
        nc = rows // MCH
        seq = lambda w: pl.BlockSpec((1, rows, w), lambda b: (b, 0, 0))
        chunked = lambda rows_: pl.BlockSpec((nc, rows_, MCH), lambda b: (b, 0, 0))
        return [seq(MP_WIDTH), seq(MP_WIDTH), chunked(MP_WIDTH), chunked(2 * N_STREAM)]

    out_spec = lambda rows: pl.BlockSpec((1, rows, MP_WIDTH), lambda b: (b, 0, 0))
    return pl.pallas_call(
        _mlstm_kernel,
        grid=(nb,),
        in_specs=specs(n) + specs(nctx) + [
            _layer_spec(wts["bir"], layer), _layer_spec(wts["bfr"], layer),
            _const_spec((N_STREAM, SEL_ROWS, MCH)),
        ],
        out_specs=[out_spec(n), out_spec(nctx)],
        out_shape=[
            jax.ShapeDtypeStruct((nb, n, MP_WIDTH), F32),
            jax.ShapeDtypeStruct((nb, nctx, MP_WIDTH), F32),
        ],
        scratch_shapes=[
            pltpu.VMEM((N_STREAM, HEAD_PAD, HEAD_PAD), F32),
            pltpu.VMEM((N_STREAM, nct, HEAD_PAD, HEAD_PAD), BF16),
            pltpu.VMEM((N_FIELDS, nct, N_STREAM, MCH), F32),
        ],
        compiler_params=_cparams(("arbitrary",)),
        name="mlstm",
    )(*lat, *ctx, wts["bir"], wts["bfr"], sel)


ATTN_SUB = 256


def _attn_kernel(*refs, n_sets):
    q_ref = refs[0]
    kv_refs = refs[1:1 + 2 * n_sets]
    o_ref = refs[1 + 2 * n_sets]
    sub = min(ATTN_SUB, q_ref.shape[1])
    n_sub = q_ref.shape[1] // sub

    def scores_of(t):
        q = q_ref[0, t * sub:(t + 1) * sub, :]
        return [_dot_nt(q, kv_refs[2 * i][0]) for i in range(n_sets)]

    nxt = scores_of(0)
    for t in range(n_sub):
        rows = slice(t * sub, (t + 1) * sub)
        scores = nxt
        if t + 1 < n_sub:
            nxt = scores_of(t + 1)
        m = None
        for s in scores:
            sm = jnp.max(s, axis=-1, keepdims=True)
            m = sm if m is None else jnp.maximum(m, sm)
        acc = None
        den = None
        for i, s in enumerate(scores):
            p = jnp.exp(s - m)
            l = jnp.sum(p, axis=-1, keepdims=True)
            o = _dot(p.astype(BF16), kv_refs[2 * i + 1][0])
            acc = o if acc is None else acc + o
            den = l if den is None else den + l
        o_ref[0, rows, :] = (acc / den).astype(BF16)


def _attention(q, key_sets, tq):
    nb, n, _ = q.shape
    n_sets = len(key_sets)
    in_specs = [pl.BlockSpec((1, tq, HEAD_PAD), lambda b, h, i: (b, i, h))]
    args = [q]
    for k, v in key_sets:
        nk = k.shape[1]
        spec = pl.BlockSpec((1, nk, HEAD_PAD), lambda b, h, i: (b, 0, h))
        in_specs += [spec, spec]
        args += [k, v]
    return pl.pallas_call(
        functools.partial(_attn_kernel, n_sets=n_sets),
        grid=(nb, A_HEADS, n // tq),
        in_specs=in_specs,
        out_specs=pl.BlockSpec((1, tq, HEAD_PAD), lambda b, h, i: (b, i, h)),
        out_shape=jax.ShapeDtypeStruct((nb, n, AP_WIDTH), BF16),
        compiler_params=_cparams(("arbitrary", "arbitrary", "arbitrary")),
        name="attention",
    )(*args)


MLP_CHUNK = 1024


def _out_mlp_kernel(x_ref, yf_ref, hm_ref, mo_ref, ya_ref, mod_ref,
                    gm_ref, g2_ref, gfin_ref, wof_ref, wom_ref, woa_ref, wup_ref, wdn_ref,
                    o_ref, *, final_norm):
    mod = mod_ref[0]
    ga1, sh2, sc2, ga2 = mod[2:3], mod[3:4], mod[4:5], mod[5:6]
    mix = _dot(ya_ref[...], woa_ref[0])
    for slab in range(yf_ref.shape[0]):
        mix = mix + _dot(yf_ref[slab].astype(BF16), wof_ref[0, slab * LANES:(slab + 1) * LANES, :])
    gm = gm_ref[0]
    for hd in range(M_HEADS):
        lanes = slice(hd * HEAD_PAD, (hd + 1) * HEAD_PAD)
        hh = hm_ref[:, lanes]
        ms = jnp.sum(hh * hh, axis=-1, keepdims=True) * (1.0 / M_HEAD_DIM)
        ym = hh * lax.rsqrt(ms + EPS) * gm[:, lanes] * jax.nn.sigmoid(mo_ref[:, lanes])
        mix = mix + _dot(ym.astype(BF16), wom_ref[0, lanes, :])
    x1 = x_ref[...] + ga1 * mix
    h2 = (_rms(x1, g2_ref[0]) * (1.0 + sc2) + sh2).astype(BF16)
    acc = None
    for c in range(wup_ref.shape[2] // MLP_CHUNK):
        cols = slice(c * MLP_CHUNK, (c + 1) * MLP_CHUNK)
        u = jnp.maximum(_dot(h2, wup_ref[0, :, cols]), 0.0)
        d = _dot((u * u).astype(BF16), wdn_ref[0, cols, :])
        acc = d if acc is None else acc + d
    x2 = x1 + ga2 * acc
    if final_norm:
        x2 = _rms(x2, gfin_ref[...])
    o_ref[...] = x2


def _out_mlp(x2d, seq, yf, hm, mo, ya, layer, wts, mod, mod_row0, per_batch_mod, gfin, tm, final_norm):
    t, d = x2d.shape
    tiles_per_seq = seq // tm
    tok = lambda w: pl.BlockSpec((tm, w), lambda i: (i, 0))
    yf_spec = pl.BlockSpec((F_WIDTH // LANES, tm, LANES), lambda i: (i // tiles_per_seq, i % tiles_per_seq, 0))
    names = ("gm", "g2", "wof", "wom", "woa", "wup", "wdn")
    lay = {name: _layer_spec(wts[name], layer) for name in names}
    return pl.pallas_call(
        functools.partial(_out_mlp_kernel, final_norm=final_norm),
        grid=(t // tm,),
        in_specs=[
            tok(d), yf_spec, tok(MP_WIDTH), tok(MP_WIDTH), tok(AP_WIDTH),
            pl.BlockSpec((1, 6, d), _mod_map(mod_row0, per_batch_mod, tiles_per_seq)),
            lay["gm"], lay["g2"], _const_spec((1, d)),
            lay["wof"], lay["wom"], lay["woa"], lay["wup"], lay["wdn"],
        ],
        out_specs=tok(d),
        out_shape=jax.ShapeDtypeStruct((t, d), F32),
        compiler_params=_cparams(("arbitrary",)),
        name="out_mlp",
    )(x2d, yf, hm, mo, ya, mod, wts["gm"], wts["g2"], gfin,
      wts["wof"], wts["wom"], wts["woa"], wts["wup"], wts["wdn"])


def _dft_tables(n):
    idx = (np.arange(n, dtype=np.int64)[:, None] * np.arange(n, dtype=np.int64)[None, :]) % n
    ang = 2.0 * np.pi * idx.astype(np.float64) / n
    scale = 1.0 / np.sqrt(n)
    return np.cos(ang) * scale, np.sin(ang) * scale


def _channel_dft():
    c, s = _dft_tables(F_GROUP_DIM)
    eye = np.eye(F_GROUPS)
    return (jnp.asarray(np.kron(eye, c), dtype=F32).astype(BF16),
            jnp.asarray(np.kron(eye, s), dtype=F32).astype(BF16))


def _position_dft(n):
    c, s = _dft_tables(n)
    return jnp.asarray(c, dtype=F32).astype(BF16), jnp.asarray(s, dtype=F32).astype(BF16)


def _rope_tables(n, rotate):
    cos = np.zeros((n, HEAD_PAD), np.float32)
    sin = np.zeros((n, HEAD_PAD), np.float32)
    cos[:, :A_NOPE + A_ROPE] = 1.0
    if rotate:
        nf = A_ROPE // 4
        t = np.arange(n)
        row = (t // GRID_W).astype(np.float32)
        col = (t % GRID_W).astype(np.float32)
        freqs = (np.float32(ROPE_THETA) ** (-np.arange(nf, dtype=np.float32) / np.float32(nf))).astype(np.float32)
        for seg, pos in enumerate((row, col)):
            ang = pos[:, None] * freqs[None, :]
            c, s = np.cos(ang), np.sin(ang)
            base = A_NOPE + seg * 2 * nf
            cos[:, base:base + nf] = c
            cos[:, base + nf:base + 2 * nf] = c
            sin[:, base:base + nf] = -s
            sin[:, base + nf:base + 2 * nf] = s
    return jnp.asarray(cos), jnp.asarray(sin)


def _pad_heads_cols(w, heads, width):
    lead = w.shape[:-1]
    w = w.reshape(lead + (heads, width))
    w = jnp.pad(w, [(0, 0)] * len(lead) + [(0, 0), (0, HEAD_PAD - width)])
    return w.reshape(lead + (heads * HEAD_PAD,))


def _pad_heads_rows(w, heads, width):
    depth, _, n = w.shape
    w = jnp.pad(w.reshape(depth, heads, width, n), [(0, 0), (0, 0), (0, HEAD_PAD - width), (0, 0)])
    return w.reshape(depth, heads * HEAD_PAD, n)


GATE_I_COLS = np.concatenate([np.arange(M_HEADS), 2 * M_HEADS + np.arange(M_HEADS)])
GATE_F_COLS = GATE_I_COLS + M_HEADS


def _prepare_weights(max_chunks, g_norm1, g_norm2, w_in, b_gates, conv_qk, g_mlstm, g_q_norm, g_kv_norm,
                     w_uq, w_ukv, w_out, w_up, w_down):
    offs = np.cumsum([0, F_WIDTH, M_WIDTH, M_WIDTH, M_WIDTH, M_WIDTH, 4 * M_HEADS, Q_LORA, KV_LORA, A_ROPE])
    part = lambda i: w_in[:, :, offs[i]:offs[i + 1]]
    heads = lambda w: _pad_heads_cols(w, M_HEADS, M_HEAD_DIM)
    w_in_p = jnp.concatenate([
        part(0), heads(part(1)), heads(part(2)), heads(part(4)), part(6), part(7),
        jnp.pad(part(8), [(0, 0), (0, 0), (A_NOPE, LANES - A_NOPE - A_ROPE)]),
    ], axis=2).astype(BF16)
    assert w_in_p.shape[2] == IN_PAD
    gates = part(5)
    w_vt = jnp.concatenate([heads(part(3)), gates[:, :, GATE_I_COLS], gates[:, :, GATE_F_COLS]],
                           axis=2).transpose(0, 2, 1).astype(BF16)

    conv_p = jnp.concatenate([heads(conv_qk[:, :, :M_WIDTH]), heads(conv_qk[:, :, M_WIDTH:])], axis=2)
    conv_p = jnp.pad(conv_p, [(0, 0), (0, 8 - K_CONV), (0, 0)])

    tile_rows = lambda b: jnp.tile(b[:, :, None], (1, max_chunks, 1))
    ukv = w_ukv.reshape(w_ukv.shape[0], KV_LORA, A_HEADS, A_NOPE + A_V)
    pad_kv = lambda w: jnp.pad(w, [(0, 0), (0, 0), (0, 0), (0, HEAD_PAD - w.shape[-1])]).reshape(
        w.shape[0], KV_LORA, AP_WIDTH).astype(BF16)
    vec = lambda g: g[:, None, :]
    return dict(
        g1=vec(g_norm1), g2=vec(g_norm2), gq=vec(g_q_norm), gkv=vec(g_kv_norm),
        w_in_p=w_in_p, w_vt=w_vt, conv_p=conv_p,
        bir=tile_rows(b_gates[:, GATE_I_COLS]), bfr=tile_rows(b_gates[:, GATE_F_COLS]),
        gm=_pad_heads_cols(g_mlstm, M_HEADS, M_HEAD_DIM)[:, None, :],
        wq=_pad_heads_cols(w_uq, A_HEADS, A_NOPE + A_ROPE).astype(BF16),
        wk=pad_kv(ukv[..., :A_NOPE]), wv=pad_kv(ukv[..., A_NOPE:]),
        wof=w_out[:, :F_WIDTH].astype(BF16),
        wom=_pad_heads_rows(w_out[:, F_WIDTH:F_WIDTH + M_WIDTH], M_HEADS, M_HEAD_DIM).astype(BF16),
        woa=_pad_heads_rows(w_out[:, F_WIDTH + M_WIDTH:], A_HEADS, A_V).astype(BF16),
        wup=w_up.astype(BF16), wdn=w_down.astype(BF16),
    )


def _stream_selectors():
    sel = np.zeros((N_STREAM, SEL_ROWS, MCH), np.float32)
    for s in range(N_STREAM):
        for part in range(3):
            sel[s, part * N_STREAM + s, :] = 1.0
    return jnp.asarray(sel, dtype=BF16)


def kernel(x, c, ctx, c_ctx, w_mod, b_mod, g_norm1, g_norm2, w_in, b_gates, conv_qk, g_mlstm,
           g_q_norm, g_kv_norm, w_uq, w_ukv, w_out, w_up, w_down, g_final):
    nb, seq, d = x.shape
    nctx = ctx.shape[1]
    depth = w_mod.shape[0]
    assert d == D_MODEL and seq % 256 == 0 and nctx % MCH == 0
    sel = _stream_selectors()

    tm = 256
    tm_ctx = min(256, nctx)
    tq = min(8 * ATTN_SUB, seq)
    tq_ctx = min(ATTN_SUB, nctx)

    dft_cc, dft_cs = _channel_dft()
    fft_lat = _fourier4_tables(seq)
    dft_ctx = _position_dft(nctx)
    rope_lat = _rope_tables(seq, True)
    rope_ctx = _rope_tables(nctx, False)
    wts = _prepare_weights(max(seq, nctx) // MCH, g_norm1, g_norm2, w_in, b_gates, conv_qk, g_mlstm,
                           g_q_norm, g_kv_norm, w_uq, w_ukv, w_out, w_up, w_down)
    gfin = g_final.reshape(1, d)

    rows = ((nb + 1 + 7) // 8) * 8
    cc = jnp.concatenate([c, c_ctx[None, :], jnp.zeros((rows - nb - 1, d), F32)], axis=0)
    mod_all = _modulation(cc, w_mod, b_mod).reshape(depth * rows, 6, d)

    xl = x.reshape(nb * seq, d)
    xc = ctx.reshape(nb * nctx, d)

    for l in range(depth):
        last = l == depth - 1
        row_lat, row_ctx = l * rows, l * rows + nb

        zc, zs, mqk, vt, mo, gr, q_a, k_a, v_a = _inproj(
            xl, seq, l, wts, mod_all, row_lat, True, dft_cc, dft_cs, *rope_lat, tm, True)
        zc_c, zs_c, mqk_c, vt_c, mo_c, gr_c, q_ac, k_ac, v_ac = _inproj(
            xc, nctx, l, wts, mod_all, row_ctx, False, dft_cc, dft_cs, *rope_ctx, tm_ctx, not last)

        yf = _fourier4(fft_lat, zc, zs)

        def mlstm_inputs(mqk_s, vt_s, gr_s, n):
            q_s, k_s = _conv_silu(mqk_s.reshape(nb, n, 2 * MP_WIDTH), l, wts)
            return (q_s, k_s, vt_s, gr_s)

        hm, hm_c = _mlstm(mlstm_inputs(mqk, vt, gr, seq), mlstm_inputs(mqk_c, vt_c, gr_c, nctx), l, wts, sel)

        b3 = lambda a, n: a.reshape(nb, n, AP_WIDTH)
        keys_ctx = (b3(k_ac, nctx), b3(v_ac, nctx))
        ya = _attention(b3(q_a, seq), [(b3(k_a, seq), b3(v_a, seq)), keys_ctx], tq)

        xl = _out_mlp(xl, seq, yf, hm.reshape(nb * seq, MP_WIDTH), mo, ya.reshape(nb * seq, AP_WIDTH),
                      l, wts, mod_all, row_lat, True, gfin, tm, last)

        if not last:
            yf_c = _fourier(*dft_ctx, zc_c, zs_c)
            ya_c = _attention(b3(q_ac, nctx), [keys_ctx], tq_ctx)
            xc = _out_mlp(xc, nctx, yf_c, hm_c.reshape(nb * nctx, MP_WIDTH), mo_c,
                          ya_c.reshape(nb * nctx, AP_WIDTH), l, wts, mod_all, row_ctx, False, gfin, tm_ctx, False)

    return xl.reshape(nb, seq, d)
```

```python
import functools

import numpy as np
import jax
import jax.numpy as jnp
from jax import lax
from jax.experimental import pallas as pl
from jax.experimental.pallas import tpu as pltpu

D_MODEL = 1024
GRID_W = 64
EPS = 1e-6
F_GROUPS = 4
F_GROUP_DIM = D_MODEL // 16
F_WIDTH = F_GROUPS * F_GROUP_DIM
M_HEADS = 4
M_HEAD_DIM = 3 * D_MODEL // 32
M_WIDTH = M_HEADS * M_HEAD_DIM
K_CONV = 5
A_HEADS = 4
A_NOPE = 64
A_ROPE = 32
A_V = 3 * D_MODEL // 32
Q_LORA = D_MODEL // 4
KV_LORA = D_MODEL // 8
ROPE_THETA = 10000.0
MLP_HIDDEN = 4 * D_MODEL

LANES = 128
HEAD_PAD = 128
MP_WIDTH = M_HEADS * HEAD_PAD
AP_WIDTH = A_HEADS * HEAD_PAD
VMEM_LIMIT = 56 * 1024 * 1024
MCH = 128
N_STREAM = 2 * M_HEADS

OFF_PF = 0
OFF_MQ = OFF_PF + F_WIDTH
OFF_MK = OFF_MQ + MP_WIDTH
OFF_MO = OFF_MK + MP_WIDTH
OFF_CQ = OFF_MO + MP_WIDTH
OFF_CKV = OFF_CQ + Q_LORA
OFF_KR = OFF_CKV + KV_LORA
IN_PAD = OFF_KR + LANES

BF16 = jnp.bfloat16
F32 = jnp.float32


def _cparams(sem):
    return pltpu.CompilerParams(dimension_semantics=sem, vmem_limit_bytes=VMEM_LIMIT)


def _const_spec(shape):
    nd = len(shape)
    return pl.BlockSpec(shape, lambda *_: (0,) * nd, pipeline_mode=pl.Buffered(1))


def _layer_spec(arr, layer):
    nd = arr.ndim
    return pl.BlockSpec((1,) + arr.shape[1:], lambda *_: (layer,) + (0,) * (nd - 1), pipeline_mode=pl.Buffered(1))


def _split3(a):
    hi = a.astype(BF16)
    r1 = a - hi.astype(F32)
    mid = r1.astype(BF16)
    lo = (r1 - mid.astype(F32)).astype(BF16)
    return hi, mid, lo


def _dot(a, b):
    return jnp.dot(a, b, preferred_element_type=F32)


def _dot_nt(a, b):
    return lax.dot_general(a, b, (((1,), (1,)), ((), ())), preferred_element_type=F32)


def _dot_tn(a, b):
    return lax.dot_general(a, b, (((0,), (0,)), ((), ())), preferred_element_type=F32)


def _rms(x, g):
    return x * lax.rsqrt(jnp.mean(x * x, axis=-1, keepdims=True) + EPS) * g


def _mod_kernel(c_ref, w_ref, b_ref, o_ref):
    c = c_ref[...]
    a = c * jax.nn.sigmoid(c)
    a_hi = a.astype(BF16)
    a_lo = (a - a_hi.astype(F32)).astype(BF16)
    w = w_ref[0]
    w_hi = w.astype(BF16)
    w_lo = (w - w_hi.astype(F32)).astype(BF16)
    acc = _dot(a_hi, w_hi) + _dot(a_hi, w_lo) + _dot(a_lo, w_hi)
    o_ref[0] = acc + b_ref[0]


def _modulation(cc, w_mod, b_mod):
    depth, d, n = w_mod.shape
    rows = cc.shape[0]
    tn = 1536
    return pl.pallas_call(
        _mod_kernel,
        grid=(depth, n // tn),
        in_specs=[
            pl.BlockSpec((rows, d), lambda l, j: (0, 0)),
            pl.BlockSpec((1, d, tn), lambda l, j: (l, 0, j)),
            pl.BlockSpec((1, 1, tn), lambda l, j: (l, 0, j)),
        ],
        out_specs=pl.BlockSpec((1, rows, tn), lambda l, j: (l, 0, j)),
        out_shape=jax.ShapeDtypeStruct((depth, rows, n), F32),
        compiler_params=_cparams(("arbitrary", "arbitrary")),
        name="modulation",
    )(cc, w_mod, b_mod.reshape(depth, 1, n))


def _rope(x, cos, sin, first_half):
    half = A_ROPE // 4
    partner = jnp.where(first_half, pltpu.roll(x, LANES - half, 1), pltpu.roll(x, half, 1))
    return x * cos + partner * sin


def _inproj_kernel(x_ref, g_ref, mod_ref, w_ref, wvt_ref, cc_ref, cs_ref,
                   cos_ref, sin_ref, gq_ref, gkv_ref, wq_ref, wk_ref, wv_ref,
                   zc_ref, zs_ref, mqk_ref, vt_ref, mo_ref, gr_ref, qa_ref, ka_ref, va_ref, *, with_q):
    x = x_ref[...]
    mod = mod_ref[0]
    h = _rms(x, g_ref[0]) * (1.0 + mod[1:2]) + mod[0:1]
    hb = h.astype(BF16)

    def proj(off, width):
        return _dot(hb, w_ref[0, :, off:off + width])

    ckv_kr = proj(OFF_CKV, 2 * LANES)
    cq = proj(OFF_CQ, Q_LORA) if with_q else None
    pf = proj(OFF_PF, F_WIDTH).astype(BF16)

    mqk_ref[...] = proj(OFF_MQ, 2 * MP_WIDTH)

    cos = cos_ref[...]
    sin = sin_ref[...]
    lane = lax.broadcasted_iota(jnp.int32, cos.shape, 1)
    first_half = ((lane - A_NOPE) & (A_ROPE // 2 - 1)) < A_ROPE // 4
    kvn = _rms(ckv_kr[:, :KV_LORA], gkv_ref[0]).astype(BF16)
    k_rope = _rope(ckv_kr[:, KV_LORA:], cos, sin, first_half)
    for hd in range(A_HEADS):
        lanes = slice(hd * HEAD_PAD, (hd + 1) * HEAD_PAD)
        ka_ref[:, lanes] = (_dot(kvn, wk_ref[0, :, lanes]) + k_rope).astype(BF16)
        va_ref[:, lanes] = _dot(kvn, wv_ref[0, :, lanes]).astype(BF16)
    if with_q:
        qn = _rms(cq, gq_ref[0]).astype(BF16)
        q_raw = [_dot(qn, wq_ref[0, :, hd * HEAD_PAD:(hd + 1) * HEAD_PAD]) for hd in range(A_HEADS)]
    zc_ref[...] = _dot(pf, cc_ref[...]).astype(BF16)
    zs_ref[...] = _dot(pf, cs_ref[...]).astype(BF16)

    vg = _dot_nt(wvt_ref[0], hb)
    vt = vg[:MP_WIDTH].astype(BF16)
    for j in range(vt_ref.shape[0]):
        vt_ref[j] = vt[:, j * MCH:(j + 1) * MCH]
        gr_ref[j] = vg[MP_WIDTH:, j * MCH:(j + 1) * MCH]
    mo_ref[...] = proj(OFF_MO, MP_WIDTH)

    if with_q:
        scale = (A_NOPE + A_ROPE) ** -0.5
        for hd in range(A_HEADS):
            q = _rope(q_raw[hd], cos, sin, first_half)
            qa_ref[:, hd * HEAD_PAD:(hd + 1) * HEAD_PAD] = (q * scale).astype(BF16)
    else:
        qa_ref[...] = jnp.zeros_like(qa_ref)


def _mod_map(mod_row0, per_batch_mod, tiles_per_seq):
    if per_batch_mod:
        return lambda i: (mod_row0 + i // tiles_per_seq, 0, 0)
    return lambda i: (mod_row0, 0, 0)


def _inproj(x2d, seq, layer, wts, mod, mod_row0, per_batch_mod, dft_cc, dft_cs, cos, sin, tm, with_q):
    t, d = x2d.shape
    nb = t // seq
    tiles_per_seq = seq // tm
    tok = lambda w: pl.BlockSpec((tm, w), lambda i: (i, 0))
    z_spec = pl.BlockSpec((tm, F_WIDTH), lambda i: (i % tiles_per_seq, i // tiles_per_seq))
    pos = pl.BlockSpec((tm, LANES), lambda i: (i % tiles_per_seq, 0))
    heads_bf16 = jax.ShapeDtypeStruct((t, AP_WIDTH), BF16)
    shapes = [
        jax.ShapeDtypeStruct((seq, nb * F_WIDTH), BF16),
        jax.ShapeDtypeStruct((seq, nb * F_WIDTH), BF16),
        jax.ShapeDtypeStruct((t, 2 * MP_WIDTH), F32),
        jax.ShapeDtypeStruct((t // MCH, MP_WIDTH, MCH), BF16),
        jax.ShapeDtypeStruct((t, MP_WIDTH), F32),
        jax.ShapeDtypeStruct((t // MCH, 2 * N_STREAM, MCH), F32),
        heads_bf16, heads_bf16, heads_bf16,
    ]
    vt_spec = pl.BlockSpec((tm // MCH, MP_WIDTH, MCH), lambda i: (i, 0, 0))
    gr_spec = pl.BlockSpec((tm // MCH, 2 * N_STREAM, MCH), lambda i: (i, 0, 0))
    out_specs = [z_spec, z_spec, tok(2 * MP_WIDTH), vt_spec, tok(MP_WIDTH), gr_spec,
                 tok(AP_WIDTH), tok(AP_WIDTH), tok(AP_WIDTH)]
    lay = lambda name: _layer_spec(wts[name], layer)
    return pl.pallas_call(
        functools.partial(_inproj_kernel, with_q=with_q),
        grid=(t // tm,),
        in_specs=[
            tok(d),
            lay("g1"),
            pl.BlockSpec((1, 6, d), _mod_map(mod_row0, per_batch_mod, tiles_per_seq)),
            lay("w_in_p"), lay("w_vt"),
            _const_spec((F_WIDTH, F_WIDTH)),
            _const_spec((F_WIDTH, F_WIDTH)),
            pos, pos,
            lay("gq"), lay("gkv"), lay("wq"), lay("wk"), lay("wv"),
        ],
        out_specs=out_specs,
        out_shape=shapes,
        compiler_params=_cparams(("arbitrary",)),
        name="inproj",
    )(x2d, wts["g1"], mod, wts["w_in_p"], wts["w_vt"], dft_cc, dft_cs, cos, sin,
      wts["gq"], wts["gkv"], wts["wq"], wts["wk"], wts["wv"])


def _fourier_kernel(c_ref, s_ref, zc_ref, zs_ref, o_ref):
    y = _dot(c_ref[...], zc_ref[...]) - _dot(s_ref[...], zs_ref[...])
    for slab in range(o_ref.shape[0]):
        o_ref[slab] = y[:, slab * LANES:(slab + 1) * LANES]


def _fourier(dft_c, dft_s, zc, zs):
    n, cols = zc.shape
    tr = min(n, 512)
    tc = min(cols, 512)
    return pl.pallas_call(
        _fourier_kernel,
        grid=(n // tr, cols // tc),
        in_specs=[
            pl.BlockSpec((tr, n), lambda i, j: (i, 0)),
            pl.BlockSpec((tr, n), lambda i, j: (i, 0)),
            pl.BlockSpec((n, tc), lambda i, j: (0, j)),
            pl.BlockSpec((n, tc), lambda i, j: (0, j)),
        ],
        out_specs=pl.BlockSpec((tc // LANES, tr, LANES), lambda i, j: (j, i, 0)),
        out_shape=jax.ShapeDtypeStruct((cols // LANES, n, LANES), F32),
        compiler_params=_cparams(("arbitrary", "arbitrary")),
        name="fourier",
    )(dft_c, dft_s, zc, zs)


def _fourier4_kernel(tab_ref, twc_ref, tws_ref, zc_ref, zs_ref, o_ref):
    m = zc_ref.shape[0] // 4
    reps = zc_ref.shape[1] // LANES
    k1 = pl.program_id(1)
    c0, c1, c2, c3 = (zc_ref[j * m:(j + 1) * m, :].astype(F32) for j in range(4))
    s0, s1, s2, s3 = (zs_ref[j * m:(j + 1) * m, :].astype(F32) for j in range(4))

    def emit(br, bi, k):
        if k:
            cos = jnp.concatenate([twc_ref[k - 1]] * reps, axis=1)
            sin = jnp.concatenate([tws_ref[k - 1]] * reps, axis=1)
            br, bi = br * cos + bi * sin, bi * cos - br * sin
        stacked = jnp.concatenate([br.astype(BF16), bi.astype(BF16)], axis=0)
        y = _dot(tab_ref[...], stacked)
        for slab in range(reps):
            o_ref[slab, pl.ds(k, m, stride=4), :] = y[:, slab * LANES:(slab + 1) * LANES]

    @pl.when(k1 == 0)
    def _():
        emit((c0 + c2) + (c1 + c3), -((s0 + s2) + (s1 + s3)), 0)

    @pl.when(k1 == 1)
    def _():
        emit((c0 - c2) - (s1 - s3), -(s0 - s2) - (c1 - c3), 1)

    @pl.when(k1 == 2)
    def _():
        emit((c0 + c2) - (c1 + c3), (s1 + s3) - (s0 + s2), 2)

    @pl.when(k1 == 3)
    def _():
        emit((c0 - c2) + (s1 - s3), (c1 - c3) - (s0 - s2), 3)


def _fourier4(tables, zc, zs):
    tab, twc, tws = tables
    n, cols = zc.shape
    m = n // 4
    tc = min(cols, 512)
    nj = cols // tc
    return pl.pallas_call(
        _fourier4_kernel,
        grid=(nj, 4),
        in_specs=[
            _const_spec((m, 2 * m)),
            _const_spec((3, m, LANES)),
            _const_spec((3, m, LANES)),
            pl.BlockSpec((n, tc), lambda j, k: (0, j)),
            pl.BlockSpec((n, tc), lambda j, k: (0, j)),
        ],
        out_specs=pl.BlockSpec((tc // LANES, n, LANES), lambda j, k: (j, 0, 0)),
        out_shape=jax.ShapeDtypeStruct((cols // LANES, n, LANES), F32),
        compiler_params=_cparams(("arbitrary", "arbitrary")),
        name="fourier4",
    )(tab, twc, tws, zc, zs)


def _fourier4_tables(n):
    m = n // 4
    idx = (np.arange(m, dtype=np.int64)[:, None] * np.arange(m, dtype=np.int64)[None, :]) % m
    ang = 2.0 * np.pi * idx.astype(np.float64) / m
    tab = np.concatenate([np.cos(ang), np.sin(ang)], axis=1) / np.sqrt(n)
    theta = 2.0 * np.pi * np.arange(m, dtype=np.float64)[None, :] * np.arange(1, 4, dtype=np.float64)[:, None] / n
    bcast = lambda t: jnp.asarray(np.repeat(t[:, :, None], LANES, axis=2), dtype=F32)
    return jnp.asarray(tab, dtype=F32).astype(BF16), bcast(np.cos(theta)), bcast(np.sin(theta))


CONV_ROWS = 256
CONV_HALO = 8


def _conv_kernel(u_ref, w_ref, q_ref, k_ref, pad_ref):
    n = u_ref.shape[1]
    width = u_ref.shape[2]
    zeros = jnp.zeros((CONV_HALO, width), F32)
    pad_ref[0:CONV_HALO, :] = zeros
    pad_ref[CONV_HALO + n:2 * CONV_HALO + n, :] = zeros
    pad_ref[CONV_HALO:CONV_HALO + n, :] = u_ref[0]
    w = w_ref[0]
    rows = min(CONV_ROWS, n)
    span = rows + 2 * CONV_HALO
    for r in range(n // rows):
        block = pad_ref[r * rows:r * rows + span, :]
        acc = None
        for j in range(K_CONV):
            shift = (K_CONV // 2 - j) % span
            tap = block if shift == 0 else pltpu.roll(block, shift, 0)
            term = tap[CONV_HALO:CONV_HALO + rows, :] * w[j:j + 1, :]
            acc = term if acc is None else acc + term
        act = acc * jax.nn.sigmoid(acc)
        q_ref[0, r * rows:(r + 1) * rows, :] = (act[:, :MP_WIDTH] * (M_HEAD_DIM ** -0.5)).astype(BF16)
        k_ref[0, r * rows:(r + 1) * rows, :] = act[:, MP_WIDTH:].astype(BF16)


def _conv_silu(mqk, layer, wts):
    nb, n, width = mqk.shape
    return pl.pallas_call(
        _conv_kernel,
        grid=(nb,),
        in_specs=[
            pl.BlockSpec((1, n, width), lambda b: (b, 0, 0)),
            _layer_spec(wts["conv_p"], layer),
        ],
        out_specs=[
            pl.BlockSpec((1, n, MP_WIDTH), lambda b: (b, 0, 0)),
            pl.BlockSpec((1, n, MP_WIDTH), lambda b: (b, 0, 0)),
        ],
        out_shape=[
            jax.ShapeDtypeStruct((nb, n, MP_WIDTH), BF16),
            jax.ShapeDtypeStruct((nb, n, MP_WIDTH), BF16),
        ],
        scratch_shapes=[pltpu.VMEM((n + 2 * CONV_HALO, width), F32)],
        compiler_params=_cparams(("arbitrary",)),
        name="conv_silu",
    )(mqk, wts["conv_p"])


def _log_sigmoid(x):
    return jnp.minimum(x, 0.0) - jnp.log(1.0 + jnp.exp(-jnp.abs(x)))


def _exact_dot_01(a, tri_bf16, a_on_left):
    out = None
    for term in _split3(a):
        d = _dot(term, tri_bf16) if a_on_left else _dot(tri_bf16, term)
        out = d if out is None else out + d
    return out


ONE_ROW = M_HEAD_DIM
(F_R, F_B, F_CM, F_TOT, F_CML, F_MP, F_A, F_WI, F_ELD, F_DEC, F_WK, F_HI, F_MID, F_LO) = range(14)
N_FIELDS = 14
OUT_GROUP = 8
SEL_ROWS = 32


def _mlstm_kernel(ql_ref, kl_ref, vtl_ref, grl_ref,
                  qc_ref, kc_ref, vtc_ref, grc_ref,
                  bir_ref, bfr_ref, sel_ref,
                  hl_ref, hc_ref,
                  ct_st, st_sc, rows_sc):
    L = MCH
    nh = M_HEADS
    ncc = qc_ref.shape[1] // L
    ncl = ql_ref.shape[1] // L

    d0 = lax.broadcasted_iota(jnp.int32, (L, L), 0)
    d1 = lax.broadcasted_iota(jnp.int32, (L, L), 1)
    le = d0 <= d1
    ge = d0 >= d1
    tri_le = le.astype(BF16)
    fwd_rows = lax.broadcasted_iota(jnp.int32, (N_STREAM, L), 0) < nh
    feat = lax.broadcasted_iota(jnp.int32, (HEAD_PAD, L), 0)
    one_row = feat == ONE_ROW
    keep_rows = feat < M_HEAD_DIM

    ct_st[...] = jnp.zeros_like(ct_st)

    def field(f, slot0, nc):
        return rows_sc[f, slot0:slot0 + nc].reshape(nc * N_STREAM, L)

    def set_field(f, slot0, nc, val):
        rows_sc[f, slot0:slot0 + nc] = val.reshape(nc, N_STREAM, L)

    def gate_pass(gr_ref, nc, slot0):
        n8 = nc * N_STREAM
        fwd = (lax.broadcasted_iota(jnp.int32, (n8, L), 0) & (N_STREAM - 1)) < nh
        lane = lax.broadcasted_iota(jnp.int32, (n8, L), 1)
        gi = gr_ref[:, 0:N_STREAM, :].reshape(n8, L) + bir_ref[0, 0:n8, :]
        f = _log_sigmoid(gr_ref[:, N_STREAM:2 * N_STREAM, :].reshape(n8, L) + bfr_ref[0, 0:n8, :])
        pre = _exact_dot_01(f, tri_le, a_on_left=True)
        total = jnp.sum(f, axis=1, keepdims=True)
        b = jnp.where(fwd, pre, total - pre + f)
        r = gi - b
        cm = r
        sh = 1
        while sh < L:
            from_left = jnp.where(lane >= sh, pltpu.roll(cm, sh, 1), -jnp.inf)
            from_right = jnp.where(lane < L - sh, pltpu.roll(cm, L - sh, 1), -jnp.inf)
            cm = jnp.maximum(cm, jnp.where(fwd, from_left, from_right))
            sh *= 2
        set_field(F_R, slot0, nc, r)
        set_field(F_B, slot0, nc, b)
        set_field(F_CM, slot0, nc, cm)
        set_field(F_TOT, slot0, nc, jnp.broadcast_to(total, (n8, L)))
        set_field(F_CML, slot0, nc, jnp.broadcast_to(jnp.max(r, axis=1, keepdims=True), (n8, L)))

    def m_scan(nc, slot0, m0):
        def step(j, m):
            sf = slot0 + j
            sb = slot0 + nc - 1 - j
            tot = jnp.where(fwd_rows, rows_sc[F_TOT, sf], rows_sc[F_TOT, sb])
            cml = jnp.where(fwd_rows, rows_sc[F_CML, sf], rows_sc[F_CML, sb])
            rows_sc[F_MP, sf, 0:nh, :] = m[0:nh]
            rows_sc[F_MP, sb, nh:N_STREAM, :] = m[nh:N_STREAM]
            return tot + jnp.maximum(m, cml)

        return lax.fori_loop(0, nc, step, m0)

    def weight_pass(nc, slot0):
        mp, cm, b, r = (field(f, slot0, nc) for f in (F_MP, F_CM, F_B, F_R))
        big = jnp.maximum(mp, field(F_CML, slot0, nc))
        a = -jnp.maximum(mp, cm)
        set_field(F_A, slot0, nc, a)
        set_field(F_WI, slot0, nc, jnp.exp(mp + a))
        set_field(F_ELD, slot0, nc, jnp.exp(a - b))
        set_field(F_DEC, slot0, nc, jnp.exp(mp - big))
        set_field(F_WK, slot0, nc, jnp.exp(r - big))
        for f, term in zip((F_HI, F_MID, F_LO), _split3(r)):
            set_field(f, slot0, nc, term.astype(F32))

    def value_slab(vt_ref, c, hd):
        vt = vt_ref[c, hd * HEAD_PAD:(hd + 1) * HEAD_PAD, :].astype(F32)
        return jnp.where(one_row, 1.0, vt)

    def state_pass(k_ref, vt_ref, nc, slot0):
        def step(j, carry):
            uts = []
            for sidx in range(N_STREAM):
                hd = sidx % nh
                c = j if sidx < nh else nc - 1 - j
                wk = rows_sc[F_WK, slot0 + c, sidx:sidx + 1, :]
                kk = k_ref[0, pl.ds(pl.multiple_of(c * L, L), L), hd * HEAD_PAD:(hd + 1) * HEAD_PAD]
                uts.append(_dot((value_slab(vt_ref, c, hd) * wk).astype(BF16), kk))
            for sidx in range(N_STREAM):
                slot = slot0 + (j if sidx < nh else nc - 1 - j)
                prev = ct_st[sidx]
                st_sc[sidx, slot] = prev.astype(BF16)
                ct_st[sidx] = rows_sc[F_DEC, slot, sidx:sidx + 1, :] * prev + uts[sidx]
            return carry

        lax.fori_loop(0, nc, step, 0)

    def output_pass(q_ref, k_ref, vt_ref, out_ref, nc, slot0):
        head_lanes = [slice(hd * HEAD_PAD, (hd + 1) * HEAD_PAD) for hd in range(nh)]

        def independent_matmuls(c):
            rows = pl.ds(pl.multiple_of(c * L, L), L)
            slot = slot0 + c
            r3 = jnp.concatenate([rows_sc[F_HI, slot], rows_sc[F_MID, slot], rows_sc[F_LO, slot],
                                  jnp.zeros((N_STREAM, L), F32)], axis=0).astype(BF16)
            qs = [q_ref[0, rows, lanes] for lanes in head_lanes]
            s_ts = [_dot_nt(k_ref[0, rows, lanes], q) for lanes, q in zip(head_lanes, qs)]
            inters = [_dot_nt(st_sc[sidx, slot], qs[sidx % nh]) for sidx in range(N_STREAM)]
            r_ts = [_dot_tn(r3, sel_ref[sidx]) for sidx in range(N_STREAM)]
            return s_ts, inters, r_ts

        def finish(c, s_ts, inters, r_ts):
            rows = pl.ds(pl.multiple_of(c * L, L), L)
            slot = slot0 + c
            a_rows = rows_sc[F_A, slot]
            wi_rows = rows_sc[F_WI, slot]
            eld_rows = rows_sc[F_ELD, slot]
            p_ts = []
            for sidx in range(N_STREAM):
                one = slice(sidx, sidx + 1)
                dm = jnp.where(le if sidx < nh else ge, r_ts[sidx] + a_rows[one, :], -jnp.inf)
                p_ts.append((s_ts[sidx % nh] * jnp.exp(dm)).astype(BF16))
            vtas = [value_slab(vt_ref, c, hd).astype(BF16) for hd in range(nh)]
            z_ts = [_dot(vtas[sidx % nh], p_ts[sidx]) for sidx in range(N_STREAM)]
            for hd in range(nh):
                hsum = None
                for sidx in (hd, nh + hd):
                    one = slice(sidx, sidx + 1)
                    z_t = z_ts[sidx] + inters[sidx] * wi_rows[one, :]
                    den = z_t[ONE_ROW:ONE_ROW + 1, :]
                    h_t = z_t * (1.0 / jnp.maximum(jnp.abs(den), eld_rows[one, :]))
                    hsum = h_t if hsum is None else hsum + h_t
                out_ref[0, rows, head_lanes[hd]] = jnp.where(keep_rows, hsum, 0.0).T

        group = min(OUT_GROUP, nc)

        def step(g, carry):
            ahead = independent_matmuls(g * group)
            for u in range(group):
                cur = ahead
                if u + 1 < group:
                    ahead = independent_matmuls(g * group + u + 1)
                finish(g * group + u, *cur)
            return carry

        lax.fori_loop(0, nc // group, step, 0)

    gate_pass(grc_ref, ncc, 0)
    gate_pass(grl_ref, ncl, ncc)
    m1 = m_scan(ncc, 0, jnp.zeros((N_STREAM, L), F32))
    m_scan(ncl, ncc, m1)
    weight_pass(ncc, 0)
    weight_pass(ncl, ncc)
    state_pass(kc_ref, vtc_ref, ncc, 0)
    state_pass(kl_ref, vtl_ref, ncl, ncc)
    output_pass(qc_ref, kc_ref, vtc_ref, hc_ref, ncc, 0)
    output_pass(ql_ref, kl_ref, vtl_ref, hl_ref, ncl, ncc)


def _mlstm(lat, ctx, layer, wts, sel):
    nb, n, _ = lat[0].shape
    nctx = ctx[0].shape[1]
    assert MCH == LANES == HEAD_PAD
    nct = (n + nctx) // MCH
    assert wts["bir"].shape[1] >= max(n, nctx) // MCH * N_STREAM

    def specs(rows):
        nc = rows // MCH
        seq = lambda w: pl.BlockSpec((1, rows, w), lambda b: (b, 0, 0))
        chunked = lambda rows_: pl.BlockSpec((nc, rows_, MCH), lambda b: (b, 0, 0))
        return [seq(MP_WIDTH), seq(MP_WIDTH), chunked(MP_WIDTH), chunked(2 * N_STREAM)]

    out_spec = lambda rows: pl.BlockSpec((1, rows, MP_WIDTH), lambda b: (b, 0, 0))
    return pl.pallas_call(
        _mlstm_kernel,
        grid=(nb,),
        in_specs=specs(n) + specs(nctx) + [
            _layer_spec(wts["bir"], layer), _layer_spec(wts["bfr"], layer),
            _const_spec((N_STREAM, SEL_ROWS, MCH)),
        ],
        out_specs=[out_spec(n), out_spec(nctx)],
        out_shape=[
            jax.ShapeDtypeStruct((nb, n, MP_WIDTH), F32),
            jax.ShapeDtypeStruct((nb, nctx, MP_WIDTH), F32),
        ],
        scratch_shapes=[
            pltpu.VMEM((N_STREAM, HEAD_PAD, HEAD_PAD), F32),
            pltpu.VMEM((N_STREAM, nct, HEAD_PAD, HEAD_PAD), BF16),
            pltpu.VMEM((N_FIELDS, nct, N_STREAM, MCH), F32),
        ],
        compiler_params=_cparams(("arbitrary",)),
        name="mlstm",
    )(*lat, *ctx, wts["bir"], wts["bfr"], sel)


ATTN_SUB = 256


def _attn_kernel(*refs, n_sets):
    q_ref = refs[0]
    kv_refs = refs[1:1 + 2 * n_sets]
    o_ref = refs[1 + 2 * n_sets]
    sub = min(ATTN_SUB, q_ref.shape[1])
    n_sub = q_ref.shape[1] // sub

    def scores_of(t):
        q = q_ref[0, t * sub:(t + 1) * sub, :]
        return [_dot_nt(q, kv_refs[2 * i][0]) for i in range(n_sets)]

    nxt = scores_of(0)
    for t in range(n_sub):
        rows = slice(t * sub, (t + 1) * sub)
        scores = nxt
        if t + 1 < n_sub:
            nxt = scores_of(t + 1)
        m = None
        for s in scores:
            sm = jnp.max(s, axis=-1, keepdims=True)
            m = sm if m is None else jnp.maximum(m, sm)
        acc = None
        den = None
        for i, s in enumerate(scores):
            p = jnp.exp(s - m)
            l = jnp.sum(p, axis=-1, keepdims=True)
            o = _dot(p.astype(BF16), kv_refs[2 * i + 1][0])
            acc = o if acc is None else acc + o
            den = l if den is None else den + l
        o_ref[0, rows, :] = (acc / den).astype(BF16)


def _attention(q, key_sets, tq):
    nb, n, _ = q.shape
    n_sets = len(key_sets)
    in_specs = [pl.BlockSpec((1, tq, HEAD_PAD), lambda b, h, i: (b, i, h))]
    args = [q]
    for k, v in key_sets:
        nk = k.shape[1]
        spec = pl.BlockSpec((1, nk, HEAD_PAD), lambda b, h, i: (b, 0, h))
        in_specs += [spec, spec]
        args += [k, v]
    return pl.pallas_call(
        functools.partial(_attn_kernel, n_sets=n_sets),
        grid=(nb, A_HEADS, n // tq),
        in_specs=in_specs,
        out_specs=pl.BlockSpec((1, tq, HEAD_PAD), lambda b, h, i: (b, i, h)),
        out_shape=jax.ShapeDtypeStruct((nb, n, AP_WIDTH), BF16),
        compiler_params=_cparams(("arbitrary", "arbitrary", "arbitrary")),
        name="attention",
    )(*args)


MLP_CHUNK = 1024


def _out_mlp_kernel(x_ref, yf_ref, hm_ref, mo_ref, ya_ref, mod_ref,
                    gm_ref, g2_ref, gfin_ref, wof_ref, wom_ref, woa_ref, wup_ref, wdn_ref,
                    o_ref, *, final_norm):
    mod = mod_ref[0]
    ga1, sh2, sc2, ga2 = mod[2:3], mod[3:4], mod[4:5], mod[5:6]
    mix = _dot(ya_ref[...], woa_ref[0])
    for slab in range(yf_ref.shape[0]):
        mix = mix + _dot(yf_ref[slab].astype(BF16), wof_ref[0, slab * LANES:(slab + 1) * LANES, :])
    gm = gm_ref[0]
    for hd in range(M_HEADS):
        lanes = slice(hd * HEAD_PAD, (hd + 1) * HEAD_PAD)
        hh = hm_ref[:, lanes]
        ms = jnp.sum(hh * hh, axis=-1, keepdims=True) * (1.0 / M_HEAD_DIM)
        ym = hh * lax.rsqrt(ms + EPS) * gm[:, lanes] * jax.nn.sigmoid(mo_ref[:, lanes])
        mix = mix + _dot(ym.astype(BF16), wom_ref[0, lanes, :])
    x1 = x_ref[...] + ga1 * mix
    h2 = (_rms(x1, g2_ref[0]) * (1.0 + sc2) + sh2).astype(BF16)
    acc = None
    for c in range(wup_ref.shape[2] // MLP_CHUNK):
        cols = slice(c * MLP_CHUNK, (c + 1) * MLP_CHUNK)
        u = jnp.maximum(_dot(h2, wup_ref[0, :, cols]), 0.0)
        d = _dot((u * u).astype(BF16), wdn_ref[0, cols, :])
        acc = d if acc is None else acc + d
    x2 = x1 + ga2 * acc
    if final_norm:
        x2 = _rms(x2, gfin_ref[...])
    o_ref[...] = x2


def _out_mlp(x2d, seq, yf, hm, mo, ya, layer, wts, mod, mod_row0, per_batch_mod, gfin, tm, final_norm):
    t, d = x2d.shape
    tiles_per_seq = seq // tm
    tok = lambda w: pl.BlockSpec((tm, w), lambda i: (i, 0))
    yf_spec = pl.BlockSpec((F_WIDTH // LANES, tm, LANES), lambda i: (i // tiles_per_seq, i % tiles_per_seq, 0))
    names = ("gm", "g2", "wof", "wom", "woa", "wup", "wdn")
    lay = {name: _layer_spec(wts[name], layer) for name in names}
    return pl.pallas_call(
        functools.partial(_out_mlp_kernel, final_norm=final_norm),
        grid=(t // tm,),
        in_specs=[
            tok(d), yf_spec, tok(MP_WIDTH), tok(MP_WIDTH), tok(AP_WIDTH),
            pl.BlockSpec((1, 6, d), _mod_map(mod_row0, per_batch_mod, tiles_per_seq)),
            lay["gm"], lay["g2"], _const_spec((1, d)),
            lay["wof"], lay["wom"], lay["woa"], lay["wup"], lay["wdn"],
        ],
        out_specs=tok(d),
        out_shape=jax.ShapeDtypeStruct((t, d), F32),
        compiler_params=_cparams(("arbitrary",)),
        name="out_mlp",
    )(x2d, yf, hm, mo, ya, mod, wts["gm"], wts["g2"], gfin,
      wts["wof"], wts["wom"], wts["woa"], wts["wup"], wts["wdn"])


def _dft_tables(n):
    idx = (np.arange(n, dtype=np.int64)[:, None] * np.arange(n, dtype=np.int64)[None, :]) % n
    ang = 2.0 * np.pi * idx.astype(np.float64) / n
    scale = 1.0 / np.sqrt(n)
    return np.cos(ang) * scale, np.sin(ang) * scale


def _channel_dft():
    c, s = _dft_tables(F_GROUP_DIM)
    eye = np.eye(F_GROUPS)
    return (jnp.asarray(np.kron(eye, c), dtype=F32).astype(BF16),
            jnp.asarray(np.kron(eye, s), dtype=F32).astype(BF16))


def _position_dft(n):
    c, s = _dft_tables(n)
    return jnp.asarray(c, dtype=F32).astype(BF16), jnp.asarray(s, dtype=F32).astype(BF16)


def _rope_tables(n, rotate):
    cos = np.zeros((n, HEAD_PAD), np.float32)
    sin = np.zeros((n, HEAD_PAD), np.float32)
    cos[:, :A_NOPE + A_ROPE] = 1.0
    if rotate:
        nf = A_ROPE // 4
        t = np.arange(n)
        row = (t // GRID_W).astype(np.float32)
        col = (t % GRID_W).astype(np.float32)
        freqs = (np.float32(ROPE_THETA) ** (-np.arange(nf, dtype=np.float32) / np.float32(nf))).astype(np.float32)
        for seg, pos in enumerate((row, col)):
            ang = pos[:, None] * freqs[None, :]
            c, s = np.cos(ang), np.sin(ang)
            base = A_NOPE + seg * 2 * nf
            cos[:, base:base + nf] = c
            cos[:, base + nf:base + 2 * nf] = c
            sin[:, base:base + nf] = -s
            sin[:, base + nf:base + 2 * nf] = s
    return jnp.asarray(cos), jnp.asarray(sin)


def _pad_heads_cols(w, heads, width):
    lead = w.shape[:-1]
    w = w.reshape(lead + (heads, width))
    w = jnp.pad(w, [(0, 0)] * len(lead) + [(0, 0), (0, HEAD_PAD - width)])
    return w.reshape(lead + (heads * HEAD_PAD,))


def _pad_heads_rows(w, heads, width):
    depth, _, n = w.shape
    w = jnp.pad(w.reshape(depth, heads, width, n), [(0, 0), (0, 0), (0, HEAD_PAD - width), (0, 0)])
    return w.reshape(depth, heads * HEAD_PAD, n)


GATE_I_COLS = np.concatenate([np.arange(M_HEADS), 2 * M_HEADS + np.arange(M_HEADS)])
GATE_F_COLS = GATE_I_COLS + M_HEADS


def _prepare_weights(max_chunks, g_norm1, g_norm2, w_in, b_gates, conv_qk, g_mlstm, g_q_norm, g_kv_norm,
                     w_uq, w_ukv, w_out, w_up, w_down):
    offs = np.cumsum([0, F_WIDTH, M_WIDTH, M_WIDTH, M_WIDTH, M_WIDTH, 4 * M_HEADS, Q_LORA, KV_LORA, A_ROPE])
    part = lambda i: w_in[:, :, offs[i]:offs[i + 1]]
    heads = lambda w: _pad_heads_cols(w, M_HEADS, M_HEAD_DIM)
    w_in_p = jnp.concatenate([
        part(0), heads(part(1)), heads(part(2)), heads(part(4)), part(6), part(7),
        jnp.pad(part(8), [(0, 0), (0, 0), (A_NOPE, LANES - A_NOPE - A_ROPE)]),
    ], axis=2).astype(BF16)
    assert w_in_p.shape[2] == IN_PAD
    gates = part(5)
    w_vt = jnp.concatenate([heads(part(3)), gates[:, :, GATE_I_COLS], gates[:, :, GATE_F_COLS]],
                           axis=2).transpose(0, 2, 1).astype(BF16)

    conv_p = jnp.concatenate([heads(conv_qk[:, :, :M_WIDTH]), heads(conv_qk[:, :, M_WIDTH:])], axis=2)
    conv_p = jnp.pad(conv_p, [(0, 0), (0, 8 - K_CONV), (0, 0)])

    tile_rows = lambda b: jnp.tile(b[:, :, None], (1, max_chunks, 1))
    ukv = w_ukv.reshape(w_ukv.shape[0], KV_LORA, A_HEADS, A_NOPE + A_V)
    pad_kv = lambda w: jnp.pad(w, [(0, 0), (0, 0), (0, 0), (0, HEAD_PAD - w.shape[-1])]).reshape(
        w.shape[0], KV_LORA, AP_WIDTH).astype(BF16)
    vec = lambda g: g[:, None, :]
    return dict(
        g1=vec(g_norm1), g2=vec(g_norm2), gq=vec(g_q_norm), gkv=vec(g_kv_norm),
        w_in_p=w_in_p, w_vt=w_vt, conv_p=conv_p,
        bir=tile_rows(b_gates[:, GATE_I_COLS]), bfr=tile_rows(b_gates[:, GATE_F_COLS]),
        gm=_pad_heads_cols(g_mlstm, M_HEADS, M_HEAD_DIM)[:, None, :],
        wq=_pad_heads_cols(w_uq, A_HEADS, A_NOPE + A_ROPE).astype(BF16),
        wk=pad_kv(ukv[..., :A_NOPE]), wv=pad_kv(ukv[..., A_NOPE:]),
        wof=w_out[:, :F_WIDTH].astype(BF16),
        wom=_pad_heads_rows(w_out[:, F_WIDTH:F_WIDTH + M_WIDTH], M_HEADS, M_HEAD_DIM).astype(BF16),
        woa=_pad_heads_rows(w_out[:, F_WIDTH + M_WIDTH:], A_HEADS, A_V).astype(BF16),
        wup=w_up.astype(BF16), wdn=w_down.astype(BF16),
    )


def _stream_selectors():
    sel = np.zeros((N_STREAM, SEL_ROWS, MCH), np.float32)
    for s in range(N_STREAM):
        for part in range(3):
            sel[s, part * N_STREAM + s, :] = 1.0
    return jnp.asarray(sel, dtype=BF16)


def kernel(x, c, ctx, c_ctx, w_mod, b_mod, g_norm1, g_norm2, w_in, b_gates, conv_qk, g_mlstm,
           g_q_norm, g_kv_norm, w_uq, w_ukv, w_out, w_up, w_down, g_final):
    nb, seq, d = x.shape
    nctx = ctx.shape[1]
    depth = w_mod.shape[0]
    assert d == D_MODEL and seq % 256 == 0 and nctx % MCH == 0
    sel = _stream_selectors()

    tm = min(512, seq)
    tm_mlp = min(512, seq)
    tm_ctx = min(256, nctx)
    tq = min(8 * ATTN_SUB, seq)
    tq_ctx = min(ATTN_SUB, nctx)

    dft_cc, dft_cs = _channel_dft()
    fft_lat = _fourier4_tables(seq)
    dft_ctx = _position_dft(nctx)
    rope_lat = _rope_tables(seq, True)
    rope_ctx = _rope_tables(nctx, False)
    wts = _prepare_weights(max(seq, nctx) // MCH, g_norm1, g_norm2, w_in, b_gates, conv_qk, g_mlstm,
                           g_q_norm, g_kv_norm, w_uq, w_ukv, w_out, w_up, w_down)
    gfin = g_final.reshape(1, d)

    rows = ((nb + 1 + 7) // 8) * 8
    cc = jnp.concatenate([c, c_ctx[None, :], jnp.zeros((rows - nb - 1, d), F32)], axis=0)
    mod_all = _modulation(cc, w_mod, b_mod).reshape(depth * rows, 6, d)

    xl = x.reshape(nb * seq, d)
    xc = ctx.reshape(nb * nctx, d)

    for l in range(depth):
        last = l == depth - 1
        row_lat, row_ctx = l * rows, l * rows + nb

        zc, zs, mqk, vt, mo, gr, q_a, k_a, v_a = _inproj(
            xl, seq, l, wts, mod_all, row_lat, True, dft_cc, dft_cs, *rope_lat, tm, True)
        zc_c, zs_c, mqk_c, vt_c, mo_c, gr_c, q_ac, k_ac, v_ac = _inproj(
            xc, nctx, l, wts, mod_all, row_ctx, False, dft_cc, dft_cs, *rope_ctx, tm_ctx, not last)

        yf = _fourier4(fft_lat, zc, zs)

        def mlstm_inputs(mqk_s, vt_s, gr_s, n):
            q_s, k_s = _conv_silu(mqk_s.reshape(nb, n, 2 * MP_WIDTH), l, wts)
            return (q_s, k_s, vt_s, gr_s)

        hm, hm_c = _mlstm(mlstm_inputs(mqk, vt, gr, seq), mlstm_inputs(mqk_c, vt_c, gr_c, nctx), l, wts, sel)

        b3 = lambda a, n: a.reshape(nb, n, AP_WIDTH)
        keys_ctx = (b3(k_ac, nctx), b3(v_ac, nctx))
        ya = _attention(b3(q_a, seq), [(b3(k_a, seq), b3(v_a, seq)), keys_ctx], tq)

        xl = _out_mlp(xl, seq, yf, hm.reshape(nb * seq, MP_WIDTH), mo, ya.reshape(nb * seq, AP_WIDTH),
                      l, wts, mod_all, row_lat, True, gfin, tm_mlp, last)

        if not last:
            yf_c = _fourier(*dft_ctx, zc_c, zs_c)
            ya_c = _attention(b3(q_ac, nctx), [keys_ctx], tq_ctx)
            xc = _out_mlp(xc, nctx, yf_c, hm_c.reshape(nb * nctx, MP_WIDTH), mo_c,
                          ya_c.reshape(nb * nctx, AP_WIDTH), l, wts, mod_all, row_ctx, False, gfin, tm_ctx, False)

    return xl.reshape(nb, seq, d)
```

```python
import functools

import numpy as np
import jax
import jax.numpy as jnp
from jax import lax
from jax.experimental import pallas as pl
from jax.experimental.pallas import tpu as pltpu

D_MODEL = 1024
GRID_W = 64
EPS = 1e-6
F_GROUPS = 4
F_GROUP_DIM = D_MODEL // 16
F_WIDTH = F_GROUPS * F_GROUP_DIM
M_HEADS = 4
M_HEAD_DIM = 3 * D_MODEL // 32
M_WIDTH = M_HEADS * M_HEAD_DIM
K_CONV = 5
A_HEADS = 4
A_NOPE = 64
A_ROPE = 32
A_V = 3 * D_MODEL // 32
Q_LORA = D_MODEL // 4
KV_LORA = D_MODEL // 8
ROPE_THETA = 10000.0
MLP_HIDDEN = 4 * D_MODEL

LANES = 128
HEAD_PAD = 128
MP_WIDTH = M_HEADS * HEAD_PAD
AP_WIDTH = A_HEADS * HEAD_PAD
VMEM_LIMIT = 56 * 1024 * 1024
MCH = 128
N_STREAM = 2 * M_HEADS

OFF_PF = 0
OFF_MQ = OFF_PF + F_WIDTH
OFF_MK = OFF_MQ + MP_WIDTH
OFF_MO = OFF_MK + MP_WIDTH
OFF_CQ = OFF_MO + MP_WIDTH
OFF_CKV = OFF_CQ + Q_LORA
OFF_KR = OFF_CKV + KV_LORA
IN_PAD = OFF_KR + LANES

BF16 = jnp.bfloat16
F32 = jnp.float32


def _cparams(sem):
    return pltpu.CompilerParams(dimension_semantics=sem, vmem_limit_bytes=VMEM_LIMIT)


def _const_spec(shape):
    nd = len(shape)
    return pl.BlockSpec(shape, lambda *_: (0,) * nd, pipeline_mode=pl.Buffered(1))


def _layer_spec(arr, layer):
    nd = arr.ndim
    return pl.BlockSpec((1,) + arr.shape[1:], lambda *_: (layer,) + (0,) * (nd - 1), pipeline_mode=pl.Buffered(1))


def _split3(a):
    hi = a.astype(BF16)
    r1 = a - hi.astype(F32)
    mid = r1.astype(BF16)
    lo = (r1 - mid.astype(F32)).astype(BF16)
    return hi, mid, lo


def _dot(a, b):
    return jnp.dot(a, b, preferred_element_type=F32)


def _dot_nt(a, b):
    return lax.dot_general(a, b, (((1,), (1,)), ((), ())), preferred_element_type=F32)


def _dot_tn(a, b):
    return lax.dot_general(a, b, (((0,), (0,)), ((), ())), preferred_element_type=F32)


def _rms(x, g):
    return x * lax.rsqrt(jnp.mean(x * x, axis=-1, keepdims=True) + EPS) * g


def _mod_kernel(c_ref, w_ref, b_ref, o_ref):
    c = c_ref[...]
    a = c * jax.nn.sigmoid(c)
    a_hi = a.astype(BF16)
    a_lo = (a - a_hi.astype(F32)).astype(BF16)
    w = w_ref[0]
    w_hi = w.astype(BF16)
    w_lo = (w - w_hi.astype(F32)).astype(BF16)
    acc = _dot(a_hi, w_hi) + _dot(a_hi, w_lo) + _dot(a_lo, w_hi)
    o_ref[0] = acc + b_ref[0]


def _modulation(cc, w_mod, b_mod):
    depth, d, n = w_mod.shape
    rows = cc.shape[0]
    tn = 1536
    return pl.pallas_call(
        _mod_kernel,
        grid=(depth, n // tn),
        in_specs=[
            pl.BlockSpec((rows, d), lambda l, j: (0, 0)),
            pl.BlockSpec((1, d, tn), lambda l, j: (l, 0, j)),
            pl.BlockSpec((1, 1, tn), lambda l, j: (l, 0, j)),
        ],
        out_specs=pl.BlockSpec((1, rows, tn), lambda l, j: (l, 0, j)),
        out_shape=jax.ShapeDtypeStruct((depth, rows, n), F32),
        compiler_params=_cparams(("arbitrary", "arbitrary")),
        name="modulation",
    )(cc, w_mod, b_mod.reshape(depth, 1, n))


def _rope(x, cos, sin, first_half):
    half = A_ROPE // 4
    partner = jnp.where(first_half, pltpu.roll(x, LANES - half, 1), pltpu.roll(x, half, 1))
    return x * cos + partner * sin


def _inproj_kernel(x_ref, g_ref, mod_ref, w_ref, wvt_ref, cc_ref, cs_ref,
                   cos_ref, sin_ref, gq_ref, gkv_ref, wq_ref, wk_ref, wv_ref,
                   zc_ref, zs_ref, mqk_ref, vt_ref, mo_ref, gr_ref, qa_ref, ka_ref, va_ref, *, with_q):
    x = x_ref[...]
    mod = mod_ref[0]
    h = _rms(x, g_ref[0]) * (1.0 + mod[1:2]) + mod[0:1]
    hb = h.astype(BF16)

    def proj(off, width):
        return _dot(hb, w_ref[0, :, off:off + width])

    ckv_kr = proj(OFF_CKV, 2 * LANES)
    cq = proj(OFF_CQ, Q_LORA) if with_q else None
    pf = proj(OFF_PF, F_WIDTH).astype(BF16)

    mqk_ref[...] = proj(OFF_MQ, 2 * MP_WIDTH)

    cos = cos_ref[...]
    sin = sin_ref[...]
    lane = lax.broadcasted_iota(jnp.int32, cos.shape, 1)
    first_half = ((lane - A_NOPE) & (A_ROPE // 2 - 1)) < A_ROPE // 4
    kvn = _rms(ckv_kr[:, :KV_LORA], gkv_ref[0]).astype(BF16)
    k_rope = _rope(ckv_kr[:, KV_LORA:], cos, sin, first_half)
    for hd in range(A_HEADS):
        lanes = slice(hd * HEAD_PAD, (hd + 1) * HEAD_PAD)
        ka_ref[:, lanes] = (_dot(kvn, wk_ref[0, :, lanes]) + k_rope).astype(BF16)
        va_ref[:, lanes] = _dot(kvn, wv_ref[0, :, lanes]).astype(BF16)
    if with_q:
        qn = _rms(cq, gq_ref[0]).astype(BF16)
        q_raw = [_dot(qn, wq_ref[0, :, hd * HEAD_PAD:(hd + 1) * HEAD_PAD]) for hd in range(A_HEADS)]
    zc_ref[...] = _dot(pf, cc_ref[...]).astype(BF16)
    zs_ref[...] = _dot(pf, cs_ref[...]).astype(BF16)

    vg = _dot_nt(wvt_ref[0], hb)
    vt = vg[:MP_WIDTH].astype(BF16)
    for j in range(vt_ref.shape[0]):
        vt_ref[j] = vt[:, j * MCH:(j + 1) * MCH]
        gr_ref[j] = vg[MP_WIDTH:, j * MCH:(j + 1) * MCH]
    mo_ref[...] = proj(OFF_MO, MP_WIDTH)

    if with_q:
        scale = (A_NOPE + A_ROPE) ** -0.5
        for hd in range(A_HEADS):
            q = _rope(q_raw[hd], cos, sin, first_half)
            qa_ref[:, hd * HEAD_PAD:(hd + 1) * HEAD_PAD] = (q * scale).astype(BF16)
    else:
        qa_ref[...] = jnp.zeros_like(qa_ref)


def _mod_map(mod_row0, per_batch_mod, tiles_per_seq):
    if per_batch_mod:
        return lambda i: (mod_row0 + i // tiles_per_seq, 0, 0)
    return lambda i: (mod_row0, 0, 0)


def _inproj(x2d, seq, layer, wts, mod, mod_row0, per_batch_mod, dft_cc, dft_cs, cos, sin, tm, with_q):
    t, d = x2d.shape
    nb = t // seq
    tiles_per_seq = seq // tm
    tok = lambda w: pl.BlockSpec((tm, w), lambda i: (i, 0))
    z_spec = pl.BlockSpec((tm, F_WIDTH), lambda i: (i % tiles_per_seq, i // tiles_per_seq))
    pos = pl.BlockSpec((tm, LANES), lambda i: (i % tiles_per_seq, 0))
    heads_bf16 = jax.ShapeDtypeStruct((t, AP_WIDTH), BF16)
    shapes = [
        jax.ShapeDtypeStruct((seq, nb * F_WIDTH), BF16),
        jax.ShapeDtypeStruct((seq, nb * F_WIDTH), BF16),
        jax.ShapeDtypeStruct((t, 2 * MP_WIDTH), F32),
        jax.ShapeDtypeStruct((t // MCH, MP_WIDTH, MCH), BF16),
        jax.ShapeDtypeStruct((t, MP_WIDTH), F32),
        jax.ShapeDtypeStruct((t // MCH, 2 * N_STREAM, MCH), F32),
        heads_bf16, heads_bf16, heads_bf16,
    ]
    vt_spec = pl.BlockSpec((tm // MCH, MP_WIDTH, MCH), lambda i: (i, 0, 0))
    gr_spec = pl.BlockSpec((tm // MCH, 2 * N_STREAM, MCH), lambda i: (i, 0, 0))
    out_specs = [z_spec, z_spec, tok(2 * MP_WIDTH), vt_spec, tok(MP_WIDTH), gr_spec,
                 tok(AP_WIDTH), tok(AP_WIDTH), tok(AP_WIDTH)]
    lay = lambda name: _layer_spec(wts[name], layer)
    return pl.pallas_call(
        functools.partial(_inproj_kernel, with_q=with_q),
        grid=(t // tm,),
        in_specs=[
            tok(d),
            lay("g1"),
            pl.BlockSpec((1, 6, d), _mod_map(mod_row0, per_batch_mod, tiles_per_seq)),
            lay("w_in_p"), lay("w_vt"),
            _const_spec((F_WIDTH, F_WIDTH)),
            _const_spec((F_WIDTH, F_WIDTH)),
            pos, pos,
            lay("gq"), lay("gkv"), lay("wq"), lay("wk"), lay("wv"),
        ],
        out_specs=out_specs,
        out_shape=shapes,
        compiler_params=_cparams(("arbitrary",)),
        name="inproj",
    )(x2d, wts["g1"], mod, wts["w_in_p"], wts["w_vt"], dft_cc, dft_cs, cos, sin,
      wts["gq"], wts["gkv"], wts["wq"], wts["wk"], wts["wv"])


def _fourier_kernel(c_ref, s_ref, zc_ref, zs_ref, o_ref):
    y = _dot(c_ref[...], zc_ref[...]) - _dot(s_ref[...], zs_ref[...])
    for slab in range(o_ref.shape[0]):
        o_ref[slab] = y[:, slab * LANES:(slab + 1) * LANES]


def _fourier(dft_c, dft_s, zc, zs):
    n, cols = zc.shape
    tr = min(n, 512)
    tc = min(cols, 512)
    return pl.pallas_call(
        _fourier_kernel,
        grid=(n // tr, cols // tc),
        in_specs=[
            pl.BlockSpec((tr, n), lambda i, j: (i, 0)),
            pl.BlockSpec((tr, n), lambda i, j: (i, 0)),
            pl.BlockSpec((n, tc), lambda i, j: (0, j)),
            pl.BlockSpec((n, tc), lambda i, j: (0, j)),
        ],
        out_specs=pl.BlockSpec((tc // LANES, tr, LANES), lambda i, j: (j, i, 0)),
        out_shape=jax.ShapeDtypeStruct((cols // LANES, n, LANES), F32),
        compiler_params=_cparams(("arbitrary", "arbitrary")),
        name="fourier",
    )(dft_c, dft_s, zc, zs)


def _fourier4_kernel(tab_ref, twc_ref, tws_ref, zc_ref, zs_ref, o_ref):
    m = zc_ref.shape[0] // 4
    reps = zc_ref.shape[1] // LANES
    k1 = pl.program_id(1)
    c0, c1, c2, c3 = (zc_ref[j * m:(j + 1) * m, :].astype(F32) for j in range(4))
    s0, s1, s2, s3 = (zs_ref[j * m:(j + 1) * m, :].astype(F32) for j in range(4))

    def emit(br, bi, k):
        if k:
            cos = jnp.concatenate([twc_ref[k - 1]] * reps, axis=1)
            sin = jnp.concatenate([tws_ref[k - 1]] * reps, axis=1)
            br, bi = br * cos + bi * sin, bi * cos - br * sin
        stacked = jnp.concatenate([br.astype(BF16), bi.astype(BF16)], axis=0)
        y = _dot(tab_ref[...], stacked)
        for slab in range(reps):
            o_ref[slab, pl.ds(k, m, stride=4), :] = y[:, slab * LANES:(slab + 1) * LANES]

    @pl.when(k1 == 0)
    def _():
        emit((c0 + c2) + (c1 + c3), -((s0 + s2) + (s1 + s3)), 0)

    @pl.when(k1 == 1)
    def _():
        emit((c0 - c2) - (s1 - s3), -(s0 - s2) - (c1 - c3), 1)

    @pl.when(k1 == 2)
    def _():
        emit((c0 + c2) - (c1 + c3), (s1 + s3) - (s0 + s2), 2)

    @pl.when(k1 == 3)
    def _():
        emit((c0 - c2) + (s1 - s3), (c1 - c3) - (s0 - s2), 3)


def _fourier4(tables, zc, zs):
    tab, twc, tws = tables
    n, cols = zc.shape
    m = n // 4
    tc = min(cols, 512)
    nj = cols // tc
    return pl.pallas_call(
        _fourier4_kernel,
        grid=(nj, 4),
        in_specs=[
            _const_spec((m, 2 * m)),
            _const_spec((3, m, LANES)),
            _const_spec((3, m, LANES)),
            pl.BlockSpec((n, tc), lambda j, k: (0, j)),
            pl.BlockSpec((n, tc), lambda j, k: (0, j)),
        ],
        out_specs=pl.BlockSpec((tc // LANES, n, LANES), lambda j, k: (j, 0, 0)),
        out_shape=jax.ShapeDtypeStruct((cols // LANES, n, LANES), F32),
        compiler_params=_cparams(("arbitrary", "arbitrary")),
        name="fourier4",
    )(tab, twc, tws, zc, zs)


def _fourier4_tables(n):
    m = n // 4
    idx = (np.arange(m, dtype=np.int64)[:, None] * np.arange(m, dtype=np.int64)[None, :]) % m
    ang = 2.0 * np.pi * idx.astype(np.float64) / m
    tab = np.concatenate([np.cos(ang), np.sin(ang)], axis=1) / np.sqrt(n)
    theta = 2.0 * np.pi * np.arange(m, dtype=np.float64)[None, :] * np.arange(1, 4, dtype=np.float64)[:, None] / n
    bcast = lambda t: jnp.asarray(np.repeat(t[:, :, None], LANES, axis=2), dtype=F32)
    return jnp.asarray(tab, dtype=F32).astype(BF16), bcast(np.cos(theta)), bcast(np.sin(theta))


CONV_ROWS = 256
CONV_HALO = 8


def _conv_kernel(u_ref, w_ref, q_ref, k_ref, pad_ref):
    n = u_ref.shape[1]
    width = u_ref.shape[2]
    zeros = jnp.zeros((CONV_HALO, width), F32)
    pad_ref[0:CONV_HALO, :] = zeros
    pad_ref[CONV_HALO + n:2 * CONV_HALO + n, :] = zeros
    pad_ref[CONV_HALO:CONV_HALO + n, :] = u_ref[0]
    w = w_ref[0]
    rows = min(CONV_ROWS, n)
    span = rows + 2 * CONV_HALO
    for r in range(n // rows):
        block = pad_ref[r * rows:r * rows + span, :]
        acc = None
        for j in range(K_CONV):
            shift = (K_CONV // 2 - j) % span
            tap = block if shift == 0 else pltpu.roll(block, shift, 0)
            term = tap[CONV_HALO:CONV_HALO + rows, :] * w[j:j + 1, :]
            acc = term if acc is None else acc + term
        act = acc * jax.nn.sigmoid(acc)
        q_ref[0, r * rows:(r + 1) * rows, :] = (act[:, :MP_WIDTH] * (M_HEAD_DIM ** -0.5)).astype(BF16)
        k_ref[0, r * rows:(r + 1) * rows, :] = act[:, MP_WIDTH:].astype(BF16)


def _conv_silu(mqk, layer, wts):
    nb, n, width = mqk.shape
    return pl.pallas_call(
        _conv_kernel,
        grid=(nb,),
        in_specs=[
            pl.BlockSpec((1, n, width), lambda b: (b, 0, 0)),
            _layer_spec(wts["conv_p"], layer),
        ],
        out_specs=[
            pl.BlockSpec((1, n, MP_WIDTH), lambda b: (b, 0, 0)),
            pl.BlockSpec((1, n, MP_WIDTH), lambda b: (b, 0, 0)),
        ],
        out_shape=[
            jax.ShapeDtypeStruct((nb, n, MP_WIDTH), BF16),
            jax.ShapeDtypeStruct((nb, n, MP_WIDTH), BF16),
        ],
        scratch_shapes=[pltpu.VMEM((n + 2 * CONV_HALO, width), F32)],
        compiler_params=_cparams(("arbitrary",)),
        name="conv_silu",
    )(mqk, wts["conv_p"])


def _log_sigmoid(x):
    return jnp.minimum(x, 0.0) - jnp.log(1.0 + jnp.exp(-jnp.abs(x)))


def _exact_dot_01(a, tri_bf16, a_on_left):
    out = None
    for term in _split3(a):
        d = _dot(term, tri_bf16) if a_on_left else _dot(tri_bf16, term)
        out = d if out is None else out + d
    return out


ONE_ROW = M_HEAD_DIM
(F_R, F_B, F_CM, F_TOT, F_CML, F_MP, F_A, F_WI, F_ELD, F_DEC, F_WK, F_HI, F_MID, F_LO) = range(14)
N_FIELDS = 14
OUT_GROUP = 8
SEL_ROWS = 32


def _mlstm_kernel(ql_ref, kl_ref, vtl_ref, grl_ref,
                  qc_ref, kc_ref, vtc_ref, grc_ref,
                  bir_ref, bfr_ref, sel_ref,
                  hl_ref, hc_ref,
                  ct_st, st_sc, rows_sc):
    L = MCH
    nh = M_HEADS
    ncc = qc_ref.shape[1] // L
    ncl = ql_ref.shape[1] // L

    d0 = lax.broadcasted_iota(jnp.int32, (L, L), 0)
    d1 = lax.broadcasted_iota(jnp.int32, (L, L), 1)
    le = d0 <= d1
    ge = d0 >= d1
    tri_le = le.astype(BF16)
    fwd_rows = lax.broadcasted_iota(jnp.int32, (N_STREAM, L), 0) < nh
    feat = lax.broadcasted_iota(jnp.int32, (HEAD_PAD, L), 0)
    one_row = feat == ONE_ROW
    keep_rows = feat < M_HEAD_DIM

    ct_st[...] = jnp.zeros_like(ct_st)

    def field(f, slot0, nc):
        return rows_sc[f, slot0:slot0 + nc].reshape(nc * N_STREAM, L)

    def set_field(f, slot0, nc, val):
        rows_sc[f, slot0:slot0 + nc] = val.reshape(nc, N_STREAM, L)

    def gate_pass(gr_ref, nc, slot0):
        n8 = nc * N_STREAM
        fwd = (lax.broadcasted_iota(jnp.int32, (n8, L), 0) & (N_STREAM - 1)) < nh
        lane = lax.broadcasted_iota(jnp.int32, (n8, L), 1)
        gi = gr_ref[:, 0:N_STREAM, :].reshape(n8, L) + bir_ref[0, 0:n8, :]
        f = _log_sigmoid(gr_ref[:, N_STREAM:2 * N_STREAM, :].reshape(n8, L) + bfr_ref[0, 0:n8, :])
        pre = _exact_dot_01(f, tri_le, a_on_left=True)
        total = jnp.sum(f, axis=1, keepdims=True)
        b = jnp.where(fwd, pre, total - pre + f)
        r = gi - b
        cm = r
        sh = 1
        while sh < L:
            from_left = jnp.where(lane >= sh, pltpu.roll(cm, sh, 1), -jnp.inf)
            from_right = jnp.where(lane < L - sh, pltpu.roll(cm, L - sh, 1), -jnp.inf)
            cm = jnp.maximum(cm, jnp.where(fwd, from_left, from_right))
            sh *= 2
        set_field(F_R, slot0, nc, r)
        set_field(F_B, slot0, nc, b)
        set_field(F_CM, slot0, nc, cm)
        set_field(F_TOT, slot0, nc, jnp.broadcast_to(total, (n8, L)))
        set_field(F_CML, slot0, nc, jnp.broadcast_to(jnp.max(r, axis=1, keepdims=True), (n8, L)))

    def m_scan(nc, slot0, m0):
        def step(j, m):
            sf = slot0 + j
            sb = slot0 + nc - 1 - j
            tot = jnp.where(fwd_rows, rows_sc[F_TOT, sf], rows_sc[F_TOT, sb])
            cml = jnp.where(fwd_rows, rows_sc[F_CML, sf], rows_sc[F_CML, sb])
            rows_sc[F_MP, sf, 0:nh, :] = m[0:nh]
            rows_sc[F_MP, sb, nh:N_STREAM, :] = m[nh:N_STREAM]
            return tot + jnp.maximum(m, cml)

        return lax.fori_loop(0, nc, step, m0)

    def weight_pass(nc, slot0):
        mp, cm, b, r = (field(f, slot0, nc) for f in (F_MP, F_CM, F_B, F_R))
        big = jnp.maximum(mp, field(F_CML, slot0, nc))
        a = -jnp.maximum(mp, cm)
        set_field(F_A, slot0, nc, a)
        set_field(F_WI, slot0, nc, jnp.exp(mp + a))
        set_field(F_ELD, slot0, nc, jnp.exp(a - b))
        set_field(F_DEC, slot0, nc, jnp.exp(mp - big))
        set_field(F_WK, slot0, nc, jnp.exp(r - big))
        for f, term in zip((F_HI, F_MID, F_LO), _split3(r)):
            set_field(f, slot0, nc, term.astype(F32))

    def value_slab(vt_ref, c, hd):
        vt = vt_ref[c, hd * HEAD_PAD:(hd + 1) * HEAD_PAD, :].astype(F32)
        return jnp.where(one_row, 1.0, vt)

    def state_pass(k_ref, vt_ref, nc, slot0):
        def step(j, carry):
            uts = []
            for sidx in range(N_STREAM):
                hd = sidx % nh
                c = j if sidx < nh else nc - 1 - j
                wk = rows_sc[F_WK, slot0 + c, sidx:sidx + 1, :]
                kk = k_ref[0, pl.ds(pl.multiple_of(c * L, L), L), hd * HEAD_PAD:(hd + 1) * HEAD_PAD]
                uts.append(_dot((value_slab(vt_ref, c, hd) * wk).astype(BF16), kk))
            for sidx in range(N_STREAM):
                slot = slot0 + (j if sidx < nh else nc - 1 - j)
                prev = ct_st[sidx]
                st_sc[sidx, slot] = prev.astype(BF16)
                ct_st[sidx] = rows_sc[F_DEC, slot, sidx:sidx + 1, :] * prev + uts[sidx]
            return carry

        lax.fori_loop(0, nc, step, 0)

    def output_pass(q_ref, k_ref, vt_ref, out_ref, nc, slot0):
        head_lanes = [slice(hd * HEAD_PAD, (hd + 1) * HEAD_PAD) for hd in range(nh)]

        def independent_matmuls(c):
            rows = pl.ds(pl.multiple_of(c * L, L), L)
            slot = slot0 + c
            r3 = jnp.concatenate([rows_sc[F_HI, slot], rows_sc[F_MID, slot], rows_sc[F_LO, slot],
                                  jnp.zeros((N_STREAM, L), F32)], axis=0).astype(BF16)
            qs = [q_ref[0, rows, lanes] for lanes in head_lanes]
            s_ts = [_dot_nt(k_ref[0, rows, lanes], q) for lanes, q in zip(head_lanes, qs)]
            inters = [_dot_nt(st_sc[sidx, slot], qs[sidx % nh]) for sidx in range(N_STREAM)]
            r_all = _dot_tn(r3, sel_ref[...])
            r_ts = [r_all[:, sidx * L:(sidx + 1) * L] for sidx in range(N_STREAM)]
            return s_ts, inters, r_ts

        def finish(c, s_ts, inters, r_ts):
            rows = pl.ds(pl.multiple_of(c * L, L), L)
            slot = slot0 + c
            a_rows = rows_sc[F_A, slot]
            wi_rows = rows_sc[F_WI, slot]
            eld_rows = rows_sc[F_ELD, slot]
            p_ts = []
            for sidx in range(N_STREAM):
                one = slice(sidx, sidx + 1)
                dm = jnp.where(le if sidx < nh else ge, r_ts[sidx] + a_rows[one, :], -jnp.inf)
                p_ts.append((s_ts[sidx % nh] * jnp.exp(dm)).astype(BF16))
            z_pairs = [_dot(value_slab(vt_ref, c, hd).astype(BF16),
                            jnp.concatenate([p_ts[hd], p_ts[nh + hd]], axis=1)) for hd in range(nh)]
            for hd in range(nh):
                hsum = None
                for dr, sidx in enumerate((hd, nh + hd)):
                    one = slice(sidx, sidx + 1)
                    z_t = z_pairs[hd][:, dr * L:(dr + 1) * L] + inters[sidx] * wi_rows[one, :]
                    den = z_t[ONE_ROW:ONE_ROW + 1, :]
                    h_t = z_t * (1.0 / jnp.maximum(jnp.abs(den), eld_rows[one, :]))
                    hsum = h_t if hsum is None else hsum + h_t
                out_ref[0, rows, head_lanes[hd]] = jnp.where(keep_rows, hsum, 0.0).T

        group = min(OUT_GROUP, nc)

        def step(g, carry):
            ahead = independent_matmuls(g * group)
            for u in range(group):
                cur = ahead
                if u + 1 < group:
                    ahead = independent_matmuls(g * group + u + 1)
                finish(g * group + u, *cur)
            return carry

        lax.fori_loop(0, nc // group, step, 0)

    gate_pass(grc_ref, ncc, 0)
    gate_pass(grl_ref, ncl, ncc)
    m1 = m_scan(ncc, 0, jnp.zeros((N_STREAM, L), F32))
    m_scan(ncl, ncc, m1)
    weight_pass(ncc, 0)
    weight_pass(ncl, ncc)
    state_pass(kc_ref, vtc_ref, ncc, 0)
    state_pass(kl_ref, vtl_ref, ncl, ncc)
    output_pass(qc_ref, kc_ref, vtc_ref, hc_ref, ncc, 0)
    output_pass(ql_ref, kl_ref, vtl_ref, hl_ref, ncl, ncc)


def _mlstm(lat, ctx, layer, wts, sel):
    nb, n, _ = lat[0].shape
    nctx = ctx[0].shape[1]
    assert MCH == LANES == HEAD_PAD
    nct = (n + nctx) // MCH
    assert wts["bir"].shape[1] >= max(n, nctx) // MCH * N_STREAM

    def specs(rows):
        nc = rows // MCH
        seq = lambda w: pl.BlockSpec((1, rows, w), lambda b: (b, 0, 0))
        chunked = lambda rows_: pl.BlockSpec((nc, rows_, MCH), lambda b: (b, 0, 0))
        return [seq(MP_WIDTH), seq(MP_WIDTH), chunked(MP_WIDTH), chunked(2 * N_STREAM)]

    out_spec = lambda rows: pl.BlockSpec((1, rows, MP_WIDTH), lambda b: (b, 0, 0))
    return pl.pallas_call(
        _mlstm_kernel,
        grid=(nb,),
        in_specs=specs(n) + specs(nctx) + [
            _layer_spec(wts["bir"], layer), _layer_spec(wts["bfr"], layer),
            _const_spec((SEL_ROWS, N_STREAM * MCH)),
        ],
        out_specs=[out_spec(n), out_spec(nctx)],
        out_shape=[
            jax.ShapeDtypeStruct((nb, n, MP_WIDTH), F32),
            jax.ShapeDtypeStruct((nb, nctx, MP_WIDTH), F32),
        ],
        scratch_shapes=[
            pltpu.VMEM((N_STREAM, HEAD_PAD, HEAD_PAD), F32),
            pltpu.VMEM((N_STREAM, nct, HEAD_PAD, HEAD_PAD), BF16),
            pltpu.VMEM((N_FIELDS, nct, N_STREAM, MCH), F32),
        ],
        compiler_params=_cparams(("arbitrary",)),
        name="mlstm",
    )(*lat, *ctx, wts["bir"], wts["bfr"], sel)


ATTN_SUB = 256


def _attn_kernel(*refs, n_sets):
    q_ref = refs[0]
    kv_refs = refs[1:1 + 2 * n_sets]
    o_ref = refs[1 + 2 * n_sets]
    sub = min(ATTN_SUB, q_ref.shape[1])
    n_sub = q_ref.shape[1] // sub

    def scores_of(t):
        q = q_ref[0, t * sub:(t + 1) * sub, :]
        return [_dot_nt(q, kv_refs[2 * i][0]) for i in range(n_sets)]

    nxt = scores_of(0)
    for t in range(n_sub):
        rows = slice(t * sub, (t + 1) * sub)
        scores = nxt
        if t + 1 < n_sub:
            nxt = scores_of(t + 1)
        m = None
        for s in scores:
            sm = jnp.max(s, axis=-1, keepdims=True)
            m = sm if m is None else jnp.maximum(m, sm)
        acc = None
        den = None
        for i, s in enumerate(scores):
            p = jnp.exp(s - m)
            l = jnp.sum(p, axis=-1, keepdims=True)
            o = _dot(p.astype(BF16), kv_refs[2 * i + 1][0])
            acc = o if acc is None else acc + o
            den = l if den is None else den + l
        o_ref[0, rows, :] = (acc / den).astype(BF16)


def _attention(q, key_sets, tq):
    nb, n, _ = q.shape
    n_sets = len(key_sets)
    in_specs = [pl.BlockSpec((1, tq, HEAD_PAD), lambda b, h, i: (b, i, h))]
    args = [q]
    for k, v in key_sets:
        nk = k.shape[1]
        spec = pl.BlockSpec((1, nk, HEAD_PAD), lambda b, h, i: (b, 0, h))
        in_specs += [spec, spec]
        args += [k, v]
    return pl.pallas_call(
        functools.partial(_attn_kernel, n_sets=n_sets),
        grid=(nb, A_HEADS, n // tq),
        in_specs=in_specs,
        out_specs=pl.BlockSpec((1, tq, HEAD_PAD), lambda b, h, i: (b, i, h)),
        out_shape=jax.ShapeDtypeStruct((nb, n, AP_WIDTH), BF16),
        compiler_params=_cparams(("arbitrary", "arbitrary", "arbitrary")),
        name="attention",
    )(*args)


MLP_CHUNK = 1024


def _out_mlp_kernel(x_ref, yf_ref, hm_ref, mo_ref, ya_ref, mod_ref,
                    gm_ref, g2_ref, gfin_ref, wof_ref, wom_ref, woa_ref, wup_ref, wdn_ref,
                    o_ref, *, final_norm):
    mod = mod_ref[0]
    ga1, sh2, sc2, ga2 = mod[2:3], mod[3:4], mod[4:5], mod[5:6]
    mix = _dot(ya_ref[...], woa_ref[0])
    for slab in range(yf_ref.shape[0]):
        mix = mix + _dot(yf_ref[slab].astype(BF16), wof_ref[0, slab * LANES:(slab + 1) * LANES, :])
    gm = gm_ref[0]
    for hd in range(M_HEADS):
        lanes = slice(hd * HEAD_PAD, (hd + 1) * HEAD_PAD)
        hh = hm_ref[:, lanes]
        ms = jnp.sum(hh * hh, axis=-1, keepdims=True) * (1.0 / M_HEAD_DIM)
        ym = hh * lax.rsqrt(ms + EPS) * gm[:, lanes] * jax.nn.sigmoid(mo_ref[:, lanes])
        mix = mix + _dot(ym.astype(BF16), wom_ref[0, lanes, :])
    x1 = x_ref[...] + ga1 * mix
    h2 = (_rms(x1, g2_ref[0]) * (1.0 + sc2) + sh2).astype(BF16)
    acc = None
    for c in range(wup_ref.shape[2] // MLP_CHUNK):
        cols = slice(c * MLP_CHUNK, (c + 1) * MLP_CHUNK)
        u = jnp.maximum(_dot(h2, wup_ref[0, :, cols]), 0.0)
        d = _dot((u * u).astype(BF16), wdn_ref[0, cols, :])
        acc = d if acc is None else acc + d
    x2 = x1 + ga2 * acc
    if final_norm:
        x2 = _rms(x2, gfin_ref[...])
    o_ref[...] = x2


def _out_mlp(x2d, seq, yf, hm, mo, ya, layer, wts, mod, mod_row0, per_batch_mod, gfin, tm, final_norm):
    t, d = x2d.shape
    tiles_per_seq = seq // tm
    tok = lambda w: pl.BlockSpec((tm, w), lambda i: (i, 0))
    yf_spec = pl.BlockSpec((F_WIDTH // LANES, tm, LANES), lambda i: (i // tiles_per_seq, i % tiles_per_seq, 0))
    names = ("gm", "g2", "wof", "wom", "woa", "wup", "wdn")
    lay = {name: _layer_spec(wts[name], layer) for name in names}
    return pl.pallas_call(
        functools.partial(_out_mlp_kernel, final_norm=final_norm),
        grid=(t // tm,),
        in_specs=[
            tok(d), yf_spec, tok(MP_WIDTH), tok(MP_WIDTH), tok(AP_WIDTH),
            pl.BlockSpec((1, 6, d), _mod_map(mod_row0, per_batch_mod, tiles_per_seq)),
            lay["gm"], lay["g2"], _const_spec((1, d)),
            lay["wof"], lay["wom"], lay["woa"], lay["wup"], lay["wdn"],
        ],
        out_specs=tok(d),
        out_shape=jax.ShapeDtypeStruct((t, d), F32),
        compiler_params=_cparams(("arbitrary",)),
        name="out_mlp",
    )(x2d, yf, hm, mo, ya, mod, wts["gm"], wts["g2"], gfin,
      wts["wof"], wts["wom"], wts["woa"], wts["wup"], wts["wdn"])


def _dft_tables(n):
    idx = (np.arange(n, dtype=np.int64)[:, None] * np.arange(n, dtype=np.int64)[None, :]) % n
    ang = 2.0 * np.pi * idx.astype(np.float64) / n
    scale = 1.0 / np.sqrt(n)
    return np.cos(ang) * scale, np.sin(ang) * scale


def _channel_dft():
    c, s = _dft_tables(F_GROUP_DIM)
    eye = np.eye(F_GROUPS)
    return (jnp.asarray(np.kron(eye, c), dtype=F32).astype(BF16),
            jnp.asarray(np.kron(eye, s), dtype=F32).astype(BF16))


def _position_dft(n):
    c, s = _dft_tables(n)
    return jnp.asarray(c, dtype=F32).astype(BF16), jnp.asarray(s, dtype=F32).astype(BF16)


def _rope_tables(n, rotate):
    cos = np.zeros((n, HEAD_PAD), np.float32)
    sin = np.zeros((n, HEAD_PAD), np.float32)
    cos[:, :A_NOPE + A_ROPE] = 1.0
    if rotate:
        nf = A_ROPE // 4
        t = np.arange(n)
        row = (t // GRID_W).astype(np.float32)
        col = (t % GRID_W).astype(np.float32)
        freqs = (np.float32(ROPE_THETA) ** (-np.arange(nf, dtype=np.float32) / np.float32(nf))).astype(np.float32)
        for seg, pos in enumerate((row, col)):
            ang = pos[:, None] * freqs[None, :]
            c, s = np.cos(ang), np.sin(ang)
            base = A_NOPE + seg * 2 * nf
            cos[:, base:base + nf] = c
            cos[:, base + nf:base + 2 * nf] = c
            sin[:, base:base + nf] = -s
            sin[:, base + nf:base + 2 * nf] = s
    return jnp.asarray(cos), jnp.asarray(sin)


def _pad_heads_cols(w, heads, width):
    lead = w.shape[:-1]
    w = w.reshape(lead + (heads, width))
    w = jnp.pad(w, [(0, 0)] * len(lead) + [(0, 0), (0, HEAD_PAD - width)])
    return w.reshape(lead + (heads * HEAD_PAD,))


def _pad_heads_rows(w, heads, width):
    depth, _, n = w.shape
    w = jnp.pad(w.reshape(depth, heads, width, n), [(0, 0), (0, 0), (0, HEAD_PAD - width), (0, 0)])
    return w.reshape(depth, heads * HEAD_PAD, n)


GATE_I_COLS = np.concatenate([np.arange(M_HEADS), 2 * M_HEADS + np.arange(M_HEADS)])
GATE_F_COLS = GATE_I_COLS + M_HEADS


def _prepare_weights(max_chunks, g_norm1, g_norm2, w_in, b_gates, conv_qk, g_mlstm, g_q_norm, g_kv_norm,
                     w_uq, w_ukv, w_out, w_up, w_down):
    offs = np.cumsum([0, F_WIDTH, M_WIDTH, M_WIDTH, M_WIDTH, M_WIDTH, 4 * M_HEADS, Q_LORA, KV_LORA, A_ROPE])
    part = lambda i: w_in[:, :, offs[i]:offs[i + 1]]
    heads = lambda w: _pad_heads_cols(w, M_HEADS, M_HEAD_DIM)
    w_in_p = jnp.concatenate([
        part(0), heads(part(1)), heads(part(2)), heads(part(4)), part(6), part(7),
        jnp.pad(part(8), [(0, 0), (0, 0), (A_NOPE, LANES - A_NOPE - A_ROPE)]),
    ], axis=2).astype(BF16)
    assert w_in_p.shape[2] == IN_PAD
    gates = part(5)
    w_vt = jnp.concatenate([heads(part(3)), gates[:, :, GATE_I_COLS], gates[:, :, GATE_F_COLS]],
                           axis=2).transpose(0, 2, 1).astype(BF16)

    conv_p = jnp.concatenate([heads(conv_qk[:, :, :M_WIDTH]), heads(conv_qk[:, :, M_WIDTH:])], axis=2)
    conv_p = jnp.pad(conv_p, [(0, 0), (0, 8 - K_CONV), (0, 0)])

    tile_rows = lambda b: jnp.tile(b[:, :, None], (1, max_chunks, 1))
    ukv = w_ukv.reshape(w_ukv.shape[0], KV_LORA, A_HEADS, A_NOPE + A_V)
    pad_kv = lambda w: jnp.pad(w, [(0, 0), (0, 0), (0, 0), (0, HEAD_PAD - w.shape[-1])]).reshape(
        w.shape[0], KV_LORA, AP_WIDTH).astype(BF16)
    vec = lambda g: g[:, None, :]
    return dict(
        g1=vec(g_norm1), g2=vec(g_norm2), gq=vec(g_q_norm), gkv=vec(g_kv_norm),
        w_in_p=w_in_p, w_vt=w_vt, conv_p=conv_p,
        bir=tile_rows(b_gates[:, GATE_I_COLS]), bfr=tile_rows(b_gates[:, GATE_F_COLS]),
        gm=_pad_heads_cols(g_mlstm, M_HEADS, M_HEAD_DIM)[:, None, :],
        wq=_pad_heads_cols(w_uq, A_HEADS, A_NOPE + A_ROPE).astype(BF16),
        wk=pad_kv(ukv[..., :A_NOPE]), wv=pad_kv(ukv[..., A_NOPE:]),
        wof=w_out[:, :F_WIDTH].astype(BF16),
        wom=_pad_heads_rows(w_out[:, F_WIDTH:F_WIDTH + M_WIDTH], M_HEADS, M_HEAD_DIM).astype(BF16),
        woa=_pad_heads_rows(w_out[:, F_WIDTH + M_WIDTH:], A_HEADS, A_V).astype(BF16),
        wup=w_up.astype(BF16), wdn=w_down.astype(BF16),
    )


def _stream_selectors():
    sel = np.zeros((SEL_ROWS, N_STREAM, MCH), np.float32)
    for s in range(N_STREAM):
        for part in range(3):
            sel[part * N_STREAM + s, s, :] = 1.0
    return jnp.asarray(sel.reshape(SEL_ROWS, N_STREAM * MCH), dtype=BF16)


def kernel(x, c, ctx, c_ctx, w_mod, b_mod, g_norm1, g_norm2, w_in, b_gates, conv_qk, g_mlstm,
           g_q_norm, g_kv_norm, w_uq, w_ukv, w_out, w_up, w_down, g_final):
    nb, seq, d = x.shape
    nctx = ctx.shape[1]
    depth = w_mod.shape[0]
    assert d == D_MODEL and seq % 256 == 0 and nctx % MCH == 0
    sel = _stream_selectors()

    tm = min(512, seq)
    tm_mlp = min(512, seq)
    tm_ctx = min(256, nctx)
    tq = min(8 * ATTN_SUB, seq)
    tq_ctx = min(ATTN_SUB, nctx)

    dft_cc, dft_cs = _channel_dft()
    fft_lat = _fourier4_tables(seq)
    dft_ctx = _position_dft(nctx)
    rope_lat = _rope_tables(seq, True)
    rope_ctx = _rope_tables(nctx, False)
    wts = _prepare_weights(max(seq, nctx) // MCH, g_norm1, g_norm2, w_in, b_gates, conv_qk, g_mlstm,
                           g_q_norm, g_kv_norm, w_uq, w_ukv, w_out, w_up, w_down)
    gfin = g_final.reshape(1, d)

    rows = ((nb + 1 + 7) // 8) * 8
    cc = jnp.concatenate([c, c_ctx[None, :], jnp.zeros((rows - nb - 1, d), F32)], axis=0)
    mod_all = _modulation(cc, w_mod, b_mod).reshape(depth * rows, 6, d)

    xl = x.reshape(nb * seq, d)
    xc = ctx.reshape(nb * nctx, d)

    for l in range(depth):
        last = l == depth - 1
        row_lat, row_ctx = l * rows, l * rows + nb

        zc, zs, mqk, vt, mo, gr, q_a, k_a, v_a = _inproj(
            xl, seq, l, wts, mod_all, row_lat, True, dft_cc, dft_cs, *rope_lat, tm, True)
        zc_c, zs_c, mqk_c, vt_c, mo_c, gr_c, q_ac, k_ac, v_ac = _inproj(
            xc, nctx, l, wts, mod_all, row_ctx, False, dft_cc, dft_cs, *rope_ctx, tm_ctx, not last)

        yf = _fourier4(fft_lat, zc, zs)

        def mlstm_inputs(mqk_s, vt_s, gr_s, n):
            q_s, k_s = _conv_silu(mqk_s.reshape(nb, n, 2 * MP_WIDTH), l, wts)
            return (q_s, k_s, vt_s, gr_s)

        hm, hm_c = _mlstm(mlstm_inputs(mqk, vt, gr, seq), mlstm_inputs(mqk_c, vt_c, gr_c, nctx), l, wts, sel)

        b3 = lambda a, n: a.reshape(nb, n, AP_WIDTH)
        keys_ctx = (b3(k_ac, nctx), b3(v_ac, nctx))
        ya = _attention(b3(q_a, seq), [(b3(k_a, seq), b3(v_a, seq)), keys_ctx], tq)

        xl = _out_mlp(xl, seq, yf, hm.reshape(nb * seq, MP_WIDTH), mo, ya.reshape(nb * seq, AP_WIDTH),
                      l, wts, mod_all, row_lat, True, gfin, tm_mlp, last)

        if not last:
            yf_c = _fourier(*dft_ctx, zc_c, zs_c)
            ya_c = _attention(b3(q_ac, nctx), [keys_ctx], tq_ctx)
            xc = _out_mlp(xc, nctx, yf_c, hm_c.reshape(nb * nctx, MP_WIDTH), mo_c,
                          ya_c.reshape(nb * nctx, AP_WIDTH), l, wts, mod_all, row_ctx, False, gfin, tm_ctx, False)

    return xl.reshape(nb, seq, d)
```

```python
import functools

import numpy as np
import jax
import jax.numpy as jnp
from jax import lax
from jax.experimental import pallas as pl
from jax.experimental.pallas import tpu as pltpu

D_MODEL = 1024
GRID_W = 64
EPS = 1e-6
F_GROUPS = 4
F_GROUP_DIM = D_MODEL // 16
F_WIDTH = F_GROUPS * F_GROUP_DIM
M_HEADS = 4
M_HEAD_DIM = 3 * D_MODEL // 32
M_WIDTH = M_HEADS * M_HEAD_DIM
K_CONV = 5
A_HEADS = 4
A_NOPE = 64
A_ROPE = 32
A_V = 3 * D_MODEL // 32
Q_LORA = D_MODEL // 4
KV_LORA = D_MODEL // 8
ROPE_THETA = 10000.0
MLP_HIDDEN = 4 * D_MODEL

LANES = 128
HEAD_PAD = 128
MP_WIDTH = M_HEADS * HEAD_PAD
AP_WIDTH = A_HEADS * HEAD_PAD
VMEM_LIMIT = 56 * 1024 * 1024
MCH = 128
N_STREAM = 2 * M_HEADS

OFF_PF = 0
OFF_MQ = OFF_PF + F_WIDTH
OFF_MK = OFF_MQ + MP_WIDTH
OFF_MO = OFF_MK + MP_WIDTH
OFF_CQ = OFF_MO + MP_WIDTH
OFF_CKV = OFF_CQ + Q_LORA
OFF_KR = OFF_CKV + KV_LORA
IN_PAD = OFF_KR + LANES

BF16 = jnp.bfloat16
F32 = jnp.float32


def _cparams(sem):
    return pltpu.CompilerParams(dimension_semantics=sem, vmem_limit_bytes=VMEM_LIMIT)


def _const_spec(shape):
    nd = len(shape)
    return pl.BlockSpec(shape, lambda *_: (0,) * nd, pipeline_mode=pl.Buffered(1))


def _layer_spec(arr, layer):
    nd = arr.ndim
    return pl.BlockSpec((1,) + arr.shape[1:], lambda *_: (layer,) + (0,) * (nd - 1), pipeline_mode=pl.Buffered(1))


def _split3(a):
    hi = a.astype(BF16)
    r1 = a - hi.astype(F32)
    mid = r1.astype(BF16)
    lo = (r1 - mid.astype(F32)).astype(BF16)
    return hi, mid, lo


def _dot(a, b):
    return jnp.dot(a, b, preferred_element_type=F32)


def _dot_nt(a, b):
    return lax.dot_general(a, b, (((1,), (1,)), ((), ())), preferred_element_type=F32)


def _dot_tn(a, b):
    return lax.dot_general(a, b, (((0,), (0,)), ((), ())), preferred_element_type=F32)


def _rms(x, g):
    return x * lax.rsqrt(jnp.mean(x * x, axis=-1, keepdims=True) + EPS) * g


def _mod_kernel(c_ref, w_ref, b_ref, o_ref):
    c = c_ref[...]
    a = c * jax.nn.sigmoid(c)
    a_hi = a.astype(BF16)
    a_lo = (a - a_hi.astype(F32)).astype(BF16)
    w = w_ref[0]
    w_hi = w.astype(BF16)
    w_lo = (w - w_hi.astype(F32)).astype(BF16)
    acc = _dot(a_hi, w_hi) + _dot(a_hi, w_lo) + _dot(a_lo, w_hi)
    o_ref[0] = acc + b_ref[0]


def _modulation(cc, w_mod, b_mod):
    depth, d, n = w_mod.shape
    rows = cc.shape[0]
    tn = 1536
    return pl.pallas_call(
        _mod_kernel,
        grid=(depth, n // tn),
        in_specs=[
            pl.BlockSpec((rows, d), lambda l, j: (0, 0)),
            pl.BlockSpec((1, d, tn), lambda l, j: (l, 0, j)),
            pl.BlockSpec((1, 1, tn), lambda l, j: (l, 0, j)),
        ],
        out_specs=pl.BlockSpec((1, rows, tn), lambda l, j: (l, 0, j)),
        out_shape=jax.ShapeDtypeStruct((depth, rows, n), F32),
        compiler_params=_cparams(("arbitrary", "arbitrary")),
        name="modulation",
    )(cc, w_mod, b_mod.reshape(depth, 1, n))


def _rope(x, cos, sin, first_half):
    half = A_ROPE // 4
    partner = jnp.where(first_half, pltpu.roll(x, LANES - half, 1), pltpu.roll(x, half, 1))
    return x * cos + partner * sin


def _inproj_kernel(x_ref, g_ref, mod_ref, w_ref, wvt_ref, cc_ref, cs_ref,
                   cos_ref, sin_ref, gq_ref, gkv_ref, wq_ref, wk_ref, wv_ref,
                   zc_ref, zs_ref, mqk_ref, vt_ref, mo_ref, gr_ref, qa_ref, ka_ref, va_ref, *, with_q):
    x = x_ref[...]
    mod = mod_ref[0]
    h = _rms(x, g_ref[0]) * (1.0 + mod[1:2]) + mod[0:1]
    hb = h.astype(BF16)

    def proj(off, width):
        return _dot(hb, w_ref[0, :, off:off + width])

    ckv_kr = proj(OFF_CKV, 2 * LANES)
    cq = proj(OFF_CQ, Q_LORA) if with_q else None
    pf = proj(OFF_PF, F_WIDTH).astype(BF16)

    mqk_ref[...] = proj(OFF_MQ, 2 * MP_WIDTH)

    cos = cos_ref[...]
    sin = sin_ref[...]
    lane = lax.broadcasted_iota(jnp.int32, cos.shape, 1)
    first_half = ((lane - A_NOPE) & (A_ROPE // 2 - 1)) < A_ROPE // 4
    kvn = _rms(ckv_kr[:, :KV_LORA], gkv_ref[0]).astype(BF16)
    k_rope = _rope(ckv_kr[:, KV_LORA:], cos, sin, first_half)
    k_nope = _dot(kvn, wk_ref[0])
    va_ref[...] = _dot(kvn, wv_ref[0]).astype(BF16)
    for hd in range(A_HEADS):
        lanes = slice(hd * HEAD_PAD, (hd + 1) * HEAD_PAD)
        ka_ref[:, lanes] = (k_nope[:, lanes] + k_rope).astype(BF16)
    if with_q:
        qn = _rms(cq, gq_ref[0]).astype(BF16)
        q_all = _dot(qn, wq_ref[0])
        q_raw = [q_all[:, hd * HEAD_PAD:(hd + 1) * HEAD_PAD] for hd in range(A_HEADS)]
    zc_ref[...] = _dot(pf, cc_ref[...]).astype(BF16)
    zs_ref[...] = _dot(pf, cs_ref[...]).astype(BF16)

    vg = _dot_nt(wvt_ref[0], hb)
    vt = vg[:MP_WIDTH].astype(BF16)
    for j in range(vt_ref.shape[0]):
        vt_ref[j] = vt[:, j * MCH:(j + 1) * MCH]
        gr_ref[j] = vg[MP_WIDTH:, j * MCH:(j + 1) * MCH]
    mo_ref[...] = proj(OFF_MO, MP_WIDTH)

    if with_q:
        scale = (A_NOPE + A_ROPE) ** -0.5
        for hd in range(A_HEADS):
            q = _rope(q_raw[hd], cos, sin, first_half)
            qa_ref[:, hd * HEAD_PAD:(hd + 1) * HEAD_PAD] = (q * scale).astype(BF16)
    else:
        qa_ref[...] = jnp.zeros_like(qa_ref)


def _mod_map(mod_row0, per_batch_mod, tiles_per_seq):
    if per_batch_mod:
        return lambda i: (mod_row0 + i // tiles_per_seq, 0, 0)
    return lambda i: (mod_row0, 0, 0)


def _inproj(x2d, seq, layer, wts, mod, mod_row0, per_batch_mod, dft_cc, dft_cs, cos, sin, tm, with_q):
    t, d = x2d.shape
    nb = t // seq
    tiles_per_seq = seq // tm
    tok = lambda w: pl.BlockSpec((tm, w), lambda i: (i, 0))
    z_spec = pl.BlockSpec((tm, F_WIDTH), lambda i: (i % tiles_per_seq, i // tiles_per_seq))
    pos = pl.BlockSpec((tm, LANES), lambda i: (i % tiles_per_seq, 0))
    heads_bf16 = jax.ShapeDtypeStruct((t, AP_WIDTH), BF16)
    shapes = [
        jax.ShapeDtypeStruct((seq, nb * F_WIDTH), BF16),
        jax.ShapeDtypeStruct((seq, nb * F_WIDTH), BF16),
        jax.ShapeDtypeStruct((t, 2 * MP_WIDTH), F32),
        jax.ShapeDtypeStruct((t // MCH, MP_WIDTH, MCH), BF16),
        jax.ShapeDtypeStruct((t, MP_WIDTH), F32),
        jax.ShapeDtypeStruct((t // MCH, 2 * N_STREAM, MCH), F32),
        heads_bf16, heads_bf16, heads_bf16,
    ]
    vt_spec = pl.BlockSpec((tm // MCH, MP_WIDTH, MCH), lambda i: (i, 0, 0))
    gr_spec = pl.BlockSpec((tm // MCH, 2 * N_STREAM, MCH), lambda i: (i, 0, 0))
    out_specs = [z_spec, z_spec, tok(2 * MP_WIDTH), vt_spec, tok(MP_WIDTH), gr_spec,
                 tok(AP_WIDTH), tok(AP_WIDTH), tok(AP_WIDTH)]
    lay = lambda name: _layer_spec(wts[name], layer)
    return pl.pallas_call(
        functools.partial(_inproj_kernel, with_q=with_q),
        grid=(t // tm,),
        in_specs=[
            tok(d),
            lay("g1"),
            pl.BlockSpec((1, 6, d), _mod_map(mod_row0, per_batch_mod, tiles_per_seq)),
            lay("w_in_p"), lay("w_vt"),
            _const_spec((F_WIDTH, F_WIDTH)),
            _const_spec((F_WIDTH, F_WIDTH)),
            pos, pos,
            lay("gq"), lay("gkv"), lay("wq"), lay("wk"), lay("wv"),
        ],
        out_specs=out_specs,
        out_shape=shapes,
        compiler_params=_cparams(("arbitrary",)),
        name="inproj",
    )(x2d, wts["g1"], mod, wts["w_in_p"], wts["w_vt"], dft_cc, dft_cs, cos, sin,
      wts["gq"], wts["gkv"], wts["wq"], wts["wk"], wts["wv"])


def _fourier_kernel(c_ref, s_ref, zc_ref, zs_ref, o_ref):
    y = _dot(c_ref[...], zc_ref[...]) - _dot(s_ref[...], zs_ref[...])
    for slab in range(o_ref.shape[0]):
        o_ref[slab] = y[:, slab * LANES:(slab + 1) * LANES]


def _fourier(dft_c, dft_s, zc, zs):
    n, cols = zc.shape
    tr = min(n, 512)
    tc = min(cols, 512)
    return pl.pallas_call(
        _fourier_kernel,
        grid=(n // tr, cols // tc),
        in_specs=[
            pl.BlockSpec((tr, n), lambda i, j: (i, 0)),
            pl.BlockSpec((tr, n), lambda i, j: (i, 0)),
            pl.BlockSpec((n, tc), lambda i, j: (0, j)),
            pl.BlockSpec((n, tc), lambda i, j: (0, j)),
        ],
        out_specs=pl.BlockSpec((tc // LANES, tr, LANES), lambda i, j: (j, i, 0)),
        out_shape=jax.ShapeDtypeStruct((cols // LANES, n, LANES), F32),
        compiler_params=_cparams(("arbitrary", "arbitrary")),
        name="fourier",
    )(dft_c, dft_s, zc, zs)


FFT_COLS = 256


def _fourier4_kernel(tab_ref, twc_ref, tws_ref, zc_ref, zs_ref, o_ref):
    m = zc_ref.shape[0] // 4
    reps = zc_ref.shape[1] // LANES
    c0, c1, c2, c3 = (zc_ref[j * m:(j + 1) * m, :].astype(F32) for j in range(4))
    s0, s1, s2, s3 = (zs_ref[j * m:(j + 1) * m, :].astype(F32) for j in range(4))
    ce, co, cd, cu = c0 + c2, c1 + c3, c0 - c2, c1 - c3
    se, so, sd, su = s0 + s2, s1 + s3, s0 - s2, s1 - s3

    def emit(br, bi, k):
        if k:
            cos = jnp.concatenate([twc_ref[k - 1]] * reps, axis=1)
            sin = jnp.concatenate([tws_ref[k - 1]] * reps, axis=1)
            br, bi = br * cos + bi * sin, bi * cos - br * sin
        stacked = jnp.concatenate([br.astype(BF16), bi.astype(BF16)], axis=0)
        y = _dot(tab_ref[...], stacked)
        for slab in range(reps):
            o_ref[slab, pl.ds(k, m, stride=4), :] = y[:, slab * LANES:(slab + 1) * LANES]

    emit(ce + co, -(se + so), 0)
    emit(cd - su, -sd - cu, 1)
    emit(ce - co, so - se, 2)
    emit(cd + su, cu - sd, 3)


def _fourier4(tables, zc, zs):
    tab, twc, tws = tables
    n, cols = zc.shape
    m = n // 4
    tc = min(cols, FFT_COLS)
    return pl.pallas_call(
        _fourier4_kernel,
        grid=(cols // tc,),
        in_specs=[
            _const_spec((m, 2 * m)),
            _const_spec((3, m, LANES)),
            _const_spec((3, m, LANES)),
            pl.BlockSpec((n, tc), lambda j: (0, j)),
            pl.BlockSpec((n, tc), lambda j: (0, j)),
        ],
        out_specs=pl.BlockSpec((tc // LANES, n, LANES), lambda j: (j, 0, 0)),
        out_shape=jax.ShapeDtypeStruct((cols // LANES, n, LANES), F32),
        compiler_params=_cparams(("arbitrary",)),
        name="fourier4",
    )(tab, twc, tws, zc, zs)


def _fourier4_tables(n):
    m = n // 4
    idx = (np.arange(m, dtype=np.int64)[:, None] * np.arange(m, dtype=np.int64)[None, :]) % m
    ang = 2.0 * np.pi * idx.astype(np.float64) / m
    tab = np.concatenate([np.cos(ang), np.sin(ang)], axis=1) / np.sqrt(n)
    theta = 2.0 * np.pi * np.arange(m, dtype=np.float64)[None, :] * np.arange(1, 4, dtype=np.float64)[:, None] / n
    bcast = lambda t: jnp.asarray(np.repeat(t[:, :, None], LANES, axis=2), dtype=F32)
    return jnp.asarray(tab, dtype=F32).astype(BF16), bcast(np.cos(theta)), bcast(np.sin(theta))


CONV_ROWS = 256
CONV_HALO = 8


def _conv_kernel(u_ref, w_ref, q_ref, k_ref, pad_ref):
    n = u_ref.shape[1]
    width = u_ref.shape[2]
    zeros = jnp.zeros((CONV_HALO, width), F32)
    pad_ref[0:CONV_HALO, :] = zeros
    pad_ref[CONV_HALO + n:2 * CONV_HALO + n, :] = zeros
    pad_ref[CONV_HALO:CONV_HALO + n, :] = u_ref[0]
    w = w_ref[0]
    rows = min(CONV_ROWS, n)
    span = rows + 2 * CONV_HALO
    for r in range(n // rows):
        block = pad_ref[r * rows:r * rows + span, :]
        acc = None
        for j in range(K_CONV):
            shift = (K_CONV // 2 - j) % span
            tap = block if shift == 0 else pltpu.roll(block, shift, 0)
            term = tap[CONV_HALO:CONV_HALO + rows, :] * w[j:j + 1, :]
            acc = term if acc is None else acc + term
        act = acc * jax.nn.sigmoid(acc)
        q_ref[0, r * rows:(r + 1) * rows, :] = (act[:, :MP_WIDTH] * (M_HEAD_DIM ** -0.5)).astype(BF16)
        k_ref[0, r * rows:(r + 1) * rows, :] = act[:, MP_WIDTH:].astype(BF16)


def _conv_silu(mqk, layer, wts):
    nb, n, width = mqk.shape
    return pl.pallas_call(
        _conv_kernel,
        grid=(nb,),
        in_specs=[
            pl.BlockSpec((1, n, width), lambda b: (b, 0, 0)),
            _layer_spec(wts["conv_p"], layer),
        ],
        out_specs=[
            pl.BlockSpec((1, n, MP_WIDTH), lambda b: (b, 0, 0)),
            pl.BlockSpec((1, n, MP_WIDTH), lambda b: (b, 0, 0)),
        ],
        out_shape=[
            jax.ShapeDtypeStruct((nb, n, MP_WIDTH), BF16),
            jax.ShapeDtypeStruct((nb, n, MP_WIDTH), BF16),
        ],
        scratch_shapes=[pltpu.VMEM((n + 2 * CONV_HALO, width), F32)],
        compiler_params=_cparams(("arbitrary",)),
        name="conv_silu",
    )(mqk, wts["conv_p"])


def _log_sigmoid(x):
    return jnp.minimum(x, 0.0) - jnp.log(1.0 + jnp.exp(-jnp.abs(x)))


def _exact_dot_01(a, tri_bf16, a_on_left):
    out = None
    for term in _split3(a):
        d = _dot(term, tri_bf16) if a_on_left else _dot(tri_bf16, term)
        out = d if out is None else out + d
    return out


ONE_ROW = M_HEAD_DIM
(F_R, F_B, F_CM, F_TOT, F_CML, F_MP, F_A, F_WI, F_ELD, F_DEC, F_WK, F_HI, F_MID, F_LO) = range(14)
N_FIELDS = 14
OUT_GROUP = 8
SEL_ROWS = 32


def _mlstm_kernel(ql_ref, kl_ref, vtl_ref, grl_ref,
                  qc_ref, kc_ref, vtc_ref, grc_ref,
                  bir_ref, bfr_ref, sel_ref,
                  hl_ref, hc_ref,
                  ct_st, st_sc, rows_sc):
    L = MCH
    nh = M_HEADS
    ncc = qc_ref.shape[1] // L
    ncl = ql_ref.shape[1] // L

    d0 = lax.broadcasted_iota(jnp.int32, (L, L), 0)
    d1 = lax.broadcasted_iota(jnp.int32, (L, L), 1)
    le = d0 <= d1
    ge = d0 >= d1
    tri_le = le.astype(BF16)
    fwd_rows = lax.broadcasted_iota(jnp.int32, (N_STREAM, L), 0) < nh
    feat = lax.broadcasted_iota(jnp.int32, (HEAD_PAD, L), 0)
    one_row = feat == ONE_ROW
    keep_rows = feat < M_HEAD_DIM

    ct_st[...] = jnp.zeros_like(ct_st)

    def field(f, slot0, nc):
        return rows_sc[f, slot0:slot0 + nc].reshape(nc * N_STREAM, L)

    def set_field(f, slot0, nc, val):
        rows_sc[f, slot0:slot0 + nc] = val.reshape(nc, N_STREAM, L)

    def gate_pass(gr_ref, nc, slot0):
        n8 = nc * N_STREAM
        fwd = (lax.broadcasted_iota(jnp.int32, (n8, L), 0) & (N_STREAM - 1)) < nh
        lane = lax.broadcasted_iota(jnp.int32, (n8, L), 1)
        gi = gr_ref[:, 0:N_STREAM, :].reshape(n8, L) + bir_ref[0, 0:n8, :]
        f = _log_sigmoid(gr_ref[:, N_STREAM:2 * N_STREAM, :].reshape(n8, L) + bfr_ref[0, 0:n8, :])
        pre = _exact_dot_01(f, tri_le, a_on_left=True)
        total = jnp.sum(f, axis=1, keepdims=True)
        b = jnp.where(fwd, pre, total - pre + f)
        r = gi - b
        cm = r
        sh = 1
        while sh < L:
            from_left = jnp.where(lane >= sh, pltpu.roll(cm, sh, 1), -jnp.inf)
            from_right = jnp.where(lane < L - sh, pltpu.roll(cm, L - sh, 1), -jnp.inf)
            cm = jnp.maximum(cm, jnp.where(fwd, from_left, from_right))
            sh *= 2
        set_field(F_R, slot0, nc, r)
        set_field(F_B, slot0, nc, b)
        set_field(F_CM, slot0, nc, cm)
        set_field(F_TOT, slot0, nc, jnp.broadcast_to(total, (n8, L)))
        set_field(F_CML, slot0, nc, jnp.broadcast_to(jnp.max(r, axis=1, keepdims=True), (n8, L)))

    def m_scan(nc, slot0, m0):
        def step(j, m):
            sf = slot0 + j
            sb = slot0 + nc - 1 - j
            tot = jnp.where(fwd_rows, rows_sc[F_TOT, sf], rows_sc[F_TOT, sb])
            cml = jnp.where(fwd_rows, rows_sc[F_CML, sf], rows_sc[F_CML, sb])
            rows_sc[F_MP, sf, 0:nh, :] = m[0:nh]
            rows_sc[F_MP, sb, nh:N_STREAM, :] = m[nh:N_STREAM]
            return tot + jnp.maximum(m, cml)

        return lax.fori_loop(0, nc, step, m0)

    def weight_pass(nc, slot0):
        mp, cm, b, r = (field(f, slot0, nc) for f in (F_MP, F_CM, F_B, F_R))
        big = jnp.maximum(mp, field(F_CML, slot0, nc))
        a = -jnp.maximum(mp, cm)
        set_field(F_A, slot0, nc, a)
        set_field(F_WI, slot0, nc, jnp.exp(mp + a))
        set_field(F_ELD, slot0, nc, jnp.exp(a - b))
        set_field(F_DEC, slot0, nc, jnp.exp(mp - big))
        set_field(F_WK, slot0, nc, jnp.exp(r - big))
        for f, term in zip((F_HI, F_MID, F_LO), _split3(r)):
            set_field(f, slot0, nc, term.astype(F32))

    def value_slab(vt_ref, c, hd):
        vt = vt_ref[c, hd * HEAD_PAD:(hd + 1) * HEAD_PAD, :].astype(F32)
        return jnp.where(one_row, 1.0, vt)

    def state_pass(k_ref, vt_ref, nc, slot0):
        def step(j, carry):
            uts = []
            for sidx in range(N_STREAM):
                hd = sidx % nh
                c = j if sidx < nh else nc - 1 - j
                wk = rows_sc[F_WK, slot0 + c, sidx:sidx + 1, :]
                kk = k_ref[0, pl.ds(pl.multiple_of(c * L, L), L), hd * HEAD_PAD:(hd + 1) * HEAD_PAD]
                uts.append(_dot((value_slab(vt_ref, c, hd) * wk).astype(BF16), kk))
            for sidx in range(N_STREAM):
                slot = slot0 + (j if sidx < nh else nc - 1 - j)
                prev = ct_st[sidx]
                st_sc[sidx, slot] = prev.astype(BF16)
                ct_st[sidx] = rows_sc[F_DEC, slot, sidx:sidx + 1, :] * prev + uts[sidx]
            return carry

        lax.fori_loop(0, nc, step, 0)

    def output_pass(q_ref, k_ref, vt_ref, out_ref, nc, slot0):
        head_lanes = [slice(hd * HEAD_PAD, (hd + 1) * HEAD_PAD) for hd in range(nh)]

        def independent_matmuls(c):
            rows = pl.ds(pl.multiple_of(c * L, L), L)
            slot = slot0 + c
            r3 = jnp.concatenate([rows_sc[F_HI, slot], rows_sc[F_MID, slot], rows_sc[F_LO, slot],
                                  jnp.zeros((N_STREAM, L), F32)], axis=0).astype(BF16)
            qs = [q_ref[0, rows, lanes] for lanes in head_lanes]
            s_ts = [_dot_nt(k_ref[0, rows, lanes], q) for lanes, q in zip(head_lanes, qs)]
            inters = [_dot_nt(st_sc[sidx, slot], qs[sidx % nh]) for sidx in range(N_STREAM)]
            r_all = _dot_tn(r3, sel_ref[...])
            r_ts = [r_all[:, sidx * L:(sidx + 1) * L] for sidx in range(N_STREAM)]
            return s_ts, inters, r_ts

        def finish(c, s_ts, inters, r_ts):
            rows = pl.ds(pl.multiple_of(c * L, L), L)
            slot = slot0 + c
            a_rows = rows_sc[F_A, slot]
            wi_rows = rows_sc[F_WI, slot]
            eld_rows = rows_sc[F_ELD, slot]
            p_ts = []
            for sidx in range(N_STREAM):
                one = slice(sidx, sidx + 1)
                dm = jnp.where(le if sidx < nh else ge, r_ts[sidx] + a_rows[one, :], -jnp.inf)
                p_ts.append((s_ts[sidx % nh] * jnp.exp(dm)).astype(BF16))
            z_pairs = [_dot(value_slab(vt_ref, c, hd).astype(BF16),
                            jnp.concatenate([p_ts[hd], p_ts[nh + hd]], axis=1)) for hd in range(nh)]
            for hd in range(nh):
                hsum = None
                for dr, sidx in enumerate((hd, nh + hd)):
                    one = slice(sidx, sidx + 1)
                    z_t = z_pairs[hd][:, dr * L:(dr + 1) * L] + inters[sidx] * wi_rows[one, :]
                    den = z_t[ONE_ROW:ONE_ROW + 1, :]
                    h_t = z_t * (1.0 / jnp.maximum(jnp.abs(den), eld_rows[one, :]))
                    hsum = h_t if hsum is None else hsum + h_t
                out_ref[0, rows, head_lanes[hd]] = jnp.where(keep_rows, hsum, 0.0).T

        group = min(OUT_GROUP, nc)

        def step(g, carry):
            ahead = independent_matmuls(g * group)
            for u in range(group):
                cur = ahead
                if u + 1 < group:
                    ahead = independent_matmuls(g * group + u + 1)
                finish(g * group + u, *cur)
            return carry

        lax.fori_loop(0, nc // group, step, 0)

    gate_pass(grc_ref, ncc, 0)
    gate_pass(grl_ref, ncl, ncc)
    m1 = m_scan(ncc, 0, jnp.zeros((N_STREAM, L), F32))
    m_scan(ncl, ncc, m1)
    weight_pass(ncc, 0)
    weight_pass(ncl, ncc)
    state_pass(kc_ref, vtc_ref, ncc, 0)
    state_pass(kl_ref, vtl_ref, ncl, ncc)
    output_pass(qc_ref, kc_ref, vtc_ref, hc_ref, ncc, 0)
    output_pass(ql_ref, kl_ref, vtl_ref, hl_ref, ncl, ncc)


def _mlstm(lat, ctx, layer, wts, sel):
    nb, n, _ = lat[0].shape
    nctx = ctx[0].shape[1]
    assert MCH == LANES == HEAD_PAD
    nct = (n + nctx) // MCH
    assert wts["bir"].shape[1] >= max(n, nctx) // MCH * N_STREAM

    def specs(rows):
        nc = rows // MCH
        seq = lambda w: pl.BlockSpec((1, rows, w), lambda b: (b, 0, 0))
        chunked = lambda rows_: pl.BlockSpec((nc, rows_, MCH), lambda b: (b, 0, 0))
        return [seq(MP_WIDTH), seq(MP_WIDTH), chunked(MP_WIDTH), chunked(2 * N_STREAM)]

    out_spec = lambda rows: pl.BlockSpec((1, rows, MP_WIDTH), lambda b: (b, 0, 0))
    return pl.pallas_call(
        _mlstm_kernel,
        grid=(nb,),
        in_specs=specs(n) + specs(nctx) + [
            _layer_spec(wts["bir"], layer), _layer_spec(wts["bfr"], layer),
            _const_spec((SEL_ROWS, N_STREAM * MCH)),
        ],
        out_specs=[out_spec(n), out_spec(nctx)],
        out_shape=[
            jax.ShapeDtypeStruct((nb, n, MP_WIDTH), F32),
            jax.ShapeDtypeStruct((nb, nctx, MP_WIDTH), F32),
        ],
        scratch_shapes=[
            pltpu.VMEM((N_STREAM, HEAD_PAD, HEAD_PAD), F32),
            pltpu.VMEM((N_STREAM, nct, HEAD_PAD, HEAD_PAD), BF16),
            pltpu.VMEM((N_FIELDS, nct, N_STREAM, MCH), F32),
        ],
        compiler_params=_cparams(("arbitrary",)),
        name="mlstm",
    )(*lat, *ctx, wts["bir"], wts["bfr"], sel)


ATTN_SUB = 256


def _attn_kernel(*refs, n_sets):
    q_ref = refs[0]
    kv_refs = refs[1:1 + 2 * n_sets]
    o_ref = refs[1 + 2 * n_sets]
    sub = min(ATTN_SUB, q_ref.shape[1])
    n_sub = q_ref.shape[1] // sub

    def scores_of(t):
        q = q_ref[0, t * sub:(t + 1) * sub, :]
        return [_dot_nt(q, kv_refs[2 * i][0]) for i in range(n_sets)]

    nxt = scores_of(0)
    for t in range(n_sub):
        rows = slice(t * sub, (t + 1) * sub)
        scores = nxt
        if t + 1 < n_sub:
            nxt = scores_of(t + 1)
        m = None
        for s in scores:
            sm = jnp.max(s, axis=-1, keepdims=True)
            m = sm if m is None else jnp.maximum(m, sm)
        acc = None
        den = None
        for i, s in enumerate(scores):
            p = jnp.exp(s - m)
            l = jnp.sum(p, axis=-1, keepdims=True)
            o = _dot(p.astype(BF16), kv_refs[2 * i + 1][0])
            acc = o if acc is None else acc + o
            den = l if den is None else den + l
        o_ref[0, rows, :] = (acc / den).astype(BF16)


def _attention(q, key_sets, tq):
    nb, n, _ = q.shape
    n_sets = len(key_sets)
    in_specs = [pl.BlockSpec((1, tq, HEAD_PAD), lambda b, h, i: (b, i, h))]
    args = [q]
    for k, v in key_sets:
        nk = k.shape[1]
        spec = pl.BlockSpec((1, nk, HEAD_PAD), lambda b, h, i: (b, 0, h))
        in_specs += [spec, spec]
        args += [k, v]
    return pl.pallas_call(
        functools.partial(_attn_kernel, n_sets=n_sets),
        grid=(nb, A_HEADS, n // tq),
        in_specs=in_specs,
        out_specs=pl.BlockSpec((1, tq, HEAD_PAD), lambda b, h, i: (b, i, h)),
        out_shape=jax.ShapeDtypeStruct((nb, n, AP_WIDTH), BF16),
        compiler_params=_cparams(("arbitrary", "arbitrary", "arbitrary")),
        name="attention",
    )(*args)


MLP_CHUNK = 1024


def _out_mlp_kernel(x_ref, yf_ref, hm_ref, mo_ref, ya_ref, mod_ref,
                    gm_ref, g2_ref, gfin_ref, wof_ref, wom_ref, woa_ref, wup_ref, wdn_ref,
                    o_ref, *, final_norm):
    mod = mod_ref[0]
    ga1, sh2, sc2, ga2 = mod[2:3], mod[3:4], mod[4:5], mod[5:6]
    yf = jnp.concatenate([yf_ref[slab].astype(BF16) for slab in range(yf_ref.shape[0])], axis=1)
    mix = _dot(ya_ref[...], woa_ref[0]) + _dot(yf, wof_ref[0])
    gm = gm_ref[0]
    yms = []
    for hd in range(M_HEADS):
        lanes = slice(hd * HEAD_PAD, (hd + 1) * HEAD_PAD)
        hh = hm_ref[:, lanes]
        ms = jnp.sum(hh * hh, axis=-1, keepdims=True) * (1.0 / M_HEAD_DIM)
        ym = hh * lax.rsqrt(ms + EPS) * gm[:, lanes] * jax.nn.sigmoid(mo_ref[:, lanes])
        yms.append(ym.astype(BF16))
    mix = mix + _dot(jnp.concatenate(yms, axis=1), wom_ref[0])
    x1 = x_ref[...] + ga1 * mix
    h2 = (_rms(x1, g2_ref[0]) * (1.0 + sc2) + sh2).astype(BF16)
    acc = None
    for c in range(wup_ref.shape[2] // MLP_CHUNK):
        cols = slice(c * MLP_CHUNK, (c + 1) * MLP_CHUNK)
        u = jnp.maximum(_dot(h2, wup_ref[0, :, cols]), 0.0)
        d = _dot((u * u).astype(BF16), wdn_ref[0, cols, :])
        acc = d if acc is None else acc + d
    x2 = x1 + ga2 * acc
    if final_norm:
        x2 = _rms(x2, gfin_ref[...])
    o_ref[...] = x2


def _out_mlp(x2d, seq, yf, hm, mo, ya, layer, wts, mod, mod_row0, per_batch_mod, gfin, tm, final_norm):
    t, d = x2d.shape
    tiles_per_seq = seq // tm
    tok = lambda w: pl.BlockSpec((tm, w), lambda i: (i, 0))
    yf_spec = pl.BlockSpec((F_WIDTH // LANES, tm, LANES), lambda i: (i // tiles_per_seq, i % tiles_per_seq, 0))
    names = ("gm", "g2", "wof", "wom", "woa", "wup", "wdn")
    lay = {name: _layer_spec(wts[name], layer) for name in names}
    return pl.pallas_call(
        functools.partial(_out_mlp_kernel, final_norm=final_norm),
        grid=(t // tm,),
        in_specs=[
            tok(d), yf_spec, tok(MP_WIDTH), tok(MP_WIDTH), tok(AP_WIDTH),
            pl.BlockSpec((1, 6, d), _mod_map(mod_row0, per_batch_mod, tiles_per_seq)),
            lay["gm"], lay["g2"], _const_spec((1, d)),
            lay["wof"], lay["wom"], lay["woa"], lay["wup"], lay["wdn"],
        ],
        out_specs=tok(d),
        out_shape=jax.ShapeDtypeStruct((t, d), F32),
        compiler_params=_cparams(("arbitrary",)),
        name="out_mlp",
    )(x2d, yf, hm, mo, ya, mod, wts["gm"], wts["g2"], gfin,
      wts["wof"], wts["wom"], wts["woa"], wts["wup"], wts["wdn"])


def _dft_tables(n):
    idx = (np.arange(n, dtype=np.int64)[:, None] * np.arange(n, dtype=np.int64)[None, :]) % n
    ang = 2.0 * np.pi * idx.astype(np.float64) / n
    scale = 1.0 / np.sqrt(n)
    return np.cos(ang) * scale, np.sin(ang) * scale


def _channel_dft():
    c, s = _dft_tables(F_GROUP_DIM)
    eye = np.eye(F_GROUPS)
    return (jnp.asarray(np.kron(eye, c), dtype=F32).astype(BF16),
            jnp.asarray(np.kron(eye, s), dtype=F32).astype(BF16))


def _position_dft(n):
    c, s = _dft_tables(n)
    return jnp.asarray(c, dtype=F32).astype(BF16), jnp.asarray(s, dtype=F32).astype(BF16)


def _rope_tables(n, rotate):
    cos = np.zeros((n, HEAD_PAD), np.float32)
    sin = np.zeros((n, HEAD_PAD), np.float32)
    cos[:, :A_NOPE + A_ROPE] = 1.0
    if rotate:
        nf = A_ROPE // 4
        t = np.arange(n)
        row = (t // GRID_W).astype(np.float32)
        col = (t % GRID_W).astype(np.float32)
        freqs = (np.float32(ROPE_THETA) ** (-np.arange(nf, dtype=np.float32) / np.float32(nf))).astype(np.float32)
        for seg, pos in enumerate((row, col)):
            ang = pos[:, None] * freqs[None, :]
            c, s = np.cos(ang), np.sin(ang)
            base = A_NOPE + seg * 2 * nf
            cos[:, base:base + nf] = c
            cos[:, base + nf:base + 2 * nf] = c
            sin[:, base:base + nf] = -s
            sin[:, base + nf:base + 2 * nf] = s
    return jnp.asarray(cos), jnp.asarray(sin)


def _pad_heads_cols(w, heads, width):
    lead = w.shape[:-1]
    w = w.reshape(lead + (heads, width))
    w = jnp.pad(w, [(0, 0)] * len(lead) + [(0, 0), (0, HEAD_PAD - width)])
    return w.reshape(lead + (heads * HEAD_PAD,))


def _pad_heads_rows(w, heads, width):
    depth, _, n = w.shape
    w = jnp.pad(w.reshape(depth, heads, width, n), [(0, 0), (0, 0), (0, HEAD_PAD - width), (0, 0)])
    return w.reshape(depth, heads * HEAD_PAD, n)


GATE_I_COLS = np.concatenate([np.arange(M_HEADS), 2 * M_HEADS + np.arange(M_HEADS)])
GATE_F_COLS = GATE_I_COLS + M_HEADS


def _prepare_weights(max_chunks, g_norm1, g_norm2, w_in, b_gates, conv_qk, g_mlstm, g_q_norm, g_kv_norm,
                     w_uq, w_ukv, w_out, w_up, w_down):
    offs = np.cumsum([0, F_WIDTH, M_WIDTH, M_WIDTH, M_WIDTH, M_WIDTH, 4 * M_HEADS, Q_LORA, KV_LORA, A_ROPE])
    part = lambda i: w_in[:, :, offs[i]:offs[i + 1]]
    heads = lambda w: _pad_heads_cols(w, M_HEADS, M_HEAD_DIM)
    w_in_p = jnp.concatenate([
        part(0), heads(part(1)), heads(part(2)), heads(part(4)), part(6), part(7),
        jnp.pad(part(8), [(0, 0), (0, 0), (A_NOPE, LANES - A_NOPE - A_ROPE)]),
    ], axis=2).astype(BF16)
    assert w_in_p.shape[2] == IN_PAD
    gates = part(5)
    w_vt = jnp.concatenate([heads(part(3)), gates[:, :, GATE_I_COLS], gates[:, :, GATE_F_COLS]],
                           axis=2).transpose(0, 2, 1).astype(BF16)

    conv_p = jnp.concatenate([heads(conv_qk[:, :, :M_WIDTH]), heads(conv_qk[:, :, M_WIDTH:])], axis=2)
    conv_p = jnp.pad(conv_p, [(0, 0), (0, 8 - K_CONV), (0, 0)])

    tile_rows = lambda b: jnp.tile(b[:, :, None], (1, max_chunks, 1))
    ukv = w_ukv.reshape(w_ukv.shape[0], KV_LORA, A_HEADS, A_NOPE + A_V)
    pad_kv = lambda w: jnp.pad(w, [(0, 0), (0, 0), (0, 0), (0, HEAD_PAD - w.shape[-1])]).reshape(
        w.shape[0], KV_LORA, AP_WIDTH).astype(BF16)
    vec = lambda g: g[:, None, :]
    return dict(
        g1=vec(g_norm1), g2=vec(g_norm2), gq=vec(g_q_norm), gkv=vec(g_kv_norm),
        w_in_p=w_in_p, w_vt=w_vt, conv_p=conv_p,
        bir=tile_rows(b_gates[:, GATE_I_COLS]), bfr=tile_rows(b_gates[:, GATE_F_COLS]),
        gm=_pad_heads_cols(g_mlstm, M_HEADS, M_HEAD_DIM)[:, None, :],
        wq=_pad_heads_cols(w_uq, A_HEADS, A_NOPE + A_ROPE).astype(BF16),
        wk=pad_kv(ukv[..., :A_NOPE]), wv=pad_kv(ukv[..., A_NOPE:]),
        wof=w_out[:, :F_WIDTH].astype(BF16),
        wom=_pad_heads_rows(w_out[:, F_WIDTH:F_WIDTH + M_WIDTH], M_HEADS, M_HEAD_DIM).astype(BF16),
        woa=_pad_heads_rows(w_out[:, F_WIDTH + M_WIDTH:], A_HEADS, A_V).astype(BF16),
        wup=w_up.astype(BF16), wdn=w_down.astype(BF16),
    )


def _stream_selectors():
    sel = np.zeros((SEL_ROWS, N_STREAM, MCH), np.float32)
    for s in range(N_STREAM):
        for part in range(3):
            sel[part * N_STREAM + s, s, :] = 1.0
    return jnp.asarray(sel.reshape(SEL_ROWS, N_STREAM * MCH), dtype=BF16)


def kernel(x, c, ctx, c_ctx, w_mod, b_mod, g_norm1, g_norm2, w_in, b_gates, conv_qk, g_mlstm,
           g_q_norm, g_kv_norm, w_uq, w_ukv, w_out, w_up, w_down, g_final):
    nb, seq, d = x.shape
    nctx = ctx.shape[1]
    depth = w_mod.shape[0]
    assert d == D_MODEL and seq % 256 == 0 and nctx % MCH == 0
    sel = _stream_selectors()

    tm = min(512, seq)
    tm_mlp = min(512, seq)
    tm_ctx = min(256, nctx)
    tq = min(8 * ATTN_SUB, seq)
    tq_ctx = min(ATTN_SUB, nctx)

    dft_cc, dft_cs = _channel_dft()
    fft_lat = _fourier4_tables(seq)
    dft_ctx = _position_dft(nctx)
    rope_lat = _rope_tables(seq, True)
    rope_ctx = _rope_tables(nctx, False)
    wts = _prepare_weights(max(seq, nctx) // MCH, g_norm1, g_norm2, w_in, b_gates, conv_qk, g_mlstm,
                           g_q_norm, g_kv_norm, w_uq, w_ukv, w_out, w_up, w_down)
    gfin = g_final.reshape(1, d)

    rows = ((nb + 1 + 7) // 8) * 8
    cc = jnp.concatenate([c, c_ctx[None, :], jnp.zeros((rows - nb - 1, d), F32)], axis=0)
    mod_all = _modulation(cc, w_mod, b_mod).reshape(depth * rows, 6, d)

    xl = x.reshape(nb * seq, d)
    xc = ctx.reshape(nb * nctx, d)

    for l in range(depth):
        last = l == depth - 1
        row_lat, row_ctx = l * rows, l * rows + nb

        zc, zs, mqk, vt, mo, gr, q_a, k_a, v_a = _inproj(
            xl, seq, l, wts, mod_all, row_lat, True, dft_cc, dft_cs, *rope_lat, tm, True)
        zc_c, zs_c, mqk_c, vt_c, mo_c, gr_c, q_ac, k_ac, v_ac = _inproj(
            xc, nctx, l, wts, mod_all, row_ctx, False, dft_cc, dft_cs, *rope_ctx, tm_ctx, not last)

        yf = _fourier4(fft_lat, zc, zs)

        def mlstm_inputs(mqk_s, vt_s, gr_s, n):
            q_s, k_s = _conv_silu(mqk_s.reshape(nb, n, 2 * MP_WIDTH), l, wts)
            return (q_s, k_s, vt_s, gr_s)

        hm, hm_c = _mlstm(mlstm_inputs(mqk, vt, gr, seq), mlstm_inputs(mqk_c, vt_c, gr_c, nctx), l, wts, sel)

        b3 = lambda a, n: a.reshape(nb, n, AP_WIDTH)
        keys_ctx = (b3(k_ac, nctx), b3(v_ac, nctx))
        ya = _attention(b3(q_a, seq), [(b3(k_a, seq), b3(v_a, seq)), keys_ctx], tq)

        xl = _out_mlp(xl, seq, yf, hm.reshape(nb * seq, MP_WIDTH), mo, ya.reshape(nb * seq, AP_WIDTH),
                      l, wts, mod_all, row_lat, True, gfin, tm_mlp, last)

        if not last:
            yf_c = _fourier(*dft_ctx, zc_c, zs_c)
            ya_c = _attention(b3(q_ac, nctx), [keys_ctx], tq_ctx)
            xc = _out_mlp(xc, nctx, yf_c, hm_c.reshape(nb * nctx, MP_WIDTH), mo_c,
                          ya_c.reshape(nb * nctx, AP_WIDTH), l, wts, mod_all, row_ctx, False, gfin, tm_ctx, False)

    return xl.reshape(nb, seq, d)
```

```python
import functools

import numpy as np
import jax
import jax.numpy as jnp
from jax import lax
from jax.experimental import pallas as pl
from jax.experimental.pallas import tpu as pltpu

D_MODEL = 1024
GRID_W = 64
EPS = 1e-6
F_GROUPS = 4
F_GROUP_DIM = D_MODEL // 16
F_WIDTH = F_GROUPS * F_GROUP_DIM
M_HEADS = 4
M_HEAD_DIM = 3 * D_MODEL // 32
M_WIDTH = M_HEADS * M_HEAD_DIM
K_CONV = 5
A_HEADS = 4
A_NOPE = 64
A_ROPE = 32
A_V = 3 * D_MODEL // 32
Q_LORA = D_MODEL // 4
KV_LORA = D_MODEL // 8
ROPE_THETA = 10000.0
MLP_HIDDEN = 4 * D_MODEL

LANES = 128
HEAD_PAD = 128
MP_WIDTH = M_HEADS * HEAD_PAD
AP_WIDTH = A_HEADS * HEAD_PAD
VMEM_LIMIT = 56 * 1024 * 1024
MCH = 128
N_STREAM = 2 * M_HEADS

OFF_PF = 0
OFF_MQ = OFF_PF + F_WIDTH
OFF_MK = OFF_MQ + MP_WIDTH
OFF_MO = OFF_MK + MP_WIDTH
OFF_CQ = OFF_MO + MP_WIDTH
OFF_CKV = OFF_CQ + Q_LORA
OFF_KR = OFF_CKV + KV_LORA
IN_PAD = OFF_KR + LANES

BF16 = jnp.bfloat16
F32 = jnp.float32


def _cparams(sem):
    return pltpu.CompilerParams(dimension_semantics=sem, vmem_limit_bytes=VMEM_LIMIT)


def _const_spec(shape):
    nd = len(shape)
    return pl.BlockSpec(shape, lambda *_: (0,) * nd, pipeline_mode=pl.Buffered(1))


def _layer_spec(arr, layer):
    nd = arr.ndim
    return pl.BlockSpec((1,) + arr.shape[1:], lambda *_: (layer,) + (0,) * (nd - 1), pipeline_mode=pl.Buffered(1))


def _split3(a):
    hi = a.astype(BF16)
    r1 = a - hi.astype(F32)
    mid = r1.astype(BF16)
    lo = (r1 - mid.astype(F32)).astype(BF16)
    return hi, mid, lo


def _dot(a, b):
    return jnp.dot(a, b, preferred_element_type=F32)


def _dot_nt(a, b):
    return lax.dot_general(a, b, (((1,), (1,)), ((), ())), preferred_element_type=F32)


def _dot_tn(a, b):
    return lax.dot_general(a, b, (((0,), (0,)), ((), ())), preferred_element_type=F32)


def _rms(x, g):
    return x * lax.rsqrt(jnp.mean(x * x, axis=-1, keepdims=True) + EPS) * g


def _mod_kernel(c_ref, w_ref, b_ref, o_ref):
    c = c_ref[...]
    a = c * jax.nn.sigmoid(c)
    a_hi = a.astype(BF16)
    a_lo = (a - a_hi.astype(F32)).astype(BF16)
    w = w_ref[0]
    w_hi = w.astype(BF16)
    w_lo = (w - w_hi.astype(F32)).astype(BF16)
    acc = _dot(a_hi, w_hi) + _dot(a_hi, w_lo) + _dot(a_lo, w_hi)
    o_ref[0] = acc + b_ref[0]


def _modulation(cc, w_mod, b_mod):
    depth, d, n = w_mod.shape
    rows = cc.shape[0]
    tn = 1536
    return pl.pallas_call(
        _mod_kernel,
        grid=(depth, n // tn),
        in_specs=[
            pl.BlockSpec((rows, d), lambda l, j: (0, 0)),
            pl.BlockSpec((1, d, tn), lambda l, j: (l, 0, j)),
            pl.BlockSpec((1, 1, tn), lambda l, j: (l, 0, j)),
        ],
        out_specs=pl.BlockSpec((1, rows, tn), lambda l, j: (l, 0, j)),
        out_shape=jax.ShapeDtypeStruct((depth, rows, n), F32),
        compiler_params=_cparams(("arbitrary", "arbitrary")),
        name="modulation",
    )(cc, w_mod, b_mod.reshape(depth, 1, n))


def _rope(x, cos, sin, first_half):
    half = A_ROPE // 4
    partner = jnp.where(first_half, pltpu.roll(x, LANES - half, 1), pltpu.roll(x, half, 1))
    return x * cos + partner * sin


def _inproj_kernel(x_ref, g_ref, mod_ref, w_ref, wvt_ref, cc_ref, cs_ref,
                   cos_ref, sin_ref, gq_ref, gkv_ref, wq_ref, wk_ref, wv_ref,
                   zc_ref, zs_ref, mqk_ref, vt_ref, mo_ref, gr_ref, qa_ref, ka_ref, va_ref, *, with_q):
    x = x_ref[...]
    mod = mod_ref[0]
    h = _rms(x, g_ref[0]) * (1.0 + mod[1:2]) + mod[0:1]
    hb = h.astype(BF16)

    def proj(off, width):
        return _dot(hb, w_ref[0, :, off:off + width])

    ckv_kr = proj(OFF_CKV, 2 * LANES)
    cq = proj(OFF_CQ, Q_LORA) if with_q else None
    pf = proj(OFF_PF, F_WIDTH).astype(BF16)

    mqk_ref[...] = proj(OFF_MQ, 2 * MP_WIDTH)

    cos = cos_ref[...]
    sin = sin_ref[...]
    lane = lax.broadcasted_iota(jnp.int32, cos.shape, 1)
    first_half = ((lane - A_NOPE) & (A_ROPE // 2 - 1)) < A_ROPE // 4
    kvn = _rms(ckv_kr[:, :KV_LORA], gkv_ref[0]).astype(BF16)
    k_rope = _rope(ckv_kr[:, KV_LORA:], cos, sin, first_half)
    k_nope = _dot(kvn, wk_ref[0])
    va_ref[...] = _dot(kvn, wv_ref[0]).astype(BF16)
    for hd in range(A_HEADS):
        lanes = slice(hd * HEAD_PAD, (hd + 1) * HEAD_PAD)
        ka_ref[:, lanes] = (k_nope[:, lanes] + k_rope).astype(BF16)
    if with_q:
        qn = _rms(cq, gq_ref[0]).astype(BF16)
        q_all = _dot(qn, wq_ref[0])
        q_raw = [q_all[:, hd * HEAD_PAD:(hd + 1) * HEAD_PAD] for hd in range(A_HEADS)]
    zc_ref[...] = _dot(pf, cc_ref[...]).astype(BF16)
    zs_ref[...] = _dot(pf, cs_ref[...]).astype(BF16)

    vg = _dot_nt(wvt_ref[0], hb)
    vt = vg[:MP_WIDTH].astype(BF16)
    for j in range(vt_ref.shape[0]):
        vt_ref[j] = vt[:, j * MCH:(j + 1) * MCH]
        gr_ref[j] = vg[MP_WIDTH:, j * MCH:(j + 1) * MCH]
    mo_ref[...] = proj(OFF_MO, MP_WIDTH)

    if with_q:
        scale = (A_NOPE + A_ROPE) ** -0.5
        for hd in range(A_HEADS):
            q = _rope(q_raw[hd], cos, sin, first_half)
            qa_ref[:, hd * HEAD_PAD:(hd + 1) * HEAD_PAD] = (q * scale).astype(BF16)
    else:
        qa_ref[...] = jnp.zeros_like(qa_ref)


def _mod_map(mod_row0, per_batch_mod, tiles_per_seq):
    if per_batch_mod:
        return lambda i: (mod_row0 + i // tiles_per_seq, 0, 0)
    return lambda i: (mod_row0, 0, 0)


def _inproj(x2d, seq, layer, wts, mod, mod_row0, per_batch_mod, dft_cc, dft_cs, cos, sin, tm, with_q):
    t, d = x2d.shape
    nb = t // seq
    tiles_per_seq = seq // tm
    tok = lambda w: pl.BlockSpec((tm, w), lambda i: (i, 0))
    z_spec = pl.BlockSpec((tm, F_WIDTH), lambda i: (i % tiles_per_seq, i // tiles_per_seq))
    pos = pl.BlockSpec((tm, LANES), lambda i: (i % tiles_per_seq, 0))
    heads_bf16 = jax.ShapeDtypeStruct((t, AP_WIDTH), BF16)
    shapes = [
        jax.ShapeDtypeStruct((seq, nb * F_WIDTH), BF16),
        jax.ShapeDtypeStruct((seq, nb * F_WIDTH), BF16),
        jax.ShapeDtypeStruct((t, 2 * MP_WIDTH), F32),
        jax.ShapeDtypeStruct((t // MCH, MP_WIDTH, MCH), BF16),
        jax.ShapeDtypeStruct((t, MP_WIDTH), F32),
        jax.ShapeDtypeStruct((t // MCH, 2 * N_STREAM, MCH), F32),
        heads_bf16, heads_bf16, heads_bf16,
    ]
    vt_spec = pl.BlockSpec((tm // MCH, MP_WIDTH, MCH), lambda i: (i, 0, 0))
    gr_spec = pl.BlockSpec((tm // MCH, 2 * N_STREAM, MCH), lambda i: (i, 0, 0))
    out_specs = [z_spec, z_spec, tok(2 * MP_WIDTH), vt_spec, tok(MP_WIDTH), gr_spec,
                 tok(AP_WIDTH), tok(AP_WIDTH), tok(AP_WIDTH)]
    lay = lambda name: _layer_spec(wts[name], layer)
    return pl.pallas_call(
        functools.partial(_inproj_kernel, with_q=with_q),
        grid=(t // tm,),
        in_specs=[
            tok(d),
            lay("g1"),
            pl.BlockSpec((1, 6, d), _mod_map(mod_row0, per_batch_mod, tiles_per_seq)),
            lay("w_in_p"), lay("w_vt"),
            _const_spec((F_WIDTH, F_WIDTH)),
            _const_spec((F_WIDTH, F_WIDTH)),
            pos, pos,
            lay("gq"), lay("gkv"), lay("wq"), lay("wk"), lay("wv"),
        ],
        out_specs=out_specs,
        out_shape=shapes,
        compiler_params=_cparams(("arbitrary",)),
        name="inproj",
    )(x2d, wts["g1"], mod, wts["w_in_p"], wts["w_vt"], dft_cc, dft_cs, cos, sin,
      wts["gq"], wts["gkv"], wts["wq"], wts["wk"], wts["wv"])


def _fourier_kernel(c_ref, s_ref, zc_ref, zs_ref, o_ref):
    y = _dot(c_ref[...], zc_ref[...]) - _dot(s_ref[...], zs_ref[...])
    for slab in range(o_ref.shape[0]):
        o_ref[slab] = y[:, slab * LANES:(slab + 1) * LANES]


def _fourier(dft_c, dft_s, zc, zs):
    n, cols = zc.shape
    tr = min(n, 512)
    tc = min(cols, 512)
    return pl.pallas_call(
        _fourier_kernel,
        grid=(n // tr, cols // tc),
        in_specs=[
            pl.BlockSpec((tr, n), lambda i, j: (i, 0)),
            pl.BlockSpec((tr, n), lambda i, j: (i, 0)),
            pl.BlockSpec((n, tc), lambda i, j: (0, j)),
            pl.BlockSpec((n, tc), lambda i, j: (0, j)),
        ],
        out_specs=pl.BlockSpec((tc // LANES, tr, LANES), lambda i, j: (j, i, 0)),
        out_shape=jax.ShapeDtypeStruct((cols // LANES, n, LANES), F32),
        compiler_params=_cparams(("arbitrary", "arbitrary")),
        name="fourier",
    )(dft_c, dft_s, zc, zs)


FFT_COLS = 256


def _fourier4_kernel(tab_ref, twc_ref, tws_ref, zc_ref, zs_ref, o_ref):
    m = zc_ref.shape[0] // 4
    reps = zc_ref.shape[1] // LANES
    c0, c1, c2, c3 = (zc_ref[j * m:(j + 1) * m, :].astype(F32) for j in range(4))
    s0, s1, s2, s3 = (zs_ref[j * m:(j + 1) * m, :].astype(F32) for j in range(4))
    ce, co, cd, cu = c0 + c2, c1 + c3, c0 - c2, c1 - c3
    se, so, sd, su = s0 + s2, s1 + s3, s0 - s2, s1 - s3

    def emit(br, bi, k):
        if k:
            cos = jnp.concatenate([twc_ref[k - 1]] * reps, axis=1)
            sin = jnp.concatenate([tws_ref[k - 1]] * reps, axis=1)
            br, bi = br * cos + bi * sin, bi * cos - br * sin
        stacked = jnp.concatenate([br.astype(BF16), bi.astype(BF16)], axis=0)
        y = _dot(tab_ref[...], stacked)
        for slab in range(reps):
            o_ref[slab, pl.ds(k, m, stride=4), :] = y[:, slab * LANES:(slab + 1) * LANES]

    emit(ce + co, -(se + so), 0)
    emit(cd - su, -sd - cu, 1)
    emit(ce - co, so - se, 2)
    emit(cd + su, cu - sd, 3)


def _fourier4(tables, zc, zs):
    tab, twc, tws = tables
    n, cols = zc.shape
    m = n // 4
    tc = min(cols, FFT_COLS)
    return pl.pallas_call(
        _fourier4_kernel,
        grid=(cols // tc,),
        in_specs=[
            _const_spec((m, 2 * m)),
            _const_spec((3, m, LANES)),
            _const_spec((3, m, LANES)),
            pl.BlockSpec((n, tc), lambda j: (0, j)),
            pl.BlockSpec((n, tc), lambda j: (0, j)),
        ],
        out_specs=pl.BlockSpec((tc // LANES, n, LANES), lambda j: (j, 0, 0)),
        out_shape=jax.ShapeDtypeStruct((cols // LANES, n, LANES), F32),
        compiler_params=_cparams(("arbitrary",)),
        name="fourier4",
    )(tab, twc, tws, zc, zs)


def _fourier4_tables(n):
    m = n // 4
    idx = (np.arange(m, dtype=np.int64)[:, None] * np.arange(m, dtype=np.int64)[None, :]) % m
    ang = 2.0 * np.pi * idx.astype(np.float64) / m
    tab = np.concatenate([np.cos(ang), np.sin(ang)], axis=1) / np.sqrt(n)
    theta = 2.0 * np.pi * np.arange(m, dtype=np.float64)[None, :] * np.arange(1, 4, dtype=np.float64)[:, None] / n
    bcast = lambda t: jnp.asarray(np.repeat(t[:, :, None], LANES, axis=2), dtype=F32)
    return jnp.asarray(tab, dtype=F32).astype(BF16), bcast(np.cos(theta)), bcast(np.sin(theta))


CONV_ROWS = 256
CONV_HALO = 8


def _conv_kernel(u_ref, w_ref, q_ref, k_ref, pad_ref):
    n = u_ref.shape[1]
    width = u_ref.shape[2]
    zeros = jnp.zeros((CONV_HALO, width), F32)
    pad_ref[0:CONV_HALO, :] = zeros
    pad_ref[CONV_HALO + n:2 * CONV_HALO + n, :] = zeros
    pad_ref[CONV_HALO:CONV_HALO + n, :] = u_ref[0]
    w = w_ref[0]
    rows = min(CONV_ROWS, n)
    span = rows + 2 * CONV_HALO
    for r in range(n // rows):
        block = pad_ref[r * rows:r * rows + span, :]
        acc = None
        for j in range(K_CONV):
            shift = (K_CONV // 2 - j) % span
            tap = block if shift == 0 else pltpu.roll(block, shift, 0)
            term = tap[CONV_HALO:CONV_HALO + rows, :] * w[j:j + 1, :]
            acc = term if acc is None else acc + term
        act = acc * jax.nn.sigmoid(acc)
        q_ref[0, r * rows:(r + 1) * rows, :] = (act[:, :MP_WIDTH] * (M_HEAD_DIM ** -0.5)).astype(BF16)
        k_ref[0, r * rows:(r + 1) * rows, :] = act[:, MP_WIDTH:].astype(BF16)


def _conv_silu(mqk, layer, wts):
    nb, n, width = mqk.shape
    return pl.pallas_call(
        _conv_kernel,
        grid=(nb,),
        in_specs=[
            pl.BlockSpec((1, n, width), lambda b: (b, 0, 0)),
            _layer_spec(wts["conv_p"], layer),
        ],
        out_specs=[
            pl.BlockSpec((1, n, MP_WIDTH), lambda b: (b, 0, 0)),
            pl.BlockSpec((1, n, MP_WIDTH), lambda b: (b, 0, 0)),
        ],
        out_shape=[
            jax.ShapeDtypeStruct((nb, n, MP_WIDTH), BF16),
            jax.ShapeDtypeStruct((nb, n, MP_WIDTH), BF16),
        ],
        scratch_shapes=[pltpu.VMEM((n + 2 * CONV_HALO, width), F32)],
        compiler_params=_cparams(("arbitrary",)),
        name="conv_silu",
    )(mqk, wts["conv_p"])


def _log_sigmoid(x):
    return jnp.minimum(x, 0.0) - jnp.log(1.0 + jnp.exp(-jnp.abs(x)))


def _exact_dot_01(a, tri_bf16, a_on_left):
    out = None
    for term in _split3(a):
        d = _dot(term, tri_bf16) if a_on_left else _dot(tri_bf16, term)
        out = d if out is None else out + d
    return out


ONE_ROW = M_HEAD_DIM
(F_R, F_B, F_CM, F_TOT, F_CML, F_MP, F_A, F_WI, F_ELD, F_DEC, F_WK, F_HI, F_MID, F_LO) = range(14)
N_FIELDS = 14
STATE_GROUP = 4
OUT_GROUP = 8
SEL_ROWS = 32


def _mlstm_kernel(ql_ref, kl_ref, vtl_ref, grl_ref,
                  qc_ref, kc_ref, vtc_ref, grc_ref,
                  bir_ref, bfr_ref, sel_ref,
                  hl_ref, hc_ref,
                  ct_st, st_sc, rows_sc):
    L = MCH
    nh = M_HEADS
    ncc = qc_ref.shape[1] // L
    ncl = ql_ref.shape[1] // L

    d0 = lax.broadcasted_iota(jnp.int32, (L, L), 0)
    d1 = lax.broadcasted_iota(jnp.int32, (L, L), 1)
    le = d0 <= d1
    ge = d0 >= d1
    tri_le = le.astype(BF16)
    fwd_rows = lax.broadcasted_iota(jnp.int32, (N_STREAM, L), 0) < nh
    feat = lax.broadcasted_iota(jnp.int32, (HEAD_PAD, L), 0)
    one_row = feat == ONE_ROW
    keep_rows = feat < M_HEAD_DIM

    ct_st[...] = jnp.zeros_like(ct_st)

    def field(f, slot0, nc):
        return rows_sc[f, slot0:slot0 + nc].reshape(nc * N_STREAM, L)

    def set_field(f, slot0, nc, val):
        rows_sc[f, slot0:slot0 + nc] = val.reshape(nc, N_STREAM, L)

    def gate_pass(gr_ref, nc, slot0):
        n8 = nc * N_STREAM
        fwd = (lax.broadcasted_iota(jnp.int32, (n8, L), 0) & (N_STREAM - 1)) < nh
        lane = lax.broadcasted_iota(jnp.int32, (n8, L), 1)
        gi = gr_ref[:, 0:N_STREAM, :].reshape(n8, L) + bir_ref[0, 0:n8, :]
        f = _log_sigmoid(gr_ref[:, N_STREAM:2 * N_STREAM, :].reshape(n8, L) + bfr_ref[0, 0:n8, :])
        pre = _exact_dot_01(f, tri_le, a_on_left=True)
        total = jnp.sum(f, axis=1, keepdims=True)
        b = jnp.where(fwd, pre, total - pre + f)
        r = gi - b
        cm = r
        sh = 1
        while sh < L:
            from_left = jnp.where(lane >= sh, pltpu.roll(cm, sh, 1), -jnp.inf)
            from_right = jnp.where(lane < L - sh, pltpu.roll(cm, L - sh, 1), -jnp.inf)
            cm = jnp.maximum(cm, jnp.where(fwd, from_left, from_right))
            sh *= 2
        set_field(F_R, slot0, nc, r)
        set_field(F_B, slot0, nc, b)
        set_field(F_CM, slot0, nc, cm)
        set_field(F_TOT, slot0, nc, jnp.broadcast_to(total, (n8, L)))
        set_field(F_CML, slot0, nc, jnp.broadcast_to(jnp.max(r, axis=1, keepdims=True), (n8, L)))

    def m_scan(nc, slot0, m0):
        def step(j, m):
            sf = slot0 + j
            sb = slot0 + nc - 1 - j
            tot = jnp.where(fwd_rows, rows_sc[F_TOT, sf], rows_sc[F_TOT, sb])
            cml = jnp.where(fwd_rows, rows_sc[F_CML, sf], rows_sc[F_CML, sb])
            rows_sc[F_MP, sf, 0:nh, :] = m[0:nh]
            rows_sc[F_MP, sb, nh:N_STREAM, :] = m[nh:N_STREAM]
            return tot + jnp.maximum(m, cml)

        return lax.fori_loop(0, nc, step, m0)

    def weight_pass(nc, slot0):
        mp, cm, b, r = (field(f, slot0, nc) for f in (F_MP, F_CM, F_B, F_R))
        big = jnp.maximum(mp, field(F_CML, slot0, nc))
        a = -jnp.maximum(mp, cm)
        set_field(F_A, slot0, nc, a)
        set_field(F_WI, slot0, nc, jnp.exp(mp + a))
        set_field(F_ELD, slot0, nc, jnp.exp(a - b))
        set_field(F_DEC, slot0, nc, jnp.exp(mp - big))
        set_field(F_WK, slot0, nc, jnp.exp(r - big))
        for f, term in zip((F_HI, F_MID, F_LO), _split3(r)):
            set_field(f, slot0, nc, term.astype(F32))

    def value_slab(vt_ref, c, hd):
        vt = vt_ref[c, hd * HEAD_PAD:(hd + 1) * HEAD_PAD, :].astype(F32)
        return jnp.where(one_row, 1.0, vt)

    def state_pass(k_ref, vt_ref, nc, slot0):
        group = min(STATE_GROUP, nc)

        def updates_of(j):
            uts = []
            for sidx in range(N_STREAM):
                hd = sidx % nh
                c = j if sidx < nh else nc - 1 - j
                wk = rows_sc[F_WK, slot0 + c, sidx:sidx + 1, :]
                kk = k_ref[0, pl.ds(pl.multiple_of(c * L, L), L), hd * HEAD_PAD:(hd + 1) * HEAD_PAD]
                uts.append(_dot((value_slab(vt_ref, c, hd) * wk).astype(BF16), kk))
            return uts

        def step(g, carry):
            all_uts = [updates_of(g * group + u) for u in range(group)]
            for u in range(group):
                j = g * group + u
                for sidx in range(N_STREAM):
                    slot = slot0 + (j if sidx < nh else nc - 1 - j)
                    prev = ct_st[sidx]
                    st_sc[sidx, slot] = prev.astype(BF16)
                    ct_st[sidx] = rows_sc[F_DEC, slot, sidx:sidx + 1, :] * prev + all_uts[u][sidx]
            return carry

        lax.fori_loop(0, nc // group, step, 0)

    def output_pass(q_ref, k_ref, vt_ref, out_ref, nc, slot0):
        head_lanes = [slice(hd * HEAD_PAD, (hd + 1) * HEAD_PAD) for hd in range(nh)]

        def independent_matmuls(c):
            rows = pl.ds(pl.multiple_of(c * L, L), L)
            slot = slot0 + c
            r3 = jnp.concatenate([rows_sc[F_HI, slot], rows_sc[F_MID, slot], rows_sc[F_LO, slot],
                                  jnp.zeros((N_STREAM, L), F32)], axis=0).astype(BF16)
            qs = [q_ref[0, rows, lanes] for lanes in head_lanes]
            s_ts = [_dot_nt(k_ref[0, rows, lanes], q) for lanes, q in zip(head_lanes, qs)]
            inters = [_dot_nt(st_sc[sidx, slot], qs[sidx % nh]) for sidx in range(N_STREAM)]
            r_all = _dot_tn(r3, sel_ref[...])
            r_ts = [r_all[:, sidx * L:(sidx + 1) * L] for sidx in range(N_STREAM)]
            return s_ts, inters, r_ts

        def finish(c, s_ts, inters, r_ts):
            rows = pl.ds(pl.multiple_of(c * L, L), L)
            slot = slot0 + c
            a_rows = rows_sc[F_A, slot]
            wi_rows = rows_sc[F_WI, slot]
            eld_rows = rows_sc[F_ELD, slot]
            p_ts = []
            for sidx in range(N_STREAM):
                one = slice(sidx, sidx + 1)
                dm = jnp.where(le if sidx < nh else ge, r_ts[sidx] + a_rows[one, :], -jnp.inf)
                p_ts.append((s_ts[sidx % nh] * jnp.exp(dm)).astype(BF16))
            z_pairs = [_dot(value_slab(vt_ref, c, hd).astype(BF16),
                            jnp.concatenate([p_ts[hd], p_ts[nh + hd]], axis=1)) for hd in range(nh)]
            for hd in range(nh):
                hsum = None
                for dr, sidx in enumerate((hd, nh + hd)):
                    one = slice(sidx, sidx + 1)
                    z_t = z_pairs[hd][:, dr * L:(dr + 1) * L] + inters[sidx] * wi_rows[one, :]
                    den = z_t[ONE_ROW:ONE_ROW + 1, :]
                    h_t = z_t * (1.0 / jnp.maximum(jnp.abs(den), eld_rows[one, :]))
                    hsum = h_t if hsum is None else hsum + h_t
                out_ref[0, rows, head_lanes[hd]] = jnp.where(keep_rows, hsum, 0.0).T

        group = min(OUT_GROUP, nc)

        def step(g, carry):
            ahead = independent_matmuls(g * group)
            for u in range(group):
                cur = ahead
                if u + 1 < group:
                    ahead = independent_matmuls(g * group + u + 1)
                finish(g * group + u, *cur)
            return carry

        lax.fori_loop(0, nc // group, step, 0)

    gate_pass(grc_ref, ncc, 0)
    gate_pass(grl_ref, ncl, ncc)
    m1 = m_scan(ncc, 0, jnp.zeros((N_STREAM, L), F32))
    m_scan(ncl, ncc, m1)
    weight_pass(ncc, 0)
    weight_pass(ncl, ncc)
    state_pass(kc_ref, vtc_ref, ncc, 0)
    state_pass(kl_ref, vtl_ref, ncl, ncc)
    output_pass(qc_ref, kc_ref, vtc_ref, hc_ref, ncc, 0)
    output_pass(ql_ref, kl_ref, vtl_ref, hl_ref, ncl, ncc)


def _mlstm(lat, ctx, layer, wts, sel):
    nb, n, _ = lat[0].shape
    nctx = ctx[0].shape[1]
    assert MCH == LANES == HEAD_PAD
    nct = (n + nctx) // MCH
    assert wts["bir"].shape[1] >= max(n, nctx) // MCH * N_STREAM

    def specs(rows):
        nc = rows // MCH
        seq = lambda w: pl.BlockSpec((1, rows, w), lambda b: (b, 0, 0))
        chunked = lambda rows_: pl.BlockSpec((nc, rows_, MCH), lambda b: (b, 0, 0))
        return [seq(MP_WIDTH), seq(MP_WIDTH), chunked(MP_WIDTH), chunked(2 * N_STREAM)]

    out_spec = lambda rows: pl.BlockSpec((1, rows, MP_WIDTH), lambda b: (b, 0, 0))
    return pl.pallas_call(
        _mlstm_kernel,
        grid=(nb,),
        in_specs=specs(n) + specs(nctx) + [
            _layer_spec(wts["bir"], layer), _layer_spec(wts["bfr"], layer),
            _const_spec((SEL_ROWS, N_STREAM * MCH)),
        ],
        out_specs=[out_spec(n), out_spec(nctx)],
        out_shape=[
            jax.ShapeDtypeStruct((nb, n, MP_WIDTH), F32),
            jax.ShapeDtypeStruct((nb, nctx, MP_WIDTH), F32),
        ],
        scratch_shapes=[
            pltpu.VMEM((N_STREAM, HEAD_PAD, HEAD_PAD), F32),
            pltpu.VMEM((N_STREAM, nct, HEAD_PAD, HEAD_PAD), BF16),
            pltpu.VMEM((N_FIELDS, nct, N_STREAM, MCH), F32),
        ],
        compiler_params=_cparams(("arbitrary",)),
        name="mlstm",
    )(*lat, *ctx, wts["bir"], wts["bfr"], sel)


ATTN_SUB = 256
ATTN_HEADS_PER_STEP = 1


def _attn_kernel(*refs, n_sets):
    q_ref = refs[0]
    kv_refs = refs[1:1 + 2 * n_sets]
    o_ref = refs[1 + 2 * n_sets]
    sub = min(ATTN_SUB, q_ref.shape[1])
    n_sub = q_ref.shape[1] // sub
    items = [(hd, t) for hd in range(q_ref.shape[2] // HEAD_PAD) for t in range(n_sub)]

    def scores_of(item):
        hd, t = item
        lanes = slice(hd * HEAD_PAD, (hd + 1) * HEAD_PAD)
        q = q_ref[0, t * sub:(t + 1) * sub, lanes]
        return [_dot_nt(q, kv_refs[2 * i][0, :, lanes]) for i in range(n_sets)]

    nxt = scores_of(items[0])
    for idx, (hd, t) in enumerate(items):
        rows = slice(t * sub, (t + 1) * sub)
        lanes = slice(hd * HEAD_PAD, (hd + 1) * HEAD_PAD)
        scores = nxt
        if idx + 1 < len(items):
            nxt = scores_of(items[idx + 1])
        m = None
        for s in scores:
            sm = jnp.max(s, axis=-1, keepdims=True)
            m = sm if m is None else jnp.maximum(m, sm)
        acc = None
        den = None
        for i, s in enumerate(scores):
            p = jnp.exp(s - m)
            l = jnp.sum(p, axis=-1, keepdims=True)
            o = _dot(p.astype(BF16), kv_refs[2 * i + 1][0, :, lanes])
            acc = o if acc is None else acc + o
            den = l if den is None else den + l
        o_ref[0, rows, lanes] = (acc / den).astype(BF16)


def _attention(q, key_sets, tq, heads_per_step):
    nb, n, _ = q.shape
    n_sets = len(key_sets)
    width = heads_per_step * HEAD_PAD
    in_specs = [pl.BlockSpec((1, tq, width), lambda b, h, i: (b, i, h))]
    args = [q]
    for k, v in key_sets:
        nk = k.shape[1]
        spec = pl.BlockSpec((1, nk, width), lambda b, h, i: (b, 0, h))
        in_specs += [spec, spec]
        args += [k, v]
    return pl.pallas_call(
        functools.partial(_attn_kernel, n_sets=n_sets),
        grid=(nb, A_HEADS // heads_per_step, n // tq),
        in_specs=in_specs,
        out_specs=pl.BlockSpec((1, tq, width), lambda b, h, i: (b, i, h)),
        out_shape=jax.ShapeDtypeStruct((nb, n, AP_WIDTH), BF16),
        compiler_params=_cparams(("arbitrary", "arbitrary", "arbitrary")),
        name="attention",
    )(*args)


MLP_CHUNK = 1024


def _out_mlp_kernel(x_ref, yf_ref, hm_ref, mo_ref, ya_ref, mod_ref,
                    gm_ref, g2_ref, gfin_ref, wof_ref, wom_ref, woa_ref, wup_ref, wdn_ref,
                    o_ref, *, final_norm):
    mod = mod_ref[0]
    ga1, sh2, sc2, ga2 = mod[2:3], mod[3:4], mod[4:5], mod[5:6]
    yf = jnp.concatenate([yf_ref[slab].astype(BF16) for slab in range(yf_ref.shape[0])], axis=1)
    mix = _dot(ya_ref[...], woa_ref[0]) + _dot(yf, wof_ref[0])
    gm = gm_ref[0]
    yms = []
    for hd in range(M_HEADS):
        lanes = slice(hd * HEAD_PAD, (hd + 1) * HEAD_PAD)
        hh = hm_ref[:, lanes]
        ms = jnp.sum(hh * hh, axis=-1, keepdims=True) * (1.0 / M_HEAD_DIM)
        ym = hh * lax.rsqrt(ms + EPS) * gm[:, lanes] * jax.nn.sigmoid(mo_ref[:, lanes])
        yms.append(ym.astype(BF16))
    mix = mix + _dot(jnp.concatenate(yms, axis=1), wom_ref[0])
    x1 = x_ref[...] + ga1 * mix
    h2 = (_rms(x1, g2_ref[0]) * (1.0 + sc2) + sh2).astype(BF16)
    acc = None
    for c in range(wup_ref.shape[2] // MLP_CHUNK):
        cols = slice(c * MLP_CHUNK, (c + 1) * MLP_CHUNK)
        u = jnp.maximum(_dot(h2, wup_ref[0, :, cols]), 0.0)
        d = _dot((u * u).astype(BF16), wdn_ref[0, cols, :])
        acc = d if acc is None else acc + d
    x2 = x1 + ga2 * acc
    if final_norm:
        x2 = _rms(x2, gfin_ref[...])
    o_ref[...] = x2


def _out_mlp(x2d, seq, yf, hm, mo, ya, layer, wts, mod, mod_row0, per_batch_mod, gfin, tm, final_norm):
    t, d = x2d.shape
    tiles_per_seq = seq // tm
    tok = lambda w: pl.BlockSpec((tm, w), lambda i: (i, 0))
    yf_spec = pl.BlockSpec((F_WIDTH // LANES, tm, LANES), lambda i: (i // tiles_per_seq, i % tiles_per_seq, 0))
    names = ("gm", "g2", "wof", "wom", "woa", "wup", "wdn")
    lay = {name: _layer_spec(wts[name], layer) for name in names}
    return pl.pallas_call(
        functools.partial(_out_mlp_kernel, final_norm=final_norm),
        grid=(t // tm,),
        in_specs=[
            tok(d), yf_spec, tok(MP_WIDTH), tok(MP_WIDTH), tok(AP_WIDTH),
            pl.BlockSpec((1, 6, d), _mod_map(mod_row0, per_batch_mod, tiles_per_seq)),
            lay["gm"], lay["g2"], _const_spec((1, d)),
            lay["wof"], lay["wom"], lay["woa"], lay["wup"], lay["wdn"],
        ],
        out_specs=tok(d),
        out_shape=jax.ShapeDtypeStruct((t, d), F32),
        compiler_params=_cparams(("arbitrary",)),
        name="out_mlp",
    )(x2d, yf, hm, mo, ya, mod, wts["gm"], wts["g2"], gfin,
      wts["wof"], wts["wom"], wts["woa"], wts["wup"], wts["wdn"])


def _dft_tables(n):
    idx = (np.arange(n, dtype=np.int64)[:, None] * np.arange(n, dtype=np.int64)[None, :]) % n
    ang = 2.0 * np.pi * idx.astype(np.float64) / n
    scale = 1.0 / np.sqrt(n)
    return np.cos(ang) * scale, np.sin(ang) * scale


def _channel_dft():
    c, s = _dft_tables(F_GROUP_DIM)
    eye = np.eye(F_GROUPS)
    return (jnp.asarray(np.kron(eye, c), dtype=F32).astype(BF16),
            jnp.asarray(np.kron(eye, s), dtype=F32).astype(BF16))


def _position_dft(n):
    c, s = _dft_tables(n)
    return jnp.asarray(c, dtype=F32).astype(BF16), jnp.asarray(s, dtype=F32).astype(BF16)


def _rope_tables(n, rotate):
    cos = np.zeros((n, HEAD_PAD), np.float32)
    sin = np.zeros((n, HEAD_PAD), np.float32)
    cos[:, :A_NOPE + A_ROPE] = 1.0
    if rotate:
        nf = A_ROPE // 4
        t = np.arange(n)
        row = (t // GRID_W).astype(np.float32)
        col = (t % GRID_W).astype(np.float32)
        freqs = (np.float32(ROPE_THETA) ** (-np.arange(nf, dtype=np.float32) / np.float32(nf))).astype(np.float32)
        for seg, pos in enumerate((row, col)):
            ang = pos[:, None] * freqs[None, :]
            c, s = np.cos(ang), np.sin(ang)
            base = A_NOPE + seg * 2 * nf
            cos[:, base:base + nf] = c
            cos[:, base + nf:base + 2 * nf] = c
            sin[:, base:base + nf] = -s
            sin[:, base + nf:base + 2 * nf] = s
    return jnp.asarray(cos), jnp.asarray(sin)


def _pad_heads_cols(w, heads, width):
    lead = w.shape[:-1]
    w = w.reshape(lead + (heads, width))
    w = jnp.pad(w, [(0, 0)] * len(lead) + [(0, 0), (0, HEAD_PAD - width)])
    return w.reshape(lead + (heads * HEAD_PAD,))


def _pad_heads_rows(w, heads, width):
    depth, _, n = w.shape
    w = jnp.pad(w.reshape(depth, heads, width, n), [(0, 0), (0, 0), (0, HEAD_PAD - width), (0, 0)])
    return w.reshape(depth, heads * HEAD_PAD, n)


GATE_I_COLS = np.concatenate([np.arange(M_HEADS), 2 * M_HEADS + np.arange(M_HEADS)])
GATE_F_COLS = GATE_I_COLS + M_HEADS


def _prepare_weights(max_chunks, g_norm1, g_norm2, w_in, b_gates, conv_qk, g_mlstm, g_q_norm, g_kv_norm,
                     w_uq, w_ukv, w_out, w_up, w_down):
    offs = np.cumsum([0, F_WIDTH, M_WIDTH, M_WIDTH, M_WIDTH, M_WIDTH, 4 * M_HEADS, Q_LORA, KV_LORA, A_ROPE])
    part = lambda i: w_in[:, :, offs[i]:offs[i + 1]]
    heads = lambda w: _pad_heads_cols(w, M_HEADS, M_HEAD_DIM)
    w_in_p = jnp.concatenate([
        part(0), heads(part(1)), heads(part(2)), heads(part(4)), part(6), part(7),
        jnp.pad(part(8), [(0, 0), (0, 0), (A_NOPE, LANES - A_NOPE - A_ROPE)]),
    ], axis=2).astype(BF16)
    assert w_in_p.shape[2] == IN_PAD
    gates = part(5)
    w_vt = jnp.concatenate([heads(part(3)), gates[:, :, GATE_I_COLS], gates[:, :, GATE_F_COLS]],
                           axis=2).transpose(0, 2, 1).astype(BF16)

    conv_p = jnp.concatenate([heads(conv_qk[:, :, :M_WIDTH]), heads(conv_qk[:, :, M_WIDTH:])], axis=2)
    conv_p = jnp.pad(conv_p, [(0, 0), (0, 8 - K_CONV), (0, 0)])

    tile_rows = lambda b: jnp.tile(b[:, :, None], (1, max_chunks, 1))
    ukv = w_ukv.reshape(w_ukv.shape[0], KV_LORA, A_HEADS, A_NOPE + A_V)
    pad_kv = lambda w: jnp.pad(w, [(0, 0), (0, 0), (0, 0), (0, HEAD_PAD - w.shape[-1])]).reshape(
        w.shape[0], KV_LORA, AP_WIDTH).astype(BF16)
    vec = lambda g: g[:, None, :]
    return dict(
        g1=vec(g_norm1), g2=vec(g_norm2), gq=vec(g_q_norm), gkv=vec(g_kv_norm),
        w_in_p=w_in_p, w_vt=w_vt, conv_p=conv_p,
        bir=tile_rows(b_gates[:, GATE_I_COLS]), bfr=tile_rows(b_gates[:, GATE_F_COLS]),
        gm=_pad_heads_cols(g_mlstm, M_HEADS, M_HEAD_DIM)[:, None, :],
        wq=_pad_heads_cols(w_uq, A_HEADS, A_NOPE + A_ROPE).astype(BF16),
        wk=pad_kv(ukv[..., :A_NOPE]), wv=pad_kv(ukv[..., A_NOPE:]),
        wof=w_out[:, :F_WIDTH].astype(BF16),
        wom=_pad_heads_rows(w_out[:, F_WIDTH:F_WIDTH + M_WIDTH], M_HEADS, M_HEAD_DIM).astype(BF16),
        woa=_pad_heads_rows(w_out[:, F_WIDTH + M_WIDTH:], A_HEADS, A_V).astype(BF16),
        wup=w_up.astype(BF16), wdn=w_down.astype(BF16),
    )


def _stream_selectors():
    sel = np.zeros((SEL_ROWS, N_STREAM, MCH), np.float32)
    for s in range(N_STREAM):
        for part in range(3):
            sel[part * N_STREAM + s, s, :] = 1.0
    return jnp.asarray(sel.reshape(SEL_ROWS, N_STREAM * MCH), dtype=BF16)


def kernel(x, c, ctx, c_ctx, w_mod, b_mod, g_norm1, g_norm2, w_in, b_gates, conv_qk, g_mlstm,
           g_q_norm, g_kv_norm, w_uq, w_ukv, w_out, w_up, w_down, g_final):
    nb, seq, d = x.shape
    nctx = ctx.shape[1]
    depth = w_mod.shape[0]
    assert d == D_MODEL and seq % 256 == 0 and nctx % MCH == 0
    sel = _stream_selectors()

    tm = min(512, seq)
    tm_mlp = min(512, seq)
    tm_ctx = min(256, nctx)
    tq = min(8 * ATTN_SUB, seq)
    tq_ctx = min(ATTN_SUB, nctx)

    dft_cc, dft_cs = _channel_dft()
    fft_lat = _fourier4_tables(seq)
    dft_ctx = _position_dft(nctx)
    rope_lat = _rope_tables(seq, True)
    rope_ctx = _rope_tables(nctx, False)
    wts = _prepare_weights(max(seq, nctx) // MCH, g_norm1, g_norm2, w_in, b_gates, conv_qk, g_mlstm,
                           g_q_norm, g_kv_norm, w_uq, w_ukv, w_out, w_up, w_down)
    gfin = g_final.reshape(1, d)

    rows = ((nb + 1 + 7) // 8) * 8
    cc = jnp.concatenate([c, c_ctx[None, :], jnp.zeros((rows - nb - 1, d), F32)], axis=0)
    mod_all = _modulation(cc, w_mod, b_mod).reshape(depth * rows, 6, d)

    xl = x.reshape(nb * seq, d)
    xc = ctx.reshape(nb * nctx, d)

    for l in range(depth):
        last = l == depth - 1
        row_lat, row_ctx = l * rows, l * rows + nb

        zc, zs, mqk, vt, mo, gr, q_a, k_a, v_a = _inproj(
            xl, seq, l, wts, mod_all, row_lat, True, dft_cc, dft_cs, *rope_lat, tm, True)
        zc_c, zs_c, mqk_c, vt_c, mo_c, gr_c, q_ac, k_ac, v_ac = _inproj(
            xc, nctx, l, wts, mod_all, row_ctx, False, dft_cc, dft_cs, *rope_ctx, tm_ctx, not last)

        yf = _fourier4(fft_lat, zc, zs)

        def mlstm_inputs(mqk_s, vt_s, gr_s, n):
            q_s, k_s = _conv_silu(mqk_s.reshape(nb, n, 2 * MP_WIDTH), l, wts)
            return (q_s, k_s, vt_s, gr_s)

        hm, hm_c = _mlstm(mlstm_inputs(mqk, vt, gr, seq), mlstm_inputs(mqk_c, vt_c, gr_c, nctx), l, wts, sel)

        b3 = lambda a, n: a.reshape(nb, n, AP_WIDTH)
        keys_ctx = (b3(k_ac, nctx), b3(v_ac, nctx))
        ya = _attention(b3(q_a, seq), [(b3(k_a, seq), b3(v_a, seq)), keys_ctx], tq, ATTN_HEADS_PER_STEP)

        xl = _out_mlp(xl, seq, yf, hm.reshape(nb * seq, MP_WIDTH), mo, ya.reshape(nb * seq, AP_WIDTH),
                      l, wts, mod_all, row_lat, True, gfin, tm_mlp, last)

        if not last:
            yf_c = _fourier(*dft_ctx, zc_c, zs_c)
            ya_c = _attention(b3(q_ac, nctx), [keys_ctx], tq_ctx, A_HEADS)
            xc = _out_mlp(xc, nctx, yf_c, hm_c.reshape(nb * nctx, MP_WIDTH), mo_c,
                          ya_c.reshape(nb * nctx, AP_WIDTH), l, wts, mod_all, row_ctx, False, gfin, tm_ctx, False)

    return xl.reshape(nb, seq, d)
```

```python
import functools

import numpy as np
import jax
import jax.numpy as jnp
from jax import lax
from jax.experimental import pallas as pl
from jax.experimental.pallas import tpu as pltpu

D_MODEL = 1024
GRID_W = 64
EPS = 1e-6
F_GROUPS = 4
F_GROUP_DIM = D_MODEL // 16
F_WIDTH = F_GROUPS * F_GROUP_DIM
M_HEADS = 4
M_HEAD_DIM = 3 * D_MODEL // 32
M_WIDTH = M_HEADS * M_HEAD_DIM
K_CONV = 5
A_HEADS = 4
A_NOPE = 64
A_ROPE = 32
A_V = 3 * D_MODEL // 32
Q_LORA = D_MODEL // 4
KV_LORA = D_MODEL // 8
ROPE_THETA = 10000.0
MLP_HIDDEN = 4 * D_MODEL

LANES = 128
HEAD_PAD = 128
MP_WIDTH = M_HEADS * HEAD_PAD
AP_WIDTH = A_HEADS * HEAD_PAD
VMEM_LIMIT = 56 * 1024 * 1024
MCH = 128
N_STREAM = 2 * M_HEADS

OFF_PF = 0
OFF_MQ = OFF_PF + F_WIDTH
OFF_MK = OFF_MQ + MP_WIDTH
OFF_MO = OFF_MK + MP_WIDTH
OFF_CQ = OFF_MO + MP_WIDTH
OFF_CKV = OFF_CQ + Q_LORA
OFF_KR = OFF_CKV + KV_LORA
IN_PAD = OFF_KR + LANES

BF16 = jnp.bfloat16
F32 = jnp.float32
LOG2_E = 1.4426950408889634


def _cparams(sem):
    return pltpu.CompilerParams(dimension_semantics=sem, vmem_limit_bytes=VMEM_LIMIT)


def _const_spec(shape):
    nd = len(shape)
    return pl.BlockSpec(shape, lambda *_: (0,) * nd, pipeline_mode=pl.Buffered(1))


def _layer_spec(arr, layer):
    nd = arr.ndim
    return pl.BlockSpec((1,) + arr.shape[1:], lambda *_: (layer,) + (0,) * (nd - 1), pipeline_mode=pl.Buffered(1))


def _split3(a):
    hi = a.astype(BF16)
    r1 = a - hi.astype(F32)
    mid = r1.astype(BF16)
    lo = (r1 - mid.astype(F32)).astype(BF16)
    return hi, mid, lo


def _dot(a, b):
    return jnp.dot(a, b, preferred_element_type=F32)


def _dot_nt(a, b):
    return lax.dot_general(a, b, (((1,), (1,)), ((), ())), preferred_element_type=F32)


def _dot_tn(a, b):
    return lax.dot_general(a, b, (((0,), (0,)), ((), ())), preferred_element_type=F32)


def _rms(x, g):
    return x * lax.rsqrt(jnp.mean(x * x, axis=-1, keepdims=True) + EPS) * g


def _mod_kernel(c_ref, w_ref, b_ref, o_ref):
    c = c_ref[...]
    a = c * jax.nn.sigmoid(c)
    a_hi = a.astype(BF16)
    a_lo = (a - a_hi.astype(F32)).astype(BF16)
    w = w_ref[0]
    w_hi = w.astype(BF16)
    w_lo = (w - w_hi.astype(F32)).astype(BF16)
    acc = _dot(a_hi, w_hi) + _dot(a_hi, w_lo) + _dot(a_lo, w_hi)
    o_ref[0] = acc + b_ref[0]


def _modulation(cc, w_mod, b_mod):
    depth, d, n = w_mod.shape
    rows = cc.shape[0]
    tn = 1536
    return pl.pallas_call(
        _mod_kernel,
        grid=(depth, n // tn),
        in_specs=[
            pl.BlockSpec((rows, d), lambda l, j: (0, 0)),
            pl.BlockSpec((1, d, tn), lambda l, j: (l, 0, j)),
            pl.BlockSpec((1, 1, tn), lambda l, j: (l, 0, j)),
        ],
        out_specs=pl.BlockSpec((1, rows, tn), lambda l, j: (l, 0, j)),
        out_shape=jax.ShapeDtypeStruct((depth, rows, n), F32),
        compiler_params=_cparams(("arbitrary", "arbitrary")),
        name="modulation",
    )(cc, w_mod, b_mod.reshape(depth, 1, n))


def _rope(x, cos, sin, first_half):
    half = A_ROPE // 4
    partner = jnp.where(first_half, pltpu.roll(x, LANES - half, 1), pltpu.roll(x, half, 1))
    return x * cos + partner * sin


def _inproj_kernel(x_ref, g_ref, mod_ref, w_ref, wvt_ref, cc_ref, cs_ref,
                   cos_ref, sin_ref, gq_ref, gkv_ref, wq_ref, wk_ref, wv_ref,
                   zc_ref, zs_ref, mqk_ref, vt_ref, mo_ref, gr_ref, qa_ref, ka_ref, va_ref, *, with_q):
    x = x_ref[...]
    mod = mod_ref[0]
    h = _rms(x, g_ref[0]) * (1.0 + mod[1:2]) + mod[0:1]
    hb = h.astype(BF16)

    def proj(off, width):
        return _dot(hb, w_ref[0, :, off:off + width])

    ckv_kr = proj(OFF_CKV, 2 * LANES)
    cq = proj(OFF_CQ, Q_LORA) if with_q else None
    pf = proj(OFF_PF, F_WIDTH).astype(BF16)

    mqk_ref[...] = proj(OFF_MQ, 2 * MP_WIDTH)

    cos = cos_ref[...]
    sin = sin_ref[...]
    lane = lax.broadcasted_iota(jnp.int32, cos.shape, 1)
    first_half = ((lane - A_NOPE) & (A_ROPE // 2 - 1)) < A_ROPE // 4
    kvn = _rms(ckv_kr[:, :KV_LORA], gkv_ref[0]).astype(BF16)
    k_rope = _rope(ckv_kr[:, KV_LORA:], cos, sin, first_half)
    k_nope = _dot(kvn, wk_ref[0])
    va = _dot(kvn, wv_ref[0])
    head_lane = lax.broadcasted_iota(jnp.int32, va.shape, 1) & (HEAD_PAD - 1)
    va_ref[...] = jnp.where(head_lane == A_V, 1.0, va).astype(BF16)
    for hd in range(A_HEADS):
        lanes = slice(hd * HEAD_PAD, (hd + 1) * HEAD_PAD)
        ka_ref[:, lanes] = (k_nope[:, lanes] + k_rope).astype(BF16)
    if with_q:
        qn = _rms(cq, gq_ref[0]).astype(BF16)
        q_all = _dot(qn, wq_ref[0])
        q_raw = [q_all[:, hd * HEAD_PAD:(hd + 1) * HEAD_PAD] for hd in range(A_HEADS)]
    zc_ref[...] = _dot(pf, cc_ref[...]).astype(BF16)
    zs_ref[...] = _dot(pf, cs_ref[...]).astype(BF16)

    vg = _dot_nt(wvt_ref[0], hb)
    vt = vg[:MP_WIDTH].astype(BF16)
    for j in range(vt_ref.shape[0]):
        vt_ref[j] = vt[:, j * MCH:(j + 1) * MCH]
        gr_ref[j] = vg[MP_WIDTH:, j * MCH:(j + 1) * MCH]
    mo_ref[...] = proj(OFF_MO, MP_WIDTH)

    if with_q:
        scale = (A_NOPE + A_ROPE) ** -0.5 * LOG2_E
        for hd in range(A_HEADS):
            q = _rope(q_raw[hd], cos, sin, first_half)
            qa_ref[:, hd * HEAD_PAD:(hd + 1) * HEAD_PAD] = (q * scale).astype(BF16)
    else:
        qa_ref[...] = jnp.zeros_like(qa_ref)


def _mod_map(mod_row0, per_batch_mod, tiles_per_seq):
    if per_batch_mod:
        return lambda i: (mod_row0 + i // tiles_per_seq, 0, 0)
    return lambda i: (mod_row0, 0, 0)


def _inproj(x2d, seq, layer, wts, mod, mod_row0, per_batch_mod, dft_cc, dft_cs, cos, sin, tm, with_q):
    t, d = x2d.shape
    nb = t // seq
    tiles_per_seq = seq // tm
    tok = lambda w: pl.BlockSpec((tm, w), lambda i: (i, 0))
    z_spec = pl.BlockSpec((tm, F_WIDTH), lambda i: (i % tiles_per_seq, i // tiles_per_seq))
    pos = pl.BlockSpec((tm, LANES), lambda i: (i % tiles_per_seq, 0))
    heads_bf16 = jax.ShapeDtypeStruct((t, AP_WIDTH), BF16)
    shapes = [
        jax.ShapeDtypeStruct((seq, nb * F_WIDTH), BF16),
        jax.ShapeDtypeStruct((seq, nb * F_WIDTH), BF16),
        jax.ShapeDtypeStruct((t, 2 * MP_WIDTH), F32),
        jax.ShapeDtypeStruct((t // MCH, MP_WIDTH, MCH), BF16),
        jax.ShapeDtypeStruct((t, MP_WIDTH), F32),
        jax.ShapeDtypeStruct((t // MCH, 2 * N_STREAM, MCH), F32),
        heads_bf16, heads_bf16, heads_bf16,
    ]
    vt_spec = pl.BlockSpec((tm // MCH, MP_WIDTH, MCH), lambda i: (i, 0, 0))
    gr_spec = pl.BlockSpec((tm // MCH, 2 * N_STREAM, MCH), lambda i: (i, 0, 0))
    out_specs = [z_spec, z_spec, tok(2 * MP_WIDTH), vt_spec, tok(MP_WIDTH), gr_spec,
                 tok(AP_WIDTH), tok(AP_WIDTH), tok(AP_WIDTH)]
    lay = lambda name: _layer_spec(wts[name], layer)
    return pl.pallas_call(
        functools.partial(_inproj_kernel, with_q=with_q),
        grid=(t // tm,),
        in_specs=[
            tok(d),
            lay("g1"),
            pl.BlockSpec((1, 6, d), _mod_map(mod_row0, per_batch_mod, tiles_per_seq)),
            lay("w_in_p"), lay("w_vt"),
            _const_spec((F_WIDTH, F_WIDTH)),
            _const_spec((F_WIDTH, F_WIDTH)),
            pos, pos,
            lay("gq"), lay("gkv"), lay("wq"), lay("wk"), lay("wv"),
        ],
        out_specs=out_specs,
        out_shape=shapes,
        compiler_params=_cparams(("arbitrary",)),
        name="inproj",
    )(x2d, wts["g1"], mod, wts["w_in_p"], wts["w_vt"], dft_cc, dft_cs, cos, sin,
      wts["gq"], wts["gkv"], wts["wq"], wts["wk"], wts["wv"])


def _fourier_kernel(c_ref, s_ref, zc_ref, zs_ref, o_ref):
    y = _dot(c_ref[...], zc_ref[...]) - _dot(s_ref[...], zs_ref[...])
    for slab in range(o_ref.shape[0]):
        o_ref[slab] = y[:, slab * LANES:(slab + 1) * LANES]


def _fourier(dft_c, dft_s, zc, zs):
    n, cols = zc.shape
    tr = min(n, 512)
    tc = min(cols, 512)
    return pl.pallas_call(
        _fourier_kernel,
        grid=(n // tr, cols // tc),
        in_specs=[
            pl.BlockSpec((tr, n), lambda i, j: (i, 0)),
            pl.BlockSpec((tr, n), lambda i, j: (i, 0)),
            pl.BlockSpec((n, tc), lambda i, j: (0, j)),
            pl.BlockSpec((n, tc), lambda i, j: (0, j)),
        ],
        out_specs=pl.BlockSpec((tc // LANES, tr, LANES), lambda i, j: (j, i, 0)),
        out_shape=jax.ShapeDtypeStruct((cols // LANES, n, LANES), F32),
        compiler_params=_cparams(("arbitrary", "arbitrary")),
        name="fourier",
    )(dft_c, dft_s, zc, zs)


FFT_COLS = 256


def _fourier4_kernel(tab_ref, twc_ref, tws_ref, zc_ref, zs_ref, o_ref):
    m = zc_ref.shape[0] // 4
    reps = zc_ref.shape[1] // LANES
    c0, c1, c2, c3 = (zc_ref[j * m:(j + 1) * m, :].astype(F32) for j in range(4))
    s0, s1, s2, s3 = (zs_ref[j * m:(j + 1) * m, :].astype(F32) for j in range(4))
    ce, co, cd, cu = c0 + c2, c1 + c3, c0 - c2, c1 - c3
    se, so, sd, su = s0 + s2, s1 + s3, s0 - s2, s1 - s3

    def emit(br, bi, k):
        if k:
            cos = jnp.concatenate([twc_ref[k - 1]] * reps, axis=1)
            sin = jnp.concatenate([tws_ref[k - 1]] * reps, axis=1)
            br, bi = br * cos + bi * sin, bi * cos - br * sin
        stacked = jnp.concatenate([br.astype(BF16), bi.astype(BF16)], axis=0)
        y = _dot(tab_ref[...], stacked)
        for slab in range(reps):
            o_ref[slab, pl.ds(k, m, stride=4), :] = y[:, slab * LANES:(slab + 1) * LANES]

    emit(ce + co, -(se + so), 0)
    emit(cd - su, -sd - cu, 1)
    emit(ce - co, so - se, 2)
    emit(cd + su, cu - sd, 3)


def _fourier4(tables, zc, zs):
    tab, twc, tws = tables
    n, cols = zc.shape
    m = n // 4
    tc = min(cols, FFT_COLS)
    return pl.pallas_call(
        _fourier4_kernel,
        grid=(cols // tc,),
        in_specs=[
            _const_spec((m, 2 * m)),
            _const_spec((3, m, LANES)),
            _const_spec((3, m, LANES)),
            pl.BlockSpec((n, tc), lambda j: (0, j)),
            pl.BlockSpec((n, tc), lambda j: (0, j)),
        ],
        out_specs=pl.BlockSpec((tc // LANES, n, LANES), lambda j: (j, 0, 0)),
        out_shape=jax.ShapeDtypeStruct((cols // LANES, n, LANES), F32),
        compiler_params=_cparams(("arbitrary",)),
        name="fourier4",
    )(tab, twc, tws, zc, zs)


def _fourier4_tables(n):
    m = n // 4
    idx = (np.arange(m, dtype=np.int64)[:, None] * np.arange(m, dtype=np.int64)[None, :]) % m
    ang = 2.0 * np.pi * idx.astype(np.float64) / m
    tab = np.concatenate([np.cos(ang), np.sin(ang)], axis=1) / np.sqrt(n)
    theta = 2.0 * np.pi * np.arange(m, dtype=np.float64)[None, :] * np.arange(1, 4, dtype=np.float64)[:, None] / n
    bcast = lambda t: jnp.asarray(np.repeat(t[:, :, None], LANES, axis=2), dtype=F32)
    return jnp.asarray(tab, dtype=F32).astype(BF16), bcast(np.cos(theta)), bcast(np.sin(theta))


CONV_ROWS = 256
CONV_HALO = 8


def _conv_kernel(u_ref, w_ref, q_ref, k_ref, pad_ref):
    n = u_ref.shape[1]
    width = u_ref.shape[2]
    zeros = jnp.zeros((CONV_HALO, width), F32)
    pad_ref[0:CONV_HALO, :] = zeros
    pad_ref[CONV_HALO + n:2 * CONV_HALO + n, :] = zeros
    pad_ref[CONV_HALO:CONV_HALO + n, :] = u_ref[0]
    w = w_ref[0]
    rows = min(CONV_ROWS, n)
    span = rows + 2 * CONV_HALO
    for r in range(n // rows):
        block = pad_ref[r * rows:r * rows + span, :]
        acc = None
        for j in range(K_CONV):
            shift = (K_CONV // 2 - j) % span
            tap = block if shift == 0 else pltpu.roll(block, shift, 0)
            term = tap[CONV_HALO:CONV_HALO + rows, :] * w[j:j + 1, :]
            acc = term if acc is None else acc + term
        act = acc * jax.nn.sigmoid(acc)
        q_ref[0, r * rows:(r + 1) * rows, :] = (act[:, :MP_WIDTH] * (M_HEAD_DIM ** -0.5)).astype(BF16)
        k_ref[0, r * rows:(r + 1) * rows, :] = act[:, MP_WIDTH:].astype(BF16)


def _conv_silu(mqk, layer, wts):
    nb, n, width = mqk.shape
    return pl.pallas_call(
        _conv_kernel,
        grid=(nb,),
        in_specs=[
            pl.BlockSpec((1, n, width), lambda b: (b, 0, 0)),
            _layer_spec(wts["conv_p"], layer),
        ],
        out_specs=[
            pl.BlockSpec((1, n, MP_WIDTH), lambda b: (b, 0, 0)),
            pl.BlockSpec((1, n, MP_WIDTH), lambda b: (b, 0, 0)),
        ],
        out_shape=[
            jax.ShapeDtypeStruct((nb, n, MP_WIDTH), BF16),
            jax.ShapeDtypeStruct((nb, n, MP_WIDTH), BF16),
        ],
        scratch_shapes=[pltpu.VMEM((n + 2 * CONV_HALO, width), F32)],
        compiler_params=_cparams(("arbitrary",)),
        name="conv_silu",
    )(mqk, wts["conv_p"])


def _log_sigmoid(x):
    return jnp.minimum(x, 0.0) - jnp.log(1.0 + jnp.exp(-jnp.abs(x)))


def _exact_dot_01(a, tri_bf16, a_on_left):
    out = None
    for term in _split3(a):
        d = _dot(term, tri_bf16) if a_on_left else _dot(tri_bf16, term)
        out = d if out is None else out + d
    return out


ONE_ROW = M_HEAD_DIM
(F_R, F_B, F_CM, F_TOT, F_CML, F_MP, F_A, F_WI, F_ELD, F_DEC, F_WK, F_HI, F_MID, F_LO) = range(14)
N_FIELDS = 14
STATE_GROUP = 4
OUT_GROUP = 8
SEL_ROWS = 32


def _mlstm_kernel(ql_ref, kl_ref, vtl_ref, grl_ref,
                  qc_ref, kc_ref, vtc_ref, grc_ref,
                  bir_ref, bfr_ref, sel_ref,
                  hl_ref, hc_ref,
                  ct_st, st_sc, rows_sc):
    L = MCH
    nh = M_HEADS
    ncc = qc_ref.shape[1] // L
    ncl = ql_ref.shape[1] // L

    d0 = lax.broadcasted_iota(jnp.int32, (L, L), 0)
    d1 = lax.broadcasted_iota(jnp.int32, (L, L), 1)
    le = d0 <= d1
    ge = d0 >= d1
    tri_le = le.astype(BF16)
    fwd_rows = lax.broadcasted_iota(jnp.int32, (N_STREAM, L), 0) < nh
    feat = lax.broadcasted_iota(jnp.int32, (HEAD_PAD, L), 0)
    one_row = feat == ONE_ROW
    keep_rows = feat < M_HEAD_DIM

    ct_st[...] = jnp.zeros_like(ct_st)

    def field(f, slot0, nc):
        return rows_sc[f, slot0:slot0 + nc].reshape(nc * N_STREAM, L)

    def set_field(f, slot0, nc, val):
        rows_sc[f, slot0:slot0 + nc] = val.reshape(nc, N_STREAM, L)

    def gate_pass(gr_ref, nc, slot0):
        n8 = nc * N_STREAM
        fwd = (lax.broadcasted_iota(jnp.int32, (n8, L), 0) & (N_STREAM - 1)) < nh
        lane = lax.broadcasted_iota(jnp.int32, (n8, L), 1)
        gi = gr_ref[:, 0:N_STREAM, :].reshape(n8, L) + bir_ref[0, 0:n8, :]
        f = _log_sigmoid(gr_ref[:, N_STREAM:2 * N_STREAM, :].reshape(n8, L) + bfr_ref[0, 0:n8, :])
        pre = _exact_dot_01(f, tri_le, a_on_left=True)
        total = jnp.sum(f, axis=1, keepdims=True)
        b = jnp.where(fwd, pre, total - pre + f)
        r = gi - b
        cm = r
        sh = 1
        while sh < L:
            from_left = jnp.where(lane >= sh, pltpu.roll(cm, sh, 1), -jnp.inf)
            from_right = jnp.where(lane < L - sh, pltpu.roll(cm, L - sh, 1), -jnp.inf)
            cm = jnp.maximum(cm, jnp.where(fwd, from_left, from_right))
            sh *= 2
        set_field(F_R, slot0, nc, r)
        set_field(F_B, slot0, nc, b)
        set_field(F_CM, slot0, nc, cm)
        set_field(F_TOT, slot0, nc, jnp.broadcast_to(total, (n8, L)))
        set_field(F_CML, slot0, nc, jnp.broadcast_to(jnp.max(r, axis=1, keepdims=True), (n8, L)))

    def m_scan(nc, slot0, m0):
        def step(j, m):
            sf = slot0 + j
            sb = slot0 + nc - 1 - j
            tot = jnp.where(fwd_rows, rows_sc[F_TOT, sf], rows_sc[F_TOT, sb])
            cml = jnp.where(fwd_rows, rows_sc[F_CML, sf], rows_sc[F_CML, sb])
            rows_sc[F_MP, sf, 0:nh, :] = m[0:nh]
            rows_sc[F_MP, sb, nh:N_STREAM, :] = m[nh:N_STREAM]
            return tot + jnp.maximum(m, cml)

        return lax.fori_loop(0, nc, step, m0)

    def weight_pass(nc, slot0):
        mp, cm, b, r = (field(f, slot0, nc) for f in (F_MP, F_CM, F_B, F_R))
        big = jnp.maximum(mp, field(F_CML, slot0, nc))
        a = -jnp.maximum(mp, cm)
        set_field(F_A, slot0, nc, a)
        set_field(F_WI, slot0, nc, jnp.exp(mp + a))
        set_field(F_ELD, slot0, nc, jnp.exp(a - b))
        set_field(F_DEC, slot0, nc, jnp.exp(mp - big))
        set_field(F_WK, slot0, nc, jnp.exp(r - big))
        for f, term in zip((F_HI, F_MID, F_LO), _split3(r)):
            set_field(f, slot0, nc, term.astype(F32))

    def value_slab(vt_ref, c, hd):
        vt = vt_ref[c, hd * HEAD_PAD:(hd + 1) * HEAD_PAD, :].astype(F32)
        return jnp.where(one_row, 1.0, vt)

    def state_pass(k_ref, vt_ref, nc, slot0):
        group = min(STATE_GROUP, nc)

        def updates_of(j):
            uts = []
            for sidx in range(N_STREAM):
                hd = sidx % nh
                c = j if sidx < nh else nc - 1 - j
                wk = rows_sc[F_WK, slot0 + c, sidx:sidx + 1, :]
                kk = k_ref[0, pl.ds(pl.multiple_of(c * L, L), L), hd * HEAD_PAD:(hd + 1) * HEAD_PAD]
                uts.append(_dot((value_slab(vt_ref, c, hd) * wk).astype(BF16), kk))
            return uts

        def step(g, carry):
            all_uts = [updates_of(g * group + u) for u in range(group)]
            for u in range(group):
                j = g * group + u
                for sidx in range(N_STREAM):
                    slot = slot0 + (j if sidx < nh else nc - 1 - j)
                    prev = ct_st[sidx]
                    st_sc[sidx, slot] = prev.astype(BF16)
                    ct_st[sidx] = rows_sc[F_DEC, slot, sidx:sidx + 1, :] * prev + all_uts[u][sidx]
            return carry

        lax.fori_loop(0, nc // group, step, 0)

    def output_pass(q_ref, k_ref, vt_ref, out_ref, nc, slot0):
        head_lanes = [slice(hd * HEAD_PAD, (hd + 1) * HEAD_PAD) for hd in range(nh)]

        def independent_matmuls(c):
            rows = pl.ds(pl.multiple_of(c * L, L), L)
            slot = slot0 + c
            r3 = jnp.concatenate([rows_sc[F_HI, slot], rows_sc[F_MID, slot], rows_sc[F_LO, slot],
                                  jnp.zeros((N_STREAM, L), F32)], axis=0).astype(BF16)
            qs = [q_ref[0, rows, lanes] for lanes in head_lanes]
            s_ts = [_dot_nt(k_ref[0, rows, lanes], q) for lanes, q in zip(head_lanes, qs)]
            inters = [_dot_nt(st_sc[sidx, slot], qs[sidx % nh]) for sidx in range(N_STREAM)]
            r_all = _dot_tn(r3, sel_ref[...])
            r_ts = [r_all[:, sidx * L:(sidx + 1) * L] for sidx in range(N_STREAM)]
            return s_ts, inters, r_ts

        def finish(c, s_ts, inters, r_ts):
            rows = pl.ds(pl.multiple_of(c * L, L), L)
            slot = slot0 + c
            a_rows = rows_sc[F_A, slot]
            wi_rows = rows_sc[F_WI, slot]
            eld_rows = rows_sc[F_ELD, slot]
            p_ts = []
            for sidx in range(N_STREAM):
                one = slice(sidx, sidx + 1)
                dm = jnp.where(le if sidx < nh else ge, r_ts[sidx] + a_rows[one, :], -jnp.inf)
                p_ts.append((s_ts[sidx % nh] * jnp.exp(dm)).astype(BF16))
            z_pairs = [_dot(value_slab(vt_ref, c, hd).astype(BF16),
                            jnp.concatenate([p_ts[hd], p_ts[nh + hd]], axis=1)) for hd in range(nh)]
            for hd in range(nh):
                hsum = None
                for dr, sidx in enumerate((hd, nh + hd)):
                    one = slice(sidx, sidx + 1)
                    z_t = z_pairs[hd][:, dr * L:(dr + 1) * L] + inters[sidx] * wi_rows[one, :]
                    den = z_t[ONE_ROW:ONE_ROW + 1, :]
                    h_t = z_t * (1.0 / jnp.maximum(jnp.abs(den), eld_rows[one, :]))
                    hsum = h_t if hsum is None else hsum + h_t
                out_ref[0, rows, head_lanes[hd]] = jnp.where(keep_rows, hsum, 0.0).T

        group = min(OUT_GROUP, nc)

        def step(g, carry):
            ahead = independent_matmuls(g * group)
            for u in range(group):
                cur = ahead
                if u + 1 < group:
                    ahead = independent_matmuls(g * group + u + 1)
                finish(g * group + u, *cur)
            return carry

        lax.fori_loop(0, nc // group, step, 0)

    gate_pass(grc_ref, ncc, 0)
    gate_pass(grl_ref, ncl, ncc)
    m1 = m_scan(ncc, 0, jnp.zeros((N_STREAM, L), F32))
    m_scan(ncl, ncc, m1)
    weight_pass(ncc, 0)
    weight_pass(ncl, ncc)
    state_pass(kc_ref, vtc_ref, ncc, 0)
    state_pass(kl_ref, vtl_ref, ncl, ncc)
    output_pass(qc_ref, kc_ref, vtc_ref, hc_ref, ncc, 0)
    output_pass(ql_ref, kl_ref, vtl_ref, hl_ref, ncl, ncc)


def _mlstm(lat, ctx, layer, wts, sel):
    nb, n, _ = lat[0].shape
    nctx = ctx[0].shape[1]
    assert MCH == LANES == HEAD_PAD
    nct = (n + nctx) // MCH
    assert wts["bir"].shape[1] >= max(n, nctx) // MCH * N_STREAM

    def specs(rows):
        nc = rows // MCH
        seq = lambda w: pl.BlockSpec((1, rows, w), lambda b: (b, 0, 0))
        chunked = lambda rows_: pl.BlockSpec((nc, rows_, MCH), lambda b: (b, 0, 0))
        return [seq(MP_WIDTH), seq(MP_WIDTH), chunked(MP_WIDTH), chunked(2 * N_STREAM)]

    out_spec = lambda rows: pl.BlockSpec((1, rows, MP_WIDTH), lambda b: (b, 0, 0))
    return pl.pallas_call(
        _mlstm_kernel,
        grid=(nb,),
        in_specs=specs(n) + specs(nctx) + [
            _layer_spec(wts["bir"], layer), _layer_spec(wts["bfr"], layer),
            _const_spec((SEL_ROWS, N_STREAM * MCH)),
        ],
        out_specs=[out_spec(n), out_spec(nctx)],
        out_shape=[
            jax.ShapeDtypeStruct((nb, n, MP_WIDTH), F32),
            jax.ShapeDtypeStruct((nb, nctx, MP_WIDTH), F32),
        ],
        scratch_shapes=[
            pltpu.VMEM((N_STREAM, HEAD_PAD, HEAD_PAD), F32),
            pltpu.VMEM((N_STREAM, nct, HEAD_PAD, HEAD_PAD), BF16),
            pltpu.VMEM((N_FIELDS, nct, N_STREAM, MCH), F32),
        ],
        compiler_params=_cparams(("arbitrary",)),
        name="mlstm",
    )(*lat, *ctx, wts["bir"], wts["bfr"], sel)


ATTN_SUB = 256
ATTN_HEADS_PER_STEP = 1


def _attn_kernel(*refs, n_sets):
    q_ref = refs[0]
    kv_refs = refs[1:1 + 2 * n_sets]
    o_ref = refs[1 + 2 * n_sets]
    sub = min(ATTN_SUB, q_ref.shape[1])
    n_sub = q_ref.shape[1] // sub
    items = [(hd, t) for hd in range(q_ref.shape[2] // HEAD_PAD) for t in range(n_sub)]

    def scores_of(item):
        hd, t = item
        lanes = slice(hd * HEAD_PAD, (hd + 1) * HEAD_PAD)
        q = q_ref[0, t * sub:(t + 1) * sub, lanes]
        return [_dot_nt(q, kv_refs[2 * i][0, :, lanes]) for i in range(n_sets)]

    nxt = scores_of(items[0])
    for idx, (hd, t) in enumerate(items):
        rows = slice(t * sub, (t + 1) * sub)
        lanes = slice(hd * HEAD_PAD, (hd + 1) * HEAD_PAD)
        scores = nxt
        if idx + 1 < len(items):
            nxt = scores_of(items[idx + 1])
        m = None
        for s in scores:
            sm = jnp.max(s, axis=-1, keepdims=True)
            m = sm if m is None else jnp.maximum(m, sm)
        acc = None
        for i, s in enumerate(scores):
            p = jnp.exp2(s - m)
            o = _dot(p.astype(BF16), kv_refs[2 * i + 1][0, :, lanes])
            acc = o if acc is None else acc + o
        o_ref[0, rows, lanes] = (acc / acc[:, A_V:A_V + 1]).astype(BF16)


def _attention(q, key_sets, tq, heads_per_step):
    nb, n, _ = q.shape
    n_sets = len(key_sets)
    width = heads_per_step * HEAD_PAD
    in_specs = [pl.BlockSpec((1, tq, width), lambda b, h, i: (b, i, h))]
    args = [q]
    for k, v in key_sets:
        nk = k.shape[1]
        spec = pl.BlockSpec((1, nk, width), lambda b, h, i: (b, 0, h))
        in_specs += [spec, spec]
        args += [k, v]
    return pl.pallas_call(
        functools.partial(_attn_kernel, n_sets=n_sets),
        grid=(nb, A_HEADS // heads_per_step, n // tq),
        in_specs=in_specs,
        out_specs=pl.BlockSpec((1, tq, width), lambda b, h, i: (b, i, h)),
        out_shape=jax.ShapeDtypeStruct((nb, n, AP_WIDTH), BF16),
        compiler_params=_cparams(("arbitrary", "arbitrary", "arbitrary")),
        name="attention",
    )(*args)


MLP_CHUNK = 1024


def _out_mlp_kernel(x_ref, yf_ref, hm_ref, mo_ref, ya_ref, mod_ref,
                    gm_ref, g2_ref, gfin_ref, wof_ref, wom_ref, woa_ref, wup_ref, wdn_ref,
                    o_ref, *, final_norm):
    mod = mod_ref[0]
    ga1, sh2, sc2, ga2 = mod[2:3], mod[3:4], mod[4:5], mod[5:6]
    yf = jnp.concatenate([yf_ref[slab].astype(BF16) for slab in range(yf_ref.shape[0])], axis=1)
    mix = _dot(ya_ref[...], woa_ref[0]) + _dot(yf, wof_ref[0])
    gm = gm_ref[0]
    yms = []
    for hd in range(M_HEADS):
        lanes = slice(hd * HEAD_PAD, (hd + 1) * HEAD_PAD)
        hh = hm_ref[:, lanes]
        ms = jnp.sum(hh * hh, axis=-1, keepdims=True) * (1.0 / M_HEAD_DIM)
        ym = hh * lax.rsqrt(ms + EPS) * gm[:, lanes] * jax.nn.sigmoid(mo_ref[:, lanes])
        yms.append(ym.astype(BF16))
    mix = mix + _dot(jnp.concatenate(yms, axis=1), wom_ref[0])
    x1 = x_ref[...] + ga1 * mix
    h2 = (_rms(x1, g2_ref[0]) * (1.0 + sc2) + sh2).astype(BF16)
    acc = None
    for c in range(wup_ref.shape[2] // MLP_CHUNK):
        cols = slice(c * MLP_CHUNK, (c + 1) * MLP_CHUNK)
        u = jnp.maximum(_dot(h2, wup_ref[0, :, cols]), 0.0)
        d = _dot((u * u).astype(BF16), wdn_ref[0, cols, :])
        acc = d if acc is None else acc + d
    x2 = x1 + ga2 * acc
    if final_norm:
        x2 = _rms(x2, gfin_ref[...])
    o_ref[...] = x2


def _out_mlp(x2d, seq, yf, hm, mo, ya, layer, wts, mod, mod_row0, per_batch_mod, gfin, tm, final_norm):
    t, d = x2d.shape
    tiles_per_seq = seq // tm
    tok = lambda w: pl.BlockSpec((tm, w), lambda i: (i, 0))
    yf_spec = pl.BlockSpec((F_WIDTH // LANES, tm, LANES), lambda i: (i // tiles_per_seq, i % tiles_per_seq, 0))
    names = ("gm", "g2", "wof", "wom", "woa", "wup", "wdn")
    lay = {name: _layer_spec(wts[name], layer) for name in names}
    return pl.pallas_call(
        functools.partial(_out_mlp_kernel, final_norm=final_norm),
        grid=(t // tm,),
        in_specs=[
            tok(d), yf_spec, tok(MP_WIDTH), tok(MP_WIDTH), tok(AP_WIDTH),
            pl.BlockSpec((1, 6, d), _mod_map(mod_row0, per_batch_mod, tiles_per_seq)),
            lay["gm"], lay["g2"], _const_spec((1, d)),
            lay["wof"], lay["wom"], lay["woa"], lay["wup"], lay["wdn"],
        ],
        out_specs=tok(d),
        out_shape=jax.ShapeDtypeStruct((t, d), F32),
        compiler_params=_cparams(("arbitrary",)),
        name="out_mlp",
    )(x2d, yf, hm, mo, ya, mod, wts["gm"], wts["g2"], gfin,
      wts["wof"], wts["wom"], wts["woa"], wts["wup"], wts["wdn"])


def _dft_tables(n):
    idx = (np.arange(n, dtype=np.int64)[:, None] * np.arange(n, dtype=np.int64)[None, :]) % n
    ang = 2.0 * np.pi * idx.astype(np.float64) / n
    scale = 1.0 / np.sqrt(n)
    return np.cos(ang) * scale, np.sin(ang) * scale


def _channel_dft():
    c, s = _dft_tables(F_GROUP_DIM)
    eye = np.eye(F_GROUPS)
    return (jnp.asarray(np.kron(eye, c), dtype=F32).astype(BF16),
            jnp.asarray(np.kron(eye, s), dtype=F32).astype(BF16))


def _position_dft(n):
    c, s = _dft_tables(n)
    return jnp.asarray(c, dtype=F32).astype(BF16), jnp.asarray(s, dtype=F32).astype(BF16)


def _rope_tables(n, rotate):
    cos = np.zeros((n, HEAD_PAD), np.float32)
    sin = np.zeros((n, HEAD_PAD), np.float32)
    cos[:, :A_NOPE + A_ROPE] = 1.0
    if rotate:
        nf = A_ROPE // 4
        t = np.arange(n)
        row = (t // GRID_W).astype(np.float32)
        col = (t % GRID_W).astype(np.float32)
        freqs = (np.float32(ROPE_THETA) ** (-np.arange(nf, dtype=np.float32) / np.float32(nf))).astype(np.float32)
        for seg, pos in enumerate((row, col)):
            ang = pos[:, None] * freqs[None, :]
            c, s = np.cos(ang), np.sin(ang)
            base = A_NOPE + seg * 2 * nf
            cos[:, base:base + nf] = c
            cos[:, base + nf:base + 2 * nf] = c
            sin[:, base:base + nf] = -s
            sin[:, base + nf:base + 2 * nf] = s
    return jnp.asarray(cos), jnp.asarray(sin)


def _pad_heads_cols(w, heads, width):
    lead = w.shape[:-1]
    w = w.reshape(lead + (heads, width))
    w = jnp.pad(w, [(0, 0)] * len(lead) + [(0, 0), (0, HEAD_PAD - width)])
    return w.reshape(lead + (heads * HEAD_PAD,))


def _pad_heads_rows(w, heads, width):
    depth, _, n = w.shape
    w = jnp.pad(w.reshape(depth, heads, width, n), [(0, 0), (0, 0), (0, HEAD_PAD - width), (0, 0)])
    return w.reshape(depth, heads * HEAD_PAD, n)


GATE_I_COLS = np.concatenate([np.arange(M_HEADS), 2 * M_HEADS + np.arange(M_HEADS)])
GATE_F_COLS = GATE_I_COLS + M_HEADS


def _prepare_weights(max_chunks, g_norm1, g_norm2, w_in, b_gates, conv_qk, g_mlstm, g_q_norm, g_kv_norm,
                     w_uq, w_ukv, w_out, w_up, w_down):
    offs = np.cumsum([0, F_WIDTH, M_WIDTH, M_WIDTH, M_WIDTH, M_WIDTH, 4 * M_HEADS, Q_LORA, KV_LORA, A_ROPE])
    part = lambda i: w_in[:, :, offs[i]:offs[i + 1]]
    heads = lambda w: _pad_heads_cols(w, M_HEADS, M_HEAD_DIM)
    w_in_p = jnp.concatenate([
        part(0), heads(part(1)), heads(part(2)), heads(part(4)), part(6), part(7),
        jnp.pad(part(8), [(0, 0), (0, 0), (A_NOPE, LANES - A_NOPE - A_ROPE)]),
    ], axis=2).astype(BF16)
    assert w_in_p.shape[2] == IN_PAD
    gates = part(5)
    w_vt = jnp.concatenate([heads(part(3)), gates[:, :, GATE_I_COLS], gates[:, :, GATE_F_COLS]],
                           axis=2).transpose(0, 2, 1).astype(BF16)

    conv_p = jnp.concatenate([heads(conv_qk[:, :, :M_WIDTH]), heads(conv_qk[:, :, M_WIDTH:])], axis=2)
    conv_p = jnp.pad(conv_p, [(0, 0), (0, 8 - K_CONV), (0, 0)])

    tile_rows = lambda b: jnp.tile(b[:, :, None], (1, max_chunks, 1))
    ukv = w_ukv.reshape(w_ukv.shape[0], KV_LORA, A_HEADS, A_NOPE + A_V)
    pad_kv = lambda w: jnp.pad(w, [(0, 0), (0, 0), (0, 0), (0, HEAD_PAD - w.shape[-1])]).reshape(
        w.shape[0], KV_LORA, AP_WIDTH).astype(BF16)
    vec = lambda g: g[:, None, :]
    return dict(
        g1=vec(g_norm1), g2=vec(g_norm2), gq=vec(g_q_norm), gkv=vec(g_kv_norm),
        w_in_p=w_in_p, w_vt=w_vt, conv_p=conv_p,
        bir=tile_rows(b_gates[:, GATE_I_COLS]), bfr=tile_rows(b_gates[:, GATE_F_COLS]),
        gm=_pad_heads_cols(g_mlstm, M_HEADS, M_HEAD_DIM)[:, None, :],
        wq=_pad_heads_cols(w_uq, A_HEADS, A_NOPE + A_ROPE).astype(BF16),
        wk=pad_kv(ukv[..., :A_NOPE]), wv=pad_kv(ukv[..., A_NOPE:]),
        wof=w_out[:, :F_WIDTH].astype(BF16),
        wom=_pad_heads_rows(w_out[:, F_WIDTH:F_WIDTH + M_WIDTH], M_HEADS, M_HEAD_DIM).astype(BF16),
        woa=_pad_heads_rows(w_out[:, F_WIDTH + M_WIDTH:], A_HEADS, A_V).astype(BF16),
        wup=w_up.astype(BF16), wdn=w_down.astype(BF16),
    )


def _stream_selectors():
    sel = np.zeros((SEL_ROWS, N_STREAM, MCH), np.float32)
    for s in range(N_STREAM):
        for part in range(3):
            sel[part * N_STREAM + s, s, :] = 1.0
    return jnp.asarray(sel.reshape(SEL_ROWS, N_STREAM * MCH), dtype=BF16)


def kernel(x, c, ctx, c_ctx, w_mod, b_mod, g_norm1, g_norm2, w_in, b_gates, conv_qk, g_mlstm,
           g_q_norm, g_kv_norm, w_uq, w_ukv, w_out, w_up, w_down, g_final):
    nb, seq, d = x.shape
    nctx = ctx.shape[1]
    depth = w_mod.shape[0]
    assert d == D_MODEL and seq % 256 == 0 and nctx % MCH == 0
    sel = _stream_selectors()

    tm = min(512, seq)
    tm_mlp = min(512, seq)
    tm_ctx = min(256, nctx)
    tq = min(8 * ATTN_SUB, seq)
    tq_ctx = min(ATTN_SUB, nctx)

    dft_cc, dft_cs = _channel_dft()
    fft_lat = _fourier4_tables(seq)
    dft_ctx = _position_dft(nctx)
    rope_lat = _rope_tables(seq, True)
    rope_ctx = _rope_tables(nctx, False)
    wts = _prepare_weights(max(seq, nctx) // MCH, g_norm1, g_norm2, w_in, b_gates, conv_qk, g_mlstm,
                           g_q_norm, g_kv_norm, w_uq, w_ukv, w_out, w_up, w_down)
    gfin = g_final.reshape(1, d)

    rows = ((nb + 1 + 7) // 8) * 8
    cc = jnp.concatenate([c, c_ctx[None, :], jnp.zeros((rows - nb - 1, d), F32)], axis=0)
    mod_all = _modulation(cc, w_mod, b_mod).reshape(depth * rows, 6, d)

    xl = x.reshape(nb * seq, d)
    xc = ctx.reshape(nb * nctx, d)

    for l in range(depth):
        last = l == depth - 1
        row_lat, row_ctx = l * rows, l * rows + nb

        zc, zs, mqk, vt, mo, gr, q_a, k_a, v_a = _inproj(
            xl, seq, l, wts, mod_all, row_lat, True, dft_cc, dft_cs, *rope_lat, tm, True)
        zc_c, zs_c, mqk_c, vt_c, mo_c, gr_c, q_ac, k_ac, v_ac = _inproj(
            xc, nctx, l, wts, mod_all, row_ctx, False, dft_cc, dft_cs, *rope_ctx, tm_ctx, not last)

        yf = _fourier4(fft_lat, zc, zs)

        def mlstm_inputs(mqk_s, vt_s, gr_s, n):
            q_s, k_s = _conv_silu(mqk_s.reshape(nb, n, 2 * MP_WIDTH), l, wts)
            return (q_s, k_s, vt_s, gr_s)

        hm, hm_c = _mlstm(mlstm_inputs(mqk, vt, gr, seq), mlstm_inputs(mqk_c, vt_c, gr_c, nctx), l, wts, sel)

        b3 = lambda a, n: a.reshape(nb, n, AP_WIDTH)
        keys_ctx = (b3(k_ac, nctx), b3(v_ac, nctx))
        ya = _attention(b3(q_a, seq), [(b3(k_a, seq), b3(v_a, seq)), keys_ctx], tq, ATTN_HEADS_PER_STEP)

        xl = _out_mlp(xl, seq, yf, hm.reshape(nb * seq, MP_WIDTH), mo, ya.reshape(nb * seq, AP_WIDTH),
                      l, wts, mod_all, row_lat, True, gfin, tm_mlp, last)

        if not last:
            yf_c = _fourier(*dft_ctx, zc_c, zs_c)
            ya_c = _attention(b3(q_ac, nctx), [keys_ctx], tq_ctx, A_HEADS)
            xc = _out_mlp(xc, nctx, yf_c, hm_c.reshape(nb * nctx, MP_WIDTH), mo_c,
                          ya_c.reshape(nb * nctx, AP_WIDTH), l, wts, mod_all, row_ctx, False, gfin, tm_ctx, False)

    return xl.reshape(nb, seq, d)
```

```python
import functools

import numpy as np
import jax
import jax.numpy as jnp
from jax import lax
from jax.experimental import pallas as pl
from jax.experimental.pallas import tpu as pltpu

D_MODEL = 1024
GRID_W = 64
EPS = 1e-6
F_GROUPS = 4
F_GROUP_DIM = D_MODEL // 16
F_WIDTH = F_GROUPS * F_GROUP_DIM
M_HEADS = 4
M_HEAD_DIM = 3 * D_MODEL // 32
M_WIDTH = M_HEADS * M_HEAD_DIM
K_CONV = 5
A_HEADS = 4
A_NOPE = 64
A_ROPE = 32
A_V = 3 * D_MODEL // 32
Q_LORA = D_MODEL // 4
KV_LORA = D_MODEL // 8
ROPE_THETA = 10000.0
MLP_HIDDEN = 4 * D_MODEL

LANES = 128
HEAD_PAD = 128
MP_WIDTH = M_HEADS * HEAD_PAD
AP_WIDTH = A_HEADS * HEAD_PAD
VMEM_LIMIT = 56 * 1024 * 1024
MCH = 128
N_STREAM = 2 * M_HEADS

OFF_PF = 0
OFF_MQ = OFF_PF + F_WIDTH
OFF_MO = OFF_MQ + 2 * M_WIDTH
OFF_CQ = OFF_MO + MP_WIDTH
OFF_CKV = OFF_CQ + Q_LORA
OFF_KR = OFF_CKV + KV_LORA
IN_PAD = OFF_KR + LANES

BF16 = jnp.bfloat16
F32 = jnp.float32
LOG2_E = 1.4426950408889634


def _cparams(sem):
    return pltpu.CompilerParams(dimension_semantics=sem, vmem_limit_bytes=VMEM_LIMIT)


def _const_spec(shape):
    nd = len(shape)
    return pl.BlockSpec(shape, lambda *_: (0,) * nd, pipeline_mode=pl.Buffered(1))


def _layer_spec(arr, layer):
    nd = arr.ndim
    return pl.BlockSpec((1,) + arr.shape[1:], lambda *_: (layer,) + (0,) * (nd - 1), pipeline_mode=pl.Buffered(1))


def _split3(a):
    hi = a.astype(BF16)
    r1 = a - hi.astype(F32)
    mid = r1.astype(BF16)
    lo = (r1 - mid.astype(F32)).astype(BF16)
    return hi, mid, lo


def _dot(a, b):
    return jnp.dot(a, b, preferred_element_type=F32)


def _dot_nt(a, b):
    return lax.dot_general(a, b, (((1,), (1,)), ((), ())), preferred_element_type=F32)


def _dot_tn(a, b):
    return lax.dot_general(a, b, (((0,), (0,)), ((), ())), preferred_element_type=F32)


def _rms(x, g):
    return x * lax.rsqrt(jnp.mean(x * x, axis=-1, keepdims=True) + EPS) * g


def _mod_kernel(c_ref, w_ref, b_ref, o_ref):
    c = c_ref[...]
    a = c * jax.nn.sigmoid(c)
    a_hi = a.astype(BF16)
    a_lo = (a - a_hi.astype(F32)).astype(BF16)
    w = w_ref[0]
    w_hi = w.astype(BF16)
    w_lo = (w - w_hi.astype(F32)).astype(BF16)
    acc = _dot(a_hi, w_hi) + _dot(a_hi, w_lo) + _dot(a_lo, w_hi)
    o_ref[0] = acc + b_ref[0]


def _modulation(cc, w_mod, b_mod):
    depth, d, n = w_mod.shape
    rows = cc.shape[0]
    tn = 1536
    return pl.pallas_call(
        _mod_kernel,
        grid=(depth, n // tn),
        in_specs=[
            pl.BlockSpec((rows, d), lambda l, j: (0, 0)),
            pl.BlockSpec((1, d, tn), lambda l, j: (l, 0, j)),
            pl.BlockSpec((1, 1, tn), lambda l, j: (l, 0, j)),
        ],
        out_specs=pl.BlockSpec((1, rows, tn), lambda l, j: (l, 0, j)),
        out_shape=jax.ShapeDtypeStruct((depth, rows, n), F32),
        compiler_params=_cparams(("arbitrary", "arbitrary")),
        name="modulation",
    )(cc, w_mod, b_mod.reshape(depth, 1, n))


def _rope(x, cos, sin, first_half):
    half = A_ROPE // 4
    partner = jnp.where(first_half, pltpu.roll(x, LANES - half, 1), pltpu.roll(x, half, 1))
    return x * cos + partner * sin


def _inproj_kernel(x_ref, g_ref, mod_ref, w_ref, wvt_ref, cc_ref, cs_ref,
                   cos_ref, sin_ref, gq_ref, gkv_ref, wq_ref, wk_ref, wv_ref,
                   zc_ref, zs_ref, mqk_ref, vt_ref, mo_ref, gr_ref, qa_ref, ka_ref, va_ref, *, with_q):
    x = x_ref[...]
    mod = mod_ref[0]
    h = _rms(x, g_ref[0]) * (1.0 + mod[1:2]) + mod[0:1]
    hb = h.astype(BF16)

    def proj(off, width):
        return _dot(hb, w_ref[0, :, off:off + width])

    ckv_kr = proj(OFF_CKV, 2 * LANES)
    cq = proj(OFF_CQ, Q_LORA) if with_q else None
    pf = proj(OFF_PF, F_WIDTH).astype(BF16)

    mqk_ref[...] = proj(OFF_MQ, 2 * M_WIDTH)

    cos = cos_ref[...]
    sin = sin_ref[...]
    lane = lax.broadcasted_iota(jnp.int32, cos.shape, 1)
    first_half = ((lane - A_NOPE) & (A_ROPE // 2 - 1)) < A_ROPE // 4
    kvn = _rms(ckv_kr[:, :KV_LORA], gkv_ref[0]).astype(BF16)
    k_rope = _rope(ckv_kr[:, KV_LORA:], cos, sin, first_half)
    k_nope = _dot(kvn, wk_ref[0])
    va = _dot(kvn, wv_ref[0])
    head_lane = lax.broadcasted_iota(jnp.int32, va.shape, 1) & (HEAD_PAD - 1)
    va_ref[...] = jnp.where(head_lane == A_V, 1.0, va).astype(BF16)
    for hd in range(A_HEADS):
        lanes = slice(hd * HEAD_PAD, (hd + 1) * HEAD_PAD)
        ka_ref[:, lanes] = (k_nope[:, lanes] + k_rope).astype(BF16)
    if with_q:
        qn = _rms(cq, gq_ref[0]).astype(BF16)
        q_all = _dot(qn, wq_ref[0])
        q_raw = [q_all[:, hd * HEAD_PAD:(hd + 1) * HEAD_PAD] for hd in range(A_HEADS)]
    zc_ref[...] = _dot(pf, cc_ref[...]).astype(BF16)
    zs_ref[...] = _dot(pf, cs_ref[...]).astype(BF16)

    vg = _dot_nt(wvt_ref[0], hb)
    vt = vg[:MP_WIDTH].astype(BF16)
    for j in range(vt_ref.shape[0]):
        vt_ref[j] = vt[:, j * MCH:(j + 1) * MCH]
        gr_ref[j] = vg[MP_WIDTH:, j * MCH:(j + 1) * MCH]
    mo_ref[...] = proj(OFF_MO, MP_WIDTH)

    if with_q:
        scale = (A_NOPE + A_ROPE) ** -0.5 * LOG2_E
        for hd in range(A_HEADS):
            q = _rope(q_raw[hd], cos, sin, first_half)
            qa_ref[:, hd * HEAD_PAD:(hd + 1) * HEAD_PAD] = (q * scale).astype(BF16)
    else:
        qa_ref[...] = jnp.zeros_like(qa_ref)


def _mod_map(mod_row0, per_batch_mod, tiles_per_seq):
    if per_batch_mod:
        return lambda i: (mod_row0 + i // tiles_per_seq, 0, 0)
    return lambda i: (mod_row0, 0, 0)


def _inproj(x2d, seq, layer, wts, mod, mod_row0, per_batch_mod, dft_cc, dft_cs, cos, sin, tm, with_q):
    t, d = x2d.shape
    nb = t // seq
    tiles_per_seq = seq // tm
    tok = lambda w: pl.BlockSpec((tm, w), lambda i: (i, 0))
    z_spec = pl.BlockSpec((tm, F_WIDTH), lambda i: (i % tiles_per_seq, i // tiles_per_seq))
    pos = pl.BlockSpec((tm, LANES), lambda i: (i % tiles_per_seq, 0))
    heads_bf16 = jax.ShapeDtypeStruct((t, AP_WIDTH), BF16)
    shapes = [
        jax.ShapeDtypeStruct((seq, nb * F_WIDTH), BF16),
        jax.ShapeDtypeStruct((seq, nb * F_WIDTH), BF16),
        jax.ShapeDtypeStruct((t, 2 * M_WIDTH), F32),
        jax.ShapeDtypeStruct((t // MCH, MP_WIDTH, MCH), BF16),
        jax.ShapeDtypeStruct((t, MP_WIDTH), F32),
        jax.ShapeDtypeStruct((t // MCH, 2 * N_STREAM, MCH), F32),
        heads_bf16, heads_bf16, heads_bf16,
    ]
    vt_spec = pl.BlockSpec((tm // MCH, MP_WIDTH, MCH), lambda i: (i, 0, 0))
    gr_spec = pl.BlockSpec((tm // MCH, 2 * N_STREAM, MCH), lambda i: (i, 0, 0))
    out_specs = [z_spec, z_spec, tok(2 * M_WIDTH), vt_spec, tok(MP_WIDTH), gr_spec,
                 tok(AP_WIDTH), tok(AP_WIDTH), tok(AP_WIDTH)]
    lay = lambda name: _layer_spec(wts[name], layer)
    return pl.pallas_call(
        functools.partial(_inproj_kernel, with_q=with_q),
        grid=(t // tm,),
        in_specs=[
            tok(d),
            lay("g1"),
            pl.BlockSpec((1, 6, d), _mod_map(mod_row0, per_batch_mod, tiles_per_seq)),
            lay("w_in_p"), lay("w_vt"),
            _const_spec((F_WIDTH, F_WIDTH)),
            _const_spec((F_WIDTH, F_WIDTH)),
            pos, pos,
            lay("gq"), lay("gkv"), lay("wq"), lay("wk"), lay("wv"),
        ],
        out_specs=out_specs,
        out_shape=shapes,
        compiler_params=_cparams(("arbitrary",)),
        name="inproj",
    )(x2d, wts["g1"], mod, wts["w_in_p"], wts["w_vt"], dft_cc, dft_cs, cos, sin,
      wts["gq"], wts["gkv"], wts["wq"], wts["wk"], wts["wv"])


def _fourier_kernel(c_ref, s_ref, zc_ref, zs_ref, o_ref):
    y = _dot(c_ref[...], zc_ref[...]) - _dot(s_ref[...], zs_ref[...])
    for slab in range(o_ref.shape[0]):
        o_ref[slab] = y[:, slab * LANES:(slab + 1) * LANES]


def _fourier(dft_c, dft_s, zc, zs):
    n, cols = zc.shape
    tr = min(n, 512)
    tc = min(cols, 512)
    return pl.pallas_call(
        _fourier_kernel,
        grid=(n // tr, cols // tc),
        in_specs=[
            pl.BlockSpec((tr, n), lambda i, j: (i, 0)),
            pl.BlockSpec((tr, n), lambda i, j: (i, 0)),
            pl.BlockSpec((n, tc), lambda i, j: (0, j)),
            pl.BlockSpec((n, tc), lambda i, j: (0, j)),
        ],
        out_specs=pl.BlockSpec((tc // LANES, tr, LANES), lambda i, j: (j, i, 0)),
        out_shape=jax.ShapeDtypeStruct((cols // LANES, n, LANES), F32),
        compiler_params=_cparams(("arbitrary", "arbitrary")),
        name="fourier",
    )(dft_c, dft_s, zc, zs)


FFT_COLS = 256


def _fourier4_kernel(tab_ref, twc_ref, tws_ref, zc_ref, zs_ref, o_ref):
    m = zc_ref.shape[0] // 4
    reps = zc_ref.shape[1] // LANES
    c0, c1, c2, c3 = (zc_ref[j * m:(j + 1) * m, :].astype(F32) for j in range(4))
    s0, s1, s2, s3 = (zs_ref[j * m:(j + 1) * m, :].astype(F32) for j in range(4))
    ce, co, cd, cu = c0 + c2, c1 + c3, c0 - c2, c1 - c3
    se, so, sd, su = s0 + s2, s1 + s3, s0 - s2, s1 - s3

    def emit(br, bi, k):
        if k:
            cos = jnp.concatenate([twc_ref[k - 1]] * reps, axis=1)
            sin = jnp.concatenate([tws_ref[k - 1]] * reps, axis=1)
            br, bi = br * cos + bi * sin, bi * cos - br * sin
        stacked = jnp.concatenate([br.astype(BF16), bi.astype(BF16)], axis=0)
        y = _dot(tab_ref[...], stacked)
        for slab in range(reps):
            o_ref[slab, pl.ds(k, m, stride=4), :] = y[:, slab * LANES:(slab + 1) * LANES]

    emit(ce + co, -(se + so), 0)
    emit(cd - su, -sd - cu, 1)
    emit(ce - co, so - se, 2)
    emit(cd + su, cu - sd, 3)


def _fourier4(tables, zc, zs):
    tab, twc, tws = tables
    n, cols = zc.shape
    m = n // 4
    tc = min(cols, FFT_COLS)
    return pl.pallas_call(
        _fourier4_kernel,
        grid=(cols // tc,),
        in_specs=[
            _const_spec((m, 2 * m)),
            _const_spec((3, m, LANES)),
            _const_spec((3, m, LANES)),
            pl.BlockSpec((n, tc), lambda j: (0, j)),
            pl.BlockSpec((n, tc), lambda j: (0, j)),
        ],
        out_specs=pl.BlockSpec((tc // LANES, n, LANES), lambda j: (j, 0, 0)),
        out_shape=jax.ShapeDtypeStruct((cols // LANES, n, LANES), F32),
        compiler_params=_cparams(("arbitrary",)),
        name="fourier4",
    )(tab, twc, tws, zc, zs)


def _fourier4_tables(n):
    m = n // 4
    idx = (np.arange(m, dtype=np.int64)[:, None] * np.arange(m, dtype=np.int64)[None, :]) % m
    ang = 2.0 * np.pi * idx.astype(np.float64) / m
    tab = np.concatenate([np.cos(ang), np.sin(ang)], axis=1) / np.sqrt(n)
    theta = 2.0 * np.pi * np.arange(m, dtype=np.float64)[None, :] * np.arange(1, 4, dtype=np.float64)[:, None] / n
    bcast = lambda t: jnp.asarray(np.repeat(t[:, :, None], LANES, axis=2), dtype=F32)
    return jnp.asarray(tab, dtype=F32).astype(BF16), bcast(np.cos(theta)), bcast(np.sin(theta))


CONV_ROWS = 256
CONV_HALO = 8


def _head_tile(x, start, lane):
    t, o = divmod(start, LANES)
    tile = lambda i: x[:, i * LANES:(i + 1) * LANES]
    if o == 0:
        return tile(t)
    shift = LANES - o
    out = pltpu.roll(tile(t), shift, 1)
    if shift < M_HEAD_DIM:
        out = jnp.where(lane < shift, out, pltpu.roll(tile(t + 1), shift, 1))
    return out


def _conv_kernel(u_ref, w_ref, q_ref, k_ref, pad_ref):
    n = u_ref.shape[1]
    width = u_ref.shape[2]
    zeros = jnp.zeros((CONV_HALO, width), F32)
    pad_ref[0:CONV_HALO, :] = zeros
    pad_ref[CONV_HALO + n:2 * CONV_HALO + n, :] = zeros
    pad_ref[CONV_HALO:CONV_HALO + n, :] = u_ref[0]
    w = w_ref[0]
    rows = min(CONV_ROWS, n)
    span = rows + 2 * CONV_HALO
    for r in range(n // rows):
        block = pad_ref[r * rows:r * rows + span, :]
        acc = None
        for j in range(K_CONV):
            shift = (K_CONV // 2 - j) % span
            tap = block if shift == 0 else pltpu.roll(block, shift, 0)
            term = tap[CONV_HALO:CONV_HALO + rows, :] * w[j:j + 1, :]
            acc = term if acc is None else acc + term
        act = acc * jax.nn.sigmoid(acc)
        lane = lax.broadcasted_iota(jnp.int32, (rows, LANES), 1)
        for hd in range(M_HEADS):
            lanes = slice(hd * HEAD_PAD, (hd + 1) * HEAD_PAD)
            q_tile = _head_tile(act, hd * M_HEAD_DIM, lane)
            k_tile = _head_tile(act, M_WIDTH + hd * M_HEAD_DIM, lane)
            q_ref[0, r * rows:(r + 1) * rows, lanes] = (q_tile * (M_HEAD_DIM ** -0.5)).astype(BF16)
            k_ref[0, r * rows:(r + 1) * rows, lanes] = jnp.where(lane < M_HEAD_DIM, k_tile, 0.0).astype(BF16)


def _conv_silu(mqk, layer, wts):
    nb, n, width = mqk.shape
    return pl.pallas_call(
        _conv_kernel,
        grid=(nb,),
        in_specs=[
            pl.BlockSpec((1, n, width), lambda b: (b, 0, 0)),
            _layer_spec(wts["conv_p"], layer),
        ],
        out_specs=[
            pl.BlockSpec((1, n, MP_WIDTH), lambda b: (b, 0, 0)),
            pl.BlockSpec((1, n, MP_WIDTH), lambda b: (b, 0, 0)),
        ],
        out_shape=[
            jax.ShapeDtypeStruct((nb, n, MP_WIDTH), BF16),
            jax.ShapeDtypeStruct((nb, n, MP_WIDTH), BF16),
        ],
        scratch_shapes=[pltpu.VMEM((n + 2 * CONV_HALO, width), F32)],
        compiler_params=_cparams(("arbitrary",)),
        name="conv_silu",
    )(mqk, wts["conv_p"])


def _log_sigmoid(x):
    return jnp.minimum(x, 0.0) - jnp.log(1.0 + jnp.exp(-jnp.abs(x)))


def _exact_dot_01(a, tri_bf16, a_on_left):
    out = None
    for term in _split3(a):
        d = _dot(term, tri_bf16) if a_on_left else _dot(tri_bf16, term)
        out = d if out is None else out + d
    return out


ONE_ROW = M_HEAD_DIM
(F_R, F_B, F_CM, F_TOT, F_CML, F_MP, F_A, F_WI, F_ELD, F_DEC, F_WK, F_HI, F_MID, F_LO) = range(14)
N_FIELDS = 14
STATE_GROUP = 4
OUT_GROUP = 8
SEL_ROWS = 32


def _mlstm_kernel(ql_ref, kl_ref, vtl_ref, grl_ref,
                  qc_ref, kc_ref, vtc_ref, grc_ref,
                  bir_ref, bfr_ref, sel_ref,
                  hl_ref, hc_ref,
                  ct_st, st_sc, rows_sc):
    L = MCH
    nh = M_HEADS
    ncc = qc_ref.shape[1] // L
    ncl = ql_ref.shape[1] // L

    d0 = lax.broadcasted_iota(jnp.int32, (L, L), 0)
    d1 = lax.broadcasted_iota(jnp.int32, (L, L), 1)
    le = d0 <= d1
    ge = d0 >= d1
    tri_le = le.astype(BF16)
    fwd_rows = lax.broadcasted_iota(jnp.int32, (N_STREAM, L), 0) < nh
    feat = lax.broadcasted_iota(jnp.int32, (HEAD_PAD, L), 0)
    one_row = feat == ONE_ROW
    keep_rows = feat < M_HEAD_DIM

    ct_st[...] = jnp.zeros_like(ct_st)

    def field(f, slot0, nc):
        return rows_sc[f, slot0:slot0 + nc].reshape(nc * N_STREAM, L)

    def set_field(f, slot0, nc, val):
        rows_sc[f, slot0:slot0 + nc] = val.reshape(nc, N_STREAM, L)

    def gate_pass(gr_ref, nc, slot0):
        n8 = nc * N_STREAM
        fwd = (lax.broadcasted_iota(jnp.int32, (n8, L), 0) & (N_STREAM - 1)) < nh
        lane = lax.broadcasted_iota(jnp.int32, (n8, L), 1)
        gi = gr_ref[:, 0:N_STREAM, :].reshape(n8, L) + bir_ref[0, 0:n8, :]
        f = _log_sigmoid(gr_ref[:, N_STREAM:2 * N_STREAM, :].reshape(n8, L) + bfr_ref[0, 0:n8, :])
        pre = _exact_dot_01(f, tri_le, a_on_left=True)
        total = jnp.sum(f, axis=1, keepdims=True)
        b = jnp.where(fwd, pre, total - pre + f)
        r = gi - b
        cm = r
        sh = 1
        while sh < L:
            from_left = jnp.where(lane >= sh, pltpu.roll(cm, sh, 1), -jnp.inf)
            from_right = jnp.where(lane < L - sh, pltpu.roll(cm, L - sh, 1), -jnp.inf)
            cm = jnp.maximum(cm, jnp.where(fwd, from_left, from_right))
            sh *= 2
        set_field(F_R, slot0, nc, r)
        set_field(F_B, slot0, nc, b)
        set_field(F_CM, slot0, nc, cm)
        set_field(F_TOT, slot0, nc, jnp.broadcast_to(total, (n8, L)))
        set_field(F_CML, slot0, nc, jnp.broadcast_to(jnp.max(r, axis=1, keepdims=True), (n8, L)))

    def m_scan(nc, slot0, m0):
        def step(j, m):
            sf = slot0 + j
            sb = slot0 + nc - 1 - j
            tot = jnp.where(fwd_rows, rows_sc[F_TOT, sf], rows_sc[F_TOT, sb])
            cml = jnp.where(fwd_rows, rows_sc[F_CML, sf], rows_sc[F_CML, sb])
            rows_sc[F_MP, sf, 0:nh, :] = m[0:nh]
            rows_sc[F_MP, sb, nh:N_STREAM, :] = m[nh:N_STREAM]
            return tot + jnp.maximum(m, cml)

        return lax.fori_loop(0, nc, step, m0)

    def weight_pass(nc, slot0):
        mp, cm, b, r = (field(f, slot0, nc) for f in (F_MP, F_CM, F_B, F_R))
        big = jnp.maximum(mp, field(F_CML, slot0, nc))
        a = -jnp.maximum(mp, cm)
        set_field(F_A, slot0, nc, a)
        set_field(F_WI, slot0, nc, jnp.exp(mp + a))
        set_field(F_ELD, slot0, nc, jnp.exp(a - b))
        set_field(F_DEC, slot0, nc, jnp.exp(mp - big))
        set_field(F_WK, slot0, nc, jnp.exp(r - big))
        for f, term in zip((F_HI, F_MID, F_LO), _split3(r)):
            set_field(f, slot0, nc, term.astype(F32))

    def value_slab(vt_ref, c, hd):
        vt = vt_ref[c, hd * HEAD_PAD:(hd + 1) * HEAD_PAD, :].astype(F32)
        return jnp.where(one_row, 1.0, vt)

    def state_pass(k_ref, vt_ref, nc, slot0):
        group = min(STATE_GROUP, nc)

        def updates_of(j):
            uts = []
            for sidx in range(N_STREAM):
                hd = sidx % nh
                c = j if sidx < nh else nc - 1 - j
                wk = rows_sc[F_WK, slot0 + c, sidx:sidx + 1, :]
                kk = k_ref[0, pl.ds(pl.multiple_of(c * L, L), L), hd * HEAD_PAD:(hd + 1) * HEAD_PAD]
                uts.append(_dot((value_slab(vt_ref, c, hd) * wk).astype(BF16), kk))
            return uts

        def step(g, carry):
            all_uts = [updates_of(g * group + u) for u in range(group)]
            for u in range(group):
                j = g * group + u
                for sidx in range(N_STREAM):
                    slot = slot0 + (j if sidx < nh else nc - 1 - j)
                    prev = ct_st[sidx]
                    st_sc[sidx, slot] = prev.astype(BF16)
                    ct_st[sidx] = rows_sc[F_DEC, slot, sidx:sidx + 1, :] * prev + all_uts[u][sidx]
            return carry

        lax.fori_loop(0, nc // group, step, 0)

    def output_pass(q_ref, k_ref, vt_ref, out_ref, nc, slot0):
        head_lanes = [slice(hd * HEAD_PAD, (hd + 1) * HEAD_PAD) for hd in range(nh)]

        def independent_matmuls(c):
            rows = pl.ds(pl.multiple_of(c * L, L), L)
            slot = slot0 + c
            r3 = jnp.concatenate([rows_sc[F_HI, slot], rows_sc[F_MID, slot], rows_sc[F_LO, slot],
                                  jnp.zeros((N_STREAM, L), F32)], axis=0).astype(BF16)
            qs = [q_ref[0, rows, lanes] for lanes in head_lanes]
            s_ts = [_dot_nt(k_ref[0, rows, lanes], q) for lanes, q in zip(head_lanes, qs)]
            inters = [_dot_nt(st_sc[sidx, slot], qs[sidx % nh]) for sidx in range(N_STREAM)]
            r_all = _dot_tn(r3, sel_ref[...])
            r_ts = [r_all[:, sidx * L:(sidx + 1) * L] for sidx in range(N_STREAM)]
            return s_ts, inters, r_ts

        def finish(c, s_ts, inters, r_ts):
            rows = pl.ds(pl.multiple_of(c * L, L), L)
            slot = slot0 + c
            a_rows = rows_sc[F_A, slot]
            wi_rows = rows_sc[F_WI, slot]
            eld_rows = rows_sc[F_ELD, slot]
            p_ts = []
            for sidx in range(N_STREAM):
                one = slice(sidx, sidx + 1)
                dm = jnp.where(le if sidx < nh else ge, r_ts[sidx] + a_rows[one, :], -jnp.inf)
                p_ts.append((s_ts[sidx % nh] * jnp.exp(dm)).astype(BF16))
            z_pairs = [_dot(value_slab(vt_ref, c, hd).astype(BF16),
                            jnp.concatenate([p_ts[hd], p_ts[nh + hd]], axis=1)) for hd in range(nh)]
            for hd in range(nh):
                hsum = None
                for dr, sidx in enumerate((hd, nh + hd)):
                    one = slice(sidx, sidx + 1)
                    z_t = z_pairs[hd][:, dr * L:(dr + 1) * L] + inters[sidx] * wi_rows[one, :]
                    den = z_t[ONE_ROW:ONE_ROW + 1, :]
                    h_t = z_t * (1.0 / jnp.maximum(jnp.abs(den), eld_rows[one, :]))
                    hsum = h_t if hsum is None else hsum + h_t
                out_ref[0, rows, head_lanes[hd]] = jnp.where(keep_rows, hsum, 0.0).T

        group = min(OUT_GROUP, nc)

        def step(g, carry):
            ahead = independent_matmuls(g * group)
            for u in range(group):
                cur = ahead
                if u + 1 < group:
                    ahead = independent_matmuls(g * group + u + 1)
                finish(g * group + u, *cur)
            return carry

        lax.fori_loop(0, nc // group, step, 0)

    gate_pass(grc_ref, ncc, 0)
    gate_pass(grl_ref, ncl, ncc)
    m1 = m_scan(ncc, 0, jnp.zeros((N_STREAM, L), F32))
    m_scan(ncl, ncc, m1)
    weight_pass(ncc, 0)
    weight_pass(ncl, ncc)
    state_pass(kc_ref, vtc_ref, ncc, 0)
    state_pass(kl_ref, vtl_ref, ncl, ncc)
    output_pass(qc_ref, kc_ref, vtc_ref, hc_ref, ncc, 0)
    output_pass(ql_ref, kl_ref, vtl_ref, hl_ref, ncl, ncc)


def _mlstm(lat, ctx, layer, wts, sel):
    nb, n, _ = lat[0].shape
    nctx = ctx[0].shape[1]
    assert MCH == LANES == HEAD_PAD
    nct = (n + nctx) // MCH
    assert wts["bir"].shape[1] >= max(n, nctx) // MCH * N_STREAM

    def specs(rows):
        nc = rows // MCH
        seq = lambda w: pl.BlockSpec((1, rows, w), lambda b: (b, 0, 0))
        chunked = lambda rows_: pl.BlockSpec((nc, rows_, MCH), lambda b: (b, 0, 0))
        return [seq(MP_WIDTH), seq(MP_WIDTH), chunked(MP_WIDTH), chunked(2 * N_STREAM)]

    out_spec = lambda rows: pl.BlockSpec((1, rows, MP_WIDTH), lambda b: (b, 0, 0))
    return pl.pallas_call(
        _mlstm_kernel,
        grid=(nb,),
        in_specs=specs(n) + specs(nctx) + [
            _layer_spec(wts["bir"], layer), _layer_spec(wts["bfr"], layer),
            _const_spec((SEL_ROWS, N_STREAM * MCH)),
        ],
        out_specs=[out_spec(n), out_spec(nctx)],
        out_shape=[
            jax.ShapeDtypeStruct((nb, n, MP_WIDTH), F32),
            jax.ShapeDtypeStruct((nb, nctx, MP_WIDTH), F32),
        ],
        scratch_shapes=[
            pltpu.VMEM((N_STREAM, HEAD_PAD, HEAD_PAD), F32),
            pltpu.VMEM((N_STREAM, nct, HEAD_PAD, HEAD_PAD), BF16),
            pltpu.VMEM((N_FIELDS, nct, N_STREAM, MCH), F32),
        ],
        compiler_params=_cparams(("arbitrary",)),
        name="mlstm",
    )(*lat, *ctx, wts["bir"], wts["bfr"], sel)


ATTN_SUB = 256
ATTN_HEADS_PER_STEP = 1


def _attn_kernel(*refs, n_sets):
    q_ref = refs[0]
    kv_refs = refs[1:1 + 2 * n_sets]
    o_ref = refs[1 + 2 * n_sets]
    sub = min(ATTN_SUB, q_ref.shape[1])
    n_sub = q_ref.shape[1] // sub
    items = [(hd, t) for hd in range(q_ref.shape[2] // HEAD_PAD) for t in range(n_sub)]

    def scores_of(item):
        hd, t = item
        lanes = slice(hd * HEAD_PAD, (hd + 1) * HEAD_PAD)
        q = q_ref[0, t * sub:(t + 1) * sub, lanes]
        return [_dot_nt(q, kv_refs[2 * i][0, :, lanes]) for i in range(n_sets)]

    nxt = scores_of(items[0])
    for idx, (hd, t) in enumerate(items):
        rows = slice(t * sub, (t + 1) * sub)
        lanes = slice(hd * HEAD_PAD, (hd + 1) * HEAD_PAD)
        scores = nxt
        if idx + 1 < len(items):
            nxt = scores_of(items[idx + 1])
        m = None
        for s in scores:
            sm = jnp.max(s, axis=-1, keepdims=True)
            m = sm if m is None else jnp.maximum(m, sm)
        acc = None
        for i, s in enumerate(scores):
            p = jnp.exp2(s - m)
            o = _dot(p.astype(BF16), kv_refs[2 * i + 1][0, :, lanes])
            acc = o if acc is None else acc + o
        o_ref[0, rows, lanes] = (acc / acc[:, A_V:A_V + 1]).astype(BF16)


def _attention(q, key_sets, tq, heads_per_step):
    nb, n, _ = q.shape
    n_sets = len(key_sets)
    width = heads_per_step * HEAD_PAD
    in_specs = [pl.BlockSpec((1, tq, width), lambda b, h, i: (b, i, h))]
    args = [q]
    for k, v in key_sets:
        nk = k.shape[1]
        spec = pl.BlockSpec((1, nk, width), lambda b, h, i: (b, 0, h))
        in_specs += [spec, spec]
        args += [k, v]
    return pl.pallas_call(
        functools.partial(_attn_kernel, n_sets=n_sets),
        grid=(nb, A_HEADS // heads_per_step, n // tq),
        in_specs=in_specs,
        out_specs=pl.BlockSpec((1, tq, width), lambda b, h, i: (b, i, h)),
        out_shape=jax.ShapeDtypeStruct((nb, n, AP_WIDTH), BF16),
        compiler_params=_cparams(("arbitrary", "arbitrary", "arbitrary")),
        name="attention",
    )(*args)


MLP_CHUNK = 1024


def _out_mlp_kernel(x_ref, yf_ref, hm_ref, mo_ref, ya_ref, mod_ref,
                    gm_ref, g2_ref, gfin_ref, wof_ref, wom_ref, woa_ref, wup_ref, wdn_ref,
                    o_ref, *, final_norm):
    mod = mod_ref[0]
    ga1, sh2, sc2, ga2 = mod[2:3], mod[3:4], mod[4:5], mod[5:6]
    yf = jnp.concatenate([yf_ref[slab].astype(BF16) for slab in range(yf_ref.shape[0])], axis=1)
    mix = _dot(ya_ref[...], woa_ref[0]) + _dot(yf, wof_ref[0])
    gm = gm_ref[0]
    yms = []
    for hd in range(M_HEADS):
        lanes = slice(hd * HEAD_PAD, (hd + 1) * HEAD_PAD)
        hh = hm_ref[:, lanes]
        ms = jnp.sum(hh * hh, axis=-1, keepdims=True) * (1.0 / M_HEAD_DIM)
        ym = hh * lax.rsqrt(ms + EPS) * gm[:, lanes] * jax.nn.sigmoid(mo_ref[:, lanes])
        yms.append(ym.astype(BF16))
    mix = mix + _dot(jnp.concatenate(yms, axis=1), wom_ref[0])
    x1 = x_ref[...] + ga1 * mix
    h2 = (_rms(x1, g2_ref[0]) * (1.0 + sc2) + sh2).astype(BF16)
    acc = None
    for c in range(wup_ref.shape[2] // MLP_CHUNK):
        cols = slice(c * MLP_CHUNK, (c + 1) * MLP_CHUNK)
        u = jnp.maximum(_dot(h2, wup_ref[0, :, cols]), 0.0)
        d = _dot((u * u).astype(BF16), wdn_ref[0, cols, :])
        acc = d if acc is None else acc + d
    x2 = x1 + ga2 * acc
    if final_norm:
        x2 = _rms(x2, gfin_ref[...])
    o_ref[...] = x2


def _out_mlp(x2d, seq, yf, hm, mo, ya, layer, wts, mod, mod_row0, per_batch_mod, gfin, tm, final_norm):
    t, d = x2d.shape
    tiles_per_seq = seq // tm
    tok = lambda w: pl.BlockSpec((tm, w), lambda i: (i, 0))
    yf_spec = pl.BlockSpec((F_WIDTH // LANES, tm, LANES), lambda i: (i // tiles_per_seq, i % tiles_per_seq, 0))
    names = ("gm", "g2", "wof", "wom", "woa", "wup", "wdn")
    lay = {name: _layer_spec(wts[name], layer) for name in names}
    return pl.pallas_call(
        functools.partial(_out_mlp_kernel, final_norm=final_norm),
        grid=(t // tm,),
        in_specs=[
            tok(d), yf_spec, tok(MP_WIDTH), tok(MP_WIDTH), tok(AP_WIDTH),
            pl.BlockSpec((1, 6, d), _mod_map(mod_row0, per_batch_mod, tiles_per_seq)),
            lay["gm"], lay["g2"], _const_spec((1, d)),
            lay["wof"], lay["wom"], lay["woa"], lay["wup"], lay["wdn"],
        ],
        out_specs=tok(d),
        out_shape=jax.ShapeDtypeStruct((t, d), F32),
        compiler_params=_cparams(("arbitrary",)),
        name="out_mlp",
    )(x2d, yf, hm, mo, ya, mod, wts["gm"], wts["g2"], gfin,
      wts["wof"], wts["wom"], wts["woa"], wts["wup"], wts["wdn"])


def _dft_tables(n):
    idx = (np.arange(n, dtype=np.int64)[:, None] * np.arange(n, dtype=np.int64)[None, :]) % n
    ang = 2.0 * np.pi * idx.astype(np.float64) / n
    scale = 1.0 / np.sqrt(n)
    return np.cos(ang) * scale, np.sin(ang) * scale


def _channel_dft():
    c, s = _dft_tables(F_GROUP_DIM)
    eye = np.eye(F_GROUPS)
    return (jnp.asarray(np.kron(eye, c), dtype=F32).astype(BF16),
            jnp.asarray(np.kron(eye, s), dtype=F32).astype(BF16))


def _position_dft(n):
    c, s = _dft_tables(n)
    return jnp.asarray(c, dtype=F32).astype(BF16), jnp.asarray(s, dtype=F32).astype(BF16)


def _rope_tables(n, rotate):
    cos = np.zeros((n, HEAD_PAD), np.float32)
    sin = np.zeros((n, HEAD_PAD), np.float32)
    cos[:, :A_NOPE + A_ROPE] = 1.0
    if rotate:
        nf = A_ROPE // 4
        t = np.arange(n)
        row = (t // GRID_W).astype(np.float32)
        col = (t % GRID_W).astype(np.float32)
        freqs = (np.float32(ROPE_THETA) ** (-np.arange(nf, dtype=np.float32) / np.float32(nf))).astype(np.float32)
        for seg, pos in enumerate((row, col)):
            ang = pos[:, None] * freqs[None, :]
            c, s = np.cos(ang), np.sin(ang)
            base = A_NOPE + seg * 2 * nf
            cos[:, base:base + nf] = c
            cos[:, base + nf:base + 2 * nf] = c
            sin[:, base:base + nf] = -s
            sin[:, base + nf:base + 2 * nf] = s
    return jnp.asarray(cos), jnp.asarray(sin)


def _pad_heads_cols(w, heads, width):
    lead = w.shape[:-1]
    w = w.reshape(lead + (heads, width))
    w = jnp.pad(w, [(0, 0)] * len(lead) + [(0, 0), (0, HEAD_PAD - width)])
    return w.reshape(lead + (heads * HEAD_PAD,))


def _pad_heads_rows(w, heads, width):
    depth, _, n = w.shape
    w = jnp.pad(w.reshape(depth, heads, width, n), [(0, 0), (0, 0), (0, HEAD_PAD - width), (0, 0)])
    return w.reshape(depth, heads * HEAD_PAD, n)


GATE_I_COLS = np.concatenate([np.arange(M_HEADS), 2 * M_HEADS + np.arange(M_HEADS)])
GATE_F_COLS = GATE_I_COLS + M_HEADS


def _prepare_weights(max_chunks, g_norm1, g_norm2, w_in, b_gates, conv_qk, g_mlstm, g_q_norm, g_kv_norm,
                     w_uq, w_ukv, w_out, w_up, w_down):
    offs = np.cumsum([0, F_WIDTH, M_WIDTH, M_WIDTH, M_WIDTH, M_WIDTH, 4 * M_HEADS, Q_LORA, KV_LORA, A_ROPE])
    part = lambda i: w_in[:, :, offs[i]:offs[i + 1]]
    heads = lambda w: _pad_heads_cols(w, M_HEADS, M_HEAD_DIM)
    w_in_p = jnp.concatenate([
        part(0), part(1), part(2), heads(part(4)), part(6), part(7),
        jnp.pad(part(8), [(0, 0), (0, 0), (A_NOPE, LANES - A_NOPE - A_ROPE)]),
    ], axis=2).astype(BF16)
    assert w_in_p.shape[2] == IN_PAD
    gates = part(5)
    w_vt = jnp.concatenate([heads(part(3)), gates[:, :, GATE_I_COLS], gates[:, :, GATE_F_COLS]],
                           axis=2).transpose(0, 2, 1).astype(BF16)

    conv_p = jnp.pad(conv_qk, [(0, 0), (0, 8 - K_CONV), (0, 0)])

    tile_rows = lambda b: jnp.tile(b[:, :, None], (1, max_chunks, 1))
    ukv = w_ukv.reshape(w_ukv.shape[0], KV_LORA, A_HEADS, A_NOPE + A_V)
    pad_kv = lambda w: jnp.pad(w, [(0, 0), (0, 0), (0, 0), (0, HEAD_PAD - w.shape[-1])]).reshape(
        w.shape[0], KV_LORA, AP_WIDTH).astype(BF16)
    vec = lambda g: g[:, None, :]
    return dict(
        g1=vec(g_norm1), g2=vec(g_norm2), gq=vec(g_q_norm), gkv=vec(g_kv_norm),
        w_in_p=w_in_p, w_vt=w_vt, conv_p=conv_p,
        bir=tile_rows(b_gates[:, GATE_I_COLS]), bfr=tile_rows(b_gates[:, GATE_F_COLS]),
        gm=_pad_heads_cols(g_mlstm, M_HEADS, M_HEAD_DIM)[:, None, :],
        wq=_pad_heads_cols(w_uq, A_HEADS, A_NOPE + A_ROPE).astype(BF16),
        wk=pad_kv(ukv[..., :A_NOPE]), wv=pad_kv(ukv[..., A_NOPE:]),
        wof=w_out[:, :F_WIDTH].astype(BF16),
        wom=_pad_heads_rows(w_out[:, F_WIDTH:F_WIDTH + M_WIDTH], M_HEADS, M_HEAD_DIM).astype(BF16),
        woa=_pad_heads_rows(w_out[:, F_WIDTH + M_WIDTH:], A_HEADS, A_V).astype(BF16),
        wup=w_up.astype(BF16), wdn=w_down.astype(BF16),
    )


def _stream_selectors():
    sel = np.zeros((SEL_ROWS, N_STREAM, MCH), np.float32)
    for s in range(N_STREAM):
        for part in range(3):
            sel[part * N_STREAM + s, s, :] = 1.0
    return jnp.asarray(sel.reshape(SEL_ROWS, N_STREAM * MCH), dtype=BF16)


def kernel(x, c, ctx, c_ctx, w_mod, b_mod, g_norm1, g_norm2, w_in, b_gates, conv_qk, g_mlstm,
           g_q_norm, g_kv_norm, w_uq, w_ukv, w_out, w_up, w_down, g_final):
    nb, seq, d = x.shape
    nctx = ctx.shape[1]
    depth = w_mod.shape[0]
    assert d == D_MODEL and seq % 256 == 0 and nctx % MCH == 0
    sel = _stream_selectors()

    tm = min(512, seq)
    tm_mlp = min(512, seq)
    tm_ctx = min(256, nctx)
    tq = min(8 * ATTN_SUB, seq)
    tq_ctx = min(ATTN_SUB, nctx)

    dft_cc, dft_cs = _channel_dft()
    fft_lat = _fourier4_tables(seq)
    dft_ctx = _position_dft(nctx)
    rope_lat = _rope_tables(seq, True)
    rope_ctx = _rope_tables(nctx, False)
    wts = _prepare_weights(max(seq, nctx) // MCH, g_norm1, g_norm2, w_in, b_gates, conv_qk, g_mlstm,
                           g_q_norm, g_kv_norm, w_uq, w_ukv, w_out, w_up, w_down)
    gfin = g_final.reshape(1, d)

    rows = ((nb + 1 + 7) // 8) * 8
    cc = jnp.concatenate([c, c_ctx[None, :], jnp.zeros((rows - nb - 1, d), F32)], axis=0)
    mod_all = _modulation(cc, w_mod, b_mod).reshape(depth * rows, 6, d)

    xl = x.reshape(nb * seq, d)
    xc = ctx.reshape(nb * nctx, d)

    for l in range(depth):
        last = l == depth - 1
        row_lat, row_ctx = l * rows, l * rows + nb

        zc, zs, mqk, vt, mo, gr, q_a, k_a, v_a = _inproj(
            xl, seq, l, wts, mod_all, row_lat, True, dft_cc, dft_cs, *rope_lat, tm, True)
        zc_c, zs_c, mqk_c, vt_c, mo_c, gr_c, q_ac, k_ac, v_ac = _inproj(
            xc, nctx, l, wts, mod_all, row_ctx, False, dft_cc, dft_cs, *rope_ctx, tm_ctx, not last)

        yf = _fourier4(fft_lat, zc, zs)

        def mlstm_inputs(mqk_s, vt_s, gr_s, n):
            q_s, k_s = _conv_silu(mqk_s.reshape(nb, n, 2 * M_WIDTH), l, wts)
            return (q_s, k_s, vt_s, gr_s)

        hm, hm_c = _mlstm(mlstm_inputs(mqk, vt, gr, seq), mlstm_inputs(mqk_c, vt_c, gr_c, nctx), l, wts, sel)

        b3 = lambda a, n: a.reshape(nb, n, AP_WIDTH)
        keys_ctx = (b3(k_ac, nctx), b3(v_ac, nctx))
        ya = _attention(b3(q_a, seq), [(b3(k_a, seq), b3(v_a, seq)), keys_ctx], tq, ATTN_HEADS_PER_STEP)

        xl = _out_mlp(xl, seq, yf, hm.reshape(nb * seq, MP_WIDTH), mo, ya.reshape(nb * seq, AP_WIDTH),
                      l, wts, mod_all, row_lat, True, gfin, tm_mlp, last)

        if not last:
            yf_c = _fourier(*dft_ctx, zc_c, zs_c)
            ya_c = _attention(b3(q_ac, nctx), [keys_ctx], tq_ctx, A_HEADS)
            xc = _out_mlp(xc, nctx, yf_c, hm_c.reshape(nb * nctx, MP_WIDTH), mo_c,
                          ya_c.reshape(nb * nctx, AP_WIDTH), l, wts, mod_all, row_ctx, False, gfin, tm_ctx, False)

    return xl.reshape(nb, seq, d)
```

```python
import functools

import numpy as np
import jax
import jax.numpy as jnp
from jax import lax
from jax.experimental import pallas as pl
from jax.experimental.pallas import tpu as pltpu

D_MODEL = 1024
GRID_W = 64
EPS = 1e-6
F_GROUPS = 4
F_GROUP_DIM = D_MODEL // 16
F_WIDTH = F_GROUPS * F_GROUP_DIM
M_HEADS = 4
M_HEAD_DIM = 3 * D_MODEL // 32
M_WIDTH = M_HEADS * M_HEAD_DIM
K_CONV = 5
A_HEADS = 4
A_NOPE = 64
A_ROPE = 32
A_V = 3 * D_MODEL // 32
Q_LORA = D_MODEL // 4
KV_LORA = D_MODEL // 8
ROPE_THETA = 10000.0
MLP_HIDDEN = 4 * D_MODEL

LANES = 128
HEAD_PAD = 128
MP_WIDTH = M_HEADS * HEAD_PAD
AP_WIDTH = A_HEADS * HEAD_PAD
VMEM_LIMIT = 56 * 1024 * 1024
MCH = 128
N_STREAM = 2 * M_HEADS

OFF_PF = 0
OFF_MQ = OFF_PF + F_WIDTH
OFF_MO = OFF_MQ + 2 * M_WIDTH
OFF_CQ = OFF_MO + MP_WIDTH
OFF_CKV = OFF_CQ + Q_LORA
OFF_KR = OFF_CKV + KV_LORA
IN_PAD = OFF_KR + LANES

BF16 = jnp.bfloat16
F32 = jnp.float32
LOG2_E = 1.4426950408889634


def _cparams(sem):
    return pltpu.CompilerParams(dimension_semantics=sem, vmem_limit_bytes=VMEM_LIMIT)


def _const_spec(shape):
    nd = len(shape)
    return pl.BlockSpec(shape, lambda *_: (0,) * nd, pipeline_mode=pl.Buffered(1))


def _layer_spec(arr, layer):
    nd = arr.ndim
    return pl.BlockSpec((1,) + arr.shape[1:], lambda *_: (layer,) + (0,) * (nd - 1), pipeline_mode=pl.Buffered(1))


def _split3(a):
    hi = a.astype(BF16)
    r1 = a - hi.astype(F32)
    mid = r1.astype(BF16)
    lo = (r1 - mid.astype(F32)).astype(BF16)
    return hi, mid, lo


def _dot(a, b):
    return jnp.dot(a, b, preferred_element_type=F32)


def _dot_nt(a, b):
    return lax.dot_general(a, b, (((1,), (1,)), ((), ())), preferred_element_type=F32)


def _dot_tn(a, b):
    return lax.dot_general(a, b, (((0,), (0,)), ((), ())), preferred_element_type=F32)


def _rms(x, g):
    return x * lax.rsqrt(jnp.mean(x * x, axis=-1, keepdims=True) + EPS) * g


def _mod_kernel(c_ref, w_ref, b_ref, o_ref):
    c = c_ref[...]
    a = c * jax.nn.sigmoid(c)
    a_hi = a.astype(BF16)
    a_lo = (a - a_hi.astype(F32)).astype(BF16)
    w = w_ref[0]
    w_hi = w.astype(BF16)
    w_lo = (w - w_hi.astype(F32)).astype(BF16)
    acc = _dot(a_hi, w_hi) + _dot(a_hi, w_lo) + _dot(a_lo, w_hi)
    o_ref[0] = acc + b_ref[0]


def _modulation(cc, w_mod, b_mod):
    depth, d, n = w_mod.shape
    rows = cc.shape[0]
    tn = 1536
    return pl.pallas_call(
        _mod_kernel,
        grid=(depth, n // tn),
        in_specs=[
            pl.BlockSpec((rows, d), lambda l, j: (0, 0)),
            pl.BlockSpec((1, d, tn), lambda l, j: (l, 0, j)),
            pl.BlockSpec((1, 1, tn), lambda l, j: (l, 0, j)),
        ],
        out_specs=pl.BlockSpec((1, rows, tn), lambda l, j: (l, 0, j)),
        out_shape=jax.ShapeDtypeStruct((depth, rows, n), F32),
        compiler_params=_cparams(("arbitrary", "arbitrary")),
        name="modulation",
    )(cc, w_mod, b_mod.reshape(depth, 1, n))


def _rope(x, cos, sin, first_half):
    half = A_ROPE // 4
    partner = jnp.where(first_half, pltpu.roll(x, LANES - half, 1), pltpu.roll(x, half, 1))
    return x * cos + partner * sin


def _inproj_kernel(x_ref, g_ref, mod_ref, w_ref, wvt_ref, cc_ref, cs_ref,
                   cos_ref, sin_ref, gq_ref, gkv_ref, wq_ref, wk_ref, wv_ref,
                   zc_ref, zs_ref, mqk_ref, vt_ref, mo_ref, gr_ref, qa_ref, ka_ref, va_ref, *, with_q):
    x = x_ref[...]
    mod = mod_ref[0]
    h = _rms(x, g_ref[0]) * (1.0 + mod[1:2]) + mod[0:1]
    hb = h.astype(BF16)

    def proj(off, width):
        return _dot(hb, w_ref[0, :, off:off + width])

    ckv_kr = proj(OFF_CKV, 2 * LANES)
    cq = proj(OFF_CQ, Q_LORA) if with_q else None
    pf = proj(OFF_PF, F_WIDTH).astype(BF16)

    mqk_ref[...] = proj(OFF_MQ, 2 * M_WIDTH)

    cos = cos_ref[...]
    sin = sin_ref[...]
    lane = lax.broadcasted_iota(jnp.int32, cos.shape, 1)
    first_half = ((lane - A_NOPE) & (A_ROPE // 2 - 1)) < A_ROPE // 4
    kvn = _rms(ckv_kr[:, :KV_LORA], gkv_ref[0]).astype(BF16)
    k_rope = _rope(ckv_kr[:, KV_LORA:], cos, sin, first_half)
    k_nope = _dot(kvn, wk_ref[0])
    va = _dot(kvn, wv_ref[0])
    head_lane = lax.broadcasted_iota(jnp.int32, va.shape, 1) & (HEAD_PAD - 1)
    va_ref[...] = jnp.where(head_lane == A_V, 1.0, va).astype(BF16)
    for hd in range(A_HEADS):
        lanes = slice(hd * HEAD_PAD, (hd + 1) * HEAD_PAD)
        ka_ref[:, lanes] = (k_nope[:, lanes] + k_rope).astype(BF16)
    if with_q:
        qn = _rms(cq, gq_ref[0]).astype(BF16)
        q_all = _dot(qn, wq_ref[0])
        q_raw = [q_all[:, hd * HEAD_PAD:(hd + 1) * HEAD_PAD] for hd in range(A_HEADS)]
    zc_ref[...] = _dot(pf, cc_ref[...]).astype(BF16)
    zs_ref[...] = _dot(pf, cs_ref[...]).astype(BF16)

    vg = _dot_nt(wvt_ref[0], hb)
    vt = vg[:M_WIDTH].astype(BF16)
    for j in range(vt_ref.shape[0]):
        vt_ref[j] = vt[:, j * MCH:(j + 1) * MCH]
        gr_ref[j] = vg[M_WIDTH:, j * MCH:(j + 1) * MCH]
    mo_ref[...] = proj(OFF_MO, MP_WIDTH)

    if with_q:
        scale = (A_NOPE + A_ROPE) ** -0.5 * LOG2_E
        for hd in range(A_HEADS):
            q = _rope(q_raw[hd], cos, sin, first_half)
            qa_ref[:, hd * HEAD_PAD:(hd + 1) * HEAD_PAD] = (q * scale).astype(BF16)
    else:
        qa_ref[...] = jnp.zeros_like(qa_ref)


def _mod_map(mod_row0, per_batch_mod, tiles_per_seq):
    if per_batch_mod:
        return lambda i: (mod_row0 + i // tiles_per_seq, 0, 0)
    return lambda i: (mod_row0, 0, 0)


def _inproj(x2d, seq, layer, wts, mod, mod_row0, per_batch_mod, dft_cc, dft_cs, cos, sin, tm, with_q):
    t, d = x2d.shape
    nb = t // seq
    tiles_per_seq = seq // tm
    tok = lambda w: pl.BlockSpec((tm, w), lambda i: (i, 0))
    z_spec = pl.BlockSpec((tm, F_WIDTH), lambda i: (i % tiles_per_seq, i // tiles_per_seq))
    pos = pl.BlockSpec((tm, LANES), lambda i: (i % tiles_per_seq, 0))
    heads_bf16 = jax.ShapeDtypeStruct((t, AP_WIDTH), BF16)
    shapes = [
        jax.ShapeDtypeStruct((seq, nb * F_WIDTH), BF16),
        jax.ShapeDtypeStruct((seq, nb * F_WIDTH), BF16),
        jax.ShapeDtypeStruct((t, 2 * M_WIDTH), F32),
        jax.ShapeDtypeStruct((t // MCH, M_WIDTH, MCH), BF16),
        jax.ShapeDtypeStruct((t, MP_WIDTH), F32),
        jax.ShapeDtypeStruct((t // MCH, 2 * N_STREAM, MCH), F32),
        heads_bf16, heads_bf16, heads_bf16,
    ]
    vt_spec = pl.BlockSpec((tm // MCH, M_WIDTH, MCH), lambda i: (i, 0, 0))
    gr_spec = pl.BlockSpec((tm // MCH, 2 * N_STREAM, MCH), lambda i: (i, 0, 0))
    out_specs = [z_spec, z_spec, tok(2 * M_WIDTH), vt_spec, tok(MP_WIDTH), gr_spec,
                 tok(AP_WIDTH), tok(AP_WIDTH), tok(AP_WIDTH)]
    lay = lambda name: _layer_spec(wts[name], layer)
    return pl.pallas_call(
        functools.partial(_inproj_kernel, with_q=with_q),
        grid=(t // tm,),
        in_specs=[
            tok(d),
            lay("g1"),
            pl.BlockSpec((1, 6, d), _mod_map(mod_row0, per_batch_mod, tiles_per_seq)),
            lay("w_in_p"), lay("w_vt"),
            _const_spec((F_WIDTH, F_WIDTH)),
            _const_spec((F_WIDTH, F_WIDTH)),
            pos, pos,
            lay("gq"), lay("gkv"), lay("wq"), lay("wk"), lay("wv"),
        ],
        out_specs=out_specs,
        out_shape=shapes,
        compiler_params=_cparams(("arbitrary",)),
        name="inproj",
    )(x2d, wts["g1"], mod, wts["w_in_p"], wts["w_vt"], dft_cc, dft_cs, cos, sin,
      wts["gq"], wts["gkv"], wts["wq"], wts["wk"], wts["wv"])


def _fourier_kernel(c_ref, s_ref, zc_ref, zs_ref, o_ref):
    y = _dot(c_ref[...], zc_ref[...]) - _dot(s_ref[...], zs_ref[...])
    for slab in range(o_ref.shape[0]):
        o_ref[slab] = y[:, slab * LANES:(slab + 1) * LANES]


def _fourier(dft_c, dft_s, zc, zs):
    n, cols = zc.shape
    tr = min(n, 512)
    tc = min(cols, 512)
    return pl.pallas_call(
        _fourier_kernel,
        grid=(n // tr, cols // tc),
        in_specs=[
            pl.BlockSpec((tr, n), lambda i, j: (i, 0)),
            pl.BlockSpec((tr, n), lambda i, j: (i, 0)),
            pl.BlockSpec((n, tc), lambda i, j: (0, j)),
            pl.BlockSpec((n, tc), lambda i, j: (0, j)),
        ],
        out_specs=pl.BlockSpec((tc // LANES, tr, LANES), lambda i, j: (j, i, 0)),
        out_shape=jax.ShapeDtypeStruct((cols // LANES, n, LANES), F32),
        compiler_params=_cparams(("arbitrary", "arbitrary")),
        name="fourier",
    )(dft_c, dft_s, zc, zs)


FFT_COLS = 256


def _fourier4_kernel(tab_ref, twc_ref, tws_ref, zc_ref, zs_ref, o_ref):
    m = zc_ref.shape[0] // 4
    reps = zc_ref.shape[1] // LANES
    c0, c1, c2, c3 = (zc_ref[j * m:(j + 1) * m, :].astype(F32) for j in range(4))
    s0, s1, s2, s3 = (zs_ref[j * m:(j + 1) * m, :].astype(F32) for j in range(4))
    ce, co, cd, cu = c0 + c2, c1 + c3, c0 - c2, c1 - c3
    se, so, sd, su = s0 + s2, s1 + s3, s0 - s2, s1 - s3

    def emit(br, bi, k):
        if k:
            cos = jnp.concatenate([twc_ref[k - 1]] * reps, axis=1)
            sin = jnp.concatenate([tws_ref[k - 1]] * reps, axis=1)
            br, bi = br * cos + bi * sin, bi * cos - br * sin
        stacked = jnp.concatenate([br.astype(BF16), bi.astype(BF16)], axis=0)
        y = _dot(tab_ref[...], stacked)
        for slab in range(reps):
            o_ref[slab, pl.ds(k, m, stride=4), :] = y[:, slab * LANES:(slab + 1) * LANES]

    emit(ce + co, -(se + so), 0)
    emit(cd - su, -sd - cu, 1)
    emit(ce - co, so - se, 2)
    emit(cd + su, cu - sd, 3)


def _fourier4(tables, zc, zs):
    tab, twc, tws = tables
    n, cols = zc.shape
    m = n // 4
    tc = min(cols, FFT_COLS)
    return pl.pallas_call(
        _fourier4_kernel,
        grid=(cols // tc,),
        in_specs=[
            _const_spec((m, 2 * m)),
            _const_spec((3, m, LANES)),
            _const_spec((3, m, LANES)),
            pl.BlockSpec((n, tc), lambda j: (0, j)),
            pl.BlockSpec((n, tc), lambda j: (0, j)),
        ],
        out_specs=pl.BlockSpec((tc // LANES, n, LANES), lambda j: (j, 0, 0)),
        out_shape=jax.ShapeDtypeStruct((cols // LANES, n, LANES), F32),
        compiler_params=_cparams(("arbitrary",)),
        name="fourier4",
    )(tab, twc, tws, zc, zs)


def _fourier4_tables(n):
    m = n // 4
    idx = (np.arange(m, dtype=np.int64)[:, None] * np.arange(m, dtype=np.int64)[None, :]) % m
    ang = 2.0 * np.pi * idx.astype(np.float64) / m
    tab = np.concatenate([np.cos(ang), np.sin(ang)], axis=1) / np.sqrt(n)
    theta = 2.0 * np.pi * np.arange(m, dtype=np.float64)[None, :] * np.arange(1, 4, dtype=np.float64)[:, None] / n
    bcast = lambda t: jnp.asarray(np.repeat(t[:, :, None], LANES, axis=2), dtype=F32)
    return jnp.asarray(tab, dtype=F32).astype(BF16), bcast(np.cos(theta)), bcast(np.sin(theta))


CONV_ROWS = 256
CONV_HALO = 8


def _head_tile(x, start, lane):
    t, o = divmod(start, LANES)
    tile = lambda i: x[:, i * LANES:(i + 1) * LANES]
    if o == 0:
        return tile(t)
    shift = LANES - o
    out = pltpu.roll(tile(t), shift, 1)
    if shift < M_HEAD_DIM:
        out = jnp.where(lane < shift, out, pltpu.roll(tile(t + 1), shift, 1))
    return out


def _conv_kernel(u_ref, w_ref, q_ref, k_ref, pad_ref):
    n = u_ref.shape[1]
    width = u_ref.shape[2]
    zeros = jnp.zeros((CONV_HALO, width), F32)
    pad_ref[0:CONV_HALO, :] = zeros
    pad_ref[CONV_HALO + n:2 * CONV_HALO + n, :] = zeros
    pad_ref[CONV_HALO:CONV_HALO + n, :] = u_ref[0]
    w = w_ref[0]
    rows = min(CONV_ROWS, n)
    span = rows + 2 * CONV_HALO
    for r in range(n // rows):
        block = pad_ref[r * rows:r * rows + span, :]
        acc = None
        for j in range(K_CONV):
            shift = (K_CONV // 2 - j) % span
            tap = block if shift == 0 else pltpu.roll(block, shift, 0)
            term = tap[CONV_HALO:CONV_HALO + rows, :] * w[j:j + 1, :]
            acc = term if acc is None else acc + term
        act = acc * jax.nn.sigmoid(acc)
        lane = lax.broadcasted_iota(jnp.int32, (rows, LANES), 1)
        for hd in range(M_HEADS):
            lanes = slice(hd * HEAD_PAD, (hd + 1) * HEAD_PAD)
            q_tile = _head_tile(act, hd * M_HEAD_DIM, lane)
            k_tile = _head_tile(act, M_WIDTH + hd * M_HEAD_DIM, lane)
            q_ref[0, r * rows:(r + 1) * rows, lanes] = (q_tile * (M_HEAD_DIM ** -0.5)).astype(BF16)
            k_ref[0, r * rows:(r + 1) * rows, lanes] = jnp.where(lane < M_HEAD_DIM, k_tile, 0.0).astype(BF16)


def _conv_silu(mqk, layer, wts):
    nb, n, width = mqk.shape
    return pl.pallas_call(
        _conv_kernel,
        grid=(nb,),
        in_specs=[
            pl.BlockSpec((1, n, width), lambda b: (b, 0, 0)),
            _layer_spec(wts["conv_p"], layer),
        ],
        out_specs=[
            pl.BlockSpec((1, n, MP_WIDTH), lambda b: (b, 0, 0)),
            pl.BlockSpec((1, n, MP_WIDTH), lambda b: (b, 0, 0)),
        ],
        out_shape=[
            jax.ShapeDtypeStruct((nb, n, MP_WIDTH), BF16),
            jax.ShapeDtypeStruct((nb, n, MP_WIDTH), BF16),
        ],
        scratch_shapes=[pltpu.VMEM((n + 2 * CONV_HALO, width), F32)],
        compiler_params=_cparams(("arbitrary",)),
        name="conv_silu",
    )(mqk, wts["conv_p"])


def _log_sigmoid(x):
    return jnp.minimum(x, 0.0) - jnp.log(1.0 + jnp.exp(-jnp.abs(x)))


def _exact_dot_01(a, tri_bf16, a_on_left):
    out = None
    for term in _split3(a):
        d = _dot(term, tri_bf16) if a_on_left else _dot(tri_bf16, term)
        out = d if out is None else out + d
    return out


ONE_ROW = M_HEAD_DIM
(F_R, F_B, F_CM, F_TOT, F_CML, F_MP, F_A, F_WI, F_ELD, F_DEC, F_WK, F_HI, F_MID, F_LO) = range(14)
N_FIELDS = 14
STATE_GROUP = 4
OUT_GROUP = 8
SEL_ROWS = 32


def _mlstm_kernel(ql_ref, kl_ref, vtl_ref, grl_ref,
                  qc_ref, kc_ref, vtc_ref, grc_ref,
                  bir_ref, bfr_ref, sel_ref,
                  hl_ref, hc_ref,
                  ct_st, st_sc, rows_sc):
    L = MCH
    nh = M_HEADS
    ncc = qc_ref.shape[1] // L
    ncl = ql_ref.shape[1] // L

    d0 = lax.broadcasted_iota(jnp.int32, (L, L), 0)
    d1 = lax.broadcasted_iota(jnp.int32, (L, L), 1)
    le = d0 <= d1
    ge = d0 >= d1
    tri_le = le.astype(BF16)
    fwd_rows = lax.broadcasted_iota(jnp.int32, (N_STREAM, L), 0) < nh
    feat = lax.broadcasted_iota(jnp.int32, (HEAD_PAD, L), 0)
    keep_rows = feat < M_HEAD_DIM
    ones_tail = (lax.broadcasted_iota(jnp.int32, (HEAD_PAD - M_HEAD_DIM, L), 0) == 0).astype(F32)

    ct_st[...] = jnp.zeros_like(ct_st)

    def field(f, slot0, nc):
        return rows_sc[f, slot0:slot0 + nc].reshape(nc * N_STREAM, L)

    def set_field(f, slot0, nc, val):
        rows_sc[f, slot0:slot0 + nc] = val.reshape(nc, N_STREAM, L)

    def gate_pass(gr_ref, nc, slot0):
        n8 = nc * N_STREAM
        fwd = (lax.broadcasted_iota(jnp.int32, (n8, L), 0) & (N_STREAM - 1)) < nh
        lane = lax.broadcasted_iota(jnp.int32, (n8, L), 1)
        gi = gr_ref[:, 0:N_STREAM, :].reshape(n8, L) + bir_ref[0, 0:n8, :]
        f = _log_sigmoid(gr_ref[:, N_STREAM:2 * N_STREAM, :].reshape(n8, L) + bfr_ref[0, 0:n8, :])
        pre = _exact_dot_01(f, tri_le, a_on_left=True)
        total = jnp.sum(f, axis=1, keepdims=True)
        b = jnp.where(fwd, pre, total - pre + f)
        r = gi - b
        cm = r
        sh = 1
        while sh < L:
            from_left = jnp.where(lane >= sh, pltpu.roll(cm, sh, 1), -jnp.inf)
            from_right = jnp.where(lane < L - sh, pltpu.roll(cm, L - sh, 1), -jnp.inf)
            cm = jnp.maximum(cm, jnp.where(fwd, from_left, from_right))
            sh *= 2
        set_field(F_R, slot0, nc, r)
        set_field(F_B, slot0, nc, b)
        set_field(F_CM, slot0, nc, cm)
        set_field(F_TOT, slot0, nc, jnp.broadcast_to(total, (n8, L)))
        set_field(F_CML, slot0, nc, jnp.broadcast_to(jnp.max(r, axis=1, keepdims=True), (n8, L)))

    def m_scan(nc, slot0, m0):
        def step(j, m):
            sf = slot0 + j
            sb = slot0 + nc - 1 - j
            tot = jnp.where(fwd_rows, rows_sc[F_TOT, sf], rows_sc[F_TOT, sb])
            cml = jnp.where(fwd_rows, rows_sc[F_CML, sf], rows_sc[F_CML, sb])
            rows_sc[F_MP, sf, 0:nh, :] = m[0:nh]
            rows_sc[F_MP, sb, nh:N_STREAM, :] = m[nh:N_STREAM]
            return tot + jnp.maximum(m, cml)

        return lax.fori_loop(0, nc, step, m0)

    def weight_pass(nc, slot0):
        mp, cm, b, r = (field(f, slot0, nc) for f in (F_MP, F_CM, F_B, F_R))
        big = jnp.maximum(mp, field(F_CML, slot0, nc))
        a = -jnp.maximum(mp, cm)
        set_field(F_A, slot0, nc, a)
        set_field(F_WI, slot0, nc, jnp.exp(mp + a))
        set_field(F_ELD, slot0, nc, jnp.exp(a - b))
        set_field(F_DEC, slot0, nc, jnp.exp(mp - big))
        set_field(F_WK, slot0, nc, jnp.exp(r - big))
        for f, term in zip((F_HI, F_MID, F_LO), _split3(r)):
            set_field(f, slot0, nc, term.astype(F32))

    def value_slab(vt_ref, c, hd):
        vt = vt_ref[c, hd * M_HEAD_DIM:(hd + 1) * M_HEAD_DIM, :].astype(F32)
        return jnp.concatenate([vt, ones_tail], axis=0)

    def state_pass(k_ref, vt_ref, nc, slot0):
        group = min(STATE_GROUP, nc)

        def updates_of(j):
            uts = []
            for sidx in range(N_STREAM):
                hd = sidx % nh
                c = j if sidx < nh else nc - 1 - j
                wk = rows_sc[F_WK, slot0 + c, sidx:sidx + 1, :]
                kk = k_ref[0, pl.ds(pl.multiple_of(c * L, L), L), hd * HEAD_PAD:(hd + 1) * HEAD_PAD]
                uts.append(_dot((value_slab(vt_ref, c, hd) * wk).astype(BF16), kk))
            return uts

        def step(g, carry):
            all_uts = [updates_of(g * group + u) for u in range(group)]
            for u in range(group):
                j = g * group + u
                for sidx in range(N_STREAM):
                    slot = slot0 + (j if sidx < nh else nc - 1 - j)
                    prev = ct_st[sidx]
                    st_sc[sidx, slot] = prev.astype(BF16)
                    ct_st[sidx] = rows_sc[F_DEC, slot, sidx:sidx + 1, :] * prev + all_uts[u][sidx]
            return carry

        lax.fori_loop(0, nc // group, step, 0)

    def output_pass(q_ref, k_ref, vt_ref, out_ref, nc, slot0):
        head_lanes = [slice(hd * HEAD_PAD, (hd + 1) * HEAD_PAD) for hd in range(nh)]

        def independent_matmuls(c):
            rows = pl.ds(pl.multiple_of(c * L, L), L)
            slot = slot0 + c
            r3 = jnp.concatenate([rows_sc[F_HI, slot], rows_sc[F_MID, slot], rows_sc[F_LO, slot],
                                  jnp.zeros((N_STREAM, L), F32)], axis=0).astype(BF16)
            qs = [q_ref[0, rows, lanes] for lanes in head_lanes]
            s_ts = [_dot_nt(k_ref[0, rows, lanes], q) for lanes, q in zip(head_lanes, qs)]
            inters = [_dot_nt(st_sc[sidx, slot], qs[sidx % nh]) for sidx in range(N_STREAM)]
            r_all = _dot_tn(r3, sel_ref[...])
            r_ts = [r_all[:, sidx * L:(sidx + 1) * L] for sidx in range(N_STREAM)]
            return s_ts, inters, r_ts

        def finish(c, s_ts, inters, r_ts):
            rows = pl.ds(pl.multiple_of(c * L, L), L)
            slot = slot0 + c
            a_rows = rows_sc[F_A, slot]
            wi_rows = rows_sc[F_WI, slot]
            eld_rows = rows_sc[F_ELD, slot]
            p_ts = []
            for sidx in range(N_STREAM):
                one = slice(sidx, sidx + 1)
                dm = jnp.where(le if sidx < nh else ge, r_ts[sidx] + a_rows[one, :], -jnp.inf)
                p_ts.append((s_ts[sidx % nh] * jnp.exp(dm)).astype(BF16))
            z_pairs = [_dot(value_slab(vt_ref, c, hd).astype(BF16),
                            jnp.concatenate([p_ts[hd], p_ts[nh + hd]], axis=1)) for hd in range(nh)]
            for hd in range(nh):
                hsum = None
                for dr, sidx in enumerate((hd, nh + hd)):
                    one = slice(sidx, sidx + 1)
                    z_t = z_pairs[hd][:, dr * L:(dr + 1) * L] + inters[sidx] * wi_rows[one, :]
                    den = z_t[ONE_ROW:ONE_ROW + 1, :]
                    h_t = z_t * (1.0 / jnp.maximum(jnp.abs(den), eld_rows[one, :]))
                    hsum = h_t if hsum is None else hsum + h_t
                out_ref[0, rows, head_lanes[hd]] = jnp.where(keep_rows, hsum, 0.0).T

        group = min(OUT_GROUP, nc)

        def step(g, carry):
            ahead = independent_matmuls(g * group)
            for u in range(group):
                cur = ahead
                if u + 1 < group:
                    ahead = independent_matmuls(g * group + u + 1)
                finish(g * group + u, *cur)
            return carry

        lax.fori_loop(0, nc // group, step, 0)

    gate_pass(grc_ref, ncc, 0)
    gate_pass(grl_ref, ncl, ncc)
    m1 = m_scan(ncc, 0, jnp.zeros((N_STREAM, L), F32))
    m_scan(ncl, ncc, m1)
    weight_pass(ncc, 0)
    weight_pass(ncl, ncc)
    state_pass(kc_ref, vtc_ref, ncc, 0)
    state_pass(kl_ref, vtl_ref, ncl, ncc)
    output_pass(qc_ref, kc_ref, vtc_ref, hc_ref, ncc, 0)
    output_pass(ql_ref, kl_ref, vtl_ref, hl_ref, ncl, ncc)


def _mlstm(lat, ctx, layer, wts, sel):
    nb, n, _ = lat[0].shape
    nctx = ctx[0].shape[1]
    assert MCH == LANES == HEAD_PAD
    nct = (n + nctx) // MCH
    assert wts["bir"].shape[1] >= max(n, nctx) // MCH * N_STREAM

    def specs(rows):
        nc = rows // MCH
        seq = lambda w: pl.BlockSpec((1, rows, w), lambda b: (b, 0, 0))
        chunked = lambda rows_: pl.BlockSpec((nc, rows_, MCH), lambda b: (b, 0, 0))
        return [seq(MP_WIDTH), seq(MP_WIDTH), chunked(M_WIDTH), chunked(2 * N_STREAM)]

    out_spec = lambda rows: pl.BlockSpec((1, rows, MP_WIDTH), lambda b: (b, 0, 0))
    return pl.pallas_call(
        _mlstm_kernel,
        grid=(nb,),
        in_specs=specs(n) + specs(nctx) + [
            _layer_spec(wts["bir"], layer), _layer_spec(wts["bfr"], layer),
            _const_spec((SEL_ROWS, N_STREAM * MCH)),
        ],
        out_specs=[out_spec(n), out_spec(nctx)],
        out_shape=[
            jax.ShapeDtypeStruct((nb, n, MP_WIDTH), F32),
            jax.ShapeDtypeStruct((nb, nctx, MP_WIDTH), F32),
        ],
        scratch_shapes=[
            pltpu.VMEM((N_STREAM, HEAD_PAD, HEAD_PAD), F32),
            pltpu.VMEM((N_STREAM, nct, HEAD_PAD, HEAD_PAD), BF16),
            pltpu.VMEM((N_FIELDS, nct, N_STREAM, MCH), F32),
        ],
        compiler_params=_cparams(("arbitrary",)),
        name="mlstm",
    )(*lat, *ctx, wts["bir"], wts["bfr"], sel)


ATTN_SUB = 256
ATTN_HEADS_PER_STEP = 1


def _attn_kernel(*refs, n_sets):
    q_ref = refs[0]
    kv_refs = refs[1:1 + 2 * n_sets]
    o_ref = refs[1 + 2 * n_sets]
    sub = min(ATTN_SUB, q_ref.shape[1])
    n_sub = q_ref.shape[1] // sub
    items = [(hd, t) for hd in range(q_ref.shape[2] // HEAD_PAD) for t in range(n_sub)]

    def scores_of(item):
        hd, t = item
        lanes = slice(hd * HEAD_PAD, (hd + 1) * HEAD_PAD)
        q = q_ref[0, t * sub:(t + 1) * sub, lanes]
        return [_dot_nt(q, kv_refs[2 * i][0, :, lanes]) for i in range(n_sets)]

    nxt = scores_of(items[0])
    for idx, (hd, t) in enumerate(items):
        rows = slice(t * sub, (t + 1) * sub)
        lanes = slice(hd * HEAD_PAD, (hd + 1) * HEAD_PAD)
        scores = nxt
        if idx + 1 < len(items):
            nxt = scores_of(items[idx + 1])
        m = None
        for s in scores:
            sm = jnp.max(s, axis=-1, keepdims=True)
            m = sm if m is None else jnp.maximum(m, sm)
        acc = None
        for i, s in enumerate(scores):
            p = jnp.exp2(s - m)
            o = _dot(p.astype(BF16), kv_refs[2 * i + 1][0, :, lanes])
            acc = o if acc is None else acc + o
        o_ref[0, rows, lanes] = (acc / acc[:, A_V:A_V + 1]).astype(BF16)


def _attention(q, key_sets, tq, heads_per_step):
    nb, n, _ = q.shape
    n_sets = len(key_sets)
    width = heads_per_step * HEAD_PAD
    in_specs = [pl.BlockSpec((1, tq, width), lambda b, h, i: (b, i, h))]
    args = [q]
    for k, v in key_sets:
        nk = k.shape[1]
        spec = pl.BlockSpec((1, nk, width), lambda b, h, i: (b, 0, h))
        in_specs += [spec, spec]
        args += [k, v]
    return pl.pallas_call(
        functools.partial(_attn_kernel, n_sets=n_sets),
        grid=(nb, A_HEADS // heads_per_step, n // tq),
        in_specs=in_specs,
        out_specs=pl.BlockSpec((1, tq, width), lambda b, h, i: (b, i, h)),
        out_shape=jax.ShapeDtypeStruct((nb, n, AP_WIDTH), BF16),
        compiler_params=_cparams(("arbitrary", "arbitrary", "arbitrary")),
        name="attention",
    )(*args)


MLP_CHUNK = 1024


def _out_mlp_kernel(x_ref, yf_ref, hm_ref, mo_ref, ya_ref, mod_ref,
                    gm_ref, g2_ref, gfin_ref, wof_ref, wom_ref, woa_ref, wup_ref, wdn_ref,
                    o_ref, *, final_norm):
    mod = mod_ref[0]
    ga1, sh2, sc2, ga2 = mod[2:3], mod[3:4], mod[4:5], mod[5:6]
    yf = jnp.concatenate([yf_ref[slab].astype(BF16) for slab in range(yf_ref.shape[0])], axis=1)
    mix = _dot(ya_ref[...], woa_ref[0]) + _dot(yf, wof_ref[0])
    gm = gm_ref[0]
    yms = []
    for hd in range(M_HEADS):
        lanes = slice(hd * HEAD_PAD, (hd + 1) * HEAD_PAD)
        hh = hm_ref[:, lanes]
        ms = jnp.sum(hh * hh, axis=-1, keepdims=True) * (1.0 / M_HEAD_DIM)
        ym = hh * lax.rsqrt(ms + EPS) * gm[:, lanes] * jax.nn.sigmoid(mo_ref[:, lanes])
        yms.append(ym.astype(BF16))
    mix = mix + _dot(jnp.concatenate(yms, axis=1), wom_ref[0])
    x1 = x_ref[...] + ga1 * mix
    h2 = (_rms(x1, g2_ref[0]) * (1.0 + sc2) + sh2).astype(BF16)
    acc = None
    for c in range(wup_ref.shape[2] // MLP_CHUNK):
        cols = slice(c * MLP_CHUNK, (c + 1) * MLP_CHUNK)
        u = jnp.maximum(_dot(h2, wup_ref[0, :, cols]), 0.0)
        d = _dot((u * u).astype(BF16), wdn_ref[0, cols, :])
        acc = d if acc is None else acc + d
    x2 = x1 + ga2 * acc
    if final_norm:
        x2 = _rms(x2, gfin_ref[...])
    o_ref[...] = x2


def _out_mlp(x2d, seq, yf, hm, mo, ya, layer, wts, mod, mod_row0, per_batch_mod, gfin, tm, final_norm):
    t, d = x2d.shape
    tiles_per_seq = seq // tm
    tok = lambda w: pl.BlockSpec((tm, w), lambda i: (i, 0))
    yf_spec = pl.BlockSpec((F_WIDTH // LANES, tm, LANES), lambda i: (i // tiles_per_seq, i % tiles_per_seq, 0))
    names = ("gm", "g2", "wof", "wom", "woa", "wup", "wdn")
    lay = {name: _layer_spec(wts[name], layer) for name in names}
    return pl.pallas_call(
        functools.partial(_out_mlp_kernel, final_norm=final_norm),
        grid=(t // tm,),
        in_specs=[
            tok(d), yf_spec, tok(MP_WIDTH), tok(MP_WIDTH), tok(AP_WIDTH),
            pl.BlockSpec((1, 6, d), _mod_map(mod_row0, per_batch_mod, tiles_per_seq)),
            lay["gm"], lay["g2"], _const_spec((1, d)),
            lay["wof"], lay["wom"], lay["woa"], lay["wup"], lay["wdn"],
        ],
        out_specs=tok(d),
        out_shape=jax.ShapeDtypeStruct((t, d), F32),
        compiler_params=_cparams(("arbitrary",)),
        name="out_mlp",
    )(x2d, yf, hm, mo, ya, mod, wts["gm"], wts["g2"], gfin,
      wts["wof"], wts["wom"], wts["woa"], wts["wup"], wts["wdn"])


def _dft_tables(n):
    idx = (np.arange(n, dtype=np.int64)[:, None] * np.arange(n, dtype=np.int64)[None, :]) % n
    ang = 2.0 * np.pi * idx.astype(np.float64) / n
    scale = 1.0 / np.sqrt(n)
    return np.cos(ang) * scale, np.sin(ang) * scale


def _channel_dft():
    c, s = _dft_tables(F_GROUP_DIM)
    eye = np.eye(F_GROUPS)
    return (jnp.asarray(np.kron(eye, c), dtype=F32).astype(BF16),
            jnp.asarray(np.kron(eye, s), dtype=F32).astype(BF16))


def _position_dft(n):
    c, s = _dft_tables(n)
    return jnp.asarray(c, dtype=F32).astype(BF16), jnp.asarray(s, dtype=F32).astype(BF16)


def _rope_tables(n, rotate):
    cos = np.zeros((n, HEAD_PAD), np.float32)
    sin = np.zeros((n, HEAD_PAD), np.float32)
    cos[:, :A_NOPE + A_ROPE] = 1.0
    if rotate:
        nf = A_ROPE // 4
        t = np.arange(n)
        row = (t // GRID_W).astype(np.float32)
        col = (t % GRID_W).astype(np.float32)
        freqs = (np.float32(ROPE_THETA) ** (-np.arange(nf, dtype=np.float32) / np.float32(nf))).astype(np.float32)
        for seg, pos in enumerate((row, col)):
            ang = pos[:, None] * freqs[None, :]
            c, s = np.cos(ang), np.sin(ang)
            base = A_NOPE + seg * 2 * nf
            cos[:, base:base + nf] = c
            cos[:, base + nf:base + 2 * nf] = c
            sin[:, base:base + nf] = -s
            sin[:, base + nf:base + 2 * nf] = s
    return jnp.asarray(cos), jnp.asarray(sin)


def _pad_heads_cols(w, heads, width):
    lead = w.shape[:-1]
    w = w.reshape(lead + (heads, width))
    w = jnp.pad(w, [(0, 0)] * len(lead) + [(0, 0), (0, HEAD_PAD - width)])
    return w.reshape(lead + (heads * HEAD_PAD,))


def _pad_heads_rows(w, heads, width):
    depth, _, n = w.shape
    w = jnp.pad(w.reshape(depth, heads, width, n), [(0, 0), (0, 0), (0, HEAD_PAD - width), (0, 0)])
    return w.reshape(depth, heads * HEAD_PAD, n)


GATE_I_COLS = np.concatenate([np.arange(M_HEADS), 2 * M_HEADS + np.arange(M_HEADS)])
GATE_F_COLS = GATE_I_COLS + M_HEADS


def _prepare_weights(max_chunks, g_norm1, g_norm2, w_in, b_gates, conv_qk, g_mlstm, g_q_norm, g_kv_norm,
                     w_uq, w_ukv, w_out, w_up, w_down):
    offs = np.cumsum([0, F_WIDTH, M_WIDTH, M_WIDTH, M_WIDTH, M_WIDTH, 4 * M_HEADS, Q_LORA, KV_LORA, A_ROPE])
    part = lambda i: w_in[:, :, offs[i]:offs[i + 1]]
    heads = lambda w: _pad_heads_cols(w, M_HEADS, M_HEAD_DIM)
    w_in_p = jnp.concatenate([
        part(0), part(1), part(2), heads(part(4)), part(6), part(7),
        jnp.pad(part(8), [(0, 0), (0, 0), (A_NOPE, LANES - A_NOPE - A_ROPE)]),
    ], axis=2).astype(BF16)
    assert w_in_p.shape[2] == IN_PAD
    gates = part(5)
    w_vt = jnp.concatenate([part(3), gates[:, :, GATE_I_COLS], gates[:, :, GATE_F_COLS]],
                           axis=2).astype(BF16).transpose(0, 2, 1)

    conv_p = jnp.pad(conv_qk, [(0, 0), (0, 8 - K_CONV), (0, 0)])

    tile_rows = lambda b: jnp.tile(b[:, :, None], (1, max_chunks, 1))
    ukv = w_ukv.reshape(w_ukv.shape[0], KV_LORA, A_HEADS, A_NOPE + A_V)
    pad_kv = lambda w: jnp.pad(w, [(0, 0), (0, 0), (0, 0), (0, HEAD_PAD - w.shape[-1])]).reshape(
        w.shape[0], KV_LORA, AP_WIDTH).astype(BF16)
    vec = lambda g: g[:, None, :]
    return dict(
        g1=vec(g_norm1), g2=vec(g_norm2), gq=vec(g_q_norm), gkv=vec(g_kv_norm),
        w_in_p=w_in_p, w_vt=w_vt, conv_p=conv_p,
        bir=tile_rows(b_gates[:, GATE_I_COLS]), bfr=tile_rows(b_gates[:, GATE_F_COLS]),
        gm=_pad_heads_cols(g_mlstm, M_HEADS, M_HEAD_DIM)[:, None, :],
        wq=_pad_heads_cols(w_uq, A_HEADS, A_NOPE + A_ROPE).astype(BF16),
        wk=pad_kv(ukv[..., :A_NOPE]), wv=pad_kv(ukv[..., A_NOPE:]),
        wof=w_out[:, :F_WIDTH].astype(BF16),
        wom=_pad_heads_rows(w_out[:, F_WIDTH:F_WIDTH + M_WIDTH], M_HEADS, M_HEAD_DIM).astype(BF16),
        woa=_pad_heads_rows(w_out[:, F_WIDTH + M_WIDTH:], A_HEADS, A_V).astype(BF16),
        wup=w_up.astype(BF16), wdn=w_down.astype(BF16),
    )


def _stream_selectors():
    sel = np.zeros((SEL_ROWS, N_STREAM, MCH), np.float32)
    for s in range(N_STREAM):
        for part in range(3):
            sel[part * N_STREAM + s, s, :] = 1.0
    return jnp.asarray(sel.reshape(SEL_ROWS, N_STREAM * MCH), dtype=BF16)


def kernel(x, c, ctx, c_ctx, w_mod, b_mod, g_norm1, g_norm2, w_in, b_gates, conv_qk, g_mlstm,
           g_q_norm, g_kv_norm, w_uq, w_ukv, w_out, w_up, w_down, g_final):
    nb, seq, d = x.shape
    nctx = ctx.shape[1]
    depth = w_mod.shape[0]
    assert d == D_MODEL and seq % 256 == 0 and nctx % MCH == 0
    sel = _stream_selectors()

    tm = min(512, seq)
    tm_mlp = min(512, seq)
    tm_ctx = min(256, nctx)
    tq = min(8 * ATTN_SUB, seq)
    tq_ctx = min(ATTN_SUB, nctx)

    dft_cc, dft_cs = _channel_dft()
    fft_lat = _fourier4_tables(seq)
    dft_ctx = _position_dft(nctx)
    rope_lat = _rope_tables(seq, True)
    rope_ctx = _rope_tables(nctx, False)
    wts = _prepare_weights(max(seq, nctx) // MCH, g_norm1, g_norm2, w_in, b_gates, conv_qk, g_mlstm,
                           g_q_norm, g_kv_norm, w_uq, w_ukv, w_out, w_up, w_down)
    gfin = g_final.reshape(1, d)

    rows = ((nb + 1 + 7) // 8) * 8
    cc = jnp.concatenate([c, c_ctx[None, :], jnp.zeros((rows - nb - 1, d), F32)], axis=0)
    mod_all = _modulation(cc, w_mod, b_mod).reshape(depth * rows, 6, d)

    xl = x.reshape(nb * seq, d)
    xc = ctx.reshape(nb * nctx, d)

    for l in range(depth):
        last = l == depth - 1
        row_lat, row_ctx = l * rows, l * rows + nb

        zc, zs, mqk, vt, mo, gr, q_a, k_a, v_a = _inproj(
            xl, seq, l, wts, mod_all, row_lat, True, dft_cc, dft_cs, *rope_lat, tm, True)
        zc_c, zs_c, mqk_c, vt_c, mo_c, gr_c, q_ac, k_ac, v_ac = _inproj(
            xc, nctx, l, wts, mod_all, row_ctx, False, dft_cc, dft_cs, *rope_ctx, tm_ctx, not last)

        yf = _fourier4(fft_lat, zc, zs)

        def mlstm_inputs(mqk_s, vt_s, gr_s, n):
            q_s, k_s = _conv_silu(mqk_s.reshape(nb, n, 2 * M_WIDTH), l, wts)
            return (q_s, k_s, vt_s, gr_s)

        hm, hm_c = _mlstm(mlstm_inputs(mqk, vt, gr, seq), mlstm_inputs(mqk_c, vt_c, gr_c, nctx), l, wts, sel)

        b3 = lambda a, n: a.reshape(nb, n, AP_WIDTH)
        keys_ctx = (b3(k_ac, nctx), b3(v_ac, nctx))
        ya = _attention(b3(q_a, seq), [(b3(k_a, seq), b3(v_a, seq)), keys_ctx], tq, ATTN_HEADS_PER_STEP)

        xl = _out_mlp(xl, seq, yf, hm.reshape(nb * seq, MP_WIDTH), mo, ya.reshape(nb * seq, AP_WIDTH),
                      l, wts, mod_all, row_lat, True, gfin, tm_mlp, last)

        if not last:
            yf_c = _fourier(*dft_ctx, zc_c, zs_c)
            ya_c = _attention(b3(q_ac, nctx), [keys_ctx], tq_ctx, A_HEADS)
            xc = _out_mlp(xc, nctx, yf_c, hm_c.reshape(nb * nctx, MP_WIDTH), mo_c,
                          ya_c.reshape(nb * nctx, AP_WIDTH), l, wts, mod_all, row_ctx, False, gfin, tm_ctx, False)

    return xl.reshape(nb, seq, d)
```

```python
import functools

import numpy as np
import jax
import jax.numpy as jnp
from jax import lax
from jax.experimental import pallas as pl
from jax.experimental.pallas import tpu as pltpu

D_MODEL = 1024
GRID_W = 64
EPS = 1e-6
F_GROUPS = 4
F_GROUP_DIM = D_MODEL // 16
F_WIDTH = F_GROUPS * F_GROUP_DIM
M_HEADS = 4
M_HEAD_DIM = 3 * D_MODEL // 32
M_WIDTH = M_HEADS * M_HEAD_DIM
K_CONV = 5
A_HEADS = 4
A_NOPE = 64
A_ROPE = 32
A_V = 3 * D_MODEL // 32
Q_LORA = D_MODEL // 4
KV_LORA = D_MODEL // 8
ROPE_THETA = 10000.0
MLP_HIDDEN = 4 * D_MODEL

LANES = 128
HEAD_PAD = 128
MP_WIDTH = M_HEADS * HEAD_PAD
AP_WIDTH = A_HEADS * HEAD_PAD
VMEM_LIMIT = 56 * 1024 * 1024
MCH = 128
N_STREAM = 2 * M_HEADS

OFF_PF = 0
OFF_MQ = OFF_PF + F_WIDTH
OFF_MO = OFF_MQ + 2 * M_WIDTH
OFF_CQ = OFF_MO + MP_WIDTH
OFF_CKV = OFF_CQ + Q_LORA
OFF_KR = OFF_CKV + KV_LORA
IN_PAD = OFF_KR + LANES

BF16 = jnp.bfloat16
F32 = jnp.float32
LOG2_E = 1.4426950408889634


def _cparams(sem):
    return pltpu.CompilerParams(dimension_semantics=sem, vmem_limit_bytes=VMEM_LIMIT)


def _const_spec(shape):
    nd = len(shape)
    return pl.BlockSpec(shape, lambda *_: (0,) * nd, pipeline_mode=pl.Buffered(1))


def _layer_spec(arr, layer):
    nd = arr.ndim
    return pl.BlockSpec((1,) + arr.shape[1:], lambda *_: (layer,) + (0,) * (nd - 1), pipeline_mode=pl.Buffered(1))


def _split3(a):
    hi = a.astype(BF16)
    r1 = a - hi.astype(F32)
    mid = r1.astype(BF16)
    lo = (r1 - mid.astype(F32)).astype(BF16)
    return hi, mid, lo


def _dot(a, b):
    return jnp.dot(a, b, preferred_element_type=F32)


def _dot_nt(a, b):
    return lax.dot_general(a, b, (((1,), (1,)), ((), ())), preferred_element_type=F32)


def _dot_tn(a, b):
    return lax.dot_general(a, b, (((0,), (0,)), ((), ())), preferred_element_type=F32)


def _rms(x, g):
    return x * lax.rsqrt(jnp.mean(x * x, axis=-1, keepdims=True) + EPS) * g


def _mod_kernel(c_ref, w_ref, b_ref, o_ref):
    c = c_ref[...]
    a = c * jax.nn.sigmoid(c)
    a_hi = a.astype(BF16)
    a_lo = (a - a_hi.astype(F32)).astype(BF16)
    w = w_ref[0]
    w_hi = w.astype(BF16)
    w_lo = (w - w_hi.astype(F32)).astype(BF16)
    acc = _dot(a_hi, w_hi) + _dot(a_hi, w_lo) + _dot(a_lo, w_hi)
    o_ref[0] = acc + b_ref[0]


def _modulation(cc, w_mod, b_mod):
    depth, d, n = w_mod.shape
    rows = cc.shape[0]
    tn = 1536
    return pl.pallas_call(
        _mod_kernel,
        grid=(depth, n // tn),
        in_specs=[
            pl.BlockSpec((rows, d), lambda l, j: (0, 0)),
            pl.BlockSpec((1, d, tn), lambda l, j: (l, 0, j)),
            pl.BlockSpec((1, 1, tn), lambda l, j: (l, 0, j)),
        ],
        out_specs=pl.BlockSpec((1, rows, tn), lambda l, j: (l, 0, j)),
        out_shape=jax.ShapeDtypeStruct((depth, rows, n), F32),
        compiler_params=_cparams(("arbitrary", "arbitrary")),
        name="modulation",
    )(cc, w_mod, b_mod.reshape(depth, 1, n))


def _rope(x, cos, sin, first_half):
    half = A_ROPE // 4
    partner = jnp.where(first_half, pltpu.roll(x, LANES - half, 1), pltpu.roll(x, half, 1))
    return x * cos + partner * sin


def _inproj_kernel(x_ref, g_ref, mod_ref, w_ref, wvt_ref, cc_ref, cs_ref,
                   cos_ref, sin_ref, gq_ref, gkv_ref, wq_ref, wk_ref, wv_ref,
                   zc_ref, zs_ref, mqk_ref, vt_ref, mo_ref, gr_ref, qa_ref, ka_ref, va_ref, *, with_q):
    x = x_ref[...]
    mod = mod_ref[0]
    h = _rms(x, g_ref[0]) * (1.0 + mod[1:2]) + mod[0:1]
    hb = h.astype(BF16)

    def proj(off, width):
        return _dot(hb, w_ref[0, :, off:off + width])

    ckv_kr = proj(OFF_CKV, 2 * LANES)
    cq = proj(OFF_CQ, Q_LORA) if with_q else None
    pf = proj(OFF_PF, F_WIDTH).astype(BF16) if with_q else None

    mqk_ref[...] = proj(OFF_MQ, 2 * M_WIDTH)

    cos = cos_ref[...]
    sin = sin_ref[...]
    lane = lax.broadcasted_iota(jnp.int32, cos.shape, 1)
    first_half = ((lane - A_NOPE) & (A_ROPE // 2 - 1)) < A_ROPE // 4
    kvn = _rms(ckv_kr[:, :KV_LORA], gkv_ref[0]).astype(BF16)
    k_rope = _rope(ckv_kr[:, KV_LORA:], cos, sin, first_half)
    k_nope = _dot(kvn, wk_ref[0])
    va = _dot(kvn, wv_ref[0])
    head_lane = lax.broadcasted_iota(jnp.int32, va.shape, 1) & (HEAD_PAD - 1)
    va_ref[...] = jnp.where(head_lane == A_V, 1.0, va).astype(BF16)
    for hd in range(A_HEADS):
        lanes = slice(hd * HEAD_PAD, (hd + 1) * HEAD_PAD)
        ka_ref[:, lanes] = (k_nope[:, lanes] + k_rope).astype(BF16)
    if with_q:
        qn = _rms(cq, gq_ref[0]).astype(BF16)
        q_all = _dot(qn, wq_ref[0])
        q_raw = [q_all[:, hd * HEAD_PAD:(hd + 1) * HEAD_PAD] for hd in range(A_HEADS)]
        zc_ref[...] = _dot(pf, cc_ref[...]).astype(BF16)
        zs_ref[...] = _dot(pf, cs_ref[...]).astype(BF16)

    vg = _dot_nt(wvt_ref[0], hb)
    vt = vg[:M_WIDTH].astype(BF16)
    for j in range(vt_ref.shape[0]):
        vt_ref[j] = vt[:, j * MCH:(j + 1) * MCH]
        gr_ref[j] = vg[M_WIDTH:, j * MCH:(j + 1) * MCH]

    if with_q:
        mo_ref[...] = proj(OFF_MO, MP_WIDTH)
        scale = (A_NOPE + A_ROPE) ** -0.5 * LOG2_E
        for hd in range(A_HEADS):
            q = _rope(q_raw[hd], cos, sin, first_half)
            qa_ref[:, hd * HEAD_PAD:(hd + 1) * HEAD_PAD] = (q * scale).astype(BF16)
    else:
        for ref in (zc_ref, zs_ref, mo_ref, qa_ref):
            ref[...] = jnp.zeros_like(ref)


def _mod_map(mod_row0, per_batch_mod, tiles_per_seq):
    if per_batch_mod:
        return lambda i: (mod_row0 + i // tiles_per_seq, 0, 0)
    return lambda i: (mod_row0, 0, 0)


def _inproj(x2d, seq, layer, wts, mod, mod_row0, per_batch_mod, dft_cc, dft_cs, cos, sin, tm, with_q):
    t, d = x2d.shape
    nb = t // seq
    tiles_per_seq = seq // tm
    tok = lambda w: pl.BlockSpec((tm, w), lambda i: (i, 0))
    z_spec = pl.BlockSpec((tm, F_WIDTH), lambda i: (i % tiles_per_seq, i // tiles_per_seq))
    pos = pl.BlockSpec((tm, LANES), lambda i: (i % tiles_per_seq, 0))
    heads_bf16 = jax.ShapeDtypeStruct((t, AP_WIDTH), BF16)
    shapes = [
        jax.ShapeDtypeStruct((seq, nb * F_WIDTH), BF16),
        jax.ShapeDtypeStruct((seq, nb * F_WIDTH), BF16),
        jax.ShapeDtypeStruct((t, 2 * M_WIDTH), F32),
        jax.ShapeDtypeStruct((t // MCH, M_WIDTH, MCH), BF16),
        jax.ShapeDtypeStruct((t, MP_WIDTH), F32),
        jax.ShapeDtypeStruct((t // MCH, 2 * N_STREAM, MCH), F32),
        heads_bf16, heads_bf16, heads_bf16,
    ]
    vt_spec = pl.BlockSpec((tm // MCH, M_WIDTH, MCH), lambda i: (i, 0, 0))
    gr_spec = pl.BlockSpec((tm // MCH, 2 * N_STREAM, MCH), lambda i: (i, 0, 0))
    out_specs = [z_spec, z_spec, tok(2 * M_WIDTH), vt_spec, tok(MP_WIDTH), gr_spec,
                 tok(AP_WIDTH), tok(AP_WIDTH), tok(AP_WIDTH)]
    lay = lambda name: _layer_spec(wts[name], layer)
    return pl.pallas_call(
        functools.partial(_inproj_kernel, with_q=with_q),
        grid=(t // tm,),
        in_specs=[
            tok(d),
            lay("g1"),
            pl.BlockSpec((1, 6, d), _mod_map(mod_row0, per_batch_mod, tiles_per_seq)),
            lay("w_in_p"), lay("w_vt"),
            _const_spec((F_WIDTH, F_WIDTH)),
            _const_spec((F_WIDTH, F_WIDTH)),
            pos, pos,
            lay("gq"), lay("gkv"), lay("wq"), lay("wk"), lay("wv"),
        ],
        out_specs=out_specs,
        out_shape=shapes,
        compiler_params=_cparams(("arbitrary",)),
        name="inproj",
    )(x2d, wts["g1"], mod, wts["w_in_p"], wts["w_vt"], dft_cc, dft_cs, cos, sin,
      wts["gq"], wts["gkv"], wts["wq"], wts["wk"], wts["wv"])


def _fourier_kernel(c_ref, s_ref, zc_ref, zs_ref, o_ref):
    y = _dot(c_ref[...], zc_ref[...]) - _dot(s_ref[...], zs_ref[...])
    for slab in range(o_ref.shape[0]):
        o_ref[slab] = y[:, slab * LANES:(slab + 1) * LANES]


def _fourier(dft_c, dft_s, zc, zs):
    n, cols = zc.shape
    tr = min(n, 512)
    tc = min(cols, 512)
    return pl.pallas_call(
        _fourier_kernel,
        grid=(n // tr, cols // tc),
        in_specs=[
            pl.BlockSpec((tr, n), lambda i, j: (i, 0)),
            pl.BlockSpec((tr, n), lambda i, j: (i, 0)),
            pl.BlockSpec((n, tc), lambda i, j: (0, j)),
            pl.BlockSpec((n, tc), lambda i, j: (0, j)),
        ],
        out_specs=pl.BlockSpec((tc // LANES, tr, LANES), lambda i, j: (j, i, 0)),
        out_shape=jax.ShapeDtypeStruct((cols // LANES, n, LANES), F32),
        compiler_params=_cparams(("arbitrary", "arbitrary")),
        name="fourier",
    )(dft_c, dft_s, zc, zs)


FFT_COLS = 256


def _fourier4_kernel(tab_ref, twc_ref, tws_ref, zc_ref, zs_ref, o_ref):
    m = zc_ref.shape[0] // 4
    reps = zc_ref.shape[1] // LANES
    c0, c1, c2, c3 = (zc_ref[j * m:(j + 1) * m, :].astype(F32) for j in range(4))
    s0, s1, s2, s3 = (zs_ref[j * m:(j + 1) * m, :].astype(F32) for j in range(4))
    ce, co, cd, cu = c0 + c2, c1 + c3, c0 - c2, c1 - c3
    se, so, sd, su = s0 + s2, s1 + s3, s0 - s2, s1 - s3

    def emit(br, bi, k):
        if k:
            cos = jnp.concatenate([twc_ref[k - 1]] * reps, axis=1)
            sin = jnp.concatenate([tws_ref[k - 1]] * reps, axis=1)
            br, bi = br * cos + bi * sin, bi * cos - br * sin
        stacked = jnp.concatenate([br.astype(BF16), bi.astype(BF16)], axis=0)
        y = _dot(tab_ref[...], stacked)
        for slab in range(reps):
            o_ref[slab, pl.ds(k, m, stride=4), :] = y[:, slab * LANES:(slab + 1) * LANES]

    emit(ce + co, -(se + so), 0)
    emit(cd - su, -sd - cu, 1)
    emit(ce - co, so - se, 2)
    emit(cd + su, cu - sd, 3)


def _fourier4(tables, zc, zs):
    tab, twc, tws = tables
    n, cols = zc.shape
    m = n // 4
    tc = min(cols, FFT_COLS)
    return pl.pallas_call(
        _fourier4_kernel,
        grid=(cols // tc,),
        in_specs=[
            _const_spec((m, 2 * m)),
            _const_spec((3, m, LANES)),
            _const_spec((3, m, LANES)),
            pl.BlockSpec((n, tc), lambda j: (0, j)),
            pl.BlockSpec((n, tc), lambda j: (0, j)),
        ],
        out_specs=pl.BlockSpec((tc // LANES, n, LANES), lambda j: (j, 0, 0)),
        out_shape=jax.ShapeDtypeStruct((cols // LANES, n, LANES), F32),
        compiler_params=_cparams(("arbitrary",)),
        name="fourier4",
    )(tab, twc, tws, zc, zs)


def _fourier4_tables(n):
    m = n // 4
    idx = (np.arange(m, dtype=np.int64)[:, None] * np.arange(m, dtype=np.int64)[None, :]) % m
    ang = 2.0 * np.pi * idx.astype(np.float64) / m
    tab = np.concatenate([np.cos(ang), np.sin(ang)], axis=1) / np.sqrt(n)
    theta = 2.0 * np.pi * np.arange(m, dtype=np.float64)[None, :] * np.arange(1, 4, dtype=np.float64)[:, None] / n
    bcast = lambda t: jnp.asarray(np.repeat(t[:, :, None], LANES, axis=2), dtype=F32)
    return jnp.asarray(tab, dtype=F32).astype(BF16), bcast(np.cos(theta)), bcast(np.sin(theta))


CONV_ROWS = 256
CONV_HALO = 8


def _head_tile(x, start, lane):
    t, o = divmod(start, LANES)
    tile = lambda i: x[:, i * LANES:(i + 1) * LANES]
    if o == 0:
        return tile(t)
    shift = LANES - o
    out = pltpu.roll(tile(t), shift, 1)
    if shift < M_HEAD_DIM:
        out = jnp.where(lane < shift, out, pltpu.roll(tile(t + 1), shift, 1))
    return out


def _conv_kernel(u_ref, w_ref, q_ref, k_ref, pad_ref):
    n = u_ref.shape[1]
    width = u_ref.shape[2]
    zeros = jnp.zeros((CONV_HALO, width), F32)
    pad_ref[0:CONV_HALO, :] = zeros
    pad_ref[CONV_HALO + n:2 * CONV_HALO + n, :] = zeros
    pad_ref[CONV_HALO:CONV_HALO + n, :] = u_ref[0]
    w = w_ref[0]
    rows = min(CONV_ROWS, n)
    span = rows + 2 * CONV_HALO
    for r in range(n // rows):
        block = pad_ref[r * rows:r * rows + span, :]
        acc = None
        for j in range(K_CONV):
            shift = (K_CONV // 2 - j) % span
            tap = block if shift == 0 else pltpu.roll(block, shift, 0)
            term = tap[CONV_HALO:CONV_HALO + rows, :] * w[j:j + 1, :]
            acc = term if acc is None else acc + term
        act = acc * jax.nn.sigmoid(acc)
        lane = lax.broadcasted_iota(jnp.int32, (rows, LANES), 1)
        for hd in range(M_HEADS):
            lanes = slice(hd * HEAD_PAD, (hd + 1) * HEAD_PAD)
            q_tile = _head_tile(act, hd * M_HEAD_DIM, lane)
            k_tile = _head_tile(act, M_WIDTH + hd * M_HEAD_DIM, lane)
            q_ref[0, r * rows:(r + 1) * rows, lanes] = (q_tile * (M_HEAD_DIM ** -0.5)).astype(BF16)
            k_ref[0, r * rows:(r + 1) * rows, lanes] = jnp.where(lane < M_HEAD_DIM, k_tile, 0.0).astype(BF16)


def _conv_silu(mqk, layer, wts):
    nb, n, width = mqk.shape
    return pl.pallas_call(
        _conv_kernel,
        grid=(nb,),
        in_specs=[
            pl.BlockSpec((1, n, width), lambda b: (b, 0, 0)),
            _layer_spec(wts["conv_p"], layer),
        ],
        out_specs=[
            pl.BlockSpec((1, n, MP_WIDTH), lambda b: (b, 0, 0)),
            pl.BlockSpec((1, n, MP_WIDTH), lambda b: (b, 0, 0)),
        ],
        out_shape=[
            jax.ShapeDtypeStruct((nb, n, MP_WIDTH), BF16),
            jax.ShapeDtypeStruct((nb, n, MP_WIDTH), BF16),
        ],
        scratch_shapes=[pltpu.VMEM((n + 2 * CONV_HALO, width), F32)],
        compiler_params=_cparams(("arbitrary",)),
        name="conv_silu",
    )(mqk, wts["conv_p"])


def _log_sigmoid(x):
    return jnp.minimum(x, 0.0) - jnp.log(1.0 + jnp.exp(-jnp.abs(x)))


def _exact_dot_01(a, tri_bf16, a_on_left):
    out = None
    for term in _split3(a):
        d = _dot(term, tri_bf16) if a_on_left else _dot(tri_bf16, term)
        out = d if out is None else out + d
    return out


ONE_ROW = M_HEAD_DIM
(F_R, F_B, F_CM, F_TOT, F_CML, F_MP, F_A, F_WI, F_ELD, F_DEC, F_WK, F_HI, F_MID, F_LO) = range(14)
N_FIELDS = 14
STATE_GROUP = 4
OUT_GROUP = 8
SEL_ROWS = 32


def _mlstm_kernel(ql_ref, kl_ref, vtl_ref, grl_ref,
                  qc_ref, kc_ref, vtc_ref, grc_ref,
                  bir_ref, bfr_ref, sel_ref,
                  hl_ref, hc_ref,
                  ct_st, st_sc, rows_sc, *, ctx_out):
    L = MCH
    nh = M_HEADS
    ncc = qc_ref.shape[1] // L
    ncl = ql_ref.shape[1] // L

    d0 = lax.broadcasted_iota(jnp.int32, (L, L), 0)
    d1 = lax.broadcasted_iota(jnp.int32, (L, L), 1)
    le = d0 <= d1
    ge = d0 >= d1
    tri_le = le.astype(BF16)
    fwd_rows = lax.broadcasted_iota(jnp.int32, (N_STREAM, L), 0) < nh
    feat = lax.broadcasted_iota(jnp.int32, (HEAD_PAD, L), 0)
    keep_rows = feat < M_HEAD_DIM
    ones_tail = (lax.broadcasted_iota(jnp.int32, (HEAD_PAD - M_HEAD_DIM, L), 0) == 0).astype(F32)

    ct_st[...] = jnp.zeros_like(ct_st)

    def field(f, slot0, nc):
        return rows_sc[f, slot0:slot0 + nc].reshape(nc * N_STREAM, L)

    def set_field(f, slot0, nc, val):
        rows_sc[f, slot0:slot0 + nc] = val.reshape(nc, N_STREAM, L)

    def gate_pass(gr_ref, nc, slot0):
        n8 = nc * N_STREAM
        fwd = (lax.broadcasted_iota(jnp.int32, (n8, L), 0) & (N_STREAM - 1)) < nh
        lane = lax.broadcasted_iota(jnp.int32, (n8, L), 1)
        gi = gr_ref[:, 0:N_STREAM, :].reshape(n8, L) + bir_ref[0, 0:n8, :]
        f = _log_sigmoid(gr_ref[:, N_STREAM:2 * N_STREAM, :].reshape(n8, L) + bfr_ref[0, 0:n8, :])
        pre = _exact_dot_01(f, tri_le, a_on_left=True)
        total = jnp.sum(f, axis=1, keepdims=True)
        b = jnp.where(fwd, pre, total - pre + f)
        r = gi - b
        cm = r
        sh = 1
        while sh < L:
            from_left = jnp.where(lane >= sh, pltpu.roll(cm, sh, 1), -jnp.inf)
            from_right = jnp.where(lane < L - sh, pltpu.roll(cm, L - sh, 1), -jnp.inf)
            cm = jnp.maximum(cm, jnp.where(fwd, from_left, from_right))
            sh *= 2
        set_field(F_R, slot0, nc, r)
        set_field(F_B, slot0, nc, b)
        set_field(F_CM, slot0, nc, cm)
        set_field(F_TOT, slot0, nc, jnp.broadcast_to(total, (n8, L)))
        set_field(F_CML, slot0, nc, jnp.broadcast_to(jnp.max(r, axis=1, keepdims=True), (n8, L)))

    def m_scan(nc, slot0, m0):
        def step(j, m):
            sf = slot0 + j
            sb = slot0 + nc - 1 - j
            tot = jnp.where(fwd_rows, rows_sc[F_TOT, sf], rows_sc[F_TOT, sb])
            cml = jnp.where(fwd_rows, rows_sc[F_CML, sf], rows_sc[F_CML, sb])
            rows_sc[F_MP, sf, 0:nh, :] = m[0:nh]
            rows_sc[F_MP, sb, nh:N_STREAM, :] = m[nh:N_STREAM]
            return tot + jnp.maximum(m, cml)

        return lax.fori_loop(0, nc, step, m0)

    def weight_pass(nc, slot0):
        mp, cm, b, r = (field(f, slot0, nc) for f in (F_MP, F_CM, F_B, F_R))
        big = jnp.maximum(mp, field(F_CML, slot0, nc))
        a = -jnp.maximum(mp, cm)
        set_field(F_A, slot0, nc, a)
        set_field(F_WI, slot0, nc, jnp.exp(mp + a))
        set_field(F_ELD, slot0, nc, jnp.exp(a - b))
        set_field(F_DEC, slot0, nc, jnp.exp(mp - big))
        set_field(F_WK, slot0, nc, jnp.exp(r - big))
        for f, term in zip((F_HI, F_MID, F_LO), _split3(r)):
            set_field(f, slot0, nc, term.astype(F32))

    def value_slab(vt_ref, c, hd):
        vt = vt_ref[c, hd * M_HEAD_DIM:(hd + 1) * M_HEAD_DIM, :].astype(F32)
        return jnp.concatenate([vt, ones_tail], axis=0)

    def state_pass(k_ref, vt_ref, nc, slot0):
        group = min(STATE_GROUP, nc)

        def updates_of(j):
            uts = []
            for sidx in range(N_STREAM):
                hd = sidx % nh
                c = j if sidx < nh else nc - 1 - j
                wk = rows_sc[F_WK, slot0 + c, sidx:sidx + 1, :]
                kk = k_ref[0, pl.ds(pl.multiple_of(c * L, L), L), hd * HEAD_PAD:(hd + 1) * HEAD_PAD]
                uts.append(_dot((value_slab(vt_ref, c, hd) * wk).astype(BF16), kk))
            return uts

        def step(g, carry):
            all_uts = [updates_of(g * group + u) for u in range(group)]
            for u in range(group):
                j = g * group + u
                for sidx in range(N_STREAM):
                    slot = slot0 + (j if sidx < nh else nc - 1 - j)
                    prev = ct_st[sidx]
                    st_sc[sidx, slot] = prev.astype(BF16)
                    ct_st[sidx] = rows_sc[F_DEC, slot, sidx:sidx + 1, :] * prev + all_uts[u][sidx]
            return carry

        lax.fori_loop(0, nc // group, step, 0)

    def output_pass(q_ref, k_ref, vt_ref, out_ref, nc, slot0):
        head_lanes = [slice(hd * HEAD_PAD, (hd + 1) * HEAD_PAD) for hd in range(nh)]

        def independent_matmuls(c):
            rows = pl.ds(pl.multiple_of(c * L, L), L)
            slot = slot0 + c
            r3 = jnp.concatenate([rows_sc[F_HI, slot], rows_sc[F_MID, slot], rows_sc[F_LO, slot],
                                  jnp.zeros((N_STREAM, L), F32)], axis=0).astype(BF16)
            qs = [q_ref[0, rows, lanes] for lanes in head_lanes]
            s_ts = [_dot_nt(k_ref[0, rows, lanes], q) for lanes, q in zip(head_lanes, qs)]
            inters = [_dot_nt(st_sc[sidx, slot], qs[sidx % nh]) for sidx in range(N_STREAM)]
            r_all = _dot_tn(r3, sel_ref[...])
            r_ts = [r_all[:, sidx * L:(sidx + 1) * L] for sidx in range(N_STREAM)]
            return s_ts, inters, r_ts

        def finish(c, s_ts, inters, r_ts):
            rows = pl.ds(pl.multiple_of(c * L, L), L)
            slot = slot0 + c
            a_rows = rows_sc[F_A, slot]
            wi_rows = rows_sc[F_WI, slot]
            eld_rows = rows_sc[F_ELD, slot]
            p_ts = []
            for sidx in range(N_STREAM):
                one = slice(sidx, sidx + 1)
                dm = jnp.where(le if sidx < nh else ge, r_ts[sidx] + a_rows[one, :], -jnp.inf)
                p_ts.append((s_ts[sidx % nh] * jnp.exp(dm)).astype(BF16))
            z_pairs = [_dot(value_slab(vt_ref, c, hd).astype(BF16),
                            jnp.concatenate([p_ts[hd], p_ts[nh + hd]], axis=1)) for hd in range(nh)]
            for hd in range(nh):
                hsum = None
                for dr, sidx in enumerate((hd, nh + hd)):
                    one = slice(sidx, sidx + 1)
                    z_t = z_pairs[hd][:, dr * L:(dr + 1) * L] + inters[sidx] * wi_rows[one, :]
                    den = z_t[ONE_ROW:ONE_ROW + 1, :]
                    h_t = z_t * (1.0 / jnp.maximum(jnp.abs(den), eld_rows[one, :]))
                    hsum = h_t if hsum is None else hsum + h_t
                out_ref[0, rows, head_lanes[hd]] = jnp.where(keep_rows, hsum, 0.0).T

        group = min(OUT_GROUP, nc)

        def step(g, carry):
            ahead = independent_matmuls(g * group)
            for u in range(group):
                cur = ahead
                if u + 1 < group:
                    ahead = independent_matmuls(g * group + u + 1)
                finish(g * group + u, *cur)
            return carry

        lax.fori_loop(0, nc // group, step, 0)

    gate_pass(grc_ref, ncc, 0)
    gate_pass(grl_ref, ncl, ncc)
    m1 = m_scan(ncc, 0, jnp.zeros((N_STREAM, L), F32))
    m_scan(ncl, ncc, m1)
    weight_pass(ncc, 0)
    weight_pass(ncl, ncc)
    state_pass(kc_ref, vtc_ref, ncc, 0)
    state_pass(kl_ref, vtl_ref, ncl, ncc)
    if ctx_out:
        output_pass(qc_ref, kc_ref, vtc_ref, hc_ref, ncc, 0)
    else:
        hc_ref[...] = jnp.zeros_like(hc_ref)
    output_pass(ql_ref, kl_ref, vtl_ref, hl_ref, ncl, ncc)


def _mlstm(lat, ctx, layer, wts, sel, ctx_out):
    nb, n, _ = lat[0].shape
    nctx = ctx[0].shape[1]
    assert MCH == LANES == HEAD_PAD
    nct = (n + nctx) // MCH
    assert wts["bir"].shape[1] >= max(n, nctx) // MCH * N_STREAM

    def specs(rows):
        nc = rows // MCH
        seq = lambda w: pl.BlockSpec((1, rows, w), lambda b: (b, 0, 0))
        chunked = lambda rows_: pl.BlockSpec((nc, rows_, MCH), lambda b: (b, 0, 0))
        return [seq(MP_WIDTH), seq(MP_WIDTH), chunked(M_WIDTH), chunked(2 * N_STREAM)]

    out_spec = lambda rows: pl.BlockSpec((1, rows, MP_WIDTH), lambda b: (b, 0, 0))
    return pl.pallas_call(
        functools.partial(_mlstm_kernel, ctx_out=ctx_out),
        grid=(nb,),
        in_specs=specs(n) + specs(nctx) + [
            _layer_spec(wts["bir"], layer), _layer_spec(wts["bfr"], layer),
            _const_spec((SEL_ROWS, N_STREAM * MCH)),
        ],
        out_specs=[out_spec(n), out_spec(nctx)],
        out_shape=[
            jax.ShapeDtypeStruct((nb, n, MP_WIDTH), F32),
            jax.ShapeDtypeStruct((nb, nctx, MP_WIDTH), F32),
        ],
        scratch_shapes=[
            pltpu.VMEM((N_STREAM, HEAD_PAD, HEAD_PAD), F32),
            pltpu.VMEM((N_STREAM, nct, HEAD_PAD, HEAD_PAD), BF16),
            pltpu.VMEM((N_FIELDS, nct, N_STREAM, MCH), F32),
        ],
        compiler_params=_cparams(("arbitrary",)),
        name="mlstm",
    )(*lat, *ctx, wts["bir"], wts["bfr"], sel)


ATTN_SUB = 256
ATTN_HEADS_PER_STEP = 2


def _attn_kernel(*refs, n_sets):
    q_ref = refs[0]
    kv_refs = refs[1:1 + 2 * n_sets]
    o_ref = refs[1 + 2 * n_sets]
    sub = min(ATTN_SUB, q_ref.shape[1])
    n_sub = q_ref.shape[1] // sub
    items = [(hd, t) for hd in range(q_ref.shape[2] // HEAD_PAD) for t in range(n_sub)]

    def scores_of(item):
        hd, t = item
        lanes = slice(hd * HEAD_PAD, (hd + 1) * HEAD_PAD)
        q = q_ref[0, t * sub:(t + 1) * sub, lanes]
        return [_dot_nt(q, kv_refs[2 * i][0, :, lanes]) for i in range(n_sets)]

    nxt = scores_of(items[0])
    for idx, (hd, t) in enumerate(items):
        rows = slice(t * sub, (t + 1) * sub)
        lanes = slice(hd * HEAD_PAD, (hd + 1) * HEAD_PAD)
        scores = nxt
        if idx + 1 < len(items):
            nxt = scores_of(items[idx + 1])
        m = None
        for s in scores:
            sm = jnp.max(s, axis=-1, keepdims=True)
            m = sm if m is None else jnp.maximum(m, sm)
        acc = None
        for i, s in enumerate(scores):
            p = jnp.exp2(s - m)
            o = _dot(p.astype(BF16), kv_refs[2 * i + 1][0, :, lanes])
            acc = o if acc is None else acc + o
        o_ref[0, rows, lanes] = (acc / acc[:, A_V:A_V + 1]).astype(BF16)


def _attention(q, key_sets, tq, heads_per_step):
    nb, n, _ = q.shape
    n_sets = len(key_sets)
    width = heads_per_step * HEAD_PAD
    in_specs = [pl.BlockSpec((1, tq, width), lambda b, h, i: (b, i, h))]
    args = [q]
    for k, v in key_sets:
        nk = k.shape[1]
        spec = pl.BlockSpec((1, nk, width), lambda b, h, i: (b, 0, h))
        in_specs += [spec, spec]
        args += [k, v]
    return pl.pallas_call(
        functools.partial(_attn_kernel, n_sets=n_sets),
        grid=(nb, A_HEADS // heads_per_step, n // tq),
        in_specs=in_specs,
        out_specs=pl.BlockSpec((1, tq, width), lambda b, h, i: (b, i, h)),
        out_shape=jax.ShapeDtypeStruct((nb, n, AP_WIDTH), BF16),
        compiler_params=_cparams(("arbitrary", "arbitrary", "arbitrary")),
        name="attention",
    )(*args)


MLP_CHUNK = 1024


def _out_mlp_kernel(x_ref, yf_ref, hm_ref, mo_ref, ya_ref, mod_ref,
                    gm_ref, g2_ref, gfin_ref, wof_ref, wom_ref, woa_ref, wup_ref, wdn_ref,
                    o_ref, *, final_norm):
    mod = mod_ref[0]
    ga1, sh2, sc2, ga2 = mod[2:3], mod[3:4], mod[4:5], mod[5:6]
    yf = jnp.concatenate([yf_ref[slab].astype(BF16) for slab in range(yf_ref.shape[0])], axis=1)
    mix = _dot(ya_ref[...], woa_ref[0]) + _dot(yf, wof_ref[0])
    gm = gm_ref[0]
    yms = []
    for hd in range(M_HEADS):
        lanes = slice(hd * HEAD_PAD, (hd + 1) * HEAD_PAD)
        hh = hm_ref[:, lanes]
        ms = jnp.sum(hh * hh, axis=-1, keepdims=True) * (1.0 / M_HEAD_DIM)
        ym = hh * lax.rsqrt(ms + EPS) * gm[:, lanes] * jax.nn.sigmoid(mo_ref[:, lanes])
        yms.append(ym.astype(BF16))
    mix = mix + _dot(jnp.concatenate(yms, axis=1), wom_ref[0])
    x1 = x_ref[...] + ga1 * mix
    h2 = (_rms(x1, g2_ref[0]) * (1.0 + sc2) + sh2).astype(BF16)
    acc = None
    for c in range(wup_ref.shape[2] // MLP_CHUNK):
        cols = slice(c * MLP_CHUNK, (c + 1) * MLP_CHUNK)
        u = jnp.maximum(_dot(h2, wup_ref[0, :, cols]), 0.0)
        d = _dot((u * u).astype(BF16), wdn_ref[0, cols, :])
        acc = d if acc is None else acc + d
    x2 = x1 + ga2 * acc
    if final_norm:
        x2 = _rms(x2, gfin_ref[...])
    o_ref[...] = x2


def _out_mlp(x2d, seq, yf, hm, mo, ya, layer, wts, mod, mod_row0, per_batch_mod, gfin, tm, final_norm):
    t, d = x2d.shape
    tiles_per_seq = seq // tm
    tok = lambda w: pl.BlockSpec((tm, w), lambda i: (i, 0))
    yf_spec = pl.BlockSpec((F_WIDTH // LANES, tm, LANES), lambda i: (i // tiles_per_seq, i % tiles_per_seq, 0))
    names = ("gm", "g2", "wof", "wom", "woa", "wup", "wdn")
    lay = {name: _layer_spec(wts[name], layer) for name in names}
    return pl.pallas_call(
        functools.partial(_out_mlp_kernel, final_norm=final_norm),
        grid=(t // tm,),
        in_specs=[
            tok(d), yf_spec, tok(MP_WIDTH), tok(MP_WIDTH), tok(AP_WIDTH),
            pl.BlockSpec((1, 6, d), _mod_map(mod_row0, per_batch_mod, tiles_per_seq)),
            lay["gm"], lay["g2"], _const_spec((1, d)),
            lay["wof"], lay["wom"], lay["woa"], lay["wup"], lay["wdn"],
        ],
        out_specs=tok(d),
        out_shape=jax.ShapeDtypeStruct((t, d), F32),
        compiler_params=_cparams(("arbitrary",)),
        name="out_mlp",
    )(x2d, yf, hm, mo, ya, mod, wts["gm"], wts["g2"], gfin,
      wts["wof"], wts["wom"], wts["woa"], wts["wup"], wts["wdn"])


def _dft_tables(n):
    idx = (np.arange(n, dtype=np.int64)[:, None] * np.arange(n, dtype=np.int64)[None, :]) % n
    ang = 2.0 * np.pi * idx.astype(np.float64) / n
    scale = 1.0 / np.sqrt(n)
    return np.cos(ang) * scale, np.sin(ang) * scale


def _channel_dft():
    c, s = _dft_tables(F_GROUP_DIM)
    eye = np.eye(F_GROUPS)
    return (jnp.asarray(np.kron(eye, c), dtype=F32).astype(BF16),
            jnp.asarray(np.kron(eye, s), dtype=F32).astype(BF16))


def _position_dft(n):
    c, s = _dft_tables(n)
    return jnp.asarray(c, dtype=F32).astype(BF16), jnp.asarray(s, dtype=F32).astype(BF16)


def _rope_tables(n, rotate):
    cos = np.zeros((n, HEAD_PAD), np.float32)
    sin = np.zeros((n, HEAD_PAD), np.float32)
    cos[:, :A_NOPE + A_ROPE] = 1.0
    if rotate:
        nf = A_ROPE // 4
        t = np.arange(n)
        row = (t // GRID_W).astype(np.float32)
        col = (t % GRID_W).astype(np.float32)
        freqs = (np.float32(ROPE_THETA) ** (-np.arange(nf, dtype=np.float32) / np.float32(nf))).astype(np.float32)
        for seg, pos in enumerate((row, col)):
            ang = pos[:, None] * freqs[None, :]
            c, s = np.cos(ang), np.sin(ang)
            base = A_NOPE + seg * 2 * nf
            cos[:, base:base + nf] = c
            cos[:, base + nf:base + 2 * nf] = c
            sin[:, base:base + nf] = -s
            sin[:, base + nf:base + 2 * nf] = s
    return jnp.asarray(cos), jnp.asarray(sin)


def _pad_heads_cols(w, heads, width):
    lead = w.shape[:-1]
    w = w.reshape(lead + (heads, width))
    w = jnp.pad(w, [(0, 0)] * len(lead) + [(0, 0), (0, HEAD_PAD - width)])
    return w.reshape(lead + (heads * HEAD_PAD,))


def _pad_heads_rows(w, heads, width):
    depth, _, n = w.shape
    w = jnp.pad(w.reshape(depth, heads, width, n), [(0, 0), (0, 0), (0, HEAD_PAD - width), (0, 0)])
    return w.reshape(depth, heads * HEAD_PAD, n)


GATE_I_COLS = np.concatenate([np.arange(M_HEADS), 2 * M_HEADS + np.arange(M_HEADS)])
GATE_F_COLS = GATE_I_COLS + M_HEADS


def _prepare_weights(max_chunks, g_norm1, g_norm2, w_in, b_gates, conv_qk, g_mlstm, g_q_norm, g_kv_norm,
                     w_uq, w_ukv, w_out, w_up, w_down):
    offs = np.cumsum([0, F_WIDTH, M_WIDTH, M_WIDTH, M_WIDTH, M_WIDTH, 4 * M_HEADS, Q_LORA, KV_LORA, A_ROPE])
    part = lambda i: w_in[:, :, offs[i]:offs[i + 1]]
    heads = lambda w: _pad_heads_cols(w, M_HEADS, M_HEAD_DIM)
    w_in_p = jnp.concatenate([
        part(0), part(1), part(2), heads(part(4)), part(6), part(7),
        jnp.pad(part(8), [(0, 0), (0, 0), (A_NOPE, LANES - A_NOPE - A_ROPE)]),
    ], axis=2).astype(BF16)
    assert w_in_p.shape[2] == IN_PAD
    gates = part(5)
    w_vt = jnp.concatenate([part(3), gates[:, :, GATE_I_COLS], gates[:, :, GATE_F_COLS]],
                           axis=2).astype(BF16).transpose(0, 2, 1)

    conv_p = jnp.pad(conv_qk, [(0, 0), (0, 8 - K_CONV), (0, 0)])

    tile_rows = lambda b: jnp.tile(b[:, :, None], (1, max_chunks, 1))
    ukv = w_ukv.reshape(w_ukv.shape[0], KV_LORA, A_HEADS, A_NOPE + A_V)
    pad_kv = lambda w: jnp.pad(w, [(0, 0), (0, 0), (0, 0), (0, HEAD_PAD - w.shape[-1])]).reshape(
        w.shape[0], KV_LORA, AP_WIDTH).astype(BF16)
    vec = lambda g: g[:, None, :]
    return dict(
        g1=vec(g_norm1), g2=vec(g_norm2), gq=vec(g_q_norm), gkv=vec(g_kv_norm),
        w_in_p=w_in_p, w_vt=w_vt, conv_p=conv_p,
        bir=tile_rows(b_gates[:, GATE_I_COLS]), bfr=tile_rows(b_gates[:, GATE_F_COLS]),
        gm=_pad_heads_cols(g_mlstm, M_HEADS, M_HEAD_DIM)[:, None, :],
        wq=_pad_heads_cols(w_uq, A_HEADS, A_NOPE + A_ROPE).astype(BF16),
        wk=pad_kv(ukv[..., :A_NOPE]), wv=pad_kv(ukv[..., A_NOPE:]),
        wof=w_out[:, :F_WIDTH].astype(BF16),
        wom=_pad_heads_rows(w_out[:, F_WIDTH:F_WIDTH + M_WIDTH], M_HEADS, M_HEAD_DIM).astype(BF16),
        woa=_pad_heads_rows(w_out[:, F_WIDTH + M_WIDTH:], A_HEADS, A_V).astype(BF16),
        wup=w_up.astype(BF16), wdn=w_down.astype(BF16),
    )


def _stream_selectors():
    sel = np.zeros((SEL_ROWS, N_STREAM, MCH), np.float32)
    for s in range(N_STREAM):
        for part in range(3):
            sel[part * N_STREAM + s, s, :] = 1.0
    return jnp.asarray(sel.reshape(SEL_ROWS, N_STREAM * MCH), dtype=BF16)


def kernel(x, c, ctx, c_ctx, w_mod, b_mod, g_norm1, g_norm2, w_in, b_gates, conv_qk, g_mlstm,
           g_q_norm, g_kv_norm, w_uq, w_ukv, w_out, w_up, w_down, g_final):
    nb, seq, d = x.shape
    nctx = ctx.shape[1]
    depth = w_mod.shape[0]
    assert d == D_MODEL and seq % 256 == 0 and nctx % MCH == 0
    sel = _stream_selectors()

    tm = min(512, seq)
    tm_mlp = min(512, seq)
    tm_ctx = min(256, nctx)
    tq = min(8 * ATTN_SUB, seq)
    tq_ctx = min(ATTN_SUB, nctx)

    dft_cc, dft_cs = _channel_dft()
    fft_lat = _fourier4_tables(seq)
    dft_ctx = _position_dft(nctx)
    rope_lat = _rope_tables(seq, True)
    rope_ctx = _rope_tables(nctx, False)
    wts = _prepare_weights(max(seq, nctx) // MCH, g_norm1, g_norm2, w_in, b_gates, conv_qk, g_mlstm,
                           g_q_norm, g_kv_norm, w_uq, w_ukv, w_out, w_up, w_down)
    gfin = g_final.reshape(1, d)

    rows = ((nb + 1 + 7) // 8) * 8
    cc = jnp.concatenate([c, c_ctx[None, :], jnp.zeros((rows - nb - 1, d), F32)], axis=0)
    mod_all = _modulation(cc, w_mod, b_mod).reshape(depth * rows, 6, d)

    xl = x.reshape(nb * seq, d)
    xc = ctx.reshape(nb * nctx, d)

    for l in range(depth):
        last = l == depth - 1
        row_lat, row_ctx = l * rows, l * rows + nb

        zc, zs, mqk, vt, mo, gr, q_a, k_a, v_a = _inproj(
            xl, seq, l, wts, mod_all, row_lat, True, dft_cc, dft_cs, *rope_lat, tm, True)
        zc_c, zs_c, mqk_c, vt_c, mo_c, gr_c, q_ac, k_ac, v_ac = _inproj(
            xc, nctx, l, wts, mod_all, row_ctx, False, dft_cc, dft_cs, *rope_ctx, tm_ctx, not last)

        yf = _fourier4(fft_lat, zc, zs)

        def mlstm_inputs(mqk_s, vt_s, gr_s, n):
            q_s, k_s = _conv_silu(mqk_s.reshape(nb, n, 2 * M_WIDTH), l, wts)
            return (q_s, k_s, vt_s, gr_s)

        hm, hm_c = _mlstm(mlstm_inputs(mqk, vt, gr, seq), mlstm_inputs(mqk_c, vt_c, gr_c, nctx), l, wts, sel,
                          ctx_out=not last)

        b3 = lambda a, n: a.reshape(nb, n, AP_WIDTH)
        keys_ctx = (b3(k_ac, nctx), b3(v_ac, nctx))
        ya = _attention(b3(q_a, seq), [(b3(k_a, seq), b3(v_a, seq)), keys_ctx], tq, ATTN_HEADS_PER_STEP)

        xl = _out_mlp(xl, seq, yf, hm.reshape(nb * seq, MP_WIDTH), mo, ya.reshape(nb * seq, AP_WIDTH),
                      l, wts, mod_all, row_lat, True, gfin, tm_mlp, last)

        if not last:
            yf_c = _fourier(*dft_ctx, zc_c, zs_c)
            ya_c = _attention(b3(q_ac, nctx), [keys_ctx], tq_ctx, A_HEADS)
            xc = _out_mlp(xc, nctx, yf_c, hm_c.reshape(nb * nctx, MP_WIDTH), mo_c,
                          ya_c.reshape(nb * nctx, AP_WIDTH), l, wts, mod_all, row_ctx, False, gfin, tm_ctx, False)

    return xl.reshape(nb, seq, d)
```

```python
import functools

import numpy as np
import jax
import jax.numpy as jnp
from jax import lax
from jax.experimental import pallas as pl
from jax.experimental.pallas import tpu as pltpu

D_MODEL = 1024
GRID_W = 64
EPS = 1e-6
F_GROUPS = 4
F_GROUP_DIM = D_MODEL // 16
F_WIDTH = F_GROUPS * F_GROUP_DIM
M_HEADS = 4
M_HEAD_DIM = 3 * D_MODEL // 32
M_WIDTH = M_HEADS * M_HEAD_DIM
K_CONV = 5
A_HEADS = 4
A_NOPE = 64
A_ROPE = 32
A_V = 3 * D_MODEL // 32
Q_LORA = D_MODEL // 4
KV_LORA = D_MODEL // 8
ROPE_THETA = 10000.0
MLP_HIDDEN = 4 * D_MODEL

LANES = 128
HEAD_PAD = 128
MP_WIDTH = M_HEADS * HEAD_PAD
AP_WIDTH = A_HEADS * HEAD_PAD
VMEM_LIMIT = 56 * 1024 * 1024
MCH = 128
N_STREAM = 2 * M_HEADS
TOKEN_TILE = 512
CTX_TILE = 256
ATTN_SUBTILES = 8

BF16 = jnp.bfloat16
F32 = jnp.float32
LOG2_E = 1.4426950408889634


def _cparams(sem):
    return pltpu.CompilerParams(dimension_semantics=sem, vmem_limit_bytes=VMEM_LIMIT)


def _const_spec(shape):
    nd = len(shape)
    return pl.BlockSpec(shape, lambda *_: (0,) * nd, pipeline_mode=pl.Buffered(1))


def _layer_spec(arr, layer):
    nd = arr.ndim
    return pl.BlockSpec((1,) + arr.shape[1:], lambda *_: (layer,) + (0,) * (nd - 1), pipeline_mode=pl.Buffered(1))


def _split3(a):
    hi = a.astype(BF16)
    r1 = a - hi.astype(F32)
    mid = r1.astype(BF16)
    lo = (r1 - mid.astype(F32)).astype(BF16)
    return hi, mid, lo


def _dot(a, b):
    return jnp.dot(a, b, preferred_element_type=F32)


def _dot_nt(a, b):
    return lax.dot_general(a, b, (((1,), (1,)), ((), ())), preferred_element_type=F32)


def _dot_tn(a, b):
    return lax.dot_general(a, b, (((0,), (0,)), ((), ())), preferred_element_type=F32)


def _rms(x, g):
    return x * lax.rsqrt(jnp.mean(x * x, axis=-1, keepdims=True) + EPS) * g


MOD_COLS = 1536


def _mod_kernel(c_ref, w_ref, b_ref, o_ref):
    c = c_ref[...]
    a = c * jax.nn.sigmoid(c)
    a_hi = a.astype(BF16)
    a_lo = (a - a_hi.astype(F32)).astype(BF16)
    w = w_ref[0]
    w_hi = w.astype(BF16)
    w_lo = (w - w_hi.astype(F32)).astype(BF16)
    acc = _dot(a_hi, w_hi) + _dot(a_hi, w_lo) + _dot(a_lo, w_hi)
    o_ref[0] = acc + b_ref[0]


def _modulation(cc, w_mod, b_mod):
    depth, d, n = w_mod.shape
    rows = cc.shape[0]
    tn = MOD_COLS
    return pl.pallas_call(
        _mod_kernel,
        grid=(depth, n // tn),
        in_specs=[
            pl.BlockSpec((rows, d), lambda l, j: (0, 0)),
            pl.BlockSpec((1, d, tn), lambda l, j: (l, 0, j)),
            pl.BlockSpec((1, 1, tn), lambda l, j: (l, 0, j)),
        ],
        out_specs=pl.BlockSpec((1, rows, tn), lambda l, j: (l, 0, j)),
        out_shape=jax.ShapeDtypeStruct((depth, rows, n), F32),
        compiler_params=_cparams(("arbitrary", "arbitrary")),
        name="modulation",
    )(cc, w_mod, b_mod.reshape(depth, 1, n))


def _rope(x, cos, sin, first_half):
    half = A_ROPE // 4
    partner = jnp.where(first_half, pltpu.roll(x, LANES - half, 1), pltpu.roll(x, half, 1))
    return x * cos + partner * sin


def _inproj_kernel(x_ref, g_ref, mod_ref, wpf_ref, wmqk_ref, wmo_ref, wcq_ref, wkv_ref, wvt_ref, cc_ref, cs_ref,
                   cos_ref, sin_ref, gq_ref, gkv_ref, wq_ref, wk_ref, wv_ref,
                   zc_ref, zs_ref, mqk_ref, vt_ref, mo_ref, gr_ref, qa_ref, ka_ref, va_ref, *, with_q):
    x = x_ref[...]
    mod = mod_ref[0]
    h = _rms(x, g_ref[0]) * (1.0 + mod[1:2]) + mod[0:1]
    hb = h.astype(BF16)

    def proj(w_ref):
        return _dot(hb, w_ref[0])

    ckv_kr = proj(wkv_ref)
    cq = proj(wcq_ref) if with_q else None
    pf = proj(wpf_ref).astype(BF16) if with_q else None

    mqk_ref[...] = proj(wmqk_ref)

    cos = cos_ref[...]
    sin = sin_ref[...]
    lane = lax.broadcasted_iota(jnp.int32, cos.shape, 1)
    first_half = ((lane - A_NOPE) & (A_ROPE // 2 - 1)) < A_ROPE // 4
    kvn = _rms(ckv_kr[:, :KV_LORA], gkv_ref[0]).astype(BF16)
    k_rope = _rope(ckv_kr[:, KV_LORA:], cos, sin, first_half)
    k_nope = _dot(kvn, wk_ref[0])
    va = _dot(kvn, wv_ref[0])
    head_lane = lax.broadcasted_iota(jnp.int32, va.shape, 1) & (HEAD_PAD - 1)
    va_ref[...] = jnp.where(head_lane == A_V, 1.0, va).astype(BF16)
    for hd in range(A_HEADS):
        lanes = slice(hd * HEAD_PAD, (hd + 1) * HEAD_PAD)
        ka_ref[:, lanes] = (k_nope[:, lanes] + k_rope).astype(BF16)
    if with_q:
        qn = _rms(cq, gq_ref[0]).astype(BF16)
        q_all = _dot(qn, wq_ref[0])
        q_raw = [q_all[:, hd * HEAD_PAD:(hd + 1) * HEAD_PAD] for hd in range(A_HEADS)]
        zc_ref[...] = _dot(pf, cc_ref[...]).astype(BF16)
        zs_ref[...] = _dot(pf, cs_ref[...]).astype(BF16)

    vg = _dot_nt(wvt_ref[0], hb)
    vt = vg[:M_WIDTH].astype(BF16)
    for j in range(vt_ref.shape[0]):
        vt_ref[j] = vt[:, j * MCH:(j + 1) * MCH]
        gr_ref[j] = vg[M_WIDTH:, j * MCH:(j + 1) * MCH]

    if with_q:
        mo_ref[...] = proj(wmo_ref)
        scale = (A_NOPE + A_ROPE) ** -0.5 * LOG2_E
        for hd in range(A_HEADS):
            q = _rope(q_raw[hd], cos, sin, first_half)
            qa_ref[:, hd * HEAD_PAD:(hd + 1) * HEAD_PAD] = (q * scale).astype(BF16)
    else:
        for ref in (zc_ref, zs_ref, mo_ref, qa_ref):
            ref[...] = jnp.zeros_like(ref)


def _mod_map(mod_row0, per_batch_mod, tiles_per_seq):
    if per_batch_mod:
        return lambda i: (mod_row0 + i // tiles_per_seq, 0, 0)
    return lambda i: (mod_row0, 0, 0)


def _inproj(x2d, seq, layer, wts, mod, mod_row0, per_batch_mod, dft_cc, dft_cs, cos, sin, tm, with_q):
    t, d = x2d.shape
    nb = t // seq
    tiles_per_seq = seq // tm
    tok = lambda w: pl.BlockSpec((tm, w), lambda i: (i, 0))
    z_spec = pl.BlockSpec((tm, F_WIDTH), lambda i: (i % tiles_per_seq, i // tiles_per_seq))
    pos = pl.BlockSpec((tm, LANES), lambda i: (i % tiles_per_seq, 0))
    heads_bf16 = jax.ShapeDtypeStruct((t, AP_WIDTH), BF16)
    shapes = [
        jax.ShapeDtypeStruct((seq, nb * F_WIDTH), BF16),
        jax.ShapeDtypeStruct((seq, nb * F_WIDTH), BF16),
        jax.ShapeDtypeStruct((t, 2 * M_WIDTH), F32),
        jax.ShapeDtypeStruct((t // MCH, M_WIDTH, MCH), BF16),
        jax.ShapeDtypeStruct((t, MP_WIDTH), F32),
        jax.ShapeDtypeStruct((t // MCH, 2 * N_STREAM, MCH), F32),
        heads_bf16, heads_bf16, heads_bf16,
    ]
    vt_spec = pl.BlockSpec((tm // MCH, M_WIDTH, MCH), lambda i: (i, 0, 0))
    gr_spec = pl.BlockSpec((tm // MCH, 2 * N_STREAM, MCH), lambda i: (i, 0, 0))
    out_specs = [z_spec, z_spec, tok(2 * M_WIDTH), vt_spec, tok(MP_WIDTH), gr_spec,
                 tok(AP_WIDTH), tok(AP_WIDTH), tok(AP_WIDTH)]
    lay = lambda name: _layer_spec(wts[name], layer)
    return pl.pallas_call(
        functools.partial(_inproj_kernel, with_q=with_q),
        grid=(t // tm,),
        in_specs=[
            tok(d),
            lay("g1"),
            pl.BlockSpec((1, 6, d), _mod_map(mod_row0, per_batch_mod, tiles_per_seq)),
            lay("w_pf"), lay("w_mqk"), lay("w_mo"), lay("w_cq"), lay("w_kv"), lay("w_vt"),
            _const_spec((F_WIDTH, F_WIDTH)),
            _const_spec((F_WIDTH, F_WIDTH)),
            pos, pos,
            lay("gq"), lay("gkv"), lay("wq"), lay("wk"), lay("wv"),
        ],
        out_specs=out_specs,
        out_shape=shapes,
        compiler_params=_cparams(("arbitrary",)),
        name="inproj",
    )(x2d, wts["g1"], mod, wts["w_pf"], wts["w_mqk"], wts["w_mo"], wts["w_cq"], wts["w_kv"], wts["w_vt"],
      dft_cc, dft_cs, cos, sin,
      wts["gq"], wts["gkv"], wts["wq"], wts["wk"], wts["wv"])


def _fourier_kernel(c_ref, s_ref, zc_ref, zs_ref, o_ref):
    y = _dot(c_ref[...], zc_ref[...]) - _dot(s_ref[...], zs_ref[...])
    for slab in range(o_ref.shape[0]):
        o_ref[slab] = y[:, slab * LANES:(slab + 1) * LANES]


def _fourier(dft_c, dft_s, zc, zs):
    n, cols = zc.shape
    tr = min(n, 512)
    tc = min(cols, 512)
    return pl.pallas_call(
        _fourier_kernel,
        grid=(n // tr, cols // tc),
        in_specs=[
            pl.BlockSpec((tr, n), lambda i, j: (i, 0)),
            pl.BlockSpec((tr, n), lambda i, j: (i, 0)),
            pl.BlockSpec((n, tc), lambda i, j: (0, j)),
            pl.BlockSpec((n, tc), lambda i, j: (0, j)),
        ],
        out_specs=pl.BlockSpec((tc // LANES, tr, LANES), lambda i, j: (j, i, 0)),
        out_shape=jax.ShapeDtypeStruct((cols // LANES, n, LANES), F32),
        compiler_params=_cparams(("arbitrary", "arbitrary")),
        name="fourier",
    )(dft_c, dft_s, zc, zs)


FFT_COLS = 256


def _fourier4_kernel(tab_ref, twc_ref, tws_ref, zc_ref, zs_ref, o_ref):
    m = zc_ref.shape[0] // 4
    reps = zc_ref.shape[1] // LANES
    c0, c1, c2, c3 = (zc_ref[j * m:(j + 1) * m, :].astype(F32) for j in range(4))
    s0, s1, s2, s3 = (zs_ref[j * m:(j + 1) * m, :].astype(F32) for j in range(4))
    ce, co, cd, cu = c0 + c2, c1 + c3, c0 - c2, c1 - c3
    se, so, sd, su = s0 + s2, s1 + s3, s0 - s2, s1 - s3

    def emit(br, bi, k):
        if k:
            cos = jnp.concatenate([twc_ref[k - 1]] * reps, axis=1)
            sin = jnp.concatenate([tws_ref[k - 1]] * reps, axis=1)
            br, bi = br * cos + bi * sin, bi * cos - br * sin
        stacked = jnp.concatenate([br.astype(BF16), bi.astype(BF16)], axis=0)
        y = _dot(tab_ref[...], stacked)
        for slab in range(reps):
            o_ref[slab, pl.ds(k, m, stride=4), :] = y[:, slab * LANES:(slab + 1) * LANES]

    emit(ce + co, -(se + so), 0)
    emit(cd - su, -sd - cu, 1)
    emit(ce - co, so - se, 2)
    emit(cd + su, cu - sd, 3)


def _fourier4(tables, zc, zs):
    tab, twc, tws = tables
    n, cols = zc.shape
    m = n // 4
    tc = min(cols, FFT_COLS)
    return pl.pallas_call(
        _fourier4_kernel,
        grid=(cols // tc,),
        in_specs=[
            _const_spec((m, 2 * m)),
            _const_spec((3, m, LANES)),
            _const_spec((3, m, LANES)),
            pl.BlockSpec((n, tc), lambda j: (0, j)),
            pl.BlockSpec((n, tc), lambda j: (0, j)),
        ],
        out_specs=pl.BlockSpec((tc // LANES, n, LANES), lambda j: (j, 0, 0)),
        out_shape=jax.ShapeDtypeStruct((cols // LANES, n, LANES), F32),
        compiler_params=_cparams(("arbitrary",)),
        name="fourier4",
    )(tab, twc, tws, zc, zs)


def _fourier4_tables(n):
    m = n // 4
    idx = (np.arange(m, dtype=np.int64)[:, None] * np.arange(m, dtype=np.int64)[None, :]) % m
    ang = 2.0 * np.pi * idx.astype(np.float64) / m
    tab = np.concatenate([np.cos(ang), np.sin(ang)], axis=1) / np.sqrt(n)
    theta = 2.0 * np.pi * np.arange(m, dtype=np.float64)[None, :] * np.arange(1, 4, dtype=np.float64)[:, None] / n
    bcast = lambda t: jnp.asarray(np.repeat(t[:, :, None], LANES, axis=2), dtype=F32)
    return jnp.asarray(tab, dtype=F32).astype(BF16), bcast(np.cos(theta)), bcast(np.sin(theta))


CONV_ROWS = 256
CONV_HALO = 8


def _head_tile(x, start, lane):
    t, o = divmod(start, LANES)
    tile = lambda i: x[:, i * LANES:(i + 1) * LANES]
    if o == 0:
        return tile(t)
    shift = LANES - o
    out = pltpu.roll(tile(t), shift, 1)
    if shift < M_HEAD_DIM:
        out = jnp.where(lane < shift, out, pltpu.roll(tile(t + 1), shift, 1))
    return out


def _conv_kernel(u_ref, w_ref, q_ref, k_ref, pad_ref):
    n = u_ref.shape[1]
    width = u_ref.shape[2]
    zeros = jnp.zeros((CONV_HALO, width), F32)
    pad_ref[0:CONV_HALO, :] = zeros
    pad_ref[CONV_HALO + n:2 * CONV_HALO + n, :] = zeros
    pad_ref[CONV_HALO:CONV_HALO + n, :] = u_ref[0]
    w = w_ref[0]
    rows = min(CONV_ROWS, n)
    span = rows + 2 * CONV_HALO
    for r in range(n // rows):
        block = pad_ref[r * rows:r * rows + span, :]
        acc = None
        for j in range(K_CONV):
            shift = (K_CONV // 2 - j) % span
            tap = block if shift == 0 else pltpu.roll(block, shift, 0)
            term = tap[CONV_HALO:CONV_HALO + rows, :] * w[j:j + 1, :]
            acc = term if acc is None else acc + term
        act = acc * jax.nn.sigmoid(acc)
        lane = lax.broadcasted_iota(jnp.int32, (rows, LANES), 1)
        for hd in range(M_HEADS):
            lanes = slice(hd * HEAD_PAD, (hd + 1) * HEAD_PAD)
            q_tile = _head_tile(act, hd * M_HEAD_DIM, lane)
            k_tile = _head_tile(act, M_WIDTH + hd * M_HEAD_DIM, lane)
            q_ref[0, r * rows:(r + 1) * rows, lanes] = (q_tile * (M_HEAD_DIM ** -0.5)).astype(BF16)
            k_ref[0, r * rows:(r + 1) * rows, lanes] = jnp.where(lane < M_HEAD_DIM, k_tile, 0.0).astype(BF16)


def _conv_silu(mqk, layer, wts):
    nb, n, width = mqk.shape
    return pl.pallas_call(
        _conv_kernel,
        grid=(nb,),
        in_specs=[
            pl.BlockSpec((1, n, width), lambda b: (b, 0, 0)),
            _layer_spec(wts["conv_p"], layer),
        ],
        out_specs=[
            pl.BlockSpec((1, n, MP_WIDTH), lambda b: (b, 0, 0)),
            pl.BlockSpec((1, n, MP_WIDTH), lambda b: (b, 0, 0)),
        ],
        out_shape=[
            jax.ShapeDtypeStruct((nb, n, MP_WIDTH), BF16),
            jax.ShapeDtypeStruct((nb, n, MP_WIDTH), BF16),
        ],
        scratch_shapes=[pltpu.VMEM((n + 2 * CONV_HALO, width), F32)],
        compiler_params=_cparams(("arbitrary",)),
        name="conv_silu",
    )(mqk, wts["conv_p"])


def _log_sigmoid(x):
    return jnp.minimum(x, 0.0) - jnp.log(1.0 + jnp.exp(-jnp.abs(x)))


def _exact_dot_01(a, tri_bf16, a_on_left):
    out = None
    for term in _split3(a):
        d = _dot(term, tri_bf16) if a_on_left else _dot(tri_bf16, term)
        out = d if out is None else out + d
    return out


ONE_ROW = M_HEAD_DIM
(F_R, F_B, F_CM, F_TOT, F_CML, F_MP, F_A, F_WI, F_ELD, F_DEC, F_WK, F_HI, F_MID, F_LO) = range(14)
N_FIELDS = 14
STATE_GROUP = 4
OUT_GROUP = 8
SEL_ROWS = 32


def _mlstm_kernel(ql_ref, kl_ref, vtl_ref, grl_ref,
                  qc_ref, kc_ref, vtc_ref, grc_ref,
                  bir_ref, bfr_ref, sel_ref,
                  hl_ref, hc_ref,
                  ct_st, st_sc, rows_sc, *, ctx_out):
    L = MCH
    nh = M_HEADS
    ncc = qc_ref.shape[1] // L
    ncl = ql_ref.shape[1] // L

    d0 = lax.broadcasted_iota(jnp.int32, (L, L), 0)
    d1 = lax.broadcasted_iota(jnp.int32, (L, L), 1)
    le = d0 <= d1
    ge = d0 >= d1
    tri_le = le.astype(BF16)
    fwd_rows = lax.broadcasted_iota(jnp.int32, (N_STREAM, L), 0) < nh
    feat = lax.broadcasted_iota(jnp.int32, (HEAD_PAD, L), 0)
    keep_rows = feat < M_HEAD_DIM
    ones_tail = (lax.broadcasted_iota(jnp.int32, (HEAD_PAD - M_HEAD_DIM, L), 0) == 0).astype(F32)

    ct_st[...] = jnp.zeros_like(ct_st)

    def field(f, slot0, nc):
        return rows_sc[f, slot0:slot0 + nc].reshape(nc * N_STREAM, L)

    def set_field(f, slot0, nc, val):
        rows_sc[f, slot0:slot0 + nc] = val.reshape(nc, N_STREAM, L)

    def gate_pass(gr_ref, nc, slot0):
        n8 = nc * N_STREAM
        fwd = (lax.broadcasted_iota(jnp.int32, (n8, L), 0) & (N_STREAM - 1)) < nh
        lane = lax.broadcasted_iota(jnp.int32, (n8, L), 1)
        gi = gr_ref[:, 0:N_STREAM, :].reshape(n8, L) + bir_ref[0, 0:n8, :]
        f = _log_sigmoid(gr_ref[:, N_STREAM:2 * N_STREAM, :].reshape(n8, L) + bfr_ref[0, 0:n8, :])
        pre = _exact_dot_01(f, tri_le, a_on_left=True)
        total = jnp.sum(f, axis=1, keepdims=True)
        b = jnp.where(fwd, pre, total - pre + f)
        r = gi - b
        cm = r
        sh = 1
        while sh < L:
            from_left = jnp.where(lane >= sh, pltpu.roll(cm, sh, 1), -jnp.inf)
            from_right = jnp.where(lane < L - sh, pltpu.roll(cm, L - sh, 1), -jnp.inf)
            cm = jnp.maximum(cm, jnp.where(fwd, from_left, from_right))
            sh *= 2
        set_field(F_R, slot0, nc, r)
        set_field(F_B, slot0, nc, b)
        set_field(F_CM, slot0, nc, cm)
        set_field(F_TOT, slot0, nc, jnp.broadcast_to(total, (n8, L)))
        set_field(F_CML, slot0, nc, jnp.broadcast_to(jnp.max(r, axis=1, keepdims=True), (n8, L)))

    def m_scan(nc, slot0, m0):
        def step(j, m):
            sf = slot0 + j
            sb = slot0 + nc - 1 - j
            tot = jnp.where(fwd_rows, rows_sc[F_TOT, sf], rows_sc[F_TOT, sb])
            cml = jnp.where(fwd_rows, rows_sc[F_CML, sf], rows_sc[F_CML, sb])
            rows_sc[F_MP, sf, 0:nh, :] = m[0:nh]
            rows_sc[F_MP, sb, nh:N_STREAM, :] = m[nh:N_STREAM]
            return tot + jnp.maximum(m, cml)

        return lax.fori_loop(0, nc, step, m0)

    def weight_pass(nc, slot0):
        mp, cm, b, r = (field(f, slot0, nc) for f in (F_MP, F_CM, F_B, F_R))
        big = jnp.maximum(mp, field(F_CML, slot0, nc))
        a = -jnp.maximum(mp, cm)
        set_field(F_A, slot0, nc, a)
        set_field(F_WI, slot0, nc, jnp.exp(mp + a))
        set_field(F_ELD, slot0, nc, jnp.exp(a - b))
        set_field(F_DEC, slot0, nc, jnp.exp(mp - big))
        set_field(F_WK, slot0, nc, jnp.exp(r - big))
        for f, term in zip((F_HI, F_MID, F_LO), _split3(r)):
            set_field(f, slot0, nc, term.astype(F32))

    def value_slab(vt_ref, c, hd):
        vt = vt_ref[c, hd * M_HEAD_DIM:(hd + 1) * M_HEAD_DIM, :].astype(F32)
        return jnp.concatenate([vt, ones_tail], axis=0)

    def state_pass(k_ref, vt_ref, nc, slot0):
        group = min(STATE_GROUP, nc)

        def updates_of(j):
            uts = []
            for sidx in range(N_STREAM):
                hd = sidx % nh
                c = j if sidx < nh else nc - 1 - j
                wk = rows_sc[F_WK, slot0 + c, sidx:sidx + 1, :]
                kk = k_ref[0, pl.ds(pl.multiple_of(c * L, L), L), hd * HEAD_PAD:(hd + 1) * HEAD_PAD]
                uts.append(_dot((value_slab(vt_ref, c, hd) * wk).astype(BF16), kk))
            return uts

        def step(g, carry):
            all_uts = [updates_of(g * group + u) for u in range(group)]
            for u in range(group):
                j = g * group + u
                for sidx in range(N_STREAM):
                    slot = slot0 + (j if sidx < nh else nc - 1 - j)
                    prev = ct_st[sidx]
                    st_sc[sidx, slot] = prev.astype(BF16)
                    ct_st[sidx] = rows_sc[F_DEC, slot, sidx:sidx + 1, :] * prev + all_uts[u][sidx]
            return carry

        lax.fori_loop(0, nc // group, step, 0)

    def output_pass(q_ref, k_ref, vt_ref, out_ref, nc, slot0):
        head_lanes = [slice(hd * HEAD_PAD, (hd + 1) * HEAD_PAD) for hd in range(nh)]

        def independent_matmuls(c):
            rows = pl.ds(pl.multiple_of(c * L, L), L)
            slot = slot0 + c
            r3 = jnp.concatenate([rows_sc[F_HI, slot], rows_sc[F_MID, slot], rows_sc[F_LO, slot],
                                  jnp.zeros((N_STREAM, L), F32)], axis=0).astype(BF16)
            qs = [q_ref[0, rows, lanes] for lanes in head_lanes]
            s_ts = [_dot_nt(k_ref[0, rows, lanes], q) for lanes, q in zip(head_lanes, qs)]
            inters = [_dot_nt(st_sc[sidx, slot], qs[sidx % nh]) for sidx in range(N_STREAM)]
            r_all = _dot_tn(r3, sel_ref[...])
            r_ts = [r_all[:, sidx * L:(sidx + 1) * L] for sidx in range(N_STREAM)]
            return s_ts, inters, r_ts

        def finish(c, s_ts, inters, r_ts):
            rows = pl.ds(pl.multiple_of(c * L, L), L)
            slot = slot0 + c
            a_rows = rows_sc[F_A, slot]
            wi_rows = rows_sc[F_WI, slot]
            eld_rows = rows_sc[F_ELD, slot]
            p_ts = []
            for sidx in range(N_STREAM):
                one = slice(sidx, sidx + 1)
                dm = jnp.where(le if sidx < nh else ge, r_ts[sidx] + a_rows[one, :], -jnp.inf)
                p_ts.append((s_ts[sidx % nh] * jnp.exp(dm)).astype(BF16))
            z_pairs = [_dot(value_slab(vt_ref, c, hd).astype(BF16),
                            jnp.concatenate([p_ts[hd], p_ts[nh + hd]], axis=1)) for hd in range(nh)]
            for hd in range(nh):
                hsum = None
                for dr, sidx in enumerate((hd, nh + hd)):
                    one = slice(sidx, sidx + 1)
                    z_t = z_pairs[hd][:, dr * L:(dr + 1) * L] + inters[sidx] * wi_rows[one, :]
                    den = z_t[ONE_ROW:ONE_ROW + 1, :]
                    h_t = z_t * (1.0 / jnp.maximum(jnp.abs(den), eld_rows[one, :]))
                    hsum = h_t if hsum is None else hsum + h_t
                out_ref[0, rows, head_lanes[hd]] = jnp.where(keep_rows, hsum, 0.0).T

        group = min(OUT_GROUP, nc)

        def step(g, carry):
            ahead = independent_matmuls(g * group)
            for u in range(group):
                cur = ahead
                if u + 1 < group:
                    ahead = independent_matmuls(g * group + u + 1)
                finish(g * group + u, *cur)
            return carry

        lax.fori_loop(0, nc // group, step, 0)

    gate_pass(grc_ref, ncc, 0)
    gate_pass(grl_ref, ncl, ncc)
    m1 = m_scan(ncc, 0, jnp.zeros((N_STREAM, L), F32))
    m_scan(ncl, ncc, m1)
    weight_pass(ncc, 0)
    weight_pass(ncl, ncc)
    state_pass(kc_ref, vtc_ref, ncc, 0)
    state_pass(kl_ref, vtl_ref, ncl, ncc)
    if ctx_out:
        output_pass(qc_ref, kc_ref, vtc_ref, hc_ref, ncc, 0)
    else:
        hc_ref[...] = jnp.zeros_like(hc_ref)
    output_pass(ql_ref, kl_ref, vtl_ref, hl_ref, ncl, ncc)


def _mlstm(lat, ctx, layer, wts, sel, ctx_out):
    nb, n, _ = lat[0].shape
    nctx = ctx[0].shape[1]
    assert MCH == LANES == HEAD_PAD
    nct = (n + nctx) // MCH
    assert wts["bir"].shape[1] >= max(n, nctx) // MCH * N_STREAM

    def specs(rows):
        nc = rows // MCH
        seq = lambda w: pl.BlockSpec((1, rows, w), lambda b: (b, 0, 0))
        chunked = lambda rows_: pl.BlockSpec((nc, rows_, MCH), lambda b: (b, 0, 0))
        return [seq(MP_WIDTH), seq(MP_WIDTH), chunked(M_WIDTH), chunked(2 * N_STREAM)]

    out_spec = lambda rows: pl.BlockSpec((1, rows, MP_WIDTH), lambda b: (b, 0, 0))
    return pl.pallas_call(
        functools.partial(_mlstm_kernel, ctx_out=ctx_out),
        grid=(nb,),
        in_specs=specs(n) + specs(nctx) + [
            _layer_spec(wts["bir"], layer), _layer_spec(wts["bfr"], layer),
            _const_spec((SEL_ROWS, N_STREAM * MCH)),
        ],
        out_specs=[out_spec(n), out_spec(nctx)],
        out_shape=[
            jax.ShapeDtypeStruct((nb, n, MP_WIDTH), F32),
            jax.ShapeDtypeStruct((nb, nctx, MP_WIDTH), F32),
        ],
        scratch_shapes=[
            pltpu.VMEM((N_STREAM, HEAD_PAD, HEAD_PAD), F32),
            pltpu.VMEM((N_STREAM, nct, HEAD_PAD, HEAD_PAD), BF16),
            pltpu.VMEM((N_FIELDS, nct, N_STREAM, MCH), F32),
        ],
        compiler_params=_cparams(("arbitrary",)),
        name="mlstm",
    )(*lat, *ctx, wts["bir"], wts["bfr"], sel)


ATTN_SUB = 256
ATTN_HEADS_PER_STEP = 2


def _attn_kernel(*refs, n_sets):
    q_ref = refs[0]
    kv_refs = refs[1:1 + 2 * n_sets]
    o_ref = refs[1 + 2 * n_sets]
    sub = min(ATTN_SUB, q_ref.shape[1])
    n_sub = q_ref.shape[1] // sub
    items = [(hd, t) for hd in range(q_ref.shape[2] // HEAD_PAD) for t in range(n_sub)]

    def scores_of(item):
        hd, t = item
        lanes = slice(hd * HEAD_PAD, (hd + 1) * HEAD_PAD)
        q = q_ref[0, t * sub:(t + 1) * sub, lanes]
        return [_dot_nt(q, kv_refs[2 * i][0, :, lanes]) for i in range(n_sets)]

    nxt = scores_of(items[0])
    for idx, (hd, t) in enumerate(items):
        rows = slice(t * sub, (t + 1) * sub)
        lanes = slice(hd * HEAD_PAD, (hd + 1) * HEAD_PAD)
        scores = nxt
        if idx + 1 < len(items):
            nxt = scores_of(items[idx + 1])
        m = None
        for s in scores:
            sm = jnp.max(s, axis=-1, keepdims=True)
            m = sm if m is None else jnp.maximum(m, sm)
        acc = None
        for i, s in enumerate(scores):
            p = jnp.exp2(s - m)
            o = _dot(p.astype(BF16), kv_refs[2 * i + 1][0, :, lanes])
            acc = o if acc is None else acc + o
        o_ref[0, rows, lanes] = (acc / acc[:, A_V:A_V + 1]).astype(BF16)


def _attention(q, key_sets, tq, heads_per_step):
    nb, n, _ = q.shape
    n_sets = len(key_sets)
    width = heads_per_step * HEAD_PAD
    in_specs = [pl.BlockSpec((1, tq, width), lambda b, h, i: (b, i, h))]
    args = [q]
    for k, v in key_sets:
        nk = k.shape[1]
        spec = pl.BlockSpec((1, nk, width), lambda b, h, i: (b, 0, h))
        in_specs += [spec, spec]
        args += [k, v]
    return pl.pallas_call(
        functools.partial(_attn_kernel, n_sets=n_sets),
        grid=(nb, A_HEADS // heads_per_step, n // tq),
        in_specs=in_specs,
        out_specs=pl.BlockSpec((1, tq, width), lambda b, h, i: (b, i, h)),
        out_shape=jax.ShapeDtypeStruct((nb, n, AP_WIDTH), BF16),
        compiler_params=_cparams(("arbitrary", "arbitrary", "arbitrary")),
        name="attention",
    )(*args)


MLP_CHUNK = 1024


def _out_mlp_kernel(x_ref, yf_ref, hm_ref, mo_ref, ya_ref, mod_ref,
                    gm_ref, g2_ref, gfin_ref, wof_ref, wom_ref, woa_ref, wup_ref, wdn_ref,
                    o_ref, *, final_norm):
    mod = mod_ref[0]
    ga1, sh2, sc2, ga2 = mod[2:3], mod[3:4], mod[4:5], mod[5:6]
    yf = jnp.concatenate([yf_ref[slab].astype(BF16) for slab in range(yf_ref.shape[0])], axis=1)
    mix = _dot(ya_ref[...], woa_ref[0]) + _dot(yf, wof_ref[0])
    gm = gm_ref[0]
    yms = []
    for hd in range(M_HEADS):
        lanes = slice(hd * HEAD_PAD, (hd + 1) * HEAD_PAD)
        hh = hm_ref[:, lanes]
        ms = jnp.sum(hh * hh, axis=-1, keepdims=True) * (1.0 / M_HEAD_DIM)
        ym = hh * lax.rsqrt(ms + EPS) * gm[:, lanes] * jax.nn.sigmoid(mo_ref[:, lanes])
        yms.append(ym.astype(BF16))
    mix = mix + _dot(jnp.concatenate(yms, axis=1), wom_ref[0])
    x1 = x_ref[...] + ga1 * mix
    h2 = (_rms(x1, g2_ref[0]) * (1.0 + sc2) + sh2).astype(BF16)
    acc = None
    for c in range(wup_ref.shape[2] // MLP_CHUNK):
        cols = slice(c * MLP_CHUNK, (c + 1) * MLP_CHUNK)
        u = jnp.maximum(_dot(h2, wup_ref[0, :, cols]), 0.0)
        d = _dot((u * u).astype(BF16), wdn_ref[0, cols, :])
        acc = d if acc is None else acc + d
    x2 = x1 + ga2 * acc
    if final_norm:
        x2 = _rms(x2, gfin_ref[...])
    o_ref[...] = x2


def _out_mlp(x2d, seq, yf, hm, mo, ya, layer, wts, mod, mod_row0, per_batch_mod, gfin, tm, final_norm):
    t, d = x2d.shape
    tiles_per_seq = seq // tm
    tok = lambda w: pl.BlockSpec((tm, w), lambda i: (i, 0))
    yf_spec = pl.BlockSpec((F_WIDTH // LANES, tm, LANES), lambda i: (i // tiles_per_seq, i % tiles_per_seq, 0))
    names = ("gm", "g2", "wof", "wom", "woa", "wup", "wdn")
    lay = {name: _layer_spec(wts[name], layer) for name in names}
    return pl.pallas_call(
        functools.partial(_out_mlp_kernel, final_norm=final_norm),
        grid=(t // tm,),
        in_specs=[
            tok(d), yf_spec, tok(MP_WIDTH), tok(MP_WIDTH), tok(AP_WIDTH),
            pl.BlockSpec((1, 6, d), _mod_map(mod_row0, per_batch_mod, tiles_per_seq)),
            lay["gm"], lay["g2"], _const_spec((1, d)),
            lay["wof"], lay["wom"], lay["woa"], lay["wup"], lay["wdn"],
        ],
        out_specs=tok(d),
        out_shape=jax.ShapeDtypeStruct((t, d), F32),
        compiler_params=_cparams(("arbitrary",)),
        name="out_mlp",
    )(x2d, yf, hm, mo, ya, mod, wts["gm"], wts["g2"], gfin,
      wts["wof"], wts["wom"], wts["woa"], wts["wup"], wts["wdn"])


def _dft_tables(n):
    idx = (np.arange(n, dtype=np.int64)[:, None] * np.arange(n, dtype=np.int64)[None, :]) % n
    ang = 2.0 * np.pi * idx.astype(np.float64) / n
    scale = 1.0 / np.sqrt(n)
    return np.cos(ang) * scale, np.sin(ang) * scale


def _channel_dft():
    c, s = _dft_tables(F_GROUP_DIM)
    eye = np.eye(F_GROUPS)
    return (jnp.asarray(np.kron(eye, c), dtype=F32).astype(BF16),
            jnp.asarray(np.kron(eye, s), dtype=F32).astype(BF16))


def _position_dft(n):
    c, s = _dft_tables(n)
    return jnp.asarray(c, dtype=F32).astype(BF16), jnp.asarray(s, dtype=F32).astype(BF16)


def _rope_tables(n, rotate):
    cos = np.zeros((n, HEAD_PAD), np.float32)
    sin = np.zeros((n, HEAD_PAD), np.float32)
    cos[:, :A_NOPE + A_ROPE] = 1.0
    if rotate:
        nf = A_ROPE // 4
        t = np.arange(n)
        row = (t // GRID_W).astype(np.float32)
        col = (t % GRID_W).astype(np.float32)
        freqs = (np.float32(ROPE_THETA) ** (-np.arange(nf, dtype=np.float32) / np.float32(nf))).astype(np.float32)
        for seg, pos in enumerate((row, col)):
            ang = pos[:, None] * freqs[None, :]
            c, s = np.cos(ang), np.sin(ang)
            base = A_NOPE + seg * 2 * nf
            cos[:, base:base + nf] = c
            cos[:, base + nf:base + 2 * nf] = c
            sin[:, base:base + nf] = -s
            sin[:, base + nf:base + 2 * nf] = s
    return jnp.asarray(cos), jnp.asarray(sin)


def _pad_heads_cols(w, heads, width):
    lead = w.shape[:-1]
    w = w.reshape(lead + (heads, width))
    w = jnp.pad(w, [(0, 0)] * len(lead) + [(0, 0), (0, HEAD_PAD - width)])
    return w.reshape(lead + (heads * HEAD_PAD,))


def _pad_heads_rows(w, heads, width):
    depth, _, n = w.shape
    w = jnp.pad(w.reshape(depth, heads, width, n), [(0, 0), (0, 0), (0, HEAD_PAD - width), (0, 0)])
    return w.reshape(depth, heads * HEAD_PAD, n)


GATE_I_COLS = np.concatenate([np.arange(M_HEADS), 2 * M_HEADS + np.arange(M_HEADS)])
GATE_F_COLS = GATE_I_COLS + M_HEADS


def _prepare_weights(max_chunks, g_norm1, g_norm2, w_in, b_gates, conv_qk, g_mlstm, g_q_norm, g_kv_norm,
                     w_uq, w_ukv, w_out, w_up, w_down):
    offs = np.cumsum([0, F_WIDTH, M_WIDTH, M_WIDTH, M_WIDTH, M_WIDTH, 4 * M_HEADS, Q_LORA, KV_LORA, A_ROPE])
    part = lambda i: w_in[:, :, offs[i]:offs[i + 1]]
    heads = lambda w: _pad_heads_cols(w, M_HEADS, M_HEAD_DIM)
    w_kv = jnp.concatenate([part(7), jnp.pad(part(8), [(0, 0), (0, 0), (A_NOPE, LANES - A_NOPE - A_ROPE)])], axis=2)
    in_groups = dict(w_pf=part(0), w_mqk=w_in[:, :, offs[1]:offs[3]], w_mo=heads(part(4)), w_cq=part(6), w_kv=w_kv)
    in_groups = {name: w.astype(BF16) for name, w in in_groups.items()}
    gates = part(5)
    w_vt = jnp.concatenate([part(3), gates[:, :, GATE_I_COLS], gates[:, :, GATE_F_COLS]],
                           axis=2).astype(BF16).transpose(0, 2, 1)

    conv_p = jnp.pad(conv_qk, [(0, 0), (0, 8 - K_CONV), (0, 0)])

    tile_rows = lambda b: jnp.tile(b[:, :, None], (1, max_chunks, 1))
    ukv = w_ukv.reshape(w_ukv.shape[0], KV_LORA, A_HEADS, A_NOPE + A_V)
    pad_kv = lambda w: jnp.pad(w, [(0, 0), (0, 0), (0, 0), (0, HEAD_PAD - w.shape[-1])]).reshape(
        w.shape[0], KV_LORA, AP_WIDTH).astype(BF16)
    vec = lambda g: g[:, None, :]
    return dict(
        g1=vec(g_norm1), g2=vec(g_norm2), gq=vec(g_q_norm), gkv=vec(g_kv_norm),
        **in_groups, w_vt=w_vt, conv_p=conv_p,
        bir=tile_rows(b_gates[:, GATE_I_COLS]), bfr=tile_rows(b_gates[:, GATE_F_COLS]),
        gm=_pad_heads_cols(g_mlstm, M_HEADS, M_HEAD_DIM)[:, None, :],
        wq=_pad_heads_cols(w_uq, A_HEADS, A_NOPE + A_ROPE).astype(BF16),
        wk=pad_kv(ukv[..., :A_NOPE]), wv=pad_kv(ukv[..., A_NOPE:]),
        wof=w_out[:, :F_WIDTH].astype(BF16),
        wom=_pad_heads_rows(w_out[:, F_WIDTH:F_WIDTH + M_WIDTH], M_HEADS, M_HEAD_DIM).astype(BF16),
        woa=_pad_heads_rows(w_out[:, F_WIDTH + M_WIDTH:], A_HEADS, A_V).astype(BF16),
        wup=w_up.astype(BF16), wdn=w_down.astype(BF16),
    )


def _stream_selectors():
    sel = np.zeros((SEL_ROWS, N_STREAM, MCH), np.float32)
    for s in range(N_STREAM):
        for part in range(3):
            sel[part * N_STREAM + s, s, :] = 1.0
    return jnp.asarray(sel.reshape(SEL_ROWS, N_STREAM * MCH), dtype=BF16)


def kernel(x, c, ctx, c_ctx, w_mod, b_mod, g_norm1, g_norm2, w_in, b_gates, conv_qk, g_mlstm,
           g_q_norm, g_kv_norm, w_uq, w_ukv, w_out, w_up, w_down, g_final):
    nb, seq, d = x.shape
    nctx = ctx.shape[1]
    depth = w_mod.shape[0]
    assert d == D_MODEL and seq % (4 * MCH) == 0 and nctx % MCH == 0
    sel = _stream_selectors()

    tm = min(TOKEN_TILE, seq)
    tm_mlp = min(TOKEN_TILE, seq)
    tm_ctx = min(CTX_TILE, nctx)
    tq = min(ATTN_SUBTILES * ATTN_SUB, seq)
    tq_ctx = min(ATTN_SUB, nctx)

    dft_cc, dft_cs = _channel_dft()
    fft_lat = _fourier4_tables(seq)
    dft_ctx = _position_dft(nctx)
    rope_lat = _rope_tables(seq, True)
    rope_ctx = _rope_tables(nctx, False)
    wts = _prepare_weights(max(seq, nctx) // MCH, g_norm1, g_norm2, w_in, b_gates, conv_qk, g_mlstm,
                           g_q_norm, g_kv_norm, w_uq, w_ukv, w_out, w_up, w_down)
    gfin = g_final.reshape(1, d)

    rows = ((nb + 1 + 7) // 8) * 8
    cc = jnp.concatenate([c, c_ctx[None, :], jnp.zeros((rows - nb - 1, d), F32)], axis=0)
    mod_all = _modulation(cc, w_mod, b_mod).reshape(depth * rows, 6, d)

    xl = x.reshape(nb * seq, d)
    xc = ctx.reshape(nb * nctx, d)

    for l in range(depth):
        last = l == depth - 1
        row_lat, row_ctx = l * rows, l * rows + nb

        zc, zs, mqk, vt, mo, gr, q_a, k_a, v_a = _inproj(
            xl, seq, l, wts, mod_all, row_lat, True, dft_cc, dft_cs, *rope_lat, tm, True)
        zc_c, zs_c, mqk_c, vt_c, mo_c, gr_c, q_ac, k_ac, v_ac = _inproj(
            xc, nctx, l, wts, mod_all, row_ctx, False, dft_cc, dft_cs, *rope_ctx, tm_ctx, not last)

        yf = _fourier4(fft_lat, zc, zs)

        def mlstm_inputs(mqk_s, vt_s, gr_s, n):
            q_s, k_s = _conv_silu(mqk_s.reshape(nb, n, 2 * M_WIDTH), l, wts)
            return (q_s, k_s, vt_s, gr_s)

        hm, hm_c = _mlstm(mlstm_inputs(mqk, vt, gr, seq), mlstm_inputs(mqk_c, vt_c, gr_c, nctx), l, wts, sel,
                          ctx_out=not last)

        b3 = lambda a, n: a.reshape(nb, n, AP_WIDTH)
        keys_ctx = (b3(k_ac, nctx), b3(v_ac, nctx))
        ya = _attention(b3(q_a, seq), [(b3(k_a, seq), b3(v_a, seq)), keys_ctx], tq, ATTN_HEADS_PER_STEP)

        xl = _out_mlp(xl, seq, yf, hm.reshape(nb * seq, MP_WIDTH), mo, ya.reshape(nb * seq, AP_WIDTH),
                      l, wts, mod_all, row_lat, True, gfin, tm_mlp, last)

        if not last:
            yf_c = _fourier(*dft_ctx, zc_c, zs_c)
            ya_c = _attention(b3(q_ac, nctx), [keys_ctx], tq_ctx, A_HEADS)
            xc = _out_mlp(xc, nctx, yf_c, hm_c.reshape(nb * nctx, MP_WIDTH), mo_c,
                          ya_c.reshape(nb * nctx, AP_WIDTH), l, wts, mod_all, row_ctx, False, gfin, tm_ctx, False)

    return xl.reshape(nb, seq, d)
```

```python
import functools

import numpy as np
import jax
import jax.numpy as jnp
from jax import lax
from jax.experimental import pallas as pl
from jax.experimental.pallas import tpu as pltpu

D_MODEL = 1024
GRID_W = 64
EPS = 1e-6
F_GROUPS = 4
F_GROUP_DIM = D_MODEL // 16
F_WIDTH = F_GROUPS * F_GROUP_DIM
M_HEADS = 4
M_HEAD_DIM = 3 * D_MODEL // 32
M_WIDTH = M_HEADS * M_HEAD_DIM
K_CONV = 5
A_HEADS = 4
A_NOPE = 64
A_ROPE = 32
A_V = 3 * D_MODEL // 32
Q_LORA = D_MODEL // 4
KV_LORA = D_MODEL // 8
ROPE_THETA = 10000.0
MLP_HIDDEN = 4 * D_MODEL

LANES = 128
HEAD_PAD = 128
MP_WIDTH = M_HEADS * HEAD_PAD
AP_WIDTH = A_HEADS * HEAD_PAD
VMEM_LIMIT = 56 * 1024 * 1024
MCH = 128
N_STREAM = 2 * M_HEADS
TOKEN_TILE = 512
CTX_TILE = 256
ATTN_SUBTILES = 8

BF16 = jnp.bfloat16
F32 = jnp.float32
LOG2_E = 1.4426950408889634


def _cparams(sem):
    return pltpu.CompilerParams(dimension_semantics=sem, vmem_limit_bytes=VMEM_LIMIT)


def _const_spec(shape):
    nd = len(shape)
    return pl.BlockSpec(shape, lambda *_: (0,) * nd, pipeline_mode=pl.Buffered(1))


def _layer_spec(arr, layer):
    nd = arr.ndim
    return pl.BlockSpec((1,) + arr.shape[1:], lambda *_: (layer,) + (0,) * (nd - 1), pipeline_mode=pl.Buffered(1))


def _split3(a):
    hi = a.astype(BF16)
    r1 = a - hi.astype(F32)
    mid = r1.astype(BF16)
    lo = (r1 - mid.astype(F32)).astype(BF16)
    return hi, mid, lo


def _dot(a, b):
    return jnp.dot(a, b, preferred_element_type=F32)


def _dot_nt(a, b):
    return lax.dot_general(a, b, (((1,), (1,)), ((), ())), preferred_element_type=F32)


def _dot_tn(a, b):
    return lax.dot_general(a, b, (((0,), (0,)), ((), ())), preferred_element_type=F32)


def _rms(x, g):
    return x * lax.rsqrt(jnp.mean(x * x, axis=-1, keepdims=True) + EPS) * g


MOD_COLS = 1536


def _mod_kernel(c_ref, w_ref, b_ref, o_ref):
    c = c_ref[...]
    a = c * jax.nn.sigmoid(c)
    a_hi = a.astype(BF16)
    a_lo = (a - a_hi.astype(F32)).astype(BF16)
    w = w_ref[0]
    w_hi = w.astype(BF16)
    w_lo = (w - w_hi.astype(F32)).astype(BF16)
    acc = _dot(a_hi, w_hi) + _dot(a_hi, w_lo) + _dot(a_lo, w_hi)
    o_ref[0] = acc + b_ref[0]


def _modulation(cc, w_mod, b_mod):
    depth, d, n = w_mod.shape
    rows = cc.shape[0]
    tn = MOD_COLS
    return pl.pallas_call(
        _mod_kernel,
        grid=(depth, n // tn),
        in_specs=[
            pl.BlockSpec((rows, d), lambda l, j: (0, 0)),
            pl.BlockSpec((1, d, tn), lambda l, j: (l, 0, j)),
            pl.BlockSpec((1, 1, tn), lambda l, j: (l, 0, j)),
        ],
        out_specs=pl.BlockSpec((1, rows, tn), lambda l, j: (l, 0, j)),
        out_shape=jax.ShapeDtypeStruct((depth, rows, n), F32),
        compiler_params=_cparams(("arbitrary", "arbitrary")),
        name="modulation",
    )(cc, w_mod, b_mod.reshape(depth, 1, n))


def _rope(x, cos, sin, first_half):
    half = A_ROPE // 4
    partner = jnp.where(first_half, pltpu.roll(x, LANES - half, 1), pltpu.roll(x, half, 1))
    return x * cos + partner * sin


def _inproj_kernel(x_ref, g_ref, mod_ref, wpf_ref, wmqk_ref, wmo_ref, wcq_ref, wkv_ref, wvt_ref, cc_ref, cs_ref,
                   cos_ref, sin_ref, gq_ref, gkv_ref, wq_ref, wk_ref, wv_ref,
                   zc_ref, zs_ref, mqk_ref, vt_ref, mo_ref, gr_ref, qa_ref, ka_ref, va_ref, *, with_q):
    x = x_ref[...]
    mod = mod_ref[0]
    h = _rms(x, g_ref[0]) * (1.0 + mod[1:2]) + mod[0:1]
    hb = h.astype(BF16)

    def proj(w_ref):
        return _dot(hb, w_ref[0])

    ckv_kr = proj(wkv_ref)
    cq = proj(wcq_ref) if with_q else None
    pf = proj(wpf_ref).astype(BF16) if with_q else None

    mqk_ref[...] = proj(wmqk_ref)

    cos = cos_ref[...]
    sin = sin_ref[...]
    lane = lax.broadcasted_iota(jnp.int32, cos.shape, 1)
    first_half = ((lane - A_NOPE) & (A_ROPE // 2 - 1)) < A_ROPE // 4
    kvn = _rms(ckv_kr[:, :KV_LORA], gkv_ref[0]).astype(BF16)
    k_rope = _rope(ckv_kr[:, KV_LORA:], cos, sin, first_half)
    k_nope = _dot(kvn, wk_ref[0])
    va = _dot(kvn, wv_ref[0])
    head_lane = lax.broadcasted_iota(jnp.int32, va.shape, 1) & (HEAD_PAD - 1)
    va_ref[...] = jnp.where(head_lane == A_V, 1.0, va).astype(BF16)
    for hd in range(A_HEADS):
        lanes = slice(hd * HEAD_PAD, (hd + 1) * HEAD_PAD)
        ka_ref[:, lanes] = (k_nope[:, lanes] + k_rope).astype(BF16)
    if with_q:
        qn = _rms(cq, gq_ref[0]).astype(BF16)
        q_all = _dot(qn, wq_ref[0])
        q_raw = [q_all[:, hd * HEAD_PAD:(hd + 1) * HEAD_PAD] for hd in range(A_HEADS)]
        zc_ref[...] = _dot(pf, cc_ref[...]).astype(BF16)
        zs_ref[...] = _dot(pf, cs_ref[...]).astype(BF16)

    vg = _dot_nt(wvt_ref[0], hb)
    vt = vg[:M_WIDTH].astype(BF16)
    for j in range(vt_ref.shape[0]):
        vt_ref[j] = vt[:, j * MCH:(j + 1) * MCH]
        gr_ref[j] = vg[M_WIDTH:, j * MCH:(j + 1) * MCH]

    if with_q:
        mo_ref[...] = proj(wmo_ref)
        scale = (A_NOPE + A_ROPE) ** -0.5 * LOG2_E
        for hd in range(A_HEADS):
            q = _rope(q_raw[hd], cos, sin, first_half)
            qa_ref[:, hd * HEAD_PAD:(hd + 1) * HEAD_PAD] = (q * scale).astype(BF16)
    else:
        for ref in (zc_ref, zs_ref, mo_ref, qa_ref):
            ref[...] = jnp.zeros_like(ref)


def _mod_map(mod_row0, per_batch_mod, tiles_per_seq):
    if per_batch_mod:
        return lambda i: (mod_row0 + i // tiles_per_seq, 0, 0)
    return lambda i: (mod_row0, 0, 0)


def _inproj(x2d, seq, layer, wts, mod, mod_row0, per_batch_mod, dft_cc, dft_cs, cos, sin, tm, with_q):
    t, d = x2d.shape
    nb = t // seq
    tiles_per_seq = seq // tm
    tok = lambda w: pl.BlockSpec((tm, w), lambda i: (i, 0))
    z_spec = pl.BlockSpec((tm, F_WIDTH), lambda i: (i % tiles_per_seq, i // tiles_per_seq))
    pos = pl.BlockSpec((tm, LANES), lambda i: (i % tiles_per_seq, 0))
    heads_bf16 = jax.ShapeDtypeStruct((t, AP_WIDTH), BF16)
    shapes = [
        jax.ShapeDtypeStruct((seq, nb * F_WIDTH), BF16),
        jax.ShapeDtypeStruct((seq, nb * F_WIDTH), BF16),
        jax.ShapeDtypeStruct((t, 2 * M_WIDTH), F32),
        jax.ShapeDtypeStruct((t // MCH, M_WIDTH, MCH), BF16),
        jax.ShapeDtypeStruct((t, MP_WIDTH), F32),
        jax.ShapeDtypeStruct((t // MCH, 2 * N_STREAM, MCH), F32),
        heads_bf16, heads_bf16, heads_bf16,
    ]
    vt_spec = pl.BlockSpec((tm // MCH, M_WIDTH, MCH), lambda i: (i, 0, 0))
    gr_spec = pl.BlockSpec((tm // MCH, 2 * N_STREAM, MCH), lambda i: (i, 0, 0))
    out_specs = [z_spec, z_spec, tok(2 * M_WIDTH), vt_spec, tok(MP_WIDTH), gr_spec,
                 tok(AP_WIDTH), tok(AP_WIDTH), tok(AP_WIDTH)]
    lay = lambda name: _layer_spec(wts[name], layer)
    return pl.pallas_call(
        functools.partial(_inproj_kernel, with_q=with_q),
        grid=(t // tm,),
        in_specs=[
            tok(d),
            lay("g1"),
            pl.BlockSpec((1, 6, d), _mod_map(mod_row0, per_batch_mod, tiles_per_seq)),
            lay("w_pf"), lay("w_mqk"), lay("w_mo"), lay("w_cq"), lay("w_kv"), lay("w_vt"),
            _const_spec((F_WIDTH, F_WIDTH)),
            _const_spec((F_WIDTH, F_WIDTH)),
            pos, pos,
            lay("gq"), lay("gkv"), lay("wq"), lay("wk"), lay("wv"),
        ],
        out_specs=out_specs,
        out_shape=shapes,
        compiler_params=_cparams(("arbitrary",)),
        name="inproj",
    )(x2d, wts["g1"], mod, wts["w_pf"], wts["w_mqk"], wts["w_mo"], wts["w_cq"], wts["w_kv"], wts["w_vt"],
      dft_cc, dft_cs, cos, sin,
      wts["gq"], wts["gkv"], wts["wq"], wts["wk"], wts["wv"])


def _fourier_kernel(c_ref, s_ref, zc_ref, zs_ref, o_ref):
    y = _dot(c_ref[...], zc_ref[...]) - _dot(s_ref[...], zs_ref[...])
    for slab in range(o_ref.shape[0]):
        o_ref[slab] = y[:, slab * LANES:(slab + 1) * LANES]


def _fourier(dft_c, dft_s, zc, zs):
    n, cols = zc.shape
    tr = min(n, 512)
    tc = min(cols, 512)
    return pl.pallas_call(
        _fourier_kernel,
        grid=(n // tr, cols // tc),
        in_specs=[
            pl.BlockSpec((tr, n), lambda i, j: (i, 0)),
            pl.BlockSpec((tr, n), lambda i, j: (i, 0)),
            pl.BlockSpec((n, tc), lambda i, j: (0, j)),
            pl.BlockSpec((n, tc), lambda i, j: (0, j)),
        ],
        out_specs=pl.BlockSpec((tc // LANES, tr, LANES), lambda i, j: (j, i, 0)),
        out_shape=jax.ShapeDtypeStruct((cols // LANES, n, LANES), F32),
        compiler_params=_cparams(("arbitrary", "arbitrary")),
        name="fourier",
    )(dft_c, dft_s, zc, zs)


FFT_COLS = 256


def _fourier4_kernel(tab_ref, twc_ref, tws_ref, zc_ref, zs_ref, o_ref):
    m = zc_ref.shape[0] // 4
    reps = zc_ref.shape[1] // LANES
    c0, c1, c2, c3 = (zc_ref[j * m:(j + 1) * m, :].astype(F32) for j in range(4))
    s0, s1, s2, s3 = (zs_ref[j * m:(j + 1) * m, :].astype(F32) for j in range(4))
    ce, co, cd, cu = c0 + c2, c1 + c3, c0 - c2, c1 - c3
    se, so, sd, su = s0 + s2, s1 + s3, s0 - s2, s1 - s3

    def emit(br, bi, k):
        if k:
            cos = jnp.concatenate([twc_ref[k - 1]] * reps, axis=1)
            sin = jnp.concatenate([tws_ref[k - 1]] * reps, axis=1)
            br, bi = br * cos + bi * sin, bi * cos - br * sin
        stacked = jnp.concatenate([br.astype(BF16), bi.astype(BF16)], axis=0)
        y = _dot(tab_ref[...], stacked)
        for slab in range(reps):
            o_ref[slab, pl.ds(k, m, stride=4), :] = y[:, slab * LANES:(slab + 1) * LANES]

    emit(ce + co, -(se + so), 0)
    emit(cd - su, -sd - cu, 1)
    emit(ce - co, so - se, 2)
    emit(cd + su, cu - sd, 3)


def _fourier4(tables, zc, zs):
    tab, twc, tws = tables
    n, cols = zc.shape
    m = n // 4
    tc = min(cols, FFT_COLS)
    return pl.pallas_call(
        _fourier4_kernel,
        grid=(cols // tc,),
        in_specs=[
            _const_spec((m, 2 * m)),
            _const_spec((3, m, LANES)),
            _const_spec((3, m, LANES)),
            pl.BlockSpec((n, tc), lambda j: (0, j)),
            pl.BlockSpec((n, tc), lambda j: (0, j)),
        ],
        out_specs=pl.BlockSpec((tc // LANES, n, LANES), lambda j: (j, 0, 0)),
        out_shape=jax.ShapeDtypeStruct((cols // LANES, n, LANES), F32),
        compiler_params=_cparams(("arbitrary",)),
        name="fourier4",
    )(tab, twc, tws, zc, zs)


def _fourier4_tables(n):
    m = n // 4
    idx = (np.arange(m, dtype=np.int64)[:, None] * np.arange(m, dtype=np.int64)[None, :]) % m
    ang = 2.0 * np.pi * idx.astype(np.float64) / m
    tab = np.concatenate([np.cos(ang), np.sin(ang)], axis=1) / np.sqrt(n)
    theta = 2.0 * np.pi * np.arange(m, dtype=np.float64)[None, :] * np.arange(1, 4, dtype=np.float64)[:, None] / n
    bcast = lambda t: jnp.asarray(np.repeat(t[:, :, None], LANES, axis=2), dtype=F32)
    return jnp.asarray(tab, dtype=F32).astype(BF16), bcast(np.cos(theta)), bcast(np.sin(theta))


CONV_ROWS = 256
CONV_HALO = 8


def _head_tile(x, start, lane):
    t, o = divmod(start, LANES)
    tile = lambda i: x[:, i * LANES:(i + 1) * LANES]
    if o == 0:
        return tile(t)
    shift = LANES - o
    out = pltpu.roll(tile(t), shift, 1)
    if shift < M_HEAD_DIM:
        out = jnp.where(lane < shift, out, pltpu.roll(tile(t + 1), shift, 1))
    return out


def _conv_kernel(u_ref, w_ref, q_ref, k_ref, pad_ref):
    n = u_ref.shape[1]
    width = u_ref.shape[2]
    zeros = jnp.zeros((CONV_HALO, width), F32)
    pad_ref[0:CONV_HALO, :] = zeros
    pad_ref[CONV_HALO + n:2 * CONV_HALO + n, :] = zeros
    pad_ref[CONV_HALO:CONV_HALO + n, :] = u_ref[0]
    w = w_ref[0]
    rows = min(CONV_ROWS, n)
    span = rows + 2 * CONV_HALO
    for r in range(n // rows):
        block = pad_ref[r * rows:r * rows + span, :]
        acc = None
        for j in range(K_CONV):
            shift = (K_CONV // 2 - j) % span
            tap = block if shift == 0 else pltpu.roll(block, shift, 0)
            term = tap[CONV_HALO:CONV_HALO + rows, :] * w[j:j + 1, :]
            acc = term if acc is None else acc + term
        half = 0.5 * acc
        act = half + half * jnp.tanh(half)
        lane = lax.broadcasted_iota(jnp.int32, (rows, LANES), 1)
        for hd in range(M_HEADS):
            lanes = slice(hd * HEAD_PAD, (hd + 1) * HEAD_PAD)
            q_tile = _head_tile(act, hd * M_HEAD_DIM, lane)
            k_tile = _head_tile(act, M_WIDTH + hd * M_HEAD_DIM, lane)
            q_ref[0, r * rows:(r + 1) * rows, lanes] = (q_tile * (M_HEAD_DIM ** -0.5)).astype(BF16)
            k_ref[0, r * rows:(r + 1) * rows, lanes] = jnp.where(lane < M_HEAD_DIM, k_tile, 0.0).astype(BF16)


def _conv_silu(mqk, layer, wts):
    nb, n, width = mqk.shape
    return pl.pallas_call(
        _conv_kernel,
        grid=(nb,),
        in_specs=[
            pl.BlockSpec((1, n, width), lambda b: (b, 0, 0)),
            _layer_spec(wts["conv_p"], layer),
        ],
        out_specs=[
            pl.BlockSpec((1, n, MP_WIDTH), lambda b: (b, 0, 0)),
            pl.BlockSpec((1, n, MP_WIDTH), lambda b: (b, 0, 0)),
        ],
        out_shape=[
            jax.ShapeDtypeStruct((nb, n, MP_WIDTH), BF16),
            jax.ShapeDtypeStruct((nb, n, MP_WIDTH), BF16),
        ],
        scratch_shapes=[pltpu.VMEM((n + 2 * CONV_HALO, width), F32)],
        compiler_params=_cparams(("arbitrary",)),
        name="conv_silu",
    )(mqk, wts["conv_p"])


def _log_sigmoid(x):
    return jnp.minimum(x, 0.0) - jnp.log(1.0 + jnp.exp(-jnp.abs(x)))


def _exact_dot_01(a, tri_bf16, a_on_left):
    out = None
    for term in _split3(a):
        d = _dot(term, tri_bf16) if a_on_left else _dot(tri_bf16, term)
        out = d if out is None else out + d
    return out


ONE_ROW = M_HEAD_DIM
(F_R, F_B, F_CM, F_TOT, F_CML, F_MP, F_A, F_WI, F_ELD, F_DEC, F_WK, F_HI, F_MID, F_LO) = range(14)
N_FIELDS = 14
STATE_GROUP = 4
OUT_GROUP = 8
SEL_ROWS = 32


def _mlstm_kernel(ql_ref, kl_ref, vtl_ref, grl_ref,
                  qc_ref, kc_ref, vtc_ref, grc_ref,
                  bir_ref, bfr_ref, sel_ref,
                  hl_ref, hc_ref,
                  ct_st, st_sc, rows_sc, *, ctx_out):
    L = MCH
    nh = M_HEADS
    ncc = qc_ref.shape[1] // L
    ncl = ql_ref.shape[1] // L

    d0 = lax.broadcasted_iota(jnp.int32, (L, L), 0)
    d1 = lax.broadcasted_iota(jnp.int32, (L, L), 1)
    le = d0 <= d1
    ge = d0 >= d1
    tri_le = le.astype(BF16)
    fwd_rows = lax.broadcasted_iota(jnp.int32, (N_STREAM, L), 0) < nh
    feat = lax.broadcasted_iota(jnp.int32, (HEAD_PAD, L), 0)
    keep_rows = feat < M_HEAD_DIM
    ones_tail = (lax.broadcasted_iota(jnp.int32, (HEAD_PAD - M_HEAD_DIM, L), 0) == 0).astype(F32)

    ct_st[...] = jnp.zeros_like(ct_st)

    def field(f, slot0, nc):
        return rows_sc[f, slot0:slot0 + nc].reshape(nc * N_STREAM, L)

    def set_field(f, slot0, nc, val):
        rows_sc[f, slot0:slot0 + nc] = val.reshape(nc, N_STREAM, L)

    def gate_pass(gr_ref, nc, slot0):
        n8 = nc * N_STREAM
        fwd = (lax.broadcasted_iota(jnp.int32, (n8, L), 0) & (N_STREAM - 1)) < nh
        lane = lax.broadcasted_iota(jnp.int32, (n8, L), 1)
        gi = gr_ref[:, 0:N_STREAM, :].reshape(n8, L) + bir_ref[0, 0:n8, :]
        f = _log_sigmoid(gr_ref[:, N_STREAM:2 * N_STREAM, :].reshape(n8, L) + bfr_ref[0, 0:n8, :])
        pre = _exact_dot_01(f, tri_le, a_on_left=True)
        total = jnp.sum(f, axis=1, keepdims=True)
        b = jnp.where(fwd, pre, total - pre + f)
        r = gi - b
        cm = r
        sh = 1
        while sh < L:
            from_left = jnp.where(lane >= sh, pltpu.roll(cm, sh, 1), -jnp.inf)
            from_right = jnp.where(lane < L - sh, pltpu.roll(cm, L - sh, 1), -jnp.inf)
            cm = jnp.maximum(cm, jnp.where(fwd, from_left, from_right))
            sh *= 2
        set_field(F_R, slot0, nc, r)
        set_field(F_B, slot0, nc, b)
        set_field(F_CM, slot0, nc, cm)
        set_field(F_TOT, slot0, nc, jnp.broadcast_to(total, (n8, L)))
        set_field(F_CML, slot0, nc, jnp.broadcast_to(jnp.max(r, axis=1, keepdims=True), (n8, L)))

    def m_scan(nc, slot0, m0):
        def step(j, m):
            sf = slot0 + j
            sb = slot0 + nc - 1 - j
            tot = jnp.where(fwd_rows, rows_sc[F_TOT, sf], rows_sc[F_TOT, sb])
            cml = jnp.where(fwd_rows, rows_sc[F_CML, sf], rows_sc[F_CML, sb])
            rows_sc[F_MP, sf, 0:nh, :] = m[0:nh]
            rows_sc[F_MP, sb, nh:N_STREAM, :] = m[nh:N_STREAM]
            return tot + jnp.maximum(m, cml)

        return lax.fori_loop(0, nc, step, m0)

    def weight_pass(nc, slot0):
        mp, cm, b, r = (field(f, slot0, nc) for f in (F_MP, F_CM, F_B, F_R))
        big = jnp.maximum(mp, field(F_CML, slot0, nc))
        a = -jnp.maximum(mp, cm)
        set_field(F_A, slot0, nc, a)
        set_field(F_WI, slot0, nc, jnp.exp(mp + a))
        set_field(F_ELD, slot0, nc, jnp.exp(a - b))
        set_field(F_DEC, slot0, nc, jnp.exp(mp - big))
        set_field(F_WK, slot0, nc, jnp.exp(r - big))
        for f, term in zip((F_HI, F_MID, F_LO), _split3(r)):
            set_field(f, slot0, nc, term.astype(F32))

    def value_slab(vt_ref, c, hd):
        vt = vt_ref[c, hd * M_HEAD_DIM:(hd + 1) * M_HEAD_DIM, :].astype(F32)
        return jnp.concatenate([vt, ones_tail], axis=0)

    def state_pass(k_ref, vt_ref, nc, slot0):
        group = min(STATE_GROUP, nc)

        def updates_of(j):
            uts = []
            for sidx in range(N_STREAM):
                hd = sidx % nh
                c = j if sidx < nh else nc - 1 - j
                wk = rows_sc[F_WK, slot0 + c, sidx:sidx + 1, :]
                kk = k_ref[0, pl.ds(pl.multiple_of(c * L, L), L), hd * HEAD_PAD:(hd + 1) * HEAD_PAD]
                uts.append(_dot((value_slab(vt_ref, c, hd) * wk).astype(BF16), kk))
            return uts

        def step(g, carry):
            all_uts = [updates_of(g * group + u) for u in range(group)]
            for u in range(group):
                j = g * group + u
                for sidx in range(N_STREAM):
                    slot = slot0 + (j if sidx < nh else nc - 1 - j)
                    prev = ct_st[sidx]
                    st_sc[sidx, slot] = prev.astype(BF16)
                    ct_st[sidx] = rows_sc[F_DEC, slot, sidx:sidx + 1, :] * prev + all_uts[u][sidx]
            return carry

        lax.fori_loop(0, nc // group, step, 0)

    def output_pass(q_ref, k_ref, vt_ref, out_ref, nc, slot0):
        head_lanes = [slice(hd * HEAD_PAD, (hd + 1) * HEAD_PAD) for hd in range(nh)]

        def independent_matmuls(c):
            rows = pl.ds(pl.multiple_of(c * L, L), L)
            slot = slot0 + c
            r3 = jnp.concatenate([rows_sc[F_HI, slot], rows_sc[F_MID, slot], rows_sc[F_LO, slot],
                                  jnp.zeros((N_STREAM, L), F32)], axis=0).astype(BF16)
            qs = [q_ref[0, rows, lanes] for lanes in head_lanes]
            s_ts = [_dot_nt(k_ref[0, rows, lanes], q) for lanes, q in zip(head_lanes, qs)]
            inters = [_dot_nt(st_sc[sidx, slot], qs[sidx % nh]) for sidx in range(N_STREAM)]
            r_all = _dot_tn(r3, sel_ref[...])
            r_ts = [r_all[:, sidx * L:(sidx + 1) * L] for sidx in range(N_STREAM)]
            return s_ts, inters, r_ts

        def finish(c, s_ts, inters, r_ts):
            rows = pl.ds(pl.multiple_of(c * L, L), L)
            slot = slot0 + c
            a_rows = rows_sc[F_A, slot]
            wi_rows = rows_sc[F_WI, slot]
            eld_rows = rows_sc[F_ELD, slot]
            p_ts = []
            for sidx in range(N_STREAM):
                one = slice(sidx, sidx + 1)
                dm = jnp.where(le if sidx < nh else ge, r_ts[sidx] + a_rows[one, :], -jnp.inf)
                p_ts.append((s_ts[sidx % nh] * jnp.exp(dm)).astype(BF16))
            z_pairs = [_dot(value_slab(vt_ref, c, hd).astype(BF16),
                            jnp.concatenate([p_ts[hd], p_ts[nh + hd]], axis=1)) for hd in range(nh)]
            for hd in range(nh):
                hsum = None
                for dr, sidx in enumerate((hd, nh + hd)):
                    one = slice(sidx, sidx + 1)
                    z_t = z_pairs[hd][:, dr * L:(dr + 1) * L] + inters[sidx] * wi_rows[one, :]
                    den = z_t[ONE_ROW:ONE_ROW + 1, :]
                    h_t = z_t * (1.0 / jnp.maximum(jnp.abs(den), eld_rows[one, :]))
                    hsum = h_t if hsum is None else hsum + h_t
                out_ref[0, rows, head_lanes[hd]] = jnp.where(keep_rows, hsum, 0.0).T

        group = min(OUT_GROUP, nc)

        def step(g, carry):
            ahead = independent_matmuls(g * group)
            for u in range(group):
                cur = ahead
                if u + 1 < group:
                    ahead = independent_matmuls(g * group + u + 1)
                finish(g * group + u, *cur)
            return carry

        lax.fori_loop(0, nc // group, step, 0)

    gate_pass(grc_ref, ncc, 0)
    gate_pass(grl_ref, ncl, ncc)
    m1 = m_scan(ncc, 0, jnp.zeros((N_STREAM, L), F32))
    m_scan(ncl, ncc, m1)
    weight_pass(ncc, 0)
    weight_pass(ncl, ncc)
    state_pass(kc_ref, vtc_ref, ncc, 0)
    state_pass(kl_ref, vtl_ref, ncl, ncc)
    if ctx_out:
        output_pass(qc_ref, kc_ref, vtc_ref, hc_ref, ncc, 0)
    else:
        hc_ref[...] = jnp.zeros_like(hc_ref)
    output_pass(ql_ref, kl_ref, vtl_ref, hl_ref, ncl, ncc)


def _mlstm(lat, ctx, layer, wts, sel, ctx_out):
    nb, n, _ = lat[0].shape
    nctx = ctx[0].shape[1]
    assert MCH == LANES == HEAD_PAD
    nct = (n + nctx) // MCH
    assert wts["bir"].shape[1] >= max(n, nctx) // MCH * N_STREAM

    def specs(rows):
        nc = rows // MCH
        seq = lambda w: pl.BlockSpec((1, rows, w), lambda b: (b, 0, 0))
        chunked = lambda rows_: pl.BlockSpec((nc, rows_, MCH), lambda b: (b, 0, 0))
        return [seq(MP_WIDTH), seq(MP_WIDTH), chunked(M_WIDTH), chunked(2 * N_STREAM)]

    out_spec = lambda rows: pl.BlockSpec((1, rows, MP_WIDTH), lambda b: (b, 0, 0))
    return pl.pallas_call(
        functools.partial(_mlstm_kernel, ctx_out=ctx_out),
        grid=(nb,),
        in_specs=specs(n) + specs(nctx) + [
            _layer_spec(wts["bir"], layer), _layer_spec(wts["bfr"], layer),
            _const_spec((SEL_ROWS, N_STREAM * MCH)),
        ],
        out_specs=[out_spec(n), out_spec(nctx)],
        out_shape=[
            jax.ShapeDtypeStruct((nb, n, MP_WIDTH), F32),
            jax.ShapeDtypeStruct((nb, nctx, MP_WIDTH), F32),
        ],
        scratch_shapes=[
            pltpu.VMEM((N_STREAM, HEAD_PAD, HEAD_PAD), F32),
            pltpu.VMEM((N_STREAM, nct, HEAD_PAD, HEAD_PAD), BF16),
            pltpu.VMEM((N_FIELDS, nct, N_STREAM, MCH), F32),
        ],
        compiler_params=_cparams(("arbitrary",)),
        name="mlstm",
    )(*lat, *ctx, wts["bir"], wts["bfr"], sel)


ATTN_SUB = 256
ATTN_HEADS_PER_STEP = 2


def _attn_kernel(*refs, n_sets):
    q_ref = refs[0]
    kv_refs = refs[1:1 + 2 * n_sets]
    o_ref = refs[1 + 2 * n_sets]
    sub = min(ATTN_SUB, q_ref.shape[1])
    n_sub = q_ref.shape[1] // sub
    items = [(hd, t) for hd in range(q_ref.shape[2] // HEAD_PAD) for t in range(n_sub)]

    def scores_of(item):
        hd, t = item
        lanes = slice(hd * HEAD_PAD, (hd + 1) * HEAD_PAD)
        q = q_ref[0, t * sub:(t + 1) * sub, lanes]
        return [_dot_nt(q, kv_refs[2 * i][0, :, lanes]) for i in range(n_sets)]

    nxt = scores_of(items[0])
    for idx, (hd, t) in enumerate(items):
        rows = slice(t * sub, (t + 1) * sub)
        lanes = slice(hd * HEAD_PAD, (hd + 1) * HEAD_PAD)
        scores = nxt
        if idx + 1 < len(items):
            nxt = scores_of(items[idx + 1])
        m = None
        for s in scores:
            sm = jnp.max(s, axis=-1, keepdims=True)
            m = sm if m is None else jnp.maximum(m, sm)
        acc = None
        for i, s in enumerate(scores):
            p = jnp.exp2(s - m)
            o = _dot(p.astype(BF16), kv_refs[2 * i + 1][0, :, lanes])
            acc = o if acc is None else acc + o
        o_ref[0, rows, lanes] = (acc / acc[:, A_V:A_V + 1]).astype(BF16)


def _attention(q, key_sets, tq, heads_per_step):
    nb, n, _ = q.shape
    n_sets = len(key_sets)
    width = heads_per_step * HEAD_PAD
    in_specs = [pl.BlockSpec((1, tq, width), lambda b, h, i: (b, i, h))]
    args = [q]
    for k, v in key_sets:
        nk = k.shape[1]
        spec = pl.BlockSpec((1, nk, width), lambda b, h, i: (b, 0, h))
        in_specs += [spec, spec]
        args += [k, v]
    return pl.pallas_call(
        functools.partial(_attn_kernel, n_sets=n_sets),
        grid=(nb, A_HEADS // heads_per_step, n // tq),
        in_specs=in_specs,
        out_specs=pl.BlockSpec((1, tq, width), lambda b, h, i: (b, i, h)),
        out_shape=jax.ShapeDtypeStruct((nb, n, AP_WIDTH), BF16),
        compiler_params=_cparams(("arbitrary", "arbitrary", "arbitrary")),
        name="attention",
    )(*args)


MLP_CHUNK = 1024


def _out_mlp_kernel(x_ref, yf_ref, hm_ref, mo_ref, ya_ref, mod_ref,
                    gm_ref, g2_ref, gfin_ref, wof_ref, wom_ref, woa_ref, wup_ref, wdn_ref,
                    o_ref, *, final_norm):
    mod = mod_ref[0]
    ga1, sh2, sc2, ga2 = mod[2:3], mod[3:4], mod[4:5], mod[5:6]
    yf = jnp.concatenate([yf_ref[slab].astype(BF16) for slab in range(yf_ref.shape[0])], axis=1)
    mix = _dot(ya_ref[...], woa_ref[0]) + _dot(yf, wof_ref[0])
    gm = gm_ref[0]
    yms = []
    for hd in range(M_HEADS):
        lanes = slice(hd * HEAD_PAD, (hd + 1) * HEAD_PAD)
        hh = hm_ref[:, lanes]
        ms = jnp.sum(hh * hh, axis=-1, keepdims=True) * (1.0 / M_HEAD_DIM)
        ym = hh * lax.rsqrt(ms + EPS) * gm[:, lanes] * jax.nn.sigmoid(mo_ref[:, lanes])
        yms.append(ym.astype(BF16))
    mix = mix + _dot(jnp.concatenate(yms, axis=1), wom_ref[0])
    x1 = x_ref[...] + ga1 * mix
    h2 = (_rms(x1, g2_ref[0]) * (1.0 + sc2) + sh2).astype(BF16)
    acc = None
    for c in range(wup_ref.shape[2] // MLP_CHUNK):
        cols = slice(c * MLP_CHUNK, (c + 1) * MLP_CHUNK)
        u = jnp.maximum(_dot(h2, wup_ref[0, :, cols]), 0.0)
        d = _dot((u * u).astype(BF16), wdn_ref[0, cols, :])
        acc = d if acc is None else acc + d
    x2 = x1 + ga2 * acc
    if final_norm:
        x2 = _rms(x2, gfin_ref[...])
    o_ref[...] = x2


def _out_mlp(x2d, seq, yf, hm, mo, ya, layer, wts, mod, mod_row0, per_batch_mod, gfin, tm, final_norm):
    t, d = x2d.shape
    tiles_per_seq = seq // tm
    tok = lambda w: pl.BlockSpec((tm, w), lambda i: (i, 0))
    yf_spec = pl.BlockSpec((F_WIDTH // LANES, tm, LANES), lambda i: (i // tiles_per_seq, i % tiles_per_seq, 0))
    names = ("gm", "g2", "wof", "wom", "woa", "wup", "wdn")
    lay = {name: _layer_spec(wts[name], layer) for name in names}
    return pl.pallas_call(
        functools.partial(_out_mlp_kernel, final_norm=final_norm),
        grid=(t // tm,),
        in_specs=[
            tok(d), yf_spec, tok(MP_WIDTH), tok(MP_WIDTH), tok(AP_WIDTH),
            pl.BlockSpec((1, 6, d), _mod_map(mod_row0, per_batch_mod, tiles_per_seq)),
            lay["gm"], lay["g2"], _const_spec((1, d)),
            lay["wof"], lay["wom"], lay["woa"], lay["wup"], lay["wdn"],
        ],
        out_specs=tok(d),
        out_shape=jax.ShapeDtypeStruct((t, d), F32),
        compiler_params=_cparams(("arbitrary",)),
        name="out_mlp",
    )(x2d, yf, hm, mo, ya, mod, wts["gm"], wts["g2"], gfin,
      wts["wof"], wts["wom"], wts["woa"], wts["wup"], wts["wdn"])


def _dft_tables(n):
    idx = (np.arange(n, dtype=np.int64)[:, None] * np.arange(n, dtype=np.int64)[None, :]) % n
    ang = 2.0 * np.pi * idx.astype(np.float64) / n
    scale = 1.0 / np.sqrt(n)
    return np.cos(ang) * scale, np.sin(ang) * scale


def _channel_dft():
    c, s = _dft_tables(F_GROUP_DIM)
    eye = np.eye(F_GROUPS)
    return (jnp.asarray(np.kron(eye, c), dtype=F32).astype(BF16),
            jnp.asarray(np.kron(eye, s), dtype=F32).astype(BF16))


def _position_dft(n):
    c, s = _dft_tables(n)
    return jnp.asarray(c, dtype=F32).astype(BF16), jnp.asarray(s, dtype=F32).astype(BF16)


def _rope_tables(n, rotate):
    cos = np.zeros((n, HEAD_PAD), np.float32)
    sin = np.zeros((n, HEAD_PAD), np.float32)
    cos[:, :A_NOPE + A_ROPE] = 1.0
    if rotate:
        nf = A_ROPE // 4
        t = np.arange(n)
        row = (t // GRID_W).astype(np.float32)
        col = (t % GRID_W).astype(np.float32)
        freqs = (np.float32(ROPE_THETA) ** (-np.arange(nf, dtype=np.float32) / np.float32(nf))).astype(np.float32)
        for seg, pos in enumerate((row, col)):
            ang = pos[:, None] * freqs[None, :]
            c, s = np.cos(ang), np.sin(ang)
            base = A_NOPE + seg * 2 * nf
            cos[:, base:base + nf] = c
            cos[:, base + nf:base + 2 * nf] = c
            sin[:, base:base + nf] = -s
            sin[:, base + nf:base + 2 * nf] = s
    return jnp.asarray(cos), jnp.asarray(sin)


def _pad_heads_cols(w, heads, width):
    lead = w.shape[:-1]
    w = w.reshape(lead + (heads, width))
    w = jnp.pad(w, [(0, 0)] * len(lead) + [(0, 0), (0, HEAD_PAD - width)])
    return w.reshape(lead + (heads * HEAD_PAD,))


def _pad_heads_rows(w, heads, width):
    depth, _, n = w.shape
    w = jnp.pad(w.reshape(depth, heads, width, n), [(0, 0), (0, 0), (0, HEAD_PAD - width), (0, 0)])
    return w.reshape(depth, heads * HEAD_PAD, n)


GATE_I_COLS = np.concatenate([np.arange(M_HEADS), 2 * M_HEADS + np.arange(M_HEADS)])
GATE_F_COLS = GATE_I_COLS + M_HEADS


def _prepare_weights(max_chunks, g_norm1, g_norm2, w_in, b_gates, conv_qk, g_mlstm, g_q_norm, g_kv_norm,
                     w_uq, w_ukv, w_out, w_up, w_down):
    offs = np.cumsum([0, F_WIDTH, M_WIDTH, M_WIDTH, M_WIDTH, M_WIDTH, 4 * M_HEADS, Q_LORA, KV_LORA, A_ROPE])
    part = lambda i: w_in[:, :, offs[i]:offs[i + 1]]
    heads = lambda w: _pad_heads_cols(w, M_HEADS, M_HEAD_DIM)
    w_kv = jnp.concatenate([part(7), jnp.pad(part(8), [(0, 0), (0, 0), (A_NOPE, LANES - A_NOPE - A_ROPE)])], axis=2)
    in_groups = dict(w_pf=part(0), w_mqk=w_in[:, :, offs[1]:offs[3]], w_mo=heads(part(4)), w_cq=part(6), w_kv=w_kv)
    in_groups = {name: w.astype(BF16) for name, w in in_groups.items()}
    gates = part(5)
    w_vt = jnp.concatenate([part(3), gates[:, :, GATE_I_COLS], gates[:, :, GATE_F_COLS]],
                           axis=2).astype(BF16).transpose(0, 2, 1)

    conv_p = jnp.pad(conv_qk, [(0, 0), (0, 8 - K_CONV), (0, 0)])

    tile_rows = lambda b: jnp.tile(b[:, :, None], (1, max_chunks, 1))
    ukv = w_ukv.reshape(w_ukv.shape[0], KV_LORA, A_HEADS, A_NOPE + A_V)
    pad_kv = lambda w: jnp.pad(w, [(0, 0), (0, 0), (0, 0), (0, HEAD_PAD - w.shape[-1])]).reshape(
        w.shape[0], KV_LORA, AP_WIDTH).astype(BF16)
    vec = lambda g: g[:, None, :]
    return dict(
        g1=vec(g_norm1), g2=vec(g_norm2), gq=vec(g_q_norm), gkv=vec(g_kv_norm),
        **in_groups, w_vt=w_vt, conv_p=conv_p,
        bir=tile_rows(b_gates[:, GATE_I_COLS]), bfr=tile_rows(b_gates[:, GATE_F_COLS]),
        gm=_pad_heads_cols(g_mlstm, M_HEADS, M_HEAD_DIM)[:, None, :],
        wq=_pad_heads_cols(w_uq, A_HEADS, A_NOPE + A_ROPE).astype(BF16),
        wk=pad_kv(ukv[..., :A_NOPE]), wv=pad_kv(ukv[..., A_NOPE:]),
        wof=w_out[:, :F_WIDTH].astype(BF16),
        wom=_pad_heads_rows(w_out[:, F_WIDTH:F_WIDTH + M_WIDTH], M_HEADS, M_HEAD_DIM).astype(BF16),
        woa=_pad_heads_rows(w_out[:, F_WIDTH + M_WIDTH:], A_HEADS, A_V).astype(BF16),
        wup=w_up.astype(BF16), wdn=w_down.astype(BF16),
    )


def _stream_selectors():
    sel = np.zeros((SEL_ROWS, N_STREAM, MCH), np.float32)
    for s in range(N_STREAM):
        for part in range(3):
            sel[part * N_STREAM + s, s, :] = 1.0
    return jnp.asarray(sel.reshape(SEL_ROWS, N_STREAM * MCH), dtype=BF16)


def kernel(x, c, ctx, c_ctx, w_mod, b_mod, g_norm1, g_norm2, w_in, b_gates, conv_qk, g_mlstm,
           g_q_norm, g_kv_norm, w_uq, w_ukv, w_out, w_up, w_down, g_final):
    nb, seq, d = x.shape
    nctx = ctx.shape[1]
    depth = w_mod.shape[0]
    assert d == D_MODEL and seq % (4 * MCH) == 0 and nctx % MCH == 0
    sel = _stream_selectors()

    tm = min(TOKEN_TILE, seq)
    tm_mlp = min(TOKEN_TILE, seq)
    tm_ctx = min(CTX_TILE, nctx)
    tq = min(ATTN_SUBTILES * ATTN_SUB, seq)
    tq_ctx = min(ATTN_SUB, nctx)

    dft_cc, dft_cs = _channel_dft()
    fft_lat = _fourier4_tables(seq)
    dft_ctx = _position_dft(nctx)
    rope_lat = _rope_tables(seq, True)
    rope_ctx = _rope_tables(nctx, False)
    wts = _prepare_weights(max(seq, nctx) // MCH, g_norm1, g_norm2, w_in, b_gates, conv_qk, g_mlstm,
                           g_q_norm, g_kv_norm, w_uq, w_ukv, w_out, w_up, w_down)
    gfin = g_final.reshape(1, d)

    rows = ((nb + 1 + 7) // 8) * 8
    cc = jnp.concatenate([c, c_ctx[None, :], jnp.zeros((rows - nb - 1, d), F32)], axis=0)
    mod_all = _modulation(cc, w_mod, b_mod).reshape(depth * rows, 6, d)

    xl = x.reshape(nb * seq, d)
    xc = ctx.reshape(nb * nctx, d)

    for l in range(depth):
        last = l == depth - 1
        row_lat, row_ctx = l * rows, l * rows + nb

        zc, zs, mqk, vt, mo, gr, q_a, k_a, v_a = _inproj(
            xl, seq, l, wts, mod_all, row_lat, True, dft_cc, dft_cs, *rope_lat, tm, True)
        zc_c, zs_c, mqk_c, vt_c, mo_c, gr_c, q_ac, k_ac, v_ac = _inproj(
            xc, nctx, l, wts, mod_all, row_ctx, False, dft_cc, dft_cs, *rope_ctx, tm_ctx, not last)

        yf = _fourier4(fft_lat, zc, zs)

        def mlstm_inputs(mqk_s, vt_s, gr_s, n):
            q_s, k_s = _conv_silu(mqk_s.reshape(nb, n, 2 * M_WIDTH), l, wts)
            return (q_s, k_s, vt_s, gr_s)

        hm, hm_c = _mlstm(mlstm_inputs(mqk, vt, gr, seq), mlstm_inputs(mqk_c, vt_c, gr_c, nctx), l, wts, sel,
                          ctx_out=not last)

        b3 = lambda a, n: a.reshape(nb, n, AP_WIDTH)
        keys_ctx = (b3(k_ac, nctx), b3(v_ac, nctx))
        ya = _attention(b3(q_a, seq), [(b3(k_a, seq), b3(v_a, seq)), keys_ctx], tq, ATTN_HEADS_PER_STEP)

        xl = _out_mlp(xl, seq, yf, hm.reshape(nb * seq, MP_WIDTH), mo, ya.reshape(nb * seq, AP_WIDTH),
                      l, wts, mod_all, row_lat, True, gfin, tm_mlp, last)

        if not last:
            yf_c = _fourier(*dft_ctx, zc_c, zs_c)
            ya_c = _attention(b3(q_ac, nctx), [keys_ctx], tq_ctx, A_HEADS)
            xc = _out_mlp(xc, nctx, yf_c, hm_c.reshape(nb * nctx, MP_WIDTH), mo_c,
                          ya_c.reshape(nb * nctx, AP_WIDTH), l, wts, mod_all, row_ctx, False, gfin, tm_ctx, False)

    return xl.reshape(nb, seq, d)
```

```python
import functools

import numpy as np
import jax
import jax.numpy as jnp
from jax import lax
from jax.experimental import pallas as pl
from jax.experimental.pallas import tpu as pltpu

D_MODEL = 1024
GRID_W = 64
EPS = 1e-6
F_GROUPS = 4
F_GROUP_DIM = D_MODEL // 16
F_WIDTH = F_GROUPS * F_GROUP_DIM
M_HEADS = 4
M_HEAD_DIM = 3 * D_MODEL // 32
M_WIDTH = M_HEADS * M_HEAD_DIM
K_CONV = 5
A_HEADS = 4
A_NOPE = 64
A_ROPE = 32
A_V = 3 * D_MODEL // 32
Q_LORA = D_MODEL // 4
KV_LORA = D_MODEL // 8
ROPE_THETA = 10000.0
MLP_HIDDEN = 4 * D_MODEL

LANES = 128
HEAD_PAD = 128
MP_WIDTH = M_HEADS * HEAD_PAD
AP_WIDTH = A_HEADS * HEAD_PAD
VMEM_LIMIT = 56 * 1024 * 1024
MCH = 128
N_STREAM = 2 * M_HEADS
INPROJ_TILE = 1024
TOKEN_TILE = 512
CTX_TILE = 256
ATTN_SUBTILES = 8

BF16 = jnp.bfloat16
F32 = jnp.float32
LOG2_E = 1.4426950408889634


def _cparams(sem):
    return pltpu.CompilerParams(dimension_semantics=sem, vmem_limit_bytes=VMEM_LIMIT)


def _const_spec(shape):
    nd = len(shape)
    return pl.BlockSpec(shape, lambda *_: (0,) * nd, pipeline_mode=pl.Buffered(1))


def _layer_spec(arr, layer):
    nd = arr.ndim
    return pl.BlockSpec((1,) + arr.shape[1:], lambda *_: (layer,) + (0,) * (nd - 1), pipeline_mode=pl.Buffered(1))


def _split3(a):
    hi = a.astype(BF16)
    r1 = a - hi.astype(F32)
    mid = r1.astype(BF16)
    lo = (r1 - mid.astype(F32)).astype(BF16)
    return hi, mid, lo


def _dot(a, b):
    return jnp.dot(a, b, preferred_element_type=F32)


def _dot_nt(a, b):
    return lax.dot_general(a, b, (((1,), (1,)), ((), ())), preferred_element_type=F32)


def _dot_tn(a, b):
    return lax.dot_general(a, b, (((0,), (0,)), ((), ())), preferred_element_type=F32)


def _rms(x, g):
    return x * lax.rsqrt(jnp.mean(x * x, axis=-1, keepdims=True) + EPS) * g


MOD_COLS = 1536


def _mod_kernel(c_ref, w_ref, b_ref, o_ref):
    c = c_ref[...]
    a = c * jax.nn.sigmoid(c)
    a_hi = a.astype(BF16)
    a_lo = (a - a_hi.astype(F32)).astype(BF16)
    w = w_ref[0]
    w_hi = w.astype(BF16)
    w_lo = (w - w_hi.astype(F32)).astype(BF16)
    acc = _dot(a_hi, w_hi) + _dot(a_hi, w_lo) + _dot(a_lo, w_hi)
    o_ref[0] = acc + b_ref[0]


def _modulation(cc, w_mod, b_mod):
    depth, d, n = w_mod.shape
    rows = cc.shape[0]
    tn = MOD_COLS
    return pl.pallas_call(
        _mod_kernel,
        grid=(depth, n // tn),
        in_specs=[
            pl.BlockSpec((rows, d), lambda l, j: (0, 0)),
            pl.BlockSpec((1, d, tn), lambda l, j: (l, 0, j)),
            pl.BlockSpec((1, 1, tn), lambda l, j: (l, 0, j)),
        ],
        out_specs=pl.BlockSpec((1, rows, tn), lambda l, j: (l, 0, j)),
        out_shape=jax.ShapeDtypeStruct((depth, rows, n), F32),
        compiler_params=_cparams(("arbitrary", "arbitrary")),
        name="modulation",
    )(cc, w_mod, b_mod.reshape(depth, 1, n))


def _rope(x, cos, sin, first_half):
    half = A_ROPE // 4
    partner = jnp.where(first_half, pltpu.roll(x, LANES - half, 1), pltpu.roll(x, half, 1))
    return x * cos + partner * sin


def _inproj_kernel(x_ref, g_ref, mod_ref, wpf_ref, wmqk_ref, wmo_ref, wcq_ref, wkv_ref, wvt_ref, cc_ref, cs_ref,
                   cos_ref, sin_ref, gq_ref, gkv_ref, wq_ref, wk_ref, wv_ref,
                   zc_ref, zs_ref, mqk_ref, vt_ref, mo_ref, gr_ref, qa_ref, ka_ref, va_ref, *, with_q):
    x = x_ref[...]
    mod = mod_ref[0]
    h = _rms(x, g_ref[0]) * (1.0 + mod[1:2]) + mod[0:1]
    hb = h.astype(BF16)

    def proj(w_ref):
        return _dot(hb, w_ref[0])

    ckv_kr = proj(wkv_ref)
    cq = proj(wcq_ref) if with_q else None
    pf = proj(wpf_ref).astype(BF16) if with_q else None

    mqk_ref[...] = proj(wmqk_ref)

    cos = cos_ref[...]
    sin = sin_ref[...]
    lane = lax.broadcasted_iota(jnp.int32, cos.shape, 1)
    first_half = ((lane - A_NOPE) & (A_ROPE // 2 - 1)) < A_ROPE // 4
    kvn = _rms(ckv_kr[:, :KV_LORA], gkv_ref[0]).astype(BF16)
    k_rope = _rope(ckv_kr[:, KV_LORA:], cos, sin, first_half)
    k_nope = _dot(kvn, wk_ref[0])
    va = _dot(kvn, wv_ref[0])
    head_lane = lax.broadcasted_iota(jnp.int32, va.shape, 1) & (HEAD_PAD - 1)
    va_ref[...] = jnp.where(head_lane == A_V, 1.0, va).astype(BF16)
    for hd in range(A_HEADS):
        lanes = slice(hd * HEAD_PAD, (hd + 1) * HEAD_PAD)
        ka_ref[:, lanes] = (k_nope[:, lanes] + k_rope).astype(BF16)
    if with_q:
        qn = _rms(cq, gq_ref[0]).astype(BF16)
        q_all = _dot(qn, wq_ref[0])
        q_raw = [q_all[:, hd * HEAD_PAD:(hd + 1) * HEAD_PAD] for hd in range(A_HEADS)]
        zc_ref[...] = _dot(pf, cc_ref[...]).astype(BF16)
        zs_ref[...] = _dot(pf, cs_ref[...]).astype(BF16)

    vg = _dot_nt(wvt_ref[0], hb)
    vt = vg[:M_WIDTH].astype(BF16)
    for j in range(vt_ref.shape[0]):
        vt_ref[j] = vt[:, j * MCH:(j + 1) * MCH]
        gr_ref[j] = vg[M_WIDTH:, j * MCH:(j + 1) * MCH]

    if with_q:
        mo_ref[...] = proj(wmo_ref)
        scale = (A_NOPE + A_ROPE) ** -0.5 * LOG2_E
        for hd in range(A_HEADS):
            q = _rope(q_raw[hd], cos, sin, first_half)
            qa_ref[:, hd * HEAD_PAD:(hd + 1) * HEAD_PAD] = (q * scale).astype(BF16)
    else:
        for ref in (zc_ref, zs_ref, mo_ref, qa_ref):
            ref[...] = jnp.zeros_like(ref)


def _mod_map(mod_row0, per_batch_mod, tiles_per_seq):
    if per_batch_mod:
        return lambda i: (mod_row0 + i // tiles_per_seq, 0, 0)
    return lambda i: (mod_row0, 0, 0)


def _inproj(x2d, seq, layer, wts, mod, mod_row0, per_batch_mod, dft_cc, dft_cs, cos, sin, tm, with_q):
    t, d = x2d.shape
    nb = t // seq
    tiles_per_seq = seq // tm
    tok = lambda w: pl.BlockSpec((tm, w), lambda i: (i, 0))
    z_spec = pl.BlockSpec((tm, F_WIDTH), lambda i: (i % tiles_per_seq, i // tiles_per_seq))
    pos = pl.BlockSpec((tm, LANES), lambda i: (i % tiles_per_seq, 0))
    heads_bf16 = jax.ShapeDtypeStruct((t, AP_WIDTH), BF16)
    shapes = [
        jax.ShapeDtypeStruct((seq, nb * F_WIDTH), BF16),
        jax.ShapeDtypeStruct((seq, nb * F_WIDTH), BF16),
        jax.ShapeDtypeStruct((t, 2 * M_WIDTH), F32),
        jax.ShapeDtypeStruct((t // MCH, M_WIDTH, MCH), BF16),
        jax.ShapeDtypeStruct((t, MP_WIDTH), F32),
        jax.ShapeDtypeStruct((t // MCH, 2 * N_STREAM, MCH), F32),
        heads_bf16, heads_bf16, heads_bf16,
    ]
    vt_spec = pl.BlockSpec((tm // MCH, M_WIDTH, MCH), lambda i: (i, 0, 0))
    gr_spec = pl.BlockSpec((tm // MCH, 2 * N_STREAM, MCH), lambda i: (i, 0, 0))
    out_specs = [z_spec, z_spec, tok(2 * M_WIDTH), vt_spec, tok(MP_WIDTH), gr_spec,
                 tok(AP_WIDTH), tok(AP_WIDTH), tok(AP_WIDTH)]
    lay = lambda name: _layer_spec(wts[name], layer)
    return pl.pallas_call(
        functools.partial(_inproj_kernel, with_q=with_q),
        grid=(t // tm,),
        in_specs=[
            tok(d),
            lay("g1"),
            pl.BlockSpec((1, 6, d), _mod_map(mod_row0, per_batch_mod, tiles_per_seq)),
            lay("w_pf"), lay("w_mqk"), lay("w_mo"), lay("w_cq"), lay("w_kv"), lay("w_vt"),
            _const_spec((F_WIDTH, F_WIDTH)),
            _const_spec((F_WIDTH, F_WIDTH)),
            pos, pos,
            lay("gq"), lay("gkv"), lay("wq"), lay("wk"), lay("wv"),
        ],
        out_specs=out_specs,
        out_shape=shapes,
        compiler_params=_cparams(("arbitrary",)),
        name="inproj",
    )(x2d, wts["g1"], mod, wts["w_pf"], wts["w_mqk"], wts["w_mo"], wts["w_cq"], wts["w_kv"], wts["w_vt"],
      dft_cc, dft_cs, cos, sin,
      wts["gq"], wts["gkv"], wts["wq"], wts["wk"], wts["wv"])


def _fourier_kernel(c_ref, s_ref, zc_ref, zs_ref, o_ref):
    y = _dot(c_ref[...], zc_ref[...]) - _dot(s_ref[...], zs_ref[...])
    for slab in range(o_ref.shape[0]):
        o_ref[slab] = y[:, slab * LANES:(slab + 1) * LANES]


def _fourier(dft_c, dft_s, zc, zs):
    n, cols = zc.shape
    tr = min(n, 512)
    tc = min(cols, 512)
    return pl.pallas_call(
        _fourier_kernel,
        grid=(n // tr, cols // tc),
        in_specs=[
            pl.BlockSpec((tr, n), lambda i, j: (i, 0)),
            pl.BlockSpec((tr, n), lambda i, j: (i, 0)),
            pl.BlockSpec((n, tc), lambda i, j: (0, j)),
            pl.BlockSpec((n, tc), lambda i, j: (0, j)),
        ],
        out_specs=pl.BlockSpec((tc // LANES, tr, LANES), lambda i, j: (j, i, 0)),
        out_shape=jax.ShapeDtypeStruct((cols // LANES, n, LANES), F32),
        compiler_params=_cparams(("arbitrary", "arbitrary")),
        name="fourier",
    )(dft_c, dft_s, zc, zs)


FFT_COLS = 256


def _fourier4_kernel(tab_ref, twc_ref, tws_ref, zc_ref, zs_ref, o_ref):
    m = zc_ref.shape[0] // 4
    reps = zc_ref.shape[1] // LANES
    c0, c1, c2, c3 = (zc_ref[j * m:(j + 1) * m, :].astype(F32) for j in range(4))
    s0, s1, s2, s3 = (zs_ref[j * m:(j + 1) * m, :].astype(F32) for j in range(4))
    ce, co, cd, cu = c0 + c2, c1 + c3, c0 - c2, c1 - c3
    se, so, sd, su = s0 + s2, s1 + s3, s0 - s2, s1 - s3

    def emit(br, bi, k):
        if k:
            cos = jnp.concatenate([twc_ref[k - 1]] * reps, axis=1)
            sin = jnp.concatenate([tws_ref[k - 1]] * reps, axis=1)
            br, bi = br * cos + bi * sin, bi * cos - br * sin
        stacked = jnp.concatenate([br.astype(BF16), bi.astype(BF16)], axis=0)
        y = _dot(tab_ref[...], stacked)
        for slab in range(reps):
            o_ref[slab, pl.ds(k, m, stride=4), :] = y[:, slab * LANES:(slab + 1) * LANES]

    emit(ce + co, -(se + so), 0)
    emit(cd - su, -sd - cu, 1)
    emit(ce - co, so - se, 2)
    emit(cd + su, cu - sd, 3)


def _fourier4(tables, zc, zs):
    tab, twc, tws = tables
    n, cols = zc.shape
    m = n // 4
    tc = min(cols, FFT_COLS)
    return pl.pallas_call(
        _fourier4_kernel,
        grid=(cols // tc,),
        in_specs=[
            _const_spec((m, 2 * m)),
            _const_spec((3, m, LANES)),
            _const_spec((3, m, LANES)),
            pl.BlockSpec((n, tc), lambda j: (0, j)),
            pl.BlockSpec((n, tc), lambda j: (0, j)),
        ],
        out_specs=pl.BlockSpec((tc // LANES, n, LANES), lambda j: (j, 0, 0)),
        out_shape=jax.ShapeDtypeStruct((cols // LANES, n, LANES), F32),
        compiler_params=_cparams(("arbitrary",)),
        name="fourier4",
    )(tab, twc, tws, zc, zs)


def _fourier4_tables(n):
    m = n // 4
    idx = (np.arange(m, dtype=np.int64)[:, None] * np.arange(m, dtype=np.int64)[None, :]) % m
    ang = 2.0 * np.pi * idx.astype(np.float64) / m
    tab = np.concatenate([np.cos(ang), np.sin(ang)], axis=1) / np.sqrt(n)
    theta = 2.0 * np.pi * np.arange(m, dtype=np.float64)[None, :] * np.arange(1, 4, dtype=np.float64)[:, None] / n
    bcast = lambda t: jnp.asarray(np.repeat(t[:, :, None], LANES, axis=2), dtype=F32)
    return jnp.asarray(tab, dtype=F32).astype(BF16), bcast(np.cos(theta)), bcast(np.sin(theta))


CONV_ROWS = 256
CONV_HALO = 8


def _head_tile(x, start, lane):
    t, o = divmod(start, LANES)
    tile = lambda i: x[:, i * LANES:(i + 1) * LANES]
    if o == 0:
        return tile(t)
    shift = LANES - o
    out = pltpu.roll(tile(t), shift, 1)
    if shift < M_HEAD_DIM:
        out = jnp.where(lane < shift, out, pltpu.roll(tile(t + 1), shift, 1))
    return out


def _conv_kernel(u_ref, w_ref, q_ref, k_ref, pad_ref):
    n = u_ref.shape[1]
    width = u_ref.shape[2]
    zeros = jnp.zeros((CONV_HALO, width), F32)
    pad_ref[0:CONV_HALO, :] = zeros
    pad_ref[CONV_HALO + n:2 * CONV_HALO + n, :] = zeros
    pad_ref[CONV_HALO:CONV_HALO + n, :] = u_ref[0]
    w = w_ref[0]
    rows = min(CONV_ROWS, n)
    span = rows + 2 * CONV_HALO
    for r in range(n // rows):
        block = pad_ref[r * rows:r * rows + span, :]
        acc = None
        for j in range(K_CONV):
            shift = (K_CONV // 2 - j) % span
            tap = block if shift == 0 else pltpu.roll(block, shift, 0)
            term = tap[CONV_HALO:CONV_HALO + rows, :] * w[j:j + 1, :]
            acc = term if acc is None else acc + term
        half = 0.5 * acc
        act = half + half * jnp.tanh(half)
        lane = lax.broadcasted_iota(jnp.int32, (rows, LANES), 1)
        for hd in range(M_HEADS):
            lanes = slice(hd * HEAD_PAD, (hd + 1) * HEAD_PAD)
            q_tile = _head_tile(act, hd * M_HEAD_DIM, lane)
            k_tile = _head_tile(act, M_WIDTH + hd * M_HEAD_DIM, lane)
            q_ref[0, r * rows:(r + 1) * rows, lanes] = (q_tile * (M_HEAD_DIM ** -0.5)).astype(BF16)
            k_ref[0, r * rows:(r + 1) * rows, lanes] = jnp.where(lane < M_HEAD_DIM, k_tile, 0.0).astype(BF16)


def _conv_silu(mqk, layer, wts):
    nb, n, width = mqk.shape
    return pl.pallas_call(
        _conv_kernel,
        grid=(nb,),
        in_specs=[
            pl.BlockSpec((1, n, width), lambda b: (b, 0, 0)),
            _layer_spec(wts["conv_p"], layer),
        ],
        out_specs=[
            pl.BlockSpec((1, n, MP_WIDTH), lambda b: (b, 0, 0)),
            pl.BlockSpec((1, n, MP_WIDTH), lambda b: (b, 0, 0)),
        ],
        out_shape=[
            jax.ShapeDtypeStruct((nb, n, MP_WIDTH), BF16),
            jax.ShapeDtypeStruct((nb, n, MP_WIDTH), BF16),
        ],
        scratch_shapes=[pltpu.VMEM((n + 2 * CONV_HALO, width), F32)],
        compiler_params=_cparams(("arbitrary",)),
        name="conv_silu",
    )(mqk, wts["conv_p"])


def _log_sigmoid(x):
    return jnp.minimum(x, 0.0) - jnp.log(1.0 + jnp.exp(-jnp.abs(x)))


def _exact_dot_01(a, tri_bf16, a_on_left):
    out = None
    for term in _split3(a):
        d = _dot(term, tri_bf16) if a_on_left else _dot(tri_bf16, term)
        out = d if out is None else out + d
    return out


ONE_ROW = M_HEAD_DIM
(F_R, F_B, F_CM, F_TOT, F_CML, F_MP, F_A, F_WI, F_ELD, F_DEC, F_WK, F_HI, F_MID, F_LO) = range(14)
N_FIELDS = 14
STATE_GROUP = 4
OUT_GROUP = 8
SEL_ROWS = 32


def _mlstm_kernel(ql_ref, kl_ref, vtl_ref, grl_ref,
                  qc_ref, kc_ref, vtc_ref, grc_ref,
                  bir_ref, bfr_ref, sel_ref,
                  hl_ref, hc_ref,
                  ct_st, st_sc, rows_sc, *, ctx_out):
    L = MCH
    nh = M_HEADS
    ncc = qc_ref.shape[1] // L
    ncl = ql_ref.shape[1] // L

    d0 = lax.broadcasted_iota(jnp.int32, (L, L), 0)
    d1 = lax.broadcasted_iota(jnp.int32, (L, L), 1)
    le = d0 <= d1
    ge = d0 >= d1
    tri_le = le.astype(BF16)
    fwd_rows = lax.broadcasted_iota(jnp.int32, (N_STREAM, L), 0) < nh
    feat = lax.broadcasted_iota(jnp.int32, (HEAD_PAD, L), 0)
    keep_rows = feat < M_HEAD_DIM
    ones_tail = (lax.broadcasted_iota(jnp.int32, (HEAD_PAD - M_HEAD_DIM, L), 0) == 0).astype(F32)

    ct_st[...] = jnp.zeros_like(ct_st)

    def field(f, slot0, nc):
        return rows_sc[f, slot0:slot0 + nc].reshape(nc * N_STREAM, L)

    def set_field(f, slot0, nc, val):
        rows_sc[f, slot0:slot0 + nc] = val.reshape(nc, N_STREAM, L)

    def gate_pass(gr_ref, nc, slot0):
        n8 = nc * N_STREAM
        fwd = (lax.broadcasted_iota(jnp.int32, (n8, L), 0) & (N_STREAM - 1)) < nh
        lane = lax.broadcasted_iota(jnp.int32, (n8, L), 1)
        gi = gr_ref[:, 0:N_STREAM, :].reshape(n8, L) + bir_ref[0, 0:n8, :]
        f = _log_sigmoid(gr_ref[:, N_STREAM:2 * N_STREAM, :].reshape(n8, L) + bfr_ref[0, 0:n8, :])
        pre = _exact_dot_01(f, tri_le, a_on_left=True)
        total = jnp.sum(f, axis=1, keepdims=True)
        b = jnp.where(fwd, pre, total - pre + f)
        r = gi - b
        cm = r
        sh = 1
        while sh < L:
            from_left = jnp.where(lane >= sh, pltpu.roll(cm, sh, 1), -jnp.inf)
            from_right = jnp.where(lane < L - sh, pltpu.roll(cm, L - sh, 1), -jnp.inf)
            cm = jnp.maximum(cm, jnp.where(fwd, from_left, from_right))
            sh *= 2
        set_field(F_R, slot0, nc, r)
        set_field(F_B, slot0, nc, b)
        set_field(F_CM, slot0, nc, cm)
        set_field(F_TOT, slot0, nc, jnp.broadcast_to(total, (n8, L)))
        set_field(F_CML, slot0, nc, jnp.broadcast_to(jnp.max(r, axis=1, keepdims=True), (n8, L)))

    def m_scan(nc, slot0, m0):
        def step(j, m):
            sf = slot0 + j
            sb = slot0 + nc - 1 - j
            tot = jnp.where(fwd_rows, rows_sc[F_TOT, sf], rows_sc[F_TOT, sb])
            cml = jnp.where(fwd_rows, rows_sc[F_CML, sf], rows_sc[F_CML, sb])
            rows_sc[F_MP, sf, 0:nh, :] = m[0:nh]
            rows_sc[F_MP, sb, nh:N_STREAM, :] = m[nh:N_STREAM]
            return tot + jnp.maximum(m, cml)

        return lax.fori_loop(0, nc, step, m0)

    def weight_pass(nc, slot0):
        mp, cm, b, r = (field(f, slot0, nc) for f in (F_MP, F_CM, F_B, F_R))
        big = jnp.maximum(mp, field(F_CML, slot0, nc))
        a = -jnp.maximum(mp, cm)
        set_field(F_A, slot0, nc, a)
        set_field(F_WI, slot0, nc, jnp.exp(mp + a))
        set_field(F_ELD, slot0, nc, jnp.exp(a - b))
        set_field(F_DEC, slot0, nc, jnp.exp(mp - big))
        set_field(F_WK, slot0, nc, jnp.exp(r - big))
        for f, term in zip((F_HI, F_MID, F_LO), _split3(r)):
            set_field(f, slot0, nc, term.astype(F32))

    def value_slab(vt_ref, c, hd):
        vt = vt_ref[c, hd * M_HEAD_DIM:(hd + 1) * M_HEAD_DIM, :].astype(F32)
        return jnp.concatenate([vt, ones_tail], axis=0)

    def state_pass(k_ref, vt_ref, nc, slot0):
        group = min(STATE_GROUP, nc)

        def updates_of(j):
            uts = []
            for sidx in range(N_STREAM):
                hd = sidx % nh
                c = j if sidx < nh else nc - 1 - j
                wk = rows_sc[F_WK, slot0 + c, sidx:sidx + 1, :]
                kk = k_ref[0, pl.ds(pl.multiple_of(c * L, L), L), hd * HEAD_PAD:(hd + 1) * HEAD_PAD]
                uts.append(_dot((value_slab(vt_ref, c, hd) * wk).astype(BF16), kk))
            return uts

        def step(g, carry):
            all_uts = [updates_of(g * group + u) for u in range(group)]
            for u in range(group):
                j = g * group + u
                for sidx in range(N_STREAM):
                    slot = slot0 + (j if sidx < nh else nc - 1 - j)
                    prev = ct_st[sidx]
                    st_sc[sidx, slot] = prev.astype(BF16)
                    ct_st[sidx] = rows_sc[F_DEC, slot, sidx:sidx + 1, :] * prev + all_uts[u][sidx]
            return carry

        lax.fori_loop(0, nc // group, step, 0)

    def output_pass(q_ref, k_ref, vt_ref, out_ref, nc, slot0):
        head_lanes = [slice(hd * HEAD_PAD, (hd + 1) * HEAD_PAD) for hd in range(nh)]

        def independent_matmuls(c):
            rows = pl.ds(pl.multiple_of(c * L, L), L)
            slot = slot0 + c
            r3 = jnp.concatenate([rows_sc[F_HI, slot], rows_sc[F_MID, slot], rows_sc[F_LO, slot],
                                  jnp.zeros((N_STREAM, L), F32)], axis=0).astype(BF16)
            qs = [q_ref[0, rows, lanes] for lanes in head_lanes]
            s_ts = [_dot_nt(k_ref[0, rows, lanes], q) for lanes, q in zip(head_lanes, qs)]
            inters = [_dot_nt(st_sc[sidx, slot], qs[sidx % nh]) for sidx in range(N_STREAM)]
            r_all = _dot_tn(r3, sel_ref[...])
            r_ts = [r_all[:, sidx * L:(sidx + 1) * L] for sidx in range(N_STREAM)]
            return s_ts, inters, r_ts

        def finish(c, s_ts, inters, r_ts):
            rows = pl.ds(pl.multiple_of(c * L, L), L)
            slot = slot0 + c
            a_rows = rows_sc[F_A, slot]
            wi_rows = rows_sc[F_WI, slot]
            eld_rows = rows_sc[F_ELD, slot]
            p_ts = []
            for sidx in range(N_STREAM):
                one = slice(sidx, sidx + 1)
                dm = jnp.where(le if sidx < nh else ge, r_ts[sidx] + a_rows[one, :], -jnp.inf)
                p_ts.append((s_ts[sidx % nh] * jnp.exp(dm)).astype(BF16))
            z_pairs = [_dot(value_slab(vt_ref, c, hd).astype(BF16),
                            jnp.concatenate([p_ts[hd], p_ts[nh + hd]], axis=1)) for hd in range(nh)]
            for hd in range(nh):
                hsum = None
                for dr, sidx in enumerate((hd, nh + hd)):
                    one = slice(sidx, sidx + 1)
                    z_t = z_pairs[hd][:, dr * L:(dr + 1) * L] + inters[sidx] * wi_rows[one, :]
                    den = z_t[ONE_ROW:ONE_ROW + 1, :]
                    h_t = z_t * (1.0 / jnp.maximum(jnp.abs(den), eld_rows[one, :]))
                    hsum = h_t if hsum is None else hsum + h_t
                out_ref[0, rows, head_lanes[hd]] = jnp.where(keep_rows, hsum, 0.0).T

        group = min(OUT_GROUP, nc)

        def step(g, carry):
            ahead = independent_matmuls(g * group)
            for u in range(group):
                cur = ahead
                if u + 1 < group:
                    ahead = independent_matmuls(g * group + u + 1)
                finish(g * group + u, *cur)
            return carry

        lax.fori_loop(0, nc // group, step, 0)

    gate_pass(grc_ref, ncc, 0)
    gate_pass(grl_ref, ncl, ncc)
    m1 = m_scan(ncc, 0, jnp.zeros((N_STREAM, L), F32))
    m_scan(ncl, ncc, m1)
    weight_pass(ncc, 0)
    weight_pass(ncl, ncc)
    state_pass(kc_ref, vtc_ref, ncc, 0)
    state_pass(kl_ref, vtl_ref, ncl, ncc)
    if ctx_out:
        output_pass(qc_ref, kc_ref, vtc_ref, hc_ref, ncc, 0)
    else:
        hc_ref[...] = jnp.zeros_like(hc_ref)
    output_pass(ql_ref, kl_ref, vtl_ref, hl_ref, ncl, ncc)


def _mlstm(lat, ctx, layer, wts, sel, ctx_out):
    nb, n, _ = lat[0].shape
    nctx = ctx[0].shape[1]
    assert MCH == LANES == HEAD_PAD
    nct = (n + nctx) // MCH
    assert wts["bir"].shape[1] >= max(n, nctx) // MCH * N_STREAM

    def specs(rows):
        nc = rows // MCH
        seq = lambda w: pl.BlockSpec((1, rows, w), lambda b: (b, 0, 0))
        chunked = lambda rows_: pl.BlockSpec((nc, rows_, MCH), lambda b: (b, 0, 0))
        return [seq(MP_WIDTH), seq(MP_WIDTH), chunked(M_WIDTH), chunked(2 * N_STREAM)]

    out_spec = lambda rows: pl.BlockSpec((1, rows, MP_WIDTH), lambda b: (b, 0, 0))
    return pl.pallas_call(
        functools.partial(_mlstm_kernel, ctx_out=ctx_out),
        grid=(nb,),
        in_specs=specs(n) + specs(nctx) + [
            _layer_spec(wts["bir"], layer), _layer_spec(wts["bfr"], layer),
            _const_spec((SEL_ROWS, N_STREAM * MCH)),
        ],
        out_specs=[out_spec(n), out_spec(nctx)],
        out_shape=[
            jax.ShapeDtypeStruct((nb, n, MP_WIDTH), F32),
            jax.ShapeDtypeStruct((nb, nctx, MP_WIDTH), F32),
        ],
        scratch_shapes=[
            pltpu.VMEM((N_STREAM, HEAD_PAD, HEAD_PAD), F32),
            pltpu.VMEM((N_STREAM, nct, HEAD_PAD, HEAD_PAD), BF16),
            pltpu.VMEM((N_FIELDS, nct, N_STREAM, MCH), F32),
        ],
        compiler_params=_cparams(("arbitrary",)),
        name="mlstm",
    )(*lat, *ctx, wts["bir"], wts["bfr"], sel)


ATTN_SUB = 256
ATTN_HEADS_PER_STEP = 2


def _attn_kernel(*refs, n_sets):
    q_ref = refs[0]
    kv_refs = refs[1:1 + 2 * n_sets]
    o_ref = refs[1 + 2 * n_sets]
    sub = min(ATTN_SUB, q_ref.shape[1])
    n_sub = q_ref.shape[1] // sub
    items = [(hd, t) for hd in range(q_ref.shape[2] // HEAD_PAD) for t in range(n_sub)]

    def scores_of(item):
        hd, t = item
        lanes = slice(hd * HEAD_PAD, (hd + 1) * HEAD_PAD)
        q = q_ref[0, t * sub:(t + 1) * sub, lanes]
        return [_dot_nt(q, kv_refs[2 * i][0, :, lanes]) for i in range(n_sets)]

    nxt = scores_of(items[0])
    for idx, (hd, t) in enumerate(items):
        rows = slice(t * sub, (t + 1) * sub)
        lanes = slice(hd * HEAD_PAD, (hd + 1) * HEAD_PAD)
        scores = nxt
        if idx + 1 < len(items):
            nxt = scores_of(items[idx + 1])
        m = None
        for s in scores:
            sm = jnp.max(s, axis=-1, keepdims=True)
            m = sm if m is None else jnp.maximum(m, sm)
        acc = None
        for i, s in enumerate(scores):
            p = jnp.exp2(s - m)
            o = _dot(p.astype(BF16), kv_refs[2 * i + 1][0, :, lanes])
            acc = o if acc is None else acc + o
        o_ref[0, rows, lanes] = (acc / acc[:, A_V:A_V + 1]).astype(BF16)


def _attention(q, key_sets, tq, heads_per_step):
    nb, n, _ = q.shape
    n_sets = len(key_sets)
    width = heads_per_step * HEAD_PAD
    in_specs = [pl.BlockSpec((1, tq, width), lambda b, h, i: (b, i, h))]
    args = [q]
    for k, v in key_sets:
        nk = k.shape[1]
        spec = pl.BlockSpec((1, nk, width), lambda b, h, i: (b, 0, h))
        in_specs += [spec, spec]
        args += [k, v]
    return pl.pallas_call(
        functools.partial(_attn_kernel, n_sets=n_sets),
        grid=(nb, A_HEADS // heads_per_step, n // tq),
        in_specs=in_specs,
        out_specs=pl.BlockSpec((1, tq, width), lambda b, h, i: (b, i, h)),
        out_shape=jax.ShapeDtypeStruct((nb, n, AP_WIDTH), BF16),
        compiler_params=_cparams(("arbitrary", "arbitrary", "arbitrary")),
        name="attention",
    )(*args)


MLP_CHUNK = 1024


def _out_mlp_kernel(x_ref, yf_ref, hm_ref, mo_ref, ya_ref, mod_ref,
                    gm_ref, g2_ref, gfin_ref, wof_ref, wom_ref, woa_ref, wup_ref, wdn_ref,
                    o_ref, *, final_norm):
    mod = mod_ref[0]
    ga1, sh2, sc2, ga2 = mod[2:3], mod[3:4], mod[4:5], mod[5:6]
    yf = jnp.concatenate([yf_ref[slab].astype(BF16) for slab in range(yf_ref.shape[0])], axis=1)
    mix = _dot(ya_ref[...], woa_ref[0]) + _dot(yf, wof_ref[0])
    gm = gm_ref[0]
    yms = []
    for hd in range(M_HEADS):
        lanes = slice(hd * HEAD_PAD, (hd + 1) * HEAD_PAD)
        hh = hm_ref[:, lanes]
        ms = jnp.sum(hh * hh, axis=-1, keepdims=True) * (1.0 / M_HEAD_DIM)
        ym = hh * lax.rsqrt(ms + EPS) * gm[:, lanes] * jax.nn.sigmoid(mo_ref[:, lanes])
        yms.append(ym.astype(BF16))
    mix = mix + _dot(jnp.concatenate(yms, axis=1), wom_ref[0])
    x1 = x_ref[...] + ga1 * mix
    h2 = (_rms(x1, g2_ref[0]) * (1.0 + sc2) + sh2).astype(BF16)
    acc = None
    for c in range(wup_ref.shape[2] // MLP_CHUNK):
        cols = slice(c * MLP_CHUNK, (c + 1) * MLP_CHUNK)
        u = jnp.maximum(_dot(h2, wup_ref[0, :, cols]), 0.0)
        d = _dot((u * u).astype(BF16), wdn_ref[0, cols, :])
        acc = d if acc is None else acc + d
    x2 = x1 + ga2 * acc
    if final_norm:
        x2 = _rms(x2, gfin_ref[...])
    o_ref[...] = x2


def _out_mlp(x2d, seq, yf, hm, mo, ya, layer, wts, mod, mod_row0, per_batch_mod, gfin, tm, final_norm):
    t, d = x2d.shape
    tiles_per_seq = seq // tm
    tok = lambda w: pl.BlockSpec((tm, w), lambda i: (i, 0))
    yf_spec = pl.BlockSpec((F_WIDTH // LANES, tm, LANES), lambda i: (i // tiles_per_seq, i % tiles_per_seq, 0))
    names = ("gm", "g2", "wof", "wom", "woa", "wup", "wdn")
    lay = {name: _layer_spec(wts[name], layer) for name in names}
    return pl.pallas_call(
        functools.partial(_out_mlp_kernel, final_norm=final_norm),
        grid=(t // tm,),
        in_specs=[
            tok(d), yf_spec, tok(MP_WIDTH), tok(MP_WIDTH), tok(AP_WIDTH),
            pl.BlockSpec((1, 6, d), _mod_map(mod_row0, per_batch_mod, tiles_per_seq)),
            lay["gm"], lay["g2"], _const_spec((1, d)),
            lay["wof"], lay["wom"], lay["woa"], lay["wup"], lay["wdn"],
        ],
        out_specs=tok(d),
        out_shape=jax.ShapeDtypeStruct((t, d), F32),
        compiler_params=_cparams(("arbitrary",)),
        name="out_mlp",
    )(x2d, yf, hm, mo, ya, mod, wts["gm"], wts["g2"], gfin,
      wts["wof"], wts["wom"], wts["woa"], wts["wup"], wts["wdn"])


def _dft_tables(n):
    idx = (np.arange(n, dtype=np.int64)[:, None] * np.arange(n, dtype=np.int64)[None, :]) % n
    ang = 2.0 * np.pi * idx.astype(np.float64) / n
    scale = 1.0 / np.sqrt(n)
    return np.cos(ang) * scale, np.sin(ang) * scale


def _channel_dft():
    c, s = _dft_tables(F_GROUP_DIM)
    eye = np.eye(F_GROUPS)
    return (jnp.asarray(np.kron(eye, c), dtype=F32).astype(BF16),
            jnp.asarray(np.kron(eye, s), dtype=F32).astype(BF16))


def _position_dft(n):
    c, s = _dft_tables(n)
    return jnp.asarray(c, dtype=F32).astype(BF16), jnp.asarray(s, dtype=F32).astype(BF16)


def _rope_tables(n, rotate):
    cos = np.zeros((n, HEAD_PAD), np.float32)
    sin = np.zeros((n, HEAD_PAD), np.float32)
    cos[:, :A_NOPE + A_ROPE] = 1.0
    if rotate:
        nf = A_ROPE // 4
        t = np.arange(n)
        row = (t // GRID_W).astype(np.float32)
        col = (t % GRID_W).astype(np.float32)
        freqs = (np.float32(ROPE_THETA) ** (-np.arange(nf, dtype=np.float32) / np.float32(nf))).astype(np.float32)
        for seg, pos in enumerate((row, col)):
            ang = pos[:, None] * freqs[None, :]
            c, s = np.cos(ang), np.sin(ang)
            base = A_NOPE + seg * 2 * nf
            cos[:, base:base + nf] = c
            cos[:, base + nf:base + 2 * nf] = c
            sin[:, base:base + nf] = -s
            sin[:, base + nf:base + 2 * nf] = s
    return jnp.asarray(cos), jnp.asarray(sin)


def _pad_heads_cols(w, heads, width):
    lead = w.shape[:-1]
    w = w.reshape(lead + (heads, width))
    w = jnp.pad(w, [(0, 0)] * len(lead) + [(0, 0), (0, HEAD_PAD - width)])
    return w.reshape(lead + (heads * HEAD_PAD,))


def _pad_heads_rows(w, heads, width):
    depth, _, n = w.shape
    w = jnp.pad(w.reshape(depth, heads, width, n), [(0, 0), (0, 0), (0, HEAD_PAD - width), (0, 0)])
    return w.reshape(depth, heads * HEAD_PAD, n)


GATE_I_COLS = np.concatenate([np.arange(M_HEADS), 2 * M_HEADS + np.arange(M_HEADS)])
GATE_F_COLS = GATE_I_COLS + M_HEADS


def _prepare_weights(max_chunks, g_norm1, g_norm2, w_in, b_gates, conv_qk, g_mlstm, g_q_norm, g_kv_norm,
                     w_uq, w_ukv, w_out, w_up, w_down):
    offs = np.cumsum([0, F_WIDTH, M_WIDTH, M_WIDTH, M_WIDTH, M_WIDTH, 4 * M_HEADS, Q_LORA, KV_LORA, A_ROPE])
    part = lambda i: w_in[:, :, offs[i]:offs[i + 1]]
    heads = lambda w: _pad_heads_cols(w, M_HEADS, M_HEAD_DIM)
    w_kv = jnp.concatenate([part(7), jnp.pad(part(8), [(0, 0), (0, 0), (A_NOPE, LANES - A_NOPE - A_ROPE)])], axis=2)
    in_groups = dict(w_pf=part(0), w_mqk=w_in[:, :, offs[1]:offs[3]], w_mo=heads(part(4)), w_cq=part(6), w_kv=w_kv)
    in_groups = {name: w.astype(BF16) for name, w in in_groups.items()}
    gates = part(5)
    w_vt = jnp.concatenate([part(3), gates[:, :, GATE_I_COLS], gates[:, :, GATE_F_COLS]],
                           axis=2).astype(BF16).transpose(0, 2, 1)

    conv_p = jnp.pad(conv_qk, [(0, 0), (0, 8 - K_CONV), (0, 0)])

    tile_rows = lambda b: jnp.tile(b[:, :, None], (1, max_chunks, 1))
    ukv = w_ukv.reshape(w_ukv.shape[0], KV_LORA, A_HEADS, A_NOPE + A_V)
    pad_kv = lambda w: jnp.pad(w, [(0, 0), (0, 0), (0, 0), (0, HEAD_PAD - w.shape[-1])]).reshape(
        w.shape[0], KV_LORA, AP_WIDTH).astype(BF16)
    vec = lambda g: g[:, None, :]
    return dict(
        g1=vec(g_norm1), g2=vec(g_norm2), gq=vec(g_q_norm), gkv=vec(g_kv_norm),
        **in_groups, w_vt=w_vt, conv_p=conv_p,
        bir=tile_rows(b_gates[:, GATE_I_COLS]), bfr=tile_rows(b_gates[:, GATE_F_COLS]),
        gm=_pad_heads_cols(g_mlstm, M_HEADS, M_HEAD_DIM)[:, None, :],
        wq=_pad_heads_cols(w_uq, A_HEADS, A_NOPE + A_ROPE).astype(BF16),
        wk=pad_kv(ukv[..., :A_NOPE]), wv=pad_kv(ukv[..., A_NOPE:]),
        wof=w_out[:, :F_WIDTH].astype(BF16),
        wom=_pad_heads_rows(w_out[:, F_WIDTH:F_WIDTH + M_WIDTH], M_HEADS, M_HEAD_DIM).astype(BF16),
        woa=_pad_heads_rows(w_out[:, F_WIDTH + M_WIDTH:], A_HEADS, A_V).astype(BF16),
        wup=w_up.astype(BF16), wdn=w_down.astype(BF16),
    )


def _stream_selectors():
    sel = np.zeros((SEL_ROWS, N_STREAM, MCH), np.float32)
    for s in range(N_STREAM):
        for part in range(3):
            sel[part * N_STREAM + s, s, :] = 1.0
    return jnp.asarray(sel.reshape(SEL_ROWS, N_STREAM * MCH), dtype=BF16)


def kernel(x, c, ctx, c_ctx, w_mod, b_mod, g_norm1, g_norm2, w_in, b_gates, conv_qk, g_mlstm,
           g_q_norm, g_kv_norm, w_uq, w_ukv, w_out, w_up, w_down, g_final):
    nb, seq, d = x.shape
    nctx = ctx.shape[1]
    depth = w_mod.shape[0]
    assert d == D_MODEL and seq % (4 * MCH) == 0 and nctx % MCH == 0
    sel = _stream_selectors()

    tm = min(INPROJ_TILE, seq)
    tm_mlp = min(TOKEN_TILE, seq)
    tm_ctx = min(CTX_TILE, nctx)
    tq = min(ATTN_SUBTILES * ATTN_SUB, seq)
    tq_ctx = min(ATTN_SUB, nctx)

    dft_cc, dft_cs = _channel_dft()
    fft_lat = _fourier4_tables(seq)
    dft_ctx = _position_dft(nctx)
    rope_lat = _rope_tables(seq, True)
    rope_ctx = _rope_tables(nctx, False)
    wts = _prepare_weights(max(seq, nctx) // MCH, g_norm1, g_norm2, w_in, b_gates, conv_qk, g_mlstm,
                           g_q_norm, g_kv_norm, w_uq, w_ukv, w_out, w_up, w_down)
    gfin = g_final.reshape(1, d)

    rows = ((nb + 1 + 7) // 8) * 8
    cc = jnp.concatenate([c, c_ctx[None, :], jnp.zeros((rows - nb - 1, d), F32)], axis=0)
    mod_all = _modulation(cc, w_mod, b_mod).reshape(depth * rows, 6, d)

    xl = x.reshape(nb * seq, d)
    xc = ctx.reshape(nb * nctx, d)

    for l in range(depth):
        last = l == depth - 1
        row_lat, row_ctx = l * rows, l * rows + nb

        zc, zs, mqk, vt, mo, gr, q_a, k_a, v_a = _inproj(
            xl, seq, l, wts, mod_all, row_lat, True, dft_cc, dft_cs, *rope_lat, tm, True)
        zc_c, zs_c, mqk_c, vt_c, mo_c, gr_c, q_ac, k_ac, v_ac = _inproj(
            xc, nctx, l, wts, mod_all, row_ctx, False, dft_cc, dft_cs, *rope_ctx, tm_ctx, not last)

        yf = _fourier4(fft_lat, zc, zs)

        def mlstm_inputs(mqk_s, vt_s, gr_s, n):
            q_s, k_s = _conv_silu(mqk_s.reshape(nb, n, 2 * M_WIDTH), l, wts)
            return (q_s, k_s, vt_s, gr_s)

        hm, hm_c = _mlstm(mlstm_inputs(mqk, vt, gr, seq), mlstm_inputs(mqk_c, vt_c, gr_c, nctx), l, wts, sel,
                          ctx_out=not last)

        b3 = lambda a, n: a.reshape(nb, n, AP_WIDTH)
        keys_ctx = (b3(k_ac, nctx), b3(v_ac, nctx))
        ya = _attention(b3(q_a, seq), [(b3(k_a, seq), b3(v_a, seq)), keys_ctx], tq, ATTN_HEADS_PER_STEP)

        xl = _out_mlp(xl, seq, yf, hm.reshape(nb * seq, MP_WIDTH), mo, ya.reshape(nb * seq, AP_WIDTH),
                      l, wts, mod_all, row_lat, True, gfin, tm_mlp, last)

        if not last:
            yf_c = _fourier(*dft_ctx, zc_c, zs_c)
            ya_c = _attention(b3(q_ac, nctx), [keys_ctx], tq_ctx, A_HEADS)
            xc = _out_mlp(xc, nctx, yf_c, hm_c.reshape(nb * nctx, MP_WIDTH), mo_c,
                          ya_c.reshape(nb * nctx, AP_WIDTH), l, wts, mod_all, row_ctx, False, gfin, tm_ctx, False)

    return xl.reshape(nb, seq, d)
```

```python
import functools

import numpy as np
import jax
import jax.numpy as jnp
from jax import lax
from jax.experimental import pallas as pl
from jax.experimental.pallas import tpu as pltpu

D_MODEL = 1024
GRID_W = 64
EPS = 1e-6
F_GROUPS = 4
F_GROUP_DIM = D_MODEL // 16
F_WIDTH = F_GROUPS * F_GROUP_DIM
M_HEADS = 4
M_HEAD_DIM = 3 * D_MODEL // 32
M_WIDTH = M_HEADS * M_HEAD_DIM
K_CONV = 5
A_HEADS = 4
A_NOPE = 64
A_ROPE = 32
A_V = 3 * D_MODEL // 32
Q_LORA = D_MODEL // 4
KV_LORA = D_MODEL // 8
ROPE_THETA = 10000.0
MLP_HIDDEN = 4 * D_MODEL

LANES = 128
HEAD_PAD = 128
MP_WIDTH = M_HEADS * HEAD_PAD
AP_WIDTH = A_HEADS * HEAD_PAD
VMEM_LIMIT = 56 * 1024 * 1024
MCH = 128
N_STREAM = 2 * M_HEADS
INPROJ_TILE = 1024
TOKEN_TILE = 512
CTX_TILE = 256
ATTN_SUBTILES = 8

BF16 = jnp.bfloat16
F32 = jnp.float32
LOG2_E = 1.4426950408889634


def _cparams(sem):
    return pltpu.CompilerParams(dimension_semantics=sem, vmem_limit_bytes=VMEM_LIMIT)


def _const_spec(shape):
    nd = len(shape)
    return pl.BlockSpec(shape, lambda *_: (0,) * nd, pipeline_mode=pl.Buffered(1))


def _layer_spec(arr, layer):
    nd = arr.ndim
    return pl.BlockSpec((1,) + arr.shape[1:], lambda *_: (layer,) + (0,) * (nd - 1), pipeline_mode=pl.Buffered(1))


def _split3(a):
    hi = a.astype(BF16)
    r1 = a - hi.astype(F32)
    mid = r1.astype(BF16)
    lo = (r1 - mid.astype(F32)).astype(BF16)
    return hi, mid, lo


def _dot(a, b):
    return jnp.dot(a, b, preferred_element_type=F32)


def _dot_nt(a, b):
    return lax.dot_general(a, b, (((1,), (1,)), ((), ())), preferred_element_type=F32)


def _dot_tn(a, b):
    return lax.dot_general(a, b, (((0,), (0,)), ((), ())), preferred_element_type=F32)


def _rms(x, g):
    return x * lax.rsqrt(jnp.mean(x * x, axis=-1, keepdims=True) + EPS) * g


MOD_COLS = 1536


def _mod_kernel(c_ref, w_ref, b_ref, o_ref):
    c = c_ref[...]
    a = c * jax.nn.sigmoid(c)
    a_hi = a.astype(BF16)
    a_lo = (a - a_hi.astype(F32)).astype(BF16)
    w = w_ref[0]
    w_hi = w.astype(BF16)
    w_lo = (w - w_hi.astype(F32)).astype(BF16)
    acc = _dot(a_hi, w_hi) + _dot(a_hi, w_lo) + _dot(a_lo, w_hi)
    o_ref[0] = acc + b_ref[0]


def _modulation(cc, w_mod, b_mod):
    depth, d, n = w_mod.shape
    rows = cc.shape[0]
    tn = MOD_COLS
    return pl.pallas_call(
        _mod_kernel,
        grid=(depth, n // tn),
        in_specs=[
            pl.BlockSpec((rows, d), lambda l, j: (0, 0)),
            pl.BlockSpec((1, d, tn), lambda l, j: (l, 0, j)),
            pl.BlockSpec((1, 1, tn), lambda l, j: (l, 0, j)),
        ],
        out_specs=pl.BlockSpec((1, rows, tn), lambda l, j: (l, 0, j)),
        out_shape=jax.ShapeDtypeStruct((depth, rows, n), F32),
        compiler_params=_cparams(("arbitrary", "arbitrary")),
        name="modulation",
    )(cc, w_mod, b_mod.reshape(depth, 1, n))


def _rope(x, cos, sin, first_half):
    half = A_ROPE // 4
    partner = jnp.where(first_half, pltpu.roll(x, LANES - half, 1), pltpu.roll(x, half, 1))
    return x * cos + partner * sin


def _inproj_kernel(x_ref, g_ref, mod_ref, wpf_ref, wmqk_ref, wmo_ref, wcq_ref, wkv_ref, wvt_ref, cc_ref, cs_ref,
                   cos_ref, sin_ref, gq_ref, gkv_ref, wq_ref, wk_ref, wv_ref,
                   zc_ref, zs_ref, mqk_ref, vt_ref, mo_ref, gr_ref, qa_ref, ka_ref, va_ref, *, with_q):
    x = x_ref[...]
    mod = mod_ref[0]
    h = _rms(x, g_ref[0]) * (1.0 + mod[1:2]) + mod[0:1]
    hb = h.astype(BF16)

    def proj(w_ref):
        return _dot(hb, w_ref[0])

    ckv_kr = proj(wkv_ref)
    cq = proj(wcq_ref) if with_q else None
    pf = proj(wpf_ref).astype(BF16) if with_q else None

    mqk_ref[...] = proj(wmqk_ref)

    cos = cos_ref[...]
    sin = sin_ref[...]
    lane = lax.broadcasted_iota(jnp.int32, cos.shape, 1)
    first_half = ((lane - A_NOPE) & (A_ROPE // 2 - 1)) < A_ROPE // 4
    kvn = _rms(ckv_kr[:, :KV_LORA], gkv_ref[0]).astype(BF16)
    k_rope = _rope(ckv_kr[:, KV_LORA:], cos, sin, first_half)
    k_nope = _dot(kvn, wk_ref[0])
    va = _dot(kvn, wv_ref[0])
    head_lane = lax.broadcasted_iota(jnp.int32, va.shape, 1) & (HEAD_PAD - 1)
    va_ref[...] = jnp.where(head_lane == A_V, 1.0, va).astype(BF16)
    for hd in range(A_HEADS):
        lanes = slice(hd * HEAD_PAD, (hd + 1) * HEAD_PAD)
        ka_ref[:, lanes] = (k_nope[:, lanes] + k_rope).astype(BF16)
    if with_q:
        qn = _rms(cq, gq_ref[0]).astype(BF16)
        q_all = _dot(qn, wq_ref[0])
        q_raw = [q_all[:, hd * HEAD_PAD:(hd + 1) * HEAD_PAD] for hd in range(A_HEADS)]
        zc_ref[...] = _dot(pf, cc_ref[...]).astype(BF16)
        zs_ref[...] = _dot(pf, cs_ref[...]).astype(BF16)

    vg = _dot_nt(wvt_ref[0], hb)
    vt = vg[:M_WIDTH].astype(BF16)
    for j in range(vt_ref.shape[0]):
        vt_ref[j] = vt[:, j * MCH:(j + 1) * MCH]
        gr_ref[j] = vg[M_WIDTH:, j * MCH:(j + 1) * MCH]

    if with_q:
        mo_ref[...] = proj(wmo_ref)
        scale = (A_NOPE + A_ROPE) ** -0.5 * LOG2_E
        for hd in range(A_HEADS):
            q = _rope(q_raw[hd], cos, sin, first_half)
            qa_ref[:, hd * HEAD_PAD:(hd + 1) * HEAD_PAD] = (q * scale).astype(BF16)
    else:
        for ref in (zc_ref, zs_ref, mo_ref, qa_ref):
            ref[...] = jnp.zeros_like(ref)


def _mod_map(mod_row0, per_batch_mod, tiles_per_seq):
    if per_batch_mod:
        return lambda i: (mod_row0 + i // tiles_per_seq, 0, 0)
    return lambda i: (mod_row0, 0, 0)


def _inproj(x2d, seq, layer, wts, mod, mod_row0, per_batch_mod, dft_cc, dft_cs, cos, sin, tm, with_q):
    t, d = x2d.shape
    nb = t // seq
    tiles_per_seq = seq // tm
    tok = lambda w: pl.BlockSpec((tm, w), lambda i: (i, 0))
    z_spec = pl.BlockSpec((tm, F_WIDTH), lambda i: (i % tiles_per_seq, i // tiles_per_seq))
    pos = pl.BlockSpec((tm, LANES), lambda i: (i % tiles_per_seq, 0))
    heads_bf16 = jax.ShapeDtypeStruct((t, AP_WIDTH), BF16)
    shapes = [
        jax.ShapeDtypeStruct((seq, nb * F_WIDTH), BF16),
        jax.ShapeDtypeStruct((seq, nb * F_WIDTH), BF16),
        jax.ShapeDtypeStruct((t, 2 * M_WIDTH), F32),
        jax.ShapeDtypeStruct((t // MCH, M_WIDTH, MCH), BF16),
        jax.ShapeDtypeStruct((t, MP_WIDTH), F32),
        jax.ShapeDtypeStruct((t // MCH, 2 * N_STREAM, MCH), F32),
        heads_bf16, heads_bf16, heads_bf16,
    ]
    vt_spec = pl.BlockSpec((tm // MCH, M_WIDTH, MCH), lambda i: (i, 0, 0))
    gr_spec = pl.BlockSpec((tm // MCH, 2 * N_STREAM, MCH), lambda i: (i, 0, 0))
    out_specs = [z_spec, z_spec, tok(2 * M_WIDTH), vt_spec, tok(MP_WIDTH), gr_spec,
                 tok(AP_WIDTH), tok(AP_WIDTH), tok(AP_WIDTH)]
    lay = lambda name: _layer_spec(wts[name], layer)
    return pl.pallas_call(
        functools.partial(_inproj_kernel, with_q=with_q),
        grid=(t // tm,),
        in_specs=[
            tok(d),
            lay("g1"),
            pl.BlockSpec((1, 6, d), _mod_map(mod_row0, per_batch_mod, tiles_per_seq)),
            lay("w_pf"), lay("w_mqk"), lay("w_mo"), lay("w_cq"), lay("w_kv"), lay("w_vt"),
            _const_spec((F_WIDTH, F_WIDTH)),
            _const_spec((F_WIDTH, F_WIDTH)),
            pos, pos,
            lay("gq"), lay("gkv"), lay("wq"), lay("wk"), lay("wv"),
        ],
        out_specs=out_specs,
        out_shape=shapes,
        compiler_params=_cparams(("arbitrary",)),
        name="inproj",
    )(x2d, wts["g1"], mod, wts["w_pf"], wts["w_mqk"], wts["w_mo"], wts["w_cq"], wts["w_kv"], wts["w_vt"],
      dft_cc, dft_cs, cos, sin,
      wts["gq"], wts["gkv"], wts["wq"], wts["wk"], wts["wv"])


def _fourier_kernel(c_ref, s_ref, zc_ref, zs_ref, o_ref):
    y = _dot(c_ref[...], zc_ref[...]) - _dot(s_ref[...], zs_ref[...])
    for slab in range(o_ref.shape[0]):
        o_ref[slab] = y[:, slab * LANES:(slab + 1) * LANES]


def _fourier(dft_c, dft_s, zc, zs):
    n, cols = zc.shape
    tr = min(n, 512)
    tc = min(cols, 512)
    return pl.pallas_call(
        _fourier_kernel,
        grid=(n // tr, cols // tc),
        in_specs=[
            pl.BlockSpec((tr, n), lambda i, j: (i, 0)),
            pl.BlockSpec((tr, n), lambda i, j: (i, 0)),
            pl.BlockSpec((n, tc), lambda i, j: (0, j)),
            pl.BlockSpec((n, tc), lambda i, j: (0, j)),
        ],
        out_specs=pl.BlockSpec((tc // LANES, tr, LANES), lambda i, j: (j, i, 0)),
        out_shape=jax.ShapeDtypeStruct((cols // LANES, n, LANES), F32),
        compiler_params=_cparams(("arbitrary", "arbitrary")),
        name="fourier",
    )(dft_c, dft_s, zc, zs)


FFT_COLS = 512


def _fourier4_kernel(tab_ref, twc_ref, tws_ref, zc_ref, zs_ref, o_ref):
    m = zc_ref.shape[0] // 4
    reps = zc_ref.shape[1] // LANES
    c0, c1, c2, c3 = (zc_ref[j * m:(j + 1) * m, :].astype(F32) for j in range(4))
    s0, s1, s2, s3 = (zs_ref[j * m:(j + 1) * m, :].astype(F32) for j in range(4))
    ce, co, cd, cu = c0 + c2, c1 + c3, c0 - c2, c1 - c3
    se, so, sd, su = s0 + s2, s1 + s3, s0 - s2, s1 - s3

    def emit(br, bi, k):
        if k:
            cos = jnp.concatenate([twc_ref[k - 1]] * reps, axis=1)
            sin = jnp.concatenate([tws_ref[k - 1]] * reps, axis=1)
            br, bi = br * cos + bi * sin, bi * cos - br * sin
        stacked = jnp.concatenate([br.astype(BF16), bi.astype(BF16)], axis=0)
        y = _dot(tab_ref[...], stacked)
        for slab in range(reps):
            o_ref[slab, pl.ds(k, m, stride=4), :] = y[:, slab * LANES:(slab + 1) * LANES]

    emit(ce + co, -(se + so), 0)
    emit(cd - su, -sd - cu, 1)
    emit(ce - co, so - se, 2)
    emit(cd + su, cu - sd, 3)


def _fourier4(tables, zc, zs):
    tab, twc, tws = tables
    n, cols = zc.shape
    m = n // 4
    tc = min(cols, FFT_COLS)
    return pl.pallas_call(
        _fourier4_kernel,
        grid=(cols // tc,),
        in_specs=[
            _const_spec((m, 2 * m)),
            _const_spec((3, m, LANES)),
            _const_spec((3, m, LANES)),
            pl.BlockSpec((n, tc), lambda j: (0, j)),
            pl.BlockSpec((n, tc), lambda j: (0, j)),
        ],
        out_specs=pl.BlockSpec((tc // LANES, n, LANES), lambda j: (j, 0, 0)),
        out_shape=jax.ShapeDtypeStruct((cols // LANES, n, LANES), F32),
        compiler_params=_cparams(("arbitrary",)),
        name="fourier4",
    )(tab, twc, tws, zc, zs)


def _fourier4_tables(n):
    m = n // 4
    idx = (np.arange(m, dtype=np.int64)[:, None] * np.arange(m, dtype=np.int64)[None, :]) % m
    ang = 2.0 * np.pi * idx.astype(np.float64) / m
    tab = np.concatenate([np.cos(ang), np.sin(ang)], axis=1) / np.sqrt(n)
    theta = 2.0 * np.pi * np.arange(m, dtype=np.float64)[None, :] * np.arange(1, 4, dtype=np.float64)[:, None] / n
    bcast = lambda t: jnp.asarray(np.repeat(t[:, :, None], LANES, axis=2), dtype=F32)
    return jnp.asarray(tab, dtype=F32).astype(BF16), bcast(np.cos(theta)), bcast(np.sin(theta))


CONV_ROWS = 256
CONV_HALO = 8


def _head_tile(x, start, lane):
    t, o = divmod(start, LANES)
    tile = lambda i: x[:, i * LANES:(i + 1) * LANES]
    if o == 0:
        return tile(t)
    shift = LANES - o
    out = pltpu.roll(tile(t), shift, 1)
    if shift < M_HEAD_DIM:
        out = jnp.where(lane < shift, out, pltpu.roll(tile(t + 1), shift, 1))
    return out


def _conv_kernel(u_ref, w_ref, q_ref, k_ref, pad_ref):
    n = u_ref.shape[1]
    width = u_ref.shape[2]
    zeros = jnp.zeros((CONV_HALO, width), F32)
    pad_ref[0:CONV_HALO, :] = zeros
    pad_ref[CONV_HALO + n:2 * CONV_HALO + n, :] = zeros
    pad_ref[CONV_HALO:CONV_HALO + n, :] = u_ref[0]
    w = w_ref[0]
    rows = min(CONV_ROWS, n)
    span = rows + 2 * CONV_HALO
    for r in range(n // rows):
        block = pad_ref[r * rows:r * rows + span, :]
        acc = None
        for j in range(K_CONV):
            shift = (K_CONV // 2 - j) % span
            tap = block if shift == 0 else pltpu.roll(block, shift, 0)
            term = tap[CONV_HALO:CONV_HALO + rows, :] * w[j:j + 1, :]
            acc = term if acc is None else acc + term
        half = 0.5 * acc
        act = half + half * jnp.tanh(half)
        lane = lax.broadcasted_iota(jnp.int32, (rows, LANES), 1)
        for hd in range(M_HEADS):
            lanes = slice(hd * HEAD_PAD, (hd + 1) * HEAD_PAD)
            q_tile = _head_tile(act, hd * M_HEAD_DIM, lane)
            k_tile = _head_tile(act, M_WIDTH + hd * M_HEAD_DIM, lane)
            q_ref[0, r * rows:(r + 1) * rows, lanes] = (q_tile * (M_HEAD_DIM ** -0.5)).astype(BF16)
            k_ref[0, r * rows:(r + 1) * rows, lanes] = jnp.where(lane < M_HEAD_DIM, k_tile, 0.0).astype(BF16)


def _conv_silu(mqk, layer, wts):
    nb, n, width = mqk.shape
    return pl.pallas_call(
        _conv_kernel,
        grid=(nb,),
        in_specs=[
            pl.BlockSpec((1, n, width), lambda b: (b, 0, 0)),
            _layer_spec(wts["conv_p"], layer),
        ],
        out_specs=[
            pl.BlockSpec((1, n, MP_WIDTH), lambda b: (b, 0, 0)),
            pl.BlockSpec((1, n, MP_WIDTH), lambda b: (b, 0, 0)),
        ],
        out_shape=[
            jax.ShapeDtypeStruct((nb, n, MP_WIDTH), BF16),
            jax.ShapeDtypeStruct((nb, n, MP_WIDTH), BF16),
        ],
        scratch_shapes=[pltpu.VMEM((n + 2 * CONV_HALO, width), F32)],
        compiler_params=_cparams(("arbitrary",)),
        name="conv_silu",
    )(mqk, wts["conv_p"])


def _log_sigmoid(x):
    return jnp.minimum(x, 0.0) - jnp.log(1.0 + jnp.exp(-jnp.abs(x)))


def _exact_dot_01(a, tri_bf16, a_on_left):
    out = None
    for term in _split3(a):
        d = _dot(term, tri_bf16) if a_on_left else _dot(tri_bf16, term)
        out = d if out is None else out + d
    return out


ONE_ROW = M_HEAD_DIM
(F_R, F_B, F_CM, F_TOT, F_CML, F_MP, F_A, F_WI, F_ELD, F_DEC, F_WK, F_HI, F_MID, F_LO) = range(14)
N_FIELDS = 14
STATE_GROUP = 4
OUT_GROUP = 8
SEL_ROWS = 32


def _mlstm_kernel(ql_ref, kl_ref, vtl_ref, grl_ref,
                  qc_ref, kc_ref, vtc_ref, grc_ref,
                  bir_ref, bfr_ref, sel_ref,
                  hl_ref, hc_ref,
                  ct_st, st_sc, rows_sc, *, ctx_out):
    L = MCH
    nh = M_HEADS
    ncc = qc_ref.shape[1] // L
    ncl = ql_ref.shape[1] // L

    d0 = lax.broadcasted_iota(jnp.int32, (L, L), 0)
    d1 = lax.broadcasted_iota(jnp.int32, (L, L), 1)
    le = d0 <= d1
    ge = d0 >= d1
    tri_le = le.astype(BF16)
    fwd_rows = lax.broadcasted_iota(jnp.int32, (N_STREAM, L), 0) < nh
    feat = lax.broadcasted_iota(jnp.int32, (HEAD_PAD, L), 0)
    keep_rows = feat < M_HEAD_DIM
    ones_tail = (lax.broadcasted_iota(jnp.int32, (HEAD_PAD - M_HEAD_DIM, L), 0) == 0).astype(F32)

    ct_st[...] = jnp.zeros_like(ct_st)

    def field(f, slot0, nc):
        return rows_sc[f, slot0:slot0 + nc].reshape(nc * N_STREAM, L)

    def set_field(f, slot0, nc, val):
        rows_sc[f, slot0:slot0 + nc] = val.reshape(nc, N_STREAM, L)

    def gate_pass(gr_ref, nc, slot0):
        n8 = nc * N_STREAM
        fwd = (lax.broadcasted_iota(jnp.int32, (n8, L), 0) & (N_STREAM - 1)) < nh
        lane = lax.broadcasted_iota(jnp.int32, (n8, L), 1)
        gi = gr_ref[:, 0:N_STREAM, :].reshape(n8, L) + bir_ref[0, 0:n8, :]
        f = _log_sigmoid(gr_ref[:, N_STREAM:2 * N_STREAM, :].reshape(n8, L) + bfr_ref[0, 0:n8, :])
        pre = _exact_dot_01(f, tri_le, a_on_left=True)
        total = jnp.sum(f, axis=1, keepdims=True)
        b = jnp.where(fwd, pre, total - pre + f)
        r = gi - b
        cm = r
        sh = 1
        while sh < L:
            from_left = jnp.where(lane >= sh, pltpu.roll(cm, sh, 1), -jnp.inf)
            from_right = jnp.where(lane < L - sh, pltpu.roll(cm, L - sh, 1), -jnp.inf)
            cm = jnp.maximum(cm, jnp.where(fwd, from_left, from_right))
            sh *= 2
        set_field(F_R, slot0, nc, r)
        set_field(F_B, slot0, nc, b)
        set_field(F_CM, slot0, nc, cm)
        set_field(F_TOT, slot0, nc, jnp.broadcast_to(total, (n8, L)))
        set_field(F_CML, slot0, nc, jnp.broadcast_to(jnp.max(r, axis=1, keepdims=True), (n8, L)))

    def m_scan(nc, slot0, m0):
        def step(j, m):
            sf = slot0 + j
            sb = slot0 + nc - 1 - j
            tot = jnp.where(fwd_rows, rows_sc[F_TOT, sf], rows_sc[F_TOT, sb])
            cml = jnp.where(fwd_rows, rows_sc[F_CML, sf], rows_sc[F_CML, sb])
            rows_sc[F_MP, sf, 0:nh, :] = m[0:nh]
            rows_sc[F_MP, sb, nh:N_STREAM, :] = m[nh:N_STREAM]
            return tot + jnp.maximum(m, cml)

        return lax.fori_loop(0, nc, step, m0)

    def weight_pass(nc, slot0):
        mp, cm, b, r = (field(f, slot0, nc) for f in (F_MP, F_CM, F_B, F_R))
        big = jnp.maximum(mp, field(F_CML, slot0, nc))
        a = -jnp.maximum(mp, cm)
        set_field(F_A, slot0, nc, a)
        set_field(F_WI, slot0, nc, jnp.exp(mp + a))
        set_field(F_ELD, slot0, nc, jnp.exp(a - b))
        set_field(F_DEC, slot0, nc, jnp.exp(mp - big))
        set_field(F_WK, slot0, nc, jnp.exp(r - big))
        for f, term in zip((F_HI, F_MID, F_LO), _split3(r)):
            set_field(f, slot0, nc, term.astype(F32))

    def value_slab(vt_ref, c, hd):
        vt = vt_ref[c, hd * M_HEAD_DIM:(hd + 1) * M_HEAD_DIM, :].astype(F32)
        return jnp.concatenate([vt, ones_tail], axis=0)

    def state_pass(k_ref, vt_ref, nc, slot0):
        group = min(STATE_GROUP, nc)

        def updates_of(j):
            uts = []
            for sidx in range(N_STREAM):
                hd = sidx % nh
                c = j if sidx < nh else nc - 1 - j
                wk = rows_sc[F_WK, slot0 + c, sidx:sidx + 1, :]
                kk = k_ref[0, pl.ds(pl.multiple_of(c * L, L), L), hd * HEAD_PAD:(hd + 1) * HEAD_PAD]
                uts.append(_dot((value_slab(vt_ref, c, hd) * wk).astype(BF16), kk))
            return uts

        def step(g, carry):
            all_uts = [updates_of(g * group + u) for u in range(group)]
            for u in range(group):
                j = g * group + u
                for sidx in range(N_STREAM):
                    slot = slot0 + (j if sidx < nh else nc - 1 - j)
                    prev = ct_st[sidx]
                    st_sc[sidx, slot] = prev.astype(BF16)
                    ct_st[sidx] = rows_sc[F_DEC, slot, sidx:sidx + 1, :] * prev + all_uts[u][sidx]
            return carry

        lax.fori_loop(0, nc // group, step, 0)

    def output_pass(q_ref, k_ref, vt_ref, out_ref, nc, slot0):
        head_lanes = [slice(hd * HEAD_PAD, (hd + 1) * HEAD_PAD) for hd in range(nh)]

        def independent_matmuls(c):
            rows = pl.ds(pl.multiple_of(c * L, L), L)
            slot = slot0 + c
            r3 = jnp.concatenate([rows_sc[F_HI, slot], rows_sc[F_MID, slot], rows_sc[F_LO, slot],
                                  jnp.zeros((N_STREAM, L), F32)], axis=0).astype(BF16)
            qs = [q_ref[0, rows, lanes] for lanes in head_lanes]
            s_ts = [_dot_nt(k_ref[0, rows, lanes], q) for lanes, q in zip(head_lanes, qs)]
            inters = [_dot_nt(st_sc[sidx, slot], qs[sidx % nh]) for sidx in range(N_STREAM)]
            r_all = _dot_tn(r3, sel_ref[...])
            r_ts = [r_all[:, sidx * L:(sidx + 1) * L] for sidx in range(N_STREAM)]
            return s_ts, inters, r_ts

        def finish(c, s_ts, inters, r_ts):
            rows = pl.ds(pl.multiple_of(c * L, L), L)
            slot = slot0 + c
            a_rows = rows_sc[F_A, slot]
            wi_rows = rows_sc[F_WI, slot]
            eld_rows = rows_sc[F_ELD, slot]
            p_ts = []
            for sidx in range(N_STREAM):
                one = slice(sidx, sidx + 1)
                dm = jnp.where(le if sidx < nh else ge, r_ts[sidx] + a_rows[one, :], -jnp.inf)
                p_ts.append((s_ts[sidx % nh] * jnp.exp(dm)).astype(BF16))
            z_pairs = [_dot(value_slab(vt_ref, c, hd).astype(BF16),
                            jnp.concatenate([p_ts[hd], p_ts[nh + hd]], axis=1)) for hd in range(nh)]
            for hd in range(nh):
                hsum = None
                for dr, sidx in enumerate((hd, nh + hd)):
                    one = slice(sidx, sidx + 1)
                    z_t = z_pairs[hd][:, dr * L:(dr + 1) * L] + inters[sidx] * wi_rows[one, :]
                    den = z_t[ONE_ROW:ONE_ROW + 1, :]
                    h_t = z_t * (1.0 / jnp.maximum(jnp.abs(den), eld_rows[one, :]))
                    hsum = h_t if hsum is None else hsum + h_t
                out_ref[0, rows, head_lanes[hd]] = jnp.where(keep_rows, hsum, 0.0).T

        group = min(OUT_GROUP, nc)

        def step(g, carry):
            ahead = independent_matmuls(g * group)
            for u in range(group):
                cur = ahead
                if u + 1 < group:
                    ahead = independent_matmuls(g * group + u + 1)
                finish(g * group + u, *cur)
            return carry

        lax.fori_loop(0, nc // group, step, 0)

    gate_pass(grc_ref, ncc, 0)
    gate_pass(grl_ref, ncl, ncc)
    m1 = m_scan(ncc, 0, jnp.zeros((N_STREAM, L), F32))
    m_scan(ncl, ncc, m1)
    weight_pass(ncc, 0)
    weight_pass(ncl, ncc)
    state_pass(kc_ref, vtc_ref, ncc, 0)
    state_pass(kl_ref, vtl_ref, ncl, ncc)
    if ctx_out:
        output_pass(qc_ref, kc_ref, vtc_ref, hc_ref, ncc, 0)
    else:
        hc_ref[...] = jnp.zeros_like(hc_ref)
    output_pass(ql_ref, kl_ref, vtl_ref, hl_ref, ncl, ncc)


def _mlstm(lat, ctx, layer, wts, sel, ctx_out):
    nb, n, _ = lat[0].shape
    nctx = ctx[0].shape[1]
    assert MCH == LANES == HEAD_PAD
    nct = (n + nctx) // MCH
    assert wts["bir"].shape[1] >= max(n, nctx) // MCH * N_STREAM

    def specs(rows):
        nc = rows // MCH
        seq = lambda w: pl.BlockSpec((1, rows, w), lambda b: (b, 0, 0))
        chunked = lambda rows_: pl.BlockSpec((nc, rows_, MCH), lambda b: (b, 0, 0))
        return [seq(MP_WIDTH), seq(MP_WIDTH), chunked(M_WIDTH), chunked(2 * N_STREAM)]

    out_spec = lambda rows: pl.BlockSpec((1, rows, MP_WIDTH), lambda b: (b, 0, 0))
    return pl.pallas_call(
        functools.partial(_mlstm_kernel, ctx_out=ctx_out),
        grid=(nb,),
        in_specs=specs(n) + specs(nctx) + [
            _layer_spec(wts["bir"], layer), _layer_spec(wts["bfr"], layer),
            _const_spec((SEL_ROWS, N_STREAM * MCH)),
        ],
        out_specs=[out_spec(n), out_spec(nctx)],
        out_shape=[
            jax.ShapeDtypeStruct((nb, n, MP_WIDTH), F32),
            jax.ShapeDtypeStruct((nb, nctx, MP_WIDTH), F32),
        ],
        scratch_shapes=[
            pltpu.VMEM((N_STREAM, HEAD_PAD, HEAD_PAD), F32),
            pltpu.VMEM((N_STREAM, nct, HEAD_PAD, HEAD_PAD), BF16),
            pltpu.VMEM((N_FIELDS, nct, N_STREAM, MCH), F32),
        ],
        compiler_params=_cparams(("arbitrary",)),
        name="mlstm",
    )(*lat, *ctx, wts["bir"], wts["bfr"], sel)


ATTN_SUB = 256
ATTN_HEADS_PER_STEP = 2


def _attn_kernel(*refs, n_sets):
    q_ref = refs[0]
    kv_refs = refs[1:1 + 2 * n_sets]
    o_ref = refs[1 + 2 * n_sets]
    sub = min(ATTN_SUB, q_ref.shape[1])
    n_sub = q_ref.shape[1] // sub
    items = [(hd, t) for hd in range(q_ref.shape[2] // HEAD_PAD) for t in range(n_sub)]

    def scores_of(item):
        hd, t = item
        lanes = slice(hd * HEAD_PAD, (hd + 1) * HEAD_PAD)
        q = q_ref[0, t * sub:(t + 1) * sub, lanes]
        return [_dot_nt(q, kv_refs[2 * i][0, :, lanes]) for i in range(n_sets)]

    nxt = scores_of(items[0])
    for idx, (hd, t) in enumerate(items):
        rows = slice(t * sub, (t + 1) * sub)
        lanes = slice(hd * HEAD_PAD, (hd + 1) * HEAD_PAD)
        scores = nxt
        if idx + 1 < len(items):
            nxt = scores_of(items[idx + 1])
        m = None
        for s in scores:
            sm = jnp.max(s, axis=-1, keepdims=True)
            m = sm if m is None else jnp.maximum(m, sm)
        acc = None
        for i, s in enumerate(scores):
            p = jnp.exp2(s - m)
            o = _dot(p.astype(BF16), kv_refs[2 * i + 1][0, :, lanes])
            acc = o if acc is None else acc + o
        o_ref[0, rows, lanes] = (acc / acc[:, A_V:A_V + 1]).astype(BF16)


def _attention(q, key_sets, tq, heads_per_step):
    nb, n, _ = q.shape
    n_sets = len(key_sets)
    width = heads_per_step * HEAD_PAD
    in_specs = [pl.BlockSpec((1, tq, width), lambda b, h, i: (b, i, h))]
    args = [q]
    for k, v in key_sets:
        nk = k.shape[1]
        spec = pl.BlockSpec((1, nk, width), lambda b, h, i: (b, 0, h))
        in_specs += [spec, spec]
        args += [k, v]
    return pl.pallas_call(
        functools.partial(_attn_kernel, n_sets=n_sets),
        grid=(nb, A_HEADS // heads_per_step, n // tq),
        in_specs=in_specs,
        out_specs=pl.BlockSpec((1, tq, width), lambda b, h, i: (b, i, h)),
        out_shape=jax.ShapeDtypeStruct((nb, n, AP_WIDTH), BF16),
        compiler_params=_cparams(("arbitrary", "arbitrary", "arbitrary")),
        name="attention",
    )(*args)


MLP_CHUNK = 1024


def _out_mlp_kernel(x_ref, yf_ref, hm_ref, mo_ref, ya_ref, mod_ref,
                    gm_ref, g2_ref, gfin_ref, wof_ref, wom_ref, woa_ref, wup_ref, wdn_ref,
                    o_ref, *, final_norm):
    mod = mod_ref[0]
    ga1, sh2, sc2, ga2 = mod[2:3], mod[3:4], mod[4:5], mod[5:6]
    yf = jnp.concatenate([yf_ref[slab].astype(BF16) for slab in range(yf_ref.shape[0])], axis=1)
    mix = _dot(ya_ref[...], woa_ref[0]) + _dot(yf, wof_ref[0])
    gm = gm_ref[0]
    yms = []
    for hd in range(M_HEADS):
        lanes = slice(hd * HEAD_PAD, (hd + 1) * HEAD_PAD)
        hh = hm_ref[:, lanes]
        ms = jnp.sum(hh * hh, axis=-1, keepdims=True) * (1.0 / M_HEAD_DIM)
        ym = hh * lax.rsqrt(ms + EPS) * gm[:, lanes] * jax.nn.sigmoid(mo_ref[:, lanes])
        yms.append(ym.astype(BF16))
    mix = mix + _dot(jnp.concatenate(yms, axis=1), wom_ref[0])
    x1 = x_ref[...] + ga1 * mix
    h2 = (_rms(x1, g2_ref[0]) * (1.0 + sc2) + sh2).astype(BF16)
    acc = None
    for c in range(wup_ref.shape[2] // MLP_CHUNK):
        cols = slice(c * MLP_CHUNK, (c + 1) * MLP_CHUNK)
        u = jnp.maximum(_dot(h2, wup_ref[0, :, cols]), 0.0)
        d = _dot((u * u).astype(BF16), wdn_ref[0, cols, :])
        acc = d if acc is None else acc + d
    x2 = x1 + ga2 * acc
    if final_norm:
        x2 = _rms(x2, gfin_ref[...])
    o_ref[...] = x2


def _out_mlp(x2d, seq, yf, hm, mo, ya, layer, wts, mod, mod_row0, per_batch_mod, gfin, tm, final_norm):
    t, d = x2d.shape
    tiles_per_seq = seq // tm
    tok = lambda w: pl.BlockSpec((tm, w), lambda i: (i, 0))
    yf_spec = pl.BlockSpec((F_WIDTH // LANES, tm, LANES), lambda i: (i // tiles_per_seq, i % tiles_per_seq, 0))
    names = ("gm", "g2", "wof", "wom", "woa", "wup", "wdn")
    lay = {name: _layer_spec(wts[name], layer) for name in names}
    return pl.pallas_call(
        functools.partial(_out_mlp_kernel, final_norm=final_norm),
        grid=(t // tm,),
        in_specs=[
            tok(d), yf_spec, tok(MP_WIDTH), tok(MP_WIDTH), tok(AP_WIDTH),
            pl.BlockSpec((1, 6, d), _mod_map(mod_row0, per_batch_mod, tiles_per_seq)),
            lay["gm"], lay["g2"], _const_spec((1, d)),
            lay["wof"], lay["wom"], lay["woa"], lay["wup"], lay["wdn"],
        ],
        out_specs=tok(d),
        out_shape=jax.ShapeDtypeStruct((t, d), F32),
        compiler_params=_cparams(("arbitrary",)),
        name="out_mlp",
    )(x2d, yf, hm, mo, ya, mod, wts["gm"], wts["g2"], gfin,
      wts["wof"], wts["wom"], wts["woa"], wts["wup"], wts["wdn"])


def _dft_tables(n):
    idx = (np.arange(n, dtype=np.int64)[:, None] * np.arange(n, dtype=np.int64)[None, :]) % n
    ang = 2.0 * np.pi * idx.astype(np.float64) / n
    scale = 1.0 / np.sqrt(n)
    return np.cos(ang) * scale, np.sin(ang) * scale


def _channel_dft():
    c, s = _dft_tables(F_GROUP_DIM)
    eye = np.eye(F_GROUPS)
    return (jnp.asarray(np.kron(eye, c), dtype=F32).astype(BF16),
            jnp.asarray(np.kron(eye, s), dtype=F32).astype(BF16))


def _position_dft(n):
    c, s = _dft_tables(n)
    return jnp.asarray(c, dtype=F32).astype(BF16), jnp.asarray(s, dtype=F32).astype(BF16)


def _rope_tables(n, rotate):
    cos = np.zeros((n, HEAD_PAD), np.float32)
    sin = np.zeros((n, HEAD_PAD), np.float32)
    cos[:, :A_NOPE + A_ROPE] = 1.0
    if rotate:
        nf = A_ROPE // 4
        t = np.arange(n)
        row = (t // GRID_W).astype(np.float32)
        col = (t % GRID_W).astype(np.float32)
        freqs = (np.float32(ROPE_THETA) ** (-np.arange(nf, dtype=np.float32) / np.float32(nf))).astype(np.float32)
        for seg, pos in enumerate((row, col)):
            ang = pos[:, None] * freqs[None, :]
            c, s = np.cos(ang), np.sin(ang)
            base = A_NOPE + seg * 2 * nf
            cos[:, base:base + nf] = c
            cos[:, base + nf:base + 2 * nf] = c
            sin[:, base:base + nf] = -s
            sin[:, base + nf:base + 2 * nf] = s
    return jnp.asarray(cos), jnp.asarray(sin)


def _pad_heads_cols(w, heads, width):
    lead = w.shape[:-1]
    w = w.reshape(lead + (heads, width))
    w = jnp.pad(w, [(0, 0)] * len(lead) + [(0, 0), (0, HEAD_PAD - width)])
    return w.reshape(lead + (heads * HEAD_PAD,))


def _pad_heads_rows(w, heads, width):
    depth, _, n = w.shape
    w = jnp.pad(w.reshape(depth, heads, width, n), [(0, 0), (0, 0), (0, HEAD_PAD - width), (0, 0)])
    return w.reshape(depth, heads * HEAD_PAD, n)


GATE_I_COLS = np.concatenate([np.arange(M_HEADS), 2 * M_HEADS + np.arange(M_HEADS)])
GATE_F_COLS = GATE_I_COLS + M_HEADS


def _prepare_weights(max_chunks, g_norm1, g_norm2, w_in, b_gates, conv_qk, g_mlstm, g_q_norm, g_kv_norm,
                     w_uq, w_ukv, w_out, w_up, w_down):
    offs = np.cumsum([0, F_WIDTH, M_WIDTH, M_WIDTH, M_WIDTH, M_WIDTH, 4 * M_HEADS, Q_LORA, KV_LORA, A_ROPE])
    part = lambda i: w_in[:, :, offs[i]:offs[i + 1]]
    heads = lambda w: _pad_heads_cols(w, M_HEADS, M_HEAD_DIM)
    w_kv = jnp.concatenate([part(7), jnp.pad(part(8), [(0, 0), (0, 0), (A_NOPE, LANES - A_NOPE - A_ROPE)])], axis=2)
    in_groups = dict(w_pf=part(0), w_mqk=w_in[:, :, offs[1]:offs[3]], w_mo=heads(part(4)), w_cq=part(6), w_kv=w_kv)
    in_groups = {name: w.astype(BF16) for name, w in in_groups.items()}
    gates = part(5)
    w_vt = jnp.concatenate([part(3), gates[:, :, GATE_I_COLS], gates[:, :, GATE_F_COLS]],
                           axis=2).astype(BF16).transpose(0, 2, 1)

    conv_p = jnp.pad(conv_qk, [(0, 0), (0, 8 - K_CONV), (0, 0)])

    tile_rows = lambda b: jnp.tile(b[:, :, None], (1, max_chunks, 1))
    ukv = w_ukv.reshape(w_ukv.shape[0], KV_LORA, A_HEADS, A_NOPE + A_V)
    pad_kv = lambda w: jnp.pad(w, [(0, 0), (0, 0), (0, 0), (0, HEAD_PAD - w.shape[-1])]).reshape(
        w.shape[0], KV_LORA, AP_WIDTH).astype(BF16)
    vec = lambda g: g[:, None, :]
    return dict(
        g1=vec(g_norm1), g2=vec(g_norm2), gq=vec(g_q_norm), gkv=vec(g_kv_norm),
        **in_groups, w_vt=w_vt, conv_p=conv_p,
        bir=tile_rows(b_gates[:, GATE_I_COLS]), bfr=tile_rows(b_gates[:, GATE_F_COLS]),
        gm=_pad_heads_cols(g_mlstm, M_HEADS, M_HEAD_DIM)[:, None, :],
        wq=_pad_heads_cols(w_uq, A_HEADS, A_NOPE + A_ROPE).astype(BF16),
        wk=pad_kv(ukv[..., :A_NOPE]), wv=pad_kv(ukv[..., A_NOPE:]),
        wof=w_out[:, :F_WIDTH].astype(BF16),
        wom=_pad_heads_rows(w_out[:, F_WIDTH:F_WIDTH + M_WIDTH], M_HEADS, M_HEAD_DIM).astype(BF16),
        woa=_pad_heads_rows(w_out[:, F_WIDTH + M_WIDTH:], A_HEADS, A_V).astype(BF16),
        wup=w_up.astype(BF16), wdn=w_down.astype(BF16),
    )


def _stream_selectors():
    sel = np.zeros((SEL_ROWS, N_STREAM, MCH), np.float32)
    for s in range(N_STREAM):
        for part in range(3):
            sel[part * N_STREAM + s, s, :] = 1.0
    return jnp.asarray(sel.reshape(SEL_ROWS, N_STREAM * MCH), dtype=BF16)


def kernel(x, c, ctx, c_ctx, w_mod, b_mod, g_norm1, g_norm2, w_in, b_gates, conv_qk, g_mlstm,
           g_q_norm, g_kv_norm, w_uq, w_ukv, w_out, w_up, w_down, g_final):
    nb, seq, d = x.shape
    nctx = ctx.shape[1]
    depth = w_mod.shape[0]
    assert d == D_MODEL and seq % (4 * MCH) == 0 and nctx % MCH == 0
    sel = _stream_selectors()

    tm = min(INPROJ_TILE, seq)
    tm_mlp = min(TOKEN_TILE, seq)
    tm_ctx = min(CTX_TILE, nctx)
    tq = min(ATTN_SUBTILES * ATTN_SUB, seq)
    tq_ctx = min(ATTN_SUB, nctx)

    dft_cc, dft_cs = _channel_dft()
    fft_lat = _fourier4_tables(seq)
    dft_ctx = _position_dft(nctx)
    rope_lat = _rope_tables(seq, True)
    rope_ctx = _rope_tables(nctx, False)
    wts = _prepare_weights(max(seq, nctx) // MCH, g_norm1, g_norm2, w_in, b_gates, conv_qk, g_mlstm,
                           g_q_norm, g_kv_norm, w_uq, w_ukv, w_out, w_up, w_down)
    gfin = g_final.reshape(1, d)

    rows = ((nb + 1 + 7) // 8) * 8
    cc = jnp.concatenate([c, c_ctx[None, :], jnp.zeros((rows - nb - 1, d), F32)], axis=0)
    mod_all = _modulation(cc, w_mod, b_mod).reshape(depth * rows, 6, d)

    xl = x.reshape(nb * seq, d)
    xc = ctx.reshape(nb * nctx, d)

    for l in range(depth):
        last = l == depth - 1
        row_lat, row_ctx = l * rows, l * rows + nb

        zc, zs, mqk, vt, mo, gr, q_a, k_a, v_a = _inproj(
            xl, seq, l, wts, mod_all, row_lat, True, dft_cc, dft_cs, *rope_lat, tm, True)
        zc_c, zs_c, mqk_c, vt_c, mo_c, gr_c, q_ac, k_ac, v_ac = _inproj(
            xc, nctx, l, wts, mod_all, row_ctx, False, dft_cc, dft_cs, *rope_ctx, tm_ctx, not last)

        yf = _fourier4(fft_lat, zc, zs)

        def mlstm_inputs(mqk_s, vt_s, gr_s, n):
            q_s, k_s = _conv_silu(mqk_s.reshape(nb, n, 2 * M_WIDTH), l, wts)
            return (q_s, k_s, vt_s, gr_s)

        hm, hm_c = _mlstm(mlstm_inputs(mqk, vt, gr, seq), mlstm_inputs(mqk_c, vt_c, gr_c, nctx), l, wts, sel,
                          ctx_out=not last)

        b3 = lambda a, n: a.reshape(nb, n, AP_WIDTH)
        keys_ctx = (b3(k_ac, nctx), b3(v_ac, nctx))
        ya = _attention(b3(q_a, seq), [(b3(k_a, seq), b3(v_a, seq)), keys_ctx], tq, ATTN_HEADS_PER_STEP)

        xl = _out_mlp(xl, seq, yf, hm.reshape(nb * seq, MP_WIDTH), mo, ya.reshape(nb * seq, AP_WIDTH),
                      l, wts, mod_all, row_lat, True, gfin, tm_mlp, last)

        if not last:
            yf_c = _fourier(*dft_ctx, zc_c, zs_c)
            ya_c = _attention(b3(q_ac, nctx), [keys_ctx], tq_ctx, A_HEADS)
            xc = _out_mlp(xc, nctx, yf_c, hm_c.reshape(nb * nctx, MP_WIDTH), mo_c,
                          ya_c.reshape(nb * nctx, AP_WIDTH), l, wts, mod_all, row_ctx, False, gfin, tm_ctx, False)

    return xl.reshape(nb, seq, d)
```
